```python
import math
import jax, jax.numpy as jnp
from jax import lax
import numpy as np

D_MODEL = 1024
BATCH = 8
SEQ = 2048
DEPTH = 1

CHUNK = 64
Q_BLOCK = 128
N_HEADS = 8
HEAD_DIM = 64
ATTN_WIDTH = N_HEADS * 2 * HEAD_DIM
POOL_WINDOWS = (2, 4, 8, 16)
POOL_GROUP = 128
N_POOL_GROUPS = len(POOL_WINDOWS)
POOL_WIDTH = N_POOL_GROUPS * POOL_GROUP
IN_WIDTH = 3 * ATTN_WIDTH + POOL_WIDTH
N_BRANCHES = 2
N_BUCKETS = 32
MAX_DISTANCE = 128
MEM_LEN = 256
X_HEADS = 4
X_HEAD_DIM = D_MODEL // X_HEADS
N_EXPERTS = 32
TOP_K = 4
D_FF = D_MODEL
SWIGLU_ALPHA = 1.702
SWIGLU_LIMIT = 7.0
MOE_BLOCK = 128
EPS = 1e-6

kernel_name = "hybrid_diffattn_pool_moe_chunk_causal"


def rms_norm(x, g):
    xf = x.astype(jnp.float32)
    y = xf * lax.rsqrt(jnp.mean(xf * xf, axis=-1, keepdims=True) + EPS)
    return (y * g.astype(jnp.float32)).astype(x.dtype)


def t5_bucket(rel):
    nb = N_BUCKETS // 2
    ret = (rel > 0).astype(jnp.int32) * nb
    n = jnp.abs(rel)
    max_exact = nb // 2
    nf = jnp.maximum(n, 1).astype(jnp.float32)
    large = max_exact + (jnp.log(nf / max_exact) / math.log(MAX_DISTANCE / max_exact)
                         * (nb - max_exact)).astype(jnp.int32)
    large = jnp.minimum(large, nb - 1)
    return ret + jnp.where(n < max_exact, n, large)


def diff_attention(q, k, v, lam, rel_bias):
    S = q.shape[1]
    scale = HEAD_DIM ** -0.5
    pos = jnp.arange(S, dtype=jnp.int32)
    table = rel_bias.astype(jnp.float32)
    outs = []
    for start in range(0, S, Q_BLOCK):
        end = start + Q_BLOCK
        qb = q[:, start:end]
        kb = k[:, :end]
        vb = v[:, :end]
        qpos = pos[start:end]
        kpos = pos[:end]
        logits = jnp.einsum('bqhmd,bkhmd->bmhqk', qb, kb).astype(jnp.float32) * scale
        bias = jnp.transpose(table[t5_bucket(kpos[None, :] - qpos[:, None])], (2, 0, 1))
        allowed = (kpos[None, :] // CHUNK) <= (qpos[:, None] // CHUNK)
        logits = jnp.where(allowed, logits + bias, -jnp.inf)
        p = jax.nn.softmax(logits, axis=-1)
        w = p[:, 0] - lam * p[:, 1]
        outs.append(jnp.einsum('bhqk,bkhd->bqhd', w.astype(v.dtype), vb))
    return jnp.concatenate(outs, axis=1)


def multiscale_pool(u):
    S = u.shape[1]
    uf = u.astype(jnp.float32)
    cs = jnp.pad(jnp.cumsum(uf, axis=1), ((0, 0), (1, 0), (0, 0)))
    t = jnp.arange(1, S + 1, dtype=jnp.float32)
    outs = []
    for g, w in enumerate(POOL_WINDOWS):
        sl = slice(g * POOL_GROUP, (g + 1) * POOL_GROUP)
        c = cs[..., sl]
        upper = c[:, 1:]
        lower = jnp.pad(c[:, :S + 1 - w], ((0, 0), (w - 1, 0), (0, 0)))
        mean = (upper - lower) / jnp.minimum(t, float(w))[None, :, None]
        outs.append(mean - uf[..., sl])
    return jnp.stack(outs, axis=2).astype(u.dtype)


def memory_cross_attention(h, m, w_q, w_kv, w_o):
    B, S, D = h.shape
    M = m.shape[1]
    q = (h @ w_q).reshape(B, S, X_HEADS, X_HEAD_DIM)
    kv = (m @ w_kv).reshape(B, M, 2, X_HEADS, X_HEAD_DIM)
    logits = jnp.einsum('bshd,bmhd->bhsm', q, kv[:, :, 0]).astype(jnp.float32) * (X_HEAD_DIM ** -0.5)
    p = jax.nn.softmax(logits, axis=-1)
    o = jnp.einsum('bhsm,bmhd->bshd', p.astype(h.dtype), kv[:, :, 1]).reshape(B, S, D)
    return o @ w_o


def moe_ffn(h, w_router, b_router, w_gu, b_gu, w_down, b_down):
    B, S, D = h.shape
    N = B * S
    hf = h.reshape(N, D)
    logits = (hf @ w_router).astype(jnp.float32) + b_router.astype(jnp.float32)
    top_val, top_idx = lax.top_k(logits, TOP_K)
    gates = jax.nn.softmax(top_val, axis=-1)
    expert = top_idx.reshape(-1).astype(jnp.int32)
    token = jnp.repeat(jnp.arange(N, dtype=jnp.int32), TOP_K)
    gate = gates.reshape(-1)
    counts = jnp.bincount(expert, length=N_EXPERTS)
    padded = (counts + MOE_BLOCK - 1) // MOE_BLOCK * MOE_BLOCK
    start = jnp.cumsum(counts) - counts
    pend = jnp.cumsum(padded)
    pstart = pend - padded
    order = jnp.argsort(expert)
    e_sorted = expert[order]
    rank = jnp.arange(N * TOP_K, dtype=jnp.int32) - start[e_sorted]
    dest = pstart[e_sorted] + rank
    R = N * TOP_K + N_EXPERTS * MOE_BLOCK
    row_token = jnp.zeros((R,), jnp.int32).at[dest].set(token[order])
    row_gate = jnp.zeros((R,), jnp.float32).at[dest].set(gate[order])
    n_blocks = R // MOE_BLOCK
    block_start = jnp.arange(n_blocks, dtype=pend.dtype) * MOE_BLOCK
    block_expert = jnp.minimum(jnp.searchsorted(pend, block_start, side='right'), N_EXPERTS - 1)
    xs = hf[row_token].reshape(n_blocks, MOE_BLOCK, D)

    def expert_block(args):
        xb, e = args
        gu = xb @ w_gu[e] + b_gu[e]
        x_glu = jnp.minimum(gu[:, :D_FF], SWIGLU_LIMIT)
        x_lin = jnp.clip(gu[:, D_FF:], -SWIGLU_LIMIT, SWIGLU_LIMIT)
        act = x_glu * jax.nn.sigmoid(SWIGLU_ALPHA * x_glu) * (x_lin + 1)
        return act @ w_down[e] + b_down[e]

    ys = lax.map(expert_block, (xs, block_expert)).reshape(R, D)
    out = jax.ops.segment_sum(ys * row_gate[:, None].astype(ys.dtype), row_token, num_segments=N)
    return out.reshape(B, S, D)


def setup_inputs(seed: int = 0) -> dict:
    key = jax.random.key(seed)
    ks = iter(jax.random.split(key, 40))
    L = DEPTH
    f32 = jnp.float32

    def nrm(shape, scale):
        return jax.random.normal(next(ks), shape, f32) * scale

    def gain(shape):
        return 1.0 + nrm(shape, 0.05)

    return {
        "x": nrm((BATCH, SEQ, D_MODEL), 1.0),
        "mem": nrm((BATCH, MEM_LEN, D_MODEL), 1.0),
        "norm_mix_g": gain((L, D_MODEL)),
        "w_in": nrm((L, D_MODEL, IN_WIDTH), D_MODEL ** -0.5),
        "lambda_q1": nrm((L, HEAD_DIM), 0.1),
        "lambda_k1": nrm((L, HEAD_DIM), 0.1),
        "lambda_q2": nrm((L, HEAD_DIM), 0.1),
        "lambda_k2": nrm((L, HEAD_DIM), 0.1),
        "rel_bias": nrm((N_BUCKETS, N_HEADS), 0.5),
        "subln_g": gain((L, 2 * HEAD_DIM)),
        "w_attn_up": nrm((L, ATTN_WIDTH, D_MODEL), ATTN_WIDTH ** -0.5),
        "pool_mix": nrm((L, N_POOL_GROUPS, POOL_GROUP, POOL_GROUP), POOL_GROUP ** -0.5),
        "pool_scale": 1.0 + nrm((L, POOL_WIDTH), 0.1),
        "w_pool_up": nrm((L, POOL_WIDTH, D_MODEL), POOL_WIDTH ** -0.5),
        "w_gate": nrm((L, D_MODEL, N_BRANCHES * D_MODEL), D_MODEL ** -0.5),
        "b_gate": nrm((L, N_BRANCHES * D_MODEL), 0.1),
        "w_out": nrm((L, D_MODEL, D_MODEL), D_MODEL ** -0.5),
        "norm_x_g": gain((L, D_MODEL)),
        "norm_mem_g": gain((L, D_MODEL)),
        "w_xq": nrm((L, D_MODEL, D_MODEL), D_MODEL ** -0.5),
        "w_xkv": nrm((L, D_MODEL, 2 * D_MODEL), D_MODEL ** -0.5),
        "w_xo": nrm((L, D_MODEL, D_MODEL), D_MODEL ** -0.5),
        "norm_ffn_g": gain((L, D_MODEL)),
        "w_router": nrm((L, D_MODEL, N_EXPERTS), D_MODEL ** -0.5),
        "b_router": nrm((L, N_EXPERTS), 0.01),
        "w_gu": nrm((L, N_EXPERTS, D_MODEL, 2 * D_FF), D_MODEL ** -0.5),
        "b_gu": nrm((L, N_EXPERTS, 2 * D_FF), 0.01),
        "w_down": nrm((L, N_EXPERTS, D_FF, D_MODEL), D_FF ** -0.5),
        "b_down": nrm((L, N_EXPERTS, D_MODEL), 0.01),
        "final_norm_g": gain((D_MODEL,)),
    }


def reference(x, mem, norm_mix_g, w_in, lambda_q1, lambda_k1, lambda_q2, lambda_k2, rel_bias,
              subln_g, w_attn_up, pool_mix, pool_scale, w_pool_up, w_gate, b_gate, w_out,
              norm_x_g, norm_mem_g, w_xq, w_xkv, w_xo, norm_ffn_g, w_router, b_router,
              w_gu, b_gu, w_down, b_down, final_norm_g):
    B, S, D = x.shape
    for l in range(DEPTH):
        h = rms_norm(x, norm_mix_g[l])
        proj = h @ w_in[l]
        q = proj[..., :ATTN_WIDTH].reshape(B, S, N_HEADS, 2, HEAD_DIM)
        k = proj[..., ATTN_WIDTH:2 * ATTN_WIDTH].reshape(B, S, N_HEADS, 2, HEAD_DIM)
        v = proj[..., 2 * ATTN_WIDTH:3 * ATTN_WIDTH].reshape(B, S, N_HEADS, 2 * HEAD_DIM)
        u = proj[..., 3 * ATTN_WIDTH:]

        lam_init = 0.8 - 0.6 * math.exp(-0.3 * l)
        lam = (jnp.exp(jnp.sum(lambda_q1[l].astype(jnp.float32) * lambda_k1[l].astype(jnp.float32)))
               - jnp.exp(jnp.sum(lambda_q2[l].astype(jnp.float32) * lambda_k2[l].astype(jnp.float32)))
               + lam_init)
        a = diff_attention(q, k, v, lam, rel_bias)
        a = rms_norm(a, subln_g[l]) * (1.0 - lam_init)
        y_a = a.reshape(B, S, ATTN_WIDTH) @ w_attn_up[l]

        pooled = multiscale_pool(u)
        mixed = jnp.einsum('bsgc,gcd->bsgd', pooled, pool_mix[l]).reshape(B, S, POOL_WIDTH)
        y_b = (mixed * pool_scale[l]) @ w_pool_up[l]

        g = jax.nn.sigmoid(h @ w_gate[l] + b_gate[l]).reshape(B, S, N_BRANCHES, D)
        merged = g[:, :, 0] * y_a + g[:, :, 1] * y_b
        x = x + merged @ w_out[l]

        hx = rms_norm(x, norm_x_g[l])
        m = rms_norm(mem, norm_mem_g[l])
        x = x + memory_cross_attention(hx, m, w_xq[l], w_xkv[l], w_xo[l])

        hf = rms_norm(x, norm_ffn_g[l])
        x = x + moe_ffn(hf, w_router[l], b_router[l], w_gu[l], b_gu[l], w_down[l], b_down[l])
    return rms_norm(x, final_norm_g)
```

```python
import functools
import math

import numpy as np
import jax
import jax.numpy as jnp
from jax import lax
from jax.experimental import pallas as pl
from jax.experimental.pallas import tpu as pltpu

F32 = jnp.float32
BF16 = jnp.bfloat16
I32 = jnp.int32

EPS = 1e-6
CHUNK = 64
N_HEADS = 8
HEAD_DIM = 64
HEAD_W = 2 * HEAD_DIM
POOL_WINDOWS = (2, 4, 8, 16)
POOL_GROUP = 128
POOL_PAD = 16
N_BUCKETS = 32
MAX_DISTANCE = 128
X_HEADS = 4
N_EXPERTS = 32
TOP_K = 4
SWIGLU_ALPHA = 1.702
SWIGLU_LIMIT = 7.0
LAMBDA_INIT = 0.8 - 0.6 * math.exp(-0.3 * 0)

ATTN_BLOCK = 256
NEG_BIG = -1e30
VMEM_LIMIT = 56 * 1024 * 1024


def _cparams(n_axes, vmem=VMEM_LIMIT):
    return pltpu.CompilerParams(
        dimension_semantics=("arbitrary",) * n_axes, vmem_limit_bytes=vmem)


def _rms(xf, g):
    ms = jnp.mean(xf * xf, axis=-1, keepdims=True)
    return xf * lax.rsqrt(ms + EPS) * g


def _dot(a, b):
    return jnp.dot(a, b, preferred_element_type=F32)


def _dot_nt(a, b, precision=None):
    return lax.dot_general(a, b, (((1,), (1,)), ((), ())),
                           preferred_element_type=F32, precision=precision)


def _mixer_in_kernel(x_ref, g_ref, wq_ref, wk_ref, wv_ref, wu_ref, wg_ref, bg_ref,
                     pmix_ref, pscale_ref, wpu_ref,
                     q_ref, k_ref, v_ref, g0_ref, gyb_ref, ext_ref):
    ts = x_ref.shape[1]
    d = x_ref.shape[2]
    j = pl.program_id(1)
    h = _rms(x_ref[0], g_ref[...]).astype(BF16)
    q_ref[0] = (_dot(h, wq_ref[...]) * (HEAD_DIM ** -0.5)).astype(BF16)
    k_ref[0] = _dot(h, wk_ref[...]).astype(BF16)
    v_ref[0] = _dot(h, wv_ref[...]).astype(BF16)
    u = _dot(h, wu_ref[...])

    @pl.when(j == 0)
    def _():
        ext_ref[0:POOL_PAD, :] = jnp.zeros((POOL_PAD, u.shape[1]), F32)

    ext_ref[POOL_PAD:POOL_PAD + ts, :] = u
    e = ext_ref[...]
    sums = {}
    s = e
    w = 1
    while w < max(POOL_WINDOWS):
        s = s + pltpu.roll(s, w, 0)
        w *= 2
        sums[w] = s
    ext_ref[0:POOL_PAD, :] = ext_ref[ts:ts + POOL_PAD, :]

    pos = (j * ts + lax.broadcasted_iota(I32, (ts, 1), 0) + 1).astype(F32)
    mixed = []
    for gi, w in enumerate(POOL_WINDOWS):
        sl = slice(gi * POOL_GROUP, (gi + 1) * POOL_GROUP)
        win = sums[w][POOL_PAD:, sl]
        pooled = win / jnp.minimum(pos, float(w)) - u[:, sl]
        mixed.append(_dot(pooled.astype(BF16), pmix_ref[gi]) * pscale_ref[:, sl])
    mixed = jnp.concatenate(mixed, axis=1).astype(BF16)
    y_b = _dot(mixed, wpu_ref[...])

    gate = jax.nn.sigmoid(_dot(h, wg_ref[...]) + bg_ref[...])
    g0_ref[...] = gate[:, :d].astype(BF16)
    gyb_ref[...] = (gate[:, d:] * y_b).astype(BF16)


def _mixer_in(x, norm_g, w_in, w_gate, b_gate, pool_mix, pool_scale, w_pool_up, ts):
    b, s, d = x.shape
    aw = N_HEADS * HEAD_W
    pw = len(POOL_WINDOWS) * POOL_GROUP
    n = b * s
    nt = s // ts
    const = lambda *shape: pl.BlockSpec(shape, lambda bi, j: (0,) * len(shape))
    tok3 = pl.BlockSpec((1, ts, aw), lambda bi, j: (bi, j, 0))
    tok2 = pl.BlockSpec((ts, d), lambda bi, j: (bi * nt + j, 0))
    return pl.pallas_call(
        _mixer_in_kernel,
        grid=(b, nt),
        in_specs=[
            pl.BlockSpec((1, ts, d), lambda bi, j: (bi, j, 0)),
            const(1, d),
            pl.BlockSpec((d, aw), lambda bi, j: (0, 0)),
            pl.BlockSpec((d, aw), lambda bi, j: (0, 1)),
            pl.BlockSpec((d, aw), lambda bi, j: (0, 2)),
            pl.BlockSpec((d, pw), lambda bi, j: (0, 3 * aw // pw)),
            const(d, 2 * d),
            const(1, 2 * d),
            const(len(POOL_WINDOWS), POOL_GROUP, POOL_GROUP),
            const(1, pw),
            const(pw, d),
        ],
        out_specs=[tok3, tok3, tok3, tok2, tok2],
        out_shape=[jax.ShapeDtypeStruct((b, s, aw), BF16)] * 3
        + [jax.ShapeDtypeStruct((n, d), BF16)] * 2,
        scratch_shapes=[pltpu.VMEM((ts + POOL_PAD, pw), F32)],
        compiler_params=_cparams(2),
        name="mixer_in",
    )(x, norm_g, w_in, w_in, w_in, w_in, w_gate, b_gate, pool_mix, pool_scale, w_pool_up)


def _rel_bucket(rel, log=jnp.log, f32=lambda a: a.astype(jnp.float32),
                i32=lambda a: a.astype(jnp.int32), xp=jnp):
    nb = N_BUCKETS // 2
    ret = i32(rel > 0) * nb
    n = xp.abs(rel)
    max_exact = nb // 2
    nf = f32(xp.maximum(n, 1))
    large = max_exact + i32(log(nf / max_exact) / math.log(MAX_DISTANCE / max_exact)
                            * (nb - max_exact))
    large = xp.minimum(large, nb - 1)
    return ret + xp.where(n < max_exact, n, large)


def _far_bucket(block, seq):
    rel = -np.arange(block + 1, max(seq, block + 2), dtype=np.int32)
    bk = _rel_bucket(rel, log=np.log, f32=lambda a: a.astype(np.float32),
                     i32=lambda a: a.astype(np.int32), xp=np)
    assert (bk == bk[0]).all(), "far keys must share one relative-position bucket"
    return int(bk[0])


def _rel_bias_kernel(far_bucket, tab_ref, bidx_ref, lq1_ref, lk1_ref, lq2_ref, lk2_ref,
                     bias_ref, lam_ref):
    h = pl.program_id(0)
    bidx = bidx_ref[...]
    acc = jnp.zeros(bidx.shape, F32)
    for bkt in range(N_BUCKETS):
        acc = jnp.where(bidx == bkt, tab_ref[bkt, h], acc)
    acc = acc - tab_ref[far_bucket, h]
    bias_ref[0] = jnp.where(bidx < 0, -jnp.inf, acc)
    lam = (jnp.exp(jnp.sum(lq1_ref[...] * lk1_ref[...], keepdims=True))
           - jnp.exp(jnp.sum(lq2_ref[...] * lk2_ref[...], keepdims=True)) + LAMBDA_INIT)
    lam_ref[...] = jnp.broadcast_to(lam, lam_ref.shape)


def _rel_bias(rel_bias, lq1, lk1, lq2, lk2, seq):
    blk = ATTN_BLOCK
    qpos = jnp.arange(blk, dtype=I32)[:, None]
    kpos = jnp.arange(blk, dtype=I32)[None, :]
    diag = jnp.where(kpos // CHUNK <= qpos // CHUNK, _rel_bucket(kpos - qpos), -1)
    prev = _rel_bucket(kpos - (qpos + blk))
    bidx = jnp.stack([diag, prev]).astype(I32)
    vec = pl.BlockSpec((1, HEAD_DIM), lambda h: (0, 0))
    return pl.pallas_call(
        functools.partial(_rel_bias_kernel, _far_bucket(blk, seq)),
        grid=(N_HEADS,),
        in_specs=[
            pl.BlockSpec(memory_space=pltpu.SMEM),
            pl.BlockSpec((2, blk, blk), lambda h: (0, 0, 0)),
            vec, vec, vec, vec,
        ],
        out_specs=[
            pl.BlockSpec((1, 2, blk, blk), lambda h: (h, 0, 0, 0)),
            pl.BlockSpec((8, 128), lambda h: (0, 0)),
        ],
        out_shape=[jax.ShapeDtypeStruct((N_HEADS, 2, blk, blk), F32),
                   jax.ShapeDtypeStruct((8, 128), F32)],
        compiler_params=_cparams(1),
        name="rel_bias",
    )(rel_bias, bidx, lq1, lk1, lq2, lk2)


def _diff_attn_kernel(q_ref, k_ref, v_ref, bias_ref, lam_ref, sg_ref, o_ref,
                      m_ref, l_ref, acc_ref):
    tq = q_ref.shape[1]
    tk = ATTN_BLOCK
    qi = pl.program_id(2)
    q = q_ref[0]
    lane = lax.broadcasted_iota(I32, q.shape, 1)
    zero = jnp.zeros_like(q)
    qs = jnp.concatenate([jnp.where(lane < HEAD_DIM, q, zero),
                          jnp.where(lane >= HEAD_DIM, q, zero)], axis=0)

    m_ref[...] = jnp.full(m_ref.shape, NEG_BIG, F32)
    l_ref[...] = jnp.zeros(l_ref.shape, F32)
    acc_ref[...] = jnp.zeros(acc_ref.shape, F32)

    def block(j, bias):
        start = pl.multiple_of(j * tk, tk)
        kb = k_ref[0, pl.ds(start, tk), :]
        vb = v_ref[0, pl.ds(start, tk), :]
        s = _dot_nt(qs, kb)
        if bias is not None:
            s = s + jnp.concatenate([bias, bias], axis=0)
        m_old = m_ref[...]
        m_new = jnp.maximum(m_old, jnp.max(s, axis=-1, keepdims=True))
        alpha = jnp.exp(m_old - m_new)
        p = jnp.exp(s - m_new)
        l_ref[...] = alpha * l_ref[...] + jnp.sum(p, axis=-1, keepdims=True)
        acc_ref[...] = alpha * acc_ref[...] + _dot(p.astype(BF16), vb)
        m_ref[...] = m_new

    def far(j, carry):
        block(j, None)
        return carry

    lax.fori_loop(0, jnp.maximum(qi - 1, 0), far, 0)

    @pl.when(qi > 0)
    def _():
        block(qi - 1, bias_ref[0, 1])

    block(qi, bias_ref[0, 0])

    o = acc_ref[...] / l_ref[...]
    lam = lam_ref[0:1, 0:1]
    a = o[:tq] - lam * o[tq:]
    o_ref[0] = (_rms(a, sg_ref[...]) * (1.0 - LAMBDA_INIT)).astype(BF16)


def _diff_attn(q, k, v, bias, lam, subln_g):
    b, s, aw = q.shape
    tq = ATTN_BLOCK
    qspec = pl.BlockSpec((1, tq, HEAD_W), lambda bi, h, qi: (bi, qi, h))
    kvspec = pl.BlockSpec((1, s, HEAD_W), lambda bi, h, qi: (bi, 0, h))
    return pl.pallas_call(
        _diff_attn_kernel,
        grid=(b, N_HEADS, s // tq),
        in_specs=[
            qspec, kvspec, kvspec,
            pl.BlockSpec((1, 2, tq, tq), lambda bi, h, qi: (h, 0, 0, 0)),
            pl.BlockSpec((8, 128), lambda bi, h, qi: (0, 0)),
            pl.BlockSpec((1, HEAD_W), lambda bi, h, qi: (0, 0)),
        ],
        out_specs=qspec,
        out_shape=jax.ShapeDtypeStruct((b, s, aw), BF16),
        scratch_shapes=[pltpu.VMEM((2 * tq, 1), F32), pltpu.VMEM((2 * tq, 1), F32),
                        pltpu.VMEM((2 * tq, HEAD_W), F32)],
        compiler_params=_cparams(3),
        name="diff_attn",
    )(q, k, v, bias, lam, subln_g)


def _mem_kv_kernel(m_ref, g_ref, w_ref, k_ref, v_ref):
    d = m_ref.shape[2]
    m = _rms(m_ref[0], g_ref[...]).astype(BF16)
    kv = _dot(m, w_ref[...])
    k_ref[0] = kv[:, :d].astype(BF16)
    v_ref[0] = kv[:, d:].astype(BF16)


def _mem_kv(mem, norm_g, w_xkv):
    b, ml, d = mem.shape
    blk = pl.BlockSpec((1, ml, d), lambda bi: (bi, 0, 0))
    return pl.pallas_call(
        _mem_kv_kernel,
        grid=(b,),
        in_specs=[blk, pl.BlockSpec((1, d), lambda bi: (0, 0)),
                  pl.BlockSpec((d, 2 * d), lambda bi: (0, 0))],
        out_specs=[blk, blk],
        out_shape=[jax.ShapeDtypeStruct((b, ml, d), BF16)] * 2,
        compiler_params=_cparams(1),
        name="mem_kv",
    )(mem, norm_g, w_xkv)


def _post_kernel(x_ref, a_ref, g0_ref, gyb_ref, wau_ref, wo_ref, nxg_ref, wxq_ref,
                 km_ref, vm_ref, wxo_ref, nfg_ref, wrt_ref, br_ref,
                 x2_ref, hf_ref, eidx_ref, rank_ref, gate_ref, cnt_ref, run_ref):
    ts, d = x_ref.shape
    first = (pl.program_id(0) == 0) & (pl.program_id(1) == 0)

    y_a = _dot(a_ref[...], wau_ref[...])
    merged = g0_ref[...].astype(F32) * y_a + gyb_ref[...].astype(F32)
    x1 = x_ref[...] + _dot(merged.astype(BF16), wo_ref[...])

    hd = d // X_HEADS
    hx = _rms(x1, nxg_ref[...]).astype(BF16)
    qx = (_dot(hx, wxq_ref[...]) * (hd ** -0.5)).astype(BF16)
    heads = []
    for hh in range(X_HEADS):
        sl = slice(hh * hd, (hh + 1) * hd)
        s = _dot_nt(qx[:, sl], km_ref[0, :, sl])
        p = jnp.exp(s - jnp.max(s, axis=-1, keepdims=True))
        p = p / jnp.sum(p, axis=-1, keepdims=True)
        heads.append(_dot(p.astype(BF16), vm_ref[0, :, sl]))
    o = jnp.concatenate(heads, axis=1).astype(BF16)
    x2 = x1 + _dot(o, wxo_ref[...])
    x2_ref[...] = x2
    hf = _rms(x2, nfg_ref[...])
    hf_ref[...] = hf

    logits = _dot_nt(wrt_ref[...], hf, precision=lax.Precision.HIGHEST) + br_ref[...]
    ne = logits.shape[0]
    eid = lax.broadcasted_iota(I32, logits.shape, 0).astype(F32)
    work = logits
    vals, idxs, hots = [], [], []
    for _ in range(TOP_K):
        mx = jnp.max(work, axis=0, keepdims=True)
        idx = jnp.min(jnp.where(work == mx, eid, float(ne)), axis=0, keepdims=True)
        hot = eid == idx
        vals.append(mx)
        idxs.append(idx.astype(I32))
        hots.append(hot)
        work = jnp.where(hot, -jnp.inf, work)
    ex = [jnp.exp(vv - vals[0]) for vv in vals]
    den = ex[0] + ex[1] + ex[2] + ex[3]
    gates = [e_ / den for e_ in ex]

    multi = (hots[0] | hots[1] | hots[2] | hots[3])
    multi_f = jnp.where(multi, 1.0, 0.0).astype(F32)
    tri = (lax.broadcasted_iota(I32, (ts, ts), 0)
           < lax.broadcasted_iota(I32, (ts, ts), 1))
    before = _dot(multi_f.astype(BF16), jnp.where(tri, 1.0, 0.0).astype(BF16))

    @pl.when(first)
    def _():
        run_ref[...] = jnp.zeros(run_ref.shape, F32)

    run = run_ref[...]
    pos = before + run
    ranks = [jnp.sum(jnp.where(hot, pos, 0.0), axis=0, keepdims=True) for hot in hots]
    run_new = run + jnp.sum(multi_f, axis=1, keepdims=True)
    run_ref[...] = run_new
    cnt_ref[...] = jnp.broadcast_to(run_new, cnt_ref.shape).astype(I32)

    eidx_ref[...] = jnp.concatenate(idxs, axis=0)
    rank_ref[...] = jnp.concatenate(ranks, axis=0).astype(I32)
    g_rows = jnp.concatenate(gates + [jnp.zeros((128 - TOP_K, ts), F32)], axis=0)
    gate_ref[...] = g_rows.T


def _post(x2d, a2d, g0, gyb, w_attn_up, w_out, norm_x_g, w_xq, k_mem, v_mem, w_xo,
          norm_ffn_g, w_router_t, b_router, batch, ts):
    n, d = x2d.shape
    nt = n // batch // ts
    ml = k_mem.shape[1]
    tok = pl.BlockSpec((ts, d), lambda bi, j: (bi * nt + j, 0))
    lanes = pl.BlockSpec((TOP_K, ts), lambda bi, j: (0, bi * nt + j))
    const = lambda *shape: pl.BlockSpec(shape, lambda bi, j: (0,) * len(shape))
    mem = pl.BlockSpec((1, ml, d), lambda bi, j: (bi, 0, 0))
    return pl.pallas_call(
        _post_kernel,
        grid=(batch, nt),
        in_specs=[tok, tok, tok, tok, const(d, d), const(d, d), const(1, d), const(d, d),
                  mem, mem, const(d, d), const(1, d), const(N_EXPERTS, d),
                  const(N_EXPERTS, 1)],
        out_specs=[tok, tok, lanes, lanes,
                   pl.BlockSpec((ts, 128), lambda bi, j: (bi * nt + j, 0)),
                   const(N_EXPERTS, 128)],
        out_shape=[jax.ShapeDtypeStruct((n, d), F32), jax.ShapeDtypeStruct((n, d), F32),
                   jax.ShapeDtypeStruct((TOP_K, n), I32), jax.ShapeDtypeStruct((TOP_K, n), I32),
                   jax.ShapeDtypeStruct((n, 128), F32),
                   jax.ShapeDtypeStruct((N_EXPERTS, 128), I32)],
        scratch_shapes=[pltpu.VMEM((N_EXPERTS, 1), F32)],
        compiler_params=_cparams(2),
        name="post",
    )(x2d, a2d, g0, gyb, w_attn_up, w_out, norm_x_g, w_xq, k_mem, v_mem, w_xo,
      norm_ffn_g, w_router_t, b_router)


DMA_RING = 8


def _row_copy(src, s_row, dst, d_row, sem):
    return pltpu.make_async_copy(src.at[pl.ds(s_row, 1)], dst.at[pl.ds(d_row, 1)], sem)


def _dispatch_kernel(pstart_ref, eidx_ref, rank_ref, hf_ref, zero_ref, xs_ref, sem):
    del zero_ref
    ts = hf_ref.shape[0]
    total = ts * TOP_K

    def issue(i, carry):
        t = lax.shift_right_logical(i, 2)
        kk = lax.bitwise_and(i, TOP_K - 1)
        slot = lax.bitwise_and(i, DMA_RING - 1)

        @pl.when(i >= DMA_RING)
        def _():
            _row_copy(hf_ref, 0, xs_ref, 0, sem.at[slot]).wait()

        dest = pstart_ref[eidx_ref[kk, t]] + rank_ref[kk, t]
        _row_copy(hf_ref, t, xs_ref, dest, sem.at[slot]).start()
        return carry

    lax.fori_loop(0, total, issue, 0)
    for slot in range(DMA_RING):
        _row_copy(hf_ref, 0, xs_ref, 0, sem.at[slot]).wait()


def _dispatch(pstart, eidx, rank, hf, rows, ts):
    n, d = hf.shape
    smem_lanes = pl.BlockSpec((TOP_K, ts), lambda i, ps: (0, i), memory_space=pltpu.SMEM)
    grid_spec = pltpu.PrefetchScalarGridSpec(
        num_scalar_prefetch=1,
        grid=(n // ts,),
        in_specs=[smem_lanes, smem_lanes,
                  pl.BlockSpec((ts, d), lambda i, ps: (i, 0)),
                  pl.BlockSpec(memory_space=pl.ANY)],
        out_specs=pl.BlockSpec(memory_space=pl.ANY),
        scratch_shapes=[pltpu.SemaphoreType.DMA((DMA_RING,))],
    )
    return pl.pallas_call(
        _dispatch_kernel,
        grid_spec=grid_spec,
        out_shape=jax.ShapeDtypeStruct((rows, d), F32),
        input_output_aliases={4: 0},
        compiler_params=_cparams(1),
        name="dispatch",
    )(pstart, eidx, rank, hf, jnp.zeros((rows, d), F32))


def _combine_kernel(pstart_ref, eidx_ref, rank_ref, x2_ref, gate_ref, fg_ref, ys_ref,
                    o_ref, buf_ref, sem):
    ts = x2_ref.shape[0]
    total = ts * TOP_K

    def issue(i, carry):
        t = lax.shift_right_logical(i, 2)
        kk = lax.bitwise_and(i, TOP_K - 1)
        slot = lax.bitwise_and(i, DMA_RING - 1)

        @pl.when(i >= DMA_RING)
        def _():
            _row_copy(ys_ref, 0, buf_ref.at[0], 0, sem.at[slot]).wait()

        src = pstart_ref[eidx_ref[kk, t]] + rank_ref[kk, t]
        _row_copy(ys_ref, src, buf_ref.at[kk], t, sem.at[slot]).start()
        return carry

    lax.fori_loop(0, total, issue, 0)
    for slot in range(DMA_RING):
        _row_copy(ys_ref, 0, buf_ref.at[0], 0, sem.at[slot]).wait()

    g = gate_ref[...]
    acc = x2_ref[...]
    for kk in range(TOP_K):
        acc = acc + g[:, kk:kk + 1] * buf_ref[kk]
    o_ref[...] = _rms(acc, fg_ref[...])


def _combine(pstart, eidx, rank, x2, gate_tm, final_g, ys, ts):
    n, d = x2.shape
    smem_lanes = pl.BlockSpec((TOP_K, ts), lambda i, ps: (0, i), memory_space=pltpu.SMEM)
    grid_spec = pltpu.PrefetchScalarGridSpec(
        num_scalar_prefetch=1,
        grid=(n // ts,),
        in_specs=[smem_lanes, smem_lanes,
                  pl.BlockSpec((ts, d), lambda i, ps: (i, 0)),
                  pl.BlockSpec((ts, 128), lambda i, ps: (i, 0)),
                  pl.BlockSpec((1, d), lambda i, ps: (0, 0)),
                  pl.BlockSpec(memory_space=pl.ANY)],
        out_specs=pl.BlockSpec((ts, d), lambda i, ps: (i, 0)),
        scratch_shapes=[pltpu.VMEM((TOP_K, ts, d), F32),
                        pltpu.SemaphoreType.DMA((DMA_RING,))],
    )
    return pl.pallas_call(
        _combine_kernel,
        grid_spec=grid_spec,
        out_shape=jax.ShapeDtypeStruct((n, d), F32),
        compiler_params=_cparams(1),
        name="combine",
    )(pstart, eidx, rank, x2, gate_tm, final_g, ys)


def _experts_kernel(be_ref, nb_ref, xs_ref, wgu_ref, bgu_ref, wd_ref, bd_ref, ys_ref,
                    wgu_bf, wd_bf):
    i = pl.program_id(0)
    dff = wd_ref.shape[1]
    prev = be_ref[jnp.maximum(i - 1, 0)]
    changed = (i == 0) | (be_ref[i] != prev)

    @pl.when(changed)
    def _():
        wgu_bf[...] = wgu_ref[0].astype(BF16)
        wd_bf[...] = wd_ref[0].astype(BF16)

    @pl.when(i < nb_ref[0])
    def _():
        xb = xs_ref[...].astype(BF16)
        gu = _dot(xb, wgu_bf[...]) + bgu_ref[0]
        x_glu = jnp.minimum(gu[:, :dff], SWIGLU_LIMIT)
        x_lin = jnp.clip(gu[:, dff:], -SWIGLU_LIMIT, SWIGLU_LIMIT)
        act = x_glu * jax.nn.sigmoid(SWIGLU_ALPHA * x_glu) * (x_lin + 1.0)
        ys_ref[...] = _dot(act.astype(BF16), wd_bf[...]) + bd_ref[0]

    @pl.when(i >= nb_ref[0])
    def _():
        ys_ref[...] = jnp.zeros(ys_ref.shape, F32)


def _experts(block_expert, n_blocks_used, xs, w_gu, b_gu, w_down, b_down, bm):
    rows, d = xs.shape
    ne, _, dff2 = w_gu.shape
    dff = dff2 // 2
    last = lambda i, be, nb: (jnp.minimum(i, nb[0] - 1), 0)
    grid_spec = pltpu.PrefetchScalarGridSpec(
        num_scalar_prefetch=2,
        grid=(rows // bm,),
        in_specs=[
            pl.BlockSpec((bm, d), last),
            pl.BlockSpec((1, d, dff2), lambda i, be, nb: (be[i], 0, 0)),
            pl.BlockSpec((1, 1, dff2), lambda i, be, nb: (be[i], 0, 0)),
            pl.BlockSpec((1, dff, d), lambda i, be, nb: (be[i], 0, 0)),
            pl.BlockSpec((1, 1, d), lambda i, be, nb: (be[i], 0, 0)),
        ],
        out_specs=pl.BlockSpec((bm, d), lambda i, be, nb: (i, 0)),
        scratch_shapes=[pltpu.VMEM((d, dff2), BF16), pltpu.VMEM((dff, d), BF16)],
    )
    return pl.pallas_call(
        _experts_kernel,
        grid_spec=grid_spec,
        out_shape=jax.ShapeDtypeStruct((rows, d), F32),
        compiler_params=_cparams(1),
        name="experts",
    )(block_expert, n_blocks_used, xs, w_gu, b_gu.reshape(ne, 1, dff2), w_down,
      b_down.reshape(ne, 1, d))


MIXER_TILE = 512
POST_TILE = 256
MOVE_TILE = 256
EXPERT_BLOCK = 256


def kernel(x, mem, norm_mix_g, w_in, lambda_q1, lambda_k1, lambda_q2, lambda_k2, rel_bias,
           subln_g, w_attn_up, pool_mix, pool_scale, w_pool_up, w_gate, b_gate, w_out,
           norm_x_g, norm_mem_g, w_xq, w_xkv, w_xo, norm_ffn_g, w_router, b_router,
           w_gu, b_gu, w_down, b_down, final_norm_g):
    b, s, d = x.shape
    n = b * s
    assert w_in.shape[0] == 1, "single-layer block"
    row = lambda a: a.reshape(1, -1)
    bf = lambda a: a[0].astype(BF16)

    q, k, v, g0, gyb = _mixer_in(
        x, row(norm_mix_g[0]), bf(w_in), bf(w_gate), row(b_gate[0]), bf(pool_mix),
        row(pool_scale[0]), bf(w_pool_up), MIXER_TILE)

    bias, lam = _rel_bias(rel_bias, row(lambda_q1[0]), row(lambda_k1[0]),
                          row(lambda_q2[0]), row(lambda_k2[0]), s)
    a = _diff_attn(q, k, v, bias, lam, row(subln_g[0]))

    k_mem, v_mem = _mem_kv(mem, row(norm_mem_g[0]), bf(w_xkv))

    x2, hf, eidx, rank, gate_tm, counts = _post(
        x.reshape(n, d), a.reshape(n, d), g0, gyb, bf(w_attn_up), bf(w_out),
        row(norm_x_g[0]), bf(w_xq), k_mem, v_mem, bf(w_xo), row(norm_ffn_g[0]),
        w_router[0].T, b_router[0].reshape(-1, 1), b, POST_TILE)

    bm = EXPERT_BLOCK
    rows = n * TOP_K + N_EXPERTS * bm
    cnt = counts[:, 0]
    padded = (cnt + bm - 1) // bm * bm
    pend = jnp.cumsum(padded)
    pstart = (pend - padded).astype(I32)
    nb_used = (pend[-1] // bm).astype(I32).reshape(1)
    blk_start = jnp.arange(rows // bm, dtype=I32) * bm
    block_expert = jnp.minimum(jnp.searchsorted(pend, blk_start, side="right"),
                               N_EXPERTS - 1).astype(I32)

    xs = _dispatch(pstart, eidx, rank, hf, rows, MOVE_TILE)
    ys = _experts(block_expert, nb_used, xs, w_gu[0], b_gu[0], w_down[0], b_down[0], bm)
    out = _combine(pstart, eidx, rank, x2, gate_tm, row(final_norm_g), ys, MOVE_TILE)
    return out.reshape(b, s, d)
```

```python
import functools
import math

import numpy as np
import jax
import jax.numpy as jnp
from jax import lax
from jax.experimental import pallas as pl
from jax.experimental.pallas import tpu as pltpu

F32 = jnp.float32
BF16 = jnp.bfloat16
I32 = jnp.int32

EPS = 1e-6
CHUNK = 64
N_HEADS = 8
HEAD_DIM = 64
HEAD_W = 2 * HEAD_DIM
POOL_WINDOWS = (2, 4, 8, 16)
POOL_GROUP = 128
POOL_PAD = 16
N_BUCKETS = 32
MAX_DISTANCE = 128
X_HEADS = 4
N_EXPERTS = 32
TOP_K = 4
SWIGLU_ALPHA = 1.702
SWIGLU_LIMIT = 7.0
LAMBDA_INIT = 0.8 - 0.6 * math.exp(-0.3 * 0)

ATTN_BLOCK = 256
NEG_BIG = -1e30
VMEM_LIMIT = 56 * 1024 * 1024


def _cparams(n_axes, vmem=VMEM_LIMIT):
    return pltpu.CompilerParams(
        dimension_semantics=("arbitrary",) * n_axes, vmem_limit_bytes=vmem)


def _rms(xf, g):
    ms = jnp.mean(xf * xf, axis=-1, keepdims=True)
    return xf * lax.rsqrt(ms + EPS) * g


def _dot(a, b):
    return jnp.dot(a, b, preferred_element_type=F32)


def _dot_nt(a, b, precision=None):
    return lax.dot_general(a, b, (((1,), (1,)), ((), ())),
                           preferred_element_type=F32, precision=precision)


def _mixer_in_kernel(x_ref, g_ref, wq_ref, wk_ref, wv_ref, wu_ref, wg_ref, bg_ref,
                     pmix_ref, pscale_ref, wpu_ref,
                     q_ref, k_ref, v_ref, g0_ref, gyb_ref, ext_ref):
    ts = x_ref.shape[1]
    d = x_ref.shape[2]
    j = pl.program_id(1)
    h = _rms(x_ref[0], g_ref[...]).astype(BF16)
    q_ref[0] = (_dot(h, wq_ref[...]) * (HEAD_DIM ** -0.5)).astype(BF16)
    k_ref[0] = _dot(h, wk_ref[...]).astype(BF16)
    v_ref[0] = _dot(h, wv_ref[...]).astype(BF16)
    u = _dot(h, wu_ref[...])

    @pl.when(j == 0)
    def _():
        ext_ref[0:POOL_PAD, :] = jnp.zeros((POOL_PAD, u.shape[1]), F32)

    ext_ref[POOL_PAD:POOL_PAD + ts, :] = u
    e = ext_ref[...]
    sums = {}
    s = e
    w = 1
    while w < max(POOL_WINDOWS):
        s = s + pltpu.roll(s, w, 0)
        w *= 2
        sums[w] = s
    ext_ref[0:POOL_PAD, :] = ext_ref[ts:ts + POOL_PAD, :]

    pos = (j * ts + lax.broadcasted_iota(I32, (ts, 1), 0) + 1).astype(F32)
    mixed = []
    for gi, w in enumerate(POOL_WINDOWS):
        sl = slice(gi * POOL_GROUP, (gi + 1) * POOL_GROUP)
        win = sums[w][POOL_PAD:, sl]
        pooled = win / jnp.minimum(pos, float(w)) - u[:, sl]
        mixed.append(_dot(pooled.astype(BF16), pmix_ref[gi]) * pscale_ref[:, sl])
    mixed = jnp.concatenate(mixed, axis=1).astype(BF16)
    y_b = _dot(mixed, wpu_ref[...])

    gate = jax.nn.sigmoid(_dot(h, wg_ref[...]) + bg_ref[...])
    g0_ref[...] = gate[:, :d].astype(BF16)
    gyb_ref[...] = (gate[:, d:] * y_b).astype(BF16)


def _mixer_in(x, norm_g, w_in, w_gate, b_gate, pool_mix, pool_scale, w_pool_up, ts):
    b, s, d = x.shape
    aw = N_HEADS * HEAD_W
    pw = len(POOL_WINDOWS) * POOL_GROUP
    n = b * s
    nt = s // ts
    const = lambda *shape: pl.BlockSpec(shape, lambda bi, j: (0,) * len(shape))
    tok3 = pl.BlockSpec((1, ts, aw), lambda bi, j: (bi, j, 0))
    tok2 = pl.BlockSpec((ts, d), lambda bi, j: (bi * nt + j, 0))
    return pl.pallas_call(
        _mixer_in_kernel,
        grid=(b, nt),
        in_specs=[
            pl.BlockSpec((1, ts, d), lambda bi, j: (bi, j, 0)),
            const(1, d),
            pl.BlockSpec((d, aw), lambda bi, j: (0, 0)),
            pl.BlockSpec((d, aw), lambda bi, j: (0, 1)),
            pl.BlockSpec((d, aw), lambda bi, j: (0, 2)),
            pl.BlockSpec((d, pw), lambda bi, j: (0, 3 * aw // pw)),
            const(d, 2 * d),
            const(1, 2 * d),
            const(len(POOL_WINDOWS), POOL_GROUP, POOL_GROUP),
            const(1, pw),
            const(pw, d),
        ],
        out_specs=[tok3, tok3, tok3, tok2, tok2],
        out_shape=[jax.ShapeDtypeStruct((b, s, aw), BF16)] * 3
        + [jax.ShapeDtypeStruct((n, d), BF16)] * 2,
        scratch_shapes=[pltpu.VMEM((ts + POOL_PAD, pw), F32)],
        compiler_params=_cparams(2),
        name="mixer_in",
    )(x, norm_g, w_in, w_in, w_in, w_in, w_gate, b_gate, pool_mix, pool_scale, w_pool_up)


def _rel_bucket(rel, log=jnp.log, f32=lambda a: a.astype(jnp.float32),
                i32=lambda a: a.astype(jnp.int32), xp=jnp):
    nb = N_BUCKETS // 2
    ret = i32(rel > 0) * nb
    n = xp.abs(rel)
    max_exact = nb // 2
    nf = f32(xp.maximum(n, 1))
    large = max_exact + i32(log(nf / max_exact) / math.log(MAX_DISTANCE / max_exact)
                            * (nb - max_exact))
    large = xp.minimum(large, nb - 1)
    return ret + xp.where(n < max_exact, n, large)


def _far_bucket(block, seq):
    rel = -np.arange(block + 1, max(seq, block + 2), dtype=np.int32)
    bk = _rel_bucket(rel, log=np.log, f32=lambda a: a.astype(np.float32),
                     i32=lambda a: a.astype(np.int32), xp=np)
    assert (bk == bk[0]).all(), "far keys must share one relative-position bucket"
    return int(bk[0])


def _rel_bias_kernel(far_bucket, tab_ref, bidx_ref, lq1_ref, lk1_ref, lq2_ref, lk2_ref,
                     bias_ref, lam_ref):
    h = pl.program_id(0)
    bidx = bidx_ref[...]
    acc = jnp.zeros(bidx.shape, F32)
    for bkt in range(N_BUCKETS):
        acc = jnp.where(bidx == bkt, tab_ref[bkt, h], acc)
    acc = acc - tab_ref[far_bucket, h]
    bias_ref[0] = jnp.where(bidx < 0, -jnp.inf, acc)
    lam = (jnp.exp(jnp.sum(lq1_ref[...] * lk1_ref[...], keepdims=True))
           - jnp.exp(jnp.sum(lq2_ref[...] * lk2_ref[...], keepdims=True)) + LAMBDA_INIT)
    lam_ref[...] = jnp.broadcast_to(lam, lam_ref.shape)


def _rel_bias(rel_bias, lq1, lk1, lq2, lk2, seq):
    blk = ATTN_BLOCK
    qpos = jnp.arange(blk, dtype=I32)[:, None]
    kpos = jnp.arange(blk, dtype=I32)[None, :]
    diag = jnp.where(kpos // CHUNK <= qpos // CHUNK, _rel_bucket(kpos - qpos), -1)
    prev = _rel_bucket(kpos - (qpos + blk))
    bidx = jnp.stack([diag, prev]).astype(I32)
    vec = pl.BlockSpec((1, HEAD_DIM), lambda h: (0, 0))
    return pl.pallas_call(
        functools.partial(_rel_bias_kernel, _far_bucket(blk, seq)),
        grid=(N_HEADS,),
        in_specs=[
            pl.BlockSpec(memory_space=pltpu.SMEM),
            pl.BlockSpec((2, blk, blk), lambda h: (0, 0, 0)),
            vec, vec, vec, vec,
        ],
        out_specs=[
            pl.BlockSpec((1, 2, blk, blk), lambda h: (h, 0, 0, 0)),
            pl.BlockSpec((8, 128), lambda h: (0, 0)),
        ],
        out_shape=[jax.ShapeDtypeStruct((N_HEADS, 2, blk, blk), F32),
                   jax.ShapeDtypeStruct((8, 128), F32)],
        compiler_params=_cparams(1),
        name="rel_bias",
    )(rel_bias, bidx, lq1, lk1, lq2, lk2)


def _diff_attn_kernel(q_ref, k_ref, v_ref, bias_ref, lam_ref, sg_ref, o_ref,
                      m_ref, l_ref, acc_ref):
    tq = q_ref.shape[1]
    tk = ATTN_BLOCK
    qi = pl.program_id(2)
    q = q_ref[0]
    lane = lax.broadcasted_iota(I32, q.shape, 1)
    zero = jnp.zeros_like(q)
    qs = jnp.concatenate([jnp.where(lane < HEAD_DIM, q, zero),
                          jnp.where(lane >= HEAD_DIM, q, zero)], axis=0)

    m_ref[...] = jnp.full(m_ref.shape, NEG_BIG, F32)
    l_ref[...] = jnp.zeros(l_ref.shape, F32)
    acc_ref[...] = jnp.zeros(acc_ref.shape, F32)

    def block(j, bias):
        start = pl.multiple_of(j * tk, tk)
        kb = k_ref[0, pl.ds(start, tk), :]
        vb = v_ref[0, pl.ds(start, tk), :]
        s = _dot_nt(qs, kb)
        if bias is not None:
            s = s + jnp.concatenate([bias, bias], axis=0)
        m_old = m_ref[...]
        m_new = jnp.maximum(m_old, jnp.max(s, axis=-1, keepdims=True))
        alpha = jnp.exp(m_old - m_new)
        p = jnp.exp(s - m_new)
        l_ref[...] = alpha * l_ref[...] + jnp.sum(p, axis=-1, keepdims=True)
        acc_ref[...] = alpha * acc_ref[...] + _dot(p.astype(BF16), vb)
        m_ref[...] = m_new

    def far(j, carry):
        block(j, None)
        return carry

    lax.fori_loop(0, jnp.maximum(qi - 1, 0), far, 0)

    @pl.when(qi > 0)
    def _():
        block(qi - 1, bias_ref[0, 1])

    block(qi, bias_ref[0, 0])

    o = acc_ref[...] / l_ref[...]
    lam = lam_ref[0:1, 0:1]
    a = o[:tq] - lam * o[tq:]
    o_ref[0] = (_rms(a, sg_ref[...]) * (1.0 - LAMBDA_INIT)).astype(BF16)


def _diff_attn(q, k, v, bias, lam, subln_g):
    b, s, aw = q.shape
    tq = ATTN_BLOCK
    qspec = pl.BlockSpec((1, tq, HEAD_W), lambda bi, h, qi: (bi, qi, h))
    kvspec = pl.BlockSpec((1, s, HEAD_W), lambda bi, h, qi: (bi, 0, h))
    return pl.pallas_call(
        _diff_attn_kernel,
        grid=(b, N_HEADS, s // tq),
        in_specs=[
            qspec, kvspec, kvspec,
            pl.BlockSpec((1, 2, tq, tq), lambda bi, h, qi: (h, 0, 0, 0)),
            pl.BlockSpec((8, 128), lambda bi, h, qi: (0, 0)),
            pl.BlockSpec((1, HEAD_W), lambda bi, h, qi: (0, 0)),
        ],
        out_specs=qspec,
        out_shape=jax.ShapeDtypeStruct((b, s, aw), BF16),
        scratch_shapes=[pltpu.VMEM((2 * tq, 1), F32), pltpu.VMEM((2 * tq, 1), F32),
                        pltpu.VMEM((2 * tq, HEAD_W), F32)],
        compiler_params=_cparams(3),
        name="diff_attn",
    )(q, k, v, bias, lam, subln_g)


def _mem_kv_kernel(m_ref, g_ref, w_ref, k_ref, v_ref):
    d = m_ref.shape[2]
    m = _rms(m_ref[0], g_ref[...]).astype(BF16)
    kv = _dot(m, w_ref[...])
    k_ref[0] = kv[:, :d].astype(BF16)
    v_ref[0] = kv[:, d:].astype(BF16)


def _mem_kv(mem, norm_g, w_xkv):
    b, ml, d = mem.shape
    blk = pl.BlockSpec((1, ml, d), lambda bi: (bi, 0, 0))
    return pl.pallas_call(
        _mem_kv_kernel,
        grid=(b,),
        in_specs=[blk, pl.BlockSpec((1, d), lambda bi: (0, 0)),
                  pl.BlockSpec((d, 2 * d), lambda bi: (0, 0))],
        out_specs=[blk, blk],
        out_shape=[jax.ShapeDtypeStruct((b, ml, d), BF16)] * 2,
        compiler_params=_cparams(1),
        name="mem_kv",
    )(mem, norm_g, w_xkv)


def _post_kernel(x_ref, a_ref, g0_ref, gyb_ref, wau_ref, wo_ref, nxg_ref, wxq_ref,
                 km_ref, vm_ref, wxo_ref, nfg_ref, wrt_ref, br_ref,
                 x2_ref, hf_ref, eidx_ref, rank_ref, gate_ref, cnt_ref, run_ref):
    ts, d = x_ref.shape
    first = (pl.program_id(0) == 0) & (pl.program_id(1) == 0)

    y_a = _dot(a_ref[...], wau_ref[...])
    merged = g0_ref[...].astype(F32) * y_a + gyb_ref[...].astype(F32)
    x1 = x_ref[...] + _dot(merged.astype(BF16), wo_ref[...])

    hd = d // X_HEADS
    hx = _rms(x1, nxg_ref[...]).astype(BF16)
    qx = (_dot(hx, wxq_ref[...]) * (hd ** -0.5)).astype(BF16)
    heads = []
    for hh in range(X_HEADS):
        sl = slice(hh * hd, (hh + 1) * hd)
        s = _dot_nt(qx[:, sl], km_ref[0, :, sl])
        p = jnp.exp(s - jnp.max(s, axis=-1, keepdims=True))
        p = p / jnp.sum(p, axis=-1, keepdims=True)
        heads.append(_dot(p.astype(BF16), vm_ref[0, :, sl]))
    o = jnp.concatenate(heads, axis=1).astype(BF16)
    x2 = x1 + _dot(o, wxo_ref[...])
    x2_ref[...] = x2
    hf = _rms(x2, nfg_ref[...])
    hf_ref[...] = hf

    logits = _dot_nt(wrt_ref[...], hf, precision=lax.Precision.HIGHEST) + br_ref[...]
    ne = logits.shape[0]
    eid = lax.broadcasted_iota(I32, logits.shape, 0).astype(F32)
    work = logits
    vals, idxs, hots = [], [], []
    for _ in range(TOP_K):
        mx = jnp.max(work, axis=0, keepdims=True)
        idx = jnp.min(jnp.where(work == mx, eid, float(ne)), axis=0, keepdims=True)
        hot = eid == idx
        vals.append(mx)
        idxs.append(idx.astype(I32))
        hots.append(hot)
        work = jnp.where(hot, -jnp.inf, work)
    ex = [jnp.exp(vv - vals[0]) for vv in vals]
    den = ex[0] + ex[1] + ex[2] + ex[3]
    gates = [e_ / den for e_ in ex]

    multi = (hots[0] | hots[1] | hots[2] | hots[3])
    multi_f = jnp.where(multi, 1.0, 0.0).astype(F32)
    tri = (lax.broadcasted_iota(I32, (ts, ts), 0)
           < lax.broadcasted_iota(I32, (ts, ts), 1))
    before = _dot(multi_f.astype(BF16), jnp.where(tri, 1.0, 0.0).astype(BF16))

    @pl.when(first)
    def _():
        run_ref[...] = jnp.zeros(run_ref.shape, F32)

    run = run_ref[...]
    pos = before + run
    ranks = [jnp.sum(jnp.where(hot, pos, 0.0), axis=0, keepdims=True) for hot in hots]
    run_new = run + jnp.sum(multi_f, axis=1, keepdims=True)
    run_ref[...] = run_new
    cnt_ref[...] = jnp.broadcast_to(run_new, cnt_ref.shape).astype(I32)

    eidx_ref[...] = jnp.concatenate(idxs, axis=0)
    rank_ref[...] = jnp.concatenate(ranks, axis=0).astype(I32)
    g_rows = jnp.concatenate(gates + [jnp.zeros((128 - TOP_K, ts), F32)], axis=0)
    gate_ref[...] = g_rows.T


def _post(x2d, a2d, g0, gyb, w_attn_up, w_out, norm_x_g, w_xq, k_mem, v_mem, w_xo,
          norm_ffn_g, w_router_t, b_router, batch, ts):
    n, d = x2d.shape
    nt = n // batch // ts
    ml = k_mem.shape[1]
    tok = pl.BlockSpec((ts, d), lambda bi, j: (bi * nt + j, 0))
    lanes = pl.BlockSpec((TOP_K, ts), lambda bi, j: (0, bi * nt + j))
    const = lambda *shape: pl.BlockSpec(shape, lambda bi, j: (0,) * len(shape))
    mem = pl.BlockSpec((1, ml, d), lambda bi, j: (bi, 0, 0))
    return pl.pallas_call(
        _post_kernel,
        grid=(batch, nt),
        in_specs=[tok, tok, tok, tok, const(d, d), const(d, d), const(1, d), const(d, d),
                  mem, mem, const(d, d), const(1, d), const(N_EXPERTS, d),
                  const(N_EXPERTS, 1)],
        out_specs=[tok, tok, lanes, lanes,
                   pl.BlockSpec((ts, 128), lambda bi, j: (bi * nt + j, 0)),
                   const(N_EXPERTS, 128)],
        out_shape=[jax.ShapeDtypeStruct((n, d), F32), jax.ShapeDtypeStruct((n, d), F32),
                   jax.ShapeDtypeStruct((TOP_K, n), I32), jax.ShapeDtypeStruct((TOP_K, n), I32),
                   jax.ShapeDtypeStruct((n, 128), F32),
                   jax.ShapeDtypeStruct((N_EXPERTS, 128), I32)],
        scratch_shapes=[pltpu.VMEM((N_EXPERTS, 1), F32)],
        compiler_params=_cparams(2),
        name="post",
    )(x2d, a2d, g0, gyb, w_attn_up, w_out, norm_x_g, w_xq, k_mem, v_mem, w_xo,
      norm_ffn_g, w_router_t, b_router)


def _row_copy(src, s_row, dst, d_row, sem):
    return pltpu.make_async_copy(src.at[pl.ds(s_row, 1)], dst.at[pl.ds(d_row, 1)], sem)


ISSUE_UNROLL = 8


def _dispatch_kernel(pstart_ref, eidx_ref, rank_ref, hf_ref, zero_ref, xs_ref, sem):
    del zero_ref
    ts = hf_ref.shape[0]

    for kk in range(TOP_K):
        def issue(t, carry, kk=kk):
            dest = pstart_ref[eidx_ref[kk, t]] + rank_ref[kk, t]
            _row_copy(hf_ref, t, xs_ref, dest, sem).start()
            return carry

        lax.fori_loop(0, ts, issue, 0, unroll=ISSUE_UNROLL)
    for kk in range(TOP_K):
        pltpu.make_async_copy(hf_ref, xs_ref.at[pl.ds(0, ts)], sem).wait()


def _dispatch(pstart, eidx, rank, hf, rows, ts):
    n, d = hf.shape
    smem_lanes = pl.BlockSpec((TOP_K, ts), lambda i, ps: (0, i), memory_space=pltpu.SMEM)
    grid_spec = pltpu.PrefetchScalarGridSpec(
        num_scalar_prefetch=1,
        grid=(n // ts,),
        in_specs=[smem_lanes, smem_lanes,
                  pl.BlockSpec((ts, d), lambda i, ps: (i, 0)),
                  pl.BlockSpec(memory_space=pl.ANY)],
        out_specs=pl.BlockSpec(memory_space=pl.ANY),
        scratch_shapes=[pltpu.SemaphoreType.DMA(())],
    )
    return pl.pallas_call(
        _dispatch_kernel,
        grid_spec=grid_spec,
        out_shape=jax.ShapeDtypeStruct((rows, d), F32),
        input_output_aliases={4: 0},
        compiler_params=_cparams(1),
        name="dispatch",
    )(pstart, eidx, rank, hf, jnp.zeros((rows, d), F32))


def _combine_kernel(pstart_ref, eidx_ref, rank_ref, x2_ref, gate_ref, fg_ref, ys_ref,
                    o_ref, buf_ref, sem):
    ts = x2_ref.shape[0]

    for kk in range(TOP_K):
        def issue(t, carry, kk=kk):
            src = pstart_ref[eidx_ref[kk, t]] + rank_ref[kk, t]
            _row_copy(ys_ref, src, buf_ref.at[kk], t, sem).start()
            return carry

        lax.fori_loop(0, ts, issue, 0, unroll=ISSUE_UNROLL)
    for kk in range(TOP_K):
        pltpu.make_async_copy(ys_ref.at[pl.ds(0, ts)], buf_ref.at[kk], sem).wait()

    g = gate_ref[...]
    acc = x2_ref[...]
    for kk in range(TOP_K):
        acc = acc + g[:, kk:kk + 1] * buf_ref[kk]
    o_ref[...] = _rms(acc, fg_ref[...])


def _combine(pstart, eidx, rank, x2, gate_tm, final_g, ys, ts):
    n, d = x2.shape
    smem_lanes = pl.BlockSpec((TOP_K, ts), lambda i, ps: (0, i), memory_space=pltpu.SMEM)
    grid_spec = pltpu.PrefetchScalarGridSpec(
        num_scalar_prefetch=1,
        grid=(n // ts,),
        in_specs=[smem_lanes, smem_lanes,
                  pl.BlockSpec((ts, d), lambda i, ps: (i, 0)),
                  pl.BlockSpec((ts, 128), lambda i, ps: (i, 0)),
                  pl.BlockSpec((1, d), lambda i, ps: (0, 0)),
                  pl.BlockSpec(memory_space=pl.ANY)],
        out_specs=pl.BlockSpec((ts, d), lambda i, ps: (i, 0)),
        scratch_shapes=[pltpu.VMEM((TOP_K, ts, d), F32),
                        pltpu.SemaphoreType.DMA(())],
    )
    return pl.pallas_call(
        _combine_kernel,
        grid_spec=grid_spec,
        out_shape=jax.ShapeDtypeStruct((n, d), F32),
        compiler_params=_cparams(1),
        name="combine",
    )(pstart, eidx, rank, x2, gate_tm, final_g, ys)


def _experts_kernel(be_ref, nb_ref, xs_ref, wgu_ref, bgu_ref, wd_ref, bd_ref, ys_ref,
                    wgu_bf, wd_bf):
    i = pl.program_id(0)
    dff = wd_ref.shape[1]
    prev = be_ref[jnp.maximum(i - 1, 0)]
    changed = (i == 0) | (be_ref[i] != prev)

    @pl.when(changed)
    def _():
        wgu_bf[...] = wgu_ref[0].astype(BF16)
        wd_bf[...] = wd_ref[0].astype(BF16)

    @pl.when(i < nb_ref[0])
    def _():
        xb = xs_ref[...].astype(BF16)
        gu = _dot(xb, wgu_bf[...]) + bgu_ref[0]
        x_glu = jnp.minimum(gu[:, :dff], SWIGLU_LIMIT)
        x_lin = jnp.clip(gu[:, dff:], -SWIGLU_LIMIT, SWIGLU_LIMIT)
        act = x_glu * jax.nn.sigmoid(SWIGLU_ALPHA * x_glu) * (x_lin + 1.0)
        ys_ref[...] = _dot(act.astype(BF16), wd_bf[...]) + bd_ref[0]

    @pl.when(i >= nb_ref[0])
    def _():
        ys_ref[...] = jnp.zeros(ys_ref.shape, F32)


def _experts(block_expert, n_blocks_used, xs, w_gu, b_gu, w_down, b_down, bm):
    rows, d = xs.shape
    ne, _, dff2 = w_gu.shape
    dff = dff2 // 2
    last = lambda i, be, nb: (jnp.minimum(i, nb[0] - 1), 0)
    grid_spec = pltpu.PrefetchScalarGridSpec(
        num_scalar_prefetch=2,
        grid=(rows // bm,),
        in_specs=[
            pl.BlockSpec((bm, d), last),
            pl.BlockSpec((1, d, dff2), lambda i, be, nb: (be[i], 0, 0)),
            pl.BlockSpec((1, 1, dff2), lambda i, be, nb: (be[i], 0, 0)),
            pl.BlockSpec((1, dff, d), lambda i, be, nb: (be[i], 0, 0)),
            pl.BlockSpec((1, 1, d), lambda i, be, nb: (be[i], 0, 0)),
        ],
        out_specs=pl.BlockSpec((bm, d), lambda i, be, nb: (i, 0)),
        scratch_shapes=[pltpu.VMEM((d, dff2), BF16), pltpu.VMEM((dff, d), BF16)],
    )
    return pl.pallas_call(
        _experts_kernel,
        grid_spec=grid_spec,
        out_shape=jax.ShapeDtypeStruct((rows, d), F32),
        compiler_params=_cparams(1),
        name="experts",
    )(block_expert, n_blocks_used, xs, w_gu, b_gu.reshape(ne, 1, dff2), w_down,
      b_down.reshape(ne, 1, d))


MIXER_TILE = 512
POST_TILE = 256
MOVE_TILE = 256
EXPERT_BLOCK = 256


def kernel(x, mem, norm_mix_g, w_in, lambda_q1, lambda_k1, lambda_q2, lambda_k2, rel_bias,
           subln_g, w_attn_up, pool_mix, pool_scale, w_pool_up, w_gate, b_gate, w_out,
           norm_x_g, norm_mem_g, w_xq, w_xkv, w_xo, norm_ffn_g, w_router, b_router,
           w_gu, b_gu, w_down, b_down, final_norm_g):
    b, s, d = x.shape
    n = b * s
    assert w_in.shape[0] == 1, "single-layer block"
    row = lambda a: a.reshape(1, -1)
    bf = lambda a: a[0].astype(BF16)

    q, k, v, g0, gyb = _mixer_in(
        x, row(norm_mix_g[0]), bf(w_in), bf(w_gate), row(b_gate[0]), bf(pool_mix),
        row(pool_scale[0]), bf(w_pool_up), MIXER_TILE)

    bias, lam = _rel_bias(rel_bias, row(lambda_q1[0]), row(lambda_k1[0]),
                          row(lambda_q2[0]), row(lambda_k2[0]), s)
    a = _diff_attn(q, k, v, bias, lam, row(subln_g[0]))

    k_mem, v_mem = _mem_kv(mem, row(norm_mem_g[0]), bf(w_xkv))

    x2, hf, eidx, rank, gate_tm, counts = _post(
        x.reshape(n, d), a.reshape(n, d), g0, gyb, bf(w_attn_up), bf(w_out),
        row(norm_x_g[0]), bf(w_xq), k_mem, v_mem, bf(w_xo), row(norm_ffn_g[0]),
        w_router[0].T, b_router[0].reshape(-1, 1), b, POST_TILE)

    bm = EXPERT_BLOCK
    rows = n * TOP_K + N_EXPERTS * bm
    cnt = counts[:, 0]
    padded = (cnt + bm - 1) // bm * bm
    pend = jnp.cumsum(padded)
    pstart = (pend - padded).astype(I32)
    nb_used = (pend[-1] // bm).astype(I32).reshape(1)
    blk_start = jnp.arange(rows // bm, dtype=I32) * bm
    block_expert = jnp.minimum(
        jnp.sum((pend[None, :] <= blk_start[:, None]).astype(I32), axis=1), N_EXPERTS - 1)

    xs = _dispatch(pstart, eidx, rank, hf, rows, MOVE_TILE)
    ys = _experts(block_expert, nb_used, xs, w_gu[0], b_gu[0], w_down[0], b_down[0], bm)
    out = _combine(pstart, eidx, rank, x2, gate_tm, row(final_norm_g), ys, MOVE_TILE)
    return out.reshape(b, s, d)
```

```python
import functools
import math

import numpy as np
import jax
import jax.numpy as jnp
from jax import lax
from jax.experimental import pallas as pl
from jax.experimental.pallas import tpu as pltpu

F32 = jnp.float32
BF16 = jnp.bfloat16
I32 = jnp.int32

EPS = 1e-6
CHUNK = 64
N_HEADS = 8
HEAD_DIM = 64
HEAD_W = 2 * HEAD_DIM
POOL_WINDOWS = (2, 4, 8, 16)
POOL_GROUP = 128
POOL_PAD = 16
N_BUCKETS = 32
MAX_DISTANCE = 128
X_HEADS = 4
N_EXPERTS = 32
TOP_K = 4
SWIGLU_ALPHA = 1.702
SWIGLU_LIMIT = 7.0
LAMBDA_INIT = 0.8 - 0.6 * math.exp(-0.3 * 0)

ATTN_BLOCK = 256
NEG_BIG = -1e30
VMEM_LIMIT = 56 * 1024 * 1024


def _cparams(n_axes, vmem=VMEM_LIMIT):
    return pltpu.CompilerParams(
        dimension_semantics=("arbitrary",) * n_axes, vmem_limit_bytes=vmem)


def _rms(xf, g):
    ms = jnp.mean(xf * xf, axis=-1, keepdims=True)
    return xf * lax.rsqrt(ms + EPS) * g


def _dot(a, b):
    return jnp.dot(a, b, preferred_element_type=F32)


def _dot_nt(a, b, precision=None):
    return lax.dot_general(a, b, (((1,), (1,)), ((), ())),
                           preferred_element_type=F32, precision=precision)


def _mixer_in_kernel(x_ref, g_ref, wq_ref, wk_ref, wv_ref, wu_ref, wg_ref, bg_ref,
                     pmix_ref, pscale_ref, wpu_ref,
                     q_ref, k_ref, v_ref, g0_ref, gyb_ref, ext_ref):
    ts = x_ref.shape[1]
    d = x_ref.shape[2]
    j = pl.program_id(1)
    h = _rms(x_ref[0], g_ref[...]).astype(BF16)
    q_ref[0] = (_dot(h, wq_ref[...]) * (HEAD_DIM ** -0.5)).astype(BF16)
    k_ref[0] = _dot(h, wk_ref[...]).astype(BF16)
    v_ref[0] = _dot(h, wv_ref[...]).astype(BF16)
    u = _dot(h, wu_ref[...])

    @pl.when(j == 0)
    def _():
        ext_ref[0:POOL_PAD, :] = jnp.zeros((POOL_PAD, u.shape[1]), F32)

    ext_ref[POOL_PAD:POOL_PAD + ts, :] = u
    e = ext_ref[...]
    sums = {}
    s = e
    w = 1
    while w < max(POOL_WINDOWS):
        s = s + pltpu.roll(s, w, 0)
        w *= 2
        sums[w] = s
    ext_ref[0:POOL_PAD, :] = ext_ref[ts:ts + POOL_PAD, :]

    pos = (j * ts + lax.broadcasted_iota(I32, (ts, 1), 0) + 1).astype(F32)
    mixed = []
    for gi, w in enumerate(POOL_WINDOWS):
        sl = slice(gi * POOL_GROUP, (gi + 1) * POOL_GROUP)
        win = sums[w][POOL_PAD:, sl]
        pooled = win / jnp.minimum(pos, float(w)) - u[:, sl]
        mixed.append(_dot(pooled.astype(BF16), pmix_ref[gi]) * pscale_ref[:, sl])
    mixed = jnp.concatenate(mixed, axis=1).astype(BF16)
    y_b = _dot(mixed, wpu_ref[...])

    gate = jax.nn.sigmoid(_dot(h, wg_ref[...]) + bg_ref[...])
    g0_ref[...] = gate[:, :d].astype(BF16)
    gyb_ref[...] = (gate[:, d:] * y_b).astype(BF16)


def _mixer_in(x, norm_g, w_in, w_gate, b_gate, pool_mix, pool_scale, w_pool_up, ts):
    b, s, d = x.shape
    aw = N_HEADS * HEAD_W
    pw = len(POOL_WINDOWS) * POOL_GROUP
    n = b * s
    nt = s // ts
    const = lambda *shape: pl.BlockSpec(shape, lambda bi, j: (0,) * len(shape))
    tok3 = pl.BlockSpec((1, ts, aw), lambda bi, j: (bi, j, 0))
    tok2 = pl.BlockSpec((ts, d), lambda bi, j: (bi * nt + j, 0))
    return pl.pallas_call(
        _mixer_in_kernel,
        grid=(b, nt),
        in_specs=[
            pl.BlockSpec((1, ts, d), lambda bi, j: (bi, j, 0)),
            const(1, d),
            pl.BlockSpec((d, aw), lambda bi, j: (0, 0)),
            pl.BlockSpec((d, aw), lambda bi, j: (0, 1)),
            pl.BlockSpec((d, aw), lambda bi, j: (0, 2)),
            pl.BlockSpec((d, pw), lambda bi, j: (0, 3 * aw // pw)),
            const(d, 2 * d),
            const(1, 2 * d),
            const(len(POOL_WINDOWS), POOL_GROUP, POOL_GROUP),
            const(1, pw),
            const(pw, d),
        ],
        out_specs=[tok3, tok3, tok3, tok2, tok2],
        out_shape=[jax.ShapeDtypeStruct((b, s, aw), BF16)] * 3
        + [jax.ShapeDtypeStruct((n, d), BF16)] * 2,
        scratch_shapes=[pltpu.VMEM((ts + POOL_PAD, pw), F32)],
        compiler_params=_cparams(2),
        name="mixer_in",
    )(x, norm_g, w_in, w_in, w_in, w_in, w_gate, b_gate, pool_mix, pool_scale, w_pool_up)


def _rel_bucket(rel, log=jnp.log, f32=lambda a: a.astype(jnp.float32),
                i32=lambda a: a.astype(jnp.int32), xp=jnp):
    nb = N_BUCKETS // 2
    ret = i32(rel > 0) * nb
    n = xp.abs(rel)
    max_exact = nb // 2
    nf = f32(xp.maximum(n, 1))
    large = max_exact + i32(log(nf / max_exact) / math.log(MAX_DISTANCE / max_exact)
                            * (nb - max_exact))
    large = xp.minimum(large, nb - 1)
    return ret + xp.where(n < max_exact, n, large)


def _far_bucket(block, seq):
    rel = -np.arange(block + 1, max(seq, block + 2), dtype=np.int32)
    bk = _rel_bucket(rel, log=np.log, f32=lambda a: a.astype(np.float32),
                     i32=lambda a: a.astype(np.int32), xp=np)
    assert (bk == bk[0]).all(), "far keys must share one relative-position bucket"
    return int(bk[0])


def _rel_bias_kernel(far_bucket, tab_ref, bidx_ref, lq1_ref, lk1_ref, lq2_ref, lk2_ref,
                     bias_ref, lam_ref):
    h = pl.program_id(0)
    bidx = bidx_ref[...]
    acc = jnp.zeros(bidx.shape, F32)
    for bkt in range(N_BUCKETS):
        acc = jnp.where(bidx == bkt, tab_ref[bkt, h], acc)
    acc = acc - tab_ref[far_bucket, h]
    bias_ref[0] = jnp.where(bidx < 0, -jnp.inf, acc)
    lam = (jnp.exp(jnp.sum(lq1_ref[...] * lk1_ref[...], keepdims=True))
           - jnp.exp(jnp.sum(lq2_ref[...] * lk2_ref[...], keepdims=True)) + LAMBDA_INIT)
    lam_ref[...] = jnp.broadcast_to(lam, lam_ref.shape)


def _rel_bias(rel_bias, lq1, lk1, lq2, lk2, seq):
    blk = ATTN_BLOCK
    qpos = jnp.arange(blk, dtype=I32)[:, None]
    kpos = jnp.arange(blk, dtype=I32)[None, :]
    diag = jnp.where(kpos // CHUNK <= qpos // CHUNK, _rel_bucket(kpos - qpos), -1)
    prev = _rel_bucket(kpos - (qpos + blk))
    bidx = jnp.stack([diag, prev]).astype(I32)
    vec = pl.BlockSpec((1, HEAD_DIM), lambda h: (0, 0))
    return pl.pallas_call(
        functools.partial(_rel_bias_kernel, _far_bucket(blk, seq)),
        grid=(N_HEADS,),
        in_specs=[
            pl.BlockSpec(memory_space=pltpu.SMEM),
            pl.BlockSpec((2, blk, blk), lambda h: (0, 0, 0)),
            vec, vec, vec, vec,
        ],
        out_specs=[
            pl.BlockSpec((1, 2, blk, blk), lambda h: (h, 0, 0, 0)),
            pl.BlockSpec((8, 128), lambda h: (0, 0)),
        ],
        out_shape=[jax.ShapeDtypeStruct((N_HEADS, 2, blk, blk), F32),
                   jax.ShapeDtypeStruct((8, 128), F32)],
        compiler_params=_cparams(1),
        name="rel_bias",
    )(rel_bias, bidx, lq1, lk1, lq2, lk2)


def _diff_attn_kernel(q_ref, k_ref, v_ref, bias_ref, lam_ref, sg_ref, o_ref, vext_ref):
    s_len = q_ref.shape[1]
    tq = ATTN_BLOCK
    vext_ref[:, :HEAD_W] = v_ref[0]
    vext_ref[:, HEAD_W:] = jnp.ones((s_len, HEAD_W), BF16)
    lam = lam_ref[0:1, 0:1]
    lane = lax.broadcasted_iota(I32, (tq, HEAD_W), 1)
    b_diag = bias_ref[0, 0]
    b_prev = bias_ref[0, 1]
    b_diag = jnp.concatenate([b_diag, b_diag], axis=0)
    b_prev = jnp.concatenate([b_prev, b_prev], axis=0)

    for qi in range(s_len // tq):
        n_keys = (qi + 1) * tq
        q = q_ref[0, qi * tq:(qi + 1) * tq, :]
        zero = jnp.zeros_like(q)
        qs = jnp.concatenate([jnp.where(lane < HEAD_DIM, q, zero),
                              jnp.where(lane >= HEAD_DIM, q, zero)], axis=0)
        s = _dot_nt(qs, k_ref[0, :n_keys, :])
        pieces = []
        if qi >= 2:
            pieces.append(s[:, :n_keys - 2 * tq])
        if qi >= 1:
            pieces.append(s[:, n_keys - 2 * tq:n_keys - tq] + b_prev)
        pieces.append(s[:, n_keys - tq:] + b_diag)
        s = jnp.concatenate(pieces, axis=1) if len(pieces) > 1 else pieces[0]
        m = jnp.max(s, axis=-1, keepdims=True)
        p = jnp.exp(s - m).astype(BF16)
        acc = _dot(p, vext_ref[:n_keys, :])
        o = acc[:, :HEAD_W] / acc[:, HEAD_W:HEAD_W + 1]
        a = o[:tq] - lam * o[tq:]
        o_ref[0, qi * tq:(qi + 1) * tq, :] = (
            _rms(a, sg_ref[...]) * (1.0 - LAMBDA_INIT)).astype(BF16)


def _diff_attn(q, k, v, bias, lam, subln_g):
    b, s, aw = q.shape
    tq = ATTN_BLOCK
    spec = pl.BlockSpec((1, s, HEAD_W), lambda bi, h: (bi, 0, h))
    return pl.pallas_call(
        _diff_attn_kernel,
        grid=(b, N_HEADS),
        in_specs=[
            spec, spec, spec,
            pl.BlockSpec((1, 2, tq, tq), lambda bi, h: (h, 0, 0, 0)),
            pl.BlockSpec((8, 128), lambda bi, h: (0, 0)),
            pl.BlockSpec((1, HEAD_W), lambda bi, h: (0, 0)),
        ],
        out_specs=spec,
        out_shape=jax.ShapeDtypeStruct((b, s, aw), BF16),
        scratch_shapes=[pltpu.VMEM((s, 2 * HEAD_W), BF16)],
        compiler_params=_cparams(2),
        name="diff_attn",
    )(q, k, v, bias, lam, subln_g)


def _mem_kv_kernel(m_ref, g_ref, w_ref, k_ref, v_ref):
    d = m_ref.shape[2]
    m = _rms(m_ref[0], g_ref[...]).astype(BF16)
    kv = _dot(m, w_ref[...])
    k_ref[0] = kv[:, :d].astype(BF16)
    v_ref[0] = kv[:, d:].astype(BF16)


def _mem_kv(mem, norm_g, w_xkv):
    b, ml, d = mem.shape
    blk = pl.BlockSpec((1, ml, d), lambda bi: (bi, 0, 0))
    return pl.pallas_call(
        _mem_kv_kernel,
        grid=(b,),
        in_specs=[blk, pl.BlockSpec((1, d), lambda bi: (0, 0)),
                  pl.BlockSpec((d, 2 * d), lambda bi: (0, 0))],
        out_specs=[blk, blk],
        out_shape=[jax.ShapeDtypeStruct((b, ml, d), BF16)] * 2,
        compiler_params=_cparams(1),
        name="mem_kv",
    )(mem, norm_g, w_xkv)


def _post_kernel(x_ref, a_ref, g0_ref, gyb_ref, wau_ref, wo_ref, nxg_ref, wxq_ref,
                 km_ref, vm_ref, wxo_ref, nfg_ref, wrt_ref, br_ref,
                 x2_ref, hf_ref, eidx_ref, rank_ref, gate_ref, cnt_ref, run_ref):
    ts, d = x_ref.shape
    first = (pl.program_id(0) == 0) & (pl.program_id(1) == 0)

    y_a = _dot(a_ref[...], wau_ref[...])
    merged = g0_ref[...].astype(F32) * y_a + gyb_ref[...].astype(F32)
    x1 = x_ref[...] + _dot(merged.astype(BF16), wo_ref[...])

    hd = d // X_HEADS
    hx = _rms(x1, nxg_ref[...]).astype(BF16)
    qx = (_dot(hx, wxq_ref[...]) * (hd ** -0.5)).astype(BF16)
    heads = []
    for hh in range(X_HEADS):
        sl = slice(hh * hd, (hh + 1) * hd)
        s = _dot_nt(qx[:, sl], km_ref[0, :, sl])
        p = jnp.exp(s - jnp.max(s, axis=-1, keepdims=True))
        p = p / jnp.sum(p, axis=-1, keepdims=True)
        heads.append(_dot(p.astype(BF16), vm_ref[0, :, sl]))
    o = jnp.concatenate(heads, axis=1).astype(BF16)
    x2 = x1 + _dot(o, wxo_ref[...])
    x2_ref[...] = x2
    hf = _rms(x2, nfg_ref[...])
    hf_ref[...] = hf

    logits = _dot_nt(wrt_ref[...], hf, precision=lax.Precision.HIGHEST) + br_ref[...]
    ne = logits.shape[0]
    eid = lax.broadcasted_iota(I32, logits.shape, 0).astype(F32)
    work = logits
    vals, idxs, hots = [], [], []
    for _ in range(TOP_K):
        mx = jnp.max(work, axis=0, keepdims=True)
        idx = jnp.min(jnp.where(work == mx, eid, float(ne)), axis=0, keepdims=True)
        hot = eid == idx
        vals.append(mx)
        idxs.append(idx.astype(I32))
        hots.append(hot)
        work = jnp.where(hot, -jnp.inf, work)
    ex = [jnp.exp(vv - vals[0]) for vv in vals]
    den = ex[0] + ex[1] + ex[2] + ex[3]
    gates = [e_ / den for e_ in ex]

    multi = (hots[0] | hots[1] | hots[2] | hots[3])
    multi_f = jnp.where(multi, 1.0, 0.0).astype(F32)
    tri = (lax.broadcasted_iota(I32, (ts, ts), 0)
           < lax.broadcasted_iota(I32, (ts, ts), 1))
    before = _dot(multi_f.astype(BF16), jnp.where(tri, 1.0, 0.0).astype(BF16))

    @pl.when(first)
    def _():
        run_ref[...] = jnp.zeros(run_ref.shape, F32)

    run = run_ref[...]
    pos = before + run
    ranks = [jnp.sum(jnp.where(hot, pos, 0.0), axis=0, keepdims=True) for hot in hots]
    run_new = run + jnp.sum(multi_f, axis=1, keepdims=True)
    run_ref[...] = run_new
    cnt_ref[...] = jnp.broadcast_to(run_new, cnt_ref.shape).astype(I32)

    eidx_ref[...] = jnp.concatenate(idxs, axis=0)
    rank_ref[...] = jnp.concatenate(ranks, axis=0).astype(I32)
    g_rows = jnp.concatenate(gates + [jnp.zeros((128 - TOP_K, ts), F32)], axis=0)
    gate_ref[...] = g_rows.T


def _post(x2d, a2d, g0, gyb, w_attn_up, w_out, norm_x_g, w_xq, k_mem, v_mem, w_xo,
          norm_ffn_g, w_router_t, b_router, batch, ts):
    n, d = x2d.shape
    nt = n // batch // ts
    ml = k_mem.shape[1]
    tok = pl.BlockSpec((ts, d), lambda bi, j: (bi * nt + j, 0))
    lanes = pl.BlockSpec((TOP_K, ts), lambda bi, j: (0, bi * nt + j))
    const = lambda *shape: pl.BlockSpec(shape, lambda bi, j: (0,) * len(shape))
    mem = pl.BlockSpec((1, ml, d), lambda bi, j: (bi, 0, 0))
    return pl.pallas_call(
        _post_kernel,
        grid=(batch, nt),
        in_specs=[tok, tok, tok, tok, const(d, d), const(d, d), const(1, d), const(d, d),
                  mem, mem, const(d, d), const(1, d), const(N_EXPERTS, d),
                  const(N_EXPERTS, 1)],
        out_specs=[tok, tok, lanes, lanes,
                   pl.BlockSpec((ts, 128), lambda bi, j: (bi * nt + j, 0)),
                   const(N_EXPERTS, 128)],
        out_shape=[jax.ShapeDtypeStruct((n, d), F32), jax.ShapeDtypeStruct((n, d), F32),
                   jax.ShapeDtypeStruct((TOP_K, n), I32), jax.ShapeDtypeStruct((TOP_K, n), I32),
                   jax.ShapeDtypeStruct((n, 128), F32),
                   jax.ShapeDtypeStruct((N_EXPERTS, 128), I32)],
        scratch_shapes=[pltpu.VMEM((N_EXPERTS, 1), F32)],
        compiler_params=_cparams(2),
        name="post",
    )(x2d, a2d, g0, gyb, w_attn_up, w_out, norm_x_g, w_xq, k_mem, v_mem, w_xo,
      norm_ffn_g, w_router_t, b_router)


def _row_copy(src, s_row, dst, d_row, sem):
    return pltpu.make_async_copy(src.at[pl.ds(s_row, 1)], dst.at[pl.ds(d_row, 1)], sem)


ISSUE_UNROLL = 8


def _dispatch_kernel(pstart_ref, eidx_ref, rank_ref, hf_ref, zero_ref, xs_ref, sem):
    del zero_ref
    ts = hf_ref.shape[0]

    for kk in range(TOP_K):
        def issue(t, carry, kk=kk):
            dest = pstart_ref[eidx_ref[kk, t]] + rank_ref[kk, t]
            _row_copy(hf_ref, t, xs_ref, dest, sem).start()
            return carry

        lax.fori_loop(0, ts, issue, 0, unroll=ISSUE_UNROLL)
    for kk in range(TOP_K):
        pltpu.make_async_copy(hf_ref, xs_ref.at[pl.ds(0, ts)], sem).wait()


def _dispatch(pstart, eidx, rank, hf, rows, ts):
    n, d = hf.shape
    smem_lanes = pl.BlockSpec((TOP_K, ts), lambda i, ps: (0, i), memory_space=pltpu.SMEM)
    grid_spec = pltpu.PrefetchScalarGridSpec(
        num_scalar_prefetch=1,
        grid=(n // ts,),
        in_specs=[smem_lanes, smem_lanes,
                  pl.BlockSpec((ts, d), lambda i, ps: (i, 0)),
                  pl.BlockSpec(memory_space=pl.ANY)],
        out_specs=pl.BlockSpec(memory_space=pl.ANY),
        scratch_shapes=[pltpu.SemaphoreType.DMA(())],
    )
    return pl.pallas_call(
        _dispatch_kernel,
        grid_spec=grid_spec,
        out_shape=jax.ShapeDtypeStruct((rows, d), F32),
        input_output_aliases={4: 0},
        compiler_params=_cparams(1),
        name="dispatch",
    )(pstart, eidx, rank, hf, jnp.zeros((rows, d), F32))


def _combine_kernel(pstart_ref, eidx_ref, rank_ref, x2_ref, gate_ref, fg_ref, ys_ref,
                    o_ref, buf_ref, sem):
    ts = x2_ref.shape[0]

    for kk in range(TOP_K):
        def issue(t, carry, kk=kk):
            src = pstart_ref[eidx_ref[kk, t]] + rank_ref[kk, t]
            _row_copy(ys_ref, src, buf_ref.at[kk], t, sem).start()
            return carry

        lax.fori_loop(0, ts, issue, 0, unroll=ISSUE_UNROLL)
    for kk in range(TOP_K):
        pltpu.make_async_copy(ys_ref.at[pl.ds(0, ts)], buf_ref.at[kk], sem).wait()

    g = gate_ref[...]
    acc = x2_ref[...]
    for kk in range(TOP_K):
        acc = acc + g[:, kk:kk + 1] * buf_ref[kk]
    o_ref[...] = _rms(acc, fg_ref[...])


def _combine(pstart, eidx, rank, x2, gate_tm, final_g, ys, ts):
    n, d = x2.shape
    smem_lanes = pl.BlockSpec((TOP_K, ts), lambda i, ps: (0, i), memory_space=pltpu.SMEM)
    grid_spec = pltpu.PrefetchScalarGridSpec(
        num_scalar_prefetch=1,
        grid=(n // ts,),
        in_specs=[smem_lanes, smem_lanes,
                  pl.BlockSpec((ts, d), lambda i, ps: (i, 0)),
                  pl.BlockSpec((ts, 128), lambda i, ps: (i, 0)),
                  pl.BlockSpec((1, d), lambda i, ps: (0, 0)),
                  pl.BlockSpec(memory_space=pl.ANY)],
        out_specs=pl.BlockSpec((ts, d), lambda i, ps: (i, 0)),
        scratch_shapes=[pltpu.VMEM((TOP_K, ts, d), F32),
                        pltpu.SemaphoreType.DMA(())],
    )
    return pl.pallas_call(
        _combine_kernel,
        grid_spec=grid_spec,
        out_shape=jax.ShapeDtypeStruct((n, d), F32),
        compiler_params=_cparams(1),
        name="combine",
    )(pstart, eidx, rank, x2, gate_tm, final_g, ys)


def _experts_kernel(be_ref, nb_ref, xs_ref, wgu_ref, bgu_ref, wd_ref, bd_ref, ys_ref,
                    wgu_bf, wd_bf):
    i = pl.program_id(0)
    dff = wd_ref.shape[1]
    prev = be_ref[jnp.maximum(i - 1, 0)]
    changed = (i == 0) | (be_ref[i] != prev)

    @pl.when(changed)
    def _():
        wgu_bf[...] = wgu_ref[0].astype(BF16)
        wd_bf[...] = wd_ref[0].astype(BF16)

    @pl.when(i < nb_ref[0])
    def _():
        xb = xs_ref[...].astype(BF16)
        gu = _dot(xb, wgu_bf[...]) + bgu_ref[0]
        x_glu = jnp.minimum(gu[:, :dff], SWIGLU_LIMIT)
        x_lin = jnp.clip(gu[:, dff:], -SWIGLU_LIMIT, SWIGLU_LIMIT)
        act = x_glu * jax.nn.sigmoid(SWIGLU_ALPHA * x_glu) * (x_lin + 1.0)
        ys_ref[...] = _dot(act.astype(BF16), wd_bf[...]) + bd_ref[0]

    @pl.when(i >= nb_ref[0])
    def _():
        ys_ref[...] = jnp.zeros(ys_ref.shape, F32)


def _experts(block_expert, n_blocks_used, xs, w_gu, b_gu, w_down, b_down, bm):
    rows, d = xs.shape
    ne, _, dff2 = w_gu.shape
    dff = dff2 // 2
    last = lambda i, be, nb: (jnp.maximum(jnp.minimum(i, nb[0] - 1), 0), 0)
    grid_spec = pltpu.PrefetchScalarGridSpec(
        num_scalar_prefetch=2,
        grid=(rows // bm,),
        in_specs=[
            pl.BlockSpec((bm, d), last),
            pl.BlockSpec((1, d, dff2), lambda i, be, nb: (be[i], 0, 0)),
            pl.BlockSpec((1, 1, dff2), lambda i, be, nb: (be[i], 0, 0)),
            pl.BlockSpec((1, dff, d), lambda i, be, nb: (be[i], 0, 0)),
            pl.BlockSpec((1, 1, d), lambda i, be, nb: (be[i], 0, 0)),
        ],
        out_specs=pl.BlockSpec((bm, d), lambda i, be, nb: (i, 0)),
        scratch_shapes=[pltpu.VMEM((d, dff2), BF16), pltpu.VMEM((dff, d), BF16)],
    )
    return pl.pallas_call(
        _experts_kernel,
        grid_spec=grid_spec,
        out_shape=jax.ShapeDtypeStruct((rows, d), F32),
        compiler_params=_cparams(1),
        name="experts",
    )(block_expert, n_blocks_used, xs, w_gu, b_gu.reshape(ne, 1, dff2), w_down,
      b_down.reshape(ne, 1, d))


MIXER_TILE = 512
POST_TILE = 256
MOVE_TILE = 256
EXPERT_BLOCK = 256


def kernel(x, mem, norm_mix_g, w_in, lambda_q1, lambda_k1, lambda_q2, lambda_k2, rel_bias,
           subln_g, w_attn_up, pool_mix, pool_scale, w_pool_up, w_gate, b_gate, w_out,
           norm_x_g, norm_mem_g, w_xq, w_xkv, w_xo, norm_ffn_g, w_router, b_router,
           w_gu, b_gu, w_down, b_down, final_norm_g):
    b, s, d = x.shape
    n = b * s
    assert w_in.shape[0] == 1, "single-layer block"
    row = lambda a: a.reshape(1, -1)
    bf = lambda a: a[0].astype(BF16)

    q, k, v, g0, gyb = _mixer_in(
        x, row(norm_mix_g[0]), bf(w_in), bf(w_gate), row(b_gate[0]), bf(pool_mix),
        row(pool_scale[0]), bf(w_pool_up), MIXER_TILE)

    bias, lam = _rel_bias(rel_bias, row(lambda_q1[0]), row(lambda_k1[0]),
                          row(lambda_q2[0]), row(lambda_k2[0]), s)
    a = _diff_attn(q, k, v, bias, lam, row(subln_g[0]))

    k_mem, v_mem = _mem_kv(mem, row(norm_mem_g[0]), bf(w_xkv))

    x2, hf, eidx, rank, gate_tm, counts = _post(
        x.reshape(n, d), a.reshape(n, d), g0, gyb, bf(w_attn_up), bf(w_out),
        row(norm_x_g[0]), bf(w_xq), k_mem, v_mem, bf(w_xo), row(norm_ffn_g[0]),
        w_router[0].T, b_router[0].reshape(-1, 1), b, POST_TILE)

    bm = EXPERT_BLOCK
    rows = n * TOP_K + N_EXPERTS * bm
    cnt = counts[:, 0]
    padded = (cnt + bm - 1) // bm * bm
    pend = jnp.cumsum(padded)
    pstart = (pend - padded).astype(I32)
    nb_used = (pend[-1] // bm).astype(I32).reshape(1)
    blk_start = jnp.arange(rows // bm, dtype=I32) * bm
    block_expert = jnp.minimum(
        jnp.sum((pend[None, :] <= blk_start[:, None]).astype(I32), axis=1), N_EXPERTS - 1)

    xs = _dispatch(pstart, eidx, rank, hf, rows, MOVE_TILE)
    ys = _experts(block_expert, nb_used, xs, w_gu[0], b_gu[0], w_down[0], b_down[0], bm)
    out = _combine(pstart, eidx, rank, x2, gate_tm, row(final_norm_g), ys, MOVE_TILE)
    return out.reshape(b, s, d)
```

```python
import functools
import math

import numpy as np
import jax
import jax.numpy as jnp
from jax import lax
from jax.experimental import pallas as pl
from jax.experimental.pallas import tpu as pltpu
from jax.experimental.pallas import tpu_sc as plsc

F32 = jnp.float32
BF16 = jnp.bfloat16
I32 = jnp.int32

EPS = 1e-6
CHUNK = 64
N_HEADS = 8
HEAD_DIM = 64
HEAD_W = 2 * HEAD_DIM
POOL_WINDOWS = (2, 4, 8, 16)
POOL_GROUP = 128
POOL_PAD = 16
N_BUCKETS = 32
MAX_DISTANCE = 128
X_HEADS = 4
N_EXPERTS = 32
TOP_K = 4
SWIGLU_ALPHA = 1.702
SWIGLU_LIMIT = 7.0
LAMBDA_INIT = 0.8 - 0.6 * math.exp(-0.3 * 0)

ATTN_BLOCK = 256
NEG_BIG = -1e30
VMEM_LIMIT = 56 * 1024 * 1024


def _cparams(n_axes, vmem=VMEM_LIMIT):
    return pltpu.CompilerParams(
        dimension_semantics=("arbitrary",) * n_axes, vmem_limit_bytes=vmem)


def _rms(xf, g):
    ms = jnp.mean(xf * xf, axis=-1, keepdims=True)
    return xf * lax.rsqrt(ms + EPS) * g


def _dot(a, b):
    return jnp.dot(a, b, preferred_element_type=F32)


def _dot_nt(a, b, precision=None):
    return lax.dot_general(a, b, (((1,), (1,)), ((), ())),
                           preferred_element_type=F32, precision=precision)


def _mixer_in_kernel(x_ref, g_ref, wq_ref, wk_ref, wv_ref, wu_ref, wg_ref, bg_ref,
                     pmix_ref, pscale_ref, wpu_ref,
                     q_ref, k_ref, v_ref, g0_ref, gyb_ref, ext_ref):
    ts = x_ref.shape[1]
    d = x_ref.shape[2]
    j = pl.program_id(1)
    h = _rms(x_ref[0], g_ref[...]).astype(BF16)
    q_ref[0] = (_dot(h, wq_ref[...]) * (HEAD_DIM ** -0.5)).astype(BF16)
    k_ref[0] = _dot(h, wk_ref[...]).astype(BF16)
    v_ref[0] = _dot(h, wv_ref[...]).astype(BF16)
    u = _dot(h, wu_ref[...])

    @pl.when(j == 0)
    def _():
        ext_ref[0:POOL_PAD, :] = jnp.zeros((POOL_PAD, u.shape[1]), F32)

    ext_ref[POOL_PAD:POOL_PAD + ts, :] = u
    e = ext_ref[...]
    sums = {}
    s = e
    w = 1
    while w < max(POOL_WINDOWS):
        s = s + pltpu.roll(s, w, 0)
        w *= 2
        sums[w] = s
    ext_ref[0:POOL_PAD, :] = ext_ref[ts:ts + POOL_PAD, :]

    pos = (j * ts + lax.broadcasted_iota(I32, (ts, 1), 0) + 1).astype(F32)
    mixed = []
    for gi, w in enumerate(POOL_WINDOWS):
        sl = slice(gi * POOL_GROUP, (gi + 1) * POOL_GROUP)
        win = sums[w][POOL_PAD:, sl]
        pooled = win / jnp.minimum(pos, float(w)) - u[:, sl]
        mixed.append(_dot(pooled.astype(BF16), pmix_ref[gi]) * pscale_ref[:, sl])
    mixed = jnp.concatenate(mixed, axis=1).astype(BF16)
    y_b = _dot(mixed, wpu_ref[...])

    gate = jax.nn.sigmoid(_dot(h, wg_ref[...]) + bg_ref[...])
    g0_ref[...] = gate[:, :d].astype(BF16)
    gyb_ref[...] = (gate[:, d:] * y_b).astype(BF16)


def _mixer_in(x, norm_g, w_in, w_gate, b_gate, pool_mix, pool_scale, w_pool_up, ts):
    b, s, d = x.shape
    aw = N_HEADS * HEAD_W
    pw = len(POOL_WINDOWS) * POOL_GROUP
    n = b * s
    nt = s // ts
    const = lambda *shape: pl.BlockSpec(shape, lambda bi, j: (0,) * len(shape))
    tok3 = pl.BlockSpec((1, ts, aw), lambda bi, j: (bi, j, 0))
    tok2 = pl.BlockSpec((ts, d), lambda bi, j: (bi * nt + j, 0))
    return pl.pallas_call(
        _mixer_in_kernel,
        grid=(b, nt),
        in_specs=[
            pl.BlockSpec((1, ts, d), lambda bi, j: (bi, j, 0)),
            const(1, d),
            pl.BlockSpec((d, aw), lambda bi, j: (0, 0)),
            pl.BlockSpec((d, aw), lambda bi, j: (0, 1)),
            pl.BlockSpec((d, aw), lambda bi, j: (0, 2)),
            pl.BlockSpec((d, pw), lambda bi, j: (0, 3 * aw // pw)),
            const(d, 2 * d),
            const(1, 2 * d),
            const(len(POOL_WINDOWS), POOL_GROUP, POOL_GROUP),
            const(1, pw),
            const(pw, d),
        ],
        out_specs=[tok3, tok3, tok3, tok2, tok2],
        out_shape=[jax.ShapeDtypeStruct((b, s, aw), BF16)] * 3
        + [jax.ShapeDtypeStruct((n, d), BF16)] * 2,
        scratch_shapes=[pltpu.VMEM((ts + POOL_PAD, pw), F32)],
        compiler_params=_cparams(2),
        name="mixer_in",
    )(x, norm_g, w_in, w_in, w_in, w_in, w_gate, b_gate, pool_mix, pool_scale, w_pool_up)


def _rel_bucket(rel, log=jnp.log, f32=lambda a: a.astype(jnp.float32),
                i32=lambda a: a.astype(jnp.int32), xp=jnp):
    nb = N_BUCKETS // 2
    ret = i32(rel > 0) * nb
    n = xp.abs(rel)
    max_exact = nb // 2
    nf = f32(xp.maximum(n, 1))
    large = max_exact + i32(log(nf / max_exact) / math.log(MAX_DISTANCE / max_exact)
                            * (nb - max_exact))
    large = xp.minimum(large, nb - 1)
    return ret + xp.where(n < max_exact, n, large)


def _far_bucket(block, seq):
    rel = -np.arange(block + 1, max(seq, block + 2), dtype=np.int32)
    bk = _rel_bucket(rel, log=np.log, f32=lambda a: a.astype(np.float32),
                     i32=lambda a: a.astype(np.int32), xp=np)
    assert (bk == bk[0]).all(), "far keys must share one relative-position bucket"
    return int(bk[0])


def _rel_bias_kernel(far_bucket, tab_ref, bidx_ref, lq1_ref, lk1_ref, lq2_ref, lk2_ref,
                     bias_ref, lam_ref):
    h = pl.program_id(0)
    bidx = bidx_ref[...]
    acc = jnp.zeros(bidx.shape, F32)
    for bkt in range(N_BUCKETS):
        acc = jnp.where(bidx == bkt, tab_ref[bkt, h], acc)
    acc = acc - tab_ref[far_bucket, h]
    bias_ref[0] = jnp.where(bidx < 0, -jnp.inf, acc)
    lam = (jnp.exp(jnp.sum(lq1_ref[...] * lk1_ref[...], keepdims=True))
           - jnp.exp(jnp.sum(lq2_ref[...] * lk2_ref[...], keepdims=True)) + LAMBDA_INIT)
    lam_ref[...] = jnp.broadcast_to(lam, lam_ref.shape)


def _rel_bias(rel_bias, lq1, lk1, lq2, lk2, seq):
    blk = ATTN_BLOCK
    qpos = jnp.arange(blk, dtype=I32)[:, None]
    kpos = jnp.arange(blk, dtype=I32)[None, :]
    diag = jnp.where(kpos // CHUNK <= qpos // CHUNK, _rel_bucket(kpos - qpos), -1)
    prev = _rel_bucket(kpos - (qpos + blk))
    bidx = jnp.stack([diag, prev]).astype(I32)
    vec = pl.BlockSpec((1, HEAD_DIM), lambda h: (0, 0))
    return pl.pallas_call(
        functools.partial(_rel_bias_kernel, _far_bucket(blk, seq)),
        grid=(N_HEADS,),
        in_specs=[
            pl.BlockSpec(memory_space=pltpu.SMEM),
            pl.BlockSpec((2, blk, blk), lambda h: (0, 0, 0)),
            vec, vec, vec, vec,
        ],
        out_specs=[
            pl.BlockSpec((1, 2, blk, blk), lambda h: (h, 0, 0, 0)),
            pl.BlockSpec((8, 128), lambda h: (0, 0)),
        ],
        out_shape=[jax.ShapeDtypeStruct((N_HEADS, 2, blk, blk), F32),
                   jax.ShapeDtypeStruct((8, 128), F32)],
        compiler_params=_cparams(1),
        name="rel_bias",
    )(rel_bias, bidx, lq1, lk1, lq2, lk2)


def _diff_attn_kernel(q_ref, k_ref, v_ref, bias_ref, lam_ref, sg_ref, o_ref, vext_ref):
    s_len = q_ref.shape[1]
    tq = ATTN_BLOCK
    vext_ref[:, :HEAD_W] = v_ref[0]
    vext_ref[:, HEAD_W:] = jnp.ones((s_len, HEAD_W), BF16)
    lam = lam_ref[0:1, 0:1]
    lane = lax.broadcasted_iota(I32, (tq, HEAD_W), 1)
    b_diag = bias_ref[0, 0]
    b_prev = bias_ref[0, 1]
    b_diag = jnp.concatenate([b_diag, b_diag], axis=0)
    b_prev = jnp.concatenate([b_prev, b_prev], axis=0)

    for qi in range(s_len // tq):
        n_keys = (qi + 1) * tq
        q = q_ref[0, qi * tq:(qi + 1) * tq, :]
        zero = jnp.zeros_like(q)
        qs = jnp.concatenate([jnp.where(lane < HEAD_DIM, q, zero),
                              jnp.where(lane >= HEAD_DIM, q, zero)], axis=0)
        s = _dot_nt(qs, k_ref[0, :n_keys, :])
        pieces = []
        if qi >= 2:
            pieces.append(s[:, :n_keys - 2 * tq])
        if qi >= 1:
            pieces.append(s[:, n_keys - 2 * tq:n_keys - tq] + b_prev)
        pieces.append(s[:, n_keys - tq:] + b_diag)
        s = jnp.concatenate(pieces, axis=1) if len(pieces) > 1 else pieces[0]
        m = jnp.max(s, axis=-1, keepdims=True)
        p = jnp.exp(s - m).astype(BF16)
        acc = _dot(p, vext_ref[:n_keys, :])
        o = acc[:, :HEAD_W] / acc[:, HEAD_W:HEAD_W + 1]
        a = o[:tq] - lam * o[tq:]
        o_ref[0, qi * tq:(qi + 1) * tq, :] = (
            _rms(a, sg_ref[...]) * (1.0 - LAMBDA_INIT)).astype(BF16)


def _diff_attn(q, k, v, bias, lam, subln_g):
    b, s, aw = q.shape
    tq = ATTN_BLOCK
    spec = pl.BlockSpec((1, s, HEAD_W), lambda bi, h: (bi, 0, h))
    return pl.pallas_call(
        _diff_attn_kernel,
        grid=(b, N_HEADS),
        in_specs=[
            spec, spec, spec,
            pl.BlockSpec((1, 2, tq, tq), lambda bi, h: (h, 0, 0, 0)),
            pl.BlockSpec((8, 128), lambda bi, h: (0, 0)),
            pl.BlockSpec((1, HEAD_W), lambda bi, h: (0, 0)),
        ],
        out_specs=spec,
        out_shape=jax.ShapeDtypeStruct((b, s, aw), BF16),
        scratch_shapes=[pltpu.VMEM((s, 2 * HEAD_W), BF16)],
        compiler_params=_cparams(2),
        name="diff_attn",
    )(q, k, v, bias, lam, subln_g)


def _mem_kv_kernel(m_ref, g_ref, w_ref, k_ref, v_ref):
    d = m_ref.shape[2]
    m = _rms(m_ref[0], g_ref[...]).astype(BF16)
    kv = _dot(m, w_ref[...])
    k_ref[0] = kv[:, :d].astype(BF16)
    v_ref[0] = kv[:, d:].astype(BF16)


def _mem_kv(mem, norm_g, w_xkv):
    b, ml, d = mem.shape
    blk = pl.BlockSpec((1, ml, d), lambda bi: (bi, 0, 0))
    return pl.pallas_call(
        _mem_kv_kernel,
        grid=(b,),
        in_specs=[blk, pl.BlockSpec((1, d), lambda bi: (0, 0)),
                  pl.BlockSpec((d, 2 * d), lambda bi: (0, 0))],
        out_specs=[blk, blk],
        out_shape=[jax.ShapeDtypeStruct((b, ml, d), BF16)] * 2,
        compiler_params=_cparams(1),
        name="mem_kv",
    )(mem, norm_g, w_xkv)


def _post_kernel(x_ref, a_ref, g0_ref, gyb_ref, wau_ref, wo_ref, nxg_ref, wxq_ref,
                 km_ref, vm_ref, wxo_ref, nfg_ref, wrt_ref, br_ref,
                 x2_ref, hf_ref, eidx_ref, rank_ref, gate_ref, cnt_ref, run_ref):
    ts, d = x_ref.shape
    first = (pl.program_id(0) == 0) & (pl.program_id(1) == 0)

    y_a = _dot(a_ref[...], wau_ref[...])
    merged = g0_ref[...].astype(F32) * y_a + gyb_ref[...].astype(F32)
    x1 = x_ref[...] + _dot(merged.astype(BF16), wo_ref[...])

    hd = d // X_HEADS
    hx = _rms(x1, nxg_ref[...]).astype(BF16)
    qx = (_dot(hx, wxq_ref[...]) * (hd ** -0.5)).astype(BF16)
    heads = []
    for hh in range(X_HEADS):
        sl = slice(hh * hd, (hh + 1) * hd)
        s = _dot_nt(qx[:, sl], km_ref[0, :, sl])
        p = jnp.exp(s - jnp.max(s, axis=-1, keepdims=True))
        p = p / jnp.sum(p, axis=-1, keepdims=True)
        heads.append(_dot(p.astype(BF16), vm_ref[0, :, sl]))
    o = jnp.concatenate(heads, axis=1).astype(BF16)
    x2 = x1 + _dot(o, wxo_ref[...])
    x2_ref[...] = x2
    hf = _rms(x2, nfg_ref[...])
    hf_ref[...] = hf

    logits = _dot_nt(wrt_ref[...], hf, precision=lax.Precision.HIGHEST) + br_ref[...]
    ne = logits.shape[0]
    eid = lax.broadcasted_iota(I32, logits.shape, 0).astype(F32)
    work = logits
    vals, idxs, hots = [], [], []
    for _ in range(TOP_K):
        mx = jnp.max(work, axis=0, keepdims=True)
        idx = jnp.min(jnp.where(work == mx, eid, float(ne)), axis=0, keepdims=True)
        hot = eid == idx
        vals.append(mx)
        idxs.append(idx.astype(I32))
        hots.append(hot)
        work = jnp.where(hot, -jnp.inf, work)
    ex = [jnp.exp(vv - vals[0]) for vv in vals]
    den = ex[0] + ex[1] + ex[2] + ex[3]
    gates = [e_ / den for e_ in ex]

    multi = (hots[0] | hots[1] | hots[2] | hots[3])
    multi_f = jnp.where(multi, 1.0, 0.0).astype(F32)
    tri = (lax.broadcasted_iota(I32, (ts, ts), 0)
           < lax.broadcasted_iota(I32, (ts, ts), 1))
    before = _dot(multi_f.astype(BF16), jnp.where(tri, 1.0, 0.0).astype(BF16))

    @pl.when(first)
    def _():
        run_ref[...] = jnp.zeros(run_ref.shape, F32)

    run = run_ref[...]
    pos = before + run
    ranks = [jnp.sum(jnp.where(hot, pos, 0.0), axis=0, keepdims=True) for hot in hots]
    run_new = run + jnp.sum(multi_f, axis=1, keepdims=True)
    run_ref[...] = run_new
    cnt_ref[...] = jnp.broadcast_to(run_new, cnt_ref.shape).astype(I32)

    eidx_ref[...] = jnp.concatenate(idxs, axis=0)
    rank_ref[...] = jnp.concatenate(ranks, axis=0).astype(I32)
    g_rows = jnp.concatenate(gates + [jnp.zeros((128 - TOP_K, ts), F32)], axis=0)
    gate_ref[...] = g_rows.T


def _post(x2d, a2d, g0, gyb, w_attn_up, w_out, norm_x_g, w_xq, k_mem, v_mem, w_xo,
          norm_ffn_g, w_router_t, b_router, batch, ts):
    n, d = x2d.shape
    nt = n // batch // ts
    ml = k_mem.shape[1]
    tok = pl.BlockSpec((ts, d), lambda bi, j: (bi * nt + j, 0))
    lanes = pl.BlockSpec((TOP_K, ts), lambda bi, j: (0, bi * nt + j))
    const = lambda *shape: pl.BlockSpec(shape, lambda bi, j: (0,) * len(shape))
    mem = pl.BlockSpec((1, ml, d), lambda bi, j: (bi, 0, 0))
    return pl.pallas_call(
        _post_kernel,
        grid=(batch, nt),
        in_specs=[tok, tok, tok, tok, const(d, d), const(d, d), const(1, d), const(d, d),
                  mem, mem, const(d, d), const(1, d), const(N_EXPERTS, d),
                  const(N_EXPERTS, 1)],
        out_specs=[tok, tok, lanes, lanes,
                   pl.BlockSpec((ts, 128), lambda bi, j: (bi * nt + j, 0)),
                   const(N_EXPERTS, 128)],
        out_shape=[jax.ShapeDtypeStruct((n, d), F32), jax.ShapeDtypeStruct((n, d), F32),
                   jax.ShapeDtypeStruct((TOP_K, n), I32), jax.ShapeDtypeStruct((TOP_K, n), I32),
                   jax.ShapeDtypeStruct((n, 128), F32),
                   jax.ShapeDtypeStruct((N_EXPERTS, 128), I32)],
        scratch_shapes=[pltpu.VMEM((N_EXPERTS, 1), F32)],
        compiler_params=_cparams(2),
        name="post",
    )(x2d, a2d, g0, gyb, w_attn_up, w_out, norm_x_g, w_xq, k_mem, v_mem, w_xo,
      norm_ffn_g, w_router_t, b_router)


def _route_dest_kernel(pstart_ref, eidx_ref, rank_ref, dest_ref):
    eidx = eidx_ref[...]
    start = jnp.zeros(eidx.shape, I32)
    for e in range(N_EXPERTS):
        start = jnp.where(eidx == e, pstart_ref[e], start)
    dest_ref[...] = start + rank_ref[...]


def _route_dest(pstart, eidx, rank, tl):
    k, n = eidx.shape
    lanes = pl.BlockSpec((k, tl), lambda i: (0, i))
    return pl.pallas_call(
        _route_dest_kernel,
        grid=(n // tl,),
        in_specs=[pl.BlockSpec(memory_space=pltpu.SMEM), lanes, lanes],
        out_specs=lanes,
        out_shape=jax.ShapeDtypeStruct((k, n), I32),
        compiler_params=_cparams(1),
        name="route_dest",
    )(pstart, eidx, rank)


SC_CORES = 2
SC_SUBCORES = 16
SC_WORKERS = SC_CORES * SC_SUBCORES
MOVE_CHUNK = 32


def _worker_index_layout(dest, n):
    per_w = n // SC_WORKERS
    n_chunks = per_w // MOVE_CHUNK
    d4 = dest.reshape(TOP_K, SC_WORKERS, n_chunks, MOVE_CHUNK)
    return jnp.transpose(d4, (1, 2, 0, 3)).reshape(SC_WORKERS, n_chunks * TOP_K, MOVE_CHUNK)


def _sc_mesh():
    return plsc.VectorSubcoreMesh(core_axis_name="c", subcore_axis_name="s",
                                  num_cores=SC_CORES, num_subcores=SC_SUBCORES)


def _sc_dispatch(hf, dest_w, rows):
    n, d = hf.shape
    per_w = n // SC_WORKERS
    n_chunks = per_w // MOVE_CHUNK

    def body(hf_hbm, dest_hbm, xs_hbm, idx_v, rows_v, sem):
        wid = lax.axis_index("s") * SC_CORES + lax.axis_index("c")
        base = wid * per_w
        pltpu.sync_copy(dest_hbm.at[wid], idx_v)
        for c in range(n_chunks):
            pltpu.sync_copy(hf_hbm.at[pl.ds(base + c * MOVE_CHUNK, MOVE_CHUNK)], rows_v)
            copies = [pltpu.async_copy(rows_v, xs_hbm.at[idx_v.at[c * TOP_K + kk]], sem)
                      for kk in range(TOP_K)]
            for cp in copies:
                cp.wait()

    return pl.kernel(
        body,
        out_type=jax.ShapeDtypeStruct((rows, d), hf.dtype),
        mesh=_sc_mesh(),
        scratch_types=[pltpu.VMEM((n_chunks * TOP_K, MOVE_CHUNK), I32),
                       pltpu.VMEM((MOVE_CHUNK, d), hf.dtype),
                       pltpu.SemaphoreType.DMA],
        name="sc_dispatch",
    )(hf, dest_w)


def _sc_gather(ys, dest_w, n):
    _, d = ys.shape
    per_w = n // SC_WORKERS
    n_chunks = per_w // MOVE_CHUNK

    def body(ys_hbm, dest_hbm, yg_hbm, idx_v, rows_v, sem):
        wid = lax.axis_index("s") * SC_CORES + lax.axis_index("c")
        base = wid * per_w
        pltpu.sync_copy(dest_hbm.at[wid], idx_v)
        for c in range(n_chunks):
            for kk in range(TOP_K):
                pltpu.async_copy(ys_hbm.at[idx_v.at[c * TOP_K + kk]], rows_v, sem).wait()
                pltpu.sync_copy(
                    rows_v, yg_hbm.at[pl.ds(kk * n + base + c * MOVE_CHUNK, MOVE_CHUNK)])

    return pl.kernel(
        body,
        out_type=jax.ShapeDtypeStruct((TOP_K * n, d), ys.dtype),
        mesh=_sc_mesh(),
        scratch_types=[pltpu.VMEM((n_chunks * TOP_K, MOVE_CHUNK), I32),
                       pltpu.VMEM((MOVE_CHUNK, d), ys.dtype),
                       pltpu.SemaphoreType.DMA],
        name="sc_gather",
    )(ys, dest_w)


def _combine_kernel(x2_ref, gate_ref, fg_ref, yg_ref, o_ref):
    g = gate_ref[...]
    acc = x2_ref[...]
    for kk in range(TOP_K):
        acc = acc + g[:, kk:kk + 1] * yg_ref[kk]
    o_ref[...] = _rms(acc, fg_ref[...])


def _combine(x2, gate_tm, final_g, yg, ts):
    n, d = x2.shape
    return pl.pallas_call(
        _combine_kernel,
        grid=(n // ts,),
        in_specs=[pl.BlockSpec((ts, d), lambda i: (i, 0)),
                  pl.BlockSpec((ts, 128), lambda i: (i, 0)),
                  pl.BlockSpec((1, d), lambda i: (0, 0)),
                  pl.BlockSpec((TOP_K, ts, d), lambda i: (0, i, 0))],
        out_specs=pl.BlockSpec((ts, d), lambda i: (i, 0)),
        out_shape=jax.ShapeDtypeStruct((n, d), F32),
        compiler_params=_cparams(1),
        name="combine",
    )(x2, gate_tm, final_g, yg.reshape(TOP_K, n, d))


def _experts_kernel(be_ref, nb_ref, xs_ref, wgu_ref, bgu_ref, wd_ref, bd_ref, ys_ref,
                    wgu_bf, wd_bf):
    i = pl.program_id(0)
    dff = wd_ref.shape[1]
    prev = be_ref[jnp.maximum(i - 1, 0)]
    changed = (i == 0) | (be_ref[i] != prev)

    @pl.when(changed)
    def _():
        wgu_bf[...] = wgu_ref[0].astype(BF16)
        wd_bf[...] = wd_ref[0].astype(BF16)

    @pl.when(i < nb_ref[0])
    def _():
        xb = xs_ref[...].astype(BF16)
        gu = _dot(xb, wgu_bf[...]) + bgu_ref[0]
        x_glu = jnp.minimum(gu[:, :dff], SWIGLU_LIMIT)
        x_lin = jnp.clip(gu[:, dff:], -SWIGLU_LIMIT, SWIGLU_LIMIT)
        act = x_glu * jax.nn.sigmoid(SWIGLU_ALPHA * x_glu) * (x_lin + 1.0)
        ys_ref[...] = _dot(act.astype(BF16), wd_bf[...]) + bd_ref[0]

    @pl.when(i >= nb_ref[0])
    def _():
        ys_ref[...] = jnp.zeros(ys_ref.shape, F32)


def _experts(block_expert, n_blocks_used, xs, w_gu, b_gu, w_down, b_down, bm):
    rows, d = xs.shape
    ne, _, dff2 = w_gu.shape
    dff = dff2 // 2
    last = lambda i, be, nb: (jnp.maximum(jnp.minimum(i, nb[0] - 1), 0), 0)
    grid_spec = pltpu.PrefetchScalarGridSpec(
        num_scalar_prefetch=2,
        grid=(rows // bm,),
        in_specs=[
            pl.BlockSpec((bm, d), last),
            pl.BlockSpec((1, d, dff2), lambda i, be, nb: (be[i], 0, 0)),
            pl.BlockSpec((1, 1, dff2), lambda i, be, nb: (be[i], 0, 0)),
            pl.BlockSpec((1, dff, d), lambda i, be, nb: (be[i], 0, 0)),
            pl.BlockSpec((1, 1, d), lambda i, be, nb: (be[i], 0, 0)),
        ],
        out_specs=pl.BlockSpec((bm, d), lambda i, be, nb: (i, 0)),
        scratch_shapes=[pltpu.VMEM((d, dff2), BF16), pltpu.VMEM((dff, d), BF16)],
    )
    return pl.pallas_call(
        _experts_kernel,
        grid_spec=grid_spec,
        out_shape=jax.ShapeDtypeStruct((rows, d), F32),
        compiler_params=_cparams(1),
        name="experts",
    )(block_expert, n_blocks_used, xs, w_gu, b_gu.reshape(ne, 1, dff2), w_down,
      b_down.reshape(ne, 1, d))


MIXER_TILE = 512
POST_TILE = 256
DEST_TILE = 2048
COMBINE_TILE = 256
EXPERT_BLOCK = 256


def kernel(x, mem, norm_mix_g, w_in, lambda_q1, lambda_k1, lambda_q2, lambda_k2, rel_bias,
           subln_g, w_attn_up, pool_mix, pool_scale, w_pool_up, w_gate, b_gate, w_out,
           norm_x_g, norm_mem_g, w_xq, w_xkv, w_xo, norm_ffn_g, w_router, b_router,
           w_gu, b_gu, w_down, b_down, final_norm_g):
    b, s, d = x.shape
    n = b * s
    assert w_in.shape[0] == 1, "single-layer block"
    row = lambda a: a.reshape(1, -1)
    bf = lambda a: a[0].astype(BF16)

    q, k, v, g0, gyb = _mixer_in(
        x, row(norm_mix_g[0]), bf(w_in), bf(w_gate), row(b_gate[0]), bf(pool_mix),
        row(pool_scale[0]), bf(w_pool_up), MIXER_TILE)

    bias, lam = _rel_bias(rel_bias, row(lambda_q1[0]), row(lambda_k1[0]),
                          row(lambda_q2[0]), row(lambda_k2[0]), s)
    a = _diff_attn(q, k, v, bias, lam, row(subln_g[0]))

    k_mem, v_mem = _mem_kv(mem, row(norm_mem_g[0]), bf(w_xkv))

    x2, hf, eidx, rank, gate_tm, counts = _post(
        x.reshape(n, d), a.reshape(n, d), g0, gyb, bf(w_attn_up), bf(w_out),
        row(norm_x_g[0]), bf(w_xq), k_mem, v_mem, bf(w_xo), row(norm_ffn_g[0]),
        w_router[0].T, b_router[0].reshape(-1, 1), b, POST_TILE)

    bm = EXPERT_BLOCK
    rows = n * TOP_K + N_EXPERTS * bm
    cnt = counts[:, 0]
    padded = (cnt + bm - 1) // bm * bm
    pend = jnp.cumsum(padded)
    pstart = (pend - padded).astype(I32)
    nb_used = (pend[-1] // bm).astype(I32).reshape(1)
    blk_start = jnp.arange(rows // bm, dtype=I32) * bm
    block_expert = jnp.minimum(
        jnp.sum((pend[None, :] <= blk_start[:, None]).astype(I32), axis=1), N_EXPERTS - 1)

    dest_w = _worker_index_layout(_route_dest(pstart, eidx, rank, DEST_TILE), n)
    xs = _sc_dispatch(hf, dest_w, rows)
    ys = _experts(block_expert, nb_used, xs, w_gu[0], b_gu[0], w_down[0], b_down[0], bm)
    yg = _sc_gather(ys, dest_w, n)
    out = _combine(x2, gate_tm, row(final_norm_g), yg, COMBINE_TILE)
    return out.reshape(b, s, d)
```

```python
import functools
import math

import numpy as np
import jax
import jax.numpy as jnp
from jax import lax
from jax.experimental import pallas as pl
from jax.experimental.pallas import tpu as pltpu
from jax.experimental.pallas import tpu_sc as plsc

F32 = jnp.float32
BF16 = jnp.bfloat16
I32 = jnp.int32

EPS = 1e-6
CHUNK = 64
N_HEADS = 8
HEAD_DIM = 64
HEAD_W = 2 * HEAD_DIM
POOL_WINDOWS = (2, 4, 8, 16)
POOL_GROUP = 128
POOL_PAD = 16
N_BUCKETS = 32
MAX_DISTANCE = 128
X_HEADS = 4
N_EXPERTS = 32
TOP_K = 4
SWIGLU_ALPHA = 1.702
SWIGLU_LIMIT = 7.0
LAMBDA_INIT = 0.8 - 0.6 * math.exp(-0.3 * 0)

ATTN_BLOCK = 256
NEG_BIG = -1e30
VMEM_LIMIT = 56 * 1024 * 1024


def _cparams(n_axes, vmem=VMEM_LIMIT):
    return pltpu.CompilerParams(
        dimension_semantics=("arbitrary",) * n_axes, vmem_limit_bytes=vmem)


def _rms(xf, g):
    ms = jnp.mean(xf * xf, axis=-1, keepdims=True)
    return xf * lax.rsqrt(ms + EPS) * g


def _dot(a, b):
    return jnp.dot(a, b, preferred_element_type=F32)


def _dot_nt(a, b, precision=None):
    return lax.dot_general(a, b, (((1,), (1,)), ((), ())),
                           preferred_element_type=F32, precision=precision)


def _mixer_in_kernel(x_ref, g_ref, wq_ref, wk_ref, wv_ref, wu_ref, wg_ref, bg_ref,
                     pmix_ref, pscale_ref, wpu_ref,
                     q_ref, k_ref, v_ref, g0_ref, gyb_ref, ext_ref):
    ts = x_ref.shape[1]
    d = x_ref.shape[2]
    j = pl.program_id(1)
    h = _rms(x_ref[0], g_ref[...]).astype(BF16)
    q_ref[0] = (_dot(h, wq_ref[...]) * (HEAD_DIM ** -0.5)).astype(BF16)
    k_ref[0] = _dot(h, wk_ref[...]).astype(BF16)
    v_ref[0] = _dot(h, wv_ref[...]).astype(BF16)
    u = _dot(h, wu_ref[...])

    @pl.when(j == 0)
    def _():
        ext_ref[0:POOL_PAD, :] = jnp.zeros((POOL_PAD, u.shape[1]), F32)

    ext_ref[POOL_PAD:POOL_PAD + ts, :] = u
    e = ext_ref[...]
    sums = {}
    s = e
    w = 1
    while w < max(POOL_WINDOWS):
        s = s + pltpu.roll(s, w, 0)
        w *= 2
        sums[w] = s
    ext_ref[0:POOL_PAD, :] = ext_ref[ts:ts + POOL_PAD, :]

    pos = (j * ts + lax.broadcasted_iota(I32, (ts, 1), 0) + 1).astype(F32)
    mixed = []
    for gi, w in enumerate(POOL_WINDOWS):
        sl = slice(gi * POOL_GROUP, (gi + 1) * POOL_GROUP)
        win = sums[w][POOL_PAD:, sl]
        pooled = win / jnp.minimum(pos, float(w)) - u[:, sl]
        mixed.append(_dot(pooled.astype(BF16), pmix_ref[gi]) * pscale_ref[:, sl])
    mixed = jnp.concatenate(mixed, axis=1).astype(BF16)
    y_b = _dot(mixed, wpu_ref[...])

    gate = jax.nn.sigmoid(_dot(h, wg_ref[...]) + bg_ref[...])
    g0_ref[...] = gate[:, :d].astype(BF16)
    gyb_ref[...] = (gate[:, d:] * y_b).astype(BF16)


def _mixer_in(x, norm_g, w_in, w_gate, b_gate, pool_mix, pool_scale, w_pool_up, ts):
    b, s, d = x.shape
    aw = N_HEADS * HEAD_W
    pw = len(POOL_WINDOWS) * POOL_GROUP
    n = b * s
    nt = s // ts
    const = lambda *shape: pl.BlockSpec(shape, lambda bi, j: (0,) * len(shape))
    tok3 = pl.BlockSpec((1, ts, aw), lambda bi, j: (bi, j, 0))
    tok2 = pl.BlockSpec((ts, d), lambda bi, j: (bi * nt + j, 0))
    return pl.pallas_call(
        _mixer_in_kernel,
        grid=(b, nt),
        in_specs=[
            pl.BlockSpec((1, ts, d), lambda bi, j: (bi, j, 0)),
            const(1, d),
            pl.BlockSpec((d, aw), lambda bi, j: (0, 0)),
            pl.BlockSpec((d, aw), lambda bi, j: (0, 1)),
            pl.BlockSpec((d, aw), lambda bi, j: (0, 2)),
            pl.BlockSpec((d, pw), lambda bi, j: (0, 3 * aw // pw)),
            const(d, 2 * d),
            const(1, 2 * d),
            const(len(POOL_WINDOWS), POOL_GROUP, POOL_GROUP),
            const(1, pw),
            const(pw, d),
        ],
        out_specs=[tok3, tok3, tok3, tok2, tok2],
        out_shape=[jax.ShapeDtypeStruct((b, s, aw), BF16)] * 3
        + [jax.ShapeDtypeStruct((n, d), BF16)] * 2,
        scratch_shapes=[pltpu.VMEM((ts + POOL_PAD, pw), F32)],
        compiler_params=_cparams(2),
        name="mixer_in",
    )(x, norm_g, w_in, w_in, w_in, w_in, w_gate, b_gate, pool_mix, pool_scale, w_pool_up)


def _rel_bucket(rel, log=jnp.log, f32=lambda a: a.astype(jnp.float32),
                i32=lambda a: a.astype(jnp.int32), xp=jnp):
    nb = N_BUCKETS // 2
    ret = i32(rel > 0) * nb
    n = xp.abs(rel)
    max_exact = nb // 2
    nf = f32(xp.maximum(n, 1))
    large = max_exact + i32(log(nf / max_exact) / math.log(MAX_DISTANCE / max_exact)
                            * (nb - max_exact))
    large = xp.minimum(large, nb - 1)
    return ret + xp.where(n < max_exact, n, large)


def _far_bucket(block, seq):
    rel = -np.arange(block + 1, max(seq, block + 2), dtype=np.int32)
    bk = _rel_bucket(rel, log=np.log, f32=lambda a: a.astype(np.float32),
                     i32=lambda a: a.astype(np.int32), xp=np)
    assert (bk == bk[0]).all(), "far keys must share one relative-position bucket"
    return int(bk[0])


def _rel_bias_kernel(far_bucket, tab_ref, bidx_ref, lq1_ref, lk1_ref, lq2_ref, lk2_ref,
                     bias_ref, lam_ref):
    h = pl.program_id(0)
    bidx = bidx_ref[...]
    acc = jnp.zeros(bidx.shape, F32)
    for bkt in range(N_BUCKETS):
        acc = jnp.where(bidx == bkt, tab_ref[bkt, h], acc)
    acc = acc - tab_ref[far_bucket, h]
    bias_ref[0] = jnp.where(bidx < 0, -jnp.inf, acc)
    lam = (jnp.exp(jnp.sum(lq1_ref[...] * lk1_ref[...], keepdims=True))
           - jnp.exp(jnp.sum(lq2_ref[...] * lk2_ref[...], keepdims=True)) + LAMBDA_INIT)
    lam_ref[...] = jnp.broadcast_to(lam, lam_ref.shape)


def _rel_bias(rel_bias, lq1, lk1, lq2, lk2, seq):
    blk = ATTN_BLOCK
    qpos = jnp.arange(blk, dtype=I32)[:, None]
    kpos = jnp.arange(blk, dtype=I32)[None, :]
    diag = jnp.where(kpos // CHUNK <= qpos // CHUNK, _rel_bucket(kpos - qpos), -1)
    prev = _rel_bucket(kpos - (qpos + blk))
    bidx = jnp.stack([diag, prev]).astype(I32)
    vec = pl.BlockSpec((1, HEAD_DIM), lambda h: (0, 0))
    return pl.pallas_call(
        functools.partial(_rel_bias_kernel, _far_bucket(blk, seq)),
        grid=(N_HEADS,),
        in_specs=[
            pl.BlockSpec(memory_space=pltpu.SMEM),
            pl.BlockSpec((2, blk, blk), lambda h: (0, 0, 0)),
            vec, vec, vec, vec,
        ],
        out_specs=[
            pl.BlockSpec((1, 2, blk, blk), lambda h: (h, 0, 0, 0)),
            pl.BlockSpec((8, 128), lambda h: (0, 0)),
        ],
        out_shape=[jax.ShapeDtypeStruct((N_HEADS, 2, blk, blk), F32),
                   jax.ShapeDtypeStruct((8, 128), F32)],
        compiler_params=_cparams(1),
        name="rel_bias",
    )(rel_bias, bidx, lq1, lk1, lq2, lk2)


def _diff_attn_kernel(q_ref, k_ref, v_ref, bias_ref, lam_ref, sg_ref, o_ref, vext_ref):
    s_len = q_ref.shape[1]
    tq = ATTN_BLOCK
    vext_ref[:, :HEAD_W] = v_ref[0]
    vext_ref[:, HEAD_W:] = jnp.ones((s_len, HEAD_W), BF16)
    lam = lam_ref[0:1, 0:1]
    lane = lax.broadcasted_iota(I32, (tq, HEAD_W), 1)
    b_diag = bias_ref[0, 0]
    b_prev = bias_ref[0, 1]
    b_diag = jnp.concatenate([b_diag, b_diag], axis=0)
    b_prev = jnp.concatenate([b_prev, b_prev], axis=0)

    for qi in range(s_len // tq):
        n_keys = (qi + 1) * tq
        q = q_ref[0, qi * tq:(qi + 1) * tq, :]
        zero = jnp.zeros_like(q)
        qs = jnp.concatenate([jnp.where(lane < HEAD_DIM, q, zero),
                              jnp.where(lane >= HEAD_DIM, q, zero)], axis=0)
        s = _dot_nt(qs, k_ref[0, :n_keys, :])
        pieces = []
        if qi >= 2:
            pieces.append(s[:, :n_keys - 2 * tq])
        if qi >= 1:
            pieces.append(s[:, n_keys - 2 * tq:n_keys - tq] + b_prev)
        pieces.append(s[:, n_keys - tq:] + b_diag)
        s = jnp.concatenate(pieces, axis=1) if len(pieces) > 1 else pieces[0]
        m = jnp.max(s, axis=-1, keepdims=True)
        p = jnp.exp(s - m).astype(BF16)
        acc = _dot(p, vext_ref[:n_keys, :])
        o = acc[:, :HEAD_W] / acc[:, HEAD_W:HEAD_W + 1]
        a = o[:tq] - lam * o[tq:]
        o_ref[0, qi * tq:(qi + 1) * tq, :] = (
            _rms(a, sg_ref[...]) * (1.0 - LAMBDA_INIT)).astype(BF16)


def _diff_attn(q, k, v, bias, lam, subln_g):
    b, s, aw = q.shape
    tq = ATTN_BLOCK
    spec = pl.BlockSpec((1, s, HEAD_W), lambda bi, h: (bi, 0, h))
    return pl.pallas_call(
        _diff_attn_kernel,
        grid=(b, N_HEADS),
        in_specs=[
            spec, spec, spec,
            pl.BlockSpec((1, 2, tq, tq), lambda bi, h: (h, 0, 0, 0)),
            pl.BlockSpec((8, 128), lambda bi, h: (0, 0)),
            pl.BlockSpec((1, HEAD_W), lambda bi, h: (0, 0)),
        ],
        out_specs=spec,
        out_shape=jax.ShapeDtypeStruct((b, s, aw), BF16),
        scratch_shapes=[pltpu.VMEM((s, 2 * HEAD_W), BF16)],
        compiler_params=_cparams(2),
        name="diff_attn",
    )(q, k, v, bias, lam, subln_g)


def _mem_kv_kernel(m_ref, g_ref, w_ref, k_ref, v_ref):
    d = m_ref.shape[2]
    m = _rms(m_ref[0], g_ref[...]).astype(BF16)
    kv = _dot(m, w_ref[...])
    k_ref[0] = kv[:, :d].astype(BF16)
    v_ref[0] = kv[:, d:].astype(BF16)


def _mem_kv(mem, norm_g, w_xkv):
    b, ml, d = mem.shape
    blk = pl.BlockSpec((1, ml, d), lambda bi: (bi, 0, 0))
    return pl.pallas_call(
        _mem_kv_kernel,
        grid=(b,),
        in_specs=[blk, pl.BlockSpec((1, d), lambda bi: (0, 0)),
                  pl.BlockSpec((d, 2 * d), lambda bi: (0, 0))],
        out_specs=[blk, blk],
        out_shape=[jax.ShapeDtypeStruct((b, ml, d), BF16)] * 2,
        compiler_params=_cparams(1),
        name="mem_kv",
    )(mem, norm_g, w_xkv)


def _post_kernel(x_ref, a_ref, g0_ref, gyb_ref, wau_ref, wo_ref, nxg_ref, wxq_ref,
                 km_ref, vm_ref, wxo_ref, nfg_ref, wrt_ref, br_ref,
                 x2_ref, hf_ref, eidx_ref, rank_ref, gate_ref, cnt_ref, run_ref):
    ts, d = x_ref.shape
    first = (pl.program_id(0) == 0) & (pl.program_id(1) == 0)

    y_a = _dot(a_ref[...], wau_ref[...])
    merged = g0_ref[...].astype(F32) * y_a + gyb_ref[...].astype(F32)
    x1 = x_ref[...] + _dot(merged.astype(BF16), wo_ref[...])

    hd = d // X_HEADS
    hx = _rms(x1, nxg_ref[...]).astype(BF16)
    qx = (_dot(hx, wxq_ref[...]) * (hd ** -0.5)).astype(BF16)
    heads = []
    for hh in range(X_HEADS):
        sl = slice(hh * hd, (hh + 1) * hd)
        s = _dot_nt(qx[:, sl], km_ref[0, :, sl])
        p = jnp.exp(s - jnp.max(s, axis=-1, keepdims=True))
        p = p / jnp.sum(p, axis=-1, keepdims=True)
        heads.append(_dot(p.astype(BF16), vm_ref[0, :, sl]))
    o = jnp.concatenate(heads, axis=1).astype(BF16)
    x2 = x1 + _dot(o, wxo_ref[...])
    x2_ref[...] = x2
    hf = _rms(x2, nfg_ref[...])
    hf_ref[...] = hf

    logits = _dot_nt(wrt_ref[...], hf, precision=lax.Precision.HIGHEST) + br_ref[...]
    ne = logits.shape[0]
    eid = lax.broadcasted_iota(I32, logits.shape, 0).astype(F32)
    work = logits
    vals, idxs, hots = [], [], []
    for _ in range(TOP_K):
        mx = jnp.max(work, axis=0, keepdims=True)
        idx = jnp.min(jnp.where(work == mx, eid, float(ne)), axis=0, keepdims=True)
        hot = eid == idx
        vals.append(mx)
        idxs.append(idx.astype(I32))
        hots.append(hot)
        work = jnp.where(hot, -jnp.inf, work)
    ex = [jnp.exp(vv - vals[0]) for vv in vals]
    den = ex[0] + ex[1] + ex[2] + ex[3]
    gates = [e_ / den for e_ in ex]

    multi = (hots[0] | hots[1] | hots[2] | hots[3])
    multi_f = jnp.where(multi, 1.0, 0.0).astype(F32)
    tri = (lax.broadcasted_iota(I32, (ts, ts), 0)
           < lax.broadcasted_iota(I32, (ts, ts), 1))
    before = _dot(multi_f.astype(BF16), jnp.where(tri, 1.0, 0.0).astype(BF16))

    @pl.when(first)
    def _():
        run_ref[...] = jnp.zeros(run_ref.shape, F32)

    run = run_ref[...]
    pos = before + run
    ranks = [jnp.sum(jnp.where(hot, pos, 0.0), axis=0, keepdims=True) for hot in hots]
    run_new = run + jnp.sum(multi_f, axis=1, keepdims=True)
    run_ref[...] = run_new
    cnt_ref[...] = jnp.broadcast_to(run_new, cnt_ref.shape).astype(I32)

    eidx_ref[...] = jnp.concatenate(idxs, axis=0)
    rank_ref[...] = jnp.concatenate(ranks, axis=0).astype(I32)
    g_rows = jnp.concatenate(gates + [jnp.zeros((128 - TOP_K, ts), F32)], axis=0)
    gate_ref[...] = g_rows.T


def _post(x2d, a2d, g0, gyb, w_attn_up, w_out, norm_x_g, w_xq, k_mem, v_mem, w_xo,
          norm_ffn_g, w_router_t, b_router, batch, ts):
    n, d = x2d.shape
    nt = n // batch // ts
    ml = k_mem.shape[1]
    tok = pl.BlockSpec((ts, d), lambda bi, j: (bi * nt + j, 0))
    lanes = pl.BlockSpec((TOP_K, ts), lambda bi, j: (0, bi * nt + j))
    const = lambda *shape: pl.BlockSpec(shape, lambda bi, j: (0,) * len(shape))
    mem = pl.BlockSpec((1, ml, d), lambda bi, j: (bi, 0, 0))
    return pl.pallas_call(
        _post_kernel,
        grid=(batch, nt),
        in_specs=[tok, tok, tok, tok, const(d, d), const(d, d), const(1, d), const(d, d),
                  mem, mem, const(d, d), const(1, d), const(N_EXPERTS, d),
                  const(N_EXPERTS, 1)],
        out_specs=[tok, tok, lanes, lanes,
                   pl.BlockSpec((ts, 128), lambda bi, j: (bi * nt + j, 0)),
                   const(N_EXPERTS, 128)],
        out_shape=[jax.ShapeDtypeStruct((n, d), F32), jax.ShapeDtypeStruct((n, d), F32),
                   jax.ShapeDtypeStruct((TOP_K, n), I32), jax.ShapeDtypeStruct((TOP_K, n), I32),
                   jax.ShapeDtypeStruct((n, 128), F32),
                   jax.ShapeDtypeStruct((N_EXPERTS, 128), I32)],
        scratch_shapes=[pltpu.VMEM((N_EXPERTS, 1), F32)],
        compiler_params=_cparams(2),
        name="post",
    )(x2d, a2d, g0, gyb, w_attn_up, w_out, norm_x_g, w_xq, k_mem, v_mem, w_xo,
      norm_ffn_g, w_router_t, b_router)


def _route_dest_kernel(pstart_ref, eidx_ref, rank_ref, dest_ref):
    eidx = eidx_ref[...]
    start = jnp.zeros(eidx.shape, I32)
    for e in range(N_EXPERTS):
        start = jnp.where(eidx == e, pstart_ref[e], start)
    dest_ref[...] = start + rank_ref[...]


def _route_dest(pstart, eidx, rank, tl):
    k, n = eidx.shape
    lanes = pl.BlockSpec((k, tl), lambda i: (0, i))
    return pl.pallas_call(
        _route_dest_kernel,
        grid=(n // tl,),
        in_specs=[pl.BlockSpec(memory_space=pltpu.SMEM), lanes, lanes],
        out_specs=lanes,
        out_shape=jax.ShapeDtypeStruct((k, n), I32),
        compiler_params=_cparams(1),
        name="route_dest",
    )(pstart, eidx, rank)


SC_CORES = 2
SC_SUBCORES = 16
SC_WORKERS = SC_CORES * SC_SUBCORES
MOVE_CHUNK = 32


def _worker_index_layout(dest, n):
    per_w = n // SC_WORKERS
    n_chunks = per_w // MOVE_CHUNK
    d4 = dest.reshape(TOP_K, SC_WORKERS, n_chunks, MOVE_CHUNK)
    return jnp.transpose(d4, (1, 2, 0, 3)).reshape(SC_WORKERS, n_chunks * TOP_K, MOVE_CHUNK)


def _sc_mesh():
    return plsc.VectorSubcoreMesh(core_axis_name="c", subcore_axis_name="s",
                                  num_cores=SC_CORES, num_subcores=SC_SUBCORES)


def _sc_dispatch(hf, dest_w, rows):
    n, d = hf.shape
    per_w = n // SC_WORKERS
    n_chunks = per_w // MOVE_CHUNK

    def body(hf_hbm, dest_hbm, xs_hbm, idx_v, rows_v, sem):
        wid = lax.axis_index("s") * SC_CORES + lax.axis_index("c")
        base = wid * per_w
        pltpu.sync_copy(dest_hbm.at[wid], idx_v)
        for c in range(n_chunks):
            pltpu.sync_copy(hf_hbm.at[pl.ds(base + c * MOVE_CHUNK, MOVE_CHUNK)], rows_v)
            copies = [pltpu.async_copy(rows_v, xs_hbm.at[idx_v.at[c * TOP_K + kk]], sem)
                      for kk in range(TOP_K)]
            for cp in copies:
                cp.wait()

    return pl.kernel(
        body,
        out_type=jax.ShapeDtypeStruct((rows, d), hf.dtype),
        mesh=_sc_mesh(),
        scratch_types=[pltpu.VMEM((n_chunks * TOP_K, MOVE_CHUNK), I32),
                       pltpu.VMEM((MOVE_CHUNK, d), hf.dtype),
                       pltpu.SemaphoreType.DMA],
        name="sc_dispatch",
    )(hf, dest_w)


def _sc_gather(ys, dest_w, n):
    _, d = ys.shape
    per_w = n // SC_WORKERS
    n_chunks = per_w // MOVE_CHUNK

    def body(ys_hbm, dest_hbm, yg_hbm, idx_v, rows_v, sem):
        wid = lax.axis_index("s") * SC_CORES + lax.axis_index("c")
        base = wid * per_w
        pltpu.sync_copy(dest_hbm.at[wid], idx_v)
        for c in range(n_chunks):
            for kk in range(TOP_K):
                pltpu.async_copy(ys_hbm.at[idx_v.at[c * TOP_K + kk]], rows_v, sem).wait()
                pltpu.sync_copy(
                    rows_v, yg_hbm.at[pl.ds(kk * n + base + c * MOVE_CHUNK, MOVE_CHUNK)])

    return pl.kernel(
        body,
        out_type=jax.ShapeDtypeStruct((TOP_K * n, d), ys.dtype),
        mesh=_sc_mesh(),
        scratch_types=[pltpu.VMEM((n_chunks * TOP_K, MOVE_CHUNK), I32),
                       pltpu.VMEM((MOVE_CHUNK, d), ys.dtype),
                       pltpu.SemaphoreType.DMA],
        name="sc_gather",
    )(ys, dest_w)


def _combine_kernel(x2_ref, gate_ref, fg_ref, yg_ref, o_ref):
    g = gate_ref[...]
    acc = x2_ref[...]
    for kk in range(TOP_K):
        acc = acc + g[:, kk:kk + 1] * yg_ref[kk]
    o_ref[...] = _rms(acc, fg_ref[...])


def _combine(x2, gate_tm, final_g, yg, ts):
    n, d = x2.shape
    return pl.pallas_call(
        _combine_kernel,
        grid=(n // ts,),
        in_specs=[pl.BlockSpec((ts, d), lambda i: (i, 0)),
                  pl.BlockSpec((ts, 128), lambda i: (i, 0)),
                  pl.BlockSpec((1, d), lambda i: (0, 0)),
                  pl.BlockSpec((TOP_K, ts, d), lambda i: (0, i, 0))],
        out_specs=pl.BlockSpec((ts, d), lambda i: (i, 0)),
        out_shape=jax.ShapeDtypeStruct((n, d), F32),
        compiler_params=_cparams(1),
        name="combine",
    )(x2, gate_tm, final_g, yg.reshape(TOP_K, n, d))


def _experts_kernel(be_ref, first_ref, nxt_ref, slot_ref, nb_ref,
                    xs_ref, bgu_ref, bd_ref, wgu_hbm, wd_hbm, ys_ref,
                    wgu_st, wd_st, wgu_bf, wd_bf, sem):
    i = pl.program_id(0)
    dff = wd_bf.shape[0]
    e = be_ref[i]

    def fetch(ex, sl):
        return (pltpu.make_async_copy(wgu_hbm.at[ex], wgu_st.at[sl], sem.at[0, sl]),
                pltpu.make_async_copy(wd_hbm.at[ex], wd_st.at[sl], sem.at[1, sl]))

    @pl.when(i == 0)
    def _():
        for cp in fetch(e, slot_ref[e]):
            cp.start()

    @pl.when(first_ref[i] == 1)
    def _():
        sl = slot_ref[e]
        for cp in fetch(e, sl):
            cp.wait()
        nx = nxt_ref[e]

        @pl.when(nx >= 0)
        def _():
            for cp in fetch(nx, 1 - sl):
                cp.start()

        wgu_bf[...] = wgu_st[sl].astype(BF16)
        wd_bf[...] = wd_st[sl].astype(BF16)

    @pl.when(i < nb_ref[0])
    def _():
        xb = xs_ref[...].astype(BF16)
        gu = _dot(xb, wgu_bf[...]) + bgu_ref[0]
        x_glu = jnp.minimum(gu[:, :dff], SWIGLU_LIMIT)
        x_lin = jnp.clip(gu[:, dff:], -SWIGLU_LIMIT, SWIGLU_LIMIT)
        act = x_glu * jax.nn.sigmoid(SWIGLU_ALPHA * x_glu) * (x_lin + 1.0)
        ys_ref[...] = _dot(act.astype(BF16), wd_bf[...]) + bd_ref[0]

    @pl.when(i >= nb_ref[0])
    def _():
        ys_ref[...] = jnp.zeros(ys_ref.shape, F32)


def _expert_schedule(counts, bm, n_blocks):
    ne = counts.shape[0]
    padded = (counts + bm - 1) // bm * bm
    pend = jnp.cumsum(padded)
    pstart = (pend - padded).astype(I32)
    nb_used = (pend[-1] // bm).astype(I32)
    blk = jnp.arange(n_blocks, dtype=I32)
    blk_c = jnp.minimum(blk, nb_used - 1)
    be = jnp.minimum(jnp.sum((pend[None, :] <= (blk_c * bm)[:, None]).astype(I32), axis=1),
                     ne - 1)
    first = ((blk < nb_used) & (blk * bm == pstart[be])).astype(I32)
    used = counts > 0
    seq = jnp.cumsum(used.astype(I32)) - 1
    ids = jnp.arange(ne, dtype=I32)
    later = used[None, :] & (ids[None, :] > ids[:, None])
    nxt = jnp.where(jnp.any(later, axis=1), jnp.argmax(later, axis=1), -1).astype(I32)
    slot = (seq & 1).astype(I32)
    return pstart, nb_used.reshape(1), be, first, nxt, slot


def _experts(be, first, nxt, slot, nb_used, xs, w_gu, b_gu, w_down, b_down, bm):
    rows, d = xs.shape
    ne, _, dff2 = w_gu.shape
    dff = dff2 // 2
    last = lambda i, be, fi, nx, sl, nb: (jnp.maximum(jnp.minimum(i, nb[0] - 1), 0), 0)
    grid_spec = pltpu.PrefetchScalarGridSpec(
        num_scalar_prefetch=5,
        grid=(rows // bm,),
        in_specs=[
            pl.BlockSpec((bm, d), last),
            pl.BlockSpec((1, 1, dff2), lambda i, be, fi, nx, sl, nb: (be[i], 0, 0)),
            pl.BlockSpec((1, 1, d), lambda i, be, fi, nx, sl, nb: (be[i], 0, 0)),
            pl.BlockSpec(memory_space=pl.ANY),
            pl.BlockSpec(memory_space=pl.ANY),
        ],
        out_specs=pl.BlockSpec((bm, d), lambda i, be, fi, nx, sl, nb: (i, 0)),
        scratch_shapes=[pltpu.VMEM((2, d, dff2), F32), pltpu.VMEM((2, dff, d), F32),
                        pltpu.VMEM((d, dff2), BF16), pltpu.VMEM((dff, d), BF16),
                        pltpu.SemaphoreType.DMA((2, 2))],
    )
    return pl.pallas_call(
        _experts_kernel,
        grid_spec=grid_spec,
        out_shape=jax.ShapeDtypeStruct((rows, d), F32),
        compiler_params=_cparams(1),
        name="experts",
    )(be, first, nxt, slot, nb_used, xs, b_gu.reshape(ne, 1, dff2),
      b_down.reshape(ne, 1, d), w_gu, w_down)


MIXER_TILE = 512
POST_TILE = 256
DEST_TILE = 2048
COMBINE_TILE = 256
EXPERT_BLOCK = 256


def kernel(x, mem, norm_mix_g, w_in, lambda_q1, lambda_k1, lambda_q2, lambda_k2, rel_bias,
           subln_g, w_attn_up, pool_mix, pool_scale, w_pool_up, w_gate, b_gate, w_out,
           norm_x_g, norm_mem_g, w_xq, w_xkv, w_xo, norm_ffn_g, w_router, b_router,
           w_gu, b_gu, w_down, b_down, final_norm_g):
    b, s, d = x.shape
    n = b * s
    assert w_in.shape[0] == 1, "single-layer block"
    row = lambda a: a.reshape(1, -1)
    bf = lambda a: a[0].astype(BF16)

    q, k, v, g0, gyb = _mixer_in(
        x, row(norm_mix_g[0]), bf(w_in), bf(w_gate), row(b_gate[0]), bf(pool_mix),
        row(pool_scale[0]), bf(w_pool_up), MIXER_TILE)

    bias, lam = _rel_bias(rel_bias, row(lambda_q1[0]), row(lambda_k1[0]),
                          row(lambda_q2[0]), row(lambda_k2[0]), s)
    a = _diff_attn(q, k, v, bias, lam, row(subln_g[0]))

    k_mem, v_mem = _mem_kv(mem, row(norm_mem_g[0]), bf(w_xkv))

    x2, hf, eidx, rank, gate_tm, counts = _post(
        x.reshape(n, d), a.reshape(n, d), g0, gyb, bf(w_attn_up), bf(w_out),
        row(norm_x_g[0]), bf(w_xq), k_mem, v_mem, bf(w_xo), row(norm_ffn_g[0]),
        w_router[0].T, b_router[0].reshape(-1, 1), b, POST_TILE)

    bm = EXPERT_BLOCK
    rows = n * TOP_K + N_EXPERTS * bm
    pstart, nb_used, be, first, nxt, slot = _expert_schedule(counts[:, 0], bm, rows // bm)

    dest_w = _worker_index_layout(_route_dest(pstart, eidx, rank, DEST_TILE), n)
    xs = _sc_dispatch(hf, dest_w, rows)
    ys = _experts(be, first, nxt, slot, nb_used, xs, w_gu[0], b_gu[0], w_down[0], b_down[0],
                  bm)
    yg = _sc_gather(ys, dest_w, n)
    out = _combine(x2, gate_tm, row(final_norm_g), yg, COMBINE_TILE)
    return out.reshape(b, s, d)
```

```python
import functools
import math

import numpy as np
import jax
import jax.numpy as jnp
from jax import lax
from jax.experimental import pallas as pl
from jax.experimental.pallas import tpu as pltpu
from jax.experimental.pallas import tpu_sc as plsc

F32 = jnp.float32
BF16 = jnp.bfloat16
I32 = jnp.int32

EPS = 1e-6
CHUNK = 64
N_HEADS = 8
HEAD_DIM = 64
HEAD_W = 2 * HEAD_DIM
POOL_WINDOWS = (2, 4, 8, 16)
POOL_GROUP = 128
POOL_PAD = 16
N_BUCKETS = 32
MAX_DISTANCE = 128
X_HEADS = 4
N_EXPERTS = 32
TOP_K = 4
SWIGLU_ALPHA = 1.702
SWIGLU_LIMIT = 7.0
LAMBDA_INIT = 0.8 - 0.6 * math.exp(-0.3 * 0)

ATTN_BLOCK = 256
NEG_BIG = -1e30
VMEM_LIMIT = 56 * 1024 * 1024


def _cparams(n_axes, vmem=VMEM_LIMIT):
    return pltpu.CompilerParams(
        dimension_semantics=("arbitrary",) * n_axes, vmem_limit_bytes=vmem)


def _rms(xf, g):
    ms = jnp.mean(xf * xf, axis=-1, keepdims=True)
    return xf * lax.rsqrt(ms + EPS) * g


def _dot(a, b):
    return jnp.dot(a, b, preferred_element_type=F32)


def _dot_nt(a, b, precision=None):
    return lax.dot_general(a, b, (((1,), (1,)), ((), ())),
                           preferred_element_type=F32, precision=precision)


def _mixer_in_kernel(x_ref, g_ref, wq_ref, wk_ref, wv_ref, wu_ref, wg_ref, bg_ref,
                     pmix_ref, pscale_ref, wpu_ref,
                     q_ref, k_ref, v_ref, g0_ref, gyb_ref, ext_ref):
    ts = x_ref.shape[1]
    d = x_ref.shape[2]
    j = pl.program_id(1)
    h = _rms(x_ref[0], g_ref[...]).astype(BF16)
    q_ref[0] = (_dot(h, wq_ref[...]) * (HEAD_DIM ** -0.5)).astype(BF16)
    k_ref[0] = _dot(h, wk_ref[...]).astype(BF16)
    v_ref[0] = _dot(h, wv_ref[...]).astype(BF16)
    u = _dot(h, wu_ref[...])

    @pl.when(j == 0)
    def _():
        ext_ref[0:POOL_PAD, :] = jnp.zeros((POOL_PAD, u.shape[1]), F32)

    ext_ref[POOL_PAD:POOL_PAD + ts, :] = u
    e = ext_ref[...]
    sums = {}
    s = e
    w = 1
    while w < max(POOL_WINDOWS):
        s = s + pltpu.roll(s, w, 0)
        w *= 2
        sums[w] = s
    ext_ref[0:POOL_PAD, :] = ext_ref[ts:ts + POOL_PAD, :]

    pos = (j * ts + lax.broadcasted_iota(I32, (ts, 1), 0) + 1).astype(F32)
    mixed = []
    for gi, w in enumerate(POOL_WINDOWS):
        sl = slice(gi * POOL_GROUP, (gi + 1) * POOL_GROUP)
        win = sums[w][POOL_PAD:, sl]
        pooled = win / jnp.minimum(pos, float(w)) - u[:, sl]
        mixed.append(_dot(pooled.astype(BF16), pmix_ref[gi]) * pscale_ref[:, sl])
    mixed = jnp.concatenate(mixed, axis=1).astype(BF16)
    y_b = _dot(mixed, wpu_ref[...])

    gate = jax.nn.sigmoid(_dot(h, wg_ref[...]) + bg_ref[...])
    g0_ref[...] = gate[:, :d].astype(BF16)
    gyb_ref[...] = (gate[:, d:] * y_b).astype(BF16)


def _mixer_in(x, norm_g, w_in, w_gate, b_gate, pool_mix, pool_scale, w_pool_up, ts):
    b, s, d = x.shape
    aw = N_HEADS * HEAD_W
    pw = len(POOL_WINDOWS) * POOL_GROUP
    n = b * s
    nt = s // ts
    const = lambda *shape: pl.BlockSpec(shape, lambda bi, j: (0,) * len(shape))
    tok3 = pl.BlockSpec((1, ts, aw), lambda bi, j: (bi, j, 0))
    tok2 = pl.BlockSpec((ts, d), lambda bi, j: (bi * nt + j, 0))
    return pl.pallas_call(
        _mixer_in_kernel,
        grid=(b, nt),
        in_specs=[
            pl.BlockSpec((1, ts, d), lambda bi, j: (bi, j, 0)),
            const(1, d),
            pl.BlockSpec((d, aw), lambda bi, j: (0, 0)),
            pl.BlockSpec((d, aw), lambda bi, j: (0, 1)),
            pl.BlockSpec((d, aw), lambda bi, j: (0, 2)),
            pl.BlockSpec((d, pw), lambda bi, j: (0, 3 * aw // pw)),
            const(d, 2 * d),
            const(1, 2 * d),
            const(len(POOL_WINDOWS), POOL_GROUP, POOL_GROUP),
            const(1, pw),
            const(pw, d),
        ],
        out_specs=[tok3, tok3, tok3, tok2, tok2],
        out_shape=[jax.ShapeDtypeStruct((b, s, aw), BF16)] * 3
        + [jax.ShapeDtypeStruct((n, d), BF16)] * 2,
        scratch_shapes=[pltpu.VMEM((ts + POOL_PAD, pw), F32)],
        compiler_params=_cparams(2),
        name="mixer_in",
    )(x, norm_g, w_in, w_in, w_in, w_in, w_gate, b_gate, pool_mix, pool_scale, w_pool_up)


def _rel_bucket(rel, log=jnp.log, f32=lambda a: a.astype(jnp.float32),
                i32=lambda a: a.astype(jnp.int32), xp=jnp):
    nb = N_BUCKETS // 2
    ret = i32(rel > 0) * nb
    n = xp.abs(rel)
    max_exact = nb // 2
    nf = f32(xp.maximum(n, 1))
    large = max_exact + i32(log(nf / max_exact) / math.log(MAX_DISTANCE / max_exact)
                            * (nb - max_exact))
    large = xp.minimum(large, nb - 1)
    return ret + xp.where(n < max_exact, n, large)


def _far_bucket(block, seq):
    rel = -np.arange(block + 1, max(seq, block + 2), dtype=np.int32)
    bk = _rel_bucket(rel, log=np.log, f32=lambda a: a.astype(np.float32),
                     i32=lambda a: a.astype(np.int32), xp=np)
    assert (bk == bk[0]).all(), "far keys must share one relative-position bucket"
    return int(bk[0])


def _rel_bias_kernel(far_bucket, tab_ref, bidx_ref, lq1_ref, lk1_ref, lq2_ref, lk2_ref,
                     bias_ref, lam_ref):
    h = pl.program_id(0)
    bidx = bidx_ref[...]
    acc = jnp.zeros(bidx.shape, F32)
    for bkt in range(N_BUCKETS):
        acc = jnp.where(bidx == bkt, tab_ref[bkt, h], acc)
    acc = acc - tab_ref[far_bucket, h]
    bias_ref[0] = jnp.where(bidx < 0, -jnp.inf, acc)
    lam = (jnp.exp(jnp.sum(lq1_ref[...] * lk1_ref[...], keepdims=True))
           - jnp.exp(jnp.sum(lq2_ref[...] * lk2_ref[...], keepdims=True)) + LAMBDA_INIT)
    lam_ref[...] = jnp.broadcast_to(lam, lam_ref.shape)


def _rel_bias(rel_bias, lq1, lk1, lq2, lk2, seq):
    blk = ATTN_BLOCK
    qpos = jnp.arange(blk, dtype=I32)[:, None]
    kpos = jnp.arange(blk, dtype=I32)[None, :]
    diag = jnp.where(kpos // CHUNK <= qpos // CHUNK, _rel_bucket(kpos - qpos), -1)
    prev = _rel_bucket(kpos - (qpos + blk))
    bidx = jnp.stack([diag, prev]).astype(I32)
    vec = pl.BlockSpec((1, HEAD_DIM), lambda h: (0, 0))
    return pl.pallas_call(
        functools.partial(_rel_bias_kernel, _far_bucket(blk, seq)),
        grid=(N_HEADS,),
        in_specs=[
            pl.BlockSpec(memory_space=pltpu.SMEM),
            pl.BlockSpec((2, blk, blk), lambda h: (0, 0, 0)),
            vec, vec, vec, vec,
        ],
        out_specs=[
            pl.BlockSpec((1, 2, blk, blk), lambda h: (h, 0, 0, 0)),
            pl.BlockSpec((8, 128), lambda h: (0, 0)),
        ],
        out_shape=[jax.ShapeDtypeStruct((N_HEADS, 2, blk, blk), F32),
                   jax.ShapeDtypeStruct((8, 128), F32)],
        compiler_params=_cparams(1),
        name="rel_bias",
    )(rel_bias, bidx, lq1, lk1, lq2, lk2)


def _diff_attn_kernel(q_ref, k_ref, v_ref, bias_ref, lam_ref, sg_ref, o_ref, vext_ref):
    s_len = q_ref.shape[1]
    tq = ATTN_BLOCK
    vext_ref[:, :HEAD_W] = v_ref[0]
    vext_ref[:, HEAD_W:] = jnp.ones((s_len, HEAD_W), BF16)
    lam = lam_ref[0:1, 0:1]
    lane = lax.broadcasted_iota(I32, (tq, HEAD_W), 1)
    b_diag = bias_ref[0, 0]
    b_prev = bias_ref[0, 1]
    b_diag = jnp.concatenate([b_diag, b_diag], axis=0)
    b_prev = jnp.concatenate([b_prev, b_prev], axis=0)

    for qi in range(s_len // tq):
        n_keys = (qi + 1) * tq
        q = q_ref[0, qi * tq:(qi + 1) * tq, :]
        zero = jnp.zeros_like(q)
        qs = jnp.concatenate([jnp.where(lane < HEAD_DIM, q, zero),
                              jnp.where(lane >= HEAD_DIM, q, zero)], axis=0)
        s = _dot_nt(qs, k_ref[0, :n_keys, :])
        pieces = []
        if qi >= 2:
            pieces.append(s[:, :n_keys - 2 * tq])
        if qi >= 1:
            pieces.append(s[:, n_keys - 2 * tq:n_keys - tq] + b_prev)
        pieces.append(s[:, n_keys - tq:] + b_diag)
        s = jnp.concatenate(pieces, axis=1) if len(pieces) > 1 else pieces[0]
        m = jnp.max(s, axis=-1, keepdims=True)
        p = jnp.exp(s - m).astype(BF16)
        acc = _dot(p, vext_ref[:n_keys, :])
        o = acc[:, :HEAD_W] / acc[:, HEAD_W:HEAD_W + 1]
        a = o[:tq] - lam * o[tq:]
        o_ref[0, qi * tq:(qi + 1) * tq, :] = (
            _rms(a, sg_ref[...]) * (1.0 - LAMBDA_INIT)).astype(BF16)


def _diff_attn(q, k, v, bias, lam, subln_g):
    b, s, aw = q.shape
    tq = ATTN_BLOCK
    spec = pl.BlockSpec((1, s, HEAD_W), lambda bi, h: (bi, 0, h))
    return pl.pallas_call(
        _diff_attn_kernel,
        grid=(b, N_HEADS),
        in_specs=[
            spec, spec, spec,
            pl.BlockSpec((1, 2, tq, tq), lambda bi, h: (h, 0, 0, 0)),
            pl.BlockSpec((8, 128), lambda bi, h: (0, 0)),
            pl.BlockSpec((1, HEAD_W), lambda bi, h: (0, 0)),
        ],
        out_specs=spec,
        out_shape=jax.ShapeDtypeStruct((b, s, aw), BF16),
        scratch_shapes=[pltpu.VMEM((s, 2 * HEAD_W), BF16)],
        compiler_params=_cparams(2),
        name="diff_attn",
    )(q, k, v, bias, lam, subln_g)


def _mem_kv_kernel(m_ref, g_ref, w_ref, k_ref, v_ref):
    d = m_ref.shape[2]
    m = _rms(m_ref[0], g_ref[...]).astype(BF16)
    kv = _dot(m, w_ref[...])
    k_ref[0] = kv[:, :d].astype(BF16)
    v_ref[0] = kv[:, d:].astype(BF16)


def _mem_kv(mem, norm_g, w_xkv):
    b, ml, d = mem.shape
    blk = pl.BlockSpec((1, ml, d), lambda bi: (bi, 0, 0))
    return pl.pallas_call(
        _mem_kv_kernel,
        grid=(b,),
        in_specs=[blk, pl.BlockSpec((1, d), lambda bi: (0, 0)),
                  pl.BlockSpec((d, 2 * d), lambda bi: (0, 0))],
        out_specs=[blk, blk],
        out_shape=[jax.ShapeDtypeStruct((b, ml, d), BF16)] * 2,
        compiler_params=_cparams(1),
        name="mem_kv",
    )(mem, norm_g, w_xkv)


def _post_kernel(x_ref, a_ref, g0_ref, gyb_ref, wau_ref, wo_ref, nxg_ref, wxq_ref,
                 km_ref, vm_ref, wxo_ref, nfg_ref, wrt_ref, br_ref,
                 x2_ref, hf_ref, eidx_ref, rank_ref, gate_ref, cnt_ref, run_ref):
    ts, d = x_ref.shape
    first = (pl.program_id(0) == 0) & (pl.program_id(1) == 0)

    y_a = _dot(a_ref[...], wau_ref[...])
    merged = g0_ref[...].astype(F32) * y_a + gyb_ref[...].astype(F32)
    x1 = x_ref[...] + _dot(merged.astype(BF16), wo_ref[...])

    hd = d // X_HEADS
    hx = _rms(x1, nxg_ref[...]).astype(BF16)
    qx = (_dot(hx, wxq_ref[...]) * (hd ** -0.5)).astype(BF16)
    heads = []
    for hh in range(X_HEADS):
        sl = slice(hh * hd, (hh + 1) * hd)
        s = _dot_nt(qx[:, sl], km_ref[0, :, sl])
        p = jnp.exp(s - jnp.max(s, axis=-1, keepdims=True))
        p = p / jnp.sum(p, axis=-1, keepdims=True)
        heads.append(_dot(p.astype(BF16), vm_ref[0, :, sl]))
    o = jnp.concatenate(heads, axis=1).astype(BF16)
    x2 = x1 + _dot(o, wxo_ref[...])
    x2_ref[...] = x2
    hf = _rms(x2, nfg_ref[...])
    hf_ref[...] = hf

    logits = _dot_nt(wrt_ref[...], hf, precision=lax.Precision.HIGHEST) + br_ref[...]
    ne = logits.shape[0]
    eid = lax.broadcasted_iota(I32, logits.shape, 0).astype(F32)
    work = logits
    vals, idxs, hots = [], [], []
    for _ in range(TOP_K):
        mx = jnp.max(work, axis=0, keepdims=True)
        idx = jnp.min(jnp.where(work == mx, eid, float(ne)), axis=0, keepdims=True)
        hot = eid == idx
        vals.append(mx)
        idxs.append(idx.astype(I32))
        hots.append(hot)
        work = jnp.where(hot, -jnp.inf, work)
    ex = [jnp.exp(vv - vals[0]) for vv in vals]
    den = ex[0] + ex[1] + ex[2] + ex[3]
    gates = [e_ / den for e_ in ex]

    multi = (hots[0] | hots[1] | hots[2] | hots[3])
    multi_f = jnp.where(multi, 1.0, 0.0).astype(F32)
    tri = (lax.broadcasted_iota(I32, (ts, ts), 0)
           < lax.broadcasted_iota(I32, (ts, ts), 1))
    before = _dot(multi_f.astype(BF16), jnp.where(tri, 1.0, 0.0).astype(BF16))

    @pl.when(first)
    def _():
        run_ref[...] = jnp.zeros(run_ref.shape, F32)

    run = run_ref[...]
    pos = before + run
    ranks = [jnp.sum(jnp.where(hot, pos, 0.0), axis=0, keepdims=True) for hot in hots]
    run_new = run + jnp.sum(multi_f, axis=1, keepdims=True)
    run_ref[...] = run_new
    cnt_ref[...] = jnp.broadcast_to(run_new, cnt_ref.shape).astype(I32)

    eidx_ref[...] = jnp.concatenate(idxs, axis=0)
    rank_ref[...] = jnp.concatenate(ranks, axis=0).astype(I32)
    g_rows = jnp.concatenate(gates + [jnp.zeros((128 - TOP_K, ts), F32)], axis=0)
    gate_ref[...] = g_rows.T


def _post(x2d, a2d, g0, gyb, w_attn_up, w_out, norm_x_g, w_xq, k_mem, v_mem, w_xo,
          norm_ffn_g, w_router_t, b_router, batch, ts):
    n, d = x2d.shape
    nt = n // batch // ts
    ml = k_mem.shape[1]
    tok = pl.BlockSpec((ts, d), lambda bi, j: (bi * nt + j, 0))
    lanes = pl.BlockSpec((TOP_K, ts), lambda bi, j: (0, bi * nt + j))
    const = lambda *shape: pl.BlockSpec(shape, lambda bi, j: (0,) * len(shape),
                                        pipeline_mode=pl.Buffered(1))
    mem = pl.BlockSpec((1, ml, d), lambda bi, j: (bi, 0, 0))
    return pl.pallas_call(
        _post_kernel,
        grid=(batch, nt),
        in_specs=[tok, tok, tok, tok, const(d, d), const(d, d), const(1, d), const(d, d),
                  mem, mem, const(d, d), const(1, d), const(N_EXPERTS, d),
                  const(N_EXPERTS, 1)],
        out_specs=[tok, tok, lanes, lanes,
                   pl.BlockSpec((ts, 128), lambda bi, j: (bi * nt + j, 0)),
                   pl.BlockSpec((N_EXPERTS, 128), lambda bi, j: (0, 0))],
        out_shape=[jax.ShapeDtypeStruct((n, d), F32), jax.ShapeDtypeStruct((n, d), F32),
                   jax.ShapeDtypeStruct((TOP_K, n), I32), jax.ShapeDtypeStruct((TOP_K, n), I32),
                   jax.ShapeDtypeStruct((n, 128), F32),
                   jax.ShapeDtypeStruct((N_EXPERTS, 128), I32)],
        scratch_shapes=[pltpu.VMEM((N_EXPERTS, 1), F32)],
        compiler_params=_cparams(2),
        name="post",
    )(x2d, a2d, g0, gyb, w_attn_up, w_out, norm_x_g, w_xq, k_mem, v_mem, w_xo,
      norm_ffn_g, w_router_t, b_router)


def _route_dest_kernel(pstart_ref, eidx_ref, rank_ref, dest_ref):
    eidx = eidx_ref[...]
    start = jnp.zeros(eidx.shape, I32)
    for e in range(N_EXPERTS):
        start = jnp.where(eidx == e, pstart_ref[e], start)
    dest_ref[...] = start + rank_ref[...]


def _route_dest(pstart, eidx, rank, tl):
    k, n = eidx.shape
    lanes = pl.BlockSpec((k, tl), lambda i: (0, i))
    return pl.pallas_call(
        _route_dest_kernel,
        grid=(n // tl,),
        in_specs=[pl.BlockSpec(memory_space=pltpu.SMEM), lanes, lanes],
        out_specs=lanes,
        out_shape=jax.ShapeDtypeStruct((k, n), I32),
        compiler_params=_cparams(1),
        name="route_dest",
    )(pstart, eidx, rank)


SC_CORES = 2
SC_SUBCORES = 16
SC_WORKERS = SC_CORES * SC_SUBCORES
MOVE_CHUNK = 32


def _worker_index_layout(dest, n):
    per_w = n // SC_WORKERS
    n_chunks = per_w // MOVE_CHUNK
    d4 = dest.reshape(TOP_K, SC_WORKERS, n_chunks, MOVE_CHUNK)
    return jnp.transpose(d4, (1, 2, 0, 3)).reshape(SC_WORKERS, n_chunks * TOP_K, MOVE_CHUNK)


def _sc_mesh():
    return plsc.VectorSubcoreMesh(core_axis_name="c", subcore_axis_name="s",
                                  num_cores=SC_CORES, num_subcores=SC_SUBCORES)


def _sc_dispatch(hf, dest_w, rows):
    n, d = hf.shape
    per_w = n // SC_WORKERS
    n_chunks = per_w // MOVE_CHUNK

    def body(hf_hbm, dest_hbm, xs_hbm, idx_v, rows_v, sem):
        wid = lax.axis_index("s") * SC_CORES + lax.axis_index("c")
        base = wid * per_w
        pltpu.sync_copy(dest_hbm.at[wid], idx_v)
        for c in range(n_chunks):
            pltpu.sync_copy(hf_hbm.at[pl.ds(base + c * MOVE_CHUNK, MOVE_CHUNK)], rows_v)
            copies = [pltpu.async_copy(rows_v, xs_hbm.at[idx_v.at[c * TOP_K + kk]], sem)
                      for kk in range(TOP_K)]
            for cp in copies:
                cp.wait()

    return pl.kernel(
        body,
        out_type=jax.ShapeDtypeStruct((rows, d), hf.dtype),
        mesh=_sc_mesh(),
        scratch_types=[pltpu.VMEM((n_chunks * TOP_K, MOVE_CHUNK), I32),
                       pltpu.VMEM((MOVE_CHUNK, d), hf.dtype),
                       pltpu.SemaphoreType.DMA],
        name="sc_dispatch",
    )(hf, dest_w)


def _sc_gather(ys, dest_w, n):
    _, d = ys.shape
    per_w = n // SC_WORKERS
    n_chunks = per_w // MOVE_CHUNK

    def body(ys_hbm, dest_hbm, yg_hbm, idx_v, rows_v, sem):
        wid = lax.axis_index("s") * SC_CORES + lax.axis_index("c")
        base = wid * per_w
        pltpu.sync_copy(dest_hbm.at[wid], idx_v)
        for c in range(n_chunks):
            for kk in range(TOP_K):
                pltpu.async_copy(ys_hbm.at[idx_v.at[c * TOP_K + kk]], rows_v, sem).wait()
                pltpu.sync_copy(
                    rows_v, yg_hbm.at[pl.ds(kk * n + base + c * MOVE_CHUNK, MOVE_CHUNK)])

    return pl.kernel(
        body,
        out_type=jax.ShapeDtypeStruct((TOP_K * n, d), ys.dtype),
        mesh=_sc_mesh(),
        scratch_types=[pltpu.VMEM((n_chunks * TOP_K, MOVE_CHUNK), I32),
                       pltpu.VMEM((MOVE_CHUNK, d), ys.dtype),
                       pltpu.SemaphoreType.DMA],
        name="sc_gather",
    )(ys, dest_w)


def _combine_kernel(x2_ref, gate_ref, fg_ref, yg_ref, o_ref):
    g = gate_ref[...]
    acc = x2_ref[...]
    for kk in range(TOP_K):
        acc = acc + g[:, kk:kk + 1] * yg_ref[kk]
    o_ref[...] = _rms(acc, fg_ref[...])


def _combine(x2, gate_tm, final_g, yg, ts):
    n, d = x2.shape
    return pl.pallas_call(
        _combine_kernel,
        grid=(n // ts,),
        in_specs=[pl.BlockSpec((ts, d), lambda i: (i, 0)),
                  pl.BlockSpec((ts, 128), lambda i: (i, 0)),
                  pl.BlockSpec((1, d), lambda i: (0, 0)),
                  pl.BlockSpec((TOP_K, ts, d), lambda i: (0, i, 0))],
        out_specs=pl.BlockSpec((ts, d), lambda i: (i, 0)),
        out_shape=jax.ShapeDtypeStruct((n, d), F32),
        compiler_params=_cparams(1),
        name="combine",
    )(x2, gate_tm, final_g, yg.reshape(TOP_K, n, d))


def _experts_kernel(be_ref, first_ref, nxt_ref, slot_ref, nb_ref,
                    xs_ref, bgu_ref, bd_ref, wgu_hbm, wd_hbm, ys_ref,
                    wgu_st, wd_st, wgu_bf, wd_bf, sem):
    i = pl.program_id(0)
    dff = wd_bf.shape[0]
    e = be_ref[i]

    def fetch(ex, sl):
        return (pltpu.make_async_copy(wgu_hbm.at[ex], wgu_st.at[sl], sem.at[0, sl]),
                pltpu.make_async_copy(wd_hbm.at[ex], wd_st.at[sl], sem.at[1, sl]))

    @pl.when(i == 0)
    def _():
        for cp in fetch(e, slot_ref[e]):
            cp.start()

    @pl.when(first_ref[i] == 1)
    def _():
        sl = slot_ref[e]
        for cp in fetch(e, sl):
            cp.wait()
        nx = nxt_ref[e]

        @pl.when(nx >= 0)
        def _():
            for cp in fetch(nx, 1 - sl):
                cp.start()

        wgu_bf[...] = wgu_st[sl].astype(BF16)
        wd_bf[...] = wd_st[sl].astype(BF16)

    @pl.when(i < nb_ref[0])
    def _():
        xb = xs_ref[...].astype(BF16)
        gu = _dot(xb, wgu_bf[...]) + bgu_ref[0]
        x_glu = jnp.minimum(gu[:, :dff], SWIGLU_LIMIT)
        x_lin = jnp.clip(gu[:, dff:], -SWIGLU_LIMIT, SWIGLU_LIMIT)
        act = x_glu * jax.nn.sigmoid(SWIGLU_ALPHA * x_glu) * (x_lin + 1.0)
        ys_ref[...] = _dot(act.astype(BF16), wd_bf[...]) + bd_ref[0]

    @pl.when(i >= nb_ref[0])
    def _():
        ys_ref[...] = jnp.zeros(ys_ref.shape, F32)


def _expert_schedule(counts, bm, n_blocks):
    ne = counts.shape[0]
    padded = (counts + bm - 1) // bm * bm
    pend = jnp.cumsum(padded)
    pstart = (pend - padded).astype(I32)
    nb_used = (pend[-1] // bm).astype(I32)
    blk = jnp.arange(n_blocks, dtype=I32)
    blk_c = jnp.minimum(blk, nb_used - 1)
    be = jnp.minimum(jnp.sum((pend[None, :] <= (blk_c * bm)[:, None]).astype(I32), axis=1),
                     ne - 1)
    first = ((blk < nb_used) & (blk * bm == pstart[be])).astype(I32)
    used = counts > 0
    seq = jnp.cumsum(used.astype(I32)) - 1
    ids = jnp.arange(ne, dtype=I32)
    later = used[None, :] & (ids[None, :] > ids[:, None])
    nxt = jnp.where(jnp.any(later, axis=1), jnp.argmax(later, axis=1), -1).astype(I32)
    slot = (seq & 1).astype(I32)
    return pstart, nb_used.reshape(1), be, first, nxt, slot


def _experts(be, first, nxt, slot, nb_used, xs, w_gu, b_gu, w_down, b_down, bm):
    rows, d = xs.shape
    ne, _, dff2 = w_gu.shape
    dff = dff2 // 2
    last = lambda i, be, fi, nx, sl, nb: (jnp.maximum(jnp.minimum(i, nb[0] - 1), 0), 0)
    grid_spec = pltpu.PrefetchScalarGridSpec(
        num_scalar_prefetch=5,
        grid=(rows // bm,),
        in_specs=[
            pl.BlockSpec((bm, d), last),
            pl.BlockSpec((1, 1, dff2), lambda i, be, fi, nx, sl, nb: (be[i], 0, 0)),
            pl.BlockSpec((1, 1, d), lambda i, be, fi, nx, sl, nb: (be[i], 0, 0)),
            pl.BlockSpec(memory_space=pl.ANY),
            pl.BlockSpec(memory_space=pl.ANY),
        ],
        out_specs=pl.BlockSpec((bm, d), lambda i, be, fi, nx, sl, nb: (i, 0)),
        scratch_shapes=[pltpu.VMEM((2, d, dff2), F32), pltpu.VMEM((2, dff, d), F32),
                        pltpu.VMEM((d, dff2), BF16), pltpu.VMEM((dff, d), BF16),
                        pltpu.SemaphoreType.DMA((2, 2))],
    )
    return pl.pallas_call(
        _experts_kernel,
        grid_spec=grid_spec,
        out_shape=jax.ShapeDtypeStruct((rows, d), F32),
        compiler_params=_cparams(1),
        name="experts",
    )(be, first, nxt, slot, nb_used, xs, b_gu.reshape(ne, 1, dff2),
      b_down.reshape(ne, 1, d), w_gu, w_down)


MIXER_TILE = 512
POST_TILE = 512
DEST_TILE = 2048
COMBINE_TILE = 256
EXPERT_BLOCK = 256


def kernel(x, mem, norm_mix_g, w_in, lambda_q1, lambda_k1, lambda_q2, lambda_k2, rel_bias,
           subln_g, w_attn_up, pool_mix, pool_scale, w_pool_up, w_gate, b_gate, w_out,
           norm_x_g, norm_mem_g, w_xq, w_xkv, w_xo, norm_ffn_g, w_router, b_router,
           w_gu, b_gu, w_down, b_down, final_norm_g):
    b, s, d = x.shape
    n = b * s
    assert w_in.shape[0] == 1, "single-layer block"
    row = lambda a: a.reshape(1, -1)
    bf = lambda a: a[0].astype(BF16)

    q, k, v, g0, gyb = _mixer_in(
        x, row(norm_mix_g[0]), bf(w_in), bf(w_gate), row(b_gate[0]), bf(pool_mix),
        row(pool_scale[0]), bf(w_pool_up), MIXER_TILE)

    bias, lam = _rel_bias(rel_bias, row(lambda_q1[0]), row(lambda_k1[0]),
                          row(lambda_q2[0]), row(lambda_k2[0]), s)
    a = _diff_attn(q, k, v, bias, lam, row(subln_g[0]))

    k_mem, v_mem = _mem_kv(mem, row(norm_mem_g[0]), bf(w_xkv))

    x2, hf, eidx, rank, gate_tm, counts = _post(
        x.reshape(n, d), a.reshape(n, d), g0, gyb, bf(w_attn_up), bf(w_out),
        row(norm_x_g[0]), bf(w_xq), k_mem, v_mem, bf(w_xo), row(norm_ffn_g[0]),
        w_router[0].T, b_router[0].reshape(-1, 1), b, POST_TILE)

    bm = EXPERT_BLOCK
    rows = n * TOP_K + N_EXPERTS * bm
    pstart, nb_used, be, first, nxt, slot = _expert_schedule(counts[:, 0], bm, rows // bm)

    dest_w = _worker_index_layout(_route_dest(pstart, eidx, rank, DEST_TILE), n)
    xs = _sc_dispatch(hf, dest_w, rows)
    ys = _experts(be, first, nxt, slot, nb_used, xs, w_gu[0], b_gu[0], w_down[0], b_down[0],
                  bm)
    yg = _sc_gather(ys, dest_w, n)
    out = _combine(x2, gate_tm, row(final_norm_g), yg, COMBINE_TILE)
    return out.reshape(b, s, d)
```

```python
import functools
import math

import numpy as np
import jax
import jax.numpy as jnp
from jax import lax
from jax.experimental import pallas as pl
from jax.experimental.pallas import tpu as pltpu
from jax.experimental.pallas import tpu_sc as plsc

F32 = jnp.float32
BF16 = jnp.bfloat16
I32 = jnp.int32

EPS = 1e-6
CHUNK = 64
N_HEADS = 8
HEAD_DIM = 64
HEAD_W = 2 * HEAD_DIM
POOL_WINDOWS = (2, 4, 8, 16)
POOL_GROUP = 128
POOL_PAD = 16
N_BUCKETS = 32
MAX_DISTANCE = 128
X_HEADS = 4
N_EXPERTS = 32
TOP_K = 4
SWIGLU_ALPHA = 1.702
SWIGLU_LIMIT = 7.0
LAMBDA_INIT = 0.8 - 0.6 * math.exp(-0.3 * 0)

ATTN_BLOCK = 256
NEG_BIG = -1e30
VMEM_LIMIT = 56 * 1024 * 1024


def _cparams(n_axes, vmem=VMEM_LIMIT):
    return pltpu.CompilerParams(
        dimension_semantics=("arbitrary",) * n_axes, vmem_limit_bytes=vmem)


def _rms(xf, g):
    ms = jnp.mean(xf * xf, axis=-1, keepdims=True)
    return xf * lax.rsqrt(ms + EPS) * g


def _dot(a, b):
    return jnp.dot(a, b, preferred_element_type=F32)


def _pack_halves(x):
    c = x.shape[1] // 2
    as_bits = lambda v: lax.bitcast_convert_type(v.astype(BF16).astype(F32), jnp.uint32)
    return as_bits(x[:, :c]) | (as_bits(x[:, c:]) >> 16)


def _unpack_halves(w):
    hi = lax.bitcast_convert_type(w & jnp.uint32(0xFFFF0000), F32)
    lo = lax.bitcast_convert_type(w << 16, F32)
    return hi, lo


def _dot_nt(a, b, precision=None):
    return lax.dot_general(a, b, (((1,), (1,)), ((), ())),
                           preferred_element_type=F32, precision=precision)


def _mixer_in_kernel(x_ref, g_ref, wq_ref, wk_ref, wv_ref, wu_ref, wg_ref, bg_ref,
                     pmix_ref, pscale_ref, wpu_ref,
                     q_ref, k_ref, v_ref, g0_ref, gyb_ref, ext_ref):
    ts = x_ref.shape[1]
    d = x_ref.shape[2]
    j = pl.program_id(1)
    h = _rms(x_ref[0], g_ref[...]).astype(BF16)
    q_ref[0] = (_dot(h, wq_ref[...]) * (HEAD_DIM ** -0.5)).astype(BF16)
    k_ref[0] = _dot(h, wk_ref[...]).astype(BF16)
    v_ref[0] = _dot(h, wv_ref[...]).astype(BF16)
    u = _dot(h, wu_ref[...])

    @pl.when(j == 0)
    def _():
        ext_ref[0:POOL_PAD, :] = jnp.zeros((POOL_PAD, u.shape[1]), F32)

    ext_ref[POOL_PAD:POOL_PAD + ts, :] = u
    e = ext_ref[...]
    sums = {}
    s = e
    w = 1
    while w < max(POOL_WINDOWS):
        s = s + pltpu.roll(s, w, 0)
        w *= 2
        sums[w] = s
    ext_ref[0:POOL_PAD, :] = ext_ref[ts:ts + POOL_PAD, :]

    pos = (j * ts + lax.broadcasted_iota(I32, (ts, 1), 0) + 1).astype(F32)
    mixed = []
    for gi, w in enumerate(POOL_WINDOWS):
        sl = slice(gi * POOL_GROUP, (gi + 1) * POOL_GROUP)
        win = sums[w][POOL_PAD:, sl]
        pooled = win / jnp.minimum(pos, float(w)) - u[:, sl]
        mixed.append(_dot(pooled.astype(BF16), pmix_ref[gi]) * pscale_ref[:, sl])
    mixed = jnp.concatenate(mixed, axis=1).astype(BF16)
    y_b = _dot(mixed, wpu_ref[...])

    gate = jax.nn.sigmoid(_dot(h, wg_ref[...]) + bg_ref[...])
    g0_ref[...] = gate[:, :d].astype(BF16)
    gyb_ref[...] = (gate[:, d:] * y_b).astype(BF16)


def _mixer_in(x, norm_g, w_in, w_gate, b_gate, pool_mix, pool_scale, w_pool_up, ts):
    b, s, d = x.shape
    aw = N_HEADS * HEAD_W
    pw = len(POOL_WINDOWS) * POOL_GROUP
    n = b * s
    nt = s // ts
    const = lambda *shape: pl.BlockSpec(shape, lambda bi, j: (0,) * len(shape))
    tok3 = pl.BlockSpec((1, ts, aw), lambda bi, j: (bi, j, 0))
    tok2 = pl.BlockSpec((ts, d), lambda bi, j: (bi * nt + j, 0))
    return pl.pallas_call(
        _mixer_in_kernel,
        grid=(b, nt),
        in_specs=[
            pl.BlockSpec((1, ts, d), lambda bi, j: (bi, j, 0)),
            const(1, d),
            pl.BlockSpec((d, aw), lambda bi, j: (0, 0)),
            pl.BlockSpec((d, aw), lambda bi, j: (0, 1)),
            pl.BlockSpec((d, aw), lambda bi, j: (0, 2)),
            pl.BlockSpec((d, pw), lambda bi, j: (0, 3 * aw // pw)),
            const(d, 2 * d),
            const(1, 2 * d),
            const(len(POOL_WINDOWS), POOL_GROUP, POOL_GROUP),
            const(1, pw),
            const(pw, d),
        ],
        out_specs=[tok3, tok3, tok3, tok2, tok2],
        out_shape=[jax.ShapeDtypeStruct((b, s, aw), BF16)] * 3
        + [jax.ShapeDtypeStruct((n, d), BF16)] * 2,
        scratch_shapes=[pltpu.VMEM((ts + POOL_PAD, pw), F32)],
        compiler_params=_cparams(2),
        name="mixer_in",
    )(x, norm_g, w_in, w_in, w_in, w_in, w_gate, b_gate, pool_mix, pool_scale, w_pool_up)


def _rel_bucket(rel, log=jnp.log, f32=lambda a: a.astype(jnp.float32),
                i32=lambda a: a.astype(jnp.int32), xp=jnp):
    nb = N_BUCKETS // 2
    ret = i32(rel > 0) * nb
    n = xp.abs(rel)
    max_exact = nb // 2
    nf = f32(xp.maximum(n, 1))
    large = max_exact + i32(log(nf / max_exact) / math.log(MAX_DISTANCE / max_exact)
                            * (nb - max_exact))
    large = xp.minimum(large, nb - 1)
    return ret + xp.where(n < max_exact, n, large)


def _far_bucket(block, seq):
    rel = -np.arange(block + 1, max(seq, block + 2), dtype=np.int32)
    bk = _rel_bucket(rel, log=np.log, f32=lambda a: a.astype(np.float32),
                     i32=lambda a: a.astype(np.int32), xp=np)
    assert (bk == bk[0]).all(), "far keys must share one relative-position bucket"
    return int(bk[0])


def _rel_bias_kernel(far_bucket, tab_ref, bidx_ref, lq1_ref, lk1_ref, lq2_ref, lk2_ref,
                     bias_ref, lam_ref):
    h = pl.program_id(0)
    bidx = bidx_ref[...]
    acc = jnp.zeros(bidx.shape, F32)
    for bkt in range(N_BUCKETS):
        acc = jnp.where(bidx == bkt, tab_ref[bkt, h], acc)
    acc = acc - tab_ref[far_bucket, h]
    bias_ref[0] = jnp.where(bidx < 0, -jnp.inf, acc)
    lam = (jnp.exp(jnp.sum(lq1_ref[...] * lk1_ref[...], keepdims=True))
           - jnp.exp(jnp.sum(lq2_ref[...] * lk2_ref[...], keepdims=True)) + LAMBDA_INIT)
    lam_ref[...] = jnp.broadcast_to(lam, lam_ref.shape)


def _rel_bias(rel_bias, lq1, lk1, lq2, lk2, seq):
    blk = ATTN_BLOCK
    qpos = jnp.arange(blk, dtype=I32)[:, None]
    kpos = jnp.arange(blk, dtype=I32)[None, :]
    diag = jnp.where(kpos // CHUNK <= qpos // CHUNK, _rel_bucket(kpos - qpos), -1)
    prev = _rel_bucket(kpos - (qpos + blk))
    bidx = jnp.stack([diag, prev]).astype(I32)
    vec = pl.BlockSpec((1, HEAD_DIM), lambda h: (0, 0))
    return pl.pallas_call(
        functools.partial(_rel_bias_kernel, _far_bucket(blk, seq)),
        grid=(N_HEADS,),
        in_specs=[
            pl.BlockSpec(memory_space=pltpu.SMEM),
            pl.BlockSpec((2, blk, blk), lambda h: (0, 0, 0)),
            vec, vec, vec, vec,
        ],
        out_specs=[
            pl.BlockSpec((1, 2, blk, blk), lambda h: (h, 0, 0, 0)),
            pl.BlockSpec((8, 128), lambda h: (0, 0)),
        ],
        out_shape=[jax.ShapeDtypeStruct((N_HEADS, 2, blk, blk), F32),
                   jax.ShapeDtypeStruct((8, 128), F32)],
        compiler_params=_cparams(1),
        name="rel_bias",
    )(rel_bias, bidx, lq1, lk1, lq2, lk2)


def _diff_attn_kernel(q_ref, k_ref, v_ref, bias_ref, lam_ref, sg_ref, o_ref, vext_ref):
    s_len = q_ref.shape[1]
    tq = ATTN_BLOCK
    vext_ref[:, :HEAD_W] = v_ref[0]
    vext_ref[:, HEAD_W:] = jnp.ones((s_len, HEAD_W), BF16)
    lam = lam_ref[0:1, 0:1]
    lane = lax.broadcasted_iota(I32, (tq, HEAD_W), 1)
    b_diag = bias_ref[0, 0]
    b_prev = bias_ref[0, 1]
    b_diag = jnp.concatenate([b_diag, b_diag], axis=0)
    b_prev = jnp.concatenate([b_prev, b_prev], axis=0)

    for qi in range(s_len // tq):
        n_keys = (qi + 1) * tq
        q = q_ref[0, qi * tq:(qi + 1) * tq, :]
        zero = jnp.zeros_like(q)
        qs = jnp.concatenate([jnp.where(lane < HEAD_DIM, q, zero),
                              jnp.where(lane >= HEAD_DIM, q, zero)], axis=0)
        s = _dot_nt(qs, k_ref[0, :n_keys, :])
        pieces = []
        if qi >= 2:
            pieces.append(s[:, :n_keys - 2 * tq])
        if qi >= 1:
            pieces.append(s[:, n_keys - 2 * tq:n_keys - tq] + b_prev)
        pieces.append(s[:, n_keys - tq:] + b_diag)
        s = jnp.concatenate(pieces, axis=1) if len(pieces) > 1 else pieces[0]
        m = jnp.max(s, axis=-1, keepdims=True)
        p = jnp.exp(s - m).astype(BF16)
        acc = _dot(p, vext_ref[:n_keys, :])
        o = acc[:, :HEAD_W] / acc[:, HEAD_W:HEAD_W + 1]
        a = o[:tq] - lam * o[tq:]
        o_ref[0, qi * tq:(qi + 1) * tq, :] = (
            _rms(a, sg_ref[...]) * (1.0 - LAMBDA_INIT)).astype(BF16)


def _diff_attn(q, k, v, bias, lam, subln_g):
    b, s, aw = q.shape
    tq = ATTN_BLOCK
    spec = pl.BlockSpec((1, s, HEAD_W), lambda bi, h: (bi, 0, h))
    return pl.pallas_call(
        _diff_attn_kernel,
        grid=(b, N_HEADS),
        in_specs=[
            spec, spec, spec,
            pl.BlockSpec((1, 2, tq, tq), lambda bi, h: (h, 0, 0, 0)),
            pl.BlockSpec((8, 128), lambda bi, h: (0, 0)),
            pl.BlockSpec((1, HEAD_W), lambda bi, h: (0, 0)),
        ],
        out_specs=spec,
        out_shape=jax.ShapeDtypeStruct((b, s, aw), BF16),
        scratch_shapes=[pltpu.VMEM((s, 2 * HEAD_W), BF16)],
        compiler_params=_cparams(2),
        name="diff_attn",
    )(q, k, v, bias, lam, subln_g)


def _mem_kv_kernel(m_ref, g_ref, w_ref, k_ref, v_ref):
    d = m_ref.shape[2]
    m = _rms(m_ref[0], g_ref[...]).astype(BF16)
    kv = _dot(m, w_ref[...])
    k_ref[0] = kv[:, :d].astype(BF16)
    v_ref[0] = kv[:, d:].astype(BF16)


def _mem_kv(mem, norm_g, w_xkv):
    b, ml, d = mem.shape
    blk = pl.BlockSpec((1, ml, d), lambda bi: (bi, 0, 0))
    return pl.pallas_call(
        _mem_kv_kernel,
        grid=(b,),
        in_specs=[blk, pl.BlockSpec((1, d), lambda bi: (0, 0)),
                  pl.BlockSpec((d, 2 * d), lambda bi: (0, 0))],
        out_specs=[blk, blk],
        out_shape=[jax.ShapeDtypeStruct((b, ml, d), BF16)] * 2,
        compiler_params=_cparams(1),
        name="mem_kv",
    )(mem, norm_g, w_xkv)


def _post_kernel(x_ref, a_ref, g0_ref, gyb_ref, wau_ref, wo_ref, nxg_ref, wxq_ref,
                 km_ref, vm_ref, wxo_ref, nfg_ref, wrt_ref, br_ref,
                 x2_ref, hf_ref, eidx_ref, rank_ref, gate_ref, cnt_ref, run_ref):
    ts, d = x_ref.shape
    first = (pl.program_id(0) == 0) & (pl.program_id(1) == 0)

    y_a = _dot(a_ref[...], wau_ref[...])
    merged = g0_ref[...].astype(F32) * y_a + gyb_ref[...].astype(F32)
    x1 = x_ref[...] + _dot(merged.astype(BF16), wo_ref[...])

    hd = d // X_HEADS
    hx = _rms(x1, nxg_ref[...]).astype(BF16)
    qx = (_dot(hx, wxq_ref[...]) * (hd ** -0.5)).astype(BF16)
    heads = []
    for hh in range(X_HEADS):
        sl = slice(hh * hd, (hh + 1) * hd)
        s = _dot_nt(qx[:, sl], km_ref[0, :, sl])
        p = jnp.exp(s - jnp.max(s, axis=-1, keepdims=True))
        p = p / jnp.sum(p, axis=-1, keepdims=True)
        heads.append(_dot(p.astype(BF16), vm_ref[0, :, sl]))
    o = jnp.concatenate(heads, axis=1).astype(BF16)
    x2 = x1 + _dot(o, wxo_ref[...])
    x2_ref[...] = x2
    hf = _rms(x2, nfg_ref[...])
    hf_ref[...] = _pack_halves(hf)

    logits = _dot_nt(wrt_ref[...], hf, precision=lax.Precision.HIGHEST) + br_ref[...]
    ne = logits.shape[0]
    eid = lax.broadcasted_iota(I32, logits.shape, 0).astype(F32)
    work = logits
    vals, idxs, hots = [], [], []
    for _ in range(TOP_K):
        mx = jnp.max(work, axis=0, keepdims=True)
        idx = jnp.min(jnp.where(work == mx, eid, float(ne)), axis=0, keepdims=True)
        hot = eid == idx
        vals.append(mx)
        idxs.append(idx.astype(I32))
        hots.append(hot)
        work = jnp.where(hot, -jnp.inf, work)
    ex = [jnp.exp(vv - vals[0]) for vv in vals]
    den = ex[0] + ex[1] + ex[2] + ex[3]
    gates = [e_ / den for e_ in ex]

    multi = (hots[0] | hots[1] | hots[2] | hots[3])
    multi_f = jnp.where(multi, 1.0, 0.0).astype(F32)
    tri = (lax.broadcasted_iota(I32, (ts, ts), 0)
           < lax.broadcasted_iota(I32, (ts, ts), 1))
    before = _dot(multi_f.astype(BF16), jnp.where(tri, 1.0, 0.0).astype(BF16))

    @pl.when(first)
    def _():
        run_ref[...] = jnp.zeros(run_ref.shape, F32)

    run = run_ref[...]
    pos = before + run
    ranks = [jnp.sum(jnp.where(hot, pos, 0.0), axis=0, keepdims=True) for hot in hots]
    run_new = run + jnp.sum(multi_f, axis=1, keepdims=True)
    run_ref[...] = run_new
    cnt_ref[...] = jnp.broadcast_to(run_new, cnt_ref.shape).astype(I32)

    eidx_ref[...] = jnp.concatenate(idxs, axis=0)
    rank_ref[...] = jnp.concatenate(ranks, axis=0).astype(I32)
    g_rows = jnp.concatenate(gates + [jnp.zeros((128 - TOP_K, ts), F32)], axis=0)
    gate_ref[...] = g_rows.T


def _post(x2d, a2d, g0, gyb, w_attn_up, w_out, norm_x_g, w_xq, k_mem, v_mem, w_xo,
          norm_ffn_g, w_router_t, b_router, batch, ts):
    n, d = x2d.shape
    nt = n // batch // ts
    ml = k_mem.shape[1]
    tok = pl.BlockSpec((ts, d), lambda bi, j: (bi * nt + j, 0))
    lanes = pl.BlockSpec((TOP_K, ts), lambda bi, j: (0, bi * nt + j))
    const = lambda *shape: pl.BlockSpec(shape, lambda bi, j: (0,) * len(shape),
                                        pipeline_mode=pl.Buffered(1))
    mem = pl.BlockSpec((1, ml, d), lambda bi, j: (bi, 0, 0))
    return pl.pallas_call(
        _post_kernel,
        grid=(batch, nt),
        in_specs=[tok, tok, tok, tok, const(d, d), const(d, d), const(1, d), const(d, d),
                  mem, mem, const(d, d), const(1, d), const(N_EXPERTS, d),
                  const(N_EXPERTS, 1)],
        out_specs=[tok, pl.BlockSpec((ts, d // 2), lambda bi, j: (bi * nt + j, 0)),
                   lanes, lanes,
                   pl.BlockSpec((ts, 128), lambda bi, j: (bi * nt + j, 0)),
                   pl.BlockSpec((N_EXPERTS, 128), lambda bi, j: (0, 0))],
        out_shape=[jax.ShapeDtypeStruct((n, d), F32),
                   jax.ShapeDtypeStruct((n, d // 2), jnp.uint32),
                   jax.ShapeDtypeStruct((TOP_K, n), I32), jax.ShapeDtypeStruct((TOP_K, n), I32),
                   jax.ShapeDtypeStruct((n, 128), F32),
                   jax.ShapeDtypeStruct((N_EXPERTS, 128), I32)],
        scratch_shapes=[pltpu.VMEM((N_EXPERTS, 1), F32)],
        compiler_params=_cparams(2),
        name="post",
    )(x2d, a2d, g0, gyb, w_attn_up, w_out, norm_x_g, w_xq, k_mem, v_mem, w_xo,
      norm_ffn_g, w_router_t, b_router)


def _route_dest_kernel(pstart_ref, eidx_ref, rank_ref, dest_ref):
    eidx = eidx_ref[...]
    start = jnp.zeros(eidx.shape, I32)
    for e in range(N_EXPERTS):
        start = jnp.where(eidx == e, pstart_ref[e], start)
    dest_ref[...] = start + rank_ref[...]


def _route_dest(pstart, eidx, rank, tl):
    k, n = eidx.shape
    lanes = pl.BlockSpec((k, tl), lambda i: (0, i))
    return pl.pallas_call(
        _route_dest_kernel,
        grid=(n // tl,),
        in_specs=[pl.BlockSpec(memory_space=pltpu.SMEM), lanes, lanes],
        out_specs=lanes,
        out_shape=jax.ShapeDtypeStruct((k, n), I32),
        compiler_params=_cparams(1),
        name="route_dest",
    )(pstart, eidx, rank)


SC_CORES = 2
SC_SUBCORES = 16
SC_WORKERS = SC_CORES * SC_SUBCORES
MOVE_CHUNK = 64


def _worker_index_layout(dest, n):
    per_w = n // SC_WORKERS
    n_chunks = per_w // MOVE_CHUNK
    d4 = dest.reshape(TOP_K, SC_WORKERS, n_chunks, MOVE_CHUNK)
    return jnp.transpose(d4, (1, 2, 0, 3)).reshape(SC_WORKERS, n_chunks * TOP_K, MOVE_CHUNK)


def _sc_mesh():
    return plsc.VectorSubcoreMesh(core_axis_name="c", subcore_axis_name="s",
                                  num_cores=SC_CORES, num_subcores=SC_SUBCORES)


def _sc_dispatch(hf, dest_w, rows):
    n, d = hf.shape
    per_w = n // SC_WORKERS
    n_chunks = per_w // MOVE_CHUNK

    def body(hf_hbm, dest_hbm, xs_hbm, idx_v, rows_v, sem):
        wid = lax.axis_index("s") * SC_CORES + lax.axis_index("c")
        base = wid * per_w
        pltpu.sync_copy(dest_hbm.at[wid], idx_v)
        for c in range(n_chunks):
            pltpu.sync_copy(hf_hbm.at[pl.ds(base + c * MOVE_CHUNK, MOVE_CHUNK)], rows_v)
            copies = [pltpu.async_copy(rows_v, xs_hbm.at[idx_v.at[c * TOP_K + kk]], sem)
                      for kk in range(TOP_K)]
            for cp in copies:
                cp.wait()

    return pl.kernel(
        body,
        out_type=jax.ShapeDtypeStruct((rows, d), hf.dtype),
        mesh=_sc_mesh(),
        scratch_types=[pltpu.VMEM((n_chunks * TOP_K, MOVE_CHUNK), I32),
                       pltpu.VMEM((MOVE_CHUNK, d), hf.dtype),
                       pltpu.SemaphoreType.DMA],
        name="sc_dispatch",
    )(hf, dest_w)


def _sc_gather(ys, dest_w, n):
    _, d = ys.shape
    per_w = n // SC_WORKERS
    n_chunks = per_w // MOVE_CHUNK

    def body(ys_hbm, dest_hbm, yg_hbm, idx_v, rows_v, sem):
        wid = lax.axis_index("s") * SC_CORES + lax.axis_index("c")
        base = wid * per_w
        pltpu.sync_copy(dest_hbm.at[wid], idx_v)
        for c in range(n_chunks):
            for kk in range(TOP_K):
                pltpu.async_copy(ys_hbm.at[idx_v.at[c * TOP_K + kk]], rows_v, sem).wait()
                pltpu.sync_copy(
                    rows_v, yg_hbm.at[pl.ds(kk * n + base + c * MOVE_CHUNK, MOVE_CHUNK)])

    return pl.kernel(
        body,
        out_type=jax.ShapeDtypeStruct((TOP_K * n, d), ys.dtype),
        mesh=_sc_mesh(),
        scratch_types=[pltpu.VMEM((n_chunks * TOP_K, MOVE_CHUNK), I32),
                       pltpu.VMEM((MOVE_CHUNK, d), ys.dtype),
                       pltpu.SemaphoreType.DMA],
        name="sc_gather",
    )(ys, dest_w)


def _combine_kernel(x2_ref, gate_ref, fg_ref, yg_ref, o_ref):
    g = gate_ref[...]
    x2 = x2_ref[...]
    c = x2.shape[1] // 2
    acc_hi, acc_lo = x2[:, :c], x2[:, c:]
    for kk in range(TOP_K):
        hi, lo = _unpack_halves(yg_ref[kk])
        acc_hi = acc_hi + g[:, kk:kk + 1] * hi
        acc_lo = acc_lo + g[:, kk:kk + 1] * lo
    o_ref[...] = _rms(jnp.concatenate([acc_hi, acc_lo], axis=1), fg_ref[...])


def _combine(x2, gate_tm, final_g, yg, ts):
    n, d = x2.shape
    dw = yg.shape[1]
    return pl.pallas_call(
        _combine_kernel,
        grid=(n // ts,),
        in_specs=[pl.BlockSpec((ts, d), lambda i: (i, 0)),
                  pl.BlockSpec((ts, 128), lambda i: (i, 0)),
                  pl.BlockSpec((1, d), lambda i: (0, 0)),
                  pl.BlockSpec((TOP_K, ts, dw), lambda i: (0, i, 0))],
        out_specs=pl.BlockSpec((ts, d), lambda i: (i, 0)),
        out_shape=jax.ShapeDtypeStruct((n, d), F32),
        compiler_params=_cparams(1),
        name="combine",
    )(x2, gate_tm, final_g, yg.reshape(TOP_K, n, dw))


def _experts_kernel(be_ref, first_ref, nxt_ref, slot_ref, nb_ref,
                    xs_ref, bgu_ref, bd_ref, wgu_hbm, wd_hbm, ys_ref,
                    wgu_st, wd_st, wgu_bf, wd_bf, sem):
    i = pl.program_id(0)
    dff = wd_bf.shape[0]
    e = be_ref[i]

    def fetch(ex, sl):
        return (pltpu.make_async_copy(wgu_hbm.at[ex], wgu_st.at[sl], sem.at[0, sl]),
                pltpu.make_async_copy(wd_hbm.at[ex], wd_st.at[sl], sem.at[1, sl]))

    @pl.when(i == 0)
    def _():
        for cp in fetch(e, slot_ref[e]):
            cp.start()

    @pl.when(first_ref[i] == 1)
    def _():
        sl = slot_ref[e]
        for cp in fetch(e, sl):
            cp.wait()
        nx = nxt_ref[e]

        @pl.when(nx >= 0)
        def _():
            for cp in fetch(nx, 1 - sl):
                cp.start()

        wgu_bf[...] = wgu_st[sl].astype(BF16)
        wd_bf[...] = wd_st[sl].astype(BF16)

    @pl.when(i < nb_ref[0])
    def _():
        xb = jnp.concatenate(_unpack_halves(xs_ref[...]), axis=1).astype(BF16)
        gu = _dot(xb, wgu_bf[...]) + bgu_ref[0]
        x_glu = jnp.minimum(gu[:, :dff], SWIGLU_LIMIT)
        x_lin = jnp.clip(gu[:, dff:], -SWIGLU_LIMIT, SWIGLU_LIMIT)
        act = x_glu * jax.nn.sigmoid(SWIGLU_ALPHA * x_glu) * (x_lin + 1.0)
        ys_ref[...] = _pack_halves(_dot(act.astype(BF16), wd_bf[...]) + bd_ref[0])

    @pl.when(i >= nb_ref[0])
    def _():
        ys_ref[...] = jnp.zeros(ys_ref.shape, jnp.uint32)


def _expert_schedule(counts, bm, n_blocks):
    ne = counts.shape[0]
    padded = (counts + bm - 1) // bm * bm
    pend = jnp.cumsum(padded)
    pstart = (pend - padded).astype(I32)
    nb_used = (pend[-1] // bm).astype(I32)
    blk = jnp.arange(n_blocks, dtype=I32)
    blk_c = jnp.minimum(blk, nb_used - 1)
    be = jnp.minimum(jnp.sum((pend[None, :] <= (blk_c * bm)[:, None]).astype(I32), axis=1),
                     ne - 1)
    first = ((blk < nb_used) & (blk * bm == pstart[be])).astype(I32)
    used = counts > 0
    seq = jnp.cumsum(used.astype(I32)) - 1
    ids = jnp.arange(ne, dtype=I32)
    later = used[None, :] & (ids[None, :] > ids[:, None])
    nxt = jnp.where(jnp.any(later, axis=1), jnp.argmax(later, axis=1), -1).astype(I32)
    slot = (seq & 1).astype(I32)
    return pstart, nb_used.reshape(1), be, first, nxt, slot


def _experts(be, first, nxt, slot, nb_used, xs, w_gu, b_gu, w_down, b_down, bm):
    rows, dw = xs.shape
    ne, d, dff2 = w_gu.shape
    dff = dff2 // 2
    last = lambda i, be, fi, nx, sl, nb: (jnp.maximum(jnp.minimum(i, nb[0] - 1), 0), 0)
    grid_spec = pltpu.PrefetchScalarGridSpec(
        num_scalar_prefetch=5,
        grid=(rows // bm,),
        in_specs=[
            pl.BlockSpec((bm, dw), last),
            pl.BlockSpec((1, 1, dff2), lambda i, be, fi, nx, sl, nb: (be[i], 0, 0)),
            pl.BlockSpec((1, 1, d), lambda i, be, fi, nx, sl, nb: (be[i], 0, 0)),
            pl.BlockSpec(memory_space=pl.ANY),
            pl.BlockSpec(memory_space=pl.ANY),
        ],
        out_specs=pl.BlockSpec((bm, dw), lambda i, be, fi, nx, sl, nb: (i, 0)),
        scratch_shapes=[pltpu.VMEM((2, d, dff2), F32), pltpu.VMEM((2, dff, d), F32),
                        pltpu.VMEM((d, dff2), BF16), pltpu.VMEM((dff, d), BF16),
                        pltpu.SemaphoreType.DMA((2, 2))],
    )
    return pl.pallas_call(
        _experts_kernel,
        grid_spec=grid_spec,
        out_shape=jax.ShapeDtypeStruct((rows, dw), jnp.uint32),
        compiler_params=_cparams(1),
        name="experts",
    )(be, first, nxt, slot, nb_used, xs, b_gu.reshape(ne, 1, dff2),
      b_down.reshape(ne, 1, d), w_gu, w_down)


MIXER_TILE = 512
POST_TILE = 512
DEST_TILE = 2048
COMBINE_TILE = 512
EXPERT_BLOCK = 256


def kernel(x, mem, norm_mix_g, w_in, lambda_q1, lambda_k1, lambda_q2, lambda_k2, rel_bias,
           subln_g, w_attn_up, pool_mix, pool_scale, w_pool_up, w_gate, b_gate, w_out,
           norm_x_g, norm_mem_g, w_xq, w_xkv, w_xo, norm_ffn_g, w_router, b_router,
           w_gu, b_gu, w_down, b_down, final_norm_g):
    b, s, d = x.shape
    n = b * s
    assert w_in.shape[0] == 1, "single-layer block"
    row = lambda a: a.reshape(1, -1)
    bf = lambda a: a[0].astype(BF16)

    q, k, v, g0, gyb = _mixer_in(
        x, row(norm_mix_g[0]), bf(w_in), bf(w_gate), row(b_gate[0]), bf(pool_mix),
        row(pool_scale[0]), bf(w_pool_up), MIXER_TILE)

    bias, lam = _rel_bias(rel_bias, row(lambda_q1[0]), row(lambda_k1[0]),
                          row(lambda_q2[0]), row(lambda_k2[0]), s)
    a = _diff_attn(q, k, v, bias, lam, row(subln_g[0]))

    k_mem, v_mem = _mem_kv(mem, row(norm_mem_g[0]), bf(w_xkv))

    x2, hf, eidx, rank, gate_tm, counts = _post(
        x.reshape(n, d), a.reshape(n, d), g0, gyb, bf(w_attn_up), bf(w_out),
        row(norm_x_g[0]), bf(w_xq), k_mem, v_mem, bf(w_xo), row(norm_ffn_g[0]),
        w_router[0].T, b_router[0].reshape(-1, 1), b, POST_TILE)

    bm = EXPERT_BLOCK
    rows = n * TOP_K + N_EXPERTS * bm
    pstart, nb_used, be, first, nxt, slot = _expert_schedule(counts[:, 0], bm, rows // bm)

    dest_w = _worker_index_layout(_route_dest(pstart, eidx, rank, DEST_TILE), n)
    xs = _sc_dispatch(hf, dest_w, rows)
    ys = _experts(be, first, nxt, slot, nb_used, xs, w_gu[0], b_gu[0], w_down[0], b_down[0],
                  bm)
    yg = _sc_gather(ys, dest_w, n)
    out = _combine(x2, gate_tm, row(final_norm_g), yg, COMBINE_TILE)
    return out.reshape(b, s, d)
```

```python
import functools
import math

import numpy as np
import jax
import jax.numpy as jnp
from jax import lax
from jax.experimental import pallas as pl
from jax.experimental.pallas import tpu as pltpu
from jax.experimental.pallas import tpu_sc as plsc

F32 = jnp.float32
BF16 = jnp.bfloat16
I32 = jnp.int32

EPS = 1e-6
CHUNK = 64
N_HEADS = 8
HEAD_DIM = 64
HEAD_W = 2 * HEAD_DIM
POOL_WINDOWS = (2, 4, 8, 16)
POOL_GROUP = 128
POOL_PAD = 16
N_BUCKETS = 32
MAX_DISTANCE = 128
X_HEADS = 4
N_EXPERTS = 32
TOP_K = 4
SWIGLU_ALPHA = 1.702
SWIGLU_LIMIT = 7.0
LAMBDA_INIT = 0.8 - 0.6 * math.exp(-0.3 * 0)

ATTN_BLOCK = 256
NEG_BIG = -1e30
VMEM_LIMIT = 56 * 1024 * 1024


def _cparams(n_axes, vmem=VMEM_LIMIT):
    return pltpu.CompilerParams(
        dimension_semantics=("arbitrary",) * n_axes, vmem_limit_bytes=vmem)


def _rms(xf, g):
    ms = jnp.mean(xf * xf, axis=-1, keepdims=True)
    return xf * lax.rsqrt(ms + EPS) * g


def _dot(a, b):
    return jnp.dot(a, b, preferred_element_type=F32)


def _pack_halves(x):
    c = x.shape[1] // 2
    as_bits = lambda v: lax.bitcast_convert_type(v.astype(BF16).astype(F32), jnp.uint32)
    return as_bits(x[:, :c]) | (as_bits(x[:, c:]) >> 16)


def _unpack_halves(w):
    hi = lax.bitcast_convert_type(w & jnp.uint32(0xFFFF0000), F32)
    lo = lax.bitcast_convert_type(w << 16, F32)
    return hi, lo


def _dot_nt(a, b, precision=None):
    return lax.dot_general(a, b, (((1,), (1,)), ((), ())),
                           preferred_element_type=F32, precision=precision)


def _mixer_in_kernel(x_ref, g_ref, wq_ref, wk_ref, wv_ref, wu_ref, wg_ref, bg_ref,
                     pmix_ref, pscale_ref, wpu_ref,
                     q_ref, k_ref, v_ref, g0_ref, gyb_ref, ext_ref):
    ts = x_ref.shape[1]
    d = x_ref.shape[2]
    j = pl.program_id(1)
    h = _rms(x_ref[0], g_ref[...]).astype(BF16)
    q_ref[0] = (_dot(h, wq_ref[...]) * (HEAD_DIM ** -0.5)).astype(BF16)
    k_ref[0] = _dot(h, wk_ref[...]).astype(BF16)
    v_ref[0] = _dot(h, wv_ref[...]).astype(BF16)
    u = _dot(h, wu_ref[...])

    @pl.when(j == 0)
    def _():
        ext_ref[0:POOL_PAD, :] = jnp.zeros((POOL_PAD, u.shape[1]), F32)

    ext_ref[POOL_PAD:POOL_PAD + ts, :] = u
    e = ext_ref[...]
    sums = {}
    s = e
    w = 1
    while w < max(POOL_WINDOWS):
        s = s + pltpu.roll(s, w, 0)
        w *= 2
        sums[w] = s
    ext_ref[0:POOL_PAD, :] = ext_ref[ts:ts + POOL_PAD, :]

    pos = (j * ts + lax.broadcasted_iota(I32, (ts, 1), 0) + 1).astype(F32)
    mixed = []
    for gi, w in enumerate(POOL_WINDOWS):
        sl = slice(gi * POOL_GROUP, (gi + 1) * POOL_GROUP)
        win = sums[w][POOL_PAD:, sl]
        pooled = win / jnp.minimum(pos, float(w)) - u[:, sl]
        mixed.append(_dot(pooled.astype(BF16), pmix_ref[gi]) * pscale_ref[:, sl])
    mixed = jnp.concatenate(mixed, axis=1).astype(BF16)
    y_b = _dot(mixed, wpu_ref[...])

    gate = jax.nn.sigmoid(_dot(h, wg_ref[...]) + bg_ref[...])
    g0_ref[...] = gate[:, :d].astype(BF16)
    gyb_ref[...] = (gate[:, d:] * y_b).astype(BF16)


def _mixer_in(x, norm_g, w_in, w_gate, b_gate, pool_mix, pool_scale, w_pool_up, ts):
    b, s, d = x.shape
    aw = N_HEADS * HEAD_W
    pw = len(POOL_WINDOWS) * POOL_GROUP
    n = b * s
    nt = s // ts
    const = lambda *shape: pl.BlockSpec(shape, lambda bi, j: (0,) * len(shape))
    tok3 = pl.BlockSpec((1, ts, aw), lambda bi, j: (bi, j, 0))
    tok2 = pl.BlockSpec((ts, d), lambda bi, j: (bi * nt + j, 0))
    return pl.pallas_call(
        _mixer_in_kernel,
        grid=(b, nt),
        in_specs=[
            pl.BlockSpec((1, ts, d), lambda bi, j: (bi, j, 0)),
            const(1, d),
            pl.BlockSpec((d, aw), lambda bi, j: (0, 0)),
            pl.BlockSpec((d, aw), lambda bi, j: (0, 1)),
            pl.BlockSpec((d, aw), lambda bi, j: (0, 2)),
            pl.BlockSpec((d, pw), lambda bi, j: (0, 3 * aw // pw)),
            const(d, 2 * d),
            const(1, 2 * d),
            const(len(POOL_WINDOWS), POOL_GROUP, POOL_GROUP),
            const(1, pw),
            const(pw, d),
        ],
        out_specs=[tok3, tok3, tok3, tok2, tok2],
        out_shape=[jax.ShapeDtypeStruct((b, s, aw), BF16)] * 3
        + [jax.ShapeDtypeStruct((n, d), BF16)] * 2,
        scratch_shapes=[pltpu.VMEM((ts + POOL_PAD, pw), F32)],
        compiler_params=_cparams(2),
        name="mixer_in",
    )(x, norm_g, w_in, w_in, w_in, w_in, w_gate, b_gate, pool_mix, pool_scale, w_pool_up)


def _rel_bucket(rel, log=jnp.log, f32=lambda a: a.astype(jnp.float32),
                i32=lambda a: a.astype(jnp.int32), xp=jnp):
    nb = N_BUCKETS // 2
    ret = i32(rel > 0) * nb
    n = xp.abs(rel)
    max_exact = nb // 2
    nf = f32(xp.maximum(n, 1))
    large = max_exact + i32(log(nf / max_exact) / math.log(MAX_DISTANCE / max_exact)
                            * (nb - max_exact))
    large = xp.minimum(large, nb - 1)
    return ret + xp.where(n < max_exact, n, large)


def _far_bucket(block, seq):
    rel = -np.arange(block + 1, max(seq, block + 2), dtype=np.int32)
    bk = _rel_bucket(rel, log=np.log, f32=lambda a: a.astype(np.float32),
                     i32=lambda a: a.astype(np.int32), xp=np)
    assert (bk == bk[0]).all(), "far keys must share one relative-position bucket"
    return int(bk[0])


def _rel_bias_kernel(far_bucket, tab_ref, bidx_ref, lq1_ref, lk1_ref, lq2_ref, lk2_ref,
                     bias_ref, lam_ref):
    h = pl.program_id(0)
    bidx = bidx_ref[...]
    acc = jnp.zeros(bidx.shape, F32)
    for bkt in range(N_BUCKETS):
        acc = jnp.where(bidx == bkt, tab_ref[bkt, h], acc)
    acc = acc - tab_ref[far_bucket, h]
    bias_ref[0] = jnp.where(bidx < 0, -jnp.inf, acc)
    lam = (jnp.exp(jnp.sum(lq1_ref[...] * lk1_ref[...], keepdims=True))
           - jnp.exp(jnp.sum(lq2_ref[...] * lk2_ref[...], keepdims=True)) + LAMBDA_INIT)
    lam_ref[...] = jnp.broadcast_to(lam, lam_ref.shape)


def _rel_bias(rel_bias, lq1, lk1, lq2, lk2, seq):
    blk = ATTN_BLOCK
    qpos = jnp.arange(blk, dtype=I32)[:, None]
    kpos = jnp.arange(blk, dtype=I32)[None, :]
    diag = jnp.where(kpos // CHUNK <= qpos // CHUNK, _rel_bucket(kpos - qpos), -1)
    prev = _rel_bucket(kpos - (qpos + blk))
    bidx = jnp.stack([diag, prev]).astype(I32)
    vec = pl.BlockSpec((1, HEAD_DIM), lambda h: (0, 0))
    return pl.pallas_call(
        functools.partial(_rel_bias_kernel, _far_bucket(blk, seq)),
        grid=(N_HEADS,),
        in_specs=[
            pl.BlockSpec(memory_space=pltpu.SMEM),
            pl.BlockSpec((2, blk, blk), lambda h: (0, 0, 0)),
            vec, vec, vec, vec,
        ],
        out_specs=[
            pl.BlockSpec((1, 2, blk, blk), lambda h: (h, 0, 0, 0)),
            pl.BlockSpec((8, 128), lambda h: (0, 0)),
        ],
        out_shape=[jax.ShapeDtypeStruct((N_HEADS, 2, blk, blk), F32),
                   jax.ShapeDtypeStruct((8, 128), F32)],
        compiler_params=_cparams(1),
        name="rel_bias",
    )(rel_bias, bidx, lq1, lk1, lq2, lk2)


def _diff_attn_kernel(q_ref, k_ref, v_ref, bias_ref, lam_ref, sg_ref, o_ref, vext_ref):
    s_len = q_ref.shape[1]
    tq = ATTN_BLOCK
    n_heads = q_ref.shape[2] // HEAD_W
    lam = lam_ref[0:1, 0:1]
    lane = lax.broadcasted_iota(I32, (tq, HEAD_W), 1)
    cols = [slice(hh * HEAD_W, (hh + 1) * HEAD_W) for hh in range(n_heads)]
    b_diag, b_prev = [], []
    for hh in range(n_heads):
        vext_ref[hh, :, :HEAD_W] = v_ref[0, :, cols[hh]]
        vext_ref[hh, :, HEAD_W:] = jnp.ones((s_len, HEAD_W), BF16)
        b_diag.append(jnp.concatenate([bias_ref[hh, 0]] * 2, axis=0))
        b_prev.append(jnp.concatenate([bias_ref[hh, 1]] * 2, axis=0))

    for qi in range(s_len // tq):
        n_keys = (qi + 1) * tq
        rows = slice(qi * tq, (qi + 1) * tq)
        for hh in range(n_heads):
            q = q_ref[0, rows, cols[hh]]
            zero = jnp.zeros_like(q)
            qs = jnp.concatenate([jnp.where(lane < HEAD_DIM, q, zero),
                                  jnp.where(lane >= HEAD_DIM, q, zero)], axis=0)
            s = _dot_nt(qs, k_ref[0, :n_keys, cols[hh]])
            pieces = []
            if qi >= 2:
                pieces.append(s[:, :n_keys - 2 * tq])
            if qi >= 1:
                pieces.append(s[:, n_keys - 2 * tq:n_keys - tq] + b_prev[hh])
            pieces.append(s[:, n_keys - tq:] + b_diag[hh])
            s = jnp.concatenate(pieces, axis=1) if len(pieces) > 1 else pieces[0]
            m = jnp.max(s, axis=-1, keepdims=True)
            p = jnp.exp(s - m).astype(BF16)
            acc = _dot(p, vext_ref[hh, :n_keys, :])
            o = acc[:, :HEAD_W] / acc[:, HEAD_W:HEAD_W + 1]
            a = o[:tq] - lam * o[tq:]
            o_ref[0, rows, cols[hh]] = (
                _rms(a, sg_ref[...]) * (1.0 - LAMBDA_INIT)).astype(BF16)


def _diff_attn(q, k, v, bias, lam, subln_g, heads_per_step):
    b, s, aw = q.shape
    tq = ATTN_BLOCK
    hp = heads_per_step
    spec = pl.BlockSpec((1, s, hp * HEAD_W), lambda bi, h: (bi, 0, h))
    return pl.pallas_call(
        _diff_attn_kernel,
        grid=(b, N_HEADS // hp),
        in_specs=[
            spec, spec, spec,
            pl.BlockSpec((hp, 2, tq, tq), lambda bi, h: (h, 0, 0, 0)),
            pl.BlockSpec((8, 128), lambda bi, h: (0, 0)),
            pl.BlockSpec((1, HEAD_W), lambda bi, h: (0, 0)),
        ],
        out_specs=spec,
        out_shape=jax.ShapeDtypeStruct((b, s, aw), BF16),
        scratch_shapes=[pltpu.VMEM((hp, s, 2 * HEAD_W), BF16)],
        compiler_params=_cparams(2),
        name="diff_attn",
    )(q, k, v, bias, lam, subln_g)


def _mem_kv_kernel(m_ref, g_ref, w_ref, k_ref, v_ref):
    d = m_ref.shape[2]
    m = _rms(m_ref[0], g_ref[...]).astype(BF16)
    kv = _dot(m, w_ref[...])
    k_ref[0] = kv[:, :d].astype(BF16)
    v_ref[0] = kv[:, d:].astype(BF16)


def _mem_kv(mem, norm_g, w_xkv):
    b, ml, d = mem.shape
    blk = pl.BlockSpec((1, ml, d), lambda bi: (bi, 0, 0))
    return pl.pallas_call(
        _mem_kv_kernel,
        grid=(b,),
        in_specs=[blk, pl.BlockSpec((1, d), lambda bi: (0, 0)),
                  pl.BlockSpec((d, 2 * d), lambda bi: (0, 0))],
        out_specs=[blk, blk],
        out_shape=[jax.ShapeDtypeStruct((b, ml, d), BF16)] * 2,
        compiler_params=_cparams(1),
        name="mem_kv",
    )(mem, norm_g, w_xkv)


def _post_kernel(x_ref, a_ref, g0_ref, gyb_ref, wau_ref, wo_ref, nxg_ref, wxq_ref,
                 km_ref, vm_ref, wxo_ref, nfg_ref, wrt_ref, br_ref,
                 x2_ref, hf_ref, eidx_ref, rank_ref, gate_ref, cnt_ref, run_ref):
    ts, d = x_ref.shape
    first = (pl.program_id(0) == 0) & (pl.program_id(1) == 0)

    y_a = _dot(a_ref[...], wau_ref[...])
    merged = g0_ref[...].astype(F32) * y_a + gyb_ref[...].astype(F32)
    x1 = x_ref[...] + _dot(merged.astype(BF16), wo_ref[...])

    hd = d // X_HEADS
    hx = _rms(x1, nxg_ref[...]).astype(BF16)
    qx = (_dot(hx, wxq_ref[...]) * (hd ** -0.5)).astype(BF16)
    heads = []
    for hh in range(X_HEADS):
        sl = slice(hh * hd, (hh + 1) * hd)
        s = _dot_nt(qx[:, sl], km_ref[0, :, sl])
        p = jnp.exp(s - jnp.max(s, axis=-1, keepdims=True))
        p = p / jnp.sum(p, axis=-1, keepdims=True)
        heads.append(_dot(p.astype(BF16), vm_ref[0, :, sl]))
    o = jnp.concatenate(heads, axis=1).astype(BF16)
    x2 = x1 + _dot(o, wxo_ref[...])
    x2_ref[...] = x2
    hf = _rms(x2, nfg_ref[...])
    hf_ref[...] = _pack_halves(hf)

    logits = _dot_nt(wrt_ref[...], hf, precision=lax.Precision.HIGHEST) + br_ref[...]
    ne = logits.shape[0]
    eid = lax.broadcasted_iota(I32, logits.shape, 0).astype(F32)
    work = logits
    vals, idxs, hots = [], [], []
    for _ in range(TOP_K):
        mx = jnp.max(work, axis=0, keepdims=True)
        idx = jnp.min(jnp.where(work == mx, eid, float(ne)), axis=0, keepdims=True)
        hot = eid == idx
        vals.append(mx)
        idxs.append(idx.astype(I32))
        hots.append(hot)
        work = jnp.where(hot, -jnp.inf, work)
    ex = [jnp.exp(vv - vals[0]) for vv in vals]
    den = ex[0] + ex[1] + ex[2] + ex[3]
    gates = [e_ / den for e_ in ex]

    multi = (hots[0] | hots[1] | hots[2] | hots[3])
    multi_f = jnp.where(multi, 1.0, 0.0).astype(F32)
    tri = (lax.broadcasted_iota(I32, (ts, ts), 0)
           < lax.broadcasted_iota(I32, (ts, ts), 1))
    before = _dot(multi_f.astype(BF16), jnp.where(tri, 1.0, 0.0).astype(BF16))

    @pl.when(first)
    def _():
        run_ref[...] = jnp.zeros(run_ref.shape, F32)

    run = run_ref[...]
    pos = before + run
    ranks = [jnp.sum(jnp.where(hot, pos, 0.0), axis=0, keepdims=True) for hot in hots]
    run_new = run + jnp.sum(multi_f, axis=1, keepdims=True)
    run_ref[...] = run_new
    cnt_ref[...] = jnp.broadcast_to(run_new, cnt_ref.shape).astype(I32)

    eidx_ref[...] = jnp.concatenate(idxs, axis=0)
    rank_ref[...] = jnp.concatenate(ranks, axis=0).astype(I32)
    g_rows = jnp.concatenate(gates + [jnp.zeros((128 - TOP_K, ts), F32)], axis=0)
    gate_ref[...] = g_rows.T


def _post(x2d, a2d, g0, gyb, w_attn_up, w_out, norm_x_g, w_xq, k_mem, v_mem, w_xo,
          norm_ffn_g, w_router_t, b_router, batch, ts):
    n, d = x2d.shape
    nt = n // batch // ts
    ml = k_mem.shape[1]
    tok = pl.BlockSpec((ts, d), lambda bi, j: (bi * nt + j, 0))
    lanes = pl.BlockSpec((TOP_K, ts), lambda bi, j: (0, bi * nt + j))
    const = lambda *shape: pl.BlockSpec(shape, lambda bi, j: (0,) * len(shape),
                                        pipeline_mode=pl.Buffered(1))
    mem = pl.BlockSpec((1, ml, d), lambda bi, j: (bi, 0, 0))
    return pl.pallas_call(
        _post_kernel,
        grid=(batch, nt),
        in_specs=[tok, tok, tok, tok, const(d, d), const(d, d), const(1, d), const(d, d),
                  mem, mem, const(d, d), const(1, d), const(N_EXPERTS, d),
                  const(N_EXPERTS, 1)],
        out_specs=[tok, pl.BlockSpec((ts, d // 2), lambda bi, j: (bi * nt + j, 0)),
                   lanes, lanes,
                   pl.BlockSpec((ts, 128), lambda bi, j: (bi * nt + j, 0)),
                   pl.BlockSpec((N_EXPERTS, 128), lambda bi, j: (0, 0))],
        out_shape=[jax.ShapeDtypeStruct((n, d), F32),
                   jax.ShapeDtypeStruct((n, d // 2), jnp.uint32),
                   jax.ShapeDtypeStruct((TOP_K, n), I32), jax.ShapeDtypeStruct((TOP_K, n), I32),
                   jax.ShapeDtypeStruct((n, 128), F32),
                   jax.ShapeDtypeStruct((N_EXPERTS, 128), I32)],
        scratch_shapes=[pltpu.VMEM((N_EXPERTS, 1), F32)],
        compiler_params=_cparams(2),
        name="post",
    )(x2d, a2d, g0, gyb, w_attn_up, w_out, norm_x_g, w_xq, k_mem, v_mem, w_xo,
      norm_ffn_g, w_router_t, b_router)


def _route_dest_kernel(pstart_ref, eidx_ref, rank_ref, dest_ref):
    eidx = eidx_ref[...]
    start = jnp.zeros(eidx.shape, I32)
    for e in range(N_EXPERTS):
        start = jnp.where(eidx == e, pstart_ref[e], start)
    dest_ref[...] = start + rank_ref[...]


def _route_dest(pstart, eidx, rank, tl):
    k, n = eidx.shape
    lanes = pl.BlockSpec((k, tl), lambda i: (0, i))
    return pl.pallas_call(
        _route_dest_kernel,
        grid=(n // tl,),
        in_specs=[pl.BlockSpec(memory_space=pltpu.SMEM), lanes, lanes],
        out_specs=lanes,
        out_shape=jax.ShapeDtypeStruct((k, n), I32),
        compiler_params=_cparams(1),
        name="route_dest",
    )(pstart, eidx, rank)


SC_CORES = 2
SC_SUBCORES = 16
SC_WORKERS = SC_CORES * SC_SUBCORES
MOVE_CHUNK = 64


def _worker_index_layout(dest, n):
    per_w = n // SC_WORKERS
    n_chunks = per_w // MOVE_CHUNK
    d4 = dest.reshape(TOP_K, SC_WORKERS, n_chunks, MOVE_CHUNK)
    return jnp.transpose(d4, (1, 2, 0, 3)).reshape(SC_WORKERS, n_chunks * TOP_K, MOVE_CHUNK)


def _sc_mesh():
    return plsc.VectorSubcoreMesh(core_axis_name="c", subcore_axis_name="s",
                                  num_cores=SC_CORES, num_subcores=SC_SUBCORES)


def _sc_dispatch(hf, dest_w, rows):
    n, d = hf.shape
    per_w = n // SC_WORKERS
    n_chunks = per_w // MOVE_CHUNK

    def body(hf_hbm, dest_hbm, xs_hbm, idx_v, rows_v, sem):
        wid = lax.axis_index("s") * SC_CORES + lax.axis_index("c")
        base = wid * per_w
        pltpu.sync_copy(dest_hbm.at[wid], idx_v)
        for c in range(n_chunks):
            pltpu.sync_copy(hf_hbm.at[pl.ds(base + c * MOVE_CHUNK, MOVE_CHUNK)], rows_v)
            copies = [pltpu.async_copy(rows_v, xs_hbm.at[idx_v.at[c * TOP_K + kk]], sem)
                      for kk in range(TOP_K)]
            for cp in copies:
                cp.wait()

    return pl.kernel(
        body,
        out_type=jax.ShapeDtypeStruct((rows, d), hf.dtype),
        mesh=_sc_mesh(),
        scratch_types=[pltpu.VMEM((n_chunks * TOP_K, MOVE_CHUNK), I32),
                       pltpu.VMEM((MOVE_CHUNK, d), hf.dtype),
                       pltpu.SemaphoreType.DMA],
        name="sc_dispatch",
    )(hf, dest_w)


def _sc_gather(ys, dest_w, n):
    _, d = ys.shape
    per_w = n // SC_WORKERS
    n_chunks = per_w // MOVE_CHUNK

    def body(ys_hbm, dest_hbm, yg_hbm, idx_v, rows_v, sem):
        wid = lax.axis_index("s") * SC_CORES + lax.axis_index("c")
        base = wid * per_w
        pltpu.sync_copy(dest_hbm.at[wid], idx_v)
        for c in range(n_chunks):
            for kk in range(TOP_K):
                pltpu.async_copy(ys_hbm.at[idx_v.at[c * TOP_K + kk]], rows_v, sem).wait()
                pltpu.sync_copy(
                    rows_v, yg_hbm.at[pl.ds(kk * n + base + c * MOVE_CHUNK, MOVE_CHUNK)])

    return pl.kernel(
        body,
        out_type=jax.ShapeDtypeStruct((TOP_K * n, d), ys.dtype),
        mesh=_sc_mesh(),
        scratch_types=[pltpu.VMEM((n_chunks * TOP_K, MOVE_CHUNK), I32),
                       pltpu.VMEM((MOVE_CHUNK, d), ys.dtype),
                       pltpu.SemaphoreType.DMA],
        name="sc_gather",
    )(ys, dest_w)


def _combine_kernel(x2_ref, gate_ref, fg_ref, yg_ref, o_ref):
    g = gate_ref[...]
    x2 = x2_ref[...]
    c = x2.shape[1] // 2
    acc_hi, acc_lo = x2[:, :c], x2[:, c:]
    for kk in range(TOP_K):
        hi, lo = _unpack_halves(yg_ref[kk])
        acc_hi = acc_hi + g[:, kk:kk + 1] * hi
        acc_lo = acc_lo + g[:, kk:kk + 1] * lo
    o_ref[...] = _rms(jnp.concatenate([acc_hi, acc_lo], axis=1), fg_ref[...])


def _combine(x2, gate_tm, final_g, yg, ts):
    n, d = x2.shape
    dw = yg.shape[1]
    return pl.pallas_call(
        _combine_kernel,
        grid=(n // ts,),
        in_specs=[pl.BlockSpec((ts, d), lambda i: (i, 0)),
                  pl.BlockSpec((ts, 128), lambda i: (i, 0)),
                  pl.BlockSpec((1, d), lambda i: (0, 0)),
                  pl.BlockSpec((TOP_K, ts, dw), lambda i: (0, i, 0))],
        out_specs=pl.BlockSpec((ts, d), lambda i: (i, 0)),
        out_shape=jax.ShapeDtypeStruct((n, d), F32),
        compiler_params=_cparams(1),
        name="combine",
    )(x2, gate_tm, final_g, yg.reshape(TOP_K, n, dw))


def _experts_kernel(be_ref, first_ref, nxt_ref, slot_ref, nb_ref,
                    xs_ref, bgu_ref, bd_ref, wgu_hbm, wd_hbm, ys_ref,
                    wgu_st, wd_st, wgu_bf, wd_bf, sem):
    i = pl.program_id(0)
    dff = wd_bf.shape[0]
    e = be_ref[i]

    def fetch(ex, sl):
        return (pltpu.make_async_copy(wgu_hbm.at[ex], wgu_st.at[sl], sem.at[0, sl]),
                pltpu.make_async_copy(wd_hbm.at[ex], wd_st.at[sl], sem.at[1, sl]))

    @pl.when(i == 0)
    def _():
        for cp in fetch(e, slot_ref[e]):
            cp.start()

    @pl.when(first_ref[i] == 1)
    def _():
        sl = slot_ref[e]
        for cp in fetch(e, sl):
            cp.wait()
        nx = nxt_ref[e]

        @pl.when(nx >= 0)
        def _():
            for cp in fetch(nx, 1 - sl):
                cp.start()

        wgu_bf[...] = wgu_st[sl].astype(BF16)
        wd_bf[...] = wd_st[sl].astype(BF16)

    @pl.when(i < nb_ref[0])
    def _():
        xb = jnp.concatenate(_unpack_halves(xs_ref[...]), axis=1).astype(BF16)
        gu = _dot(xb, wgu_bf[...]) + bgu_ref[0]
        x_glu = jnp.minimum(gu[:, :dff], SWIGLU_LIMIT)
        x_lin = jnp.clip(gu[:, dff:], -SWIGLU_LIMIT, SWIGLU_LIMIT)
        act = x_glu * jax.nn.sigmoid(SWIGLU_ALPHA * x_glu) * (x_lin + 1.0)
        ys_ref[...] = _pack_halves(_dot(act.astype(BF16), wd_bf[...]) + bd_ref[0])

    @pl.when(i >= nb_ref[0])
    def _():
        ys_ref[...] = jnp.zeros(ys_ref.shape, jnp.uint32)


def _expert_schedule(counts, bm, n_blocks):
    ne = counts.shape[0]
    padded = (counts + bm - 1) // bm * bm
    pend = jnp.cumsum(padded)
    pstart = (pend - padded).astype(I32)
    nb_used = (pend[-1] // bm).astype(I32)
    blk = jnp.arange(n_blocks, dtype=I32)
    blk_c = jnp.minimum(blk, nb_used - 1)
    be = jnp.minimum(jnp.sum((pend[None, :] <= (blk_c * bm)[:, None]).astype(I32), axis=1),
                     ne - 1)
    first = ((blk < nb_used) & (blk * bm == pstart[be])).astype(I32)
    used = counts > 0
    seq = jnp.cumsum(used.astype(I32)) - 1
    ids = jnp.arange(ne, dtype=I32)
    later = used[None, :] & (ids[None, :] > ids[:, None])
    nxt = jnp.where(jnp.any(later, axis=1), jnp.argmax(later, axis=1), -1).astype(I32)
    slot = (seq & 1).astype(I32)
    return pstart, nb_used.reshape(1), be, first, nxt, slot


def _experts(be, first, nxt, slot, nb_used, xs, w_gu, b_gu, w_down, b_down, bm):
    rows, dw = xs.shape
    ne, d, dff2 = w_gu.shape
    dff = dff2 // 2
    last = lambda i, be, fi, nx, sl, nb: (jnp.maximum(jnp.minimum(i, nb[0] - 1), 0), 0)
    grid_spec = pltpu.PrefetchScalarGridSpec(
        num_scalar_prefetch=5,
        grid=(rows // bm,),
        in_specs=[
            pl.BlockSpec((bm, dw), last),
            pl.BlockSpec((1, 1, dff2), lambda i, be, fi, nx, sl, nb: (be[i], 0, 0)),
            pl.BlockSpec((1, 1, d), lambda i, be, fi, nx, sl, nb: (be[i], 0, 0)),
            pl.BlockSpec(memory_space=pl.ANY),
            pl.BlockSpec(memory_space=pl.ANY),
        ],
        out_specs=pl.BlockSpec((bm, dw), lambda i, be, fi, nx, sl, nb: (i, 0)),
        scratch_shapes=[pltpu.VMEM((2, d, dff2), F32), pltpu.VMEM((2, dff, d), F32),
                        pltpu.VMEM((d, dff2), BF16), pltpu.VMEM((dff, d), BF16),
                        pltpu.SemaphoreType.DMA((2, 2))],
    )
    return pl.pallas_call(
        _experts_kernel,
        grid_spec=grid_spec,
        out_shape=jax.ShapeDtypeStruct((rows, dw), jnp.uint32),
        compiler_params=_cparams(1),
        name="experts",
    )(be, first, nxt, slot, nb_used, xs, b_gu.reshape(ne, 1, dff2),
      b_down.reshape(ne, 1, d), w_gu, w_down)


MIXER_TILE = 512
ATTN_HEADS_PER_STEP = 2
POST_TILE = 512
DEST_TILE = 2048
COMBINE_TILE = 512
EXPERT_BLOCK = 256


def kernel(x, mem, norm_mix_g, w_in, lambda_q1, lambda_k1, lambda_q2, lambda_k2, rel_bias,
           subln_g, w_attn_up, pool_mix, pool_scale, w_pool_up, w_gate, b_gate, w_out,
           norm_x_g, norm_mem_g, w_xq, w_xkv, w_xo, norm_ffn_g, w_router, b_router,
           w_gu, b_gu, w_down, b_down, final_norm_g):
    b, s, d = x.shape
    n = b * s
    assert w_in.shape[0] == 1, "single-layer block"
    row = lambda a: a.reshape(1, -1)
    bf = lambda a: a[0].astype(BF16)

    q, k, v, g0, gyb = _mixer_in(
        x, row(norm_mix_g[0]), bf(w_in), bf(w_gate), row(b_gate[0]), bf(pool_mix),
        row(pool_scale[0]), bf(w_pool_up), MIXER_TILE)

    bias, lam = _rel_bias(rel_bias, row(lambda_q1[0]), row(lambda_k1[0]),
                          row(lambda_q2[0]), row(lambda_k2[0]), s)
    a = _diff_attn(q, k, v, bias, lam, row(subln_g[0]), ATTN_HEADS_PER_STEP)

    k_mem, v_mem = _mem_kv(mem, row(norm_mem_g[0]), bf(w_xkv))

    x2, hf, eidx, rank, gate_tm, counts = _post(
        x.reshape(n, d), a.reshape(n, d), g0, gyb, bf(w_attn_up), bf(w_out),
        row(norm_x_g[0]), bf(w_xq), k_mem, v_mem, bf(w_xo), row(norm_ffn_g[0]),
        w_router[0].T, b_router[0].reshape(-1, 1), b, POST_TILE)

    bm = EXPERT_BLOCK
    rows = n * TOP_K + N_EXPERTS * bm
    pstart, nb_used, be, first, nxt, slot = _expert_schedule(counts[:, 0], bm, rows // bm)

    dest_w = _worker_index_layout(_route_dest(pstart, eidx, rank, DEST_TILE), n)
    xs = _sc_dispatch(hf, dest_w, rows)
    ys = _experts(be, first, nxt, slot, nb_used, xs, w_gu[0], b_gu[0], w_down[0], b_down[0],
                  bm)
    yg = _sc_gather(ys, dest_w, n)
    out = _combine(x2, gate_tm, row(final_norm_g), yg, COMBINE_TILE)
    return out.reshape(b, s, d)
```

```python
import functools
import math

import numpy as np
import jax
import jax.numpy as jnp
from jax import lax
from jax.experimental import pallas as pl
from jax.experimental.pallas import tpu as pltpu
from jax.experimental.pallas import tpu_sc as plsc

F32 = jnp.float32
BF16 = jnp.bfloat16
I32 = jnp.int32

EPS = 1e-6
CHUNK = 64
N_HEADS = 8
HEAD_DIM = 64
HEAD_W = 2 * HEAD_DIM
POOL_WINDOWS = (2, 4, 8, 16)
POOL_GROUP = 128
POOL_PAD = 16
N_BUCKETS = 32
MAX_DISTANCE = 128
X_HEADS = 4
N_EXPERTS = 32
TOP_K = 4
SWIGLU_ALPHA = 1.702
SWIGLU_LIMIT = 7.0
LAMBDA_INIT = 0.8 - 0.6 * math.exp(-0.3 * 0)

ATTN_BLOCK = 256
VMEM_LIMIT = 56 * 1024 * 1024


def _cparams(n_axes, vmem=VMEM_LIMIT):
    return pltpu.CompilerParams(
        dimension_semantics=("arbitrary",) * n_axes, vmem_limit_bytes=vmem)


def _rms(xf, g):
    ms = jnp.mean(xf * xf, axis=-1, keepdims=True)
    return xf * lax.rsqrt(ms + EPS) * g


def _dot(a, b):
    return jnp.dot(a, b, preferred_element_type=F32)


def _pack_halves(x):
    c = x.shape[1] // 2
    as_bits = lambda v: lax.bitcast_convert_type(v.astype(BF16).astype(F32), jnp.uint32)
    return as_bits(x[:, :c]) | (as_bits(x[:, c:]) >> 16)


def _unpack_halves(w):
    hi = lax.bitcast_convert_type(w & jnp.uint32(0xFFFF0000), F32)
    lo = lax.bitcast_convert_type(w << 16, F32)
    return hi, lo


def _dot_nt(a, b, precision=None):
    return lax.dot_general(a, b, (((1,), (1,)), ((), ())),
                           preferred_element_type=F32, precision=precision)


def _mixer_in_kernel(x_ref, g_ref, wq_ref, wk_ref, wv_ref, wu_ref, wg_ref, bg_ref,
                     pmix_ref, pscale_ref, wpu_ref,
                     q_ref, k_ref, v_ref, g0_ref, gyb_ref, ext_ref):
    ts = x_ref.shape[1]
    d = x_ref.shape[2]
    j = pl.program_id(1)
    h = _rms(x_ref[0], g_ref[...]).astype(BF16)
    q_ref[0] = (_dot(h, wq_ref[...]) * (HEAD_DIM ** -0.5)).astype(BF16)
    k_ref[0] = _dot(h, wk_ref[...]).astype(BF16)
    v_ref[0] = _dot(h, wv_ref[...]).astype(BF16)
    u = _dot(h, wu_ref[...])

    @pl.when(j == 0)
    def _():
        ext_ref[0:POOL_PAD, :] = jnp.zeros((POOL_PAD, u.shape[1]), F32)

    ext_ref[POOL_PAD:POOL_PAD + ts, :] = u
    e = ext_ref[...]
    sums = {}
    s = e
    w = 1
    while w < max(POOL_WINDOWS):
        s = s + pltpu.roll(s, w, 0)
        w *= 2
        sums[w] = s
    ext_ref[0:POOL_PAD, :] = ext_ref[ts:ts + POOL_PAD, :]

    pos = (j * ts + lax.broadcasted_iota(I32, (ts, 1), 0) + 1).astype(F32)
    mixed = []
    for gi, w in enumerate(POOL_WINDOWS):
        sl = slice(gi * POOL_GROUP, (gi + 1) * POOL_GROUP)
        win = sums[w][POOL_PAD:, sl]
        pooled = win / jnp.minimum(pos, float(w)) - u[:, sl]
        mixed.append(_dot(pooled.astype(BF16), pmix_ref[gi]) * pscale_ref[:, sl])
    mixed = jnp.concatenate(mixed, axis=1).astype(BF16)
    y_b = _dot(mixed, wpu_ref[...])

    gate = jax.nn.sigmoid(_dot(h, wg_ref[...]) + bg_ref[...])
    g0_ref[...] = gate[:, :d].astype(BF16)
    gyb_ref[...] = (gate[:, d:] * y_b).astype(BF16)


def _mixer_in(x, norm_g, w_in, w_gate, b_gate, pool_mix, pool_scale, w_pool_up, ts):
    b, s, d = x.shape
    aw = N_HEADS * HEAD_W
    pw = len(POOL_WINDOWS) * POOL_GROUP
    n = b * s
    nt = s // ts
    const = lambda *shape: pl.BlockSpec(shape, lambda bi, j: (0,) * len(shape))
    tok3 = pl.BlockSpec((1, ts, aw), lambda bi, j: (bi, j, 0))
    tok2 = pl.BlockSpec((ts, d), lambda bi, j: (bi * nt + j, 0))
    return pl.pallas_call(
        _mixer_in_kernel,
        grid=(b, nt),
        in_specs=[
            pl.BlockSpec((1, ts, d), lambda bi, j: (bi, j, 0)),
            const(1, d),
            pl.BlockSpec((d, aw), lambda bi, j: (0, 0)),
            pl.BlockSpec((d, aw), lambda bi, j: (0, 1)),
            pl.BlockSpec((d, aw), lambda bi, j: (0, 2)),
            pl.BlockSpec((d, pw), lambda bi, j: (0, 3 * aw // pw)),
            const(d, 2 * d),
            const(1, 2 * d),
            const(len(POOL_WINDOWS), POOL_GROUP, POOL_GROUP),
            const(1, pw),
            const(pw, d),
        ],
        out_specs=[tok3, tok3, tok3, tok2, tok2],
        out_shape=[jax.ShapeDtypeStruct((b, s, aw), BF16)] * 3
        + [jax.ShapeDtypeStruct((n, d), BF16)] * 2,
        scratch_shapes=[pltpu.VMEM((ts + POOL_PAD, pw), F32)],
        compiler_params=_cparams(2),
        name="mixer_in",
    )(x, norm_g, w_in, w_in, w_in, w_in, w_gate, b_gate, pool_mix, pool_scale, w_pool_up)


def _rel_bucket(rel, log=jnp.log, f32=lambda a: a.astype(jnp.float32),
                i32=lambda a: a.astype(jnp.int32), xp=jnp):
    nb = N_BUCKETS // 2
    ret = i32(rel > 0) * nb
    n = xp.abs(rel)
    max_exact = nb // 2
    nf = f32(xp.maximum(n, 1))
    large = max_exact + i32(log(nf / max_exact) / math.log(MAX_DISTANCE / max_exact)
                            * (nb - max_exact))
    large = xp.minimum(large, nb - 1)
    return ret + xp.where(n < max_exact, n, large)


def _far_bucket(block, seq):
    rel = -np.arange(block + 1, max(seq, block + 2), dtype=np.int32)
    bk = _rel_bucket(rel, log=np.log, f32=lambda a: a.astype(np.float32),
                     i32=lambda a: a.astype(np.int32), xp=np)
    assert (bk == bk[0]).all(), "far keys must share one relative-position bucket"
    return int(bk[0])


def _rel_bias_kernel(far_bucket, tab_ref, bidx_ref, lq1_ref, lk1_ref, lq2_ref, lk2_ref,
                     bias_ref, lam_ref):
    h = pl.program_id(0)
    bidx = bidx_ref[...]
    acc = jnp.zeros(bidx.shape, F32)
    for bkt in range(N_BUCKETS):
        acc = jnp.where(bidx == bkt, tab_ref[bkt, h], acc)
    acc = acc - tab_ref[far_bucket, h]
    bias_ref[0] = jnp.where(bidx < 0, -jnp.inf, acc)
    lam = (jnp.exp(jnp.sum(lq1_ref[...] * lk1_ref[...], keepdims=True))
           - jnp.exp(jnp.sum(lq2_ref[...] * lk2_ref[...], keepdims=True)) + LAMBDA_INIT)
    lam_ref[...] = jnp.broadcast_to(lam, lam_ref.shape)


def _rel_bias(rel_bias, lq1, lk1, lq2, lk2, seq):
    blk = ATTN_BLOCK
    qpos = jnp.arange(blk, dtype=I32)[:, None]
    kpos = jnp.arange(blk, dtype=I32)[None, :]
    diag = jnp.where(kpos // CHUNK <= qpos // CHUNK, _rel_bucket(kpos - qpos), -1)
    prev = _rel_bucket(kpos - (qpos + blk))
    bidx = jnp.stack([diag, prev]).astype(I32)
    vec = pl.BlockSpec((1, HEAD_DIM), lambda h: (0, 0))
    return pl.pallas_call(
        functools.partial(_rel_bias_kernel, _far_bucket(blk, seq)),
        grid=(N_HEADS,),
        in_specs=[
            pl.BlockSpec(memory_space=pltpu.SMEM),
            pl.BlockSpec((2, blk, blk), lambda h: (0, 0, 0)),
            vec, vec, vec, vec,
        ],
        out_specs=[
            pl.BlockSpec((1, 2, blk, blk), lambda h: (h, 0, 0, 0)),
            pl.BlockSpec((8, 128), lambda h: (0, 0)),
        ],
        out_shape=[jax.ShapeDtypeStruct((N_HEADS, 2, blk, blk), F32),
                   jax.ShapeDtypeStruct((8, 128), F32)],
        compiler_params=_cparams(1),
        name="rel_bias",
    )(rel_bias, bidx, lq1, lk1, lq2, lk2)


def _diff_attn_kernel(q_ref, k_ref, v_ref, bias_ref, lam_ref, sg_ref, o_ref, vext_ref):
    s_len = q_ref.shape[1]
    tq = ATTN_BLOCK
    n_heads = q_ref.shape[2] // HEAD_W
    lam = lam_ref[0:1, 0:1]
    lane = lax.broadcasted_iota(I32, (tq, HEAD_W), 1)
    cols = [slice(hh * HEAD_W, (hh + 1) * HEAD_W) for hh in range(n_heads)]
    b_diag, b_prev = [], []
    for hh in range(n_heads):
        vext_ref[hh, :, :HEAD_W] = v_ref[0, :, cols[hh]]
        vext_ref[hh, :, HEAD_W:] = jnp.ones((s_len, HEAD_W), BF16)
        b_diag.append(jnp.concatenate([bias_ref[hh, 0]] * 2, axis=0))
        b_prev.append(jnp.concatenate([bias_ref[hh, 1]] * 2, axis=0))

    for qi in range(s_len // tq):
        n_keys = (qi + 1) * tq
        rows = slice(qi * tq, (qi + 1) * tq)
        for hh in range(n_heads):
            q = q_ref[0, rows, cols[hh]]
            zero = jnp.zeros_like(q)
            qs = jnp.concatenate([jnp.where(lane < HEAD_DIM, q, zero),
                                  jnp.where(lane >= HEAD_DIM, q, zero)], axis=0)
            s = _dot_nt(qs, k_ref[0, :n_keys, cols[hh]])
            pieces = []
            if qi >= 2:
                pieces.append(s[:, :n_keys - 2 * tq])
            if qi >= 1:
                pieces.append(s[:, n_keys - 2 * tq:n_keys - tq] + b_prev[hh])
            pieces.append(s[:, n_keys - tq:] + b_diag[hh])
            s = jnp.concatenate(pieces, axis=1) if len(pieces) > 1 else pieces[0]
            m = jnp.max(s, axis=-1, keepdims=True)
            p = jnp.exp(s - m).astype(BF16)
            acc = _dot(p, vext_ref[hh, :n_keys, :])
            o = acc[:, :HEAD_W] / acc[:, HEAD_W:HEAD_W + 1]
            a = o[:tq] - lam * o[tq:]
            o_ref[0, rows, cols[hh]] = (
                _rms(a, sg_ref[...]) * (1.0 - LAMBDA_INIT)).astype(BF16)


def _diff_attn(q, k, v, bias, lam, subln_g, heads_per_step):
    b, s, aw = q.shape
    tq = ATTN_BLOCK
    hp = heads_per_step
    spec = pl.BlockSpec((1, s, hp * HEAD_W), lambda bi, h: (bi, 0, h))
    return pl.pallas_call(
        _diff_attn_kernel,
        grid=(b, N_HEADS // hp),
        in_specs=[
            spec, spec, spec,
            pl.BlockSpec((hp, 2, tq, tq), lambda bi, h: (h, 0, 0, 0)),
            pl.BlockSpec((8, 128), lambda bi, h: (0, 0)),
            pl.BlockSpec((1, HEAD_W), lambda bi, h: (0, 0)),
        ],
        out_specs=spec,
        out_shape=jax.ShapeDtypeStruct((b, s, aw), BF16),
        scratch_shapes=[pltpu.VMEM((hp, s, 2 * HEAD_W), BF16)],
        compiler_params=_cparams(2),
        name="diff_attn",
    )(q, k, v, bias, lam, subln_g)


def _mem_kv_kernel(m_ref, g_ref, w_ref, k_ref, v_ref):
    d = m_ref.shape[2]
    m = _rms(m_ref[0], g_ref[...]).astype(BF16)
    kv = _dot(m, w_ref[...])
    k_ref[0] = kv[:, :d].astype(BF16)
    v_ref[0] = kv[:, d:].astype(BF16)


def _mem_kv(mem, norm_g, w_xkv):
    b, ml, d = mem.shape
    blk = pl.BlockSpec((1, ml, d), lambda bi: (bi, 0, 0))
    return pl.pallas_call(
        _mem_kv_kernel,
        grid=(b,),
        in_specs=[blk, pl.BlockSpec((1, d), lambda bi: (0, 0)),
                  pl.BlockSpec((d, 2 * d), lambda bi: (0, 0))],
        out_specs=[blk, blk],
        out_shape=[jax.ShapeDtypeStruct((b, ml, d), BF16)] * 2,
        compiler_params=_cparams(1),
        name="mem_kv",
    )(mem, norm_g, w_xkv)


def _post_kernel(x_ref, a_ref, g0_ref, gyb_ref, wau_ref, wo_ref, nxg_ref, wxq_ref,
                 km_ref, vm_ref, wxo_ref, nfg_ref, wrt_ref, br_ref,
                 x2_ref, hf_ref, eidx_ref, rank_ref, gate_ref, cnt_ref, run_ref):
    ts, d = x_ref.shape
    first = (pl.program_id(0) == 0) & (pl.program_id(1) == 0)

    y_a = _dot(a_ref[...], wau_ref[...])
    merged = g0_ref[...].astype(F32) * y_a + gyb_ref[...].astype(F32)
    x1 = x_ref[...] + _dot(merged.astype(BF16), wo_ref[...])

    hd = d // X_HEADS
    hx = _rms(x1, nxg_ref[...]).astype(BF16)
    qx = (_dot(hx, wxq_ref[...]) * (hd ** -0.5)).astype(BF16)
    heads = []
    for hh in range(X_HEADS):
        sl = slice(hh * hd, (hh + 1) * hd)
        s = _dot_nt(qx[:, sl], km_ref[0, :, sl])
        p = jnp.exp(s - jnp.max(s, axis=-1, keepdims=True))
        p = p / jnp.sum(p, axis=-1, keepdims=True)
        heads.append(_dot(p.astype(BF16), vm_ref[0, :, sl]))
    o = jnp.concatenate(heads, axis=1).astype(BF16)
    x2 = x1 + _dot(o, wxo_ref[...])
    x2_ref[...] = x2
    hf = _rms(x2, nfg_ref[...])
    hf_ref[...] = _pack_halves(hf)

    logits = _dot_nt(wrt_ref[...], hf, precision=lax.Precision.HIGHEST) + br_ref[...]
    ne = logits.shape[0]
    eid = lax.broadcasted_iota(I32, logits.shape, 0).astype(F32)
    work = logits
    vals, idxs, hots = [], [], []
    for _ in range(TOP_K):
        mx = jnp.max(work, axis=0, keepdims=True)
        idx = jnp.min(jnp.where(work == mx, eid, float(ne)), axis=0, keepdims=True)
        hot = eid == idx
        vals.append(mx)
        idxs.append(idx.astype(I32))
        hots.append(hot)
        work = jnp.where(hot, -jnp.inf, work)
    ex = [jnp.exp(vv - vals[0]) for vv in vals]
    den = ex[0] + ex[1] + ex[2] + ex[3]
    gates = [e_ / den for e_ in ex]

    multi = (hots[0] | hots[1] | hots[2] | hots[3])
    multi_f = jnp.where(multi, 1.0, 0.0).astype(F32)
    tri = (lax.broadcasted_iota(I32, (ts, ts), 0)
           < lax.broadcasted_iota(I32, (ts, ts), 1))
    before = _dot(multi_f.astype(BF16), jnp.where(tri, 1.0, 0.0).astype(BF16))

    @pl.when(first)
    def _():
        run_ref[...] = jnp.zeros(run_ref.shape, F32)

    run = run_ref[...]
    pos = before + run
    ranks = [jnp.sum(jnp.where(hot, pos, 0.0), axis=0, keepdims=True) for hot in hots]
    run_new = run + jnp.sum(multi_f, axis=1, keepdims=True)
    run_ref[...] = run_new
    cnt_ref[...] = jnp.broadcast_to(run_new, cnt_ref.shape).astype(I32)

    eidx_ref[...] = jnp.concatenate(idxs, axis=0)
    rank_ref[...] = jnp.concatenate(ranks, axis=0).astype(I32)
    g_rows = jnp.concatenate(gates + [jnp.zeros((128 - TOP_K, ts), F32)], axis=0)
    gate_ref[...] = g_rows.T


def _post(x2d, a2d, g0, gyb, w_attn_up, w_out, norm_x_g, w_xq, k_mem, v_mem, w_xo,
          norm_ffn_g, w_router_t, b_router, batch, ts):
    n, d = x2d.shape
    nt = n // batch // ts
    ml = k_mem.shape[1]
    tok = pl.BlockSpec((ts, d), lambda bi, j: (bi * nt + j, 0))
    lanes = pl.BlockSpec((TOP_K, ts), lambda bi, j: (0, bi * nt + j))
    const = lambda *shape: pl.BlockSpec(shape, lambda bi, j: (0,) * len(shape),
                                        pipeline_mode=pl.Buffered(1))
    mem = pl.BlockSpec((1, ml, d), lambda bi, j: (bi, 0, 0))
    return pl.pallas_call(
        _post_kernel,
        grid=(batch, nt),
        in_specs=[tok, tok, tok, tok, const(d, d), const(d, d), const(1, d), const(d, d),
                  mem, mem, const(d, d), const(1, d), const(N_EXPERTS, d),
                  const(N_EXPERTS, 1)],
        out_specs=[tok, pl.BlockSpec((ts, d // 2), lambda bi, j: (bi * nt + j, 0)),
                   lanes, lanes,
                   pl.BlockSpec((ts, 128), lambda bi, j: (bi * nt + j, 0)),
                   pl.BlockSpec((N_EXPERTS, 128), lambda bi, j: (0, 0))],
        out_shape=[jax.ShapeDtypeStruct((n, d), F32),
                   jax.ShapeDtypeStruct((n, d // 2), jnp.uint32),
                   jax.ShapeDtypeStruct((TOP_K, n), I32), jax.ShapeDtypeStruct((TOP_K, n), I32),
                   jax.ShapeDtypeStruct((n, 128), F32),
                   jax.ShapeDtypeStruct((N_EXPERTS, 128), I32)],
        scratch_shapes=[pltpu.VMEM((N_EXPERTS, 1), F32)],
        compiler_params=_cparams(2),
        name="post",
    )(x2d, a2d, g0, gyb, w_attn_up, w_out, norm_x_g, w_xq, k_mem, v_mem, w_xo,
      norm_ffn_g, w_router_t, b_router)


def _route_dest_kernel(pstart_ref, eidx_ref, rank_ref, dest_ref):
    eidx = eidx_ref[...]
    start = jnp.zeros(eidx.shape, I32)
    for e in range(N_EXPERTS):
        start = jnp.where(eidx == e, pstart_ref[e], start)
    dest_ref[...] = start + rank_ref[...]


def _route_dest(pstart, eidx, rank, tl):
    k, n = eidx.shape
    lanes = pl.BlockSpec((k, tl), lambda i: (0, i))
    return pl.pallas_call(
        _route_dest_kernel,
        grid=(n // tl,),
        in_specs=[pl.BlockSpec(memory_space=pltpu.SMEM), lanes, lanes],
        out_specs=lanes,
        out_shape=jax.ShapeDtypeStruct((k, n), I32),
        compiler_params=_cparams(1),
        name="route_dest",
    )(pstart, eidx, rank)


SC_CORES = 2
SC_SUBCORES = 16
SC_WORKERS = SC_CORES * SC_SUBCORES
MOVE_CHUNK = 64
MOVE_BUFFERS = 3


def _worker_index_layout(dest, n):
    per_w = n // SC_WORKERS
    n_chunks = per_w // MOVE_CHUNK
    d4 = dest.reshape(TOP_K, SC_WORKERS, n_chunks, MOVE_CHUNK)
    return jnp.transpose(d4, (1, 2, 0, 3)).reshape(SC_WORKERS, n_chunks * TOP_K, MOVE_CHUNK)


def _sc_mesh():
    return plsc.VectorSubcoreMesh(core_axis_name="c", subcore_axis_name="s",
                                  num_cores=SC_CORES, num_subcores=SC_SUBCORES)


def _sc_dispatch(hf, dest_w, rows):
    n, d = hf.shape
    per_w = n // SC_WORKERS
    n_chunks = per_w // MOVE_CHUNK

    def body(hf_hbm, dest_hbm, xs_hbm, idx_v, rows_v, rsem, wsem):
        wid = lax.axis_index("s") * SC_CORES + lax.axis_index("c")
        base = wid * per_w
        pltpu.sync_copy(dest_hbm.at[wid], idx_v)

        def read(c):
            b = c % MOVE_BUFFERS
            return pltpu.async_copy(hf_hbm.at[pl.ds(base + c * MOVE_CHUNK, MOVE_CHUNK)],
                                    rows_v.at[b], rsem.at[b])

        scatters = [[] for _ in range(MOVE_BUFFERS)]
        pending = read(0)
        for c in range(n_chunks):
            b = c % MOVE_BUFFERS
            pending.wait()
            if c + 1 < n_chunks:
                for cp in scatters[(c + 1) % MOVE_BUFFERS]:
                    cp.wait()
                pending = read(c + 1)
            scatters[b] = [
                pltpu.async_copy(rows_v.at[b], xs_hbm.at[idx_v.at[c * TOP_K + kk]], wsem.at[b])
                for kk in range(TOP_K)]
        for group in scatters:
            for cp in group:
                cp.wait()

    return pl.kernel(
        body,
        out_type=jax.ShapeDtypeStruct((rows, d), hf.dtype),
        mesh=_sc_mesh(),
        scratch_types=[pltpu.VMEM((n_chunks * TOP_K, MOVE_CHUNK), I32),
                       pltpu.VMEM((MOVE_BUFFERS, MOVE_CHUNK, d), hf.dtype),
                       pltpu.SemaphoreType.DMA((MOVE_BUFFERS,)),
                       pltpu.SemaphoreType.DMA((MOVE_BUFFERS,))],
        name="sc_dispatch",
    )(hf, dest_w)


def _sc_gather(ys, dest_w, n):
    _, d = ys.shape
    per_w = n // SC_WORKERS
    n_chunks = per_w // MOVE_CHUNK

    def body(ys_hbm, dest_hbm, yg_hbm, idx_v, rows_v, rsem, wsem):
        wid = lax.axis_index("s") * SC_CORES + lax.axis_index("c")
        base = wid * per_w
        pltpu.sync_copy(dest_hbm.at[wid], idx_v)
        n_moves = n_chunks * TOP_K

        def gather(m):
            b = m % MOVE_BUFFERS
            return pltpu.async_copy(ys_hbm.at[idx_v.at[m]], rows_v.at[b], rsem.at[b])

        def write(m):
            b = m % MOVE_BUFFERS
            c, kk = divmod(m, TOP_K)
            return pltpu.async_copy(
                rows_v.at[b], yg_hbm.at[pl.ds(kk * n + base + c * MOVE_CHUNK, MOVE_CHUNK)],
                wsem.at[b])

        writes = [None] * MOVE_BUFFERS
        pending = gather(0)
        for m in range(n_moves):
            pending.wait()
            if m + 1 < n_moves:
                nb = (m + 1) % MOVE_BUFFERS
                if writes[nb] is not None:
                    writes[nb].wait()
                pending = gather(m + 1)
            writes[m % MOVE_BUFFERS] = write(m)
        for wr in writes:
            if wr is not None:
                wr.wait()

    return pl.kernel(
        body,
        out_type=jax.ShapeDtypeStruct((TOP_K * n, d), ys.dtype),
        mesh=_sc_mesh(),
        scratch_types=[pltpu.VMEM((n_chunks * TOP_K, MOVE_CHUNK), I32),
                       pltpu.VMEM((MOVE_BUFFERS, MOVE_CHUNK, d), ys.dtype),
                       pltpu.SemaphoreType.DMA((MOVE_BUFFERS,)),
                       pltpu.SemaphoreType.DMA((MOVE_BUFFERS,))],
        name="sc_gather",
    )(ys, dest_w)


def _combine_kernel(x2_ref, gate_ref, fg_ref, yg_ref, o_ref):
    g = gate_ref[...]
    x2 = x2_ref[...]
    c = x2.shape[1] // 2
    acc_hi, acc_lo = x2[:, :c], x2[:, c:]
    for kk in range(TOP_K):
        hi, lo = _unpack_halves(yg_ref[kk])
        acc_hi = acc_hi + g[:, kk:kk + 1] * hi
        acc_lo = acc_lo + g[:, kk:kk + 1] * lo
    o_ref[...] = _rms(jnp.concatenate([acc_hi, acc_lo], axis=1), fg_ref[...])


def _combine(x2, gate_tm, final_g, yg, ts):
    n, d = x2.shape
    dw = yg.shape[1]
    return pl.pallas_call(
        _combine_kernel,
        grid=(n // ts,),
        in_specs=[pl.BlockSpec((ts, d), lambda i: (i, 0)),
                  pl.BlockSpec((ts, 128), lambda i: (i, 0)),
                  pl.BlockSpec((1, d), lambda i: (0, 0)),
                  pl.BlockSpec((TOP_K, ts, dw), lambda i: (0, i, 0))],
        out_specs=pl.BlockSpec((ts, d), lambda i: (i, 0)),
        out_shape=jax.ShapeDtypeStruct((n, d), F32),
        compiler_params=_cparams(1),
        name="combine",
    )(x2, gate_tm, final_g, yg.reshape(TOP_K, n, dw))


def _experts_kernel(be_ref, first_ref, nxt_ref, slot_ref, nb_ref,
                    xs_ref, bgu_ref, bd_ref, wgu_hbm, wd_hbm, ys_ref,
                    wgu_st, wd_st, wgu_bf, wd_bf, sem_w):
    i = pl.program_id(0)
    dff = wd_bf.shape[0]
    e = be_ref[i]

    def fetch(ex, sl):
        return (pltpu.make_async_copy(wgu_hbm.at[ex], wgu_st.at[sl], sem_w.at[0, sl]),
                pltpu.make_async_copy(wd_hbm.at[ex], wd_st.at[sl], sem_w.at[1, sl]))

    @pl.when(i == 0)
    def _():
        for cp in fetch(e, slot_ref[e]):
            cp.start()

    @pl.when(first_ref[i] == 1)
    def _():
        sl = slot_ref[e]
        for cp in fetch(e, sl):
            cp.wait()
        nx = nxt_ref[e]

        @pl.when(nx >= 0)
        def _():
            for cp in fetch(nx, 1 - sl):
                cp.start()

        wgu_bf[...] = wgu_st[sl].astype(BF16)
        wd_bf[...] = wd_st[sl].astype(BF16)

    @pl.when(i < nb_ref[0])
    def _():
        xb = jnp.concatenate(_unpack_halves(xs_ref[...]), axis=1).astype(BF16)
        gu = _dot(xb, wgu_bf[...]) + bgu_ref[0]
        x_glu = jnp.minimum(gu[:, :dff], SWIGLU_LIMIT)
        x_lin = jnp.clip(gu[:, dff:], -SWIGLU_LIMIT, SWIGLU_LIMIT)
        act = x_glu * jax.nn.sigmoid(SWIGLU_ALPHA * x_glu) * (x_lin + 1.0)
        ys_ref[...] = _pack_halves(_dot(act.astype(BF16), wd_bf[...]) + bd_ref[0])

    @pl.when(i >= nb_ref[0])
    def _():
        ys_ref[...] = jnp.zeros(ys_ref.shape, jnp.uint32)


def _expert_schedule(counts, bm, n_blocks):
    ne = counts.shape[0]
    padded = (counts + bm - 1) // bm * bm
    pend = jnp.cumsum(padded)
    pstart = (pend - padded).astype(I32)
    nb_used = (pend[-1] // bm).astype(I32)
    blk = jnp.arange(n_blocks, dtype=I32)
    blk_c = jnp.minimum(blk, nb_used - 1)
    be = jnp.minimum(jnp.sum((pend[None, :] <= (blk_c * bm)[:, None]).astype(I32), axis=1),
                     ne - 1)
    first = ((blk < nb_used) & (blk * bm == pstart[be])).astype(I32)
    used = counts > 0
    seq = jnp.cumsum(used.astype(I32)) - 1
    ids = jnp.arange(ne, dtype=I32)
    later = used[None, :] & (ids[None, :] > ids[:, None])
    nxt = jnp.where(jnp.any(later, axis=1), jnp.argmax(later, axis=1), -1).astype(I32)
    slot = (seq & 1).astype(I32)
    return pstart, nb_used.reshape(1), be, first, nxt, slot


def _experts(be, first, nxt, slot, nb_used, xs, w_gu, b_gu, w_down, b_down, bm):
    rows, dw = xs.shape
    ne, d, dff2 = w_gu.shape
    dff = dff2 // 2
    last = lambda i, be, fi, nx, sl, nb: (jnp.maximum(jnp.minimum(i, nb[0] - 1), 0), 0)
    grid_spec = pltpu.PrefetchScalarGridSpec(
        num_scalar_prefetch=5,
        grid=(rows // bm,),
        in_specs=[
            pl.BlockSpec((bm, dw), last),
            pl.BlockSpec((1, 1, dff2), lambda i, be, fi, nx, sl, nb: (be[i], 0, 0)),
            pl.BlockSpec((1, 1, d), lambda i, be, fi, nx, sl, nb: (be[i], 0, 0)),
            pl.BlockSpec(memory_space=pl.ANY),
            pl.BlockSpec(memory_space=pl.ANY),
        ],
        out_specs=pl.BlockSpec((bm, dw), lambda i, be, fi, nx, sl, nb: (i, 0)),
        scratch_shapes=[pltpu.VMEM((2, d, dff2), F32), pltpu.VMEM((2, dff, d), F32),
                        pltpu.VMEM((d, dff2), BF16), pltpu.VMEM((dff, d), BF16),
                        pltpu.SemaphoreType.DMA((2, 2))],
    )
    return pl.pallas_call(
        _experts_kernel,
        grid_spec=grid_spec,
        out_shape=jax.ShapeDtypeStruct((rows, dw), jnp.uint32),
        compiler_params=_cparams(1),
        name="experts",
    )(be, first, nxt, slot, nb_used, xs, b_gu.reshape(ne, 1, dff2),
      b_down.reshape(ne, 1, d), w_gu, w_down)


MIXER_TILE = 512
ATTN_HEADS_PER_STEP = 2
POST_TILE = 512
DEST_TILE = 2048
COMBINE_TILE = 512
EXPERT_BLOCK = 256


def kernel(x, mem, norm_mix_g, w_in, lambda_q1, lambda_k1, lambda_q2, lambda_k2, rel_bias,
           subln_g, w_attn_up, pool_mix, pool_scale, w_pool_up, w_gate, b_gate, w_out,
           norm_x_g, norm_mem_g, w_xq, w_xkv, w_xo, norm_ffn_g, w_router, b_router,
           w_gu, b_gu, w_down, b_down, final_norm_g):
    b, s, d = x.shape
    n = b * s
    assert w_in.shape[0] == 1, "single-layer block"
    row = lambda a: a.reshape(1, -1)
    bf = lambda a: a[0].astype(BF16)

    q, k, v, g0, gyb = _mixer_in(
        x, row(norm_mix_g[0]), bf(w_in), bf(w_gate), row(b_gate[0]), bf(pool_mix),
        row(pool_scale[0]), bf(w_pool_up), MIXER_TILE)

    bias, lam = _rel_bias(rel_bias, row(lambda_q1[0]), row(lambda_k1[0]),
                          row(lambda_q2[0]), row(lambda_k2[0]), s)
    a = _diff_attn(q, k, v, bias, lam, row(subln_g[0]), ATTN_HEADS_PER_STEP)

    k_mem, v_mem = _mem_kv(mem, row(norm_mem_g[0]), bf(w_xkv))

    x2, hf, eidx, rank, gate_tm, counts = _post(
        x.reshape(n, d), a.reshape(n, d), g0, gyb, bf(w_attn_up), bf(w_out),
        row(norm_x_g[0]), bf(w_xq), k_mem, v_mem, bf(w_xo), row(norm_ffn_g[0]),
        w_router[0].T, b_router[0].reshape(-1, 1), b, POST_TILE)

    bm = EXPERT_BLOCK
    rows = n * TOP_K + N_EXPERTS * bm
    pstart, nb_used, be, first, nxt, slot = _expert_schedule(counts[:, 0], bm, rows // bm)

    dest_w = _worker_index_layout(_route_dest(pstart, eidx, rank, DEST_TILE), n)
    xs = _sc_dispatch(hf, dest_w, rows)
    ys = _experts(be, first, nxt, slot, nb_used, xs, w_gu[0], b_gu[0], w_down[0], b_down[0],
                  bm)
    yg = _sc_gather(ys, dest_w, n)
    out = _combine(x2, gate_tm, row(final_norm_g), yg, COMBINE_TILE)
    return out.reshape(b, s, d)
```

```python
import functools
import math

import numpy as np
import jax
import jax.numpy as jnp
from jax import lax
from jax.experimental import pallas as pl
from jax.experimental.pallas import tpu as pltpu
from jax.experimental.pallas import tpu_sc as plsc

F32 = jnp.float32
BF16 = jnp.bfloat16
I32 = jnp.int32

EPS = 1e-6
CHUNK = 64
N_HEADS = 8
HEAD_DIM = 64
HEAD_W = 2 * HEAD_DIM
POOL_WINDOWS = (2, 4, 8, 16)
POOL_GROUP = 128
POOL_PAD = 16
N_BUCKETS = 32
MAX_DISTANCE = 128
X_HEADS = 4
N_EXPERTS = 32
TOP_K = 4
SWIGLU_ALPHA = 1.702
SWIGLU_LIMIT = 7.0
LAMBDA_INIT = 0.8 - 0.6 * math.exp(-0.3 * 0)

ATTN_BLOCK = 256
VMEM_LIMIT = 56 * 1024 * 1024


def _cparams(n_axes, vmem=VMEM_LIMIT):
    return pltpu.CompilerParams(
        dimension_semantics=("arbitrary",) * n_axes, vmem_limit_bytes=vmem)


def _rms(xf, g):
    ms = jnp.mean(xf * xf, axis=-1, keepdims=True)
    return xf * lax.rsqrt(ms + EPS) * g


def _dot(a, b):
    return jnp.dot(a, b, preferred_element_type=F32)


def _pack_halves(x):
    c = x.shape[1] // 2
    as_bits = lambda v: lax.bitcast_convert_type(v.astype(BF16).astype(F32), jnp.uint32)
    return as_bits(x[:, :c]) | (as_bits(x[:, c:]) >> 16)


def _unpack_halves(w):
    hi = lax.bitcast_convert_type(w & jnp.uint32(0xFFFF0000), F32)
    lo = lax.bitcast_convert_type(w << 16, F32)
    return hi, lo


def _dot_nt(a, b, precision=None):
    return lax.dot_general(a, b, (((1,), (1,)), ((), ())),
                           preferred_element_type=F32, precision=precision)


def _mixer_in_kernel(x_ref, g_ref, wq_ref, wk_ref, wv_ref, wu_ref, wg_ref, bg_ref,
                     pmix_ref, pscale_ref, wpu_ref,
                     q_ref, k_ref, v_ref, g0_ref, gyb_ref, ext_ref):
    ts = x_ref.shape[1]
    d = x_ref.shape[2]
    j = pl.program_id(1)
    h = _rms(x_ref[0], g_ref[...]).astype(BF16)
    q_ref[0] = (_dot(h, wq_ref[...]) * (HEAD_DIM ** -0.5)).astype(BF16)
    k_ref[0] = _dot(h, wk_ref[...]).astype(BF16)
    v_ref[0] = _dot(h, wv_ref[...]).astype(BF16)
    u = _dot(h, wu_ref[...])

    @pl.when(j == 0)
    def _():
        ext_ref[0:POOL_PAD, :] = jnp.zeros((POOL_PAD, u.shape[1]), F32)

    ext_ref[POOL_PAD:POOL_PAD + ts, :] = u
    e = ext_ref[...]
    sums = {}
    s = e
    w = 1
    while w < max(POOL_WINDOWS):
        s = s + pltpu.roll(s, w, 0)
        w *= 2
        sums[w] = s
    ext_ref[0:POOL_PAD, :] = ext_ref[ts:ts + POOL_PAD, :]

    pos = (j * ts + lax.broadcasted_iota(I32, (ts, 1), 0) + 1).astype(F32)
    mixed = []
    for gi, w in enumerate(POOL_WINDOWS):
        sl = slice(gi * POOL_GROUP, (gi + 1) * POOL_GROUP)
        win = sums[w][POOL_PAD:, sl]
        pooled = win / jnp.minimum(pos, float(w)) - u[:, sl]
        mixed.append(_dot(pooled.astype(BF16), pmix_ref[gi]) * pscale_ref[:, sl])
    mixed = jnp.concatenate(mixed, axis=1).astype(BF16)
    y_b = _dot(mixed, wpu_ref[...])

    gate = jax.nn.sigmoid(_dot(h, wg_ref[...]) + bg_ref[...])
    g0_ref[...] = gate[:, :d].astype(BF16)
    gyb_ref[...] = (gate[:, d:] * y_b).astype(BF16)


def _mixer_in(x, norm_g, w_in, w_gate, b_gate, pool_mix, pool_scale, w_pool_up, ts):
    b, s, d = x.shape
    aw = N_HEADS * HEAD_W
    pw = len(POOL_WINDOWS) * POOL_GROUP
    n = b * s
    nt = s // ts
    const = lambda *shape: pl.BlockSpec(shape, lambda bi, j: (0,) * len(shape))
    tok3 = pl.BlockSpec((1, ts, aw), lambda bi, j: (bi, j, 0))
    tok2 = pl.BlockSpec((ts, d), lambda bi, j: (bi * nt + j, 0))
    return pl.pallas_call(
        _mixer_in_kernel,
        grid=(b, nt),
        in_specs=[
            pl.BlockSpec((1, ts, d), lambda bi, j: (bi, j, 0)),
            const(1, d),
            pl.BlockSpec((d, aw), lambda bi, j: (0, 0)),
            pl.BlockSpec((d, aw), lambda bi, j: (0, 1)),
            pl.BlockSpec((d, aw), lambda bi, j: (0, 2)),
            pl.BlockSpec((d, pw), lambda bi, j: (0, 3 * aw // pw)),
            const(d, 2 * d),
            const(1, 2 * d),
            const(len(POOL_WINDOWS), POOL_GROUP, POOL_GROUP),
            const(1, pw),
            const(pw, d),
        ],
        out_specs=[tok3, tok3, tok3, tok2, tok2],
        out_shape=[jax.ShapeDtypeStruct((b, s, aw), BF16)] * 3
        + [jax.ShapeDtypeStruct((n, d), BF16)] * 2,
        scratch_shapes=[pltpu.VMEM((ts + POOL_PAD, pw), F32)],
        compiler_params=_cparams(2),
        name="mixer_in",
    )(x, norm_g, w_in, w_in, w_in, w_in, w_gate, b_gate, pool_mix, pool_scale, w_pool_up)


def _rel_bucket(rel, log=jnp.log, f32=lambda a: a.astype(jnp.float32),
                i32=lambda a: a.astype(jnp.int32), xp=jnp):
    nb = N_BUCKETS // 2
    ret = i32(rel > 0) * nb
    n = xp.abs(rel)
    max_exact = nb // 2
    nf = f32(xp.maximum(n, 1))
    large = max_exact + i32(log(nf / max_exact) / math.log(MAX_DISTANCE / max_exact)
                            * (nb - max_exact))
    large = xp.minimum(large, nb - 1)
    return ret + xp.where(n < max_exact, n, large)


def _far_bucket(block, seq):
    rel = -np.arange(block + 1, max(seq, block + 2), dtype=np.int32)
    bk = _rel_bucket(rel, log=np.log, f32=lambda a: a.astype(np.float32),
                     i32=lambda a: a.astype(np.int32), xp=np)
    assert (bk == bk[0]).all(), "far keys must share one relative-position bucket"
    return int(bk[0])


def _rel_bias_kernel(far_bucket, tab_ref, bidx_ref, lq1_ref, lk1_ref, lq2_ref, lk2_ref,
                     bias_ref, lam_ref):
    h = pl.program_id(0)
    bidx = bidx_ref[...]
    acc = jnp.zeros(bidx.shape, F32)
    for bkt in range(N_BUCKETS):
        acc = jnp.where(bidx == bkt, tab_ref[bkt, h], acc)
    acc = acc - tab_ref[far_bucket, h]
    bias_ref[0] = jnp.where(bidx < 0, -jnp.inf, acc)
    lam = (jnp.exp(jnp.sum(lq1_ref[...] * lk1_ref[...], keepdims=True))
           - jnp.exp(jnp.sum(lq2_ref[...] * lk2_ref[...], keepdims=True)) + LAMBDA_INIT)
    lam_ref[...] = jnp.broadcast_to(lam, lam_ref.shape)


def _rel_bias(rel_bias, lq1, lk1, lq2, lk2, seq):
    blk = ATTN_BLOCK
    qpos = jnp.arange(blk, dtype=I32)[:, None]
    kpos = jnp.arange(blk, dtype=I32)[None, :]
    diag = jnp.where(kpos // CHUNK <= qpos // CHUNK, _rel_bucket(kpos - qpos), -1)
    prev = _rel_bucket(kpos - (qpos + blk))
    bidx = jnp.stack([diag, prev]).astype(I32)
    vec = pl.BlockSpec((1, HEAD_DIM), lambda h: (0, 0))
    return pl.pallas_call(
        functools.partial(_rel_bias_kernel, _far_bucket(blk, seq)),
        grid=(N_HEADS,),
        in_specs=[
            pl.BlockSpec(memory_space=pltpu.SMEM),
            pl.BlockSpec((2, blk, blk), lambda h: (0, 0, 0)),
            vec, vec, vec, vec,
        ],
        out_specs=[
            pl.BlockSpec((1, 2, blk, blk), lambda h: (h, 0, 0, 0)),
            pl.BlockSpec((8, 128), lambda h: (0, 0)),
        ],
        out_shape=[jax.ShapeDtypeStruct((N_HEADS, 2, blk, blk), F32),
                   jax.ShapeDtypeStruct((8, 128), F32)],
        compiler_params=_cparams(1),
        name="rel_bias",
    )(rel_bias, bidx, lq1, lk1, lq2, lk2)


def _diff_attn_kernel(q_ref, k_ref, v_ref, bias_ref, lam_ref, sg_ref, o_ref, vext_ref):
    s_len = q_ref.shape[1]
    tq = ATTN_BLOCK
    n_heads = q_ref.shape[2] // HEAD_W
    lam = lam_ref[0:1, 0:1]
    lane = lax.broadcasted_iota(I32, (tq, HEAD_W), 1)
    cols = [slice(hh * HEAD_W, (hh + 1) * HEAD_W) for hh in range(n_heads)]
    b_diag, b_prev = [], []
    for hh in range(n_heads):
        vext_ref[hh, :, :HEAD_W] = v_ref[0, :, cols[hh]]
        vext_ref[hh, :, HEAD_W:] = jnp.ones((s_len, HEAD_W), BF16)
        b_diag.append(jnp.concatenate([bias_ref[hh, 0]] * 2, axis=0))
        b_prev.append(jnp.concatenate([bias_ref[hh, 1]] * 2, axis=0))

    for qi in range(s_len // tq):
        n_keys = (qi + 1) * tq
        rows = slice(qi * tq, (qi + 1) * tq)
        for hh in range(n_heads):
            q = q_ref[0, rows, cols[hh]]
            zero = jnp.zeros_like(q)
            qs = jnp.concatenate([jnp.where(lane < HEAD_DIM, q, zero),
                                  jnp.where(lane >= HEAD_DIM, q, zero)], axis=0)
            s = _dot_nt(qs, k_ref[0, :n_keys, cols[hh]])
            pieces = []
            if qi >= 2:
                pieces.append(s[:, :n_keys - 2 * tq])
            if qi >= 1:
                pieces.append(s[:, n_keys - 2 * tq:n_keys - tq] + b_prev[hh])
            pieces.append(s[:, n_keys - tq:] + b_diag[hh])
            s = jnp.concatenate(pieces, axis=1) if len(pieces) > 1 else pieces[0]
            m = jnp.max(s, axis=-1, keepdims=True)
            p = jnp.exp(s - m).astype(BF16)
            acc = _dot(p, vext_ref[hh, :n_keys, :])
            o = acc[:, :HEAD_W] / acc[:, HEAD_W:HEAD_W + 1]
            a = o[:tq] - lam * o[tq:]
            o_ref[0, rows, cols[hh]] = (
                _rms(a, sg_ref[...]) * (1.0 - LAMBDA_INIT)).astype(BF16)


def _diff_attn(q, k, v, bias, lam, subln_g, heads_per_step):
    b, s, aw = q.shape
    tq = ATTN_BLOCK
    hp = heads_per_step
    spec = pl.BlockSpec((1, s, hp * HEAD_W), lambda bi, h: (bi, 0, h))
    return pl.pallas_call(
        _diff_attn_kernel,
        grid=(b, N_HEADS // hp),
        in_specs=[
            spec, spec, spec,
            pl.BlockSpec((hp, 2, tq, tq), lambda bi, h: (h, 0, 0, 0)),
            pl.BlockSpec((8, 128), lambda bi, h: (0, 0)),
            pl.BlockSpec((1, HEAD_W), lambda bi, h: (0, 0)),
        ],
        out_specs=spec,
        out_shape=jax.ShapeDtypeStruct((b, s, aw), BF16),
        scratch_shapes=[pltpu.VMEM((hp, s, 2 * HEAD_W), BF16)],
        compiler_params=_cparams(2),
        name="diff_attn",
    )(q, k, v, bias, lam, subln_g)


def _mem_kv_kernel(m_ref, g_ref, w_ref, k_ref, v_ref):
    d = m_ref.shape[2]
    m = _rms(m_ref[0], g_ref[...]).astype(BF16)
    kv = _dot(m, w_ref[...])
    k_ref[0] = kv[:, :d].astype(BF16)
    v_ref[0] = kv[:, d:].astype(BF16)


def _mem_kv(mem, norm_g, w_xkv):
    b, ml, d = mem.shape
    blk = pl.BlockSpec((1, ml, d), lambda bi: (bi, 0, 0))
    return pl.pallas_call(
        _mem_kv_kernel,
        grid=(b,),
        in_specs=[blk, pl.BlockSpec((1, d), lambda bi: (0, 0)),
                  pl.BlockSpec((d, 2 * d), lambda bi: (0, 0))],
        out_specs=[blk, blk],
        out_shape=[jax.ShapeDtypeStruct((b, ml, d), BF16)] * 2,
        compiler_params=_cparams(1),
        name="mem_kv",
    )(mem, norm_g, w_xkv)


def _post_kernel(x_ref, a_ref, g0_ref, gyb_ref, wau_ref, wo_ref, nxg_ref, wxq_ref,
                 km_ref, vm_ref, wxo_ref, nfg_ref, wrt_ref, br_ref,
                 x2_ref, hf_ref, eidx_ref, rank_ref, gate_ref, cnt_ref, run_ref):
    ts, d = x_ref.shape
    first = (pl.program_id(0) == 0) & (pl.program_id(1) == 0)

    y_a = _dot(a_ref[...], wau_ref[...])
    merged = g0_ref[...].astype(F32) * y_a + gyb_ref[...].astype(F32)
    x1 = x_ref[...] + _dot(merged.astype(BF16), wo_ref[...])

    hd = d // X_HEADS
    hx = _rms(x1, nxg_ref[...]).astype(BF16)
    qx = (_dot(hx, wxq_ref[...]) * (hd ** -0.5)).astype(BF16)
    heads = []
    for hh in range(X_HEADS):
        sl = slice(hh * hd, (hh + 1) * hd)
        s = _dot_nt(qx[:, sl], km_ref[0, :, sl])
        p = jnp.exp(s - jnp.max(s, axis=-1, keepdims=True))
        p = p / jnp.sum(p, axis=-1, keepdims=True)
        heads.append(_dot(p.astype(BF16), vm_ref[0, :, sl]))
    o = jnp.concatenate(heads, axis=1).astype(BF16)
    x2 = x1 + _dot(o, wxo_ref[...])
    x2_ref[...] = x2
    hf = _rms(x2, nfg_ref[...])
    hf_ref[...] = _pack_halves(hf)

    logits = _dot_nt(wrt_ref[...], hf, precision=lax.Precision.HIGHEST) + br_ref[...]
    ne = logits.shape[0]
    eid = lax.broadcasted_iota(I32, logits.shape, 0).astype(F32)
    work = logits
    vals, idxs, hots = [], [], []
    for _ in range(TOP_K):
        mx = jnp.max(work, axis=0, keepdims=True)
        idx = jnp.min(jnp.where(work == mx, eid, float(ne)), axis=0, keepdims=True)
        hot = eid == idx
        vals.append(mx)
        idxs.append(idx.astype(I32))
        hots.append(hot)
        work = jnp.where(hot, -jnp.inf, work)
    ex = [jnp.exp(vv - vals[0]) for vv in vals]
    den = ex[0] + ex[1] + ex[2] + ex[3]
    gates = [e_ / den for e_ in ex]

    multi = (hots[0] | hots[1] | hots[2] | hots[3])
    multi_f = jnp.where(multi, 1.0, 0.0).astype(F32)
    tri = (lax.broadcasted_iota(I32, (ts, ts), 0)
           < lax.broadcasted_iota(I32, (ts, ts), 1))
    before = _dot(multi_f.astype(BF16), jnp.where(tri, 1.0, 0.0).astype(BF16))

    @pl.when(first)
    def _():
        run_ref[...] = jnp.zeros(run_ref.shape, F32)

    run = run_ref[...]
    pos = before + run
    ranks = [jnp.sum(jnp.where(hot, pos, 0.0), axis=0, keepdims=True) for hot in hots]
    run_new = run + jnp.sum(multi_f, axis=1, keepdims=True)
    run_ref[...] = run_new
    cnt_ref[...] = jnp.broadcast_to(run_new, cnt_ref.shape).astype(I32)

    eidx_ref[...] = jnp.concatenate(idxs, axis=0)
    rank_ref[...] = jnp.concatenate(ranks, axis=0).astype(I32)
    g_rows = jnp.concatenate(gates + [jnp.zeros((128 - TOP_K, ts), F32)], axis=0)
    gate_ref[...] = g_rows.T


def _post(x2d, a2d, g0, gyb, w_attn_up, w_out, norm_x_g, w_xq, k_mem, v_mem, w_xo,
          norm_ffn_g, w_router_t, b_router, batch, ts):
    n, d = x2d.shape
    nt = n // batch // ts
    ml = k_mem.shape[1]
    tok = pl.BlockSpec((ts, d), lambda bi, j: (bi * nt + j, 0))
    lanes = pl.BlockSpec((TOP_K, ts), lambda bi, j: (0, bi * nt + j))
    const = lambda *shape: pl.BlockSpec(shape, lambda bi, j: (0,) * len(shape),
                                        pipeline_mode=pl.Buffered(1))
    mem = pl.BlockSpec((1, ml, d), lambda bi, j: (bi, 0, 0))
    return pl.pallas_call(
        _post_kernel,
        grid=(batch, nt),
        in_specs=[tok, tok, tok, tok, const(d, d), const(d, d), const(1, d), const(d, d),
                  mem, mem, const(d, d), const(1, d), const(N_EXPERTS, d),
                  const(N_EXPERTS, 1)],
        out_specs=[tok, pl.BlockSpec((ts, d // 2), lambda bi, j: (bi * nt + j, 0)),
                   lanes, lanes,
                   pl.BlockSpec((ts, 128), lambda bi, j: (bi * nt + j, 0)),
                   pl.BlockSpec((N_EXPERTS, 128), lambda bi, j: (0, 0))],
        out_shape=[jax.ShapeDtypeStruct((n, d), F32),
                   jax.ShapeDtypeStruct((n, d // 2), jnp.uint32),
                   jax.ShapeDtypeStruct((TOP_K, n), I32), jax.ShapeDtypeStruct((TOP_K, n), I32),
                   jax.ShapeDtypeStruct((n, 128), F32),
                   jax.ShapeDtypeStruct((N_EXPERTS, 128), I32)],
        scratch_shapes=[pltpu.VMEM((N_EXPERTS, 1), F32)],
        compiler_params=_cparams(2),
        name="post",
    )(x2d, a2d, g0, gyb, w_attn_up, w_out, norm_x_g, w_xq, k_mem, v_mem, w_xo,
      norm_ffn_g, w_router_t, b_router)


def _route_dest_kernel(pstart_ref, eidx_ref, rank_ref, dest_ref):
    eidx = eidx_ref[...]
    start = jnp.zeros(eidx.shape, I32)
    for e in range(N_EXPERTS):
        start = jnp.where(eidx == e, pstart_ref[e], start)
    dest_ref[...] = start + rank_ref[...]


def _route_dest(pstart, eidx, rank, tl):
    k, n = eidx.shape
    lanes = pl.BlockSpec((k, tl), lambda i: (0, i))
    return pl.pallas_call(
        _route_dest_kernel,
        grid=(n // tl,),
        in_specs=[pl.BlockSpec(memory_space=pltpu.SMEM), lanes, lanes],
        out_specs=lanes,
        out_shape=jax.ShapeDtypeStruct((k, n), I32),
        compiler_params=_cparams(1),
        name="route_dest",
    )(pstart, eidx, rank)


SC_CORES = 2
SC_SUBCORES = 16
SC_WORKERS = SC_CORES * SC_SUBCORES
MOVE_CHUNK = 64
MOVE_BUFFERS = 3


def _worker_index_layout(dest, n):
    per_w = n // SC_WORKERS
    n_chunks = per_w // MOVE_CHUNK
    d4 = dest.reshape(TOP_K, SC_WORKERS, n_chunks, MOVE_CHUNK)
    return jnp.transpose(d4, (1, 2, 0, 3)).reshape(SC_WORKERS, n_chunks * TOP_K, MOVE_CHUNK)


def _sc_mesh():
    return plsc.VectorSubcoreMesh(core_axis_name="c", subcore_axis_name="s",
                                  num_cores=SC_CORES, num_subcores=SC_SUBCORES)


def _sc_dispatch(hf, dest_w, rows):
    n, d = hf.shape
    per_w = n // SC_WORKERS
    n_chunks = per_w // MOVE_CHUNK

    def body(hf_hbm, dest_hbm, xs_hbm, idx_v, rows_v, rsem, wsem):
        wid = lax.axis_index("s") * SC_CORES + lax.axis_index("c")
        base = wid * per_w
        pltpu.sync_copy(dest_hbm.at[wid], idx_v)

        def read(c):
            b = c % MOVE_BUFFERS
            return pltpu.async_copy(hf_hbm.at[pl.ds(base + c * MOVE_CHUNK, MOVE_CHUNK)],
                                    rows_v.at[b], rsem.at[b])

        scatters = [[] for _ in range(MOVE_BUFFERS)]
        pending = read(0)
        for c in range(n_chunks):
            b = c % MOVE_BUFFERS
            pending.wait()
            if c + 1 < n_chunks:
                for cp in scatters[(c + 1) % MOVE_BUFFERS]:
                    cp.wait()
                pending = read(c + 1)
            scatters[b] = [
                pltpu.async_copy(rows_v.at[b], xs_hbm.at[idx_v.at[c * TOP_K + kk]], wsem.at[b])
                for kk in range(TOP_K)]
        for group in scatters:
            for cp in group:
                cp.wait()

    return pl.kernel(
        body,
        out_type=jax.ShapeDtypeStruct((rows, d), hf.dtype),
        mesh=_sc_mesh(),
        scratch_types=[pltpu.VMEM((n_chunks * TOP_K, MOVE_CHUNK), I32),
                       pltpu.VMEM((MOVE_BUFFERS, MOVE_CHUNK, d), hf.dtype),
                       pltpu.SemaphoreType.DMA((MOVE_BUFFERS,)),
                       pltpu.SemaphoreType.DMA((MOVE_BUFFERS,))],
        name="sc_dispatch",
    )(hf, dest_w)


def _sc_gather(ys, dest_w, n):
    _, d = ys.shape
    per_w = n // SC_WORKERS
    n_chunks = per_w // MOVE_CHUNK

    def body(ys_hbm, dest_hbm, yg_hbm, idx_v, rows_v, rsem, wsem):
        wid = lax.axis_index("s") * SC_CORES + lax.axis_index("c")
        base = wid * per_w
        pltpu.sync_copy(dest_hbm.at[wid], idx_v)
        n_moves = n_chunks * TOP_K

        def gather(m):
            b = m % MOVE_BUFFERS
            return pltpu.async_copy(ys_hbm.at[idx_v.at[m]], rows_v.at[b], rsem.at[b])

        def write(m):
            b = m % MOVE_BUFFERS
            c, kk = divmod(m, TOP_K)
            return pltpu.async_copy(
                rows_v.at[b], yg_hbm.at[pl.ds(kk * n + base + c * MOVE_CHUNK, MOVE_CHUNK)],
                wsem.at[b])

        writes = [None] * MOVE_BUFFERS
        pending = gather(0)
        for m in range(n_moves):
            pending.wait()
            if m + 1 < n_moves:
                nb = (m + 1) % MOVE_BUFFERS
                if writes[nb] is not None:
                    writes[nb].wait()
                pending = gather(m + 1)
            writes[m % MOVE_BUFFERS] = write(m)
        for wr in writes:
            if wr is not None:
                wr.wait()

    return pl.kernel(
        body,
        out_type=jax.ShapeDtypeStruct((TOP_K * n, d), ys.dtype),
        mesh=_sc_mesh(),
        scratch_types=[pltpu.VMEM((n_chunks * TOP_K, MOVE_CHUNK), I32),
                       pltpu.VMEM((MOVE_BUFFERS, MOVE_CHUNK, d), ys.dtype),
                       pltpu.SemaphoreType.DMA((MOVE_BUFFERS,)),
                       pltpu.SemaphoreType.DMA((MOVE_BUFFERS,))],
        name="sc_gather",
    )(ys, dest_w)


def _combine_kernel(x2_ref, gate_ref, fg_ref, yg_ref, o_ref):
    g = gate_ref[...]
    x2 = x2_ref[...]
    c = x2.shape[1] // 2
    acc_hi, acc_lo = x2[:, :c], x2[:, c:]
    for kk in range(TOP_K):
        hi, lo = _unpack_halves(yg_ref[kk])
        acc_hi = acc_hi + g[:, kk:kk + 1] * hi
        acc_lo = acc_lo + g[:, kk:kk + 1] * lo
    o_ref[...] = _rms(jnp.concatenate([acc_hi, acc_lo], axis=1), fg_ref[...])


def _combine(x2, gate_tm, final_g, yg, ts):
    n, d = x2.shape
    dw = yg.shape[1]
    return pl.pallas_call(
        _combine_kernel,
        grid=(n // ts,),
        in_specs=[pl.BlockSpec((ts, d), lambda i: (i, 0)),
                  pl.BlockSpec((ts, 128), lambda i: (i, 0)),
                  pl.BlockSpec((1, d), lambda i: (0, 0)),
                  pl.BlockSpec((TOP_K, ts, dw), lambda i: (0, i, 0))],
        out_specs=pl.BlockSpec((ts, d), lambda i: (i, 0)),
        out_shape=jax.ShapeDtypeStruct((n, d), F32),
        compiler_params=_cparams(1),
        name="combine",
    )(x2, gate_tm, final_g, yg.reshape(TOP_K, n, dw))


def _experts_kernel(be_ref, first_ref, nxt_ref, slot_ref, nb_ref,
                    xs_ref, bgu_ref, bd_ref, wgu_hbm, wd_hbm, ys_ref,
                    wgu_st, wd_st, wgu_bf, wd_bf, sem_w):
    i = pl.program_id(0)
    dff = wd_bf.shape[0]
    e = be_ref[i]

    def fetch(ex, sl):
        return (pltpu.make_async_copy(wgu_hbm.at[ex], wgu_st.at[sl], sem_w.at[0, sl]),
                pltpu.make_async_copy(wd_hbm.at[ex], wd_st.at[sl], sem_w.at[1, sl]))

    @pl.when(i == 0)
    def _():
        for cp in fetch(e, slot_ref[e]):
            cp.start()

    @pl.when(first_ref[i] == 1)
    def _():
        sl = slot_ref[e]
        for cp in fetch(e, sl):
            cp.wait()
        nx = nxt_ref[e]

        @pl.when(nx >= 0)
        def _():
            for cp in fetch(nx, 1 - sl):
                cp.start()

        wgu_bf[...] = wgu_st[sl].astype(BF16)
        wd_bf[...] = wd_st[sl].astype(BF16)

    @pl.when(i < nb_ref[0])
    def _():
        xb = jnp.concatenate(_unpack_halves(xs_ref[...]), axis=1).astype(BF16)
        gu = _dot(xb, wgu_bf[...]) + bgu_ref[0]
        x_glu = jnp.minimum(gu[:, :dff], SWIGLU_LIMIT)
        x_lin = jnp.clip(gu[:, dff:], -SWIGLU_LIMIT, SWIGLU_LIMIT)
        act = x_glu * jax.nn.sigmoid(SWIGLU_ALPHA * x_glu) * (x_lin + 1.0)
        ys_ref[...] = _pack_halves(_dot(act.astype(BF16), wd_bf[...]) + bd_ref[0])

    @pl.when(i >= nb_ref[0])
    def _():
        ys_ref[...] = jnp.zeros(ys_ref.shape, jnp.uint32)


def _expert_schedule(counts, bm, n_blocks):
    ne = counts.shape[0]
    ids = jnp.arange(ne, dtype=I32)
    upto = ids[None, :] <= ids[:, None]
    padded = (counts + bm - 1) // bm * bm
    pend = jnp.sum(jnp.where(upto, padded[None, :], 0), axis=1)
    pstart = (pend - padded).astype(I32)
    nb_used = (jnp.sum(padded) // bm).astype(I32)
    blk = jnp.arange(n_blocks, dtype=I32)
    blk_c = jnp.minimum(blk, nb_used - 1)
    be = jnp.minimum(jnp.sum((pend[None, :] <= (blk_c * bm)[:, None]).astype(I32), axis=1),
                     ne - 1)
    started = jnp.sum(((pstart[None, :] == (blk * bm)[:, None]) & (padded[None, :] > 0))
                      .astype(I32), axis=1)
    first = ((blk < nb_used) & (started > 0)).astype(I32)
    used = counts > 0
    seq = jnp.sum(jnp.where(upto & used[None, :], 1, 0), axis=1) - 1
    later = used[None, :] & (ids[None, :] > ids[:, None])
    nxt = jnp.where(jnp.any(later, axis=1), jnp.argmax(later, axis=1), -1).astype(I32)
    slot = (seq & 1).astype(I32)
    return pstart, nb_used.reshape(1), be, first, nxt, slot


def _experts(be, first, nxt, slot, nb_used, xs, w_gu, b_gu, w_down, b_down, bm):
    rows, dw = xs.shape
    ne, d, dff2 = w_gu.shape
    dff = dff2 // 2
    last = lambda i, be, fi, nx, sl, nb: (jnp.maximum(jnp.minimum(i, nb[0] - 1), 0), 0)
    grid_spec = pltpu.PrefetchScalarGridSpec(
        num_scalar_prefetch=5,
        grid=(rows // bm,),
        in_specs=[
            pl.BlockSpec((bm, dw), last),
            pl.BlockSpec((1, 1, dff2), lambda i, be, fi, nx, sl, nb: (be[i], 0, 0)),
            pl.BlockSpec((1, 1, d), lambda i, be, fi, nx, sl, nb: (be[i], 0, 0)),
            pl.BlockSpec(memory_space=pl.ANY),
            pl.BlockSpec(memory_space=pl.ANY),
        ],
        out_specs=pl.BlockSpec((bm, dw), lambda i, be, fi, nx, sl, nb: (i, 0)),
        scratch_shapes=[pltpu.VMEM((2, d, dff2), F32), pltpu.VMEM((2, dff, d), F32),
                        pltpu.VMEM((d, dff2), BF16), pltpu.VMEM((dff, d), BF16),
                        pltpu.SemaphoreType.DMA((2, 2))],
    )
    return pl.pallas_call(
        _experts_kernel,
        grid_spec=grid_spec,
        out_shape=jax.ShapeDtypeStruct((rows, dw), jnp.uint32),
        compiler_params=_cparams(1),
        name="experts",
    )(be, first, nxt, slot, nb_used, xs, b_gu.reshape(ne, 1, dff2),
      b_down.reshape(ne, 1, d), w_gu, w_down)


MIXER_TILE = 512
ATTN_HEADS_PER_STEP = 2
POST_TILE = 512
DEST_TILE = 2048
COMBINE_TILE = 512
EXPERT_BLOCK = 256


def kernel(x, mem, norm_mix_g, w_in, lambda_q1, lambda_k1, lambda_q2, lambda_k2, rel_bias,
           subln_g, w_attn_up, pool_mix, pool_scale, w_pool_up, w_gate, b_gate, w_out,
           norm_x_g, norm_mem_g, w_xq, w_xkv, w_xo, norm_ffn_g, w_router, b_router,
           w_gu, b_gu, w_down, b_down, final_norm_g):
    b, s, d = x.shape
    n = b * s
    assert w_in.shape[0] == 1, "single-layer block"
    row = lambda a: a.reshape(1, -1)
    bf = lambda a: a[0].astype(BF16)

    q, k, v, g0, gyb = _mixer_in(
        x, row(norm_mix_g[0]), bf(w_in), bf(w_gate), row(b_gate[0]), bf(pool_mix),
        row(pool_scale[0]), bf(w_pool_up), MIXER_TILE)

    bias, lam = _rel_bias(rel_bias, row(lambda_q1[0]), row(lambda_k1[0]),
                          row(lambda_q2[0]), row(lambda_k2[0]), s)
    a = _diff_attn(q, k, v, bias, lam, row(subln_g[0]), ATTN_HEADS_PER_STEP)

    k_mem, v_mem = _mem_kv(mem, row(norm_mem_g[0]), bf(w_xkv))

    x2, hf, eidx, rank, gate_tm, counts = _post(
        x.reshape(n, d), a.reshape(n, d), g0, gyb, bf(w_attn_up), bf(w_out),
        row(norm_x_g[0]), bf(w_xq), k_mem, v_mem, bf(w_xo), row(norm_ffn_g[0]),
        w_router[0].T, b_router[0].reshape(-1, 1), b, POST_TILE)

    bm = EXPERT_BLOCK
    rows = n * TOP_K + N_EXPERTS * bm
    pstart, nb_used, be, first, nxt, slot = _expert_schedule(counts[:, 0], bm, rows // bm)

    dest_w = _worker_index_layout(_route_dest(pstart, eidx, rank, DEST_TILE), n)
    xs = _sc_dispatch(hf, dest_w, rows)
    ys = _experts(be, first, nxt, slot, nb_used, xs, w_gu[0], b_gu[0], w_down[0], b_down[0],
                  bm)
    yg = _sc_gather(ys, dest_w, n)
    out = _combine(x2, gate_tm, row(final_norm_g), yg, COMBINE_TILE)
    return out.reshape(b, s, d)
```

```python
import functools
import math

import numpy as np
import jax
import jax.numpy as jnp
from jax import lax
from jax.experimental import pallas as pl
from jax.experimental.pallas import tpu as pltpu
from jax.experimental.pallas import tpu_sc as plsc

F32 = jnp.float32
BF16 = jnp.bfloat16
I32 = jnp.int32

EPS = 1e-6
CHUNK = 64
N_HEADS = 8
HEAD_DIM = 64
HEAD_W = 2 * HEAD_DIM
POOL_WINDOWS = (2, 4, 8, 16)
POOL_GROUP = 128
POOL_PAD = 16
N_BUCKETS = 32
MAX_DISTANCE = 128
X_HEADS = 4
N_EXPERTS = 32
TOP_K = 4
SWIGLU_ALPHA = 1.702
SWIGLU_LIMIT = 7.0
LAMBDA_INIT = 0.8 - 0.6 * math.exp(-0.3 * 0)

ATTN_BLOCK = 256
VMEM_LIMIT = 56 * 1024 * 1024


def _cparams(n_axes, vmem=VMEM_LIMIT):
    return pltpu.CompilerParams(
        dimension_semantics=("arbitrary",) * n_axes, vmem_limit_bytes=vmem)


def _rms(xf, g):
    ms = jnp.mean(xf * xf, axis=-1, keepdims=True)
    return xf * lax.rsqrt(ms + EPS) * g


def _dot(a, b):
    return jnp.dot(a, b, preferred_element_type=F32)


def _pack_halves(x):
    c = x.shape[1] // 2
    as_bits = lambda v: lax.bitcast_convert_type(v.astype(BF16).astype(F32), jnp.uint32)
    return as_bits(x[:, :c]) | (as_bits(x[:, c:]) >> 16)


def _unpack_halves(w):
    hi = lax.bitcast_convert_type(w & jnp.uint32(0xFFFF0000), F32)
    lo = lax.bitcast_convert_type(w << 16, F32)
    return hi, lo


def _dot_nt(a, b, precision=None):
    return lax.dot_general(a, b, (((1,), (1,)), ((), ())),
                           preferred_element_type=F32, precision=precision)


def _mixer_in_kernel(x_ref, g_ref, wq_ref, wk_ref, wv_ref, wu_ref, wg_ref, bg_ref,
                     pmix_ref, pscale_ref, wpu_ref,
                     q_ref, k_ref, v_ref, g0_ref, gyb_ref, ext_ref):
    ts = x_ref.shape[1]
    d = x_ref.shape[2]
    j = pl.program_id(1)
    h = _rms(x_ref[0], g_ref[...]).astype(BF16)
    q_ref[0] = (_dot(h, wq_ref[...]) * (HEAD_DIM ** -0.5)).astype(BF16)
    k_ref[0] = _dot(h, wk_ref[...]).astype(BF16)
    v_ref[0] = _dot(h, wv_ref[...]).astype(BF16)
    u = _dot(h, wu_ref[...])

    @pl.when(j == 0)
    def _():
        ext_ref[0:POOL_PAD, :] = jnp.zeros((POOL_PAD, u.shape[1]), F32)

    ext_ref[POOL_PAD:POOL_PAD + ts, :] = u
    e = ext_ref[...]
    sums = {}
    s = e
    w = 1
    while w < max(POOL_WINDOWS):
        s = s + pltpu.roll(s, w, 0)
        w *= 2
        sums[w] = s
    ext_ref[0:POOL_PAD, :] = ext_ref[ts:ts + POOL_PAD, :]

    pos = (j * ts + lax.broadcasted_iota(I32, (ts, 1), 0) + 1).astype(F32)
    mixed = []
    for gi, w in enumerate(POOL_WINDOWS):
        sl = slice(gi * POOL_GROUP, (gi + 1) * POOL_GROUP)
        win = sums[w][POOL_PAD:, sl]
        pooled = win / jnp.minimum(pos, float(w)) - u[:, sl]
        mixed.append(_dot(pooled.astype(BF16), pmix_ref[gi]) * pscale_ref[:, sl])
    mixed = jnp.concatenate(mixed, axis=1).astype(BF16)
    y_b = _dot(mixed, wpu_ref[...])

    gate = jax.nn.sigmoid(_dot(h, wg_ref[...]) + bg_ref[...])
    g0_ref[...] = gate[:, :d].astype(BF16)
    gyb_ref[...] = (gate[:, d:] * y_b).astype(BF16)


def _mixer_in(x, norm_g, w_in, w_gate, b_gate, pool_mix, pool_scale, w_pool_up, ts):
    b, s, d = x.shape
    aw = N_HEADS * HEAD_W
    pw = len(POOL_WINDOWS) * POOL_GROUP
    n = b * s
    nt = s // ts
    const = lambda *shape: pl.BlockSpec(shape, lambda bi, j: (0,) * len(shape))
    tok3 = pl.BlockSpec((1, ts, aw), lambda bi, j: (bi, j, 0))
    tok2 = pl.BlockSpec((ts, d), lambda bi, j: (bi * nt + j, 0))
    return pl.pallas_call(
        _mixer_in_kernel,
        grid=(b, nt),
        in_specs=[
            pl.BlockSpec((1, ts, d), lambda bi, j: (bi, j, 0)),
            const(1, d),
            pl.BlockSpec((d, aw), lambda bi, j: (0, 0)),
            pl.BlockSpec((d, aw), lambda bi, j: (0, 1)),
            pl.BlockSpec((d, aw), lambda bi, j: (0, 2)),
            pl.BlockSpec((d, pw), lambda bi, j: (0, 3 * aw // pw)),
            const(d, 2 * d),
            const(1, 2 * d),
            const(len(POOL_WINDOWS), POOL_GROUP, POOL_GROUP),
            const(1, pw),
            const(pw, d),
        ],
        out_specs=[tok3, tok3, tok3, tok2, tok2],
        out_shape=[jax.ShapeDtypeStruct((b, s, aw), BF16)] * 3
        + [jax.ShapeDtypeStruct((n, d), BF16)] * 2,
        scratch_shapes=[pltpu.VMEM((ts + POOL_PAD, pw), F32)],
        compiler_params=_cparams(2),
        name="mixer_in",
    )(x, norm_g, w_in, w_in, w_in, w_in, w_gate, b_gate, pool_mix, pool_scale, w_pool_up)


def _rel_bucket(rel, log=jnp.log, f32=lambda a: a.astype(jnp.float32),
                i32=lambda a: a.astype(jnp.int32), xp=jnp):
    nb = N_BUCKETS // 2
    ret = i32(rel > 0) * nb
    n = xp.abs(rel)
    max_exact = nb // 2
    nf = f32(xp.maximum(n, 1))
    large = max_exact + i32(log(nf / max_exact) / math.log(MAX_DISTANCE / max_exact)
                            * (nb - max_exact))
    large = xp.minimum(large, nb - 1)
    return ret + xp.where(n < max_exact, n, large)


def _far_bucket(block, seq):
    rel = -np.arange(block + 1, max(seq, block + 2), dtype=np.int32)
    bk = _rel_bucket(rel, log=np.log, f32=lambda a: a.astype(np.float32),
                     i32=lambda a: a.astype(np.int32), xp=np)
    assert (bk == bk[0]).all(), "far keys must share one relative-position bucket"
    return int(bk[0])


def _rel_bias_kernel(far_bucket, tab_ref, bidx_ref, lq1_ref, lk1_ref, lq2_ref, lk2_ref,
                     bias_ref, lam_ref):
    h = pl.program_id(0)
    bidx = bidx_ref[...]
    acc = jnp.zeros(bidx.shape, F32)
    for bkt in range(N_BUCKETS):
        acc = jnp.where(bidx == bkt, tab_ref[bkt, h], acc)
    acc = acc - tab_ref[far_bucket, h]
    bias_ref[0] = jnp.where(bidx < 0, -jnp.inf, acc)
    lam = (jnp.exp(jnp.sum(lq1_ref[...] * lk1_ref[...], keepdims=True))
           - jnp.exp(jnp.sum(lq2_ref[...] * lk2_ref[...], keepdims=True)) + LAMBDA_INIT)
    lam_ref[...] = jnp.broadcast_to(lam, lam_ref.shape)


def _rel_bias(rel_bias, lq1, lk1, lq2, lk2, seq):
    blk = ATTN_BLOCK
    qpos = jnp.arange(blk, dtype=I32)[:, None]
    kpos = jnp.arange(blk, dtype=I32)[None, :]
    diag = jnp.where(kpos // CHUNK <= qpos // CHUNK, _rel_bucket(kpos - qpos), -1)
    prev = _rel_bucket(kpos - (qpos + blk))
    bidx = jnp.stack([diag, prev]).astype(I32)
    vec = pl.BlockSpec((1, HEAD_DIM), lambda h: (0, 0))
    return pl.pallas_call(
        functools.partial(_rel_bias_kernel, _far_bucket(blk, seq)),
        grid=(N_HEADS,),
        in_specs=[
            pl.BlockSpec(memory_space=pltpu.SMEM),
            pl.BlockSpec((2, blk, blk), lambda h: (0, 0, 0)),
            vec, vec, vec, vec,
        ],
        out_specs=[
            pl.BlockSpec((1, 2, blk, blk), lambda h: (h, 0, 0, 0)),
            pl.BlockSpec((8, 128), lambda h: (0, 0)),
        ],
        out_shape=[jax.ShapeDtypeStruct((N_HEADS, 2, blk, blk), F32),
                   jax.ShapeDtypeStruct((8, 128), F32)],
        compiler_params=_cparams(1),
        name="rel_bias",
    )(rel_bias, bidx, lq1, lk1, lq2, lk2)


def _diff_attn_kernel(q_ref, k_ref, v_ref, bias_ref, lam_ref, sg_ref, o_ref, vext_ref):
    s_len = q_ref.shape[1]
    tq = ATTN_BLOCK
    n_heads = q_ref.shape[2] // HEAD_W
    lam = lam_ref[0:1, 0:1]
    lane = lax.broadcasted_iota(I32, (tq, HEAD_W), 1)
    cols = [slice(hh * HEAD_W, (hh + 1) * HEAD_W) for hh in range(n_heads)]
    b_diag, b_prev = [], []
    for hh in range(n_heads):
        vext_ref[hh, :, :HEAD_W] = v_ref[0, :, cols[hh]]
        vext_ref[hh, :, HEAD_W:] = jnp.ones((s_len, HEAD_W), BF16)
        b_diag.append(jnp.concatenate([bias_ref[hh, 0]] * 2, axis=0))
        b_prev.append(jnp.concatenate([bias_ref[hh, 1]] * 2, axis=0))

    for qi in range(s_len // tq):
        n_keys = (qi + 1) * tq
        rows = slice(qi * tq, (qi + 1) * tq)
        for hh in range(n_heads):
            q = q_ref[0, rows, cols[hh]]
            zero = jnp.zeros_like(q)
            qs = jnp.concatenate([jnp.where(lane < HEAD_DIM, q, zero),
                                  jnp.where(lane >= HEAD_DIM, q, zero)], axis=0)
            s = _dot_nt(qs, k_ref[0, :n_keys, cols[hh]])
            pieces = []
            if qi >= 2:
                pieces.append(s[:, :n_keys - 2 * tq])
            if qi >= 1:
                pieces.append(s[:, n_keys - 2 * tq:n_keys - tq] + b_prev[hh])
            pieces.append(s[:, n_keys - tq:] + b_diag[hh])
            s = jnp.concatenate(pieces, axis=1) if len(pieces) > 1 else pieces[0]
            m = jnp.max(s, axis=-1, keepdims=True)
            p = jnp.exp(s - m).astype(BF16)
            acc = _dot(p, vext_ref[hh, :n_keys, :])
            o = acc[:, :HEAD_W] / acc[:, HEAD_W:HEAD_W + 1]
            a = o[:tq] - lam * o[tq:]
            o_ref[0, rows, cols[hh]] = (
                _rms(a, sg_ref[...]) * (1.0 - LAMBDA_INIT)).astype(BF16)


def _diff_attn(q, k, v, bias, lam, subln_g, heads_per_step):
    b, s, aw = q.shape
    tq = ATTN_BLOCK
    hp = heads_per_step
    spec = pl.BlockSpec((1, s, hp * HEAD_W), lambda bi, h: (bi, 0, h))
    return pl.pallas_call(
        _diff_attn_kernel,
        grid=(b, N_HEADS // hp),
        in_specs=[
            spec, spec, spec,
            pl.BlockSpec((hp, 2, tq, tq), lambda bi, h: (h, 0, 0, 0)),
            pl.BlockSpec((8, 128), lambda bi, h: (0, 0)),
            pl.BlockSpec((1, HEAD_W), lambda bi, h: (0, 0)),
        ],
        out_specs=spec,
        out_shape=jax.ShapeDtypeStruct((b, s, aw), BF16),
        scratch_shapes=[pltpu.VMEM((hp, s, 2 * HEAD_W), BF16)],
        compiler_params=_cparams(2),
        name="diff_attn",
    )(q, k, v, bias, lam, subln_g)


def _mem_kv_kernel(m_ref, g_ref, w_ref, k_ref, v_ref):
    d = m_ref.shape[2]
    m = _rms(m_ref[0], g_ref[...]).astype(BF16)
    kv = _dot(m, w_ref[...])
    k_ref[0] = kv[:, :d].astype(BF16)
    v_ref[0] = kv[:, d:].astype(BF16)


def _mem_kv(mem, norm_g, w_xkv):
    b, ml, d = mem.shape
    blk = pl.BlockSpec((1, ml, d), lambda bi: (bi, 0, 0))
    return pl.pallas_call(
        _mem_kv_kernel,
        grid=(b,),
        in_specs=[blk, pl.BlockSpec((1, d), lambda bi: (0, 0)),
                  pl.BlockSpec((d, 2 * d), lambda bi: (0, 0))],
        out_specs=[blk, blk],
        out_shape=[jax.ShapeDtypeStruct((b, ml, d), BF16)] * 2,
        compiler_params=_cparams(1),
        name="mem_kv",
    )(mem, norm_g, w_xkv)


def _post_kernel(x_ref, a_ref, g0_ref, gyb_ref, wau_ref, wo_ref, nxg_ref, wxq_ref,
                 km_ref, vm_ref, wxo_ref, nfg_ref, wrt_ref, br_ref,
                 x2_ref, hf_ref, eidx_ref, rank_ref, gate_ref, cnt_ref, run_ref):
    ts, d = x_ref.shape
    first = (pl.program_id(0) == 0) & (pl.program_id(1) == 0)

    y_a = _dot(a_ref[...], wau_ref[...])
    merged = g0_ref[...].astype(F32) * y_a + gyb_ref[...].astype(F32)
    x1 = x_ref[...] + _dot(merged.astype(BF16), wo_ref[...])

    hd = d // X_HEADS
    hx = _rms(x1, nxg_ref[...]).astype(BF16)
    qx = (_dot(hx, wxq_ref[...]) * (hd ** -0.5)).astype(BF16)
    heads = []
    for hh in range(X_HEADS):
        sl = slice(hh * hd, (hh + 1) * hd)
        s = _dot_nt(qx[:, sl], km_ref[0, :, sl])
        p = jnp.exp(s - jnp.max(s, axis=-1, keepdims=True))
        p = p / jnp.sum(p, axis=-1, keepdims=True)
        heads.append(_dot(p.astype(BF16), vm_ref[0, :, sl]))
    o = jnp.concatenate(heads, axis=1).astype(BF16)
    x2 = x1 + _dot(o, wxo_ref[...])
    x2_ref[...] = x2
    hf = _rms(x2, nfg_ref[...])
    hf_ref[...] = _pack_halves(hf)

    logits = _dot_nt(wrt_ref[...], hf, precision=lax.Precision.HIGHEST) + br_ref[...]
    ne = logits.shape[0]
    eid = lax.broadcasted_iota(I32, logits.shape, 0).astype(F32)
    work = logits
    vals, idxs, hots = [], [], []
    for _ in range(TOP_K):
        mx = jnp.max(work, axis=0, keepdims=True)
        idx = jnp.min(jnp.where(work == mx, eid, float(ne)), axis=0, keepdims=True)
        hot = eid == idx
        vals.append(mx)
        idxs.append(idx.astype(I32))
        hots.append(hot)
        work = jnp.where(hot, -jnp.inf, work)
    ex = [jnp.exp(vv - vals[0]) for vv in vals]
    den = ex[0] + ex[1] + ex[2] + ex[3]
    gates = [e_ / den for e_ in ex]

    multi = (hots[0] | hots[1] | hots[2] | hots[3])
    multi_f = jnp.where(multi, 1.0, 0.0).astype(F32)
    tri = (lax.broadcasted_iota(I32, (ts, ts), 0)
           < lax.broadcasted_iota(I32, (ts, ts), 1))
    before = _dot(multi_f.astype(BF16), jnp.where(tri, 1.0, 0.0).astype(BF16))

    @pl.when(first)
    def _():
        run_ref[...] = jnp.zeros(run_ref.shape, F32)

    run = run_ref[...]
    pos = before + run
    ranks = [jnp.sum(jnp.where(hot, pos, 0.0), axis=0, keepdims=True) for hot in hots]
    run_new = run + jnp.sum(multi_f, axis=1, keepdims=True)
    run_ref[...] = run_new
    cnt_ref[...] = jnp.broadcast_to(run_new, cnt_ref.shape).astype(I32)

    eidx_ref[...] = jnp.concatenate(idxs, axis=0)
    rank_ref[...] = jnp.concatenate(ranks, axis=0).astype(I32)
    g_rows = jnp.concatenate(gates + [jnp.zeros((128 - TOP_K, ts), F32)], axis=0)
    gate_ref[...] = g_rows.T


def _post(x2d, a2d, g0, gyb, w_attn_up, w_out, norm_x_g, w_xq, k_mem, v_mem, w_xo,
          norm_ffn_g, w_router_t, b_router, batch, ts):
    n, d = x2d.shape
    nt = n // batch // ts
    ml = k_mem.shape[1]
    tok = pl.BlockSpec((ts, d), lambda bi, j: (bi * nt + j, 0))
    lanes = pl.BlockSpec((TOP_K, ts), lambda bi, j: (0, bi * nt + j))
    const = lambda *shape: pl.BlockSpec(shape, lambda bi, j: (0,) * len(shape),
                                        pipeline_mode=pl.Buffered(1))
    mem = pl.BlockSpec((1, ml, d), lambda bi, j: (bi, 0, 0))
    return pl.pallas_call(
        _post_kernel,
        grid=(batch, nt),
        in_specs=[tok, tok, tok, tok, const(d, d), const(d, d), const(1, d), const(d, d),
                  mem, mem, const(d, d), const(1, d), const(N_EXPERTS, d),
                  const(N_EXPERTS, 1)],
        out_specs=[tok, pl.BlockSpec((ts, d // 2), lambda bi, j: (bi * nt + j, 0)),
                   lanes, lanes,
                   pl.BlockSpec((ts, 128), lambda bi, j: (bi * nt + j, 0)),
                   pl.BlockSpec((N_EXPERTS, 128), lambda bi, j: (0, 0))],
        out_shape=[jax.ShapeDtypeStruct((n, d), F32),
                   jax.ShapeDtypeStruct((n, d // 2), jnp.uint32),
                   jax.ShapeDtypeStruct((TOP_K, n), I32), jax.ShapeDtypeStruct((TOP_K, n), I32),
                   jax.ShapeDtypeStruct((n, 128), F32),
                   jax.ShapeDtypeStruct((N_EXPERTS, 128), I32)],
        scratch_shapes=[pltpu.VMEM((N_EXPERTS, 1), F32)],
        compiler_params=_cparams(2),
        name="post",
    )(x2d, a2d, g0, gyb, w_attn_up, w_out, norm_x_g, w_xq, k_mem, v_mem, w_xo,
      norm_ffn_g, w_router_t, b_router)


def _route_dest_kernel(pstart_ref, eidx_ref, rank_ref, dest_ref):
    eidx = eidx_ref[...]
    start = jnp.zeros(eidx.shape, I32)
    for e in range(N_EXPERTS):
        start = jnp.where(eidx == e, pstart_ref[e], start)
    dest_ref[...] = start + rank_ref[...]


def _route_dest(pstart, eidx, rank, tl):
    k, n = eidx.shape
    lanes = pl.BlockSpec((k, tl), lambda i: (0, i))
    return pl.pallas_call(
        _route_dest_kernel,
        grid=(n // tl,),
        in_specs=[pl.BlockSpec(memory_space=pltpu.SMEM), lanes, lanes],
        out_specs=lanes,
        out_shape=jax.ShapeDtypeStruct((k, n), I32),
        compiler_params=_cparams(1),
        name="route_dest",
    )(pstart, eidx, rank)


SC_CORES = 2
SC_SUBCORES = 16
SC_WORKERS = SC_CORES * SC_SUBCORES
MOVE_CHUNK = 64
MOVE_BUFFERS = 3


def _worker_index_layout(dest, n):
    per_w = n // SC_WORKERS
    n_chunks = per_w // MOVE_CHUNK
    d4 = dest.reshape(TOP_K, SC_WORKERS, n_chunks, MOVE_CHUNK)
    return jnp.transpose(d4, (1, 2, 0, 3)).reshape(SC_WORKERS, n_chunks * TOP_K, MOVE_CHUNK)


def _sc_mesh():
    return plsc.VectorSubcoreMesh(core_axis_name="c", subcore_axis_name="s",
                                  num_cores=SC_CORES, num_subcores=SC_SUBCORES)


def _sc_dispatch(hf, dest_w, rows):
    n, d = hf.shape
    per_w = n // SC_WORKERS
    n_chunks = per_w // MOVE_CHUNK

    def body(hf_hbm, dest_hbm, xs_hbm, idx_v, rows_v, rsem, wsem):
        wid = lax.axis_index("s") * SC_CORES + lax.axis_index("c")
        base = wid * per_w
        pltpu.sync_copy(dest_hbm.at[wid], idx_v)

        def read(c):
            b = c % MOVE_BUFFERS
            return pltpu.async_copy(hf_hbm.at[pl.ds(base + c * MOVE_CHUNK, MOVE_CHUNK)],
                                    rows_v.at[b], rsem.at[b])

        scatters = [[] for _ in range(MOVE_BUFFERS)]
        pending = read(0)
        for c in range(n_chunks):
            b = c % MOVE_BUFFERS
            pending.wait()
            if c + 1 < n_chunks:
                for cp in scatters[(c + 1) % MOVE_BUFFERS]:
                    cp.wait()
                pending = read(c + 1)
            scatters[b] = [
                pltpu.async_copy(rows_v.at[b], xs_hbm.at[idx_v.at[c * TOP_K + kk]], wsem.at[b])
                for kk in range(TOP_K)]
        for group in scatters:
            for cp in group:
                cp.wait()

    return pl.kernel(
        body,
        out_type=jax.ShapeDtypeStruct((rows, d), hf.dtype),
        mesh=_sc_mesh(),
        scratch_types=[pltpu.VMEM((n_chunks * TOP_K, MOVE_CHUNK), I32),
                       pltpu.VMEM((MOVE_BUFFERS, MOVE_CHUNK, d), hf.dtype),
                       pltpu.SemaphoreType.DMA((MOVE_BUFFERS,)),
                       pltpu.SemaphoreType.DMA((MOVE_BUFFERS,))],
        name="sc_dispatch",
    )(hf, dest_w)


def _sc_gather(ys, dest_w, n):
    _, d = ys.shape
    per_w = n // SC_WORKERS
    n_chunks = per_w // MOVE_CHUNK

    def body(ys_hbm, dest_hbm, yg_hbm, idx_v, rows_v, rsem, wsem):
        wid = lax.axis_index("s") * SC_CORES + lax.axis_index("c")
        base = wid * per_w
        pltpu.sync_copy(dest_hbm.at[wid], idx_v)
        n_moves = n_chunks * TOP_K

        def gather(m):
            b = m % MOVE_BUFFERS
            return pltpu.async_copy(ys_hbm.at[idx_v.at[m]], rows_v.at[b], rsem.at[b])

        def write(m):
            b = m % MOVE_BUFFERS
            c, kk = divmod(m, TOP_K)
            return pltpu.async_copy(
                rows_v.at[b], yg_hbm.at[pl.ds(kk * n + base + c * MOVE_CHUNK, MOVE_CHUNK)],
                wsem.at[b])

        writes = [None] * MOVE_BUFFERS
        pending = gather(0)
        for m in range(n_moves):
            pending.wait()
            if m + 1 < n_moves:
                nb = (m + 1) % MOVE_BUFFERS
                if writes[nb] is not None:
                    writes[nb].wait()
                pending = gather(m + 1)
            writes[m % MOVE_BUFFERS] = write(m)
        for wr in writes:
            if wr is not None:
                wr.wait()

    return pl.kernel(
        body,
        out_type=jax.ShapeDtypeStruct((TOP_K * n, d), ys.dtype),
        mesh=_sc_mesh(),
        scratch_types=[pltpu.VMEM((n_chunks * TOP_K, MOVE_CHUNK), I32),
                       pltpu.VMEM((MOVE_BUFFERS, MOVE_CHUNK, d), ys.dtype),
                       pltpu.SemaphoreType.DMA((MOVE_BUFFERS,)),
                       pltpu.SemaphoreType.DMA((MOVE_BUFFERS,))],
        name="sc_gather",
    )(ys, dest_w)


def _combine_kernel(x2_ref, gate_ref, fg_ref, yg_ref, o_ref):
    g = gate_ref[...]
    x2 = x2_ref[...]
    c = x2.shape[1] // 2
    acc_hi, acc_lo = x2[:, :c], x2[:, c:]
    for kk in range(TOP_K):
        hi, lo = _unpack_halves(yg_ref[kk])
        acc_hi = acc_hi + g[:, kk:kk + 1] * hi
        acc_lo = acc_lo + g[:, kk:kk + 1] * lo
    o_ref[...] = _rms(jnp.concatenate([acc_hi, acc_lo], axis=1), fg_ref[...])


def _combine(x2, gate_tm, final_g, yg, ts):
    n, d = x2.shape
    dw = yg.shape[1]
    return pl.pallas_call(
        _combine_kernel,
        grid=(n // ts,),
        in_specs=[pl.BlockSpec((ts, d), lambda i: (i, 0)),
                  pl.BlockSpec((ts, 128), lambda i: (i, 0)),
                  pl.BlockSpec((1, d), lambda i: (0, 0)),
                  pl.BlockSpec((TOP_K, ts, dw), lambda i: (0, i, 0))],
        out_specs=pl.BlockSpec((ts, d), lambda i: (i, 0)),
        out_shape=jax.ShapeDtypeStruct((n, d), F32),
        compiler_params=_cparams(1),
        name="combine",
    )(x2, gate_tm, final_g, yg.reshape(TOP_K, n, dw))


def _experts_kernel(be_ref, first_ref, nxt_ref, slot_ref, nb_ref,
                    xs_ref, bgu_ref, bd_ref, wgu_hbm, wd_hbm, ys_ref,
                    wgu_st, wd_st, wgu_bf, wd_bf, sem_w):
    i = pl.program_id(0)
    dff = wd_bf.shape[0]
    e = be_ref[i]

    def fetch(ex, sl):
        return (pltpu.make_async_copy(wgu_hbm.at[ex], wgu_st.at[sl], sem_w.at[0, sl]),
                pltpu.make_async_copy(wd_hbm.at[ex], wd_st.at[sl], sem_w.at[1, sl]))

    @pl.when(i == 0)
    def _():
        for cp in fetch(e, slot_ref[e]):
            cp.start()

    @pl.when(first_ref[i] == 1)
    def _():
        sl = slot_ref[e]
        for cp in fetch(e, sl):
            cp.wait()
        nx = nxt_ref[e]

        @pl.when(nx >= 0)
        def _():
            for cp in fetch(nx, 1 - sl):
                cp.start(priority=1)

        wgu_bf[...] = wgu_st[sl].astype(BF16)
        wd_bf[...] = wd_st[sl].astype(BF16)

    @pl.when(i < nb_ref[0])
    def _():
        xb = jnp.concatenate(_unpack_halves(xs_ref[...]), axis=1).astype(BF16)
        gu = _dot(xb, wgu_bf[...]) + bgu_ref[0]
        x_glu = jnp.minimum(gu[:, :dff], SWIGLU_LIMIT)
        x_lin = jnp.clip(gu[:, dff:], -SWIGLU_LIMIT, SWIGLU_LIMIT)
        act = x_glu * jax.nn.sigmoid(SWIGLU_ALPHA * x_glu) * (x_lin + 1.0)
        ys_ref[...] = _pack_halves(_dot(act.astype(BF16), wd_bf[...]) + bd_ref[0])

    @pl.when(i >= nb_ref[0])
    def _():
        ys_ref[...] = jnp.zeros(ys_ref.shape, jnp.uint32)


def _expert_schedule(counts, bm, n_blocks):
    ne = counts.shape[0]
    ids = jnp.arange(ne, dtype=I32)
    upto = ids[None, :] <= ids[:, None]
    padded = (counts + bm - 1) // bm * bm
    pend = jnp.sum(jnp.where(upto, padded[None, :], 0), axis=1)
    pstart = (pend - padded).astype(I32)
    nb_used = (jnp.sum(padded) // bm).astype(I32)
    blk = jnp.arange(n_blocks, dtype=I32)
    blk_c = jnp.minimum(blk, nb_used - 1)
    be = jnp.minimum(jnp.sum((pend[None, :] <= (blk_c * bm)[:, None]).astype(I32), axis=1),
                     ne - 1)
    started = jnp.sum(((pstart[None, :] == (blk * bm)[:, None]) & (padded[None, :] > 0))
                      .astype(I32), axis=1)
    first = ((blk < nb_used) & (started > 0)).astype(I32)
    used = counts > 0
    seq = jnp.sum(jnp.where(upto & used[None, :], 1, 0), axis=1) - 1
    later = used[None, :] & (ids[None, :] > ids[:, None])
    nxt = jnp.where(jnp.any(later, axis=1), jnp.argmax(later, axis=1), -1).astype(I32)
    slot = (seq & 1).astype(I32)
    return pstart, nb_used.reshape(1), be, first, nxt, slot


def _experts(be, first, nxt, slot, nb_used, xs, w_gu, b_gu, w_down, b_down, bm):
    rows, dw = xs.shape
    ne, d, dff2 = w_gu.shape
    dff = dff2 // 2
    last = lambda i, be, fi, nx, sl, nb: (jnp.maximum(jnp.minimum(i, nb[0] - 1), 0), 0)
    grid_spec = pltpu.PrefetchScalarGridSpec(
        num_scalar_prefetch=5,
        grid=(rows // bm,),
        in_specs=[
            pl.BlockSpec((bm, dw), last),
            pl.BlockSpec((1, 1, dff2), lambda i, be, fi, nx, sl, nb: (be[i], 0, 0)),
            pl.BlockSpec((1, 1, d), lambda i, be, fi, nx, sl, nb: (be[i], 0, 0)),
            pl.BlockSpec(memory_space=pl.ANY),
            pl.BlockSpec(memory_space=pl.ANY),
        ],
        out_specs=pl.BlockSpec((bm, dw), lambda i, be, fi, nx, sl, nb: (i, 0)),
        scratch_shapes=[pltpu.VMEM((2, d, dff2), F32), pltpu.VMEM((2, dff, d), F32),
                        pltpu.VMEM((d, dff2), BF16), pltpu.VMEM((dff, d), BF16),
                        pltpu.SemaphoreType.DMA((2, 2))],
    )
    return pl.pallas_call(
        _experts_kernel,
        grid_spec=grid_spec,
        out_shape=jax.ShapeDtypeStruct((rows, dw), jnp.uint32),
        compiler_params=_cparams(1),
        name="experts",
    )(be, first, nxt, slot, nb_used, xs, b_gu.reshape(ne, 1, dff2),
      b_down.reshape(ne, 1, d), w_gu, w_down)


MIXER_TILE = 512
ATTN_HEADS_PER_STEP = 2
POST_TILE = 512
DEST_TILE = 2048
COMBINE_TILE = 512
EXPERT_BLOCK = 256


def kernel(x, mem, norm_mix_g, w_in, lambda_q1, lambda_k1, lambda_q2, lambda_k2, rel_bias,
           subln_g, w_attn_up, pool_mix, pool_scale, w_pool_up, w_gate, b_gate, w_out,
           norm_x_g, norm_mem_g, w_xq, w_xkv, w_xo, norm_ffn_g, w_router, b_router,
           w_gu, b_gu, w_down, b_down, final_norm_g):
    b, s, d = x.shape
    n = b * s
    assert w_in.shape[0] == 1, "single-layer block"
    row = lambda a: a.reshape(1, -1)
    bf = lambda a: a[0].astype(BF16)

    q, k, v, g0, gyb = _mixer_in(
        x, row(norm_mix_g[0]), bf(w_in), bf(w_gate), row(b_gate[0]), bf(pool_mix),
        row(pool_scale[0]), bf(w_pool_up), MIXER_TILE)

    bias, lam = _rel_bias(rel_bias, row(lambda_q1[0]), row(lambda_k1[0]),
                          row(lambda_q2[0]), row(lambda_k2[0]), s)
    a = _diff_attn(q, k, v, bias, lam, row(subln_g[0]), ATTN_HEADS_PER_STEP)

    k_mem, v_mem = _mem_kv(mem, row(norm_mem_g[0]), bf(w_xkv))

    x2, hf, eidx, rank, gate_tm, counts = _post(
        x.reshape(n, d), a.reshape(n, d), g0, gyb, bf(w_attn_up), bf(w_out),
        row(norm_x_g[0]), bf(w_xq), k_mem, v_mem, bf(w_xo), row(norm_ffn_g[0]),
        w_router[0].T, b_router[0].reshape(-1, 1), b, POST_TILE)

    bm = EXPERT_BLOCK
    rows = n * TOP_K + N_EXPERTS * bm
    pstart, nb_used, be, first, nxt, slot = _expert_schedule(counts[:, 0], bm, rows // bm)

    dest_w = _worker_index_layout(_route_dest(pstart, eidx, rank, DEST_TILE), n)
    xs = _sc_dispatch(hf, dest_w, rows)
    ys = _experts(be, first, nxt, slot, nb_used, xs, w_gu[0], b_gu[0], w_down[0], b_down[0],
                  bm)
    yg = _sc_gather(ys, dest_w, n)
    out = _combine(x2, gate_tm, row(final_norm_g), yg, COMBINE_TILE)
    return out.reshape(b, s, d)
```

```python
import functools
import math

import numpy as np
import jax
import jax.numpy as jnp
from jax import lax
from jax.experimental import pallas as pl
from jax.experimental.pallas import tpu as pltpu
from jax.experimental.pallas import tpu_sc as plsc

F32 = jnp.float32
BF16 = jnp.bfloat16
I32 = jnp.int32

EPS = 1e-6
CHUNK = 64
N_HEADS = 8
HEAD_DIM = 64
HEAD_W = 2 * HEAD_DIM
POOL_WINDOWS = (2, 4, 8, 16)
POOL_GROUP = 128
POOL_PAD = 16
N_BUCKETS = 32
MAX_DISTANCE = 128
X_HEADS = 4
N_EXPERTS = 32
TOP_K = 4
SWIGLU_ALPHA = 1.702
SWIGLU_LIMIT = 7.0
LAMBDA_INIT = 0.8 - 0.6 * math.exp(-0.3 * 0)

ATTN_BLOCK = 256
VMEM_LIMIT = 56 * 1024 * 1024


def _cparams(n_axes, vmem=VMEM_LIMIT):
    return pltpu.CompilerParams(
        dimension_semantics=("arbitrary",) * n_axes, vmem_limit_bytes=vmem)


def _rms(xf, g):
    ms = jnp.mean(xf * xf, axis=-1, keepdims=True)
    return xf * lax.rsqrt(ms + EPS) * g


def _dot(a, b):
    return jnp.dot(a, b, preferred_element_type=F32)


def _pack_halves(x):
    c = x.shape[1] // 2
    as_bits = lambda v: lax.bitcast_convert_type(v.astype(BF16).astype(F32), jnp.uint32)
    return as_bits(x[:, :c]) | (as_bits(x[:, c:]) >> 16)


def _unpack_halves(w):
    hi = lax.bitcast_convert_type(w & jnp.uint32(0xFFFF0000), F32)
    lo = lax.bitcast_convert_type(w << 16, F32)
    return hi, lo


def _dot_nt(a, b, precision=None):
    return lax.dot_general(a, b, (((1,), (1,)), ((), ())),
                           preferred_element_type=F32, precision=precision)


def _mixer_in_kernel(x_ref, g_ref, wq_ref, wk_ref, wv_ref, wu_ref, wg_ref, bg_ref,
                     pmix_ref, pscale_ref, wpu_ref,
                     q_ref, k_ref, v_ref, g0_ref, gyb_ref, ext_ref):
    ts = x_ref.shape[1]
    d = x_ref.shape[2]
    j = pl.program_id(1)
    h = _rms(x_ref[0], g_ref[...]).astype(BF16)
    q_ref[0] = (_dot(h, wq_ref[...]) * (HEAD_DIM ** -0.5)).astype(BF16)
    k_ref[0] = _dot(h, wk_ref[...]).astype(BF16)
    v_ref[0] = _dot(h, wv_ref[...]).astype(BF16)
    u = _dot(h, wu_ref[...])

    @pl.when(j == 0)
    def _():
        ext_ref[0:POOL_PAD, :] = jnp.zeros((POOL_PAD, u.shape[1]), F32)

    ext_ref[POOL_PAD:POOL_PAD + ts, :] = u
    e = ext_ref[...]
    sums = {}
    s = e
    w = 1
    while w < max(POOL_WINDOWS):
        s = s + pltpu.roll(s, w, 0)
        w *= 2
        sums[w] = s
    ext_ref[0:POOL_PAD, :] = ext_ref[ts:ts + POOL_PAD, :]

    pos = (j * ts + lax.broadcasted_iota(I32, (ts, 1), 0) + 1).astype(F32)
    mixed = []
    for gi, w in enumerate(POOL_WINDOWS):
        sl = slice(gi * POOL_GROUP, (gi + 1) * POOL_GROUP)
        win = sums[w][POOL_PAD:, sl]
        pooled = win / jnp.minimum(pos, float(w)) - u[:, sl]
        mixed.append(_dot(pooled.astype(BF16), pmix_ref[gi]) * pscale_ref[:, sl])
    mixed = jnp.concatenate(mixed, axis=1).astype(BF16)
    y_b = _dot(mixed, wpu_ref[...])

    gate = jax.nn.sigmoid(_dot(h, wg_ref[...]) + bg_ref[...])
    g0_ref[...] = gate[:, :d].astype(BF16)
    gyb_ref[...] = (gate[:, d:] * y_b).astype(BF16)


def _mixer_in(x, norm_g, w_in, w_gate, b_gate, pool_mix, pool_scale, w_pool_up, ts):
    b, s, d = x.shape
    aw = N_HEADS * HEAD_W
    pw = len(POOL_WINDOWS) * POOL_GROUP
    n = b * s
    nt = s // ts
    const = lambda *shape: pl.BlockSpec(shape, lambda bi, j: (0,) * len(shape))
    tok3 = pl.BlockSpec((1, ts, aw), lambda bi, j: (bi, j, 0))
    tok2 = pl.BlockSpec((ts, d), lambda bi, j: (bi * nt + j, 0))
    return pl.pallas_call(
        _mixer_in_kernel,
        grid=(b, nt),
        in_specs=[
            pl.BlockSpec((1, ts, d), lambda bi, j: (bi, j, 0)),
            const(1, d),
            pl.BlockSpec((d, aw), lambda bi, j: (0, 0)),
            pl.BlockSpec((d, aw), lambda bi, j: (0, 1)),
            pl.BlockSpec((d, aw), lambda bi, j: (0, 2)),
            pl.BlockSpec((d, pw), lambda bi, j: (0, 3 * aw // pw)),
            const(d, 2 * d),
            const(1, 2 * d),
            const(len(POOL_WINDOWS), POOL_GROUP, POOL_GROUP),
            const(1, pw),
            const(pw, d),
        ],
        out_specs=[tok3, tok3, tok3, tok2, tok2],
        out_shape=[jax.ShapeDtypeStruct((b, s, aw), BF16)] * 3
        + [jax.ShapeDtypeStruct((n, d), BF16)] * 2,
        scratch_shapes=[pltpu.VMEM((ts + POOL_PAD, pw), F32)],
        compiler_params=_cparams(2),
        name="mixer_in",
    )(x, norm_g, w_in, w_in, w_in, w_in, w_gate, b_gate, pool_mix, pool_scale, w_pool_up)


def _rel_bucket(rel, log=jnp.log, f32=lambda a: a.astype(jnp.float32),
                i32=lambda a: a.astype(jnp.int32), xp=jnp):
    nb = N_BUCKETS // 2
    ret = i32(rel > 0) * nb
    n = xp.abs(rel)
    max_exact = nb // 2
    nf = f32(xp.maximum(n, 1))
    large = max_exact + i32(log(nf / max_exact) / math.log(MAX_DISTANCE / max_exact)
                            * (nb - max_exact))
    large = xp.minimum(large, nb - 1)
    return ret + xp.where(n < max_exact, n, large)


def _far_bucket(block, seq):
    rel = -np.arange(block + 1, max(seq, block + 2), dtype=np.int32)
    bk = _rel_bucket(rel, log=np.log, f32=lambda a: a.astype(np.float32),
                     i32=lambda a: a.astype(np.int32), xp=np)
    assert (bk == bk[0]).all(), "far keys must share one relative-position bucket"
    return int(bk[0])


def _rel_bias_kernel(far_bucket, tab_ref, bidx_ref, lq1_ref, lk1_ref, lq2_ref, lk2_ref,
                     bias_ref, lam_ref):
    h = pl.program_id(0)
    bidx = bidx_ref[...]
    acc = jnp.zeros(bidx.shape, F32)
    for bkt in range(N_BUCKETS):
        acc = jnp.where(bidx == bkt, tab_ref[bkt, h], acc)
    acc = acc - tab_ref[far_bucket, h]
    bias_ref[0] = jnp.where(bidx < 0, -jnp.inf, acc)
    lam = (jnp.exp(jnp.sum(lq1_ref[...] * lk1_ref[...], keepdims=True))
           - jnp.exp(jnp.sum(lq2_ref[...] * lk2_ref[...], keepdims=True)) + LAMBDA_INIT)
    lam_ref[...] = jnp.broadcast_to(lam, lam_ref.shape)


def _rel_bias(rel_bias, lq1, lk1, lq2, lk2, seq):
    blk = ATTN_BLOCK
    qpos = jnp.arange(blk, dtype=I32)[:, None]
    kpos = jnp.arange(blk, dtype=I32)[None, :]
    diag = jnp.where(kpos // CHUNK <= qpos // CHUNK, _rel_bucket(kpos - qpos), -1)
    prev = _rel_bucket(kpos - (qpos + blk))
    bidx = jnp.stack([diag, prev]).astype(I32)
    vec = pl.BlockSpec((1, HEAD_DIM), lambda h: (0, 0))
    return pl.pallas_call(
        functools.partial(_rel_bias_kernel, _far_bucket(blk, seq)),
        grid=(N_HEADS,),
        in_specs=[
            pl.BlockSpec(memory_space=pltpu.SMEM),
            pl.BlockSpec((2, blk, blk), lambda h: (0, 0, 0)),
            vec, vec, vec, vec,
        ],
        out_specs=[
            pl.BlockSpec((1, 2, blk, blk), lambda h: (h, 0, 0, 0)),
            pl.BlockSpec((8, 128), lambda h: (0, 0)),
        ],
        out_shape=[jax.ShapeDtypeStruct((N_HEADS, 2, blk, blk), F32),
                   jax.ShapeDtypeStruct((8, 128), F32)],
        compiler_params=_cparams(1),
        name="rel_bias",
    )(rel_bias, bidx, lq1, lk1, lq2, lk2)


def _diff_attn_kernel(q_ref, k_ref, v_ref, bias_ref, lam_ref, sg_ref, o_ref, vext_ref):
    s_len = q_ref.shape[1]
    tq = ATTN_BLOCK
    n_heads = q_ref.shape[2] // HEAD_W
    lam = lam_ref[0:1, 0:1]
    lane = lax.broadcasted_iota(I32, (tq, HEAD_W), 1)
    cols = [slice(hh * HEAD_W, (hh + 1) * HEAD_W) for hh in range(n_heads)]
    b_diag, b_prev = [], []
    for hh in range(n_heads):
        vext_ref[hh, :, :HEAD_W] = v_ref[0, :, cols[hh]]
        vext_ref[hh, :, HEAD_W:] = jnp.ones((s_len, HEAD_W), BF16)
        b_diag.append(jnp.concatenate([bias_ref[hh, 0]] * 2, axis=0))
        b_prev.append(jnp.concatenate([bias_ref[hh, 1]] * 2, axis=0))

    for qi in range(s_len // tq):
        n_keys = (qi + 1) * tq
        rows = slice(qi * tq, (qi + 1) * tq)
        for hh in range(n_heads):
            q = q_ref[0, rows, cols[hh]]
            zero = jnp.zeros_like(q)
            qs = jnp.concatenate([jnp.where(lane < HEAD_DIM, q, zero),
                                  jnp.where(lane >= HEAD_DIM, q, zero)], axis=0)
            s = _dot_nt(qs, k_ref[0, :n_keys, cols[hh]])
            pieces = []
            if qi >= 2:
                pieces.append(s[:, :n_keys - 2 * tq])
            if qi >= 1:
                pieces.append(s[:, n_keys - 2 * tq:n_keys - tq] + b_prev[hh])
            pieces.append(s[:, n_keys - tq:] + b_diag[hh])
            s = jnp.concatenate(pieces, axis=1) if len(pieces) > 1 else pieces[0]
            m = jnp.max(s, axis=-1, keepdims=True)
            p = jnp.exp(s - m).astype(BF16)
            acc = _dot(p, vext_ref[hh, :n_keys, :])
            o = acc[:, :HEAD_W] / acc[:, HEAD_W:HEAD_W + 1]
            a = o[:tq] - lam * o[tq:]
            o_ref[0, rows, cols[hh]] = (
                _rms(a, sg_ref[...]) * (1.0 - LAMBDA_INIT)).astype(BF16)


def _diff_attn(q, k, v, bias, lam, subln_g, heads_per_step):
    b, s, aw = q.shape
    tq = ATTN_BLOCK
    hp = heads_per_step
    spec = pl.BlockSpec((1, s, hp * HEAD_W), lambda bi, h: (bi, 0, h))
    return pl.pallas_call(
        _diff_attn_kernel,
        grid=(b, N_HEADS // hp),
        in_specs=[
            spec, spec, spec,
            pl.BlockSpec((hp, 2, tq, tq), lambda bi, h: (h, 0, 0, 0)),
            pl.BlockSpec((8, 128), lambda bi, h: (0, 0)),
            pl.BlockSpec((1, HEAD_W), lambda bi, h: (0, 0)),
        ],
        out_specs=spec,
        out_shape=jax.ShapeDtypeStruct((b, s, aw), BF16),
        scratch_shapes=[pltpu.VMEM((hp, s, 2 * HEAD_W), BF16)],
        compiler_params=_cparams(2),
        name="diff_attn",
    )(q, k, v, bias, lam, subln_g)


def _mem_kv_kernel(m_ref, g_ref, w_ref, k_ref, v_ref):
    d = m_ref.shape[2]
    m = _rms(m_ref[0], g_ref[...]).astype(BF16)
    kv = _dot(m, w_ref[...])
    k_ref[0] = kv[:, :d].astype(BF16)
    v_ref[0] = kv[:, d:].astype(BF16)


def _mem_kv(mem, norm_g, w_xkv):
    b, ml, d = mem.shape
    blk = pl.BlockSpec((1, ml, d), lambda bi: (bi, 0, 0))
    return pl.pallas_call(
        _mem_kv_kernel,
        grid=(b,),
        in_specs=[blk, pl.BlockSpec((1, d), lambda bi: (0, 0)),
                  pl.BlockSpec((d, 2 * d), lambda bi: (0, 0))],
        out_specs=[blk, blk],
        out_shape=[jax.ShapeDtypeStruct((b, ml, d), BF16)] * 2,
        compiler_params=_cparams(1),
        name="mem_kv",
    )(mem, norm_g, w_xkv)


def _post_kernel(x_ref, a_ref, g0_ref, gyb_ref, wau_ref, wo_ref, nxg_ref, wxq_ref,
                 km_ref, vm_ref, wxo_ref, nfg_ref, wrt_ref, br_ref,
                 x2_ref, hf_ref, eidx_ref, rank_ref, gate_ref, cnt_ref, run_ref):
    ts, d = x_ref.shape
    first = (pl.program_id(0) == 0) & (pl.program_id(1) == 0)

    y_a = _dot(a_ref[...], wau_ref[...])
    merged = g0_ref[...].astype(F32) * y_a + gyb_ref[...].astype(F32)
    x1 = x_ref[...] + _dot(merged.astype(BF16), wo_ref[...])

    hd = d // X_HEADS
    hx = _rms(x1, nxg_ref[...]).astype(BF16)
    qx = (_dot(hx, wxq_ref[...]) * (hd ** -0.5)).astype(BF16)
    heads = []
    for hh in range(X_HEADS):
        sl = slice(hh * hd, (hh + 1) * hd)
        s = _dot_nt(qx[:, sl], km_ref[0, :, sl])
        p = jnp.exp(s - jnp.max(s, axis=-1, keepdims=True))
        p = p / jnp.sum(p, axis=-1, keepdims=True)
        heads.append(_dot(p.astype(BF16), vm_ref[0, :, sl]))
    o = jnp.concatenate(heads, axis=1).astype(BF16)
    x2 = x1 + _dot(o, wxo_ref[...])
    x2_ref[...] = x2
    hf = _rms(x2, nfg_ref[...])
    hf_ref[...] = _pack_halves(hf)

    logits = _dot_nt(wrt_ref[...], hf, precision=lax.Precision.HIGHEST) + br_ref[...]
    ne = logits.shape[0]
    eid = lax.broadcasted_iota(I32, logits.shape, 0).astype(F32)
    work = logits
    vals, idxs, hots = [], [], []
    for _ in range(TOP_K):
        mx = jnp.max(work, axis=0, keepdims=True)
        idx = jnp.min(jnp.where(work == mx, eid, float(ne)), axis=0, keepdims=True)
        hot = eid == idx
        vals.append(mx)
        idxs.append(idx.astype(I32))
        hots.append(hot)
        work = jnp.where(hot, -jnp.inf, work)
    ex = [jnp.exp(vv - vals[0]) for vv in vals]
    den = ex[0] + ex[1] + ex[2] + ex[3]
    gates = [e_ / den for e_ in ex]

    multi = (hots[0] | hots[1] | hots[2] | hots[3])
    multi_f = jnp.where(multi, 1.0, 0.0).astype(F32)
    tri = (lax.broadcasted_iota(I32, (ts, ts), 0)
           < lax.broadcasted_iota(I32, (ts, ts), 1))
    before = _dot(multi_f.astype(BF16), jnp.where(tri, 1.0, 0.0).astype(BF16))

    @pl.when(first)
    def _():
        run_ref[...] = jnp.zeros(run_ref.shape, F32)

    run = run_ref[...]
    pos = before + run
    ranks = [jnp.sum(jnp.where(hot, pos, 0.0), axis=0, keepdims=True) for hot in hots]
    run_new = run + jnp.sum(multi_f, axis=1, keepdims=True)
    run_ref[...] = run_new
    cnt_ref[...] = jnp.broadcast_to(run_new, cnt_ref.shape).astype(I32)

    eidx_ref[...] = jnp.concatenate(idxs, axis=0)
    rank_ref[...] = jnp.concatenate(ranks, axis=0).astype(I32)
    g_rows = jnp.concatenate(gates + [jnp.zeros((128 - TOP_K, ts), F32)], axis=0)
    gate_ref[...] = g_rows.T


def _post(x2d, a2d, g0, gyb, w_attn_up, w_out, norm_x_g, w_xq, k_mem, v_mem, w_xo,
          norm_ffn_g, w_router_t, b_router, batch, ts):
    n, d = x2d.shape
    nt = n // batch // ts
    ml = k_mem.shape[1]
    tok = pl.BlockSpec((ts, d), lambda bi, j: (bi * nt + j, 0))
    lanes = pl.BlockSpec((TOP_K, ts), lambda bi, j: (0, bi * nt + j))
    const = lambda *shape: pl.BlockSpec(shape, lambda bi, j: (0,) * len(shape),
                                        pipeline_mode=pl.Buffered(1))
    mem = pl.BlockSpec((1, ml, d), lambda bi, j: (bi, 0, 0))
    return pl.pallas_call(
        _post_kernel,
        grid=(batch, nt),
        in_specs=[tok, tok, tok, tok, const(d, d), const(d, d), const(1, d), const(d, d),
                  mem, mem, const(d, d), const(1, d), const(N_EXPERTS, d),
                  const(N_EXPERTS, 1)],
        out_specs=[tok, pl.BlockSpec((ts, d // 2), lambda bi, j: (bi * nt + j, 0)),
                   lanes, lanes,
                   pl.BlockSpec((ts, 128), lambda bi, j: (bi * nt + j, 0)),
                   pl.BlockSpec((N_EXPERTS, 128), lambda bi, j: (0, 0))],
        out_shape=[jax.ShapeDtypeStruct((n, d), F32),
                   jax.ShapeDtypeStruct((n, d // 2), jnp.uint32),
                   jax.ShapeDtypeStruct((TOP_K, n), I32), jax.ShapeDtypeStruct((TOP_K, n), I32),
                   jax.ShapeDtypeStruct((n, 128), F32),
                   jax.ShapeDtypeStruct((N_EXPERTS, 128), I32)],
        scratch_shapes=[pltpu.VMEM((N_EXPERTS, 1), F32)],
        compiler_params=_cparams(2),
        name="post",
    )(x2d, a2d, g0, gyb, w_attn_up, w_out, norm_x_g, w_xq, k_mem, v_mem, w_xo,
      norm_ffn_g, w_router_t, b_router)


def _route_dest_kernel(pstart_ref, eidx_ref, rank_ref, dest_ref):
    eidx = eidx_ref[...]
    start = jnp.zeros(eidx.shape, I32)
    for e in range(N_EXPERTS):
        start = jnp.where(eidx == e, pstart_ref[e], start)
    dest_ref[...] = start + rank_ref[...]


def _route_dest(pstart, eidx, rank, tl):
    k, n = eidx.shape
    lanes = pl.BlockSpec((k, tl), lambda i: (0, i))
    return pl.pallas_call(
        _route_dest_kernel,
        grid=(n // tl,),
        in_specs=[pl.BlockSpec(memory_space=pltpu.SMEM), lanes, lanes],
        out_specs=lanes,
        out_shape=jax.ShapeDtypeStruct((k, n), I32),
        compiler_params=_cparams(1),
        name="route_dest",
    )(pstart, eidx, rank)


SC_CORES = 2
SC_SUBCORES = 16
SC_WORKERS = SC_CORES * SC_SUBCORES
MOVE_CHUNK = 64
MOVE_BUFFERS = 3


def _worker_index_layout(dest, n):
    per_w = n // SC_WORKERS
    n_chunks = per_w // MOVE_CHUNK
    d4 = dest.reshape(TOP_K, SC_WORKERS, n_chunks, MOVE_CHUNK)
    return jnp.transpose(d4, (1, 2, 0, 3)).reshape(SC_WORKERS, n_chunks * TOP_K, MOVE_CHUNK)


def _sc_mesh():
    return plsc.VectorSubcoreMesh(core_axis_name="c", subcore_axis_name="s",
                                  num_cores=SC_CORES, num_subcores=SC_SUBCORES)


def _sc_dispatch(hf, dest_w, rows):
    n, d = hf.shape
    per_w = n // SC_WORKERS
    n_chunks = per_w // MOVE_CHUNK

    def body(hf_hbm, dest_hbm, xs_hbm, idx_v, rows_v, rsem, wsem):
        wid = lax.axis_index("s") * SC_CORES + lax.axis_index("c")
        base = wid * per_w
        pltpu.sync_copy(dest_hbm.at[wid], idx_v)

        def read(c):
            b = c % MOVE_BUFFERS
            return pltpu.async_copy(hf_hbm.at[pl.ds(base + c * MOVE_CHUNK, MOVE_CHUNK)],
                                    rows_v.at[b], rsem.at[b])

        scatters = [[] for _ in range(MOVE_BUFFERS)]
        pending = read(0)
        for c in range(n_chunks):
            b = c % MOVE_BUFFERS
            pending.wait()
            if c + 1 < n_chunks:
                for cp in scatters[(c + 1) % MOVE_BUFFERS]:
                    cp.wait()
                pending = read(c + 1)
            scatters[b] = [
                pltpu.async_copy(rows_v.at[b], xs_hbm.at[idx_v.at[c * TOP_K + kk]], wsem.at[b])
                for kk in range(TOP_K)]
        for group in scatters:
            for cp in group:
                cp.wait()

    return pl.kernel(
        body,
        out_type=jax.ShapeDtypeStruct((rows, d), hf.dtype),
        mesh=_sc_mesh(),
        scratch_types=[pltpu.VMEM((n_chunks * TOP_K, MOVE_CHUNK), I32),
                       pltpu.VMEM((MOVE_BUFFERS, MOVE_CHUNK, d), hf.dtype),
                       pltpu.SemaphoreType.DMA((MOVE_BUFFERS,)),
                       pltpu.SemaphoreType.DMA((MOVE_BUFFERS,))],
        name="sc_dispatch",
    )(hf, dest_w)


def _sc_gather(ys, dest_w, n):
    _, d = ys.shape
    per_w = n // SC_WORKERS
    n_chunks = per_w // MOVE_CHUNK

    def body(ys_hbm, dest_hbm, yg_hbm, idx_v, rows_v, rsem, wsem):
        wid = lax.axis_index("s") * SC_CORES + lax.axis_index("c")
        base = wid * per_w
        pltpu.sync_copy(dest_hbm.at[wid], idx_v)
        n_moves = n_chunks * TOP_K

        def gather(m):
            b = m % MOVE_BUFFERS
            return pltpu.async_copy(ys_hbm.at[idx_v.at[m]], rows_v.at[b], rsem.at[b])

        def write(m):
            b = m % MOVE_BUFFERS
            c, kk = divmod(m, TOP_K)
            return pltpu.async_copy(
                rows_v.at[b], yg_hbm.at[pl.ds(kk * n + base + c * MOVE_CHUNK, MOVE_CHUNK)],
                wsem.at[b])

        writes = [None] * MOVE_BUFFERS
        pending = gather(0)
        for m in range(n_moves):
            pending.wait()
            if m + 1 < n_moves:
                nb = (m + 1) % MOVE_BUFFERS
                if writes[nb] is not None:
                    writes[nb].wait()
                pending = gather(m + 1)
            writes[m % MOVE_BUFFERS] = write(m)
        for wr in writes:
            if wr is not None:
                wr.wait()

    return pl.kernel(
        body,
        out_type=jax.ShapeDtypeStruct((TOP_K * n, d), ys.dtype),
        mesh=_sc_mesh(),
        scratch_types=[pltpu.VMEM((n_chunks * TOP_K, MOVE_CHUNK), I32),
                       pltpu.VMEM((MOVE_BUFFERS, MOVE_CHUNK, d), ys.dtype),
                       pltpu.SemaphoreType.DMA((MOVE_BUFFERS,)),
                       pltpu.SemaphoreType.DMA((MOVE_BUFFERS,))],
        name="sc_gather",
    )(ys, dest_w)


def _combine_kernel(x2_ref, gate_ref, fg_ref, yg_ref, o_ref):
    g = gate_ref[...]
    x2 = x2_ref[...]
    c = x2.shape[1] // 2
    acc_hi, acc_lo = x2[:, :c], x2[:, c:]
    for kk in range(TOP_K):
        hi, lo = _unpack_halves(yg_ref[kk])
        acc_hi = acc_hi + g[:, kk:kk + 1] * hi
        acc_lo = acc_lo + g[:, kk:kk + 1] * lo
    o_ref[...] = _rms(jnp.concatenate([acc_hi, acc_lo], axis=1), fg_ref[...])


def _combine(x2, gate_tm, final_g, yg, ts):
    n, d = x2.shape
    dw = yg.shape[1]
    return pl.pallas_call(
        _combine_kernel,
        grid=(n // ts,),
        in_specs=[pl.BlockSpec((ts, d), lambda i: (i, 0)),
                  pl.BlockSpec((ts, 128), lambda i: (i, 0)),
                  pl.BlockSpec((1, d), lambda i: (0, 0)),
                  pl.BlockSpec((TOP_K, ts, dw), lambda i: (0, i, 0))],
        out_specs=pl.BlockSpec((ts, d), lambda i: (i, 0)),
        out_shape=jax.ShapeDtypeStruct((n, d), F32),
        compiler_params=_cparams(1),
        name="combine",
    )(x2, gate_tm, final_g, yg.reshape(TOP_K, n, dw))


def _experts_kernel(be_ref, first_ref, kidx_ref, nxt_ref, slot_ref, issued_ref, nb_ref,
                    xs_ref, bgu_ref, bd_ref, wgu_hbm, wd_hbm, ys_ref,
                    wgu_st, wd_st, wgu_bf, wd_bf, sem_w):
    i = pl.program_id(0)
    d, dff = wgu_bf.shape[0], wd_bf.shape[0]
    e = be_ref[i]
    sl = slot_ref[e]

    def chunk(ex, to, c):
        r0 = pl.multiple_of(c * (d // WEIGHT_CHUNKS), d // WEIGHT_CHUNKS)
        r1 = pl.multiple_of(c * (dff // WEIGHT_CHUNKS), dff // WEIGHT_CHUNKS)
        return (pltpu.make_async_copy(wgu_hbm.at[ex, pl.ds(r0, d // WEIGHT_CHUNKS)],
                                      wgu_st.at[to, pl.ds(r0, d // WEIGHT_CHUNKS)],
                                      sem_w.at[0, to]),
                pltpu.make_async_copy(wd_hbm.at[ex, pl.ds(r1, dff // WEIGHT_CHUNKS)],
                                      wd_st.at[to, pl.ds(r1, dff // WEIGHT_CHUNKS)],
                                      sem_w.at[1, to]))

    @pl.when(first_ref[i] == 1)
    def _():
        for c in range(WEIGHT_CHUNKS):
            @pl.when(c >= issued_ref[e])
            def _(c=c):
                for cp in chunk(e, sl, c):
                    cp.start()

        pltpu.make_async_copy(wgu_hbm.at[e], wgu_st.at[sl], sem_w.at[0, sl]).wait()
        pltpu.make_async_copy(wd_hbm.at[e], wd_st.at[sl], sem_w.at[1, sl]).wait()
        wgu_bf[...] = wgu_st[sl].astype(BF16)
        wd_bf[...] = wd_st[sl].astype(BF16)

    nx = nxt_ref[e]
    k = kidx_ref[i]

    @pl.when((nx >= 0) & (k < WEIGHT_CHUNKS))
    def _():
        for cp in chunk(nx, 1 - sl, k):
            cp.start(priority=1)

    @pl.when(i < nb_ref[0])
    def _():
        xb = jnp.concatenate(_unpack_halves(xs_ref[...]), axis=1).astype(BF16)
        gu = _dot(xb, wgu_bf[...]) + bgu_ref[0]
        x_glu = jnp.minimum(gu[:, :dff], SWIGLU_LIMIT)
        x_lin = jnp.clip(gu[:, dff:], -SWIGLU_LIMIT, SWIGLU_LIMIT)
        act = x_glu * jax.nn.sigmoid(SWIGLU_ALPHA * x_glu) * (x_lin + 1.0)
        ys_ref[...] = _pack_halves(_dot(act.astype(BF16), wd_bf[...]) + bd_ref[0])

    @pl.when(i >= nb_ref[0])
    def _():
        ys_ref[...] = jnp.zeros(ys_ref.shape, jnp.uint32)


def _expert_schedule(counts, bm, n_blocks):
    ne = counts.shape[0]
    ids = jnp.arange(ne, dtype=I32)
    upto = ids[None, :] <= ids[:, None]
    padded = (counts + bm - 1) // bm * bm
    pend = jnp.sum(jnp.where(upto, padded[None, :], 0), axis=1)
    pstart = (pend - padded).astype(I32)
    nb_used = (jnp.sum(padded) // bm).astype(I32)
    blk = jnp.arange(n_blocks, dtype=I32)
    blk_c = jnp.minimum(blk, nb_used - 1)
    be = jnp.minimum(jnp.sum((pend[None, :] <= (blk_c * bm)[:, None]).astype(I32), axis=1),
                     ne - 1)
    started = jnp.sum(((pstart[None, :] == (blk * bm)[:, None]) & (padded[None, :] > 0))
                      .astype(I32), axis=1)
    first = ((blk < nb_used) & (started > 0)).astype(I32)
    used = counts > 0
    seq = jnp.sum(jnp.where(upto & used[None, :], 1, 0), axis=1) - 1
    later = used[None, :] & (ids[None, :] > ids[:, None])
    nxt = jnp.where(jnp.any(later, axis=1), jnp.argmax(later, axis=1), -1).astype(I32)
    slot = (seq & 1).astype(I32)
    hit = ids[None, :] == be[:, None]
    kidx = jnp.where(blk < nb_used,
                     blk - jnp.sum(jnp.where(hit, (pstart // bm)[None, :], 0), axis=1),
                     WEIGHT_CHUNKS).astype(I32)
    earlier = used[None, :] & (ids[None, :] < ids[:, None])
    prev = jnp.max(jnp.where(earlier, ids[None, :], -1), axis=1)
    prev_blocks = jnp.sum(jnp.where(ids[None, :] == prev[:, None], (padded // bm)[None, :], 0),
                          axis=1)
    issued = jnp.minimum(prev_blocks, WEIGHT_CHUNKS).astype(I32)
    return pstart, nb_used.reshape(1), be, first, kidx, nxt, slot, issued


WEIGHT_CHUNKS = 8


def _experts(be, first, kidx, nxt, slot, issued, nb_used, xs, w_gu, b_gu, w_down, b_down, bm):
    rows, dw = xs.shape
    ne, d, dff2 = w_gu.shape
    dff = dff2 // 2
    last = lambda i, be, fi, ki, nx, sl, iss, nb: (
        jnp.maximum(jnp.minimum(i, nb[0] - 1), 0), 0)
    grid_spec = pltpu.PrefetchScalarGridSpec(
        num_scalar_prefetch=7,
        grid=(rows // bm,),
        in_specs=[
            pl.BlockSpec((bm, dw), last),
            pl.BlockSpec((1, 1, dff2), lambda i, be, fi, ki, nx, sl, iss, nb: (be[i], 0, 0)),
            pl.BlockSpec((1, 1, d), lambda i, be, fi, ki, nx, sl, iss, nb: (be[i], 0, 0)),
            pl.BlockSpec(memory_space=pl.ANY),
            pl.BlockSpec(memory_space=pl.ANY),
        ],
        out_specs=pl.BlockSpec((bm, dw), lambda i, be, fi, ki, nx, sl, iss, nb: (i, 0)),
        scratch_shapes=[pltpu.VMEM((2, d, dff2), F32), pltpu.VMEM((2, dff, d), F32),
                        pltpu.VMEM((d, dff2), BF16), pltpu.VMEM((dff, d), BF16),
                        pltpu.SemaphoreType.DMA((2, 2))],
    )
    return pl.pallas_call(
        _experts_kernel,
        grid_spec=grid_spec,
        out_shape=jax.ShapeDtypeStruct((rows, dw), jnp.uint32),
        compiler_params=_cparams(1),
        name="experts",
    )(be, first, kidx, nxt, slot, issued, nb_used, xs, b_gu.reshape(ne, 1, dff2),
      b_down.reshape(ne, 1, d), w_gu, w_down)


MIXER_TILE = 512
ATTN_HEADS_PER_STEP = 2
POST_TILE = 512
DEST_TILE = 2048
COMBINE_TILE = 512
EXPERT_BLOCK = 256


def kernel(x, mem, norm_mix_g, w_in, lambda_q1, lambda_k1, lambda_q2, lambda_k2, rel_bias,
           subln_g, w_attn_up, pool_mix, pool_scale, w_pool_up, w_gate, b_gate, w_out,
           norm_x_g, norm_mem_g, w_xq, w_xkv, w_xo, norm_ffn_g, w_router, b_router,
           w_gu, b_gu, w_down, b_down, final_norm_g):
    b, s, d = x.shape
    n = b * s
    assert w_in.shape[0] == 1, "single-layer block"
    row = lambda a: a.reshape(1, -1)
    bf = lambda a: a[0].astype(BF16)

    q, k, v, g0, gyb = _mixer_in(
        x, row(norm_mix_g[0]), bf(w_in), bf(w_gate), row(b_gate[0]), bf(pool_mix),
        row(pool_scale[0]), bf(w_pool_up), MIXER_TILE)

    bias, lam = _rel_bias(rel_bias, row(lambda_q1[0]), row(lambda_k1[0]),
                          row(lambda_q2[0]), row(lambda_k2[0]), s)
    a = _diff_attn(q, k, v, bias, lam, row(subln_g[0]), ATTN_HEADS_PER_STEP)

    k_mem, v_mem = _mem_kv(mem, row(norm_mem_g[0]), bf(w_xkv))

    x2, hf, eidx, rank, gate_tm, counts = _post(
        x.reshape(n, d), a.reshape(n, d), g0, gyb, bf(w_attn_up), bf(w_out),
        row(norm_x_g[0]), bf(w_xq), k_mem, v_mem, bf(w_xo), row(norm_ffn_g[0]),
        w_router[0].T, b_router[0].reshape(-1, 1), b, POST_TILE)

    bm = EXPERT_BLOCK
    rows = n * TOP_K + N_EXPERTS * bm
    pstart, nb_used, be, first, kidx, nxt, slot, issued = _expert_schedule(
        counts[:, 0], bm, rows // bm)

    dest_w = _worker_index_layout(_route_dest(pstart, eidx, rank, DEST_TILE), n)
    xs = _sc_dispatch(hf, dest_w, rows)
    ys = _experts(be, first, kidx, nxt, slot, issued, nb_used, xs, w_gu[0], b_gu[0], w_down[0],
                  b_down[0], bm)
    yg = _sc_gather(ys, dest_w, n)
    out = _combine(x2, gate_tm, row(final_norm_g), yg, COMBINE_TILE)
    return out.reshape(b, s, d)
```

```python
import functools
import math

import numpy as np
import jax
import jax.numpy as jnp
from jax import lax
from jax.experimental import pallas as pl
from jax.experimental.pallas import tpu as pltpu
from jax.experimental.pallas import tpu_sc as plsc

F32 = jnp.float32
BF16 = jnp.bfloat16
I32 = jnp.int32

EPS = 1e-6
CHUNK = 64
N_HEADS = 8
HEAD_DIM = 64
HEAD_W = 2 * HEAD_DIM
POOL_WINDOWS = (2, 4, 8, 16)
POOL_GROUP = 128
POOL_PAD = 16
N_BUCKETS = 32
MAX_DISTANCE = 128
X_HEADS = 4
N_EXPERTS = 32
TOP_K = 4
SWIGLU_ALPHA = 1.702
SWIGLU_LIMIT = 7.0
LAMBDA_INIT = 0.8 - 0.6 * math.exp(-0.3 * 0)

LANES = 128
SUBLANES = 8
ATTN_BLOCK = 256
VMEM_LIMIT = 56 * 1024 * 1024


def _cparams(n_axes, vmem=VMEM_LIMIT):
    return pltpu.CompilerParams(
        dimension_semantics=("arbitrary",) * n_axes, vmem_limit_bytes=vmem)


def _rms(xf, g):
    ms = jnp.mean(xf * xf, axis=-1, keepdims=True)
    return xf * lax.rsqrt(ms + EPS) * g


def _dot(a, b):
    return jnp.dot(a, b, preferred_element_type=F32)


def _pack_halves(x):
    c = x.shape[1] // 2
    as_bits = lambda v: lax.bitcast_convert_type(v.astype(BF16).astype(F32), jnp.uint32)
    return as_bits(x[:, :c]) | (as_bits(x[:, c:]) >> 16)


def _unpack_halves(w):
    hi = lax.bitcast_convert_type(w & jnp.uint32(0xFFFF0000), F32)
    lo = lax.bitcast_convert_type(w << 16, F32)
    return hi, lo


def _dot_nt(a, b, precision=None):
    return lax.dot_general(a, b, (((1,), (1,)), ((), ())),
                           preferred_element_type=F32, precision=precision)


def _mixer_in_kernel(x_ref, g_ref, wq_ref, wk_ref, wv_ref, wu_ref, wg_ref, bg_ref,
                     pmix_ref, pscale_ref, wpu_ref,
                     q_ref, k_ref, v_ref, g0_ref, gyb_ref, ext_ref):
    ts = x_ref.shape[1]
    d = x_ref.shape[2]
    j = pl.program_id(1)
    h = _rms(x_ref[0], g_ref[...]).astype(BF16)
    q_ref[0] = (_dot(h, wq_ref[...]) * (HEAD_DIM ** -0.5)).astype(BF16)
    k_ref[0] = _dot(h, wk_ref[...]).astype(BF16)
    v_ref[0] = _dot(h, wv_ref[...]).astype(BF16)
    u = _dot(h, wu_ref[...])

    @pl.when(j == 0)
    def _():
        ext_ref[0:POOL_PAD, :] = jnp.zeros((POOL_PAD, u.shape[1]), F32)

    ext_ref[POOL_PAD:POOL_PAD + ts, :] = u
    e = ext_ref[...]
    sums = {}
    s = e
    w = 1
    while w < max(POOL_WINDOWS):
        s = s + pltpu.roll(s, w, 0)
        w *= 2
        sums[w] = s
    ext_ref[0:POOL_PAD, :] = ext_ref[ts:ts + POOL_PAD, :]

    pos = (j * ts + lax.broadcasted_iota(I32, (ts, 1), 0) + 1).astype(F32)
    mixed = []
    for gi, w in enumerate(POOL_WINDOWS):
        sl = slice(gi * POOL_GROUP, (gi + 1) * POOL_GROUP)
        win = sums[w][POOL_PAD:, sl]
        pooled = win / jnp.minimum(pos, float(w)) - u[:, sl]
        mixed.append(_dot(pooled.astype(BF16), pmix_ref[gi]) * pscale_ref[:, sl])
    mixed = jnp.concatenate(mixed, axis=1).astype(BF16)
    y_b = _dot(mixed, wpu_ref[...])

    gate = jax.nn.sigmoid(_dot(h, wg_ref[...]) + bg_ref[...])
    g0_ref[...] = gate[:, :d].astype(BF16)
    gyb_ref[...] = (gate[:, d:] * y_b).astype(BF16)


def _mixer_in(x, norm_g, w_in, w_gate, b_gate, pool_mix, pool_scale, w_pool_up, ts):
    b, s, d = x.shape
    aw = N_HEADS * HEAD_W
    pw = len(POOL_WINDOWS) * POOL_GROUP
    n = b * s
    nt = s // ts
    const = lambda *shape: pl.BlockSpec(shape, lambda bi, j: (0,) * len(shape))
    tok3 = pl.BlockSpec((1, ts, aw), lambda bi, j: (bi, j, 0))
    tok2 = pl.BlockSpec((ts, d), lambda bi, j: (bi * nt + j, 0))
    return pl.pallas_call(
        _mixer_in_kernel,
        grid=(b, nt),
        in_specs=[
            pl.BlockSpec((1, ts, d), lambda bi, j: (bi, j, 0)),
            const(1, d),
            pl.BlockSpec((d, aw), lambda bi, j: (0, 0)),
            pl.BlockSpec((d, aw), lambda bi, j: (0, 1)),
            pl.BlockSpec((d, aw), lambda bi, j: (0, 2)),
            pl.BlockSpec((d, pw), lambda bi, j: (0, 3 * aw // pw)),
            const(d, 2 * d),
            const(1, 2 * d),
            const(len(POOL_WINDOWS), POOL_GROUP, POOL_GROUP),
            const(1, pw),
            const(pw, d),
        ],
        out_specs=[tok3, tok3, tok3, tok2, tok2],
        out_shape=[jax.ShapeDtypeStruct((b, s, aw), BF16)] * 3
        + [jax.ShapeDtypeStruct((n, d), BF16)] * 2,
        scratch_shapes=[pltpu.VMEM((ts + POOL_PAD, pw), F32)],
        compiler_params=_cparams(2),
        name="mixer_in",
    )(x, norm_g, w_in, w_in, w_in, w_in, w_gate, b_gate, pool_mix, pool_scale, w_pool_up)


def _rel_bucket(rel, log=jnp.log, f32=lambda a: a.astype(jnp.float32),
                i32=lambda a: a.astype(jnp.int32), xp=jnp):
    nb = N_BUCKETS // 2
    ret = i32(rel > 0) * nb
    n = xp.abs(rel)
    max_exact = nb // 2
    nf = f32(xp.maximum(n, 1))
    large = max_exact + i32(log(nf / max_exact) / math.log(MAX_DISTANCE / max_exact)
                            * (nb - max_exact))
    large = xp.minimum(large, nb - 1)
    return ret + xp.where(n < max_exact, n, large)


def _far_bucket(block, seq):
    rel = -np.arange(block + 1, max(seq, block + 2), dtype=np.int32)
    bk = _rel_bucket(rel, log=np.log, f32=lambda a: a.astype(np.float32),
                     i32=lambda a: a.astype(np.int32), xp=np)
    assert (bk == bk[0]).all(), "far keys must share one relative-position bucket"
    return int(bk[0])


def _rel_bias_kernel(far_bucket, tab_ref, bidx_ref, lq1_ref, lk1_ref, lq2_ref, lk2_ref,
                     bias_ref, lam_ref):
    h = pl.program_id(0)
    bidx = bidx_ref[...]
    acc = jnp.zeros(bidx.shape, F32)
    for bkt in range(N_BUCKETS):
        acc = jnp.where(bidx == bkt, tab_ref[bkt, h], acc)
    acc = acc - tab_ref[far_bucket, h]
    bias_ref[0] = jnp.where(bidx < 0, -jnp.inf, acc)
    lam = (jnp.exp(jnp.sum(lq1_ref[...] * lk1_ref[...], keepdims=True))
           - jnp.exp(jnp.sum(lq2_ref[...] * lk2_ref[...], keepdims=True)) + LAMBDA_INIT)
    lam_ref[...] = jnp.broadcast_to(lam, lam_ref.shape)


def _rel_bias(rel_bias, lq1, lk1, lq2, lk2, seq):
    blk = ATTN_BLOCK
    qpos = jnp.arange(blk, dtype=I32)[:, None]
    kpos = jnp.arange(blk, dtype=I32)[None, :]
    diag = jnp.where(kpos // CHUNK <= qpos // CHUNK, _rel_bucket(kpos - qpos), -1)
    prev = _rel_bucket(kpos - (qpos + blk))
    bidx = jnp.stack([diag, prev]).astype(I32)
    vec = pl.BlockSpec((1, HEAD_DIM), lambda h: (0, 0))
    return pl.pallas_call(
        functools.partial(_rel_bias_kernel, _far_bucket(blk, seq)),
        grid=(N_HEADS,),
        in_specs=[
            pl.BlockSpec(memory_space=pltpu.SMEM),
            pl.BlockSpec((2, blk, blk), lambda h: (0, 0, 0)),
            vec, vec, vec, vec,
        ],
        out_specs=[
            pl.BlockSpec((1, 2, blk, blk), lambda h: (h, 0, 0, 0)),
            pl.BlockSpec((SUBLANES, LANES), lambda h: (0, 0)),
        ],
        out_shape=[jax.ShapeDtypeStruct((N_HEADS, 2, blk, blk), F32),
                   jax.ShapeDtypeStruct((SUBLANES, LANES), F32)],
        compiler_params=_cparams(1),
        name="rel_bias",
    )(rel_bias, bidx, lq1, lk1, lq2, lk2)


def _diff_attn_kernel(q_ref, k_ref, v_ref, bias_ref, lam_ref, sg_ref, o_ref, vext_ref):
    s_len = q_ref.shape[1]
    tq = ATTN_BLOCK
    n_heads = q_ref.shape[2] // HEAD_W
    lam = lam_ref[0:1, 0:1]
    lane = lax.broadcasted_iota(I32, (tq, HEAD_W), 1)
    cols = [slice(hh * HEAD_W, (hh + 1) * HEAD_W) for hh in range(n_heads)]
    b_diag, b_prev = [], []
    for hh in range(n_heads):
        vext_ref[hh, :, :HEAD_W] = v_ref[0, :, cols[hh]]
        vext_ref[hh, :, HEAD_W:] = jnp.ones((s_len, HEAD_W), BF16)
        b_diag.append(jnp.concatenate([bias_ref[hh, 0]] * 2, axis=0))
        b_prev.append(jnp.concatenate([bias_ref[hh, 1]] * 2, axis=0))

    for qi in range(s_len // tq):
        n_keys = (qi + 1) * tq
        rows = slice(qi * tq, (qi + 1) * tq)
        for hh in range(n_heads):
            q = q_ref[0, rows, cols[hh]]
            zero = jnp.zeros_like(q)
            qs = jnp.concatenate([jnp.where(lane < HEAD_DIM, q, zero),
                                  jnp.where(lane >= HEAD_DIM, q, zero)], axis=0)
            s = _dot_nt(qs, k_ref[0, :n_keys, cols[hh]])
            pieces = []
            if qi >= 2:
                pieces.append(s[:, :n_keys - 2 * tq])
            if qi >= 1:
                pieces.append(s[:, n_keys - 2 * tq:n_keys - tq] + b_prev[hh])
            pieces.append(s[:, n_keys - tq:] + b_diag[hh])
            s = jnp.concatenate(pieces, axis=1) if len(pieces) > 1 else pieces[0]
            m = jnp.max(s, axis=-1, keepdims=True)
            p = jnp.exp(s - m).astype(BF16)
            acc = _dot(p, vext_ref[hh, :n_keys, :])
            o = acc[:, :HEAD_W] / acc[:, HEAD_W:HEAD_W + 1]
            a = o[:tq] - lam * o[tq:]
            o_ref[0, rows, cols[hh]] = (
                _rms(a, sg_ref[...]) * (1.0 - LAMBDA_INIT)).astype(BF16)


def _diff_attn(q, k, v, bias, lam, subln_g, heads_per_step):
    b, s, aw = q.shape
    tq = ATTN_BLOCK
    hp = heads_per_step
    spec = pl.BlockSpec((1, s, hp * HEAD_W), lambda bi, h: (bi, 0, h))
    return pl.pallas_call(
        _diff_attn_kernel,
        grid=(b, N_HEADS // hp),
        in_specs=[
            spec, spec, spec,
            pl.BlockSpec((hp, 2, tq, tq), lambda bi, h: (h, 0, 0, 0)),
            pl.BlockSpec((SUBLANES, LANES), lambda bi, h: (0, 0)),
            pl.BlockSpec((1, HEAD_W), lambda bi, h: (0, 0)),
        ],
        out_specs=spec,
        out_shape=jax.ShapeDtypeStruct((b, s, aw), BF16),
        scratch_shapes=[pltpu.VMEM((hp, s, 2 * HEAD_W), BF16)],
        compiler_params=_cparams(2),
        name="diff_attn",
    )(q, k, v, bias, lam, subln_g)


def _mem_kv_kernel(m_ref, g_ref, w_ref, k_ref, v_ref):
    d = m_ref.shape[2]
    m = _rms(m_ref[0], g_ref[...]).astype(BF16)
    kv = _dot(m, w_ref[...])
    k_ref[0] = kv[:, :d].astype(BF16)
    v_ref[0] = kv[:, d:].astype(BF16)


def _mem_kv(mem, norm_g, w_xkv):
    b, ml, d = mem.shape
    blk = pl.BlockSpec((1, ml, d), lambda bi: (bi, 0, 0))
    return pl.pallas_call(
        _mem_kv_kernel,
        grid=(b,),
        in_specs=[blk, pl.BlockSpec((1, d), lambda bi: (0, 0)),
                  pl.BlockSpec((d, 2 * d), lambda bi: (0, 0))],
        out_specs=[blk, blk],
        out_shape=[jax.ShapeDtypeStruct((b, ml, d), BF16)] * 2,
        compiler_params=_cparams(1),
        name="mem_kv",
    )(mem, norm_g, w_xkv)


def _post_kernel(x_ref, a_ref, g0_ref, gyb_ref, wau_ref, wo_ref, nxg_ref, wxq_ref,
                 km_ref, vm_ref, wxo_ref, nfg_ref, wrt_ref, br_ref,
                 x2_ref, hf_ref, eidx_ref, rank_ref, gate_ref, cnt_ref, run_ref):
    ts, d = x_ref.shape
    first = (pl.program_id(0) == 0) & (pl.program_id(1) == 0)

    y_a = _dot(a_ref[...], wau_ref[...])
    merged = g0_ref[...].astype(F32) * y_a + gyb_ref[...].astype(F32)
    x1 = x_ref[...] + _dot(merged.astype(BF16), wo_ref[...])

    hd = d // X_HEADS
    hx = _rms(x1, nxg_ref[...]).astype(BF16)
    qx = (_dot(hx, wxq_ref[...]) * (hd ** -0.5)).astype(BF16)
    heads = []
    for hh in range(X_HEADS):
        sl = slice(hh * hd, (hh + 1) * hd)
        s = _dot_nt(qx[:, sl], km_ref[0, :, sl])
        p = jnp.exp(s - jnp.max(s, axis=-1, keepdims=True))
        p = p / jnp.sum(p, axis=-1, keepdims=True)
        heads.append(_dot(p.astype(BF16), vm_ref[0, :, sl]))
    o = jnp.concatenate(heads, axis=1).astype(BF16)
    x2 = x1 + _dot(o, wxo_ref[...])
    x2_ref[...] = x2
    hf = _rms(x2, nfg_ref[...])
    hf_ref[...] = _pack_halves(hf)

    logits = _dot_nt(wrt_ref[...], hf, precision=lax.Precision.HIGHEST) + br_ref[...]
    ne = logits.shape[0]
    eid = lax.broadcasted_iota(I32, logits.shape, 0).astype(F32)
    work = logits
    vals, idxs, hots = [], [], []
    for _ in range(TOP_K):
        mx = jnp.max(work, axis=0, keepdims=True)
        idx = jnp.min(jnp.where(work == mx, eid, float(ne)), axis=0, keepdims=True)
        hot = eid == idx
        vals.append(mx)
        idxs.append(idx.astype(I32))
        hots.append(hot)
        work = jnp.where(hot, -jnp.inf, work)
    ex = [jnp.exp(vv - vals[0]) for vv in vals]
    den = ex[0] + ex[1] + ex[2] + ex[3]
    gates = [e_ / den for e_ in ex]

    multi = (hots[0] | hots[1] | hots[2] | hots[3])
    multi_f = jnp.where(multi, 1.0, 0.0).astype(F32)
    tri = (lax.broadcasted_iota(I32, (ts, ts), 0)
           < lax.broadcasted_iota(I32, (ts, ts), 1))
    before = _dot(multi_f.astype(BF16), jnp.where(tri, 1.0, 0.0).astype(BF16))

    @pl.when(first)
    def _():
        run_ref[...] = jnp.zeros(run_ref.shape, F32)

    run = run_ref[...]
    pos = before + run
    ranks = [jnp.sum(jnp.where(hot, pos, 0.0), axis=0, keepdims=True) for hot in hots]
    run_new = run + jnp.sum(multi_f, axis=1, keepdims=True)
    run_ref[...] = run_new
    cnt_ref[...] = jnp.broadcast_to(run_new, cnt_ref.shape).astype(I32)

    eidx_ref[...] = jnp.concatenate(idxs, axis=0)
    rank_ref[...] = jnp.concatenate(ranks, axis=0).astype(I32)
    g_rows = jnp.concatenate(gates + [jnp.zeros((LANES - TOP_K, ts), F32)], axis=0)
    gate_ref[...] = g_rows.T


def _post(x2d, a2d, g0, gyb, w_attn_up, w_out, norm_x_g, w_xq, k_mem, v_mem, w_xo,
          norm_ffn_g, w_router_t, b_router, batch, ts):
    n, d = x2d.shape
    nt = n // batch // ts
    ml = k_mem.shape[1]
    tok = pl.BlockSpec((ts, d), lambda bi, j: (bi * nt + j, 0))
    lanes = pl.BlockSpec((TOP_K, ts), lambda bi, j: (0, bi * nt + j))
    const = lambda *shape: pl.BlockSpec(shape, lambda bi, j: (0,) * len(shape),
                                        pipeline_mode=pl.Buffered(1))
    mem = pl.BlockSpec((1, ml, d), lambda bi, j: (bi, 0, 0))
    return pl.pallas_call(
        _post_kernel,
        grid=(batch, nt),
        in_specs=[tok, tok, tok, tok, const(d, d), const(d, d), const(1, d), const(d, d),
                  mem, mem, const(d, d), const(1, d), const(N_EXPERTS, d),
                  const(N_EXPERTS, 1)],
        out_specs=[tok, pl.BlockSpec((ts, d // 2), lambda bi, j: (bi * nt + j, 0)),
                   lanes, lanes,
                   pl.BlockSpec((ts, LANES), lambda bi, j: (bi * nt + j, 0)),
                   pl.BlockSpec((N_EXPERTS, LANES), lambda bi, j: (0, 0))],
        out_shape=[jax.ShapeDtypeStruct((n, d), F32),
                   jax.ShapeDtypeStruct((n, d // 2), jnp.uint32),
                   jax.ShapeDtypeStruct((TOP_K, n), I32), jax.ShapeDtypeStruct((TOP_K, n), I32),
                   jax.ShapeDtypeStruct((n, LANES), F32),
                   jax.ShapeDtypeStruct((N_EXPERTS, LANES), I32)],
        scratch_shapes=[pltpu.VMEM((N_EXPERTS, 1), F32)],
        compiler_params=_cparams(2),
        name="post",
    )(x2d, a2d, g0, gyb, w_attn_up, w_out, norm_x_g, w_xq, k_mem, v_mem, w_xo,
      norm_ffn_g, w_router_t, b_router)


def _route_dest_kernel(pstart_ref, eidx_ref, rank_ref, dest_ref):
    eidx = eidx_ref[...]
    start = jnp.zeros(eidx.shape, I32)
    for e in range(N_EXPERTS):
        start = jnp.where(eidx == e, pstart_ref[e], start)
    dest_ref[...] = start + rank_ref[...]


def _route_dest(pstart, eidx, rank, tl):
    k, n = eidx.shape
    lanes = pl.BlockSpec((k, tl), lambda i: (0, i))
    return pl.pallas_call(
        _route_dest_kernel,
        grid=(n // tl,),
        in_specs=[pl.BlockSpec(memory_space=pltpu.SMEM), lanes, lanes],
        out_specs=lanes,
        out_shape=jax.ShapeDtypeStruct((k, n), I32),
        compiler_params=_cparams(1),
        name="route_dest",
    )(pstart, eidx, rank)


SC_CORES = 2
SC_SUBCORES = 16
SC_WORKERS = SC_CORES * SC_SUBCORES
MOVE_CHUNK = 64
MOVE_BUFFERS = 3


def _worker_index_layout(dest, n):
    per_w = n // SC_WORKERS
    n_chunks = per_w // MOVE_CHUNK
    d4 = dest.reshape(TOP_K, SC_WORKERS, n_chunks, MOVE_CHUNK)
    return jnp.transpose(d4, (1, 2, 0, 3)).reshape(SC_WORKERS, n_chunks * TOP_K, MOVE_CHUNK)


def _sc_mesh():
    return plsc.VectorSubcoreMesh(core_axis_name="c", subcore_axis_name="s",
                                  num_cores=SC_CORES, num_subcores=SC_SUBCORES)


def _sc_dispatch(hf, dest_w, rows):
    n, d = hf.shape
    per_w = n // SC_WORKERS
    n_chunks = per_w // MOVE_CHUNK

    def body(hf_hbm, dest_hbm, xs_hbm, idx_v, rows_v, rsem, wsem):
        wid = lax.axis_index("s") * SC_CORES + lax.axis_index("c")
        base = wid * per_w
        pltpu.sync_copy(dest_hbm.at[wid], idx_v)

        def read(c):
            b = c % MOVE_BUFFERS
            return pltpu.async_copy(hf_hbm.at[pl.ds(base + c * MOVE_CHUNK, MOVE_CHUNK)],
                                    rows_v.at[b], rsem.at[b])

        scatters = [[] for _ in range(MOVE_BUFFERS)]
        pending = read(0)
        for c in range(n_chunks):
            b = c % MOVE_BUFFERS
            pending.wait()
            if c + 1 < n_chunks:
                for cp in scatters[(c + 1) % MOVE_BUFFERS]:
                    cp.wait()
                pending = read(c + 1)
            scatters[b] = [
                pltpu.async_copy(rows_v.at[b], xs_hbm.at[idx_v.at[c * TOP_K + kk]], wsem.at[b])
                for kk in range(TOP_K)]
        for group in scatters:
            for cp in group:
                cp.wait()

    return pl.kernel(
        body,
        out_type=jax.ShapeDtypeStruct((rows, d), hf.dtype),
        mesh=_sc_mesh(),
        scratch_types=[pltpu.VMEM((n_chunks * TOP_K, MOVE_CHUNK), I32),
                       pltpu.VMEM((MOVE_BUFFERS, MOVE_CHUNK, d), hf.dtype),
                       pltpu.SemaphoreType.DMA((MOVE_BUFFERS,)),
                       pltpu.SemaphoreType.DMA((MOVE_BUFFERS,))],
        name="sc_dispatch",
    )(hf, dest_w)


def _sc_gather(ys, dest_w, n):
    _, d = ys.shape
    per_w = n // SC_WORKERS
    n_chunks = per_w // MOVE_CHUNK

    def body(ys_hbm, dest_hbm, yg_hbm, idx_v, rows_v, rsem, wsem):
        wid = lax.axis_index("s") * SC_CORES + lax.axis_index("c")
        base = wid * per_w
        pltpu.sync_copy(dest_hbm.at[wid], idx_v)
        n_moves = n_chunks * TOP_K

        def gather(m):
            b = m % MOVE_BUFFERS
            return pltpu.async_copy(ys_hbm.at[idx_v.at[m]], rows_v.at[b], rsem.at[b])

        def write(m):
            b = m % MOVE_BUFFERS
            c, kk = divmod(m, TOP_K)
            return pltpu.async_copy(
                rows_v.at[b], yg_hbm.at[pl.ds(kk * n + base + c * MOVE_CHUNK, MOVE_CHUNK)],
                wsem.at[b])

        writes = [None] * MOVE_BUFFERS
        pending = gather(0)
        for m in range(n_moves):
            pending.wait()
            if m + 1 < n_moves:
                nb = (m + 1) % MOVE_BUFFERS
                if writes[nb] is not None:
                    writes[nb].wait()
                pending = gather(m + 1)
            writes[m % MOVE_BUFFERS] = write(m)
        for wr in writes:
            if wr is not None:
                wr.wait()

    return pl.kernel(
        body,
        out_type=jax.ShapeDtypeStruct((TOP_K * n, d), ys.dtype),
        mesh=_sc_mesh(),
        scratch_types=[pltpu.VMEM((n_chunks * TOP_K, MOVE_CHUNK), I32),
                       pltpu.VMEM((MOVE_BUFFERS, MOVE_CHUNK, d), ys.dtype),
                       pltpu.SemaphoreType.DMA((MOVE_BUFFERS,)),
                       pltpu.SemaphoreType.DMA((MOVE_BUFFERS,))],
        name="sc_gather",
    )(ys, dest_w)


def _combine_kernel(x2_ref, gate_ref, fg_ref, yg_ref, o_ref):
    g = gate_ref[...]
    x2 = x2_ref[...]
    c = x2.shape[1] // 2
    acc_hi, acc_lo = x2[:, :c], x2[:, c:]
    for kk in range(TOP_K):
        hi, lo = _unpack_halves(yg_ref[kk])
        acc_hi = acc_hi + g[:, kk:kk + 1] * hi
        acc_lo = acc_lo + g[:, kk:kk + 1] * lo
    o_ref[...] = _rms(jnp.concatenate([acc_hi, acc_lo], axis=1), fg_ref[...])


def _combine(x2, gate_tm, final_g, yg, ts):
    n, d = x2.shape
    dw = yg.shape[1]
    return pl.pallas_call(
        _combine_kernel,
        grid=(n // ts,),
        in_specs=[pl.BlockSpec((ts, d), lambda i: (i, 0)),
                  pl.BlockSpec((ts, LANES), lambda i: (i, 0)),
                  pl.BlockSpec((1, d), lambda i: (0, 0)),
                  pl.BlockSpec((TOP_K, ts, dw), lambda i: (0, i, 0))],
        out_specs=pl.BlockSpec((ts, d), lambda i: (i, 0)),
        out_shape=jax.ShapeDtypeStruct((n, d), F32),
        compiler_params=_cparams(1),
        name="combine",
    )(x2, gate_tm, final_g, yg.reshape(TOP_K, n, dw))


def _experts_kernel(be_ref, first_ref, nxt_ref, slot_ref, nb_ref,
                    xs_ref, bgu_ref, bd_ref, wgu_hbm, wd_hbm, ys_ref,
                    wgu_st, wd_st, wgu_bf, wd_bf, sem_w):
    dff = wd_bf.shape[0]
    bm = xs_ref.shape[0] // EXPERT_BLOCKS_PER_STEP

    def fetch(ex, sl):
        return (pltpu.make_async_copy(wgu_hbm.at[ex], wgu_st.at[sl], sem_w.at[0, sl]),
                pltpu.make_async_copy(wd_hbm.at[ex], wd_st.at[sl], sem_w.at[1, sl]))

    @pl.when(pl.program_id(0) == 0)
    def _():
        e0 = be_ref[0]
        for cp in fetch(e0, slot_ref[e0]):
            cp.start()

    for sub in range(EXPERT_BLOCKS_PER_STEP):
        blk = pl.program_id(0) * EXPERT_BLOCKS_PER_STEP + sub
        e = be_ref[blk]
        rows = slice(sub * bm, (sub + 1) * bm)

        @pl.when(first_ref[blk] == 1)
        def _(e=e):
            sl = slot_ref[e]
            for cp in fetch(e, sl):
                cp.wait()
            nx = nxt_ref[e]

            @pl.when(nx >= 0)
            def _():
                for cp in fetch(nx, 1 - sl):
                    cp.start(priority=1)

            wgu_bf[...] = wgu_st[sl].astype(BF16)
            wd_bf[...] = wd_st[sl].astype(BF16)

        @pl.when(blk < nb_ref[0])
        def _(e=e, rows=rows):
            xb = jnp.concatenate(_unpack_halves(xs_ref[rows, :]), axis=1).astype(BF16)
            gu = _dot(xb, wgu_bf[...]) + bgu_ref[e]
            x_glu = jnp.minimum(gu[:, :dff], SWIGLU_LIMIT)
            x_lin = jnp.clip(gu[:, dff:], -SWIGLU_LIMIT, SWIGLU_LIMIT)
            act = x_glu * jax.nn.sigmoid(SWIGLU_ALPHA * x_glu) * (x_lin + 1.0)
            ys_ref[rows, :] = _pack_halves(_dot(act.astype(BF16), wd_bf[...]) + bd_ref[e])

        @pl.when(blk >= nb_ref[0])
        def _(rows=rows):
            ys_ref[rows, :] = jnp.zeros((bm, ys_ref.shape[1]), jnp.uint32)


def _expert_schedule(counts, bm, n_blocks):
    ne = counts.shape[0]
    ids = jnp.arange(ne, dtype=I32)
    upto = ids[None, :] <= ids[:, None]
    padded = (counts + bm - 1) // bm * bm
    pend = jnp.sum(jnp.where(upto, padded[None, :], 0), axis=1)
    pstart = (pend - padded).astype(I32)
    nb_used = (jnp.sum(padded) // bm).astype(I32)
    blk = jnp.arange(n_blocks, dtype=I32)
    blk_c = jnp.minimum(blk, nb_used - 1)
    be = jnp.minimum(jnp.sum((pend[None, :] <= (blk_c * bm)[:, None]).astype(I32), axis=1),
                     ne - 1)
    started = jnp.sum(((pstart[None, :] == (blk * bm)[:, None]) & (padded[None, :] > 0))
                      .astype(I32), axis=1)
    first = ((blk < nb_used) & (started > 0)).astype(I32)
    used = counts > 0
    seq = jnp.sum(jnp.where(upto & used[None, :], 1, 0), axis=1) - 1
    later = used[None, :] & (ids[None, :] > ids[:, None])
    nxt = jnp.where(jnp.any(later, axis=1), jnp.argmax(later, axis=1), -1).astype(I32)
    slot = (seq & 1).astype(I32)
    return pstart, nb_used.reshape(1), be, first, nxt, slot


def _experts(be, first, nxt, slot, nb_used, xs, w_gu, b_gu, w_down, b_down, bm):
    rows, dw = xs.shape
    ne, d, dff2 = w_gu.shape
    dff = dff2 // 2
    sb = EXPERT_BLOCKS_PER_STEP * bm
    assert rows % sb == 0
    last = lambda j, be, fi, nx, sl, nb: (
        jnp.maximum(jnp.minimum(j, (nb[0] - 1) // EXPERT_BLOCKS_PER_STEP), 0), 0)
    const = lambda *shape: pl.BlockSpec(shape, lambda j, be, fi, nx, sl, nb: (0,) * len(shape))
    grid_spec = pltpu.PrefetchScalarGridSpec(
        num_scalar_prefetch=5,
        grid=(rows // sb,),
        in_specs=[
            pl.BlockSpec((sb, dw), last),
            const(ne, 1, dff2),
            const(ne, 1, d),
            pl.BlockSpec(memory_space=pl.ANY),
            pl.BlockSpec(memory_space=pl.ANY),
        ],
        out_specs=pl.BlockSpec((sb, dw), lambda j, be, fi, nx, sl, nb: (j, 0)),
        scratch_shapes=[pltpu.VMEM((2, d, dff2), F32), pltpu.VMEM((2, dff, d), F32),
                        pltpu.VMEM((d, dff2), BF16), pltpu.VMEM((dff, d), BF16),
                        pltpu.SemaphoreType.DMA((2, 2))],
    )
    return pl.pallas_call(
        _experts_kernel,
        grid_spec=grid_spec,
        out_shape=jax.ShapeDtypeStruct((rows, dw), jnp.uint32),
        compiler_params=_cparams(1),
        name="experts",
    )(be, first, nxt, slot, nb_used, xs, b_gu.reshape(ne, 1, dff2),
      b_down.reshape(ne, 1, d), w_gu, w_down)


MIXER_TILE = 512
ATTN_HEADS_PER_STEP = 2
POST_TILE = 512
DEST_TILE = 2048
COMBINE_TILE = 512
EXPERT_BLOCK = 256
EXPERT_BLOCKS_PER_STEP = 2


def kernel(x, mem, norm_mix_g, w_in, lambda_q1, lambda_k1, lambda_q2, lambda_k2, rel_bias,
           subln_g, w_attn_up, pool_mix, pool_scale, w_pool_up, w_gate, b_gate, w_out,
           norm_x_g, norm_mem_g, w_xq, w_xkv, w_xo, norm_ffn_g, w_router, b_router,
           w_gu, b_gu, w_down, b_down, final_norm_g):
    b, s, d = x.shape
    n = b * s
    assert w_in.shape[0] == 1, "single-layer block"
    row = lambda a: a.reshape(1, -1)
    bf = lambda a: a[0].astype(BF16)

    q, k, v, g0, gyb = _mixer_in(
        x, row(norm_mix_g[0]), bf(w_in), bf(w_gate), row(b_gate[0]), bf(pool_mix),
        row(pool_scale[0]), bf(w_pool_up), MIXER_TILE)

    bias, lam = _rel_bias(rel_bias, row(lambda_q1[0]), row(lambda_k1[0]),
                          row(lambda_q2[0]), row(lambda_k2[0]), s)
    a = _diff_attn(q, k, v, bias, lam, row(subln_g[0]), ATTN_HEADS_PER_STEP)

    k_mem, v_mem = _mem_kv(mem, row(norm_mem_g[0]), bf(w_xkv))

    x2, hf, eidx, rank, gate_tm, counts = _post(
        x.reshape(n, d), a.reshape(n, d), g0, gyb, bf(w_attn_up), bf(w_out),
        row(norm_x_g[0]), bf(w_xq), k_mem, v_mem, bf(w_xo), row(norm_ffn_g[0]),
        w_router[0].T, b_router[0].reshape(-1, 1), b, POST_TILE)

    bm = EXPERT_BLOCK
    rows = n * TOP_K + N_EXPERTS * bm
    pstart, nb_used, be, first, nxt, slot = _expert_schedule(counts[:, 0], bm, rows // bm)

    dest_w = _worker_index_layout(_route_dest(pstart, eidx, rank, DEST_TILE), n)
    xs = _sc_dispatch(hf, dest_w, rows)
    ys = _experts(be, first, nxt, slot, nb_used, xs, w_gu[0], b_gu[0], w_down[0], b_down[0],
                  bm)
    yg = _sc_gather(ys, dest_w, n)
    out = _combine(x2, gate_tm, row(final_norm_g), yg, COMBINE_TILE)
    return out.reshape(b, s, d)
```

```python
import functools
import math

import numpy as np
import jax
import jax.numpy as jnp
from jax import lax
from jax.experimental import pallas as pl
from jax.experimental.pallas import tpu as pltpu
from jax.experimental.pallas import tpu_sc as plsc

F32 = jnp.float32
BF16 = jnp.bfloat16
I32 = jnp.int32

EPS = 1e-6
CHUNK = 64
N_HEADS = 8
HEAD_DIM = 64
HEAD_W = 2 * HEAD_DIM
POOL_WINDOWS = (2, 4, 8, 16)
POOL_GROUP = 128
POOL_PAD = 16
N_BUCKETS = 32
MAX_DISTANCE = 128
X_HEADS = 4
N_EXPERTS = 32
TOP_K = 4
SWIGLU_ALPHA = 1.702
SWIGLU_LIMIT = 7.0
LAMBDA_INIT = 0.8 - 0.6 * math.exp(-0.3 * 0)

LANES = 128
SUBLANES = 8
ATTN_BLOCK = 256
VMEM_LIMIT = 56 * 1024 * 1024


def _cparams(n_axes, vmem=VMEM_LIMIT):
    return pltpu.CompilerParams(
        dimension_semantics=("arbitrary",) * n_axes, vmem_limit_bytes=vmem)


def _rms(xf, g):
    ms = jnp.mean(xf * xf, axis=-1, keepdims=True)
    return xf * lax.rsqrt(ms + EPS) * g


def _dot(a, b):
    return jnp.dot(a, b, preferred_element_type=F32)


def _pack_halves(x):
    c = x.shape[1] // 2
    as_bits = lambda v: lax.bitcast_convert_type(v.astype(BF16).astype(F32), jnp.uint32)
    return as_bits(x[:, :c]) | (as_bits(x[:, c:]) >> 16)


def _unpack_halves(w):
    hi = lax.bitcast_convert_type(w & jnp.uint32(0xFFFF0000), F32)
    lo = lax.bitcast_convert_type(w << 16, F32)
    return hi, lo


def _dot_nt(a, b, precision=None):
    return lax.dot_general(a, b, (((1,), (1,)), ((), ())),
                           preferred_element_type=F32, precision=precision)


def _mixer_in_kernel(x_ref, g_ref, wq_ref, wk_ref, wv_ref, wu_ref, wg_ref, bg_ref,
                     pmix_ref, pscale_ref, wpu_ref,
                     q_ref, k_ref, v_ref, g0_ref, gyb_ref, ext_ref):
    ts = x_ref.shape[1]
    d = x_ref.shape[2]
    j = pl.program_id(1)
    h = _rms(x_ref[0], g_ref[...]).astype(BF16)
    q_ref[0] = (_dot(h, wq_ref[...]) * (HEAD_DIM ** -0.5)).astype(BF16)
    k_ref[0] = _dot(h, wk_ref[...]).astype(BF16)
    v_ref[0] = _dot(h, wv_ref[...]).astype(BF16)
    u = _dot(h, wu_ref[...])

    @pl.when(j == 0)
    def _():
        ext_ref[0:POOL_PAD, :] = jnp.zeros((POOL_PAD, u.shape[1]), F32)

    ext_ref[POOL_PAD:POOL_PAD + ts, :] = u
    e = ext_ref[...]
    sums = {}
    s = e
    w = 1
    while w < max(POOL_WINDOWS):
        s = s + pltpu.roll(s, w, 0)
        w *= 2
        sums[w] = s
    ext_ref[0:POOL_PAD, :] = ext_ref[ts:ts + POOL_PAD, :]

    pos = (j * ts + lax.broadcasted_iota(I32, (ts, 1), 0) + 1).astype(F32)
    mixed = []
    for gi, w in enumerate(POOL_WINDOWS):
        sl = slice(gi * POOL_GROUP, (gi + 1) * POOL_GROUP)
        win = sums[w][POOL_PAD:, sl]
        pooled = win / jnp.minimum(pos, float(w)) - u[:, sl]
        mixed.append(_dot(pooled.astype(BF16), pmix_ref[gi]) * pscale_ref[:, sl])
    mixed = jnp.concatenate(mixed, axis=1).astype(BF16)
    y_b = _dot(mixed, wpu_ref[...])

    gate = jax.nn.sigmoid(_dot(h, wg_ref[...]) + bg_ref[...])
    g0_ref[...] = gate[:, :d].astype(BF16)
    gyb_ref[...] = (gate[:, d:] * y_b).astype(BF16)


def _mixer_in(x, norm_g, w_in, w_gate, b_gate, pool_mix, pool_scale, w_pool_up, ts):
    b, s, d = x.shape
    aw = N_HEADS * HEAD_W
    pw = len(POOL_WINDOWS) * POOL_GROUP
    n = b * s
    nt = s // ts
    const = lambda *shape: pl.BlockSpec(shape, lambda bi, j: (0,) * len(shape))
    tok3 = pl.BlockSpec((1, ts, aw), lambda bi, j: (bi, j, 0))
    tok2 = pl.BlockSpec((ts, d), lambda bi, j: (bi * nt + j, 0))
    return pl.pallas_call(
        _mixer_in_kernel,
        grid=(b, nt),
        in_specs=[
            pl.BlockSpec((1, ts, d), lambda bi, j: (bi, j, 0)),
            const(1, d),
            pl.BlockSpec((d, aw), lambda bi, j: (0, 0)),
            pl.BlockSpec((d, aw), lambda bi, j: (0, 1)),
            pl.BlockSpec((d, aw), lambda bi, j: (0, 2)),
            pl.BlockSpec((d, pw), lambda bi, j: (0, 3 * aw // pw)),
            const(d, 2 * d),
            const(1, 2 * d),
            const(len(POOL_WINDOWS), POOL_GROUP, POOL_GROUP),
            const(1, pw),
            const(pw, d),
        ],
        out_specs=[tok3, tok3, tok3, tok2, tok2],
        out_shape=[jax.ShapeDtypeStruct((b, s, aw), BF16)] * 3
        + [jax.ShapeDtypeStruct((n, d), BF16)] * 2,
        scratch_shapes=[pltpu.VMEM((ts + POOL_PAD, pw), F32)],
        compiler_params=_cparams(2),
        name="mixer_in",
    )(x, norm_g, w_in, w_in, w_in, w_in, w_gate, b_gate, pool_mix, pool_scale, w_pool_up)


def _rel_bucket(rel, log=jnp.log, f32=lambda a: a.astype(jnp.float32),
                i32=lambda a: a.astype(jnp.int32), xp=jnp):
    nb = N_BUCKETS // 2
    ret = i32(rel > 0) * nb
    n = xp.abs(rel)
    max_exact = nb // 2
    nf = f32(xp.maximum(n, 1))
    large = max_exact + i32(log(nf / max_exact) / math.log(MAX_DISTANCE / max_exact)
                            * (nb - max_exact))
    large = xp.minimum(large, nb - 1)
    return ret + xp.where(n < max_exact, n, large)


def _far_bucket(block, seq):
    rel = -np.arange(block + 1, max(seq, block + 2), dtype=np.int32)
    bk = _rel_bucket(rel, log=np.log, f32=lambda a: a.astype(np.float32),
                     i32=lambda a: a.astype(np.int32), xp=np)
    assert (bk == bk[0]).all(), "far keys must share one relative-position bucket"
    return int(bk[0])


def _rel_bias_kernel(far_bucket, tab_ref, bidx_ref, lq1_ref, lk1_ref, lq2_ref, lk2_ref,
                     bias_ref, lam_ref):
    h = pl.program_id(0)
    bidx = bidx_ref[...]
    acc = jnp.zeros(bidx.shape, F32)
    for bkt in range(N_BUCKETS):
        acc = jnp.where(bidx == bkt, tab_ref[bkt, h], acc)
    acc = acc - tab_ref[far_bucket, h]
    bias_ref[0] = jnp.where(bidx < 0, -jnp.inf, acc)
    lam = (jnp.exp(jnp.sum(lq1_ref[...] * lk1_ref[...], keepdims=True))
           - jnp.exp(jnp.sum(lq2_ref[...] * lk2_ref[...], keepdims=True)) + LAMBDA_INIT)
    lam_ref[...] = jnp.broadcast_to(lam, lam_ref.shape)


def _rel_bias(rel_bias, lq1, lk1, lq2, lk2, seq):
    blk = ATTN_BLOCK
    qpos = jnp.arange(blk, dtype=I32)[:, None]
    kpos = jnp.arange(blk, dtype=I32)[None, :]
    diag = jnp.where(kpos // CHUNK <= qpos // CHUNK, _rel_bucket(kpos - qpos), -1)
    prev = _rel_bucket(kpos - (qpos + blk))
    bidx = jnp.stack([diag, prev]).astype(I32)
    vec = pl.BlockSpec((1, HEAD_DIM), lambda h: (0, 0))
    return pl.pallas_call(
        functools.partial(_rel_bias_kernel, _far_bucket(blk, seq)),
        grid=(N_HEADS,),
        in_specs=[
            pl.BlockSpec(memory_space=pltpu.SMEM),
            pl.BlockSpec((2, blk, blk), lambda h: (0, 0, 0)),
            vec, vec, vec, vec,
        ],
        out_specs=[
            pl.BlockSpec((1, 2, blk, blk), lambda h: (h, 0, 0, 0)),
            pl.BlockSpec((SUBLANES, LANES), lambda h: (0, 0)),
        ],
        out_shape=[jax.ShapeDtypeStruct((N_HEADS, 2, blk, blk), F32),
                   jax.ShapeDtypeStruct((SUBLANES, LANES), F32)],
        compiler_params=_cparams(1),
        name="rel_bias",
    )(rel_bias, bidx, lq1, lk1, lq2, lk2)


def _diff_attn_kernel(q_ref, k_ref, v_ref, bias_ref, lam_ref, sg_ref, o_ref, vext_ref):
    s_len = q_ref.shape[1]
    tq = ATTN_BLOCK
    n_heads = q_ref.shape[2] // HEAD_W
    lam = lam_ref[0:1, 0:1]
    lane = lax.broadcasted_iota(I32, (tq, HEAD_W), 1)
    cols = [slice(hh * HEAD_W, (hh + 1) * HEAD_W) for hh in range(n_heads)]
    b_diag, b_prev = [], []
    for hh in range(n_heads):
        vext_ref[hh, :, :HEAD_W] = v_ref[0, :, cols[hh]]
        vext_ref[hh, :, HEAD_W:] = jnp.ones((s_len, HEAD_W), BF16)
        b_diag.append(jnp.concatenate([bias_ref[hh, 0]] * 2, axis=0))
        b_prev.append(jnp.concatenate([bias_ref[hh, 1]] * 2, axis=0))

    for qi in range(s_len // tq):
        n_keys = (qi + 1) * tq
        rows = slice(qi * tq, (qi + 1) * tq)
        for hh in range(n_heads):
            q = q_ref[0, rows, cols[hh]]
            zero = jnp.zeros_like(q)
            qs = jnp.concatenate([jnp.where(lane < HEAD_DIM, q, zero),
                                  jnp.where(lane >= HEAD_DIM, q, zero)], axis=0)
            s = _dot_nt(qs, k_ref[0, :n_keys, cols[hh]])
            pieces = []
            if qi >= 2:
                pieces.append(s[:, :n_keys - 2 * tq])
            if qi >= 1:
                pieces.append(s[:, n_keys - 2 * tq:n_keys - tq] + b_prev[hh])
            pieces.append(s[:, n_keys - tq:] + b_diag[hh])
            s = jnp.concatenate(pieces, axis=1) if len(pieces) > 1 else pieces[0]
            m = jnp.max(s, axis=-1, keepdims=True)
            p = jnp.exp(s - m).astype(BF16)
            acc = _dot(p, vext_ref[hh, :n_keys, :])
            o = acc[:, :HEAD_W] / acc[:, HEAD_W:HEAD_W + 1]
            a = o[:tq] - lam * o[tq:]
            o_ref[0, rows, cols[hh]] = (
                _rms(a, sg_ref[...]) * (1.0 - LAMBDA_INIT)).astype(BF16)


def _diff_attn(q, k, v, bias, lam, subln_g, heads_per_step):
    b, s, aw = q.shape
    tq = ATTN_BLOCK
    hp = heads_per_step
    spec = pl.BlockSpec((1, s, hp * HEAD_W), lambda bi, h: (bi, 0, h))
    return pl.pallas_call(
        _diff_attn_kernel,
        grid=(b, N_HEADS // hp),
        in_specs=[
            spec, spec, spec,
            pl.BlockSpec((hp, 2, tq, tq), lambda bi, h: (h, 0, 0, 0)),
            pl.BlockSpec((SUBLANES, LANES), lambda bi, h: (0, 0)),
            pl.BlockSpec((1, HEAD_W), lambda bi, h: (0, 0)),
        ],
        out_specs=spec,
        out_shape=jax.ShapeDtypeStruct((b, s, aw), BF16),
        scratch_shapes=[pltpu.VMEM((hp, s, 2 * HEAD_W), BF16)],
        compiler_params=_cparams(2),
        name="diff_attn",
    )(q, k, v, bias, lam, subln_g)


def _mem_kv_kernel(m_ref, g_ref, w_ref, k_ref, v_ref):
    d = m_ref.shape[2]
    m = _rms(m_ref[0], g_ref[...]).astype(BF16)
    kv = _dot(m, w_ref[...])
    k_ref[0] = kv[:, :d].astype(BF16)
    v_ref[0] = kv[:, d:].astype(BF16)


def _mem_kv(mem, norm_g, w_xkv):
    b, ml, d = mem.shape
    blk = pl.BlockSpec((1, ml, d), lambda bi: (bi, 0, 0))
    return pl.pallas_call(
        _mem_kv_kernel,
        grid=(b,),
        in_specs=[blk, pl.BlockSpec((1, d), lambda bi: (0, 0)),
                  pl.BlockSpec((d, 2 * d), lambda bi: (0, 0))],
        out_specs=[blk, blk],
        out_shape=[jax.ShapeDtypeStruct((b, ml, d), BF16)] * 2,
        compiler_params=_cparams(1),
        name="mem_kv",
    )(mem, norm_g, w_xkv)


def _post_kernel(x_ref, a_ref, g0_ref, gyb_ref, wau_ref, wo_ref, nxg_ref, wxq_ref,
                 km_ref, vm_ref, wxo_ref, nfg_ref, wrt_ref, br_ref,
                 x2_ref, hf_ref, eidx_ref, rank_ref, gate_ref, cnt_ref, run_ref):
    ts, d = x_ref.shape
    first = (pl.program_id(0) == 0) & (pl.program_id(1) == 0)

    y_a = _dot(a_ref[...], wau_ref[...])
    merged = g0_ref[...].astype(F32) * y_a + gyb_ref[...].astype(F32)
    x1 = x_ref[...] + _dot(merged.astype(BF16), wo_ref[...])

    hd = d // X_HEADS
    hx = _rms(x1, nxg_ref[...]).astype(BF16)
    qx = (_dot(hx, wxq_ref[...]) * (hd ** -0.5)).astype(BF16)
    heads = []
    for hh in range(X_HEADS):
        sl = slice(hh * hd, (hh + 1) * hd)
        s = _dot_nt(qx[:, sl], km_ref[0, :, sl])
        p = jnp.exp(s - jnp.max(s, axis=-1, keepdims=True))
        p = p / jnp.sum(p, axis=-1, keepdims=True)
        heads.append(_dot(p.astype(BF16), vm_ref[0, :, sl]))
    o = jnp.concatenate(heads, axis=1).astype(BF16)
    x2 = x1 + _dot(o, wxo_ref[...])
    x2_ref[...] = x2
    hf = _rms(x2, nfg_ref[...])
    hf_ref[...] = _pack_halves(hf)

    logits = _dot_nt(wrt_ref[...], hf, precision=lax.Precision.HIGHEST) + br_ref[...]
    ne = logits.shape[0]
    eid = lax.broadcasted_iota(I32, logits.shape, 0).astype(F32)
    work = logits
    vals, idxs, hots = [], [], []
    for _ in range(TOP_K):
        mx = jnp.max(work, axis=0, keepdims=True)
        idx = jnp.min(jnp.where(work == mx, eid, float(ne)), axis=0, keepdims=True)
        hot = eid == idx
        vals.append(mx)
        idxs.append(idx.astype(I32))
        hots.append(hot)
        work = jnp.where(hot, -jnp.inf, work)
    ex = [jnp.exp(vv - vals[0]) for vv in vals]
    den = ex[0] + ex[1] + ex[2] + ex[3]
    gates = [e_ / den for e_ in ex]

    multi = (hots[0] | hots[1] | hots[2] | hots[3])
    multi_f = jnp.where(multi, 1.0, 0.0).astype(F32)
    tri = (lax.broadcasted_iota(I32, (ts, ts), 0)
           < lax.broadcasted_iota(I32, (ts, ts), 1))
    before = _dot(multi_f.astype(BF16), jnp.where(tri, 1.0, 0.0).astype(BF16))

    @pl.when(first)
    def _():
        run_ref[...] = jnp.zeros(run_ref.shape, F32)

    run = run_ref[...]
    pos = before + run
    ranks = [jnp.sum(jnp.where(hot, pos, 0.0), axis=0, keepdims=True) for hot in hots]
    run_new = run + jnp.sum(multi_f, axis=1, keepdims=True)
    run_ref[...] = run_new
    cnt_ref[...] = jnp.broadcast_to(run_new, cnt_ref.shape).astype(I32)

    eidx_ref[...] = jnp.concatenate(idxs, axis=0)
    rank_ref[...] = jnp.concatenate(ranks, axis=0).astype(I32)
    g_rows = jnp.concatenate(gates + [jnp.zeros((LANES - TOP_K, ts), F32)], axis=0)
    gate_ref[...] = g_rows.T


def _post(x2d, a2d, g0, gyb, w_attn_up, w_out, norm_x_g, w_xq, k_mem, v_mem, w_xo,
          norm_ffn_g, w_router_t, b_router, batch, ts):
    n, d = x2d.shape
    nt = n // batch // ts
    ml = k_mem.shape[1]
    tok = pl.BlockSpec((ts, d), lambda bi, j: (bi * nt + j, 0))
    lanes = pl.BlockSpec((TOP_K, ts), lambda bi, j: (0, bi * nt + j))
    const = lambda *shape: pl.BlockSpec(shape, lambda bi, j: (0,) * len(shape),
                                        pipeline_mode=pl.Buffered(1))
    mem = pl.BlockSpec((1, ml, d), lambda bi, j: (bi, 0, 0))
    return pl.pallas_call(
        _post_kernel,
        grid=(batch, nt),
        in_specs=[tok, tok, tok, tok, const(d, d), const(d, d), const(1, d), const(d, d),
                  mem, mem, const(d, d), const(1, d), const(N_EXPERTS, d),
                  const(N_EXPERTS, 1)],
        out_specs=[tok, pl.BlockSpec((ts, d // 2), lambda bi, j: (bi * nt + j, 0)),
                   lanes, lanes,
                   pl.BlockSpec((ts, LANES), lambda bi, j: (bi * nt + j, 0)),
                   pl.BlockSpec((N_EXPERTS, LANES), lambda bi, j: (0, 0))],
        out_shape=[jax.ShapeDtypeStruct((n, d), F32),
                   jax.ShapeDtypeStruct((n, d // 2), jnp.uint32),
                   jax.ShapeDtypeStruct((TOP_K, n), I32), jax.ShapeDtypeStruct((TOP_K, n), I32),
                   jax.ShapeDtypeStruct((n, LANES), F32),
                   jax.ShapeDtypeStruct((N_EXPERTS, LANES), I32)],
        scratch_shapes=[pltpu.VMEM((N_EXPERTS, 1), F32)],
        compiler_params=_cparams(2),
        name="post",
    )(x2d, a2d, g0, gyb, w_attn_up, w_out, norm_x_g, w_xq, k_mem, v_mem, w_xo,
      norm_ffn_g, w_router_t, b_router)


def _route_dest_kernel(pstart_ref, eidx_ref, rank_ref, dest_ref):
    eidx = eidx_ref[...]
    start = jnp.zeros(eidx.shape, I32)
    for e in range(N_EXPERTS):
        start = jnp.where(eidx == e, pstart_ref[e], start)
    dest_ref[...] = start + rank_ref[...]


def _route_dest(pstart, eidx, rank, tl):
    k, n = eidx.shape
    lanes = pl.BlockSpec((k, tl), lambda i: (0, i))
    return pl.pallas_call(
        _route_dest_kernel,
        grid=(n // tl,),
        in_specs=[pl.BlockSpec(memory_space=pltpu.SMEM), lanes, lanes],
        out_specs=lanes,
        out_shape=jax.ShapeDtypeStruct((k, n), I32),
        compiler_params=_cparams(1),
        name="route_dest",
    )(pstart, eidx, rank)


SC_CORES = 2
SC_SUBCORES = 16
SC_WORKERS = SC_CORES * SC_SUBCORES
MOVE_CHUNK = 64
MOVE_BUFFERS = 3


def _worker_index_layout(dest, n):
    per_w = n // SC_WORKERS
    n_chunks = per_w // MOVE_CHUNK
    d4 = dest.reshape(TOP_K, SC_WORKERS, n_chunks, MOVE_CHUNK)
    return jnp.transpose(d4, (1, 2, 0, 3)).reshape(SC_WORKERS, n_chunks * TOP_K, MOVE_CHUNK)


def _sc_mesh():
    return plsc.VectorSubcoreMesh(core_axis_name="c", subcore_axis_name="s",
                                  num_cores=SC_CORES, num_subcores=SC_SUBCORES)


def _sc_dispatch(hf, dest_w, rows):
    n, d = hf.shape
    per_w = n // SC_WORKERS
    n_chunks = per_w // MOVE_CHUNK

    def body(hf_hbm, dest_hbm, xs_hbm, idx_v, rows_v, rsem, wsem):
        wid = lax.axis_index("s") * SC_CORES + lax.axis_index("c")
        base = wid * per_w
        pltpu.sync_copy(dest_hbm.at[wid], idx_v)

        def read(c):
            b = c % MOVE_BUFFERS
            return pltpu.async_copy(hf_hbm.at[pl.ds(base + c * MOVE_CHUNK, MOVE_CHUNK)],
                                    rows_v.at[b], rsem.at[b])

        scatters = [[] for _ in range(MOVE_BUFFERS)]
        pending = read(0)
        for c in range(n_chunks):
            b = c % MOVE_BUFFERS
            pending.wait()
            if c + 1 < n_chunks:
                for cp in scatters[(c + 1) % MOVE_BUFFERS]:
                    cp.wait()
                pending = read(c + 1)
            scatters[b] = [
                pltpu.async_copy(rows_v.at[b], xs_hbm.at[idx_v.at[c * TOP_K + kk]], wsem.at[b])
                for kk in range(TOP_K)]
        for group in scatters:
            for cp in group:
                cp.wait()

    return pl.kernel(
        body,
        out_type=jax.ShapeDtypeStruct((rows, d), hf.dtype),
        mesh=_sc_mesh(),
        scratch_types=[pltpu.VMEM((n_chunks * TOP_K, MOVE_CHUNK), I32),
                       pltpu.VMEM((MOVE_BUFFERS, MOVE_CHUNK, d), hf.dtype),
                       pltpu.SemaphoreType.DMA((MOVE_BUFFERS,)),
                       pltpu.SemaphoreType.DMA((MOVE_BUFFERS,))],
        name="sc_dispatch",
    )(hf, dest_w)


def _sc_gather(ys, dest_w, n):
    _, d = ys.shape
    per_w = n // SC_WORKERS
    n_chunks = per_w // MOVE_CHUNK

    def body(ys_hbm, dest_hbm, yg_hbm, idx_v, rows_v, rsem, wsem):
        wid = lax.axis_index("s") * SC_CORES + lax.axis_index("c")
        base = wid * per_w
        pltpu.sync_copy(dest_hbm.at[wid], idx_v)
        n_moves = n_chunks * TOP_K

        def gather(m):
            b = m % MOVE_BUFFERS
            return pltpu.async_copy(ys_hbm.at[idx_v.at[m]], rows_v.at[b], rsem.at[b])

        def write(m):
            b = m % MOVE_BUFFERS
            c, kk = divmod(m, TOP_K)
            return pltpu.async_copy(
                rows_v.at[b], yg_hbm.at[pl.ds(kk * n + base + c * MOVE_CHUNK, MOVE_CHUNK)],
                wsem.at[b])

        writes = [None] * MOVE_BUFFERS
        pending = gather(0)
        for m in range(n_moves):
            pending.wait()
            if m + 1 < n_moves:
                nb = (m + 1) % MOVE_BUFFERS
                if writes[nb] is not None:
                    writes[nb].wait()
                pending = gather(m + 1)
            writes[m % MOVE_BUFFERS] = write(m)
        for wr in writes:
            if wr is not None:
                wr.wait()

    return pl.kernel(
        body,
        out_type=jax.ShapeDtypeStruct((TOP_K * n, d), ys.dtype),
        mesh=_sc_mesh(),
        scratch_types=[pltpu.VMEM((n_chunks * TOP_K, MOVE_CHUNK), I32),
                       pltpu.VMEM((MOVE_BUFFERS, MOVE_CHUNK, d), ys.dtype),
                       pltpu.SemaphoreType.DMA((MOVE_BUFFERS,)),
                       pltpu.SemaphoreType.DMA((MOVE_BUFFERS,))],
        name="sc_gather",
    )(ys, dest_w)


def _combine_kernel(x2_ref, gate_ref, fg_ref, yg_ref, o_ref):
    g = gate_ref[...]
    x2 = x2_ref[...]
    c = x2.shape[1] // 2
    acc_hi, acc_lo = x2[:, :c], x2[:, c:]
    for kk in range(TOP_K):
        hi, lo = _unpack_halves(yg_ref[kk])
        acc_hi = acc_hi + g[:, kk:kk + 1] * hi
        acc_lo = acc_lo + g[:, kk:kk + 1] * lo
    o_ref[...] = _rms(jnp.concatenate([acc_hi, acc_lo], axis=1), fg_ref[...])


def _combine(x2, gate_tm, final_g, yg, ts):
    n, d = x2.shape
    dw = yg.shape[1]
    return pl.pallas_call(
        _combine_kernel,
        grid=(n // ts,),
        in_specs=[pl.BlockSpec((ts, d), lambda i: (i, 0)),
                  pl.BlockSpec((ts, LANES), lambda i: (i, 0)),
                  pl.BlockSpec((1, d), lambda i: (0, 0)),
                  pl.BlockSpec((TOP_K, ts, dw), lambda i: (0, i, 0))],
        out_specs=pl.BlockSpec((ts, d), lambda i: (i, 0)),
        out_shape=jax.ShapeDtypeStruct((n, d), F32),
        compiler_params=_cparams(1),
        name="combine",
    )(x2, gate_tm, final_g, yg.reshape(TOP_K, n, dw))


def _experts_kernel(be_ref, first_ref, nxt_ref, slot_ref, nb_ref,
                    xs_ref, bgu_ref, bd_ref, wgu_hbm, wd_hbm, ys_ref,
                    wgu_st, wd_st, wgu_bf, wd_bf, sem_w):
    dff = wd_bf.shape[0]
    bm = xs_ref.shape[0] // EXPERT_BLOCKS_PER_STEP

    def fetch(ex, sl):
        return (pltpu.make_async_copy(wgu_hbm.at[ex], wgu_st.at[sl], sem_w.at[0, sl]),
                pltpu.make_async_copy(wd_hbm.at[ex], wd_st.at[sl], sem_w.at[1, sl]))

    @pl.when(pl.program_id(0) == 0)
    def _():
        e0 = be_ref[0]
        for cp in fetch(e0, slot_ref[e0]):
            cp.start()

    for sub in range(EXPERT_BLOCKS_PER_STEP):
        blk = pl.program_id(0) * EXPERT_BLOCKS_PER_STEP + sub
        e = be_ref[blk]
        rows = slice(sub * bm, (sub + 1) * bm)

        @pl.when(first_ref[blk] == 1)
        def _(e=e):
            sl = slot_ref[e]
            for cp in fetch(e, sl):
                cp.wait()
            nx = nxt_ref[e]

            @pl.when(nx >= 0)
            def _():
                for cp in fetch(nx, 1 - sl):
                    cp.start(priority=1)

            wgu_bf[...] = wgu_st[sl].astype(BF16)
            wd_bf[...] = wd_st[sl].astype(BF16)

        @pl.when(blk < nb_ref[0])
        def _(e=e, rows=rows):
            xb = jnp.concatenate(_unpack_halves(xs_ref[rows, :]), axis=1).astype(BF16)
            gu = _dot(xb, wgu_bf[...]) + bgu_ref[e]
            x_glu = jnp.minimum(gu[:, :dff], SWIGLU_LIMIT)
            x_lin = jnp.clip(gu[:, dff:], -SWIGLU_LIMIT, SWIGLU_LIMIT)
            act = x_glu * jax.nn.sigmoid(SWIGLU_ALPHA * x_glu) * (x_lin + 1.0)
            ys_ref[rows, :] = _pack_halves(_dot(act.astype(BF16), wd_bf[...]) + bd_ref[e])

        @pl.when(blk >= nb_ref[0])
        def _(rows=rows):
            ys_ref[rows, :] = jnp.zeros((bm, ys_ref.shape[1]), jnp.uint32)


def _expert_schedule(counts, bm, n_blocks):
    ne = counts.shape[0]
    ids = jnp.arange(ne, dtype=I32)
    upto = ids[None, :] <= ids[:, None]
    padded = (counts + bm - 1) // bm * bm
    pend = jnp.sum(jnp.where(upto, padded[None, :], 0), axis=1)
    pstart = (pend - padded).astype(I32)
    nb_used = (jnp.sum(padded) // bm).astype(I32)
    blk = jnp.arange(n_blocks, dtype=I32)
    blk_c = jnp.minimum(blk, nb_used - 1)
    be = jnp.minimum(jnp.sum((pend[None, :] <= (blk_c * bm)[:, None]).astype(I32), axis=1),
                     ne - 1)
    started = jnp.sum(((pstart[None, :] == (blk * bm)[:, None]) & (padded[None, :] > 0))
                      .astype(I32), axis=1)
    first = ((blk < nb_used) & (started > 0)).astype(I32)
    used = counts > 0
    seq = jnp.sum(jnp.where(upto & used[None, :], 1, 0), axis=1) - 1
    later = used[None, :] & (ids[None, :] > ids[:, None])
    nxt = jnp.where(jnp.any(later, axis=1), jnp.argmax(later, axis=1), -1).astype(I32)
    slot = (seq & 1).astype(I32)
    return pstart, nb_used.reshape(1), be, first, nxt, slot


def _experts(be, first, nxt, slot, nb_used, xs, w_gu, b_gu, w_down, b_down, bm):
    rows, dw = xs.shape
    ne, d, dff2 = w_gu.shape
    dff = dff2 // 2
    sb = EXPERT_BLOCKS_PER_STEP * bm
    assert rows % sb == 0
    last = lambda j, be, fi, nx, sl, nb: (
        jnp.maximum(jnp.minimum(j, (nb[0] - 1) // EXPERT_BLOCKS_PER_STEP), 0), 0)
    const = lambda *shape: pl.BlockSpec(shape, lambda j, be, fi, nx, sl, nb: (0,) * len(shape))
    grid_spec = pltpu.PrefetchScalarGridSpec(
        num_scalar_prefetch=5,
        grid=(rows // sb,),
        in_specs=[
            pl.BlockSpec((sb, dw), last),
            const(ne, 1, dff2),
            const(ne, 1, d),
            pl.BlockSpec(memory_space=pl.ANY),
            pl.BlockSpec(memory_space=pl.ANY),
        ],
        out_specs=pl.BlockSpec((sb, dw), lambda j, be, fi, nx, sl, nb: (j, 0)),
        scratch_shapes=[pltpu.VMEM((2, d, dff2), F32), pltpu.VMEM((2, dff, d), F32),
                        pltpu.VMEM((d, dff2), BF16), pltpu.VMEM((dff, d), BF16),
                        pltpu.SemaphoreType.DMA((2, 2))],
    )
    return pl.pallas_call(
        _experts_kernel,
        grid_spec=grid_spec,
        out_shape=jax.ShapeDtypeStruct((rows, dw), jnp.uint32),
        compiler_params=_cparams(1),
        name="experts",
    )(be, first, nxt, slot, nb_used, xs, b_gu.reshape(ne, 1, dff2),
      b_down.reshape(ne, 1, d), w_gu, w_down)


MIXER_TILE = 512
ATTN_HEADS_PER_STEP = 2
POST_TILE = 512
DEST_TILE = 2048
COMBINE_TILE = 1024
EXPERT_BLOCK = 256
EXPERT_BLOCKS_PER_STEP = 4


def kernel(x, mem, norm_mix_g, w_in, lambda_q1, lambda_k1, lambda_q2, lambda_k2, rel_bias,
           subln_g, w_attn_up, pool_mix, pool_scale, w_pool_up, w_gate, b_gate, w_out,
           norm_x_g, norm_mem_g, w_xq, w_xkv, w_xo, norm_ffn_g, w_router, b_router,
           w_gu, b_gu, w_down, b_down, final_norm_g):
    b, s, d = x.shape
    n = b * s
    assert w_in.shape[0] == 1, "single-layer block"
    row = lambda a: a.reshape(1, -1)
    bf = lambda a: a[0].astype(BF16)

    q, k, v, g0, gyb = _mixer_in(
        x, row(norm_mix_g[0]), bf(w_in), bf(w_gate), row(b_gate[0]), bf(pool_mix),
        row(pool_scale[0]), bf(w_pool_up), MIXER_TILE)

    bias, lam = _rel_bias(rel_bias, row(lambda_q1[0]), row(lambda_k1[0]),
                          row(lambda_q2[0]), row(lambda_k2[0]), s)
    a = _diff_attn(q, k, v, bias, lam, row(subln_g[0]), ATTN_HEADS_PER_STEP)

    k_mem, v_mem = _mem_kv(mem, row(norm_mem_g[0]), bf(w_xkv))

    x2, hf, eidx, rank, gate_tm, counts = _post(
        x.reshape(n, d), a.reshape(n, d), g0, gyb, bf(w_attn_up), bf(w_out),
        row(norm_x_g[0]), bf(w_xq), k_mem, v_mem, bf(w_xo), row(norm_ffn_g[0]),
        w_router[0].T, b_router[0].reshape(-1, 1), b, POST_TILE)

    bm = EXPERT_BLOCK
    rows = n * TOP_K + N_EXPERTS * bm
    pstart, nb_used, be, first, nxt, slot = _expert_schedule(counts[:, 0], bm, rows // bm)

    dest_w = _worker_index_layout(_route_dest(pstart, eidx, rank, DEST_TILE), n)
    xs = _sc_dispatch(hf, dest_w, rows)
    ys = _experts(be, first, nxt, slot, nb_used, xs, w_gu[0], b_gu[0], w_down[0], b_down[0],
                  bm)
    yg = _sc_gather(ys, dest_w, n)
    out = _combine(x2, gate_tm, row(final_norm_g), yg, COMBINE_TILE)
    return out.reshape(b, s, d)
```

```python
import functools
import math

import numpy as np
import jax
import jax.numpy as jnp
from jax import lax
from jax.experimental import pallas as pl
from jax.experimental.pallas import tpu as pltpu
from jax.experimental.pallas import tpu_sc as plsc

F32 = jnp.float32
BF16 = jnp.bfloat16
I32 = jnp.int32

EPS = 1e-6
CHUNK = 64
N_HEADS = 8
HEAD_DIM = 64
HEAD_W = 2 * HEAD_DIM
POOL_WINDOWS = (2, 4, 8, 16)
POOL_GROUP = 128
POOL_PAD = 16
N_BUCKETS = 32
MAX_DISTANCE = 128
X_HEADS = 4
N_EXPERTS = 32
TOP_K = 4
SWIGLU_ALPHA = 1.702
SWIGLU_LIMIT = 7.0
LAMBDA_INIT = 0.8 - 0.6 * math.exp(-0.3 * 0)

LANES = 128
SUBLANES = 8
ATTN_BLOCK = 256
VMEM_LIMIT = 56 * 1024 * 1024


def _cparams(n_axes, vmem=VMEM_LIMIT):
    return pltpu.CompilerParams(
        dimension_semantics=("arbitrary",) * n_axes, vmem_limit_bytes=vmem)


def _rms(xf, g):
    ms = jnp.mean(xf * xf, axis=-1, keepdims=True)
    return xf * lax.rsqrt(ms + EPS) * g


def _dot(a, b):
    return jnp.dot(a, b, preferred_element_type=F32)


def _pack_halves(x):
    c = x.shape[1] // 2
    as_bits = lambda v: lax.bitcast_convert_type(v.astype(BF16).astype(F32), jnp.uint32)
    return as_bits(x[:, :c]) | (as_bits(x[:, c:]) >> 16)


def _unpack_halves(w):
    hi = lax.bitcast_convert_type(w & jnp.uint32(0xFFFF0000), F32)
    lo = lax.bitcast_convert_type(w << 16, F32)
    return hi, lo


def _dot_nt(a, b, precision=None):
    return lax.dot_general(a, b, (((1,), (1,)), ((), ())),
                           preferred_element_type=F32, precision=precision)


def _mixer_in_kernel(x_ref, g_ref, wq_ref, wk_ref, wv_ref, wu_ref, wg_ref, bg_ref,
                     pmix_ref, pscale_ref, wpu_ref,
                     q_ref, k_ref, v_ref, g0_ref, gyb_ref, ext_ref):
    ts = x_ref.shape[1]
    d = x_ref.shape[2]
    j = pl.program_id(1)
    h = _rms(x_ref[0], g_ref[...]).astype(BF16)
    q_ref[0] = (_dot(h, wq_ref[...]) * (HEAD_DIM ** -0.5)).astype(BF16)
    k_ref[0] = _dot(h, wk_ref[...]).astype(BF16)
    v_ref[0] = _dot(h, wv_ref[...]).astype(BF16)
    u = _dot(h, wu_ref[...])

    @pl.when(j == 0)
    def _():
        ext_ref[0:POOL_PAD, :] = jnp.zeros((POOL_PAD, u.shape[1]), F32)

    ext_ref[POOL_PAD:POOL_PAD + ts, :] = u
    e = ext_ref[...]
    sums = {}
    s = e
    w = 1
    while w < max(POOL_WINDOWS):
        s = s + pltpu.roll(s, w, 0)
        w *= 2
        sums[w] = s
    ext_ref[0:POOL_PAD, :] = ext_ref[ts:ts + POOL_PAD, :]

    pos = (j * ts + lax.broadcasted_iota(I32, (ts, 1), 0) + 1).astype(F32)
    mixed = []
    for gi, w in enumerate(POOL_WINDOWS):
        sl = slice(gi * POOL_GROUP, (gi + 1) * POOL_GROUP)
        win = sums[w][POOL_PAD:, sl]
        pooled = win / jnp.minimum(pos, float(w)) - u[:, sl]
        mixed.append(_dot(pooled.astype(BF16), pmix_ref[gi]) * pscale_ref[:, sl])
    mixed = jnp.concatenate(mixed, axis=1).astype(BF16)
    y_b = _dot(mixed, wpu_ref[...])

    gate = jax.nn.sigmoid(_dot(h, wg_ref[...]) + bg_ref[...])
    g0_ref[...] = gate[:, :d].astype(BF16)
    gyb_ref[...] = (gate[:, d:] * y_b).astype(BF16)


def _mixer_in(x, norm_g, w_in, w_gate, b_gate, pool_mix, pool_scale, w_pool_up, ts):
    b, s, d = x.shape
    aw = N_HEADS * HEAD_W
    pw = len(POOL_WINDOWS) * POOL_GROUP
    n = b * s
    nt = s // ts
    const = lambda *shape: pl.BlockSpec(shape, lambda bi, j: (0,) * len(shape))
    tok3 = pl.BlockSpec((1, ts, aw), lambda bi, j: (bi, j, 0))
    tok2 = pl.BlockSpec((ts, d), lambda bi, j: (bi * nt + j, 0))
    return pl.pallas_call(
        _mixer_in_kernel,
        grid=(b, nt),
        in_specs=[
            pl.BlockSpec((1, ts, d), lambda bi, j: (bi, j, 0)),
            const(1, d),
            pl.BlockSpec((d, aw), lambda bi, j: (0, 0)),
            pl.BlockSpec((d, aw), lambda bi, j: (0, 1)),
            pl.BlockSpec((d, aw), lambda bi, j: (0, 2)),
            pl.BlockSpec((d, pw), lambda bi, j: (0, 3 * aw // pw)),
            const(d, 2 * d),
            const(1, 2 * d),
            const(len(POOL_WINDOWS), POOL_GROUP, POOL_GROUP),
            const(1, pw),
            const(pw, d),
        ],
        out_specs=[tok3, tok3, tok3, tok2, tok2],
        out_shape=[jax.ShapeDtypeStruct((b, s, aw), BF16)] * 3
        + [jax.ShapeDtypeStruct((n, d), BF16)] * 2,
        scratch_shapes=[pltpu.VMEM((ts + POOL_PAD, pw), F32)],
        compiler_params=_cparams(2),
        name="mixer_in",
    )(x, norm_g, w_in, w_in, w_in, w_in, w_gate, b_gate, pool_mix, pool_scale, w_pool_up)


def _rel_bucket(rel, log=jnp.log, f32=lambda a: a.astype(jnp.float32),
                i32=lambda a: a.astype(jnp.int32), xp=jnp):
    nb = N_BUCKETS // 2
    ret = i32(rel > 0) * nb
    n = xp.abs(rel)
    max_exact = nb // 2
    nf = f32(xp.maximum(n, 1))
    large = max_exact + i32(log(nf / max_exact) / math.log(MAX_DISTANCE / max_exact)
                            * (nb - max_exact))
    large = xp.minimum(large, nb - 1)
    return ret + xp.where(n < max_exact, n, large)


def _far_bucket(block, seq):
    rel = -np.arange(block + 1, max(seq, block + 2), dtype=np.int32)
    bk = _rel_bucket(rel, log=np.log, f32=lambda a: a.astype(np.float32),
                     i32=lambda a: a.astype(np.int32), xp=np)
    assert (bk == bk[0]).all(), "far keys must share one relative-position bucket"
    return int(bk[0])


def _rel_bias_kernel(far_bucket, tab_ref, bidx_ref, lq1_ref, lk1_ref, lq2_ref, lk2_ref,
                     bias_ref, lam_ref):
    h = pl.program_id(0)
    bidx = bidx_ref[...]
    acc = jnp.zeros(bidx.shape, F32)
    for bkt in range(N_BUCKETS):
        acc = jnp.where(bidx == bkt, tab_ref[bkt, h], acc)
    acc = acc - tab_ref[far_bucket, h]
    bias_ref[0] = jnp.where(bidx < 0, -jnp.inf, acc)
    lam = (jnp.exp(jnp.sum(lq1_ref[...] * lk1_ref[...], keepdims=True))
           - jnp.exp(jnp.sum(lq2_ref[...] * lk2_ref[...], keepdims=True)) + LAMBDA_INIT)
    lam_ref[...] = jnp.broadcast_to(lam, lam_ref.shape)


def _rel_bias(rel_bias, lq1, lk1, lq2, lk2, seq):
    blk = ATTN_BLOCK
    qpos = jnp.arange(blk, dtype=I32)[:, None]
    kpos = jnp.arange(blk, dtype=I32)[None, :]
    diag = jnp.where(kpos // CHUNK <= qpos // CHUNK, _rel_bucket(kpos - qpos), -1)
    prev = _rel_bucket(kpos - (qpos + blk))
    bidx = jnp.stack([diag, prev]).astype(I32)
    vec = pl.BlockSpec((1, HEAD_DIM), lambda h: (0, 0))
    return pl.pallas_call(
        functools.partial(_rel_bias_kernel, _far_bucket(blk, seq)),
        grid=(N_HEADS,),
        in_specs=[
            pl.BlockSpec(memory_space=pltpu.SMEM),
            pl.BlockSpec((2, blk, blk), lambda h: (0, 0, 0)),
            vec, vec, vec, vec,
        ],
        out_specs=[
            pl.BlockSpec((1, 2, blk, blk), lambda h: (h, 0, 0, 0)),
            pl.BlockSpec((SUBLANES, LANES), lambda h: (0, 0)),
        ],
        out_shape=[jax.ShapeDtypeStruct((N_HEADS, 2, blk, blk), F32),
                   jax.ShapeDtypeStruct((SUBLANES, LANES), F32)],
        compiler_params=_cparams(1),
        name="rel_bias",
    )(rel_bias, bidx, lq1, lk1, lq2, lk2)


def _diff_attn_kernel(q_ref, k_ref, v_ref, bias_ref, lam_ref, sg_ref, o_ref, vext_ref):
    s_len = q_ref.shape[1]
    tq = ATTN_BLOCK
    n_heads = q_ref.shape[2] // HEAD_W
    lam = lam_ref[0:1, 0:1]
    lane = lax.broadcasted_iota(I32, (tq, HEAD_W), 1)
    cols = [slice(hh * HEAD_W, (hh + 1) * HEAD_W) for hh in range(n_heads)]
    b_diag, b_prev = [], []
    for hh in range(n_heads):
        vext_ref[hh, :, :HEAD_W] = v_ref[0, :, cols[hh]]
        vext_ref[hh, :, HEAD_W:] = jnp.ones((s_len, HEAD_W), BF16)
        b_diag.append(jnp.concatenate([bias_ref[hh, 0]] * 2, axis=0))
        b_prev.append(jnp.concatenate([bias_ref[hh, 1]] * 2, axis=0))

    for qi in range(s_len // tq):
        n_keys = (qi + 1) * tq
        rows = slice(qi * tq, (qi + 1) * tq)
        for hh in range(n_heads):
            q = q_ref[0, rows, cols[hh]]
            zero = jnp.zeros_like(q)
            qs = jnp.concatenate([jnp.where(lane < HEAD_DIM, q, zero),
                                  jnp.where(lane >= HEAD_DIM, q, zero)], axis=0)
            s = _dot_nt(qs, k_ref[0, :n_keys, cols[hh]])
            pieces = []
            if qi >= 2:
                pieces.append(s[:, :n_keys - 2 * tq])
            if qi >= 1:
                pieces.append(s[:, n_keys - 2 * tq:n_keys - tq] + b_prev[hh])
            pieces.append(s[:, n_keys - tq:] + b_diag[hh])
            s = jnp.concatenate(pieces, axis=1) if len(pieces) > 1 else pieces[0]
            m = jnp.max(s, axis=-1, keepdims=True)
            p = jnp.exp(s - m).astype(BF16)
            acc = _dot(p, vext_ref[hh, :n_keys, :])
            o = acc[:, :HEAD_W] / acc[:, HEAD_W:HEAD_W + 1]
            a = o[:tq] - lam * o[tq:]
            o_ref[0, rows, cols[hh]] = (
                _rms(a, sg_ref[...]) * (1.0 - LAMBDA_INIT)).astype(BF16)


def _diff_attn(q, k, v, bias, lam, subln_g, heads_per_step):
    b, s, aw = q.shape
    tq = ATTN_BLOCK
    hp = heads_per_step
    spec = pl.BlockSpec((1, s, hp * HEAD_W), lambda bi, h: (bi, 0, h))
    return pl.pallas_call(
        _diff_attn_kernel,
        grid=(b, N_HEADS // hp),
        in_specs=[
            spec, spec, spec,
            pl.BlockSpec((hp, 2, tq, tq), lambda bi, h: (h, 0, 0, 0)),
            pl.BlockSpec((SUBLANES, LANES), lambda bi, h: (0, 0)),
            pl.BlockSpec((1, HEAD_W), lambda bi, h: (0, 0)),
        ],
        out_specs=spec,
        out_shape=jax.ShapeDtypeStruct((b, s, aw), BF16),
        scratch_shapes=[pltpu.VMEM((hp, s, 2 * HEAD_W), BF16)],
        compiler_params=_cparams(2),
        name="diff_attn",
    )(q, k, v, bias, lam, subln_g)


def _mem_kv_kernel(m_ref, g_ref, w_ref, k_ref, v_ref):
    d = m_ref.shape[2]
    m = _rms(m_ref[0], g_ref[...]).astype(BF16)
    kv = _dot(m, w_ref[...])
    k_ref[0] = kv[:, :d].astype(BF16)
    v_ref[0] = kv[:, d:].astype(BF16)


def _mem_kv(mem, norm_g, w_xkv):
    b, ml, d = mem.shape
    blk = pl.BlockSpec((1, ml, d), lambda bi: (bi, 0, 0))
    return pl.pallas_call(
        _mem_kv_kernel,
        grid=(b,),
        in_specs=[blk, pl.BlockSpec((1, d), lambda bi: (0, 0)),
                  pl.BlockSpec((d, 2 * d), lambda bi: (0, 0))],
        out_specs=[blk, blk],
        out_shape=[jax.ShapeDtypeStruct((b, ml, d), BF16)] * 2,
        compiler_params=_cparams(1),
        name="mem_kv",
    )(mem, norm_g, w_xkv)


def _post_kernel(x_ref, a_ref, g0_ref, gyb_ref, wau_ref, wo_ref, nxg_ref, wxq_ref,
                 km_ref, vm_ref, wxo_ref, nfg_ref, wrt_ref, br_ref,
                 x2_ref, hf_ref, eidx_ref, rank_ref, gate_ref, cnt_ref, run_ref):
    ts, d = x_ref.shape
    first = (pl.program_id(0) == 0) & (pl.program_id(1) == 0)

    y_a = _dot(a_ref[...], wau_ref[...])
    merged = g0_ref[...].astype(F32) * y_a + gyb_ref[...].astype(F32)
    x1 = x_ref[...] + _dot(merged.astype(BF16), wo_ref[...])

    hd = d // X_HEADS
    hx = _rms(x1, nxg_ref[...]).astype(BF16)
    qx = (_dot(hx, wxq_ref[...]) * (hd ** -0.5)).astype(BF16)
    heads = []
    for hh in range(X_HEADS):
        sl = slice(hh * hd, (hh + 1) * hd)
        s = _dot_nt(qx[:, sl], km_ref[0, :, sl])
        p = jnp.exp(s - jnp.max(s, axis=-1, keepdims=True))
        p = p / jnp.sum(p, axis=-1, keepdims=True)
        heads.append(_dot(p.astype(BF16), vm_ref[0, :, sl]))
    o = jnp.concatenate(heads, axis=1).astype(BF16)
    x2 = x1 + _dot(o, wxo_ref[...])
    x2_ref[...] = x2
    hf = _rms(x2, nfg_ref[...])
    hf_ref[...] = _pack_halves(hf)

    logits = _dot_nt(wrt_ref[...], hf, precision=lax.Precision.HIGHEST) + br_ref[...]
    ne = logits.shape[0]
    eid = lax.broadcasted_iota(I32, logits.shape, 0).astype(F32)
    work = logits
    vals, idxs, hots = [], [], []
    for _ in range(TOP_K):
        mx = jnp.max(work, axis=0, keepdims=True)
        idx = jnp.min(jnp.where(work == mx, eid, float(ne)), axis=0, keepdims=True)
        hot = eid == idx
        vals.append(mx)
        idxs.append(idx.astype(I32))
        hots.append(hot)
        work = jnp.where(hot, -jnp.inf, work)
    ex = [jnp.exp(vv - vals[0]) for vv in vals]
    den = ex[0] + ex[1] + ex[2] + ex[3]
    gates = [e_ / den for e_ in ex]

    multi = (hots[0] | hots[1] | hots[2] | hots[3])
    multi_f = jnp.where(multi, 1.0, 0.0).astype(F32)
    tri = (lax.broadcasted_iota(I32, (ts, ts), 0)
           < lax.broadcasted_iota(I32, (ts, ts), 1))
    before = _dot(multi_f.astype(BF16), jnp.where(tri, 1.0, 0.0).astype(BF16))

    @pl.when(first)
    def _():
        run_ref[...] = jnp.zeros(run_ref.shape, F32)

    run = run_ref[...]
    pos = before + run
    ranks = [jnp.sum(jnp.where(hot, pos, 0.0), axis=0, keepdims=True) for hot in hots]
    run_new = run + jnp.sum(multi_f, axis=1, keepdims=True)
    run_ref[...] = run_new
    cnt_ref[...] = jnp.broadcast_to(run_new, cnt_ref.shape).astype(I32)

    eidx_ref[...] = jnp.concatenate(idxs, axis=0)
    rank_ref[...] = jnp.concatenate(ranks, axis=0).astype(I32)
    g_rows = jnp.concatenate(gates + [jnp.zeros((LANES - TOP_K, ts), F32)], axis=0)
    gate_ref[...] = g_rows.T


def _post(x2d, a2d, g0, gyb, w_attn_up, w_out, norm_x_g, w_xq, k_mem, v_mem, w_xo,
          norm_ffn_g, w_router_t, b_router, batch, ts):
    n, d = x2d.shape
    nt = n // batch // ts
    ml = k_mem.shape[1]
    tok = pl.BlockSpec((ts, d), lambda bi, j: (bi * nt + j, 0))
    lanes = pl.BlockSpec((TOP_K, ts), lambda bi, j: (0, bi * nt + j))
    const = lambda *shape: pl.BlockSpec(shape, lambda bi, j: (0,) * len(shape),
                                        pipeline_mode=pl.Buffered(1))
    mem = pl.BlockSpec((1, ml, d), lambda bi, j: (bi, 0, 0))
    return pl.pallas_call(
        _post_kernel,
        grid=(batch, nt),
        in_specs=[tok, tok, tok, tok, const(d, d), const(d, d), const(1, d), const(d, d),
                  mem, mem, const(d, d), const(1, d), const(N_EXPERTS, d),
                  const(N_EXPERTS, 1)],
        out_specs=[tok, pl.BlockSpec((ts, d // 2), lambda bi, j: (bi * nt + j, 0)),
                   lanes, lanes,
                   pl.BlockSpec((ts, LANES), lambda bi, j: (bi * nt + j, 0)),
                   pl.BlockSpec((N_EXPERTS, LANES), lambda bi, j: (0, 0))],
        out_shape=[jax.ShapeDtypeStruct((n, d), F32),
                   jax.ShapeDtypeStruct((n, d // 2), jnp.uint32),
                   jax.ShapeDtypeStruct((TOP_K, n), I32), jax.ShapeDtypeStruct((TOP_K, n), I32),
                   jax.ShapeDtypeStruct((n, LANES), F32),
                   jax.ShapeDtypeStruct((N_EXPERTS, LANES), I32)],
        scratch_shapes=[pltpu.VMEM((N_EXPERTS, 1), F32)],
        compiler_params=_cparams(2),
        name="post",
    )(x2d, a2d, g0, gyb, w_attn_up, w_out, norm_x_g, w_xq, k_mem, v_mem, w_xo,
      norm_ffn_g, w_router_t, b_router)


def _route_dest_kernel(pstart_ref, eidx_ref, rank_ref, dest_ref):
    eidx = eidx_ref[...]
    start = jnp.zeros(eidx.shape, I32)
    for e in range(N_EXPERTS):
        start = jnp.where(eidx == e, pstart_ref[e], start)
    dest_ref[...] = start + rank_ref[...]


def _route_dest(pstart, eidx, rank, tl):
    k, n = eidx.shape
    lanes = pl.BlockSpec((k, tl), lambda i: (0, i))
    return pl.pallas_call(
        _route_dest_kernel,
        grid=(n // tl,),
        in_specs=[pl.BlockSpec(memory_space=pltpu.SMEM), lanes, lanes],
        out_specs=lanes,
        out_shape=jax.ShapeDtypeStruct((k, n), I32),
        compiler_params=_cparams(1),
        name="route_dest",
    )(pstart, eidx, rank)


SC_CORES = 2
SC_SUBCORES = 16
SC_WORKERS = SC_CORES * SC_SUBCORES
MOVE_CHUNK = 64
MOVE_BUFFERS = 3


def _worker_index_layout(dest, n):
    per_w = n // SC_WORKERS
    n_chunks = per_w // MOVE_CHUNK
    d4 = dest.reshape(TOP_K, SC_WORKERS, n_chunks, MOVE_CHUNK)
    return jnp.transpose(d4, (1, 2, 0, 3)).reshape(SC_WORKERS, n_chunks * TOP_K, MOVE_CHUNK)


def _sc_mesh():
    return plsc.VectorSubcoreMesh(core_axis_name="c", subcore_axis_name="s",
                                  num_cores=SC_CORES, num_subcores=SC_SUBCORES)


def _sc_dispatch(hf, dest_w, rows):
    n, d = hf.shape
    per_w = n // SC_WORKERS
    n_chunks = per_w // MOVE_CHUNK

    def body(hf_hbm, dest_hbm, xs_hbm, idx_v, rows_v, rsem, wsem):
        wid = lax.axis_index("s") * SC_CORES + lax.axis_index("c")
        base = wid * per_w
        pltpu.sync_copy(dest_hbm.at[wid], idx_v)

        def read(c):
            b = c % MOVE_BUFFERS
            return pltpu.async_copy(hf_hbm.at[pl.ds(base + c * MOVE_CHUNK, MOVE_CHUNK)],
                                    rows_v.at[b], rsem.at[b])

        scatters = [[] for _ in range(MOVE_BUFFERS)]
        pending = read(0)
        for c in range(n_chunks):
            b = c % MOVE_BUFFERS
            pending.wait()
            if c + 1 < n_chunks:
                for cp in scatters[(c + 1) % MOVE_BUFFERS]:
                    cp.wait()
                pending = read(c + 1)
            scatters[b] = [
                pltpu.async_copy(rows_v.at[b], xs_hbm.at[idx_v.at[c * TOP_K + kk]], wsem.at[b])
                for kk in range(TOP_K)]
        for group in scatters:
            for cp in group:
                cp.wait()

    return pl.kernel(
        body,
        out_type=jax.ShapeDtypeStruct((rows, d), hf.dtype),
        mesh=_sc_mesh(),
        scratch_types=[pltpu.VMEM((n_chunks * TOP_K, MOVE_CHUNK), I32),
                       pltpu.VMEM((MOVE_BUFFERS, MOVE_CHUNK, d), hf.dtype),
                       pltpu.SemaphoreType.DMA((MOVE_BUFFERS,)),
                       pltpu.SemaphoreType.DMA((MOVE_BUFFERS,))],
        name="sc_dispatch",
    )(hf, dest_w)


def _sc_gather(ys, dest_w, n):
    _, d = ys.shape
    per_w = n // SC_WORKERS
    n_chunks = per_w // MOVE_CHUNK

    def body(ys_hbm, dest_hbm, yg_hbm, idx_v, rows_v, rsem, wsem):
        wid = lax.axis_index("s") * SC_CORES + lax.axis_index("c")
        base = wid * per_w
        pltpu.sync_copy(dest_hbm.at[wid], idx_v)
        n_moves = n_chunks * TOP_K

        def gather(m):
            b = m % MOVE_BUFFERS
            return pltpu.async_copy(ys_hbm.at[idx_v.at[m]], rows_v.at[b], rsem.at[b])

        def write(m):
            b = m % MOVE_BUFFERS
            c, kk = divmod(m, TOP_K)
            return pltpu.async_copy(
                rows_v.at[b], yg_hbm.at[pl.ds(kk * n + base + c * MOVE_CHUNK, MOVE_CHUNK)],
                wsem.at[b])

        writes = [None] * MOVE_BUFFERS
        pending = gather(0)
        for m in range(n_moves):
            pending.wait()
            if m + 1 < n_moves:
                nb = (m + 1) % MOVE_BUFFERS
                if writes[nb] is not None:
                    writes[nb].wait()
                pending = gather(m + 1)
            writes[m % MOVE_BUFFERS] = write(m)
        for wr in writes:
            if wr is not None:
                wr.wait()

    return pl.kernel(
        body,
        out_type=jax.ShapeDtypeStruct((TOP_K * n, d), ys.dtype),
        mesh=_sc_mesh(),
        scratch_types=[pltpu.VMEM((n_chunks * TOP_K, MOVE_CHUNK), I32),
                       pltpu.VMEM((MOVE_BUFFERS, MOVE_CHUNK, d), ys.dtype),
                       pltpu.SemaphoreType.DMA((MOVE_BUFFERS,)),
                       pltpu.SemaphoreType.DMA((MOVE_BUFFERS,))],
        name="sc_gather",
    )(ys, dest_w)


def _combine_kernel(x2_ref, gate_ref, fg_ref, yg_ref, o_ref):
    g = gate_ref[...]
    x2 = x2_ref[...]
    c = x2.shape[1] // 2
    acc_hi, acc_lo = x2[:, :c], x2[:, c:]
    for kk in range(TOP_K):
        hi, lo = _unpack_halves(yg_ref[kk])
        acc_hi = acc_hi + g[:, kk:kk + 1] * hi
        acc_lo = acc_lo + g[:, kk:kk + 1] * lo
    o_ref[...] = _rms(jnp.concatenate([acc_hi, acc_lo], axis=1), fg_ref[...])


def _combine(x2, gate_tm, final_g, yg, ts):
    n, d = x2.shape
    dw = yg.shape[1]
    return pl.pallas_call(
        _combine_kernel,
        grid=(n // ts,),
        in_specs=[pl.BlockSpec((ts, d), lambda i: (i, 0)),
                  pl.BlockSpec((ts, LANES), lambda i: (i, 0)),
                  pl.BlockSpec((1, d), lambda i: (0, 0)),
                  pl.BlockSpec((TOP_K, ts, dw), lambda i: (0, i, 0))],
        out_specs=pl.BlockSpec((ts, d), lambda i: (i, 0)),
        out_shape=jax.ShapeDtypeStruct((n, d), F32),
        compiler_params=_cparams(1),
        name="combine",
    )(x2, gate_tm, final_g, yg.reshape(TOP_K, n, dw))


def _experts_kernel(be_ref, first_ref, halves_ref, nxt_ref, slot_ref, nb_ref,
                    xs_ref, bgu_ref, bd_ref, wgu_hbm, wd_hbm, ys_ref,
                    wgu_st, wd_st, wgu_bf, wd_bf, sem_w):
    dff = wd_bf.shape[0]
    bm = xs_ref.shape[0] // EXPERT_BLOCKS_PER_STEP

    def fetch(ex, sl):
        return (pltpu.make_async_copy(wgu_hbm.at[ex], wgu_st.at[sl], sem_w.at[0, sl]),
                pltpu.make_async_copy(wd_hbm.at[ex], wd_st.at[sl], sem_w.at[1, sl]))

    @pl.when(pl.program_id(0) == 0)
    def _():
        e0 = be_ref[0]
        for cp in fetch(e0, slot_ref[e0]):
            cp.start()

    for sub in range(EXPERT_BLOCKS_PER_STEP):
        blk = pl.program_id(0) * EXPERT_BLOCKS_PER_STEP + sub
        e = be_ref[blk]
        rows = slice(sub * bm, (sub + 1) * bm)

        @pl.when(first_ref[blk] == 1)
        def _(e=e):
            sl = slot_ref[e]
            for cp in fetch(e, sl):
                cp.wait()
            nx = nxt_ref[e]

            @pl.when(nx >= 0)
            def _():
                for cp in fetch(nx, 1 - sl):
                    cp.start(priority=1)

            wgu_bf[...] = wgu_st[sl].astype(BF16)
            wd_bf[...] = wd_st[sl].astype(BF16)

        def mlp(e, rows):
            xb = jnp.concatenate(_unpack_halves(xs_ref[rows, :]), axis=1).astype(BF16)
            gu = _dot(xb, wgu_bf[...]) + bgu_ref[e]
            x_glu = jnp.minimum(gu[:, :dff], SWIGLU_LIMIT)
            x_lin = jnp.clip(gu[:, dff:], -SWIGLU_LIMIT, SWIGLU_LIMIT)
            act = x_glu * jax.nn.sigmoid(SWIGLU_ALPHA * x_glu) * (x_lin + 1.0)
            ys_ref[rows, :] = _pack_halves(_dot(act.astype(BF16), wd_bf[...]) + bd_ref[e])

        def clear(rows):
            ys_ref[rows, :] = jnp.zeros((rows.stop - rows.start, ys_ref.shape[1]), jnp.uint32)

        head = slice(sub * bm, sub * bm + bm // 2)
        tail = slice(sub * bm + bm // 2, (sub + 1) * bm)

        @pl.when(halves_ref[blk] == 2)
        def _(e=e, rows=rows):
            mlp(e, rows)

        @pl.when(halves_ref[blk] == 1)
        def _(e=e, head=head, tail=tail):
            mlp(e, head)
            clear(tail)

        @pl.when(halves_ref[blk] == 0)
        def _(rows=rows):
            clear(rows)


def _expert_schedule(counts, bm, n_blocks):
    ne = counts.shape[0]
    ids = jnp.arange(ne, dtype=I32)
    upto = ids[None, :] <= ids[:, None]
    padded = (counts + bm - 1) // bm * bm
    pend = jnp.sum(jnp.where(upto, padded[None, :], 0), axis=1)
    pstart = (pend - padded).astype(I32)
    nb_used = (jnp.sum(padded) // bm).astype(I32)
    blk = jnp.arange(n_blocks, dtype=I32)
    blk_c = jnp.minimum(blk, nb_used - 1)
    be = jnp.minimum(jnp.sum((pend[None, :] <= (blk_c * bm)[:, None]).astype(I32), axis=1),
                     ne - 1)
    started = jnp.sum(((pstart[None, :] == (blk * bm)[:, None]) & (padded[None, :] > 0))
                      .astype(I32), axis=1)
    first = ((blk < nb_used) & (started > 0)).astype(I32)
    used = counts > 0
    seq = jnp.sum(jnp.where(upto & used[None, :], 1, 0), axis=1) - 1
    later = used[None, :] & (ids[None, :] > ids[:, None])
    nxt = jnp.where(jnp.any(later, axis=1), jnp.argmax(later, axis=1), -1).astype(I32)
    slot = (seq & 1).astype(I32)
    hit = ids[None, :] == be[:, None]
    left = jnp.sum(jnp.where(hit, (pstart + counts)[None, :], 0), axis=1) - blk * bm
    halves = jnp.where(blk < nb_used, jnp.where(left > bm // 2, 2, 1), 0).astype(I32)
    return pstart, nb_used.reshape(1), be, first, halves, nxt, slot


def _experts(be, first, halves, nxt, slot, nb_used, xs, w_gu, b_gu, w_down, b_down, bm):
    rows, dw = xs.shape
    ne, d, dff2 = w_gu.shape
    dff = dff2 // 2
    sb = EXPERT_BLOCKS_PER_STEP * bm
    assert rows % sb == 0
    last = lambda j, be, fi, ha, nx, sl, nb: (
        jnp.maximum(jnp.minimum(j, (nb[0] - 1) // EXPERT_BLOCKS_PER_STEP), 0), 0)
    const = lambda *shape: pl.BlockSpec(shape,
                                        lambda j, be, fi, ha, nx, sl, nb: (0,) * len(shape))
    grid_spec = pltpu.PrefetchScalarGridSpec(
        num_scalar_prefetch=6,
        grid=(rows // sb,),
        in_specs=[
            pl.BlockSpec((sb, dw), last),
            const(ne, 1, dff2),
            const(ne, 1, d),
            pl.BlockSpec(memory_space=pl.ANY),
            pl.BlockSpec(memory_space=pl.ANY),
        ],
        out_specs=pl.BlockSpec((sb, dw), lambda j, be, fi, ha, nx, sl, nb: (j, 0)),
        scratch_shapes=[pltpu.VMEM((2, d, dff2), F32), pltpu.VMEM((2, dff, d), F32),
                        pltpu.VMEM((d, dff2), BF16), pltpu.VMEM((dff, d), BF16),
                        pltpu.SemaphoreType.DMA((2, 2))],
    )
    return pl.pallas_call(
        _experts_kernel,
        grid_spec=grid_spec,
        out_shape=jax.ShapeDtypeStruct((rows, dw), jnp.uint32),
        compiler_params=_cparams(1),
        name="experts",
    )(be, first, halves, nxt, slot, nb_used, xs, b_gu.reshape(ne, 1, dff2),
      b_down.reshape(ne, 1, d), w_gu, w_down)


MIXER_TILE = 512
ATTN_HEADS_PER_STEP = 2
POST_TILE = 512
DEST_TILE = 2048
COMBINE_TILE = 1024
EXPERT_BLOCK = 512
EXPERT_BLOCKS_PER_STEP = 2


def kernel(x, mem, norm_mix_g, w_in, lambda_q1, lambda_k1, lambda_q2, lambda_k2, rel_bias,
           subln_g, w_attn_up, pool_mix, pool_scale, w_pool_up, w_gate, b_gate, w_out,
           norm_x_g, norm_mem_g, w_xq, w_xkv, w_xo, norm_ffn_g, w_router, b_router,
           w_gu, b_gu, w_down, b_down, final_norm_g):
    b, s, d = x.shape
    n = b * s
    assert w_in.shape[0] == 1, "single-layer block"
    row = lambda a: a.reshape(1, -1)
    bf = lambda a: a[0].astype(BF16)

    q, k, v, g0, gyb = _mixer_in(
        x, row(norm_mix_g[0]), bf(w_in), bf(w_gate), row(b_gate[0]), bf(pool_mix),
        row(pool_scale[0]), bf(w_pool_up), MIXER_TILE)

    bias, lam = _rel_bias(rel_bias, row(lambda_q1[0]), row(lambda_k1[0]),
                          row(lambda_q2[0]), row(lambda_k2[0]), s)
    a = _diff_attn(q, k, v, bias, lam, row(subln_g[0]), ATTN_HEADS_PER_STEP)

    k_mem, v_mem = _mem_kv(mem, row(norm_mem_g[0]), bf(w_xkv))

    x2, hf, eidx, rank, gate_tm, counts = _post(
        x.reshape(n, d), a.reshape(n, d), g0, gyb, bf(w_attn_up), bf(w_out),
        row(norm_x_g[0]), bf(w_xq), k_mem, v_mem, bf(w_xo), row(norm_ffn_g[0]),
        w_router[0].T, b_router[0].reshape(-1, 1), b, POST_TILE)

    bm = EXPERT_BLOCK
    rows = n * TOP_K + N_EXPERTS * bm
    pstart, nb_used, be, first, halves, nxt, slot = _expert_schedule(
        counts[:, 0], bm, rows // bm)

    dest_w = _worker_index_layout(_route_dest(pstart, eidx, rank, DEST_TILE), n)
    xs = _sc_dispatch(hf, dest_w, rows)
    ys = _experts(be, first, halves, nxt, slot, nb_used, xs, w_gu[0], b_gu[0], w_down[0],
                  b_down[0], bm)
    yg = _sc_gather(ys, dest_w, n)
    out = _combine(x2, gate_tm, row(final_norm_g), yg, COMBINE_TILE)
    return out.reshape(b, s, d)
```

```python
import functools
import math

import numpy as np
import jax
import jax.numpy as jnp
from jax import lax
from jax.experimental import pallas as pl
from jax.experimental.pallas import tpu as pltpu
from jax.experimental.pallas import tpu_sc as plsc

F32 = jnp.float32
BF16 = jnp.bfloat16
I32 = jnp.int32

EPS = 1e-6
CHUNK = 64
N_HEADS = 8
HEAD_DIM = 64
HEAD_W = 2 * HEAD_DIM
POOL_WINDOWS = (2, 4, 8, 16)
POOL_GROUP = 128
POOL_PAD = 16
N_BUCKETS = 32
MAX_DISTANCE = 128
X_HEADS = 4
N_EXPERTS = 32
TOP_K = 4
SWIGLU_ALPHA = 1.702
SWIGLU_LIMIT = 7.0
LAMBDA_INIT = 0.8 - 0.6 * math.exp(-0.3 * 0)

LANES = 128
SUBLANES = 8
ATTN_BLOCK = 256
VMEM_LIMIT = 56 * 1024 * 1024


def _cparams(n_axes, vmem=VMEM_LIMIT):
    return pltpu.CompilerParams(
        dimension_semantics=("arbitrary",) * n_axes, vmem_limit_bytes=vmem)


def _rms(xf, g):
    ms = jnp.mean(xf * xf, axis=-1, keepdims=True)
    return xf * lax.rsqrt(ms + EPS) * g


def _dot(a, b):
    return jnp.dot(a, b, preferred_element_type=F32)


def _pack_halves(x):
    c = x.shape[1] // 2
    as_bits = lambda v: lax.bitcast_convert_type(v.astype(BF16).astype(F32), jnp.uint32)
    return as_bits(x[:, :c]) | (as_bits(x[:, c:]) >> 16)


def _unpack_halves(w):
    hi = lax.bitcast_convert_type(w & jnp.uint32(0xFFFF0000), F32)
    lo = lax.bitcast_convert_type(w << 16, F32)
    return hi, lo


def _dot_nt(a, b, precision=None):
    return lax.dot_general(a, b, (((1,), (1,)), ((), ())),
                           preferred_element_type=F32, precision=precision)


def _mixer_in_kernel(x_ref, g_ref, wq_ref, wk_ref, wv_ref, wu_ref, wg_ref, bg_ref,
                     pmix_ref, pscale_ref, wpu_ref,
                     q_ref, k_ref, v_ref, g0_ref, gyb_ref, ext_ref):
    ts = x_ref.shape[1]
    d = x_ref.shape[2]
    j = pl.program_id(1)
    h = _rms(x_ref[0], g_ref[...]).astype(BF16)
    q_ref[0] = (_dot(h, wq_ref[...]) * (HEAD_DIM ** -0.5)).astype(BF16)
    k_ref[0] = _dot(h, wk_ref[...]).astype(BF16)
    v_ref[0] = _dot(h, wv_ref[...]).astype(BF16)
    u = _dot(h, wu_ref[...])

    @pl.when(j == 0)
    def _():
        ext_ref[0:POOL_PAD, :] = jnp.zeros((POOL_PAD, u.shape[1]), F32)

    ext_ref[POOL_PAD:POOL_PAD + ts, :] = u
    e = ext_ref[...]
    sums = {}
    s = e
    w = 1
    while w < max(POOL_WINDOWS):
        s = s + pltpu.roll(s, w, 0)
        w *= 2
        sums[w] = s
    ext_ref[0:POOL_PAD, :] = ext_ref[ts:ts + POOL_PAD, :]

    pos = (j * ts + lax.broadcasted_iota(I32, (ts, 1), 0) + 1).astype(F32)
    mixed = []
    for gi, w in enumerate(POOL_WINDOWS):
        sl = slice(gi * POOL_GROUP, (gi + 1) * POOL_GROUP)
        win = sums[w][POOL_PAD:, sl]
        pooled = win / jnp.minimum(pos, float(w)) - u[:, sl]
        mixed.append(_dot(pooled.astype(BF16), pmix_ref[gi]) * pscale_ref[:, sl])
    mixed = jnp.concatenate(mixed, axis=1).astype(BF16)
    y_b = _dot(mixed, wpu_ref[...])

    gate = jax.nn.sigmoid(_dot(h, wg_ref[...]) + bg_ref[...])
    g0_ref[...] = gate[:, :d].astype(BF16)
    gyb_ref[...] = (gate[:, d:] * y_b).astype(BF16)


def _mixer_in(x, norm_g, w_in, w_gate, b_gate, pool_mix, pool_scale, w_pool_up, ts):
    b, s, d = x.shape
    aw = N_HEADS * HEAD_W
    pw = len(POOL_WINDOWS) * POOL_GROUP
    n = b * s
    nt = s // ts
    const = lambda *shape: pl.BlockSpec(shape, lambda bi, j: (0,) * len(shape))
    tok3 = pl.BlockSpec((1, ts, aw), lambda bi, j: (bi, j, 0))
    tok2 = pl.BlockSpec((ts, d), lambda bi, j: (bi * nt + j, 0))
    return pl.pallas_call(
        _mixer_in_kernel,
        grid=(b, nt),
        in_specs=[
            pl.BlockSpec((1, ts, d), lambda bi, j: (bi, j, 0)),
            const(1, d),
            pl.BlockSpec((d, aw), lambda bi, j: (0, 0)),
            pl.BlockSpec((d, aw), lambda bi, j: (0, 1)),
            pl.BlockSpec((d, aw), lambda bi, j: (0, 2)),
            pl.BlockSpec((d, pw), lambda bi, j: (0, 3 * aw // pw)),
            const(d, 2 * d),
            const(1, 2 * d),
            const(len(POOL_WINDOWS), POOL_GROUP, POOL_GROUP),
            const(1, pw),
            const(pw, d),
        ],
        out_specs=[tok3, tok3, tok3, tok2, tok2],
        out_shape=[jax.ShapeDtypeStruct((b, s, aw), BF16)] * 3
        + [jax.ShapeDtypeStruct((n, d), BF16)] * 2,
        scratch_shapes=[pltpu.VMEM((ts + POOL_PAD, pw), F32)],
        compiler_params=_cparams(2),
        name="mixer_in",
    )(x, norm_g, w_in, w_in, w_in, w_in, w_gate, b_gate, pool_mix, pool_scale, w_pool_up)


def _rel_bucket(rel, log=jnp.log, f32=lambda a: a.astype(jnp.float32),
                i32=lambda a: a.astype(jnp.int32), xp=jnp):
    nb = N_BUCKETS // 2
    ret = i32(rel > 0) * nb
    n = xp.abs(rel)
    max_exact = nb // 2
    nf = f32(xp.maximum(n, 1))
    large = max_exact + i32(log(nf / max_exact) / math.log(MAX_DISTANCE / max_exact)
                            * (nb - max_exact))
    large = xp.minimum(large, nb - 1)
    return ret + xp.where(n < max_exact, n, large)


def _far_bucket(block, seq):
    rel = -np.arange(block + 1, max(seq, block + 2), dtype=np.int32)
    bk = _rel_bucket(rel, log=np.log, f32=lambda a: a.astype(np.float32),
                     i32=lambda a: a.astype(np.int32), xp=np)
    assert (bk == bk[0]).all(), "far keys must share one relative-position bucket"
    return int(bk[0])


def _rel_bias_kernel(far_bucket, tab_ref, bidx_ref, lq1_ref, lk1_ref, lq2_ref, lk2_ref,
                     bias_ref, lam_ref):
    h = pl.program_id(0)
    bidx = bidx_ref[...]
    acc = jnp.zeros(bidx.shape, F32)
    for bkt in range(N_BUCKETS):
        acc = jnp.where(bidx == bkt, tab_ref[bkt, h], acc)
    acc = acc - tab_ref[far_bucket, h]
    bias_ref[0] = jnp.where(bidx < 0, -jnp.inf, acc)
    lam = (jnp.exp(jnp.sum(lq1_ref[...] * lk1_ref[...], keepdims=True))
           - jnp.exp(jnp.sum(lq2_ref[...] * lk2_ref[...], keepdims=True)) + LAMBDA_INIT)
    lam_ref[...] = jnp.broadcast_to(lam, lam_ref.shape)


def _rel_bias(rel_bias, lq1, lk1, lq2, lk2, seq):
    blk = ATTN_BLOCK
    qpos = jnp.arange(blk, dtype=I32)[:, None]
    kpos = jnp.arange(blk, dtype=I32)[None, :]
    diag = jnp.where(kpos // CHUNK <= qpos // CHUNK, _rel_bucket(kpos - qpos), -1)
    prev = _rel_bucket(kpos - (qpos + blk))
    bidx = jnp.stack([diag, prev]).astype(I32)
    vec = pl.BlockSpec((1, HEAD_DIM), lambda h: (0, 0))
    return pl.pallas_call(
        functools.partial(_rel_bias_kernel, _far_bucket(blk, seq)),
        grid=(N_HEADS,),
        in_specs=[
            pl.BlockSpec(memory_space=pltpu.SMEM),
            pl.BlockSpec((2, blk, blk), lambda h: (0, 0, 0)),
            vec, vec, vec, vec,
        ],
        out_specs=[
            pl.BlockSpec((1, 2, blk, blk), lambda h: (h, 0, 0, 0)),
            pl.BlockSpec((SUBLANES, LANES), lambda h: (0, 0)),
        ],
        out_shape=[jax.ShapeDtypeStruct((N_HEADS, 2, blk, blk), F32),
                   jax.ShapeDtypeStruct((SUBLANES, LANES), F32)],
        compiler_params=_cparams(1),
        name="rel_bias",
    )(rel_bias, bidx, lq1, lk1, lq2, lk2)


def _diff_attn_kernel(q_ref, k_ref, v_ref, bias_ref, lam_ref, sg_ref, o_ref, vext_ref):
    s_len = q_ref.shape[1]
    tq = ATTN_BLOCK
    n_heads = q_ref.shape[2] // HEAD_W
    lam = lam_ref[0:1, 0:1]
    lane = lax.broadcasted_iota(I32, (tq, HEAD_W), 1)
    cols = [slice(hh * HEAD_W, (hh + 1) * HEAD_W) for hh in range(n_heads)]
    b_diag, b_prev = [], []
    for hh in range(n_heads):
        vext_ref[hh, :, :HEAD_W] = v_ref[0, :, cols[hh]]
        vext_ref[hh, :, HEAD_W:] = jnp.ones((s_len, HEAD_W), BF16)
        b_diag.append(jnp.concatenate([bias_ref[hh, 0]] * 2, axis=0))
        b_prev.append(jnp.concatenate([bias_ref[hh, 1]] * 2, axis=0))

    for qi in range(s_len // tq):
        n_keys = (qi + 1) * tq
        rows = slice(qi * tq, (qi + 1) * tq)
        for hh in range(n_heads):
            q = q_ref[0, rows, cols[hh]]
            zero = jnp.zeros_like(q)
            qs = jnp.concatenate([jnp.where(lane < HEAD_DIM, q, zero),
                                  jnp.where(lane >= HEAD_DIM, q, zero)], axis=0)
            s = _dot_nt(qs, k_ref[0, :n_keys, cols[hh]])
            pieces = []
            if qi >= 2:
                pieces.append(s[:, :n_keys - 2 * tq])
            if qi >= 1:
                pieces.append(s[:, n_keys - 2 * tq:n_keys - tq] + b_prev[hh])
            pieces.append(s[:, n_keys - tq:] + b_diag[hh])
            s = jnp.concatenate(pieces, axis=1) if len(pieces) > 1 else pieces[0]
            m = jnp.max(s, axis=-1, keepdims=True)
            p = jnp.exp(s - m).astype(BF16)
            acc = _dot(p, vext_ref[hh, :n_keys, :])
            o = acc[:, :HEAD_W] / acc[:, HEAD_W:HEAD_W + 1]
            a = o[:tq] - lam * o[tq:]
            o_ref[0, rows, cols[hh]] = (
                _rms(a, sg_ref[...]) * (1.0 - LAMBDA_INIT)).astype(BF16)


def _diff_attn(q, k, v, bias, lam, subln_g, heads_per_step):
    b, s, aw = q.shape
    tq = ATTN_BLOCK
    hp = heads_per_step
    spec = pl.BlockSpec((1, s, hp * HEAD_W), lambda bi, h: (bi, 0, h))
    return pl.pallas_call(
        _diff_attn_kernel,
        grid=(b, N_HEADS // hp),
        in_specs=[
            spec, spec, spec,
            pl.BlockSpec((hp, 2, tq, tq), lambda bi, h: (h, 0, 0, 0)),
            pl.BlockSpec((SUBLANES, LANES), lambda bi, h: (0, 0)),
            pl.BlockSpec((1, HEAD_W), lambda bi, h: (0, 0)),
        ],
        out_specs=spec,
        out_shape=jax.ShapeDtypeStruct((b, s, aw), BF16),
        scratch_shapes=[pltpu.VMEM((hp, s, 2 * HEAD_W), BF16)],
        compiler_params=_cparams(2),
        name="diff_attn",
    )(q, k, v, bias, lam, subln_g)


def _mem_kv_kernel(m_ref, g_ref, w_ref, k_ref, v_ref):
    d = m_ref.shape[2]
    m = _rms(m_ref[0], g_ref[...]).astype(BF16)
    kv = _dot(m, w_ref[...])
    k_ref[0] = kv[:, :d].astype(BF16)
    v_ref[0] = kv[:, d:].astype(BF16)


def _mem_kv(mem, norm_g, w_xkv):
    b, ml, d = mem.shape
    blk = pl.BlockSpec((1, ml, d), lambda bi: (bi, 0, 0))
    return pl.pallas_call(
        _mem_kv_kernel,
        grid=(b,),
        in_specs=[blk, pl.BlockSpec((1, d), lambda bi: (0, 0)),
                  pl.BlockSpec((d, 2 * d), lambda bi: (0, 0))],
        out_specs=[blk, blk],
        out_shape=[jax.ShapeDtypeStruct((b, ml, d), BF16)] * 2,
        compiler_params=_cparams(1),
        name="mem_kv",
    )(mem, norm_g, w_xkv)


def _post_kernel(x_ref, a_ref, g0_ref, gyb_ref, wau_ref, wo_ref, nxg_ref, wxq_ref,
                 km_ref, vm_ref, wxo_ref, nfg_ref, wrt_ref, br_ref,
                 x2_ref, hf_ref, eidx_ref, rank_ref, gate_ref, cnt_ref, run_ref):
    ts, d = x_ref.shape
    first = (pl.program_id(0) == 0) & (pl.program_id(1) == 0)

    y_a = _dot(a_ref[...], wau_ref[...])
    merged = g0_ref[...].astype(F32) * y_a + gyb_ref[...].astype(F32)
    x1 = x_ref[...] + _dot(merged.astype(BF16), wo_ref[...])

    hd = d // X_HEADS
    hx = _rms(x1, nxg_ref[...]).astype(BF16)
    qx = (_dot(hx, wxq_ref[...]) * (hd ** -0.5)).astype(BF16)
    heads = []
    for hh in range(X_HEADS):
        sl = slice(hh * hd, (hh + 1) * hd)
        s = _dot_nt(qx[:, sl], km_ref[0, :, sl])
        p = jnp.exp(s - jnp.max(s, axis=-1, keepdims=True))
        p = p / jnp.sum(p, axis=-1, keepdims=True)
        heads.append(_dot(p.astype(BF16), vm_ref[0, :, sl]))
    o = jnp.concatenate(heads, axis=1).astype(BF16)
    x2 = x1 + _dot(o, wxo_ref[...])
    x2_ref[...] = x2
    hf = _rms(x2, nfg_ref[...])
    hf_ref[...] = _pack_halves(hf)

    logits = _dot_nt(wrt_ref[...], hf, precision=lax.Precision.HIGHEST) + br_ref[...]
    ne = logits.shape[0]
    eid = lax.broadcasted_iota(I32, logits.shape, 0).astype(F32)
    work = logits
    vals, idxs, hots = [], [], []
    for _ in range(TOP_K):
        mx = jnp.max(work, axis=0, keepdims=True)
        idx = jnp.min(jnp.where(work == mx, eid, float(ne)), axis=0, keepdims=True)
        hot = eid == idx
        vals.append(mx)
        idxs.append(idx.astype(I32))
        hots.append(hot)
        work = jnp.where(hot, -jnp.inf, work)
    ex = [jnp.exp(vv - vals[0]) for vv in vals]
    den = ex[0] + ex[1] + ex[2] + ex[3]
    gates = [e_ / den for e_ in ex]

    multi = (hots[0] | hots[1] | hots[2] | hots[3])
    multi_f = jnp.where(multi, 1.0, 0.0).astype(F32)
    tri = (lax.broadcasted_iota(I32, (ts, ts), 0)
           < lax.broadcasted_iota(I32, (ts, ts), 1))
    before = _dot(multi_f.astype(BF16), jnp.where(tri, 1.0, 0.0).astype(BF16))

    @pl.when(first)
    def _():
        run_ref[...] = jnp.zeros(run_ref.shape, F32)

    run = run_ref[...]
    pos = before + run
    ranks = [jnp.sum(jnp.where(hot, pos, 0.0), axis=0, keepdims=True) for hot in hots]
    run_new = run + jnp.sum(multi_f, axis=1, keepdims=True)
    run_ref[...] = run_new
    cnt_ref[...] = jnp.broadcast_to(run_new, cnt_ref.shape).astype(I32)

    eidx_ref[...] = jnp.concatenate(idxs, axis=0)
    rank_ref[...] = jnp.concatenate(ranks, axis=0).astype(I32)
    g_rows = jnp.concatenate(gates + [jnp.zeros((LANES - TOP_K, ts), F32)], axis=0)
    gate_ref[...] = g_rows.T


def _post(x2d, a2d, g0, gyb, w_attn_up, w_out, norm_x_g, w_xq, k_mem, v_mem, w_xo,
          norm_ffn_g, w_router_t, b_router, batch, ts):
    n, d = x2d.shape
    nt = n // batch // ts
    ml = k_mem.shape[1]
    tok = pl.BlockSpec((ts, d), lambda bi, j: (bi * nt + j, 0))
    lanes = pl.BlockSpec((TOP_K, ts), lambda bi, j: (0, bi * nt + j))
    const = lambda *shape: pl.BlockSpec(shape, lambda bi, j: (0,) * len(shape),
                                        pipeline_mode=pl.Buffered(1))
    mem = pl.BlockSpec((1, ml, d), lambda bi, j: (bi, 0, 0))
    return pl.pallas_call(
        _post_kernel,
        grid=(batch, nt),
        in_specs=[tok, tok, tok, tok, const(d, d), const(d, d), const(1, d), const(d, d),
                  mem, mem, const(d, d), const(1, d), const(N_EXPERTS, d),
                  const(N_EXPERTS, 1)],
        out_specs=[tok, pl.BlockSpec((ts, d // 2), lambda bi, j: (bi * nt + j, 0)),
                   lanes, lanes,
                   pl.BlockSpec((ts, LANES), lambda bi, j: (bi * nt + j, 0)),
                   pl.BlockSpec((N_EXPERTS, LANES), lambda bi, j: (0, 0))],
        out_shape=[jax.ShapeDtypeStruct((n, d), F32),
                   jax.ShapeDtypeStruct((n, d // 2), jnp.uint32),
                   jax.ShapeDtypeStruct((TOP_K, n), I32), jax.ShapeDtypeStruct((TOP_K, n), I32),
                   jax.ShapeDtypeStruct((n, LANES), F32),
                   jax.ShapeDtypeStruct((N_EXPERTS, LANES), I32)],
        scratch_shapes=[pltpu.VMEM((N_EXPERTS, 1), F32)],
        compiler_params=_cparams(2),
        name="post",
    )(x2d, a2d, g0, gyb, w_attn_up, w_out, norm_x_g, w_xq, k_mem, v_mem, w_xo,
      norm_ffn_g, w_router_t, b_router)


def _route_dest_kernel(pstart_ref, eidx_ref, rank_ref, dest_ref):
    eidx = eidx_ref[...]
    start = jnp.zeros(eidx.shape, I32)
    for e in range(N_EXPERTS):
        start = jnp.where(eidx == e, pstart_ref[e], start)
    dest_ref[...] = start + rank_ref[...]


def _route_dest(pstart, eidx, rank, tl):
    k, n = eidx.shape
    lanes = pl.BlockSpec((k, tl), lambda i: (0, i))
    return pl.pallas_call(
        _route_dest_kernel,
        grid=(n // tl,),
        in_specs=[pl.BlockSpec(memory_space=pltpu.SMEM), lanes, lanes],
        out_specs=lanes,
        out_shape=jax.ShapeDtypeStruct((k, n), I32),
        compiler_params=_cparams(1),
        name="route_dest",
    )(pstart, eidx, rank)


SC_CORES = 2
SC_SUBCORES = 16
SC_WORKERS = SC_CORES * SC_SUBCORES
MOVE_CHUNK = 64
MOVE_BUFFERS = 3


def _worker_index_layout(dest, n):
    per_w = n // SC_WORKERS
    n_chunks = per_w // MOVE_CHUNK
    d4 = dest.reshape(TOP_K, SC_WORKERS, n_chunks, MOVE_CHUNK)
    return jnp.transpose(d4, (1, 2, 0, 3)).reshape(SC_WORKERS, n_chunks * TOP_K, MOVE_CHUNK)


def _sc_mesh():
    return plsc.VectorSubcoreMesh(core_axis_name="c", subcore_axis_name="s",
                                  num_cores=SC_CORES, num_subcores=SC_SUBCORES)


def _sc_dispatch(hf, dest_w, rows):
    n, d = hf.shape
    per_w = n // SC_WORKERS
    n_chunks = per_w // MOVE_CHUNK

    def body(hf_hbm, dest_hbm, xs_hbm, idx_v, rows_v, rsem, wsem):
        wid = lax.axis_index("s") * SC_CORES + lax.axis_index("c")
        base = wid * per_w
        pltpu.sync_copy(dest_hbm.at[wid], idx_v)

        def read(c):
            b = c % MOVE_BUFFERS
            return pltpu.async_copy(hf_hbm.at[pl.ds(base + c * MOVE_CHUNK, MOVE_CHUNK)],
                                    rows_v.at[b], rsem.at[b])

        scatters = [[] for _ in range(MOVE_BUFFERS)]
        pending = read(0)
        for c in range(n_chunks):
            b = c % MOVE_BUFFERS
            pending.wait()
            if c + 1 < n_chunks:
                for cp in scatters[(c + 1) % MOVE_BUFFERS]:
                    cp.wait()
                pending = read(c + 1)
            scatters[b] = [
                pltpu.async_copy(rows_v.at[b], xs_hbm.at[idx_v.at[c * TOP_K + kk]], wsem.at[b])
                for kk in range(TOP_K)]
        for group in scatters:
            for cp in group:
                cp.wait()

    return pl.kernel(
        body,
        out_type=jax.ShapeDtypeStruct((rows, d), hf.dtype),
        mesh=_sc_mesh(),
        scratch_types=[pltpu.VMEM((n_chunks * TOP_K, MOVE_CHUNK), I32),
                       pltpu.VMEM((MOVE_BUFFERS, MOVE_CHUNK, d), hf.dtype),
                       pltpu.SemaphoreType.DMA((MOVE_BUFFERS,)),
                       pltpu.SemaphoreType.DMA((MOVE_BUFFERS,))],
        name="sc_dispatch",
    )(hf, dest_w)


def _sc_gather(ys, dest_w, n):
    _, d = ys.shape
    per_w = n // SC_WORKERS
    n_chunks = per_w // MOVE_CHUNK

    def body(ys_hbm, dest_hbm, yg_hbm, idx_v, rows_v, rsem, wsem):
        wid = lax.axis_index("s") * SC_CORES + lax.axis_index("c")
        base = wid * per_w
        pltpu.sync_copy(dest_hbm.at[wid], idx_v)
        n_moves = n_chunks * TOP_K

        def gather(m):
            b = m % MOVE_BUFFERS
            return pltpu.async_copy(ys_hbm.at[idx_v.at[m]], rows_v.at[b], rsem.at[b])

        def write(m):
            b = m % MOVE_BUFFERS
            c, kk = divmod(m, TOP_K)
            return pltpu.async_copy(
                rows_v.at[b], yg_hbm.at[pl.ds(kk * n + base + c * MOVE_CHUNK, MOVE_CHUNK)],
                wsem.at[b])

        writes = [None] * MOVE_BUFFERS
        pending = gather(0)
        for m in range(n_moves):
            pending.wait()
            if m + 1 < n_moves:
                nb = (m + 1) % MOVE_BUFFERS
                if writes[nb] is not None:
                    writes[nb].wait()
                pending = gather(m + 1)
            writes[m % MOVE_BUFFERS] = write(m)
        for wr in writes:
            if wr is not None:
                wr.wait()

    return pl.kernel(
        body,
        out_type=jax.ShapeDtypeStruct((TOP_K * n, d), ys.dtype),
        mesh=_sc_mesh(),
        scratch_types=[pltpu.VMEM((n_chunks * TOP_K, MOVE_CHUNK), I32),
                       pltpu.VMEM((MOVE_BUFFERS, MOVE_CHUNK, d), ys.dtype),
                       pltpu.SemaphoreType.DMA((MOVE_BUFFERS,)),
                       pltpu.SemaphoreType.DMA((MOVE_BUFFERS,))],
        name="sc_gather",
    )(ys, dest_w)


def _combine_kernel(x2_ref, gate_ref, fg_ref, yg_ref, o_ref):
    g = gate_ref[...]
    x2 = x2_ref[...]
    c = x2.shape[1] // 2
    acc_hi, acc_lo = x2[:, :c], x2[:, c:]
    for kk in range(TOP_K):
        hi, lo = _unpack_halves(yg_ref[kk])
        acc_hi = acc_hi + g[:, kk:kk + 1] * hi
        acc_lo = acc_lo + g[:, kk:kk + 1] * lo
    o_ref[...] = _rms(jnp.concatenate([acc_hi, acc_lo], axis=1), fg_ref[...])


def _combine(x2, gate_tm, final_g, yg, ts):
    n, d = x2.shape
    dw = yg.shape[1]
    return pl.pallas_call(
        _combine_kernel,
        grid=(n // ts,),
        in_specs=[pl.BlockSpec((ts, d), lambda i: (i, 0)),
                  pl.BlockSpec((ts, LANES), lambda i: (i, 0)),
                  pl.BlockSpec((1, d), lambda i: (0, 0)),
                  pl.BlockSpec((TOP_K, ts, dw), lambda i: (0, i, 0))],
        out_specs=pl.BlockSpec((ts, d), lambda i: (i, 0)),
        out_shape=jax.ShapeDtypeStruct((n, d), F32),
        compiler_params=_cparams(1),
        name="combine",
    )(x2, gate_tm, final_g, yg.reshape(TOP_K, n, dw))


def _experts_kernel(be_ref, first_ref, parts_ref, nxt_ref, slot_ref, nb_ref,
                    xs_ref, bgu_ref, bd_ref, wgu_hbm, wd_hbm, ys_ref,
                    wgu_st, wd_st, wgu_bf, wd_bf, sem_w):
    dff = wd_bf.shape[0]
    bm = xs_ref.shape[0] // EXPERT_BLOCKS_PER_STEP

    def fetch(ex, sl):
        return (pltpu.make_async_copy(wgu_hbm.at[ex], wgu_st.at[sl], sem_w.at[0, sl]),
                pltpu.make_async_copy(wd_hbm.at[ex], wd_st.at[sl], sem_w.at[1, sl]))

    @pl.when(pl.program_id(0) == 0)
    def _():
        e0 = be_ref[0]
        for cp in fetch(e0, slot_ref[e0]):
            cp.start()

    for sub in range(EXPERT_BLOCKS_PER_STEP):
        blk = pl.program_id(0) * EXPERT_BLOCKS_PER_STEP + sub
        e = be_ref[blk]
        rows = slice(sub * bm, (sub + 1) * bm)

        @pl.when(first_ref[blk] == 1)
        def _(e=e):
            sl = slot_ref[e]
            for cp in fetch(e, sl):
                cp.wait()
            nx = nxt_ref[e]

            @pl.when(nx >= 0)
            def _():
                for cp in fetch(nx, 1 - sl):
                    cp.start(priority=1)

            wgu_bf[...] = wgu_st[sl].astype(BF16)
            wd_bf[...] = wd_st[sl].astype(BF16)

        def mlp(e, rows):
            xb = jnp.concatenate(_unpack_halves(xs_ref[rows, :]), axis=1).astype(BF16)
            gu = _dot(xb, wgu_bf[...]) + bgu_ref[e]
            x_glu = jnp.minimum(gu[:, :dff], SWIGLU_LIMIT)
            x_lin = jnp.clip(gu[:, dff:], -SWIGLU_LIMIT, SWIGLU_LIMIT)
            act = x_glu * jax.nn.sigmoid(SWIGLU_ALPHA * x_glu) * (x_lin + 1.0)
            ys_ref[rows, :] = _pack_halves(_dot(act.astype(BF16), wd_bf[...]) + bd_ref[e])

        def clear(rows):
            ys_ref[rows, :] = jnp.zeros((rows.stop - rows.start, ys_ref.shape[1]), jnp.uint32)

        part = bm // EXPERT_TAIL_PARTS
        for live in range(EXPERT_TAIL_PARTS + 1):
            @pl.when(parts_ref[blk] == live)
            def _(e=e, rows=rows, live=live):
                cut = rows.start + live * part
                if live:
                    mlp(e, slice(rows.start, cut))
                if live < EXPERT_TAIL_PARTS:
                    clear(slice(cut, rows.stop))


def _expert_schedule(counts, bm, n_blocks):
    ne = counts.shape[0]
    ids = jnp.arange(ne, dtype=I32)
    upto = ids[None, :] <= ids[:, None]
    padded = (counts + bm - 1) // bm * bm
    pend = jnp.sum(jnp.where(upto, padded[None, :], 0), axis=1)
    pstart = (pend - padded).astype(I32)
    nb_used = (jnp.sum(padded) // bm).astype(I32)
    blk = jnp.arange(n_blocks, dtype=I32)
    blk_c = jnp.minimum(blk, nb_used - 1)
    be = jnp.minimum(jnp.sum((pend[None, :] <= (blk_c * bm)[:, None]).astype(I32), axis=1),
                     ne - 1)
    started = jnp.sum(((pstart[None, :] == (blk * bm)[:, None]) & (padded[None, :] > 0))
                      .astype(I32), axis=1)
    first = ((blk < nb_used) & (started > 0)).astype(I32)
    used = counts > 0
    seq = jnp.sum(jnp.where(upto & used[None, :], 1, 0), axis=1) - 1
    later = used[None, :] & (ids[None, :] > ids[:, None])
    nxt = jnp.where(jnp.any(later, axis=1), jnp.argmax(later, axis=1), -1).astype(I32)
    slot = (seq & 1).astype(I32)
    hit = ids[None, :] == be[:, None]
    left = jnp.sum(jnp.where(hit, (pstart + counts)[None, :], 0), axis=1) - blk * bm
    part = bm // EXPERT_TAIL_PARTS
    parts = jnp.where(blk < nb_used,
                      jnp.minimum((left + part - 1) // part, EXPERT_TAIL_PARTS), 0).astype(I32)
    return pstart, nb_used.reshape(1), be, first, parts, nxt, slot


def _experts(be, first, parts, nxt, slot, nb_used, xs, w_gu, b_gu, w_down, b_down, bm):
    rows, dw = xs.shape
    ne, d, dff2 = w_gu.shape
    dff = dff2 // 2
    sb = EXPERT_BLOCKS_PER_STEP * bm
    assert rows % sb == 0
    last = lambda j, be, fi, ha, nx, sl, nb: (
        jnp.maximum(jnp.minimum(j, (nb[0] - 1) // EXPERT_BLOCKS_PER_STEP), 0), 0)
    const = lambda *shape: pl.BlockSpec(shape,
                                        lambda j, be, fi, ha, nx, sl, nb: (0,) * len(shape))
    grid_spec = pltpu.PrefetchScalarGridSpec(
        num_scalar_prefetch=6,
        grid=(rows // sb,),
        in_specs=[
            pl.BlockSpec((sb, dw), last),
            const(ne, 1, dff2),
            const(ne, 1, d),
            pl.BlockSpec(memory_space=pl.ANY),
            pl.BlockSpec(memory_space=pl.ANY),
        ],
        out_specs=pl.BlockSpec((sb, dw), lambda j, be, fi, ha, nx, sl, nb: (j, 0)),
        scratch_shapes=[pltpu.VMEM((2, d, dff2), F32), pltpu.VMEM((2, dff, d), F32),
                        pltpu.VMEM((d, dff2), BF16), pltpu.VMEM((dff, d), BF16),
                        pltpu.SemaphoreType.DMA((2, 2))],
    )
    return pl.pallas_call(
        _experts_kernel,
        grid_spec=grid_spec,
        out_shape=jax.ShapeDtypeStruct((rows, dw), jnp.uint32),
        compiler_params=_cparams(1),
        name="experts",
    )(be, first, parts, nxt, slot, nb_used, xs, b_gu.reshape(ne, 1, dff2),
      b_down.reshape(ne, 1, d), w_gu, w_down)


MIXER_TILE = 512
ATTN_HEADS_PER_STEP = 2
POST_TILE = 512
DEST_TILE = 2048
COMBINE_TILE = 1024
EXPERT_BLOCK = 1024
EXPERT_TAIL_PARTS = 4
EXPERT_BLOCKS_PER_STEP = 1


def kernel(x, mem, norm_mix_g, w_in, lambda_q1, lambda_k1, lambda_q2, lambda_k2, rel_bias,
           subln_g, w_attn_up, pool_mix, pool_scale, w_pool_up, w_gate, b_gate, w_out,
           norm_x_g, norm_mem_g, w_xq, w_xkv, w_xo, norm_ffn_g, w_router, b_router,
           w_gu, b_gu, w_down, b_down, final_norm_g):
    b, s, d = x.shape
    n = b * s
    assert w_in.shape[0] == 1, "single-layer block"
    row = lambda a: a.reshape(1, -1)
    bf = lambda a: a[0].astype(BF16)

    q, k, v, g0, gyb = _mixer_in(
        x, row(norm_mix_g[0]), bf(w_in), bf(w_gate), row(b_gate[0]), bf(pool_mix),
        row(pool_scale[0]), bf(w_pool_up), MIXER_TILE)

    bias, lam = _rel_bias(rel_bias, row(lambda_q1[0]), row(lambda_k1[0]),
                          row(lambda_q2[0]), row(lambda_k2[0]), s)
    a = _diff_attn(q, k, v, bias, lam, row(subln_g[0]), ATTN_HEADS_PER_STEP)

    k_mem, v_mem = _mem_kv(mem, row(norm_mem_g[0]), bf(w_xkv))

    x2, hf, eidx, rank, gate_tm, counts = _post(
        x.reshape(n, d), a.reshape(n, d), g0, gyb, bf(w_attn_up), bf(w_out),
        row(norm_x_g[0]), bf(w_xq), k_mem, v_mem, bf(w_xo), row(norm_ffn_g[0]),
        w_router[0].T, b_router[0].reshape(-1, 1), b, POST_TILE)

    bm = EXPERT_BLOCK
    rows = n * TOP_K + N_EXPERTS * bm
    pstart, nb_used, be, first, parts, nxt, slot = _expert_schedule(
        counts[:, 0], bm, rows // bm)

    dest_w = _worker_index_layout(_route_dest(pstart, eidx, rank, DEST_TILE), n)
    xs = _sc_dispatch(hf, dest_w, rows)
    ys = _experts(be, first, parts, nxt, slot, nb_used, xs, w_gu[0], b_gu[0], w_down[0],
                  b_down[0], bm)
    yg = _sc_gather(ys, dest_w, n)
    out = _combine(x2, gate_tm, row(final_norm_g), yg, COMBINE_TILE)
    return out.reshape(b, s, d)
```

```python
import functools
import math

import numpy as np
import jax
import jax.numpy as jnp
from jax import lax
from jax.experimental import pallas as pl
from jax.experimental.pallas import tpu as pltpu
from jax.experimental.pallas import tpu_sc as plsc

F32 = jnp.float32
BF16 = jnp.bfloat16
I32 = jnp.int32

EPS = 1e-6
CHUNK = 64
N_HEADS = 8
HEAD_DIM = 64
HEAD_W = 2 * HEAD_DIM
POOL_WINDOWS = (2, 4, 8, 16)
POOL_GROUP = 128
POOL_PAD = 16
N_BUCKETS = 32
MAX_DISTANCE = 128
X_HEADS = 4
N_EXPERTS = 32
TOP_K = 4
SWIGLU_ALPHA = 1.702
SWIGLU_LIMIT = 7.0
LAMBDA_INIT = 0.8 - 0.6 * math.exp(-0.3 * 0)

LANES = 128
SUBLANES = 8
ATTN_BLOCK = 256
VMEM_LIMIT = 56 * 1024 * 1024


def _cparams(n_axes, vmem=VMEM_LIMIT):
    return pltpu.CompilerParams(
        dimension_semantics=("arbitrary",) * n_axes, vmem_limit_bytes=vmem)


def _rms(xf, g):
    ms = jnp.mean(xf * xf, axis=-1, keepdims=True)
    return xf * lax.rsqrt(ms + EPS) * g


def _dot(a, b):
    return jnp.dot(a, b, preferred_element_type=F32)


def _pack_halves(x):
    c = x.shape[1] // 2
    as_bits = lambda v: lax.bitcast_convert_type(v.astype(BF16).astype(F32), jnp.uint32)
    return as_bits(x[:, :c]) | (as_bits(x[:, c:]) >> 16)


def _unpack_halves(w):
    hi = lax.bitcast_convert_type(w & jnp.uint32(0xFFFF0000), F32)
    lo = lax.bitcast_convert_type(w << 16, F32)
    return hi, lo


def _dot_nt(a, b, precision=None):
    return lax.dot_general(a, b, (((1,), (1,)), ((), ())),
                           preferred_element_type=F32, precision=precision)


def _mixer_in_kernel(x_ref, g_ref, wq_ref, wk_ref, wv_ref, wu_ref, wg_ref, bg_ref,
                     pmix_ref, pscale_ref, wpu_ref,
                     q_ref, k_ref, v_ref, g0_ref, gyb_ref, ext_ref):
    ts = x_ref.shape[1]
    d = x_ref.shape[2]
    j = pl.program_id(1)
    h = _rms(x_ref[0], g_ref[...]).astype(BF16)
    q_ref[0] = (_dot(h, wq_ref[...]) * (HEAD_DIM ** -0.5)).astype(BF16)
    k_ref[0] = _dot(h, wk_ref[...]).astype(BF16)
    v_ref[0] = _dot(h, wv_ref[...]).astype(BF16)
    u = _dot(h, wu_ref[...])

    @pl.when(j == 0)
    def _():
        ext_ref[0:POOL_PAD, :] = jnp.zeros((POOL_PAD, u.shape[1]), F32)

    ext_ref[POOL_PAD:POOL_PAD + ts, :] = u
    e = ext_ref[...]
    sums = {}
    s = e
    w = 1
    while w < max(POOL_WINDOWS):
        s = s + pltpu.roll(s, w, 0)
        w *= 2
        sums[w] = s
    ext_ref[0:POOL_PAD, :] = ext_ref[ts:ts + POOL_PAD, :]

    pos = (j * ts + lax.broadcasted_iota(I32, (ts, 1), 0) + 1).astype(F32)
    mixed = []
    for gi, w in enumerate(POOL_WINDOWS):
        sl = slice(gi * POOL_GROUP, (gi + 1) * POOL_GROUP)
        win = sums[w][POOL_PAD:, sl]
        pooled = win / jnp.minimum(pos, float(w)) - u[:, sl]
        mixed.append(_dot(pooled.astype(BF16), pmix_ref[gi]) * pscale_ref[:, sl])
    mixed = jnp.concatenate(mixed, axis=1).astype(BF16)
    y_b = _dot(mixed, wpu_ref[...])

    gate = jax.nn.sigmoid(_dot(h, wg_ref[...]) + bg_ref[...])
    g0_ref[...] = gate[:, :d].astype(BF16)
    gyb_ref[...] = (gate[:, d:] * y_b).astype(BF16)


def _mixer_in(x, norm_g, w_in, w_gate, b_gate, pool_mix, pool_scale, w_pool_up, ts):
    b, s, d = x.shape
    aw = N_HEADS * HEAD_W
    pw = len(POOL_WINDOWS) * POOL_GROUP
    n = b * s
    nt = s // ts
    const = lambda *shape: pl.BlockSpec(shape, lambda bi, j: (0,) * len(shape))
    tok3 = pl.BlockSpec((1, ts, aw), lambda bi, j: (bi, j, 0))
    tok2 = pl.BlockSpec((ts, d), lambda bi, j: (bi * nt + j, 0))
    return pl.pallas_call(
        _mixer_in_kernel,
        grid=(b, nt),
        in_specs=[
            pl.BlockSpec((1, ts, d), lambda bi, j: (bi, j, 0)),
            const(1, d),
            pl.BlockSpec((d, aw), lambda bi, j: (0, 0)),
            pl.BlockSpec((d, aw), lambda bi, j: (0, 1)),
            pl.BlockSpec((d, aw), lambda bi, j: (0, 2)),
            pl.BlockSpec((d, pw), lambda bi, j: (0, 3 * aw // pw)),
            const(d, 2 * d),
            const(1, 2 * d),
            const(len(POOL_WINDOWS), POOL_GROUP, POOL_GROUP),
            const(1, pw),
            const(pw, d),
        ],
        out_specs=[tok3, tok3, tok3, tok2, tok2],
        out_shape=[jax.ShapeDtypeStruct((b, s, aw), BF16)] * 3
        + [jax.ShapeDtypeStruct((n, d), BF16)] * 2,
        scratch_shapes=[pltpu.VMEM((ts + POOL_PAD, pw), F32)],
        compiler_params=_cparams(2),
        name="mixer_in",
    )(x, norm_g, w_in, w_in, w_in, w_in, w_gate, b_gate, pool_mix, pool_scale, w_pool_up)


def _rel_bucket(rel, log=jnp.log, f32=lambda a: a.astype(jnp.float32),
                i32=lambda a: a.astype(jnp.int32), xp=jnp):
    nb = N_BUCKETS // 2
    ret = i32(rel > 0) * nb
    n = xp.abs(rel)
    max_exact = nb // 2
    nf = f32(xp.maximum(n, 1))
    large = max_exact + i32(log(nf / max_exact) / math.log(MAX_DISTANCE / max_exact)
                            * (nb - max_exact))
    large = xp.minimum(large, nb - 1)
    return ret + xp.where(n < max_exact, n, large)


def _far_bucket(block, seq):
    rel = -np.arange(block + 1, max(seq, block + 2), dtype=np.int32)
    bk = _rel_bucket(rel, log=np.log, f32=lambda a: a.astype(np.float32),
                     i32=lambda a: a.astype(np.int32), xp=np)
    assert (bk == bk[0]).all(), "far keys must share one relative-position bucket"
    return int(bk[0])


def _rel_bias_kernel(far_bucket, tab_ref, bidx_ref, lq1_ref, lk1_ref, lq2_ref, lk2_ref,
                     bias_ref, lam_ref):
    h = pl.program_id(0)
    bidx = bidx_ref[...]
    acc = jnp.zeros(bidx.shape, F32)
    for bkt in range(N_BUCKETS):
        acc = jnp.where(bidx == bkt, tab_ref[bkt, h], acc)
    acc = acc - tab_ref[far_bucket, h]
    bias_ref[0] = jnp.where(bidx < 0, -jnp.inf, acc)
    lam = (jnp.exp(jnp.sum(lq1_ref[...] * lk1_ref[...], keepdims=True))
           - jnp.exp(jnp.sum(lq2_ref[...] * lk2_ref[...], keepdims=True)) + LAMBDA_INIT)
    lam_ref[...] = jnp.broadcast_to(lam, lam_ref.shape)


def _rel_bias(rel_bias, lq1, lk1, lq2, lk2, seq):
    blk = ATTN_BLOCK
    qpos = jnp.arange(blk, dtype=I32)[:, None]
    kpos = jnp.arange(blk, dtype=I32)[None, :]
    diag = jnp.where(kpos // CHUNK <= qpos // CHUNK, _rel_bucket(kpos - qpos), -1)
    prev = _rel_bucket(kpos - (qpos + blk))
    bidx = jnp.stack([diag, prev]).astype(I32)
    vec = pl.BlockSpec((1, HEAD_DIM), lambda h: (0, 0))
    return pl.pallas_call(
        functools.partial(_rel_bias_kernel, _far_bucket(blk, seq)),
        grid=(N_HEADS,),
        in_specs=[
            pl.BlockSpec(memory_space=pltpu.SMEM),
            pl.BlockSpec((2, blk, blk), lambda h: (0, 0, 0)),
            vec, vec, vec, vec,
        ],
        out_specs=[
            pl.BlockSpec((1, 2, blk, blk), lambda h: (h, 0, 0, 0)),
            pl.BlockSpec((SUBLANES, LANES), lambda h: (0, 0)),
        ],
        out_shape=[jax.ShapeDtypeStruct((N_HEADS, 2, blk, blk), F32),
                   jax.ShapeDtypeStruct((SUBLANES, LANES), F32)],
        compiler_params=_cparams(1),
        name="rel_bias",
    )(rel_bias, bidx, lq1, lk1, lq2, lk2)


def _diff_attn_kernel(q_ref, k_ref, v_ref, bias_ref, lam_ref, sg_ref, o_ref, vext_ref):
    s_len = q_ref.shape[1]
    tq = ATTN_BLOCK
    n_heads = q_ref.shape[2] // HEAD_W
    lam = lam_ref[0:1, 0:1]
    lane = lax.broadcasted_iota(I32, (tq, HEAD_W), 1)
    cols = [slice(hh * HEAD_W, (hh + 1) * HEAD_W) for hh in range(n_heads)]
    b_diag, b_prev = [], []
    for hh in range(n_heads):
        vext_ref[hh, :, :HEAD_W] = v_ref[0, :, cols[hh]]
        vext_ref[hh, :, HEAD_W:] = jnp.ones((s_len, HEAD_W), BF16)
        b_diag.append(jnp.concatenate([bias_ref[hh, 0]] * 2, axis=0))
        b_prev.append(jnp.concatenate([bias_ref[hh, 1]] * 2, axis=0))

    for qi in range(s_len // tq):
        n_keys = (qi + 1) * tq
        rows = slice(qi * tq, (qi + 1) * tq)
        for hh in range(n_heads):
            q = q_ref[0, rows, cols[hh]]
            zero = jnp.zeros_like(q)
            qs = jnp.concatenate([jnp.where(lane < HEAD_DIM, q, zero),
                                  jnp.where(lane >= HEAD_DIM, q, zero)], axis=0)
            s = _dot_nt(qs, k_ref[0, :n_keys, cols[hh]])
            pieces = []
            if qi >= 2:
                pieces.append(s[:, :n_keys - 2 * tq])
            if qi >= 1:
                pieces.append(s[:, n_keys - 2 * tq:n_keys - tq] + b_prev[hh])
            pieces.append(s[:, n_keys - tq:] + b_diag[hh])
            s = jnp.concatenate(pieces, axis=1) if len(pieces) > 1 else pieces[0]
            m = jnp.max(s, axis=-1, keepdims=True)
            p = jnp.exp(s - m).astype(BF16)
            acc = _dot(p, vext_ref[hh, :n_keys, :])
            o = acc[:, :HEAD_W] / acc[:, HEAD_W:HEAD_W + 1]
            a = o[:tq] - lam * o[tq:]
            o_ref[0, rows, cols[hh]] = (
                _rms(a, sg_ref[...]) * (1.0 - LAMBDA_INIT)).astype(BF16)


def _diff_attn(q, k, v, bias, lam, subln_g, heads_per_step):
    b, s, aw = q.shape
    tq = ATTN_BLOCK
    hp = heads_per_step
    spec = pl.BlockSpec((1, s, hp * HEAD_W), lambda bi, h: (bi, 0, h))
    return pl.pallas_call(
        _diff_attn_kernel,
        grid=(b, N_HEADS // hp),
        in_specs=[
            spec, spec, spec,
            pl.BlockSpec((hp, 2, tq, tq), lambda bi, h: (h, 0, 0, 0)),
            pl.BlockSpec((SUBLANES, LANES), lambda bi, h: (0, 0)),
            pl.BlockSpec((1, HEAD_W), lambda bi, h: (0, 0)),
        ],
        out_specs=spec,
        out_shape=jax.ShapeDtypeStruct((b, s, aw), BF16),
        scratch_shapes=[pltpu.VMEM((hp, s, 2 * HEAD_W), BF16)],
        compiler_params=_cparams(2),
        name="diff_attn",
    )(q, k, v, bias, lam, subln_g)


def _mem_fold_kernel(m_ref, g_ref, wkv_ref, wq_ref, wo_ref, qk_ref, vo_ref):
    d = m_ref.shape[2]
    hd = d // X_HEADS
    m = _rms(m_ref[0], g_ref[...]).astype(BF16)
    kv = _dot(m, wkv_ref[...])
    k = kv[:, :d].astype(BF16)
    v = kv[:, d:].astype(BF16)
    for hh in range(X_HEADS):
        sl = slice(hh * hd, (hh + 1) * hd)
        qk_ref[0, :, hh * m.shape[0]:(hh + 1) * m.shape[0]] = (
            _dot_nt(wq_ref[:, sl], k[:, sl]) * (hd ** -0.5)).astype(BF16)
        vo_ref[0, hh * m.shape[0]:(hh + 1) * m.shape[0], :] = _dot(v[:, sl], wo_ref[sl, :]).astype(BF16)


def _mem_fold(mem, norm_g, w_xkv, w_xq, w_xo):
    b, ml, d = mem.shape
    const = lambda *shape: pl.BlockSpec(shape, lambda bi: (0,) * len(shape))
    return pl.pallas_call(
        _mem_fold_kernel,
        grid=(b,),
        in_specs=[pl.BlockSpec((1, ml, d), lambda bi: (bi, 0, 0)), const(1, d),
                  const(d, 2 * d), const(d, d), const(d, d)],
        out_specs=[pl.BlockSpec((1, d, X_HEADS * ml), lambda bi: (bi, 0, 0)),
                   pl.BlockSpec((1, X_HEADS * ml, d), lambda bi: (bi, 0, 0))],
        out_shape=[jax.ShapeDtypeStruct((b, d, X_HEADS * ml), BF16),
                   jax.ShapeDtypeStruct((b, X_HEADS * ml, d), BF16)],
        compiler_params=_cparams(1),
        name="mem_fold",
    )(mem, norm_g, w_xkv, w_xq, w_xo)


def _post_kernel(x_ref, a_ref, g0_ref, gyb_ref, wau_ref, wo_ref, nxg_ref,
                 qk_ref, vo_ref, nfg_ref, wrt_ref, br_ref,
                 x2_ref, hf_ref, eidx_ref, rank_ref, gate_ref, cnt_ref, run_ref):
    ts, d = x_ref.shape
    first = (pl.program_id(0) == 0) & (pl.program_id(1) == 0)

    y_a = _dot(a_ref[...], wau_ref[...])
    merged = g0_ref[...].astype(F32) * y_a + gyb_ref[...].astype(F32)
    x1 = x_ref[...] + _dot(merged.astype(BF16), wo_ref[...])

    ml = qk_ref.shape[2] // X_HEADS
    hx = _rms(x1, nxg_ref[...]).astype(BF16)
    s_all = _dot(hx, qk_ref[0])
    probs = []
    for hh in range(X_HEADS):
        s = s_all[:, hh * ml:(hh + 1) * ml]
        p = jnp.exp(s - jnp.max(s, axis=-1, keepdims=True))
        probs.append((p / jnp.sum(p, axis=-1, keepdims=True)).astype(BF16))
    x2 = x1 + _dot(jnp.concatenate(probs, axis=1), vo_ref[0])
    x2_ref[...] = x2
    hf = _rms(x2, nfg_ref[...])
    hf_ref[...] = _pack_halves(hf)

    logits = _dot_nt(wrt_ref[...], hf, precision=lax.Precision.HIGHEST) + br_ref[...]
    ne = logits.shape[0]
    eid = lax.broadcasted_iota(I32, logits.shape, 0).astype(F32)
    work = logits
    vals, idxs, hots = [], [], []
    for _ in range(TOP_K):
        mx = jnp.max(work, axis=0, keepdims=True)
        idx = jnp.min(jnp.where(work == mx, eid, float(ne)), axis=0, keepdims=True)
        hot = eid == idx
        vals.append(mx)
        idxs.append(idx.astype(I32))
        hots.append(hot)
        work = jnp.where(hot, -jnp.inf, work)
    ex = [jnp.exp(vv - vals[0]) for vv in vals]
    den = ex[0] + ex[1] + ex[2] + ex[3]
    gates = [e_ / den for e_ in ex]

    multi = (hots[0] | hots[1] | hots[2] | hots[3])
    multi_f = jnp.where(multi, 1.0, 0.0).astype(F32)
    tri = (lax.broadcasted_iota(I32, (ts, ts), 0)
           < lax.broadcasted_iota(I32, (ts, ts), 1))
    before = _dot(multi_f.astype(BF16), jnp.where(tri, 1.0, 0.0).astype(BF16))

    @pl.when(first)
    def _():
        run_ref[...] = jnp.zeros(run_ref.shape, F32)

    run = run_ref[...]
    pos = before + run
    ranks = [jnp.sum(jnp.where(hot, pos, 0.0), axis=0, keepdims=True) for hot in hots]
    run_new = run + jnp.sum(multi_f, axis=1, keepdims=True)
    run_ref[...] = run_new
    cnt_ref[...] = jnp.broadcast_to(run_new, cnt_ref.shape).astype(I32)

    eidx_ref[...] = jnp.concatenate(idxs, axis=0)
    rank_ref[...] = jnp.concatenate(ranks, axis=0).astype(I32)
    g_rows = jnp.concatenate(gates + [jnp.zeros((LANES - TOP_K, ts), F32)], axis=0)
    gate_ref[...] = g_rows.T


def _post(x2d, a2d, g0, gyb, w_attn_up, w_out, norm_x_g, qk_fold, vo_fold,
          norm_ffn_g, w_router_t, b_router, batch, ts):
    n, d = x2d.shape
    nt = n // batch // ts
    tok = pl.BlockSpec((ts, d), lambda bi, j: (bi * nt + j, 0))
    lanes = pl.BlockSpec((TOP_K, ts), lambda bi, j: (0, bi * nt + j))
    const = lambda *shape: pl.BlockSpec(shape, lambda bi, j: (0,) * len(shape),
                                        pipeline_mode=pl.Buffered(1))
    fold = lambda arr: pl.BlockSpec((1,) + arr.shape[1:], lambda bi, j: (bi, 0, 0))
    return pl.pallas_call(
        _post_kernel,
        grid=(batch, nt),
        in_specs=[tok, tok, tok, tok, const(d, d), const(d, d), const(1, d),
                  fold(qk_fold), fold(vo_fold), const(1, d), const(N_EXPERTS, d),
                  const(N_EXPERTS, 1)],
        out_specs=[tok, pl.BlockSpec((ts, d // 2), lambda bi, j: (bi * nt + j, 0)),
                   lanes, lanes,
                   pl.BlockSpec((ts, LANES), lambda bi, j: (bi * nt + j, 0)),
                   pl.BlockSpec((N_EXPERTS, LANES), lambda bi, j: (0, 0))],
        out_shape=[jax.ShapeDtypeStruct((n, d), F32),
                   jax.ShapeDtypeStruct((n, d // 2), jnp.uint32),
                   jax.ShapeDtypeStruct((TOP_K, n), I32), jax.ShapeDtypeStruct((TOP_K, n), I32),
                   jax.ShapeDtypeStruct((n, LANES), F32),
                   jax.ShapeDtypeStruct((N_EXPERTS, LANES), I32)],
        scratch_shapes=[pltpu.VMEM((N_EXPERTS, 1), F32)],
        compiler_params=_cparams(2),
        name="post",
    )(x2d, a2d, g0, gyb, w_attn_up, w_out, norm_x_g, qk_fold, vo_fold,
      norm_ffn_g, w_router_t, b_router)


def _route_dest_kernel(pstart_ref, eidx_ref, rank_ref, dest_ref):
    eidx = eidx_ref[...]
    start = jnp.zeros(eidx.shape, I32)
    for e in range(N_EXPERTS):
        start = jnp.where(eidx == e, pstart_ref[e], start)
    dest_ref[...] = start + rank_ref[...]


def _route_dest(pstart, eidx, rank, tl):
    k, n = eidx.shape
    lanes = pl.BlockSpec((k, tl), lambda i: (0, i))
    return pl.pallas_call(
        _route_dest_kernel,
        grid=(n // tl,),
        in_specs=[pl.BlockSpec(memory_space=pltpu.SMEM), lanes, lanes],
        out_specs=lanes,
        out_shape=jax.ShapeDtypeStruct((k, n), I32),
        compiler_params=_cparams(1),
        name="route_dest",
    )(pstart, eidx, rank)


SC_CORES = 2
SC_SUBCORES = 16
SC_WORKERS = SC_CORES * SC_SUBCORES
MOVE_CHUNK = 64
MOVE_BUFFERS = 3


def _worker_index_layout(dest, n):
    per_w = n // SC_WORKERS
    n_chunks = per_w // MOVE_CHUNK
    d4 = dest.reshape(TOP_K, SC_WORKERS, n_chunks, MOVE_CHUNK)
    return jnp.transpose(d4, (1, 2, 0, 3)).reshape(SC_WORKERS, n_chunks * TOP_K, MOVE_CHUNK)


def _sc_mesh():
    return plsc.VectorSubcoreMesh(core_axis_name="c", subcore_axis_name="s",
                                  num_cores=SC_CORES, num_subcores=SC_SUBCORES)


def _sc_dispatch(hf, dest_w, rows):
    n, d = hf.shape
    per_w = n // SC_WORKERS
    n_chunks = per_w // MOVE_CHUNK

    def body(hf_hbm, dest_hbm, xs_hbm, idx_v, rows_v, rsem, wsem):
        wid = lax.axis_index("s") * SC_CORES + lax.axis_index("c")
        base = wid * per_w
        pltpu.sync_copy(dest_hbm.at[wid], idx_v)

        def read(c):
            b = c % MOVE_BUFFERS
            return pltpu.async_copy(hf_hbm.at[pl.ds(base + c * MOVE_CHUNK, MOVE_CHUNK)],
                                    rows_v.at[b], rsem.at[b])

        scatters = [[] for _ in range(MOVE_BUFFERS)]
        pending = read(0)
        for c in range(n_chunks):
            b = c % MOVE_BUFFERS
            pending.wait()
            if c + 1 < n_chunks:
                for cp in scatters[(c + 1) % MOVE_BUFFERS]:
                    cp.wait()
                pending = read(c + 1)
            scatters[b] = [
                pltpu.async_copy(rows_v.at[b], xs_hbm.at[idx_v.at[c * TOP_K + kk]], wsem.at[b])
                for kk in range(TOP_K)]
        for group in scatters:
            for cp in group:
                cp.wait()

    return pl.kernel(
        body,
        out_type=jax.ShapeDtypeStruct((rows, d), hf.dtype),
        mesh=_sc_mesh(),
        scratch_types=[pltpu.VMEM((n_chunks * TOP_K, MOVE_CHUNK), I32),
                       pltpu.VMEM((MOVE_BUFFERS, MOVE_CHUNK, d), hf.dtype),
                       pltpu.SemaphoreType.DMA((MOVE_BUFFERS,)),
                       pltpu.SemaphoreType.DMA((MOVE_BUFFERS,))],
        name="sc_dispatch",
    )(hf, dest_w)


def _sc_gather(ys, dest_w, n):
    _, d = ys.shape
    per_w = n // SC_WORKERS
    n_chunks = per_w // MOVE_CHUNK

    def body(ys_hbm, dest_hbm, yg_hbm, idx_v, rows_v, rsem, wsem):
        wid = lax.axis_index("s") * SC_CORES + lax.axis_index("c")
        base = wid * per_w
        pltpu.sync_copy(dest_hbm.at[wid], idx_v)
        n_moves = n_chunks * TOP_K

        def gather(m):
            b = m % MOVE_BUFFERS
            return pltpu.async_copy(ys_hbm.at[idx_v.at[m]], rows_v.at[b], rsem.at[b])

        def write(m):
            b = m % MOVE_BUFFERS
            c, kk = divmod(m, TOP_K)
            return pltpu.async_copy(
                rows_v.at[b], yg_hbm.at[pl.ds(kk * n + base + c * MOVE_CHUNK, MOVE_CHUNK)],
                wsem.at[b])

        writes = [None] * MOVE_BUFFERS
        pending = gather(0)
        for m in range(n_moves):
            pending.wait()
            if m + 1 < n_moves:
                nb = (m + 1) % MOVE_BUFFERS
                if writes[nb] is not None:
                    writes[nb].wait()
                pending = gather(m + 1)
            writes[m % MOVE_BUFFERS] = write(m)
        for wr in writes:
            if wr is not None:
                wr.wait()

    return pl.kernel(
        body,
        out_type=jax.ShapeDtypeStruct((TOP_K * n, d), ys.dtype),
        mesh=_sc_mesh(),
        scratch_types=[pltpu.VMEM((n_chunks * TOP_K, MOVE_CHUNK), I32),
                       pltpu.VMEM((MOVE_BUFFERS, MOVE_CHUNK, d), ys.dtype),
                       pltpu.SemaphoreType.DMA((MOVE_BUFFERS,)),
                       pltpu.SemaphoreType.DMA((MOVE_BUFFERS,))],
        name="sc_gather",
    )(ys, dest_w)


def _combine_kernel(x2_ref, gate_ref, fg_ref, yg_ref, o_ref):
    g = gate_ref[...]
    x2 = x2_ref[...]
    c = x2.shape[1] // 2
    acc_hi, acc_lo = x2[:, :c], x2[:, c:]
    for kk in range(TOP_K):
        hi, lo = _unpack_halves(yg_ref[kk])
        acc_hi = acc_hi + g[:, kk:kk + 1] * hi
        acc_lo = acc_lo + g[:, kk:kk + 1] * lo
    o_ref[...] = _rms(jnp.concatenate([acc_hi, acc_lo], axis=1), fg_ref[...])


def _combine(x2, gate_tm, final_g, yg, ts):
    n, d = x2.shape
    dw = yg.shape[1]
    return pl.pallas_call(
        _combine_kernel,
        grid=(n // ts,),
        in_specs=[pl.BlockSpec((ts, d), lambda i: (i, 0)),
                  pl.BlockSpec((ts, LANES), lambda i: (i, 0)),
                  pl.BlockSpec((1, d), lambda i: (0, 0)),
                  pl.BlockSpec((TOP_K, ts, dw), lambda i: (0, i, 0))],
        out_specs=pl.BlockSpec((ts, d), lambda i: (i, 0)),
        out_shape=jax.ShapeDtypeStruct((n, d), F32),
        compiler_params=_cparams(1),
        name="combine",
    )(x2, gate_tm, final_g, yg.reshape(TOP_K, n, dw))


def _experts_kernel(be_ref, first_ref, parts_ref, nxt_ref, slot_ref, nb_ref,
                    xs_ref, bgu_ref, bd_ref, wgu_hbm, wd_hbm, ys_ref,
                    wgu_st, wd_st, wgu_bf, wd_bf, sem_w):
    dff = wd_bf.shape[0]
    bm = xs_ref.shape[0] // EXPERT_BLOCKS_PER_STEP

    def fetch(ex, sl):
        return (pltpu.make_async_copy(wgu_hbm.at[ex], wgu_st.at[sl], sem_w.at[0, sl]),
                pltpu.make_async_copy(wd_hbm.at[ex], wd_st.at[sl], sem_w.at[1, sl]))

    @pl.when(pl.program_id(0) == 0)
    def _():
        e0 = be_ref[0]
        for cp in fetch(e0, slot_ref[e0]):
            cp.start()

    for sub in range(EXPERT_BLOCKS_PER_STEP):
        blk = pl.program_id(0) * EXPERT_BLOCKS_PER_STEP + sub
        e = be_ref[blk]
        rows = slice(sub * bm, (sub + 1) * bm)

        @pl.when(first_ref[blk] == 1)
        def _(e=e):
            sl = slot_ref[e]
            for cp in fetch(e, sl):
                cp.wait()
            nx = nxt_ref[e]

            @pl.when(nx >= 0)
            def _():
                for cp in fetch(nx, 1 - sl):
                    cp.start(priority=1)

            wgu_bf[...] = wgu_st[sl].astype(BF16)
            wd_bf[...] = wd_st[sl].astype(BF16)

        def mlp(e, rows):
            xb = jnp.concatenate(_unpack_halves(xs_ref[rows, :]), axis=1).astype(BF16)
            gu = _dot(xb, wgu_bf[...]) + bgu_ref[e]
            x_glu = jnp.minimum(gu[:, :dff], SWIGLU_LIMIT)
            x_lin = jnp.clip(gu[:, dff:], -SWIGLU_LIMIT, SWIGLU_LIMIT)
            act = x_glu * jax.nn.sigmoid(SWIGLU_ALPHA * x_glu) * (x_lin + 1.0)
            ys_ref[rows, :] = _pack_halves(_dot(act.astype(BF16), wd_bf[...]) + bd_ref[e])

        def clear(rows):
            ys_ref[rows, :] = jnp.zeros((rows.stop - rows.start, ys_ref.shape[1]), jnp.uint32)

        part = bm // EXPERT_TAIL_PARTS
        for live in range(EXPERT_TAIL_PARTS + 1):
            @pl.when(parts_ref[blk] == live)
            def _(e=e, rows=rows, live=live):
                cut = rows.start + live * part
                if live:
                    mlp(e, slice(rows.start, cut))
                if live < EXPERT_TAIL_PARTS:
                    clear(slice(cut, rows.stop))


def _expert_schedule(counts, bm, n_blocks):
    ne = counts.shape[0]
    ids = jnp.arange(ne, dtype=I32)
    upto = ids[None, :] <= ids[:, None]
    padded = (counts + bm - 1) // bm * bm
    pend = jnp.sum(jnp.where(upto, padded[None, :], 0), axis=1)
    pstart = (pend - padded).astype(I32)
    nb_used = (jnp.sum(padded) // bm).astype(I32)
    blk = jnp.arange(n_blocks, dtype=I32)
    blk_c = jnp.minimum(blk, nb_used - 1)
    be = jnp.minimum(jnp.sum((pend[None, :] <= (blk_c * bm)[:, None]).astype(I32), axis=1),
                     ne - 1)
    started = jnp.sum(((pstart[None, :] == (blk * bm)[:, None]) & (padded[None, :] > 0))
                      .astype(I32), axis=1)
    first = ((blk < nb_used) & (started > 0)).astype(I32)
    used = counts > 0
    seq = jnp.sum(jnp.where(upto & used[None, :], 1, 0), axis=1) - 1
    later = used[None, :] & (ids[None, :] > ids[:, None])
    nxt = jnp.where(jnp.any(later, axis=1), jnp.argmax(later, axis=1), -1).astype(I32)
    slot = (seq & 1).astype(I32)
    hit = ids[None, :] == be[:, None]
    left = jnp.sum(jnp.where(hit, (pstart + counts)[None, :], 0), axis=1) - blk * bm
    part = bm // EXPERT_TAIL_PARTS
    parts = jnp.where(blk < nb_used,
                      jnp.minimum((left + part - 1) // part, EXPERT_TAIL_PARTS), 0).astype(I32)
    return pstart, nb_used.reshape(1), be, first, parts, nxt, slot


def _experts(be, first, parts, nxt, slot, nb_used, xs, w_gu, b_gu, w_down, b_down, bm):
    rows, dw = xs.shape
    ne, d, dff2 = w_gu.shape
    dff = dff2 // 2
    sb = EXPERT_BLOCKS_PER_STEP * bm
    assert rows % sb == 0
    last = lambda j, be, fi, ha, nx, sl, nb: (
        jnp.maximum(jnp.minimum(j, (nb[0] - 1) // EXPERT_BLOCKS_PER_STEP), 0), 0)
    const = lambda *shape: pl.BlockSpec(shape,
                                        lambda j, be, fi, ha, nx, sl, nb: (0,) * len(shape))
    grid_spec = pltpu.PrefetchScalarGridSpec(
        num_scalar_prefetch=6,
        grid=(rows // sb,),
        in_specs=[
            pl.BlockSpec((sb, dw), last),
            const(ne, 1, dff2),
            const(ne, 1, d),
            pl.BlockSpec(memory_space=pl.ANY),
            pl.BlockSpec(memory_space=pl.ANY),
        ],
        out_specs=pl.BlockSpec((sb, dw), lambda j, be, fi, ha, nx, sl, nb: (j, 0)),
        scratch_shapes=[pltpu.VMEM((2, d, dff2), F32), pltpu.VMEM((2, dff, d), F32),
                        pltpu.VMEM((d, dff2), BF16), pltpu.VMEM((dff, d), BF16),
                        pltpu.SemaphoreType.DMA((2, 2))],
    )
    return pl.pallas_call(
        _experts_kernel,
        grid_spec=grid_spec,
        out_shape=jax.ShapeDtypeStruct((rows, dw), jnp.uint32),
        compiler_params=_cparams(1),
        name="experts",
    )(be, first, parts, nxt, slot, nb_used, xs, b_gu.reshape(ne, 1, dff2),
      b_down.reshape(ne, 1, d), w_gu, w_down)


MIXER_TILE = 512
ATTN_HEADS_PER_STEP = 2
POST_TILE = 512
DEST_TILE = 2048
COMBINE_TILE = 1024
EXPERT_BLOCK = 1024
EXPERT_TAIL_PARTS = 4
EXPERT_BLOCKS_PER_STEP = 1


def kernel(x, mem, norm_mix_g, w_in, lambda_q1, lambda_k1, lambda_q2, lambda_k2, rel_bias,
           subln_g, w_attn_up, pool_mix, pool_scale, w_pool_up, w_gate, b_gate, w_out,
           norm_x_g, norm_mem_g, w_xq, w_xkv, w_xo, norm_ffn_g, w_router, b_router,
           w_gu, b_gu, w_down, b_down, final_norm_g):
    b, s, d = x.shape
    n = b * s
    assert w_in.shape[0] == 1, "single-layer block"
    row = lambda a: a.reshape(1, -1)
    bf = lambda a: a[0].astype(BF16)

    q, k, v, g0, gyb = _mixer_in(
        x, row(norm_mix_g[0]), bf(w_in), bf(w_gate), row(b_gate[0]), bf(pool_mix),
        row(pool_scale[0]), bf(w_pool_up), MIXER_TILE)

    bias, lam = _rel_bias(rel_bias, row(lambda_q1[0]), row(lambda_k1[0]),
                          row(lambda_q2[0]), row(lambda_k2[0]), s)
    a = _diff_attn(q, k, v, bias, lam, row(subln_g[0]), ATTN_HEADS_PER_STEP)

    qk_fold, vo_fold = _mem_fold(mem, row(norm_mem_g[0]), bf(w_xkv), bf(w_xq), bf(w_xo))

    x2, hf, eidx, rank, gate_tm, counts = _post(
        x.reshape(n, d), a.reshape(n, d), g0, gyb, bf(w_attn_up), bf(w_out),
        row(norm_x_g[0]), qk_fold, vo_fold, row(norm_ffn_g[0]),
        w_router[0].T, b_router[0].reshape(-1, 1), b, POST_TILE)

    bm = EXPERT_BLOCK
    rows = n * TOP_K + N_EXPERTS * bm
    pstart, nb_used, be, first, parts, nxt, slot = _expert_schedule(
        counts[:, 0], bm, rows // bm)

    dest_w = _worker_index_layout(_route_dest(pstart, eidx, rank, DEST_TILE), n)
    xs = _sc_dispatch(hf, dest_w, rows)
    ys = _experts(be, first, parts, nxt, slot, nb_used, xs, w_gu[0], b_gu[0], w_down[0],
                  b_down[0], bm)
    yg = _sc_gather(ys, dest_w, n)
    out = _combine(x2, gate_tm, row(final_norm_g), yg, COMBINE_TILE)
    return out.reshape(b, s, d)
```

```python
import functools
import math

import numpy as np
import jax
import jax.numpy as jnp
from jax import lax
from jax.experimental import pallas as pl
from jax.experimental.pallas import tpu as pltpu
from jax.experimental.pallas import tpu_sc as plsc

F32 = jnp.float32
BF16 = jnp.bfloat16
I32 = jnp.int32

EPS = 1e-6
CHUNK = 64
N_HEADS = 8
HEAD_DIM = 64
HEAD_W = 2 * HEAD_DIM
POOL_WINDOWS = (2, 4, 8, 16)
POOL_GROUP = 128
POOL_PAD = 16
N_BUCKETS = 32
MAX_DISTANCE = 128
X_HEADS = 4
N_EXPERTS = 32
TOP_K = 4
SWIGLU_ALPHA = 1.702
SWIGLU_LIMIT = 7.0
LAMBDA_INIT = 0.8 - 0.6 * math.exp(-0.3 * 0)

LANES = 128
SUBLANES = 8
ATTN_BLOCK = 256
VMEM_LIMIT = 56 * 1024 * 1024


def _cparams(n_axes, vmem=VMEM_LIMIT):
    return pltpu.CompilerParams(
        dimension_semantics=("arbitrary",) * n_axes, vmem_limit_bytes=vmem)


def _rms(xf, g):
    ms = jnp.mean(xf * xf, axis=-1, keepdims=True)
    return xf * lax.rsqrt(ms + EPS) * g


def _dot(a, b):
    return jnp.dot(a, b, preferred_element_type=F32)


def _pack_halves(x):
    c = x.shape[1] // 2
    as_bits = lambda v: lax.bitcast_convert_type(v.astype(BF16).astype(F32), jnp.uint32)
    return as_bits(x[:, :c]) | (as_bits(x[:, c:]) >> 16)


def _unpack_halves(w):
    hi = lax.bitcast_convert_type(w & jnp.uint32(0xFFFF0000), F32)
    lo = lax.bitcast_convert_type(w << 16, F32)
    return hi, lo


def _dot_nt(a, b, precision=None):
    return lax.dot_general(a, b, (((1,), (1,)), ((), ())),
                           preferred_element_type=F32, precision=precision)


def _mixer_in_kernel(x_ref, g_ref, wq_ref, wk_ref, wv_ref, wu_ref, wg_ref, bg_ref,
                     pmix_ref, pscale_ref, wpu_ref,
                     q_ref, k_ref, v_ref, g0_ref, gyb_ref, ext_ref):
    ts = x_ref.shape[1]
    d = x_ref.shape[2]
    j = pl.program_id(1)
    h = _rms(x_ref[0], g_ref[...]).astype(BF16)
    q_ref[0] = (_dot(h, wq_ref[...]) * (HEAD_DIM ** -0.5)).astype(BF16)
    k_ref[0] = _dot(h, wk_ref[...]).astype(BF16)
    v_ref[0] = _dot(h, wv_ref[...]).astype(BF16)
    u = _dot(h, wu_ref[...])

    @pl.when(j == 0)
    def _():
        ext_ref[0:POOL_PAD, :] = jnp.zeros((POOL_PAD, u.shape[1]), F32)

    ext_ref[POOL_PAD:POOL_PAD + ts, :] = u
    e = ext_ref[...]
    sums = {}
    s = e
    w = 1
    while w < max(POOL_WINDOWS):
        s = s + pltpu.roll(s, w, 0)
        w *= 2
        sums[w] = s
    ext_ref[0:POOL_PAD, :] = ext_ref[ts:ts + POOL_PAD, :]

    pos = (j * ts + lax.broadcasted_iota(I32, (ts, 1), 0) + 1).astype(F32)
    mixed = []
    for gi, w in enumerate(POOL_WINDOWS):
        sl = slice(gi * POOL_GROUP, (gi + 1) * POOL_GROUP)
        win = sums[w][POOL_PAD:, sl]
        pooled = win / jnp.minimum(pos, float(w)) - u[:, sl]
        mixed.append(_dot(pooled.astype(BF16), pmix_ref[gi]) * pscale_ref[:, sl])
    mixed = jnp.concatenate(mixed, axis=1).astype(BF16)
    y_b = _dot(mixed, wpu_ref[...])

    gate = jax.nn.sigmoid(_dot(h, wg_ref[...]) + bg_ref[...])
    g0_ref[...] = gate[:, :d].astype(BF16)
    gyb_ref[...] = (gate[:, d:] * y_b).astype(BF16)


def _mixer_in(x, norm_g, w_in, w_gate, b_gate, pool_mix, pool_scale, w_pool_up, ts):
    b, s, d = x.shape
    aw = N_HEADS * HEAD_W
    pw = len(POOL_WINDOWS) * POOL_GROUP
    n = b * s
    nt = s // ts
    const = lambda *shape: pl.BlockSpec(shape, lambda bi, j: (0,) * len(shape))
    tok3 = pl.BlockSpec((1, ts, aw), lambda bi, j: (bi, j, 0))
    tok2 = pl.BlockSpec((ts, d), lambda bi, j: (bi * nt + j, 0))
    return pl.pallas_call(
        _mixer_in_kernel,
        grid=(b, nt),
        in_specs=[
            pl.BlockSpec((1, ts, d), lambda bi, j: (bi, j, 0)),
            const(1, d),
            pl.BlockSpec((d, aw), lambda bi, j: (0, 0)),
            pl.BlockSpec((d, aw), lambda bi, j: (0, 1)),
            pl.BlockSpec((d, aw), lambda bi, j: (0, 2)),
            pl.BlockSpec((d, pw), lambda bi, j: (0, 3 * aw // pw)),
            const(d, 2 * d),
            const(1, 2 * d),
            const(len(POOL_WINDOWS), POOL_GROUP, POOL_GROUP),
            const(1, pw),
            const(pw, d),
        ],
        out_specs=[tok3, tok3, tok3, tok2, tok2],
        out_shape=[jax.ShapeDtypeStruct((b, s, aw), BF16)] * 3
        + [jax.ShapeDtypeStruct((n, d), BF16)] * 2,
        scratch_shapes=[pltpu.VMEM((ts + POOL_PAD, pw), F32)],
        compiler_params=_cparams(2),
        name="mixer_in",
    )(x, norm_g, w_in, w_in, w_in, w_in, w_gate, b_gate, pool_mix, pool_scale, w_pool_up)


def _rel_bucket(rel, log=jnp.log, f32=lambda a: a.astype(jnp.float32),
                i32=lambda a: a.astype(jnp.int32), xp=jnp):
    nb = N_BUCKETS // 2
    ret = i32(rel > 0) * nb
    n = xp.abs(rel)
    max_exact = nb // 2
    nf = f32(xp.maximum(n, 1))
    large = max_exact + i32(log(nf / max_exact) / math.log(MAX_DISTANCE / max_exact)
                            * (nb - max_exact))
    large = xp.minimum(large, nb - 1)
    return ret + xp.where(n < max_exact, n, large)


def _far_bucket(block, seq):
    rel = -np.arange(block + 1, max(seq, block + 2), dtype=np.int32)
    bk = _rel_bucket(rel, log=np.log, f32=lambda a: a.astype(np.float32),
                     i32=lambda a: a.astype(np.int32), xp=np)
    assert (bk == bk[0]).all(), "far keys must share one relative-position bucket"
    return int(bk[0])


def _rel_bias_kernel(far_bucket, tab_ref, bidx_ref, lq1_ref, lk1_ref, lq2_ref, lk2_ref,
                     bias_ref, lam_ref):
    h = pl.program_id(0)
    bidx = bidx_ref[...]
    acc = jnp.zeros(bidx.shape, F32)
    for bkt in range(N_BUCKETS):
        acc = jnp.where(bidx == bkt, tab_ref[bkt, h], acc)
    acc = acc - tab_ref[far_bucket, h]
    bias_ref[0] = jnp.where(bidx < 0, -jnp.inf, acc)
    lam = (jnp.exp(jnp.sum(lq1_ref[...] * lk1_ref[...], keepdims=True))
           - jnp.exp(jnp.sum(lq2_ref[...] * lk2_ref[...], keepdims=True)) + LAMBDA_INIT)
    lam_ref[...] = jnp.broadcast_to(lam, lam_ref.shape)


def _rel_bias(rel_bias, lq1, lk1, lq2, lk2, seq):
    blk = ATTN_BLOCK
    qpos = jnp.arange(blk, dtype=I32)[:, None]
    kpos = jnp.arange(blk, dtype=I32)[None, :]
    diag = jnp.where(kpos // CHUNK <= qpos // CHUNK, _rel_bucket(kpos - qpos), -1)
    prev = _rel_bucket(kpos - (qpos + blk))
    bidx = jnp.stack([diag, prev]).astype(I32)
    vec = pl.BlockSpec((1, HEAD_DIM), lambda h: (0, 0))
    return pl.pallas_call(
        functools.partial(_rel_bias_kernel, _far_bucket(blk, seq)),
        grid=(N_HEADS,),
        in_specs=[
            pl.BlockSpec(memory_space=pltpu.SMEM),
            pl.BlockSpec((2, blk, blk), lambda h: (0, 0, 0)),
            vec, vec, vec, vec,
        ],
        out_specs=[
            pl.BlockSpec((1, 2, blk, blk), lambda h: (h, 0, 0, 0)),
            pl.BlockSpec((SUBLANES, LANES), lambda h: (0, 0)),
        ],
        out_shape=[jax.ShapeDtypeStruct((N_HEADS, 2, blk, blk), F32),
                   jax.ShapeDtypeStruct((SUBLANES, LANES), F32)],
        compiler_params=_cparams(1),
        name="rel_bias",
    )(rel_bias, bidx, lq1, lk1, lq2, lk2)


def _diff_attn_kernel(q_ref, k_ref, v_ref, bias_ref, lam_ref, sg_ref, o_ref, vext_ref):
    s_len = q_ref.shape[1]
    tq = ATTN_BLOCK
    n_heads = q_ref.shape[2] // HEAD_W
    lam = lam_ref[0:1, 0:1]
    lane = lax.broadcasted_iota(I32, (tq, HEAD_W), 1)
    cols = [slice(hh * HEAD_W, (hh + 1) * HEAD_W) for hh in range(n_heads)]
    b_diag, b_prev = [], []
    for hh in range(n_heads):
        vext_ref[hh, :, :HEAD_W] = v_ref[0, :, cols[hh]]
        vext_ref[hh, :, HEAD_W:] = jnp.ones((s_len, HEAD_W), BF16)
        b_diag.append(jnp.concatenate([bias_ref[hh, 0]] * 2, axis=0))
        b_prev.append(jnp.concatenate([bias_ref[hh, 1]] * 2, axis=0))

    for qi in range(s_len // tq):
        n_keys = (qi + 1) * tq
        rows = slice(qi * tq, (qi + 1) * tq)
        for hh in range(n_heads):
            q = q_ref[0, rows, cols[hh]]
            zero = jnp.zeros_like(q)
            qs = jnp.concatenate([jnp.where(lane < HEAD_DIM, q, zero),
                                  jnp.where(lane >= HEAD_DIM, q, zero)], axis=0)
            s = _dot_nt(qs, k_ref[0, :n_keys, cols[hh]])
            pieces = []
            if qi >= 2:
                pieces.append(s[:, :n_keys - 2 * tq])
            if qi >= 1:
                pieces.append(s[:, n_keys - 2 * tq:n_keys - tq] + b_prev[hh])
            pieces.append(s[:, n_keys - tq:] + b_diag[hh])
            s = jnp.concatenate(pieces, axis=1) if len(pieces) > 1 else pieces[0]
            m = jnp.max(s, axis=-1, keepdims=True)
            p = jnp.exp(s - m).astype(BF16)
            acc = _dot(p, vext_ref[hh, :n_keys, :])
            o = acc[:, :HEAD_W] / acc[:, HEAD_W:HEAD_W + 1]
            a = o[:tq] - lam * o[tq:]
            o_ref[0, rows, cols[hh]] = (
                _rms(a, sg_ref[...]) * (1.0 - LAMBDA_INIT)).astype(BF16)


def _diff_attn(q, k, v, bias, lam, subln_g, heads_per_step):
    b, s, aw = q.shape
    tq = ATTN_BLOCK
    hp = heads_per_step
    spec = pl.BlockSpec((1, s, hp * HEAD_W), lambda bi, h: (bi, 0, h))
    return pl.pallas_call(
        _diff_attn_kernel,
        grid=(b, N_HEADS // hp),
        in_specs=[
            spec, spec, spec,
            pl.BlockSpec((hp, 2, tq, tq), lambda bi, h: (h, 0, 0, 0)),
            pl.BlockSpec((SUBLANES, LANES), lambda bi, h: (0, 0)),
            pl.BlockSpec((1, HEAD_W), lambda bi, h: (0, 0)),
        ],
        out_specs=spec,
        out_shape=jax.ShapeDtypeStruct((b, s, aw), BF16),
        scratch_shapes=[pltpu.VMEM((hp, s, 2 * HEAD_W), BF16)],
        compiler_params=_cparams(2),
        name="diff_attn",
    )(q, k, v, bias, lam, subln_g)


def _mem_fold_kernel(m_ref, g_ref, wkv_ref, wq_ref, wo_ref, qk_ref, vo_ref):
    d = m_ref.shape[2]
    hd = d // X_HEADS
    m = _rms(m_ref[0], g_ref[...]).astype(BF16)
    kv = _dot(m, wkv_ref[...])
    k = kv[:, :d].astype(BF16)
    v = kv[:, d:].astype(BF16)
    for hh in range(X_HEADS):
        sl = slice(hh * hd, (hh + 1) * hd)
        qk_ref[0, :, hh * m.shape[0]:(hh + 1) * m.shape[0]] = (
            _dot_nt(wq_ref[:, sl], k[:, sl]) * (hd ** -0.5)).astype(BF16)
        vo_ref[0, hh * m.shape[0]:(hh + 1) * m.shape[0], :] = _dot(v[:, sl], wo_ref[sl, :]).astype(BF16)


def _mem_fold(mem, norm_g, w_xkv, w_xq, w_xo):
    b, ml, d = mem.shape
    const = lambda *shape: pl.BlockSpec(shape, lambda bi: (0,) * len(shape))
    return pl.pallas_call(
        _mem_fold_kernel,
        grid=(b,),
        in_specs=[pl.BlockSpec((1, ml, d), lambda bi: (bi, 0, 0)), const(1, d),
                  const(d, 2 * d), const(d, d), const(d, d)],
        out_specs=[pl.BlockSpec((1, d, X_HEADS * ml), lambda bi: (bi, 0, 0)),
                   pl.BlockSpec((1, X_HEADS * ml, d), lambda bi: (bi, 0, 0))],
        out_shape=[jax.ShapeDtypeStruct((b, d, X_HEADS * ml), BF16),
                   jax.ShapeDtypeStruct((b, X_HEADS * ml, d), BF16)],
        compiler_params=_cparams(1),
        name="mem_fold",
    )(mem, norm_g, w_xkv, w_xq, w_xo)


def _post_kernel(x_ref, a_ref, g0_ref, gyb_ref, wau_ref, wo_ref, nxg_ref,
                 qk_ref, vo_ref, nfg_ref, wrt_ref, br_ref,
                 x2_ref, hf_ref, eidx_ref, rank_ref, gate_ref, cnt_ref, run_ref):
    ts, d = x_ref.shape
    first = (pl.program_id(0) == 0) & (pl.program_id(1) == 0)

    y_a = _dot(a_ref[...], wau_ref[...])
    merged = g0_ref[...].astype(F32) * y_a + gyb_ref[...].astype(F32)
    x1 = x_ref[...] + _dot(merged.astype(BF16), wo_ref[...])

    ml = qk_ref.shape[2] // X_HEADS
    hx = _rms(x1, nxg_ref[...]).astype(BF16)
    s_all = _dot(hx, qk_ref[0])
    probs = []
    for hh in range(X_HEADS):
        s = s_all[:, hh * ml:(hh + 1) * ml]
        p = jnp.exp(s - jnp.max(s, axis=-1, keepdims=True))
        probs.append((p / jnp.sum(p, axis=-1, keepdims=True)).astype(BF16))
    x2 = x1 + _dot(jnp.concatenate(probs, axis=1), vo_ref[0])
    x2_ref[...] = x2
    hf = _rms(x2, nfg_ref[...])
    hf_ref[...] = _pack_halves(hf)

    logits = _dot_nt(wrt_ref[...], hf, precision=lax.Precision.HIGHEST) + br_ref[...]
    ne = logits.shape[0]
    eid = lax.broadcasted_iota(I32, logits.shape, 0).astype(F32)
    work = logits
    vals, idxs, hots = [], [], []
    for _ in range(TOP_K):
        mx = jnp.max(work, axis=0, keepdims=True)
        idx = jnp.min(jnp.where(work == mx, eid, float(ne)), axis=0, keepdims=True)
        hot = eid == idx
        vals.append(mx)
        idxs.append(idx.astype(I32))
        hots.append(hot)
        work = jnp.where(hot, -jnp.inf, work)
    ex = [jnp.exp(vv - vals[0]) for vv in vals]
    den = ex[0] + ex[1] + ex[2] + ex[3]
    gates = [e_ / den for e_ in ex]

    multi = (hots[0] | hots[1] | hots[2] | hots[3])
    multi_f = jnp.where(multi, 1.0, 0.0).astype(F32)
    tri = (lax.broadcasted_iota(I32, (ts, ts), 0)
           < lax.broadcasted_iota(I32, (ts, ts), 1))
    before = _dot(multi_f.astype(BF16), jnp.where(tri, 1.0, 0.0).astype(BF16))

    @pl.when(first)
    def _():
        run_ref[...] = jnp.zeros(run_ref.shape, F32)

    run = run_ref[...]
    pos = before + run
    ranks = [jnp.sum(jnp.where(hot, pos, 0.0), axis=0, keepdims=True) for hot in hots]
    run_new = run + jnp.sum(multi_f, axis=1, keepdims=True)
    run_ref[...] = run_new
    cnt_ref[...] = jnp.broadcast_to(run_new, cnt_ref.shape).astype(I32)

    eidx_ref[...] = jnp.concatenate(idxs, axis=0)
    rank_ref[...] = jnp.concatenate(ranks, axis=0).astype(I32)
    g_rows = jnp.concatenate(gates + [jnp.zeros((LANES - TOP_K, ts), F32)], axis=0)
    gate_ref[...] = g_rows.T


def _post(x2d, a2d, g0, gyb, w_attn_up, w_out, norm_x_g, qk_fold, vo_fold,
          norm_ffn_g, w_router_t, b_router, batch, ts):
    n, d = x2d.shape
    nt = n // batch // ts
    tok = pl.BlockSpec((ts, d), lambda bi, j: (bi * nt + j, 0))
    lanes = pl.BlockSpec((TOP_K, ts), lambda bi, j: (0, bi * nt + j))
    const = lambda *shape: pl.BlockSpec(shape, lambda bi, j: (0,) * len(shape),
                                        pipeline_mode=pl.Buffered(1))
    fold = lambda arr: pl.BlockSpec((1,) + arr.shape[1:], lambda bi, j: (bi, 0, 0))
    return pl.pallas_call(
        _post_kernel,
        grid=(batch, nt),
        in_specs=[tok, tok, tok, tok, const(d, d), const(d, d), const(1, d),
                  fold(qk_fold), fold(vo_fold), const(1, d), const(N_EXPERTS, d),
                  const(N_EXPERTS, 1)],
        out_specs=[tok, pl.BlockSpec((ts, d // 2), lambda bi, j: (bi * nt + j, 0)),
                   lanes, lanes,
                   pl.BlockSpec((ts, LANES), lambda bi, j: (bi * nt + j, 0)),
                   pl.BlockSpec((N_EXPERTS, LANES), lambda bi, j: (0, 0))],
        out_shape=[jax.ShapeDtypeStruct((n, d), F32),
                   jax.ShapeDtypeStruct((n, d // 2), jnp.uint32),
                   jax.ShapeDtypeStruct((TOP_K, n), I32), jax.ShapeDtypeStruct((TOP_K, n), I32),
                   jax.ShapeDtypeStruct((n, LANES), F32),
                   jax.ShapeDtypeStruct((N_EXPERTS, LANES), I32)],
        scratch_shapes=[pltpu.VMEM((N_EXPERTS, 1), F32)],
        compiler_params=_cparams(2),
        name="post",
    )(x2d, a2d, g0, gyb, w_attn_up, w_out, norm_x_g, qk_fold, vo_fold,
      norm_ffn_g, w_router_t, b_router)


def _route_dest_kernel(pstart_ref, eidx_ref, rank_ref, dest_ref):
    eidx = eidx_ref[...]
    start = jnp.zeros(eidx.shape, I32)
    for e in range(N_EXPERTS):
        start = jnp.where(eidx == e, pstart_ref[e], start)
    dest_ref[...] = start + rank_ref[...]


def _route_dest(pstart, eidx, rank, tl):
    k, n = eidx.shape
    lanes = pl.BlockSpec((k, tl), lambda i: (0, i))
    return pl.pallas_call(
        _route_dest_kernel,
        grid=(n // tl,),
        in_specs=[pl.BlockSpec(memory_space=pltpu.SMEM), lanes, lanes],
        out_specs=lanes,
        out_shape=jax.ShapeDtypeStruct((k, n), I32),
        compiler_params=_cparams(1),
        name="route_dest",
    )(pstart, eidx, rank)


SC_CORES = 2
SC_SUBCORES = 16
SC_WORKERS = SC_CORES * SC_SUBCORES
MOVE_CHUNK = 64
MOVE_BUFFERS = 3


def _worker_index_layout(dest, n):
    per_w = n // SC_WORKERS
    n_chunks = per_w // MOVE_CHUNK
    d4 = dest.reshape(TOP_K, SC_WORKERS, n_chunks, MOVE_CHUNK)
    return jnp.transpose(d4, (1, 2, 0, 3)).reshape(SC_WORKERS, n_chunks * TOP_K, MOVE_CHUNK)


def _sc_mesh():
    return plsc.VectorSubcoreMesh(core_axis_name="c", subcore_axis_name="s",
                                  num_cores=SC_CORES, num_subcores=SC_SUBCORES)


def _sc_dispatch(hf, dest_w, rows):
    n, d = hf.shape
    per_w = n // SC_WORKERS
    n_chunks = per_w // MOVE_CHUNK

    def body(hf_hbm, dest_hbm, xs_hbm, idx_v, rows_v, rsem, wsem):
        wid = lax.axis_index("s") * SC_CORES + lax.axis_index("c")
        base = wid * per_w
        pltpu.sync_copy(dest_hbm.at[wid], idx_v)

        def read(c):
            b = c % MOVE_BUFFERS
            return pltpu.async_copy(hf_hbm.at[pl.ds(base + c * MOVE_CHUNK, MOVE_CHUNK)],
                                    rows_v.at[b], rsem.at[b])

        scatters = [[] for _ in range(MOVE_BUFFERS)]
        pending = read(0)
        for c in range(n_chunks):
            b = c % MOVE_BUFFERS
            pending.wait()
            if c + 1 < n_chunks:
                for cp in scatters[(c + 1) % MOVE_BUFFERS]:
                    cp.wait()
                pending = read(c + 1)
            scatters[b] = [
                pltpu.async_copy(rows_v.at[b], xs_hbm.at[idx_v.at[c * TOP_K + kk]], wsem.at[b])
                for kk in range(TOP_K)]
        for group in scatters:
            for cp in group:
                cp.wait()

    return pl.kernel(
        body,
        out_type=jax.ShapeDtypeStruct((rows, d), hf.dtype),
        mesh=_sc_mesh(),
        scratch_types=[pltpu.VMEM((n_chunks * TOP_K, MOVE_CHUNK), I32),
                       pltpu.VMEM((MOVE_BUFFERS, MOVE_CHUNK, d), hf.dtype),
                       pltpu.SemaphoreType.DMA((MOVE_BUFFERS,)),
                       pltpu.SemaphoreType.DMA((MOVE_BUFFERS,))],
        name="sc_dispatch",
    )(hf, dest_w)


def _sc_gather(ys, dest_w, n):
    _, d = ys.shape
    per_w = n // SC_WORKERS
    n_chunks = per_w // MOVE_CHUNK

    def body(ys_hbm, dest_hbm, yg_hbm, idx_v, rows_v, rsem, wsem):
        wid = lax.axis_index("s") * SC_CORES + lax.axis_index("c")
        base = wid * per_w
        pltpu.sync_copy(dest_hbm.at[wid], idx_v)
        n_moves = n_chunks * TOP_K

        def gather(m):
            b = m % MOVE_BUFFERS
            return pltpu.async_copy(ys_hbm.at[idx_v.at[m]], rows_v.at[b], rsem.at[b])

        def write(m):
            b = m % MOVE_BUFFERS
            c, kk = divmod(m, TOP_K)
            return pltpu.async_copy(
                rows_v.at[b], yg_hbm.at[pl.ds(kk * n + base + c * MOVE_CHUNK, MOVE_CHUNK)],
                wsem.at[b])

        writes = [None] * MOVE_BUFFERS
        pending = gather(0)
        for m in range(n_moves):
            pending.wait()
            if m + 1 < n_moves:
                nb = (m + 1) % MOVE_BUFFERS
                if writes[nb] is not None:
                    writes[nb].wait()
                pending = gather(m + 1)
            writes[m % MOVE_BUFFERS] = write(m)
        for wr in writes:
            if wr is not None:
                wr.wait()

    return pl.kernel(
        body,
        out_type=jax.ShapeDtypeStruct((TOP_K * n, d), ys.dtype),
        mesh=_sc_mesh(),
        scratch_types=[pltpu.VMEM((n_chunks * TOP_K, MOVE_CHUNK), I32),
                       pltpu.VMEM((MOVE_BUFFERS, MOVE_CHUNK, d), ys.dtype),
                       pltpu.SemaphoreType.DMA((MOVE_BUFFERS,)),
                       pltpu.SemaphoreType.DMA((MOVE_BUFFERS,))],
        name="sc_gather",
    )(ys, dest_w)


def _combine_kernel(x2_ref, gate_ref, fg_ref, yg_ref, o_ref):
    g = gate_ref[...]
    x2 = x2_ref[...]
    c = x2.shape[1] // 2
    acc_hi, acc_lo = x2[:, :c], x2[:, c:]
    for kk in range(TOP_K):
        hi, lo = _unpack_halves(yg_ref[kk])
        acc_hi = acc_hi + g[:, kk:kk + 1] * hi
        acc_lo = acc_lo + g[:, kk:kk + 1] * lo
    o_ref[...] = _rms(jnp.concatenate([acc_hi, acc_lo], axis=1), fg_ref[...])


def _combine(x2, gate_tm, final_g, yg, ts):
    n, d = x2.shape
    dw = yg.shape[1]
    return pl.pallas_call(
        _combine_kernel,
        grid=(n // ts,),
        in_specs=[pl.BlockSpec((ts, d), lambda i: (i, 0)),
                  pl.BlockSpec((ts, LANES), lambda i: (i, 0)),
                  pl.BlockSpec((1, d), lambda i: (0, 0)),
                  pl.BlockSpec((TOP_K, ts, dw), lambda i: (0, i, 0))],
        out_specs=pl.BlockSpec((ts, d), lambda i: (i, 0)),
        out_shape=jax.ShapeDtypeStruct((n, d), F32),
        compiler_params=_cparams(1),
        name="combine",
    )(x2, gate_tm, final_g, yg.reshape(TOP_K, n, dw))


def _experts_kernel(be_ref, first_ref, parts_ref, nxt_ref, slot_ref, nb_ref,
                    xs_ref, bgu_ref, bd_ref, wgu_hbm, wd_hbm, ys_ref,
                    wgu_st, wd_st, wgu_bf, wd_bf, sem_w):
    dff = wd_bf.shape[0]
    bm = xs_ref.shape[0] // EXPERT_BLOCKS_PER_STEP

    def fetch(ex, sl):
        return (pltpu.make_async_copy(wgu_hbm.at[ex], wgu_st.at[sl], sem_w.at[0, sl]),
                pltpu.make_async_copy(wd_hbm.at[ex], wd_st.at[sl], sem_w.at[1, sl]))

    @pl.when(pl.program_id(0) == 0)
    def _():
        e0 = be_ref[0]
        for cp in fetch(e0, slot_ref[e0]):
            cp.start()

    for sub in range(EXPERT_BLOCKS_PER_STEP):
        blk = pl.program_id(0) * EXPERT_BLOCKS_PER_STEP + sub
        e = be_ref[blk]
        rows = slice(sub * bm, (sub + 1) * bm)

        @pl.when(first_ref[blk] == 1)
        def _(e=e):
            sl = slot_ref[e]
            for cp in fetch(e, sl):
                cp.wait()
            nx = nxt_ref[e]

            @pl.when(nx >= 0)
            def _():
                for cp in fetch(nx, 1 - sl):
                    cp.start(priority=1)

            wgu_bf[...] = wgu_st[sl].astype(BF16)
            wd_bf[...] = wd_st[sl].astype(BF16)

        def mlp(e, rows):
            xb = jnp.concatenate(_unpack_halves(xs_ref[rows, :]), axis=1).astype(BF16)
            gu = _dot(xb, wgu_bf[...]) + bgu_ref[e]
            x_glu = jnp.minimum(gu[:, :dff], SWIGLU_LIMIT)
            x_lin = jnp.clip(gu[:, dff:], -SWIGLU_LIMIT, SWIGLU_LIMIT)
            act = x_glu * jax.nn.sigmoid(SWIGLU_ALPHA * x_glu) * (x_lin + 1.0)
            ys_ref[rows, :] = _pack_halves(_dot(act.astype(BF16), wd_bf[...]) + bd_ref[e])

        def clear(rows):
            ys_ref[rows, :] = jnp.zeros((rows.stop - rows.start, ys_ref.shape[1]), jnp.uint32)

        part = bm // EXPERT_TAIL_PARTS
        for live in range(EXPERT_TAIL_PARTS + 1):
            @pl.when(parts_ref[blk] == live)
            def _(e=e, rows=rows, live=live):
                cut = rows.start + live * part
                if live:
                    mlp(e, slice(rows.start, cut))
                if live < EXPERT_TAIL_PARTS:
                    clear(slice(cut, rows.stop))


def _expert_schedule(counts, bm, n_blocks):
    ne = counts.shape[0]
    ids = jnp.arange(ne, dtype=I32)
    upto = ids[None, :] <= ids[:, None]
    padded = (counts + bm - 1) // bm * bm
    pend = jnp.sum(jnp.where(upto, padded[None, :], 0), axis=1)
    pstart = (pend - padded).astype(I32)
    nb_used = (jnp.sum(padded) // bm).astype(I32)
    blk = jnp.arange(n_blocks, dtype=I32)
    blk_c = jnp.minimum(blk, nb_used - 1)
    be = jnp.minimum(jnp.sum((pend[None, :] <= (blk_c * bm)[:, None]).astype(I32), axis=1),
                     ne - 1)
    started = jnp.sum(((pstart[None, :] == (blk * bm)[:, None]) & (padded[None, :] > 0))
                      .astype(I32), axis=1)
    first = ((blk < nb_used) & (started > 0)).astype(I32)
    used = counts > 0
    seq = jnp.sum(jnp.where(upto & used[None, :], 1, 0), axis=1) - 1
    later = used[None, :] & (ids[None, :] > ids[:, None])
    nxt = jnp.where(jnp.any(later, axis=1), jnp.argmax(later, axis=1), -1).astype(I32)
    slot = (seq & 1).astype(I32)
    hit = ids[None, :] == be[:, None]
    left = jnp.sum(jnp.where(hit, (pstart + counts)[None, :], 0), axis=1) - blk * bm
    part = bm // EXPERT_TAIL_PARTS
    parts = jnp.where(blk < nb_used,
                      jnp.minimum((left + part - 1) // part, EXPERT_TAIL_PARTS), 0).astype(I32)
    return pstart, nb_used.reshape(1), be, first, parts, nxt, slot


def _experts(be, first, parts, nxt, slot, nb_used, xs, w_gu, b_gu, w_down, b_down, bm):
    rows, dw = xs.shape
    ne, d, dff2 = w_gu.shape
    dff = dff2 // 2
    sb = EXPERT_BLOCKS_PER_STEP * bm
    assert rows % sb == 0
    last = lambda j, be, fi, ha, nx, sl, nb: (
        jnp.maximum(jnp.minimum(j, (nb[0] - 1) // EXPERT_BLOCKS_PER_STEP), 0), 0)
    const = lambda *shape: pl.BlockSpec(shape,
                                        lambda j, be, fi, ha, nx, sl, nb: (0,) * len(shape))
    grid_spec = pltpu.PrefetchScalarGridSpec(
        num_scalar_prefetch=6,
        grid=(rows // sb,),
        in_specs=[
            pl.BlockSpec((sb, dw), last),
            const(ne, 1, dff2),
            const(ne, 1, d),
            pl.BlockSpec(memory_space=pl.ANY),
            pl.BlockSpec(memory_space=pl.ANY),
        ],
        out_specs=pl.BlockSpec((sb, dw), lambda j, be, fi, ha, nx, sl, nb: (j, 0)),
        scratch_shapes=[pltpu.VMEM((2, d, dff2), F32), pltpu.VMEM((2, dff, d), F32),
                        pltpu.VMEM((d, dff2), BF16), pltpu.VMEM((dff, d), BF16),
                        pltpu.SemaphoreType.DMA((2, 2))],
    )
    return pl.pallas_call(
        _experts_kernel,
        grid_spec=grid_spec,
        out_shape=jax.ShapeDtypeStruct((rows, dw), jnp.uint32),
        compiler_params=_cparams(1),
        name="experts",
    )(be, first, parts, nxt, slot, nb_used, xs, b_gu.reshape(ne, 1, dff2),
      b_down.reshape(ne, 1, d), w_gu, w_down)


MIXER_TILE = 512
ATTN_HEADS_PER_STEP = 4
POST_TILE = 512
DEST_TILE = 2048
COMBINE_TILE = 1024
EXPERT_BLOCK = 1024
EXPERT_TAIL_PARTS = 4
EXPERT_BLOCKS_PER_STEP = 1


def kernel(x, mem, norm_mix_g, w_in, lambda_q1, lambda_k1, lambda_q2, lambda_k2, rel_bias,
           subln_g, w_attn_up, pool_mix, pool_scale, w_pool_up, w_gate, b_gate, w_out,
           norm_x_g, norm_mem_g, w_xq, w_xkv, w_xo, norm_ffn_g, w_router, b_router,
           w_gu, b_gu, w_down, b_down, final_norm_g):
    b, s, d = x.shape
    n = b * s
    assert w_in.shape[0] == 1, "single-layer block"
    row = lambda a: a.reshape(1, -1)
    bf = lambda a: a[0].astype(BF16)

    q, k, v, g0, gyb = _mixer_in(
        x, row(norm_mix_g[0]), bf(w_in), bf(w_gate), row(b_gate[0]), bf(pool_mix),
        row(pool_scale[0]), bf(w_pool_up), MIXER_TILE)

    bias, lam = _rel_bias(rel_bias, row(lambda_q1[0]), row(lambda_k1[0]),
                          row(lambda_q2[0]), row(lambda_k2[0]), s)
    a = _diff_attn(q, k, v, bias, lam, row(subln_g[0]), ATTN_HEADS_PER_STEP)

    qk_fold, vo_fold = _mem_fold(mem, row(norm_mem_g[0]), bf(w_xkv), bf(w_xq), bf(w_xo))

    x2, hf, eidx, rank, gate_tm, counts = _post(
        x.reshape(n, d), a.reshape(n, d), g0, gyb, bf(w_attn_up), bf(w_out),
        row(norm_x_g[0]), qk_fold, vo_fold, row(norm_ffn_g[0]),
        w_router[0].T, b_router[0].reshape(-1, 1), b, POST_TILE)

    bm = EXPERT_BLOCK
    rows = n * TOP_K + N_EXPERTS * bm
    pstart, nb_used, be, first, parts, nxt, slot = _expert_schedule(
        counts[:, 0], bm, rows // bm)

    dest_w = _worker_index_layout(_route_dest(pstart, eidx, rank, DEST_TILE), n)
    xs = _sc_dispatch(hf, dest_w, rows)
    ys = _experts(be, first, parts, nxt, slot, nb_used, xs, w_gu[0], b_gu[0], w_down[0],
                  b_down[0], bm)
    yg = _sc_gather(ys, dest_w, n)
    out = _combine(x2, gate_tm, row(final_norm_g), yg, COMBINE_TILE)
    return out.reshape(b, s, d)
```

```python
import functools
import math

import numpy as np
import jax
import jax.numpy as jnp
from jax import lax
from jax.experimental import pallas as pl
from jax.experimental.pallas import tpu as pltpu
from jax.experimental.pallas import tpu_sc as plsc

F32 = jnp.float32
BF16 = jnp.bfloat16
I32 = jnp.int32

EPS = 1e-6
CHUNK = 64
N_HEADS = 8
HEAD_DIM = 64
HEAD_W = 2 * HEAD_DIM
POOL_WINDOWS = (2, 4, 8, 16)
POOL_GROUP = 128
POOL_PAD = 16
N_BUCKETS = 32
MAX_DISTANCE = 128
X_HEADS = 4
N_EXPERTS = 32
TOP_K = 4
SWIGLU_ALPHA = 1.702
SWIGLU_LIMIT = 7.0
LAMBDA_INIT = 0.8 - 0.6 * math.exp(-0.3 * 0)

LANES = 128
SUBLANES = 8
ATTN_BLOCK = 256
VMEM_LIMIT = 56 * 1024 * 1024


def _cparams(n_axes, vmem=VMEM_LIMIT):
    return pltpu.CompilerParams(
        dimension_semantics=("arbitrary",) * n_axes, vmem_limit_bytes=vmem)


def _rms(xf, g):
    ms = jnp.mean(xf * xf, axis=-1, keepdims=True)
    return xf * lax.rsqrt(ms + EPS) * g


def _dot(a, b):
    return jnp.dot(a, b, preferred_element_type=F32)


def _pack_halves(x):
    c = x.shape[1] // 2
    as_bits = lambda v: lax.bitcast_convert_type(v.astype(BF16).astype(F32), jnp.uint32)
    return as_bits(x[:, :c]) | (as_bits(x[:, c:]) >> 16)


def _unpack_halves(w):
    hi = lax.bitcast_convert_type(w & jnp.uint32(0xFFFF0000), F32)
    lo = lax.bitcast_convert_type(w << 16, F32)
    return hi, lo


def _dot_nt(a, b, precision=None):
    return lax.dot_general(a, b, (((1,), (1,)), ((), ())),
                           preferred_element_type=F32, precision=precision)


def _mixer_in_kernel(x_ref, g_ref, win_ref, wg_ref, bg_ref, wpool_ref,
                     q_ref, k_ref, v_ref, g0_ref, gyb_ref, ext_ref):
    ts = x_ref.shape[1]
    d = x_ref.shape[2]
    j = pl.program_id(1)
    h = _rms(x_ref[0], g_ref[...]).astype(BF16)
    proj = _dot(h, win_ref[...])
    aw = q_ref.shape[2]
    q_ref[0] = (proj[:, :aw] * (HEAD_DIM ** -0.5)).astype(BF16)
    k_ref[0] = proj[:, aw:2 * aw].astype(BF16)
    v_ref[0] = proj[:, 2 * aw:3 * aw].astype(BF16)
    u = proj[:, 3 * aw:]

    @pl.when(j == 0)
    def _():
        ext_ref[0:POOL_PAD, :] = jnp.zeros((POOL_PAD, u.shape[1]), F32)

    ext_ref[POOL_PAD:POOL_PAD + ts, :] = u
    e = ext_ref[...]
    sums = {}
    s = e
    w = 1
    while w < max(POOL_WINDOWS):
        s = s + pltpu.roll(s, w, 0)
        w *= 2
        sums[w] = s
    ext_ref[0:POOL_PAD, :] = ext_ref[ts:ts + POOL_PAD, :]

    pos = (j * ts + lax.broadcasted_iota(I32, (ts, 1), 0) + 1).astype(F32)
    pooled = []
    for gi, w in enumerate(POOL_WINDOWS):
        sl = slice(gi * POOL_GROUP, (gi + 1) * POOL_GROUP)
        win = sums[w][POOL_PAD:, sl]
        pooled.append(win / jnp.minimum(pos, float(w)) - u[:, sl])
    y_b = _dot(jnp.concatenate(pooled, axis=1).astype(BF16), wpool_ref[...])

    gate = jax.nn.sigmoid(_dot(h, wg_ref[...]) + bg_ref[...])
    g0_ref[...] = gate[:, :d].astype(BF16)
    gyb_ref[...] = (gate[:, d:] * y_b).astype(BF16)


def _pool_fold_kernel(pmix_ref, pscale_ref, wpu_ref, o_ref):
    mixed = (pmix_ref[0] * pscale_ref[...]).astype(BF16)
    o_ref[...] = _dot(mixed, wpu_ref[...]).astype(BF16)


def _pool_fold(pool_mix, pool_scale, w_pool_up):
    ng, grp, _ = pool_mix.shape
    pw, d = w_pool_up.shape
    return pl.pallas_call(
        _pool_fold_kernel,
        grid=(ng,),
        in_specs=[pl.BlockSpec((1, grp, grp), lambda g: (g, 0, 0)),
                  pl.BlockSpec((1, grp), lambda g: (0, g)),
                  pl.BlockSpec((grp, d), lambda g: (g, 0))],
        out_specs=pl.BlockSpec((grp, d), lambda g: (g, 0)),
        out_shape=jax.ShapeDtypeStruct((pw, d), BF16),
        compiler_params=_cparams(1),
        name="pool_fold",
    )(pool_mix, pool_scale, w_pool_up)


def _mixer_in(x, norm_g, w_in, w_gate, b_gate, w_pool, ts):
    b, s, d = x.shape
    aw = N_HEADS * HEAD_W
    pw = len(POOL_WINDOWS) * POOL_GROUP
    n = b * s
    nt = s // ts
    const = lambda *shape: pl.BlockSpec(shape, lambda bi, j: (0,) * len(shape))
    tok3 = pl.BlockSpec((1, ts, aw), lambda bi, j: (bi, j, 0))
    tok2 = pl.BlockSpec((ts, d), lambda bi, j: (bi * nt + j, 0))
    return pl.pallas_call(
        _mixer_in_kernel,
        grid=(b, nt),
        in_specs=[
            pl.BlockSpec((1, ts, d), lambda bi, j: (bi, j, 0)),
            const(1, d),
            const(d, 3 * aw + pw),
            const(d, 2 * d),
            const(1, 2 * d),
            const(pw, d),
        ],
        out_specs=[tok3, tok3, tok3, tok2, tok2],
        out_shape=[jax.ShapeDtypeStruct((b, s, aw), BF16)] * 3
        + [jax.ShapeDtypeStruct((n, d), BF16)] * 2,
        scratch_shapes=[pltpu.VMEM((ts + POOL_PAD, pw), F32)],
        compiler_params=_cparams(2),
        name="mixer_in",
    )(x, norm_g, w_in, w_gate, b_gate, w_pool)


def _rel_bucket(rel, log=jnp.log, f32=lambda a: a.astype(jnp.float32),
                i32=lambda a: a.astype(jnp.int32), xp=jnp):
    nb = N_BUCKETS // 2
    ret = i32(rel > 0) * nb
    n = xp.abs(rel)
    max_exact = nb // 2
    nf = f32(xp.maximum(n, 1))
    large = max_exact + i32(log(nf / max_exact) / math.log(MAX_DISTANCE / max_exact)
                            * (nb - max_exact))
    large = xp.minimum(large, nb - 1)
    return ret + xp.where(n < max_exact, n, large)


def _far_bucket(block, seq):
    rel = -np.arange(block + 1, max(seq, block + 2), dtype=np.int32)
    bk = _rel_bucket(rel, log=np.log, f32=lambda a: a.astype(np.float32),
                     i32=lambda a: a.astype(np.int32), xp=np)
    assert (bk == bk[0]).all(), "far keys must share one relative-position bucket"
    return int(bk[0])


def _rel_bias_kernel(far_bucket, tab_ref, bidx_ref, lq1_ref, lk1_ref, lq2_ref, lk2_ref,
                     bias_ref, lam_ref):
    h = pl.program_id(0)
    bidx = bidx_ref[...]
    acc = jnp.zeros(bidx.shape, F32)
    for bkt in range(N_BUCKETS):
        acc = jnp.where(bidx == bkt, tab_ref[bkt, h], acc)
    acc = acc - tab_ref[far_bucket, h]
    bias_ref[0] = jnp.where(bidx < 0, -jnp.inf, acc)
    lam = (jnp.exp(jnp.sum(lq1_ref[...] * lk1_ref[...], keepdims=True))
           - jnp.exp(jnp.sum(lq2_ref[...] * lk2_ref[...], keepdims=True)) + LAMBDA_INIT)
    lam_ref[...] = jnp.broadcast_to(lam, lam_ref.shape)


def _rel_bias(rel_bias, lq1, lk1, lq2, lk2, seq):
    blk = ATTN_BLOCK
    qpos = jnp.arange(blk, dtype=I32)[:, None]
    kpos = jnp.arange(blk, dtype=I32)[None, :]
    diag = jnp.where(kpos // CHUNK <= qpos // CHUNK, _rel_bucket(kpos - qpos), -1)
    prev = _rel_bucket(kpos - (qpos + blk))
    bidx = jnp.stack([diag, prev]).astype(I32)
    vec = pl.BlockSpec((1, HEAD_DIM), lambda h: (0, 0))
    return pl.pallas_call(
        functools.partial(_rel_bias_kernel, _far_bucket(blk, seq)),
        grid=(N_HEADS,),
        in_specs=[
            pl.BlockSpec(memory_space=pltpu.SMEM),
            pl.BlockSpec((2, blk, blk), lambda h: (0, 0, 0)),
            vec, vec, vec, vec,
        ],
        out_specs=[
            pl.BlockSpec((1, 2, blk, blk), lambda h: (h, 0, 0, 0)),
            pl.BlockSpec((SUBLANES, LANES), lambda h: (0, 0)),
        ],
        out_shape=[jax.ShapeDtypeStruct((N_HEADS, 2, blk, blk), F32),
                   jax.ShapeDtypeStruct((SUBLANES, LANES), F32)],
        compiler_params=_cparams(1),
        name="rel_bias",
    )(rel_bias, bidx, lq1, lk1, lq2, lk2)


def _diff_attn_kernel(q_ref, k_ref, v_ref, bias_ref, lam_ref, sg_ref, o_ref, vext_ref):
    s_len = q_ref.shape[1]
    tq = ATTN_BLOCK
    n_heads = q_ref.shape[2] // HEAD_W
    lam = lam_ref[0:1, 0:1]
    lane = lax.broadcasted_iota(I32, (tq, HEAD_W), 1)
    cols = [slice(hh * HEAD_W, (hh + 1) * HEAD_W) for hh in range(n_heads)]
    b_diag, b_prev = [], []
    for hh in range(n_heads):
        vext_ref[hh, :, :HEAD_W] = v_ref[0, :, cols[hh]]
        vext_ref[hh, :, HEAD_W:] = jnp.ones((s_len, HEAD_W), BF16)
        b_diag.append(jnp.concatenate([bias_ref[hh, 0]] * 2, axis=0))
        b_prev.append(jnp.concatenate([bias_ref[hh, 1]] * 2, axis=0))

    for qi in range(s_len // tq):
        n_keys = (qi + 1) * tq
        rows = slice(qi * tq, (qi + 1) * tq)
        for hh in range(n_heads):
            q = q_ref[0, rows, cols[hh]]
            zero = jnp.zeros_like(q)
            qs = jnp.concatenate([jnp.where(lane < HEAD_DIM, q, zero),
                                  jnp.where(lane >= HEAD_DIM, q, zero)], axis=0)
            s = _dot_nt(qs, k_ref[0, :n_keys, cols[hh]])
            pieces = []
            if qi >= 2:
                pieces.append(s[:, :n_keys - 2 * tq])
            if qi >= 1:
                pieces.append(s[:, n_keys - 2 * tq:n_keys - tq] + b_prev[hh])
            pieces.append(s[:, n_keys - tq:] + b_diag[hh])
            s = jnp.concatenate(pieces, axis=1) if len(pieces) > 1 else pieces[0]
            m = jnp.max(s, axis=-1, keepdims=True)
            p = jnp.exp(s - m).astype(BF16)
            acc = _dot(p, vext_ref[hh, :n_keys, :])
            o = acc[:, :HEAD_W] / acc[:, HEAD_W:HEAD_W + 1]
            a = o[:tq] - lam * o[tq:]
            o_ref[0, rows, cols[hh]] = (
                _rms(a, sg_ref[...]) * (1.0 - LAMBDA_INIT)).astype(BF16)


def _diff_attn(q, k, v, bias, lam, subln_g, heads_per_step):
    b, s, aw = q.shape
    tq = ATTN_BLOCK
    hp = heads_per_step
    spec = pl.BlockSpec((1, s, hp * HEAD_W), lambda bi, h: (bi, 0, h))
    return pl.pallas_call(
        _diff_attn_kernel,
        grid=(b, N_HEADS // hp),
        in_specs=[
            spec, spec, spec,
            pl.BlockSpec((hp, 2, tq, tq), lambda bi, h: (h, 0, 0, 0)),
            pl.BlockSpec((SUBLANES, LANES), lambda bi, h: (0, 0)),
            pl.BlockSpec((1, HEAD_W), lambda bi, h: (0, 0)),
        ],
        out_specs=spec,
        out_shape=jax.ShapeDtypeStruct((b, s, aw), BF16),
        scratch_shapes=[pltpu.VMEM((hp, s, 2 * HEAD_W), BF16)],
        compiler_params=_cparams(2),
        name="diff_attn",
    )(q, k, v, bias, lam, subln_g)


def _mem_fold_kernel(m_ref, g_ref, wkv_ref, wq_ref, wo_ref, qk_ref, vo_ref):
    d = m_ref.shape[2]
    hd = d // X_HEADS
    m = _rms(m_ref[0], g_ref[...]).astype(BF16)
    kv = _dot(m, wkv_ref[...])
    k = kv[:, :d].astype(BF16)
    v = kv[:, d:].astype(BF16)
    for hh in range(X_HEADS):
        sl = slice(hh * hd, (hh + 1) * hd)
        qk_ref[0, :, hh * m.shape[0]:(hh + 1) * m.shape[0]] = (
            _dot_nt(wq_ref[:, sl], k[:, sl]) * (hd ** -0.5)).astype(BF16)
        vo_ref[0, hh * m.shape[0]:(hh + 1) * m.shape[0], :] = _dot(v[:, sl], wo_ref[sl, :]).astype(BF16)


def _mem_fold(mem, norm_g, w_xkv, w_xq, w_xo):
    b, ml, d = mem.shape
    const = lambda *shape: pl.BlockSpec(shape, lambda bi: (0,) * len(shape))
    return pl.pallas_call(
        _mem_fold_kernel,
        grid=(b,),
        in_specs=[pl.BlockSpec((1, ml, d), lambda bi: (bi, 0, 0)), const(1, d),
                  const(d, 2 * d), const(d, d), const(d, d)],
        out_specs=[pl.BlockSpec((1, d, X_HEADS * ml), lambda bi: (bi, 0, 0)),
                   pl.BlockSpec((1, X_HEADS * ml, d), lambda bi: (bi, 0, 0))],
        out_shape=[jax.ShapeDtypeStruct((b, d, X_HEADS * ml), BF16),
                   jax.ShapeDtypeStruct((b, X_HEADS * ml, d), BF16)],
        compiler_params=_cparams(1),
        name="mem_fold",
    )(mem, norm_g, w_xkv, w_xq, w_xo)


def _post_kernel(x_ref, a_ref, g0_ref, gyb_ref, wau_ref, wo_ref, nxg_ref,
                 qk_ref, vo_ref, nfg_ref, wrt_ref, br_ref,
                 x2_ref, hf_ref, eidx_ref, rank_ref, gate_ref, cnt_ref, run_ref):
    ts, d = x_ref.shape
    first = (pl.program_id(0) == 0) & (pl.program_id(1) == 0)

    y_a = _dot(a_ref[...], wau_ref[...])
    merged = g0_ref[...].astype(F32) * y_a + gyb_ref[...].astype(F32)
    x1 = x_ref[...] + _dot(merged.astype(BF16), wo_ref[...])

    ml = qk_ref.shape[2] // X_HEADS
    hx = _rms(x1, nxg_ref[...]).astype(BF16)
    s_all = _dot(hx, qk_ref[0])
    probs = []
    for hh in range(X_HEADS):
        s = s_all[:, hh * ml:(hh + 1) * ml]
        p = jnp.exp(s - jnp.max(s, axis=-1, keepdims=True))
        probs.append((p / jnp.sum(p, axis=-1, keepdims=True)).astype(BF16))
    x2 = x1 + _dot(jnp.concatenate(probs, axis=1), vo_ref[0])
    x2_ref[...] = x2
    hf = _rms(x2, nfg_ref[...])
    hf_ref[...] = _pack_halves(hf)

    logits = _dot_nt(wrt_ref[...], hf, precision=lax.Precision.HIGHEST) + br_ref[...]
    ne = logits.shape[0]
    eid = lax.broadcasted_iota(I32, logits.shape, 0).astype(F32)
    work = logits
    vals, idxs, hots = [], [], []
    for _ in range(TOP_K):
        mx = jnp.max(work, axis=0, keepdims=True)
        idx = jnp.min(jnp.where(work == mx, eid, float(ne)), axis=0, keepdims=True)
        hot = eid == idx
        vals.append(mx)
        idxs.append(idx.astype(I32))
        hots.append(hot)
        work = jnp.where(hot, -jnp.inf, work)
    ex = [jnp.exp(vv - vals[0]) for vv in vals]
    den = ex[0] + ex[1] + ex[2] + ex[3]
    gates = [e_ / den for e_ in ex]

    multi = (hots[0] | hots[1] | hots[2] | hots[3])
    multi_f = jnp.where(multi, 1.0, 0.0).astype(F32)
    tri = (lax.broadcasted_iota(I32, (ts, ts), 0)
           < lax.broadcasted_iota(I32, (ts, ts), 1))
    before = _dot(multi_f.astype(BF16), jnp.where(tri, 1.0, 0.0).astype(BF16))

    @pl.when(first)
    def _():
        run_ref[...] = jnp.zeros(run_ref.shape, F32)

    run = run_ref[...]
    pos = before + run
    ranks = [jnp.sum(jnp.where(hot, pos, 0.0), axis=0, keepdims=True) for hot in hots]
    run_new = run + jnp.sum(multi_f, axis=1, keepdims=True)
    run_ref[...] = run_new
    cnt_ref[...] = jnp.broadcast_to(run_new, cnt_ref.shape).astype(I32)

    eidx_ref[...] = jnp.concatenate(idxs, axis=0)
    rank_ref[...] = jnp.concatenate(ranks, axis=0).astype(I32)
    g_rows = jnp.concatenate(gates + [jnp.zeros((LANES - TOP_K, ts), F32)], axis=0)
    gate_ref[...] = g_rows.T


def _post(x2d, a2d, g0, gyb, w_attn_up, w_out, norm_x_g, qk_fold, vo_fold,
          norm_ffn_g, w_router_t, b_router, batch, ts):
    n, d = x2d.shape
    nt = n // batch // ts
    tok = pl.BlockSpec((ts, d), lambda bi, j: (bi * nt + j, 0))
    lanes = pl.BlockSpec((TOP_K, ts), lambda bi, j: (0, bi * nt + j))
    const = lambda *shape: pl.BlockSpec(shape, lambda bi, j: (0,) * len(shape),
                                        pipeline_mode=pl.Buffered(1))
    fold = lambda arr: pl.BlockSpec((1,) + arr.shape[1:], lambda bi, j: (bi, 0, 0))
    return pl.pallas_call(
        _post_kernel,
        grid=(batch, nt),
        in_specs=[tok, tok, tok, tok, const(d, d), const(d, d), const(1, d),
                  fold(qk_fold), fold(vo_fold), const(1, d), const(N_EXPERTS, d),
                  const(N_EXPERTS, 1)],
        out_specs=[tok, pl.BlockSpec((ts, d // 2), lambda bi, j: (bi * nt + j, 0)),
                   lanes, lanes,
                   pl.BlockSpec((ts, LANES), lambda bi, j: (bi * nt + j, 0)),
                   pl.BlockSpec((N_EXPERTS, LANES), lambda bi, j: (0, 0))],
        out_shape=[jax.ShapeDtypeStruct((n, d), F32),
                   jax.ShapeDtypeStruct((n, d // 2), jnp.uint32),
                   jax.ShapeDtypeStruct((TOP_K, n), I32), jax.ShapeDtypeStruct((TOP_K, n), I32),
                   jax.ShapeDtypeStruct((n, LANES), F32),
                   jax.ShapeDtypeStruct((N_EXPERTS, LANES), I32)],
        scratch_shapes=[pltpu.VMEM((N_EXPERTS, 1), F32)],
        compiler_params=_cparams(2),
        name="post",
    )(x2d, a2d, g0, gyb, w_attn_up, w_out, norm_x_g, qk_fold, vo_fold,
      norm_ffn_g, w_router_t, b_router)


def _route_dest_kernel(pstart_ref, eidx_ref, rank_ref, dest_ref):
    eidx = eidx_ref[...]
    start = jnp.zeros(eidx.shape, I32)
    for e in range(N_EXPERTS):
        start = jnp.where(eidx == e, pstart_ref[e], start)
    dest_ref[...] = start + rank_ref[...]


def _route_dest(pstart, eidx, rank, tl):
    k, n = eidx.shape
    lanes = pl.BlockSpec((k, tl), lambda i: (0, i))
    return pl.pallas_call(
        _route_dest_kernel,
        grid=(n // tl,),
        in_specs=[pl.BlockSpec(memory_space=pltpu.SMEM), lanes, lanes],
        out_specs=lanes,
        out_shape=jax.ShapeDtypeStruct((k, n), I32),
        compiler_params=_cparams(1),
        name="route_dest",
    )(pstart, eidx, rank)


SC_CORES = 2
SC_SUBCORES = 16
SC_WORKERS = SC_CORES * SC_SUBCORES
MOVE_CHUNK = 64
MOVE_BUFFERS = 3


def _worker_index_layout(dest, n):
    per_w = n // SC_WORKERS
    n_chunks = per_w // MOVE_CHUNK
    d4 = dest.reshape(TOP_K, SC_WORKERS, n_chunks, MOVE_CHUNK)
    return jnp.transpose(d4, (1, 2, 0, 3)).reshape(SC_WORKERS, n_chunks * TOP_K, MOVE_CHUNK)


def _sc_mesh():
    return plsc.VectorSubcoreMesh(core_axis_name="c", subcore_axis_name="s",
                                  num_cores=SC_CORES, num_subcores=SC_SUBCORES)


def _sc_dispatch(hf, dest_w, rows):
    n, d = hf.shape
    per_w = n // SC_WORKERS
    n_chunks = per_w // MOVE_CHUNK

    def body(hf_hbm, dest_hbm, xs_hbm, idx_v, rows_v, rsem, wsem):
        wid = lax.axis_index("s") * SC_CORES + lax.axis_index("c")
        base = wid * per_w
        pltpu.sync_copy(dest_hbm.at[wid], idx_v)

        def read(c):
            b = c % MOVE_BUFFERS
            return pltpu.async_copy(hf_hbm.at[pl.ds(base + c * MOVE_CHUNK, MOVE_CHUNK)],
                                    rows_v.at[b], rsem.at[b])

        scatters = [[] for _ in range(MOVE_BUFFERS)]
        pending = read(0)
        for c in range(n_chunks):
            b = c % MOVE_BUFFERS
            pending.wait()
            if c + 1 < n_chunks:
                for cp in scatters[(c + 1) % MOVE_BUFFERS]:
                    cp.wait()
                pending = read(c + 1)
            scatters[b] = [
                pltpu.async_copy(rows_v.at[b], xs_hbm.at[idx_v.at[c * TOP_K + kk]], wsem.at[b])
                for kk in range(TOP_K)]
        for group in scatters:
            for cp in group:
                cp.wait()

    return pl.kernel(
        body,
        out_type=jax.ShapeDtypeStruct((rows, d), hf.dtype),
        mesh=_sc_mesh(),
        scratch_types=[pltpu.VMEM((n_chunks * TOP_K, MOVE_CHUNK), I32),
                       pltpu.VMEM((MOVE_BUFFERS, MOVE_CHUNK, d), hf.dtype),
                       pltpu.SemaphoreType.DMA((MOVE_BUFFERS,)),
                       pltpu.SemaphoreType.DMA((MOVE_BUFFERS,))],
        name="sc_dispatch",
    )(hf, dest_w)


def _sc_gather(ys, dest_w, n):
    _, d = ys.shape
    per_w = n // SC_WORKERS
    n_chunks = per_w // MOVE_CHUNK

    def body(ys_hbm, dest_hbm, yg_hbm, idx_v, rows_v, rsem, wsem):
        wid = lax.axis_index("s") * SC_CORES + lax.axis_index("c")
        base = wid * per_w
        pltpu.sync_copy(dest_hbm.at[wid], idx_v)
        n_moves = n_chunks * TOP_K

        def gather(m):
            b = m % MOVE_BUFFERS
            return pltpu.async_copy(ys_hbm.at[idx_v.at[m]], rows_v.at[b], rsem.at[b])

        def write(m):
            b = m % MOVE_BUFFERS
            c, kk = divmod(m, TOP_K)
            return pltpu.async_copy(
                rows_v.at[b], yg_hbm.at[pl.ds(kk * n + base + c * MOVE_CHUNK, MOVE_CHUNK)],
                wsem.at[b])

        writes = [None] * MOVE_BUFFERS
        pending = gather(0)
        for m in range(n_moves):
            pending.wait()
            if m + 1 < n_moves:
                nb = (m + 1) % MOVE_BUFFERS
                if writes[nb] is not None:
                    writes[nb].wait()
                pending = gather(m + 1)
            writes[m % MOVE_BUFFERS] = write(m)
        for wr in writes:
            if wr is not None:
                wr.wait()

    return pl.kernel(
        body,
        out_type=jax.ShapeDtypeStruct((TOP_K * n, d), ys.dtype),
        mesh=_sc_mesh(),
        scratch_types=[pltpu.VMEM((n_chunks * TOP_K, MOVE_CHUNK), I32),
                       pltpu.VMEM((MOVE_BUFFERS, MOVE_CHUNK, d), ys.dtype),
                       pltpu.SemaphoreType.DMA((MOVE_BUFFERS,)),
                       pltpu.SemaphoreType.DMA((MOVE_BUFFERS,))],
        name="sc_gather",
    )(ys, dest_w)


def _combine_kernel(x2_ref, gate_ref, fg_ref, yg_ref, o_ref):
    g = gate_ref[...]
    x2 = x2_ref[...]
    c = x2.shape[1] // 2
    acc_hi, acc_lo = x2[:, :c], x2[:, c:]
    for kk in range(TOP_K):
        hi, lo = _unpack_halves(yg_ref[kk])
        acc_hi = acc_hi + g[:, kk:kk + 1] * hi
        acc_lo = acc_lo + g[:, kk:kk + 1] * lo
    o_ref[...] = _rms(jnp.concatenate([acc_hi, acc_lo], axis=1), fg_ref[...])


def _combine(x2, gate_tm, final_g, yg, ts):
    n, d = x2.shape
    dw = yg.shape[1]
    return pl.pallas_call(
        _combine_kernel,
        grid=(n // ts,),
        in_specs=[pl.BlockSpec((ts, d), lambda i: (i, 0)),
                  pl.BlockSpec((ts, LANES), lambda i: (i, 0)),
                  pl.BlockSpec((1, d), lambda i: (0, 0)),
                  pl.BlockSpec((TOP_K, ts, dw), lambda i: (0, i, 0))],
        out_specs=pl.BlockSpec((ts, d), lambda i: (i, 0)),
        out_shape=jax.ShapeDtypeStruct((n, d), F32),
        compiler_params=_cparams(1),
        name="combine",
    )(x2, gate_tm, final_g, yg.reshape(TOP_K, n, dw))


def _experts_kernel(be_ref, first_ref, parts_ref, nxt_ref, slot_ref, nb_ref,
                    xs_ref, bgu_ref, bd_ref, wgu_hbm, wd_hbm, ys_ref,
                    wgu_st, wd_st, wgu_bf, wd_bf, sem_w):
    dff = wd_bf.shape[0]
    bm = xs_ref.shape[0] // EXPERT_BLOCKS_PER_STEP

    def fetch(ex, sl):
        return (pltpu.make_async_copy(wgu_hbm.at[ex], wgu_st.at[sl], sem_w.at[0, sl]),
                pltpu.make_async_copy(wd_hbm.at[ex], wd_st.at[sl], sem_w.at[1, sl]))

    @pl.when(pl.program_id(0) == 0)
    def _():
        e0 = be_ref[0]
        for cp in fetch(e0, slot_ref[e0]):
            cp.start()

    for sub in range(EXPERT_BLOCKS_PER_STEP):
        blk = pl.program_id(0) * EXPERT_BLOCKS_PER_STEP + sub
        e = be_ref[blk]
        rows = slice(sub * bm, (sub + 1) * bm)

        @pl.when(first_ref[blk] == 1)
        def _(e=e):
            sl = slot_ref[e]
            for cp in fetch(e, sl):
                cp.wait()
            nx = nxt_ref[e]

            @pl.when(nx >= 0)
            def _():
                for cp in fetch(nx, 1 - sl):
                    cp.start(priority=1)

            wgu_bf[...] = wgu_st[sl].astype(BF16)
            wd_bf[...] = wd_st[sl].astype(BF16)

        def mlp(e, rows):
            xb = jnp.concatenate(_unpack_halves(xs_ref[rows, :]), axis=1).astype(BF16)
            gu = _dot(xb, wgu_bf[...]) + bgu_ref[e]
            x_glu = jnp.minimum(gu[:, :dff], SWIGLU_LIMIT)
            x_lin = jnp.clip(gu[:, dff:], -SWIGLU_LIMIT, SWIGLU_LIMIT)
            act = x_glu * jax.nn.sigmoid(SWIGLU_ALPHA * x_glu) * (x_lin + 1.0)
            ys_ref[rows, :] = _pack_halves(_dot(act.astype(BF16), wd_bf[...]) + bd_ref[e])

        def clear(rows):
            ys_ref[rows, :] = jnp.zeros((rows.stop - rows.start, ys_ref.shape[1]), jnp.uint32)

        part = bm // EXPERT_TAIL_PARTS
        for live in range(EXPERT_TAIL_PARTS + 1):
            @pl.when(parts_ref[blk] == live)
            def _(e=e, rows=rows, live=live):
                cut = rows.start + live * part
                if live:
                    mlp(e, slice(rows.start, cut))
                if live < EXPERT_TAIL_PARTS:
                    clear(slice(cut, rows.stop))


def _expert_schedule(counts, bm, n_blocks):
    ne = counts.shape[0]
    ids = jnp.arange(ne, dtype=I32)
    upto = ids[None, :] <= ids[:, None]
    padded = (counts + bm - 1) // bm * bm
    pend = jnp.sum(jnp.where(upto, padded[None, :], 0), axis=1)
    pstart = (pend - padded).astype(I32)
    nb_used = (jnp.sum(padded) // bm).astype(I32)
    blk = jnp.arange(n_blocks, dtype=I32)
    blk_c = jnp.minimum(blk, nb_used - 1)
    be = jnp.minimum(jnp.sum((pend[None, :] <= (blk_c * bm)[:, None]).astype(I32), axis=1),
                     ne - 1)
    started = jnp.sum(((pstart[None, :] == (blk * bm)[:, None]) & (padded[None, :] > 0))
                      .astype(I32), axis=1)
    first = ((blk < nb_used) & (started > 0)).astype(I32)
    used = counts > 0
    seq = jnp.sum(jnp.where(upto & used[None, :], 1, 0), axis=1) - 1
    later = used[None, :] & (ids[None, :] > ids[:, None])
    nxt = jnp.where(jnp.any(later, axis=1), jnp.argmax(later, axis=1), -1).astype(I32)
    slot = (seq & 1).astype(I32)
    hit = ids[None, :] == be[:, None]
    left = jnp.sum(jnp.where(hit, (pstart + counts)[None, :], 0), axis=1) - blk * bm
    part = bm // EXPERT_TAIL_PARTS
    parts = jnp.where(blk < nb_used,
                      jnp.minimum((left + part - 1) // part, EXPERT_TAIL_PARTS), 0).astype(I32)
    return pstart, nb_used.reshape(1), be, first, parts, nxt, slot


def _experts(be, first, parts, nxt, slot, nb_used, xs, w_gu, b_gu, w_down, b_down, bm):
    rows, dw = xs.shape
    ne, d, dff2 = w_gu.shape
    dff = dff2 // 2
    sb = EXPERT_BLOCKS_PER_STEP * bm
    assert rows % sb == 0
    last = lambda j, be, fi, ha, nx, sl, nb: (
        jnp.maximum(jnp.minimum(j, (nb[0] - 1) // EXPERT_BLOCKS_PER_STEP), 0), 0)
    const = lambda *shape: pl.BlockSpec(shape,
                                        lambda j, be, fi, ha, nx, sl, nb: (0,) * len(shape))
    grid_spec = pltpu.PrefetchScalarGridSpec(
        num_scalar_prefetch=6,
        grid=(rows // sb,),
        in_specs=[
            pl.BlockSpec((sb, dw), last),
            const(ne, 1, dff2),
            const(ne, 1, d),
            pl.BlockSpec(memory_space=pl.ANY),
            pl.BlockSpec(memory_space=pl.ANY),
        ],
        out_specs=pl.BlockSpec((sb, dw), lambda j, be, fi, ha, nx, sl, nb: (j, 0)),
        scratch_shapes=[pltpu.VMEM((2, d, dff2), F32), pltpu.VMEM((2, dff, d), F32),
                        pltpu.VMEM((d, dff2), BF16), pltpu.VMEM((dff, d), BF16),
                        pltpu.SemaphoreType.DMA((2, 2))],
    )
    return pl.pallas_call(
        _experts_kernel,
        grid_spec=grid_spec,
        out_shape=jax.ShapeDtypeStruct((rows, dw), jnp.uint32),
        compiler_params=_cparams(1),
        name="experts",
    )(be, first, parts, nxt, slot, nb_used, xs, b_gu.reshape(ne, 1, dff2),
      b_down.reshape(ne, 1, d), w_gu, w_down)


MIXER_TILE = 512
ATTN_HEADS_PER_STEP = 4
POST_TILE = 512
DEST_TILE = 2048
COMBINE_TILE = 1024
EXPERT_BLOCK = 1024
EXPERT_TAIL_PARTS = 8
EXPERT_BLOCKS_PER_STEP = 1


def kernel(x, mem, norm_mix_g, w_in, lambda_q1, lambda_k1, lambda_q2, lambda_k2, rel_bias,
           subln_g, w_attn_up, pool_mix, pool_scale, w_pool_up, w_gate, b_gate, w_out,
           norm_x_g, norm_mem_g, w_xq, w_xkv, w_xo, norm_ffn_g, w_router, b_router,
           w_gu, b_gu, w_down, b_down, final_norm_g):
    b, s, d = x.shape
    n = b * s
    assert w_in.shape[0] == 1, "single-layer block"
    row = lambda a: a.reshape(1, -1)
    bf = lambda a: a[0].astype(BF16)

    w_pool = _pool_fold(pool_mix[0], row(pool_scale[0]), bf(w_pool_up))
    q, k, v, g0, gyb = _mixer_in(
        x, row(norm_mix_g[0]), bf(w_in), bf(w_gate), row(b_gate[0]), w_pool, MIXER_TILE)

    bias, lam = _rel_bias(rel_bias, row(lambda_q1[0]), row(lambda_k1[0]),
                          row(lambda_q2[0]), row(lambda_k2[0]), s)
    a = _diff_attn(q, k, v, bias, lam, row(subln_g[0]), ATTN_HEADS_PER_STEP)

    qk_fold, vo_fold = _mem_fold(mem, row(norm_mem_g[0]), bf(w_xkv), bf(w_xq), bf(w_xo))

    x2, hf, eidx, rank, gate_tm, counts = _post(
        x.reshape(n, d), a.reshape(n, d), g0, gyb, bf(w_attn_up), bf(w_out),
        row(norm_x_g[0]), qk_fold, vo_fold, row(norm_ffn_g[0]),
        w_router[0].T, b_router[0].reshape(-1, 1), b, POST_TILE)

    bm = EXPERT_BLOCK
    rows = n * TOP_K + N_EXPERTS * bm
    pstart, nb_used, be, first, parts, nxt, slot = _expert_schedule(
        counts[:, 0], bm, rows // bm)

    dest_w = _worker_index_layout(_route_dest(pstart, eidx, rank, DEST_TILE), n)
    xs = _sc_dispatch(hf, dest_w, rows)
    ys = _experts(be, first, parts, nxt, slot, nb_used, xs, w_gu[0], b_gu[0], w_down[0],
                  b_down[0], bm)
    yg = _sc_gather(ys, dest_w, n)
    out = _combine(x2, gate_tm, row(final_norm_g), yg, COMBINE_TILE)
    return out.reshape(b, s, d)
```

```python
import functools
import math

import numpy as np
import jax
import jax.numpy as jnp
from jax import lax
from jax.experimental import pallas as pl
from jax.experimental.pallas import tpu as pltpu
from jax.experimental.pallas import tpu_sc as plsc

F32 = jnp.float32
BF16 = jnp.bfloat16
I32 = jnp.int32

EPS = 1e-6
CHUNK = 64
N_HEADS = 8
HEAD_DIM = 64
HEAD_W = 2 * HEAD_DIM
POOL_WINDOWS = (2, 4, 8, 16)
POOL_GROUP = 128
POOL_PAD = 16
N_BUCKETS = 32
MAX_DISTANCE = 128
X_HEADS = 4
N_EXPERTS = 32
TOP_K = 4
SWIGLU_ALPHA = 1.702
SWIGLU_LIMIT = 7.0
LAMBDA_INIT = 0.8 - 0.6 * math.exp(-0.3 * 0)

LANES = 128
SUBLANES = 8
ATTN_BLOCK = 256
VMEM_LIMIT = 56 * 1024 * 1024


def _cparams(n_axes, vmem=VMEM_LIMIT):
    return pltpu.CompilerParams(
        dimension_semantics=("arbitrary",) * n_axes, vmem_limit_bytes=vmem)


def _rms(xf, g):
    ms = jnp.mean(xf * xf, axis=-1, keepdims=True)
    return xf * lax.rsqrt(ms + EPS) * g


def _dot(a, b):
    return jnp.dot(a, b, preferred_element_type=F32)


def _pack_halves(x):
    c = x.shape[1] // 2
    as_bits = lambda v: lax.bitcast_convert_type(v.astype(BF16).astype(F32), jnp.uint32)
    return as_bits(x[:, :c]) | (as_bits(x[:, c:]) >> 16)


def _unpack_halves(w):
    hi = lax.bitcast_convert_type(w & jnp.uint32(0xFFFF0000), F32)
    lo = lax.bitcast_convert_type(w << 16, F32)
    return hi, lo


def _dot_nt(a, b, precision=None):
    return lax.dot_general(a, b, (((1,), (1,)), ((), ())),
                           preferred_element_type=F32, precision=precision)


def _mixer_in_kernel(x_ref, g_ref, wq_ref, wk_ref, wv_ref, wu_ref, wg_ref, bg_ref,
                     pmix_ref, pscale_ref, wpu_ref,
                     q_ref, k_ref, v_ref, g0_ref, gyb_ref, ext_ref, wpool_ref):
    ts = x_ref.shape[1]
    d = x_ref.shape[2]
    j = pl.program_id(1)

    @pl.when((pl.program_id(0) == 0) & (j == 0))
    def _():
        for gi in range(len(POOL_WINDOWS)):
            sl = slice(gi * POOL_GROUP, (gi + 1) * POOL_GROUP)
            mix = (pmix_ref[gi].astype(F32) * pscale_ref[:, sl]).astype(BF16)
            wpool_ref[sl, :] = _dot(mix, wpu_ref[sl, :]).astype(BF16)

    h = _rms(x_ref[0], g_ref[...]).astype(BF16)
    q_ref[0] = (_dot(h, wq_ref[...]) * (HEAD_DIM ** -0.5)).astype(BF16)
    k_ref[0] = _dot(h, wk_ref[...]).astype(BF16)
    v_ref[0] = _dot(h, wv_ref[...]).astype(BF16)
    u = _dot(h, wu_ref[...])

    @pl.when(j == 0)
    def _():
        ext_ref[0:POOL_PAD, :] = jnp.zeros((POOL_PAD, u.shape[1]), F32)

    ext_ref[POOL_PAD:POOL_PAD + ts, :] = u
    e = ext_ref[...]
    sums = {}
    s = e
    w = 1
    while w < max(POOL_WINDOWS):
        s = s + pltpu.roll(s, w, 0)
        w *= 2
        sums[w] = s
    ext_ref[0:POOL_PAD, :] = ext_ref[ts:ts + POOL_PAD, :]

    pos = (j * ts + lax.broadcasted_iota(I32, (ts, 1), 0) + 1).astype(F32)
    pooled = []
    for gi, w in enumerate(POOL_WINDOWS):
        sl = slice(gi * POOL_GROUP, (gi + 1) * POOL_GROUP)
        win = sums[w][POOL_PAD:, sl]
        pooled.append(win / jnp.minimum(pos, float(w)) - u[:, sl])
    y_b = _dot(jnp.concatenate(pooled, axis=1).astype(BF16), wpool_ref[...])

    gate = jax.nn.sigmoid(_dot(h, wg_ref[...]) + bg_ref[...])
    g0_ref[...] = gate[:, :d].astype(BF16)
    gyb_ref[...] = (gate[:, d:] * y_b).astype(BF16)


def _mixer_in(x, norm_g, w_in, w_gate, b_gate, pool_mix, pool_scale, w_pool_up, ts):
    b, s, d = x.shape
    aw = N_HEADS * HEAD_W
    pw = len(POOL_WINDOWS) * POOL_GROUP
    n = b * s
    nt = s // ts
    const = lambda *shape: pl.BlockSpec(shape, lambda bi, j: (0,) * len(shape))
    tok3 = pl.BlockSpec((1, ts, aw), lambda bi, j: (bi, j, 0))
    tok2 = pl.BlockSpec((ts, d), lambda bi, j: (bi * nt + j, 0))
    return pl.pallas_call(
        _mixer_in_kernel,
        grid=(b, nt),
        in_specs=[
            pl.BlockSpec((1, ts, d), lambda bi, j: (bi, j, 0)),
            const(1, d),
            pl.BlockSpec((d, aw), lambda bi, j: (0, 0)),
            pl.BlockSpec((d, aw), lambda bi, j: (0, 1)),
            pl.BlockSpec((d, aw), lambda bi, j: (0, 2)),
            pl.BlockSpec((d, pw), lambda bi, j: (0, 3 * aw // pw)),
            const(d, 2 * d),
            const(1, 2 * d),
            const(len(POOL_WINDOWS), POOL_GROUP, POOL_GROUP),
            const(1, pw),
            const(pw, d),
        ],
        out_specs=[tok3, tok3, tok3, tok2, tok2],
        out_shape=[jax.ShapeDtypeStruct((b, s, aw), BF16)] * 3
        + [jax.ShapeDtypeStruct((n, d), BF16)] * 2,
        scratch_shapes=[pltpu.VMEM((ts + POOL_PAD, pw), F32), pltpu.VMEM((pw, d), BF16)],
        compiler_params=_cparams(2),
        name="mixer_in",
    )(x, norm_g, w_in, w_in, w_in, w_in, w_gate, b_gate, pool_mix, pool_scale, w_pool_up)


def _rel_bucket(rel, log=jnp.log, f32=lambda a: a.astype(jnp.float32),
                i32=lambda a: a.astype(jnp.int32), xp=jnp):
    nb = N_BUCKETS // 2
    ret = i32(rel > 0) * nb
    n = xp.abs(rel)
    max_exact = nb // 2
    nf = f32(xp.maximum(n, 1))
    large = max_exact + i32(log(nf / max_exact) / math.log(MAX_DISTANCE / max_exact)
                            * (nb - max_exact))
    large = xp.minimum(large, nb - 1)
    return ret + xp.where(n < max_exact, n, large)


def _far_bucket(block, seq):
    rel = -np.arange(block + 1, max(seq, block + 2), dtype=np.int32)
    bk = _rel_bucket(rel, log=np.log, f32=lambda a: a.astype(np.float32),
                     i32=lambda a: a.astype(np.int32), xp=np)
    assert (bk == bk[0]).all(), "far keys must share one relative-position bucket"
    return int(bk[0])


def _rel_bias_kernel(far_bucket, tab_ref, bidx_ref, lq1_ref, lk1_ref, lq2_ref, lk2_ref,
                     bias_ref, lam_ref):
    h = pl.program_id(0)
    bidx = bidx_ref[...]
    acc = jnp.zeros(bidx.shape, F32)
    for bkt in range(N_BUCKETS):
        acc = jnp.where(bidx == bkt, tab_ref[bkt, h], acc)
    acc = acc - tab_ref[far_bucket, h]
    bias_ref[0] = jnp.where(bidx < 0, -jnp.inf, acc)
    lam = (jnp.exp(jnp.sum(lq1_ref[...] * lk1_ref[...], keepdims=True))
           - jnp.exp(jnp.sum(lq2_ref[...] * lk2_ref[...], keepdims=True)) + LAMBDA_INIT)
    lam_ref[...] = jnp.broadcast_to(lam, lam_ref.shape)


def _rel_bias(rel_bias, lq1, lk1, lq2, lk2, seq):
    blk = ATTN_BLOCK
    qpos = jnp.arange(blk, dtype=I32)[:, None]
    kpos = jnp.arange(blk, dtype=I32)[None, :]
    diag = jnp.where(kpos // CHUNK <= qpos // CHUNK, _rel_bucket(kpos - qpos), -1)
    prev = _rel_bucket(kpos - (qpos + blk))
    bidx = jnp.stack([diag, prev]).astype(I32)
    vec = pl.BlockSpec((1, HEAD_DIM), lambda h: (0, 0))
    return pl.pallas_call(
        functools.partial(_rel_bias_kernel, _far_bucket(blk, seq)),
        grid=(N_HEADS,),
        in_specs=[
            pl.BlockSpec(memory_space=pltpu.SMEM),
            pl.BlockSpec((2, blk, blk), lambda h: (0, 0, 0)),
            vec, vec, vec, vec,
        ],
        out_specs=[
            pl.BlockSpec((1, 2, blk, blk), lambda h: (h, 0, 0, 0)),
            pl.BlockSpec((SUBLANES, LANES), lambda h: (0, 0)),
        ],
        out_shape=[jax.ShapeDtypeStruct((N_HEADS, 2, blk, blk), F32),
                   jax.ShapeDtypeStruct((SUBLANES, LANES), F32)],
        compiler_params=_cparams(1),
        name="rel_bias",
    )(rel_bias, bidx, lq1, lk1, lq2, lk2)


def _diff_attn_kernel(q_ref, k_ref, v_ref, bias_ref, lam_ref, sg_ref, o_ref, vext_ref):
    s_len = q_ref.shape[1]
    tq = ATTN_BLOCK
    n_heads = q_ref.shape[2] // HEAD_W
    lam = lam_ref[0:1, 0:1]
    lane = lax.broadcasted_iota(I32, (tq, HEAD_W), 1)
    cols = [slice(hh * HEAD_W, (hh + 1) * HEAD_W) for hh in range(n_heads)]
    b_diag, b_prev = [], []
    for hh in range(n_heads):
        vext_ref[hh, :, :HEAD_W] = v_ref[0, :, cols[hh]]
        vext_ref[hh, :, HEAD_W:] = jnp.ones((s_len, HEAD_W), BF16)
        b_diag.append(jnp.concatenate([bias_ref[hh, 0]] * 2, axis=0))
        b_prev.append(jnp.concatenate([bias_ref[hh, 1]] * 2, axis=0))

    for qi in range(s_len // tq):
        n_keys = (qi + 1) * tq
        rows = slice(qi * tq, (qi + 1) * tq)
        for hh in range(n_heads):
            q = q_ref[0, rows, cols[hh]]
            zero = jnp.zeros_like(q)
            qs = jnp.concatenate([jnp.where(lane < HEAD_DIM, q, zero),
                                  jnp.where(lane >= HEAD_DIM, q, zero)], axis=0)
            s = _dot_nt(qs, k_ref[0, :n_keys, cols[hh]])
            pieces = []
            if qi >= 2:
                pieces.append(s[:, :n_keys - 2 * tq])
            if qi >= 1:
                pieces.append(s[:, n_keys - 2 * tq:n_keys - tq] + b_prev[hh])
            pieces.append(s[:, n_keys - tq:] + b_diag[hh])
            s = jnp.concatenate(pieces, axis=1) if len(pieces) > 1 else pieces[0]
            m = jnp.max(s, axis=-1, keepdims=True)
            p = jnp.exp(s - m).astype(BF16)
            acc = _dot(p, vext_ref[hh, :n_keys, :])
            o = acc[:, :HEAD_W] / acc[:, HEAD_W:HEAD_W + 1]
            a = o[:tq] - lam * o[tq:]
            o_ref[0, rows, cols[hh]] = (
                _rms(a, sg_ref[...]) * (1.0 - LAMBDA_INIT)).astype(BF16)


def _diff_attn(q, k, v, bias, lam, subln_g, heads_per_step):
    b, s, aw = q.shape
    tq = ATTN_BLOCK
    hp = heads_per_step
    spec = pl.BlockSpec((1, s, hp * HEAD_W), lambda bi, h: (bi, 0, h))
    return pl.pallas_call(
        _diff_attn_kernel,
        grid=(b, N_HEADS // hp),
        in_specs=[
            spec, spec, spec,
            pl.BlockSpec((hp, 2, tq, tq), lambda bi, h: (h, 0, 0, 0)),
            pl.BlockSpec((SUBLANES, LANES), lambda bi, h: (0, 0)),
            pl.BlockSpec((1, HEAD_W), lambda bi, h: (0, 0)),
        ],
        out_specs=spec,
        out_shape=jax.ShapeDtypeStruct((b, s, aw), BF16),
        scratch_shapes=[pltpu.VMEM((hp, s, 2 * HEAD_W), BF16)],
        compiler_params=_cparams(2),
        name="diff_attn",
    )(q, k, v, bias, lam, subln_g)


def _mem_fold_kernel(m_ref, g_ref, wkv_ref, wq_ref, wo_ref, qk_ref, vo_ref):
    d = m_ref.shape[2]
    hd = d // X_HEADS
    m = _rms(m_ref[0], g_ref[...]).astype(BF16)
    kv = _dot(m, wkv_ref[...])
    k = kv[:, :d].astype(BF16)
    v = kv[:, d:].astype(BF16)
    for hh in range(X_HEADS):
        sl = slice(hh * hd, (hh + 1) * hd)
        qk_ref[0, :, hh * m.shape[0]:(hh + 1) * m.shape[0]] = (
            _dot_nt(wq_ref[:, sl], k[:, sl]) * (hd ** -0.5)).astype(BF16)
        vo_ref[0, hh * m.shape[0]:(hh + 1) * m.shape[0], :] = _dot(v[:, sl], wo_ref[sl, :]).astype(BF16)


def _mem_fold(mem, norm_g, w_xkv, w_xq, w_xo):
    b, ml, d = mem.shape
    const = lambda *shape: pl.BlockSpec(shape, lambda bi: (0,) * len(shape))
    return pl.pallas_call(
        _mem_fold_kernel,
        grid=(b,),
        in_specs=[pl.BlockSpec((1, ml, d), lambda bi: (bi, 0, 0)), const(1, d),
                  const(d, 2 * d), const(d, d), const(d, d)],
        out_specs=[pl.BlockSpec((1, d, X_HEADS * ml), lambda bi: (bi, 0, 0)),
                   pl.BlockSpec((1, X_HEADS * ml, d), lambda bi: (bi, 0, 0))],
        out_shape=[jax.ShapeDtypeStruct((b, d, X_HEADS * ml), BF16),
                   jax.ShapeDtypeStruct((b, X_HEADS * ml, d), BF16)],
        compiler_params=_cparams(1),
        name="mem_fold",
    )(mem, norm_g, w_xkv, w_xq, w_xo)


def _post_kernel(x_ref, a_ref, g0_ref, gyb_ref, wau_ref, wo_ref, nxg_ref,
                 qk_ref, vo_ref, nfg_ref, wrt_ref, br_ref,
                 x2_ref, hf_ref, eidx_ref, rank_ref, gate_ref, cnt_ref, run_ref):
    ts, d = x_ref.shape
    first = (pl.program_id(0) == 0) & (pl.program_id(1) == 0)

    y_a = _dot(a_ref[...], wau_ref[...])
    merged = g0_ref[...].astype(F32) * y_a + gyb_ref[...].astype(F32)
    x1 = x_ref[...] + _dot(merged.astype(BF16), wo_ref[...])

    ml = qk_ref.shape[2] // X_HEADS
    hx = _rms(x1, nxg_ref[...]).astype(BF16)
    s_all = _dot(hx, qk_ref[0])
    probs = []
    for hh in range(X_HEADS):
        s = s_all[:, hh * ml:(hh + 1) * ml]
        p = jnp.exp(s - jnp.max(s, axis=-1, keepdims=True))
        probs.append((p / jnp.sum(p, axis=-1, keepdims=True)).astype(BF16))
    x2 = x1 + _dot(jnp.concatenate(probs, axis=1), vo_ref[0])
    x2_ref[...] = x2
    hf = _rms(x2, nfg_ref[...])
    hf_ref[...] = _pack_halves(hf)

    logits = _dot_nt(wrt_ref[...], hf, precision=lax.Precision.HIGHEST) + br_ref[...]
    ne = logits.shape[0]
    eid = lax.broadcasted_iota(I32, logits.shape, 0).astype(F32)
    work = logits
    vals, idxs, hots = [], [], []
    for _ in range(TOP_K):
        mx = jnp.max(work, axis=0, keepdims=True)
        idx = jnp.min(jnp.where(work == mx, eid, float(ne)), axis=0, keepdims=True)
        hot = eid == idx
        vals.append(mx)
        idxs.append(idx.astype(I32))
        hots.append(hot)
        work = jnp.where(hot, -jnp.inf, work)
    ex = [jnp.exp(vv - vals[0]) for vv in vals]
    den = ex[0] + ex[1] + ex[2] + ex[3]
    gates = [e_ / den for e_ in ex]

    multi = (hots[0] | hots[1] | hots[2] | hots[3])
    multi_f = jnp.where(multi, 1.0, 0.0).astype(F32)
    tri = (lax.broadcasted_iota(I32, (ts, ts), 0)
           < lax.broadcasted_iota(I32, (ts, ts), 1))
    before = _dot(multi_f.astype(BF16), jnp.where(tri, 1.0, 0.0).astype(BF16))

    @pl.when(first)
    def _():
        run_ref[...] = jnp.zeros(run_ref.shape, F32)

    run = run_ref[...]
    pos = before + run
    ranks = [jnp.sum(jnp.where(hot, pos, 0.0), axis=0, keepdims=True) for hot in hots]
    run_new = run + jnp.sum(multi_f, axis=1, keepdims=True)
    run_ref[...] = run_new
    cnt_ref[...] = jnp.broadcast_to(run_new, cnt_ref.shape).astype(I32)

    eidx_ref[...] = jnp.concatenate(idxs, axis=0)
    rank_ref[...] = jnp.concatenate(ranks, axis=0).astype(I32)
    g_rows = jnp.concatenate(gates + [jnp.zeros((LANES - TOP_K, ts), F32)], axis=0)
    gate_ref[...] = g_rows.T


def _post(x2d, a2d, g0, gyb, w_attn_up, w_out, norm_x_g, qk_fold, vo_fold,
          norm_ffn_g, w_router_t, b_router, batch, ts):
    n, d = x2d.shape
    nt = n // batch // ts
    tok = pl.BlockSpec((ts, d), lambda bi, j: (bi * nt + j, 0))
    lanes = pl.BlockSpec((TOP_K, ts), lambda bi, j: (0, bi * nt + j))
    const = lambda *shape: pl.BlockSpec(shape, lambda bi, j: (0,) * len(shape),
                                        pipeline_mode=pl.Buffered(1))
    fold = lambda arr: pl.BlockSpec((1,) + arr.shape[1:], lambda bi, j: (bi, 0, 0))
    return pl.pallas_call(
        _post_kernel,
        grid=(batch, nt),
        in_specs=[tok, tok, tok, tok, const(d, d), const(d, d), const(1, d),
                  fold(qk_fold), fold(vo_fold), const(1, d), const(N_EXPERTS, d),
                  const(N_EXPERTS, 1)],
        out_specs=[tok, pl.BlockSpec((ts, d // 2), lambda bi, j: (bi * nt + j, 0)),
                   lanes, lanes,
                   pl.BlockSpec((ts, LANES), lambda bi, j: (bi * nt + j, 0)),
                   pl.BlockSpec((N_EXPERTS, LANES), lambda bi, j: (0, 0))],
        out_shape=[jax.ShapeDtypeStruct((n, d), F32),
                   jax.ShapeDtypeStruct((n, d // 2), jnp.uint32),
                   jax.ShapeDtypeStruct((TOP_K, n), I32), jax.ShapeDtypeStruct((TOP_K, n), I32),
                   jax.ShapeDtypeStruct((n, LANES), F32),
                   jax.ShapeDtypeStruct((N_EXPERTS, LANES), I32)],
        scratch_shapes=[pltpu.VMEM((N_EXPERTS, 1), F32)],
        compiler_params=_cparams(2),
        name="post",
    )(x2d, a2d, g0, gyb, w_attn_up, w_out, norm_x_g, qk_fold, vo_fold,
      norm_ffn_g, w_router_t, b_router)


def _route_dest_kernel(pstart_ref, eidx_ref, rank_ref, dest_ref):
    eidx = eidx_ref[...]
    start = jnp.zeros(eidx.shape, I32)
    for e in range(N_EXPERTS):
        start = jnp.where(eidx == e, pstart_ref[e], start)
    dest_ref[...] = start + rank_ref[...]


def _route_dest(pstart, eidx, rank, tl):
    k, n = eidx.shape
    lanes = pl.BlockSpec((k, tl), lambda i: (0, i))
    return pl.pallas_call(
        _route_dest_kernel,
        grid=(n // tl,),
        in_specs=[pl.BlockSpec(memory_space=pltpu.SMEM), lanes, lanes],
        out_specs=lanes,
        out_shape=jax.ShapeDtypeStruct((k, n), I32),
        compiler_params=_cparams(1),
        name="route_dest",
    )(pstart, eidx, rank)


SC_CORES = 2
SC_SUBCORES = 16
SC_WORKERS = SC_CORES * SC_SUBCORES
MOVE_CHUNK = 64
MOVE_BUFFERS = 3


def _worker_index_layout(dest, n):
    per_w = n // SC_WORKERS
    n_chunks = per_w // MOVE_CHUNK
    d4 = dest.reshape(TOP_K, SC_WORKERS, n_chunks, MOVE_CHUNK)
    return jnp.transpose(d4, (1, 2, 0, 3)).reshape(SC_WORKERS, n_chunks * TOP_K, MOVE_CHUNK)


def _sc_mesh():
    return plsc.VectorSubcoreMesh(core_axis_name="c", subcore_axis_name="s",
                                  num_cores=SC_CORES, num_subcores=SC_SUBCORES)


def _sc_dispatch(hf, dest_w, rows):
    n, d = hf.shape
    per_w = n // SC_WORKERS
    n_chunks = per_w // MOVE_CHUNK

    def body(hf_hbm, dest_hbm, xs_hbm, idx_v, rows_v, rsem, wsem):
        wid = lax.axis_index("s") * SC_CORES + lax.axis_index("c")
        base = wid * per_w
        pltpu.sync_copy(dest_hbm.at[wid], idx_v)

        def read(c):
            b = c % MOVE_BUFFERS
            return pltpu.async_copy(hf_hbm.at[pl.ds(base + c * MOVE_CHUNK, MOVE_CHUNK)],
                                    rows_v.at[b], rsem.at[b])

        scatters = [[] for _ in range(MOVE_BUFFERS)]
        pending = read(0)
        for c in range(n_chunks):
            b = c % MOVE_BUFFERS
            pending.wait()
            if c + 1 < n_chunks:
                for cp in scatters[(c + 1) % MOVE_BUFFERS]:
                    cp.wait()
                pending = read(c + 1)
            scatters[b] = [
                pltpu.async_copy(rows_v.at[b], xs_hbm.at[idx_v.at[c * TOP_K + kk]], wsem.at[b])
                for kk in range(TOP_K)]
        for group in scatters:
            for cp in group:
                cp.wait()

    return pl.kernel(
        body,
        out_type=jax.ShapeDtypeStruct((rows, d), hf.dtype),
        mesh=_sc_mesh(),
        scratch_types=[pltpu.VMEM((n_chunks * TOP_K, MOVE_CHUNK), I32),
                       pltpu.VMEM((MOVE_BUFFERS, MOVE_CHUNK, d), hf.dtype),
                       pltpu.SemaphoreType.DMA((MOVE_BUFFERS,)),
                       pltpu.SemaphoreType.DMA((MOVE_BUFFERS,))],
        name="sc_dispatch",
    )(hf, dest_w)


def _sc_gather(ys, dest_w, n):
    _, d = ys.shape
    per_w = n // SC_WORKERS
    n_chunks = per_w // MOVE_CHUNK

    def body(ys_hbm, dest_hbm, yg_hbm, idx_v, rows_v, rsem, wsem):
        wid = lax.axis_index("s") * SC_CORES + lax.axis_index("c")
        base = wid * per_w
        pltpu.sync_copy(dest_hbm.at[wid], idx_v)
        n_moves = n_chunks * TOP_K

        def gather(m):
            b = m % MOVE_BUFFERS
            return pltpu.async_copy(ys_hbm.at[idx_v.at[m]], rows_v.at[b], rsem.at[b])

        def write(m):
            b = m % MOVE_BUFFERS
            c, kk = divmod(m, TOP_K)
            return pltpu.async_copy(
                rows_v.at[b], yg_hbm.at[pl.ds(kk * n + base + c * MOVE_CHUNK, MOVE_CHUNK)],
                wsem.at[b])

        writes = [None] * MOVE_BUFFERS
        pending = gather(0)
        for m in range(n_moves):
            pending.wait()
            if m + 1 < n_moves:
                nb = (m + 1) % MOVE_BUFFERS
                if writes[nb] is not None:
                    writes[nb].wait()
                pending = gather(m + 1)
            writes[m % MOVE_BUFFERS] = write(m)
        for wr in writes:
            if wr is not None:
                wr.wait()

    return pl.kernel(
        body,
        out_type=jax.ShapeDtypeStruct((TOP_K * n, d), ys.dtype),
        mesh=_sc_mesh(),
        scratch_types=[pltpu.VMEM((n_chunks * TOP_K, MOVE_CHUNK), I32),
                       pltpu.VMEM((MOVE_BUFFERS, MOVE_CHUNK, d), ys.dtype),
                       pltpu.SemaphoreType.DMA((MOVE_BUFFERS,)),
                       pltpu.SemaphoreType.DMA((MOVE_BUFFERS,))],
        name="sc_gather",
    )(ys, dest_w)


def _combine_kernel(x2_ref, gate_ref, fg_ref, yg_ref, o_ref):
    g = gate_ref[...]
    x2 = x2_ref[...]
    c = x2.shape[1] // 2
    acc_hi, acc_lo = x2[:, :c], x2[:, c:]
    for kk in range(TOP_K):
        hi, lo = _unpack_halves(yg_ref[kk])
        acc_hi = acc_hi + g[:, kk:kk + 1] * hi
        acc_lo = acc_lo + g[:, kk:kk + 1] * lo
    o_ref[...] = _rms(jnp.concatenate([acc_hi, acc_lo], axis=1), fg_ref[...])


def _combine(x2, gate_tm, final_g, yg, ts):
    n, d = x2.shape
    dw = yg.shape[1]
    return pl.pallas_call(
        _combine_kernel,
        grid=(n // ts,),
        in_specs=[pl.BlockSpec((ts, d), lambda i: (i, 0)),
                  pl.BlockSpec((ts, LANES), lambda i: (i, 0)),
                  pl.BlockSpec((1, d), lambda i: (0, 0)),
                  pl.BlockSpec((TOP_K, ts, dw), lambda i: (0, i, 0))],
        out_specs=pl.BlockSpec((ts, d), lambda i: (i, 0)),
        out_shape=jax.ShapeDtypeStruct((n, d), F32),
        compiler_params=_cparams(1),
        name="combine",
    )(x2, gate_tm, final_g, yg.reshape(TOP_K, n, dw))


def _experts_kernel(be_ref, first_ref, parts_ref, nxt_ref, slot_ref, nb_ref,
                    xs_ref, bgu_ref, bd_ref, wgu_hbm, wd_hbm, ys_ref,
                    wgu_st, wd_st, wgu_bf, wd_bf, sem_w):
    dff = wd_bf.shape[0]
    bm = xs_ref.shape[0] // EXPERT_BLOCKS_PER_STEP

    def fetch(ex, sl):
        return (pltpu.make_async_copy(wgu_hbm.at[ex], wgu_st.at[sl], sem_w.at[0, sl]),
                pltpu.make_async_copy(wd_hbm.at[ex], wd_st.at[sl], sem_w.at[1, sl]))

    @pl.when(pl.program_id(0) == 0)
    def _():
        e0 = be_ref[0]
        for cp in fetch(e0, slot_ref[e0]):
            cp.start()

    for sub in range(EXPERT_BLOCKS_PER_STEP):
        blk = pl.program_id(0) * EXPERT_BLOCKS_PER_STEP + sub
        e = be_ref[blk]
        rows = slice(sub * bm, (sub + 1) * bm)

        @pl.when(first_ref[blk] == 1)
        def _(e=e):
            sl = slot_ref[e]
            for cp in fetch(e, sl):
                cp.wait()
            nx = nxt_ref[e]

            @pl.when(nx >= 0)
            def _():
                for cp in fetch(nx, 1 - sl):
                    cp.start(priority=1)

            wgu_bf[...] = wgu_st[sl].astype(BF16)
            wd_bf[...] = wd_st[sl].astype(BF16)

        def mlp(e, rows):
            xb = jnp.concatenate(_unpack_halves(xs_ref[rows, :]), axis=1).astype(BF16)
            gu = _dot(xb, wgu_bf[...]) + bgu_ref[e]
            x_glu = jnp.minimum(gu[:, :dff], SWIGLU_LIMIT)
            x_lin = jnp.clip(gu[:, dff:], -SWIGLU_LIMIT, SWIGLU_LIMIT)
            act = x_glu * jax.nn.sigmoid(SWIGLU_ALPHA * x_glu) * (x_lin + 1.0)
            ys_ref[rows, :] = _pack_halves(_dot(act.astype(BF16), wd_bf[...]) + bd_ref[e])

        def clear(rows):
            ys_ref[rows, :] = jnp.zeros((rows.stop - rows.start, ys_ref.shape[1]), jnp.uint32)

        part = bm // EXPERT_TAIL_PARTS
        for live in range(EXPERT_TAIL_PARTS + 1):
            @pl.when(parts_ref[blk] == live)
            def _(e=e, rows=rows, live=live):
                cut = rows.start + live * part
                if live:
                    mlp(e, slice(rows.start, cut))
                if live < EXPERT_TAIL_PARTS:
                    clear(slice(cut, rows.stop))


def _expert_schedule(counts, bm, n_blocks):
    ne = counts.shape[0]
    ids = jnp.arange(ne, dtype=I32)
    upto = ids[None, :] <= ids[:, None]
    padded = (counts + bm - 1) // bm * bm
    pend = jnp.sum(jnp.where(upto, padded[None, :], 0), axis=1)
    pstart = (pend - padded).astype(I32)
    nb_used = (jnp.sum(padded) // bm).astype(I32)
    blk = jnp.arange(n_blocks, dtype=I32)
    blk_c = jnp.minimum(blk, nb_used - 1)
    be = jnp.minimum(jnp.sum((pend[None, :] <= (blk_c * bm)[:, None]).astype(I32), axis=1),
                     ne - 1)
    started = jnp.sum(((pstart[None, :] == (blk * bm)[:, None]) & (padded[None, :] > 0))
                      .astype(I32), axis=1)
    first = ((blk < nb_used) & (started > 0)).astype(I32)
    used = counts > 0
    seq = jnp.sum(jnp.where(upto & used[None, :], 1, 0), axis=1) - 1
    later = used[None, :] & (ids[None, :] > ids[:, None])
    nxt = jnp.where(jnp.any(later, axis=1), jnp.argmax(later, axis=1), -1).astype(I32)
    slot = (seq & 1).astype(I32)
    hit = ids[None, :] == be[:, None]
    left = jnp.sum(jnp.where(hit, (pstart + counts)[None, :], 0), axis=1) - blk * bm
    part = bm // EXPERT_TAIL_PARTS
    parts = jnp.where(blk < nb_used,
                      jnp.minimum((left + part - 1) // part, EXPERT_TAIL_PARTS), 0).astype(I32)
    return pstart, nb_used.reshape(1), be, first, parts, nxt, slot


def _experts(be, first, parts, nxt, slot, nb_used, xs, w_gu, b_gu, w_down, b_down, bm):
    rows, dw = xs.shape
    ne, d, dff2 = w_gu.shape
    dff = dff2 // 2
    sb = EXPERT_BLOCKS_PER_STEP * bm
    assert rows % sb == 0
    last = lambda j, be, fi, ha, nx, sl, nb: (
        jnp.maximum(jnp.minimum(j, (nb[0] - 1) // EXPERT_BLOCKS_PER_STEP), 0), 0)
    const = lambda *shape: pl.BlockSpec(shape,
                                        lambda j, be, fi, ha, nx, sl, nb: (0,) * len(shape))
    grid_spec = pltpu.PrefetchScalarGridSpec(
        num_scalar_prefetch=6,
        grid=(rows // sb,),
        in_specs=[
            pl.BlockSpec((sb, dw), last),
            const(ne, 1, dff2),
            const(ne, 1, d),
            pl.BlockSpec(memory_space=pl.ANY),
            pl.BlockSpec(memory_space=pl.ANY),
        ],
        out_specs=pl.BlockSpec((sb, dw), lambda j, be, fi, ha, nx, sl, nb: (j, 0)),
        scratch_shapes=[pltpu.VMEM((2, d, dff2), F32), pltpu.VMEM((2, dff, d), F32),
                        pltpu.VMEM((d, dff2), BF16), pltpu.VMEM((dff, d), BF16),
                        pltpu.SemaphoreType.DMA((2, 2))],
    )
    return pl.pallas_call(
        _experts_kernel,
        grid_spec=grid_spec,
        out_shape=jax.ShapeDtypeStruct((rows, dw), jnp.uint32),
        compiler_params=_cparams(1),
        name="experts",
    )(be, first, parts, nxt, slot, nb_used, xs, b_gu.reshape(ne, 1, dff2),
      b_down.reshape(ne, 1, d), w_gu, w_down)


MIXER_TILE = 512
ATTN_HEADS_PER_STEP = 4
POST_TILE = 512
DEST_TILE = 2048
COMBINE_TILE = 1024
EXPERT_BLOCK = 1024
EXPERT_TAIL_PARTS = 8
EXPERT_BLOCKS_PER_STEP = 1


def kernel(x, mem, norm_mix_g, w_in, lambda_q1, lambda_k1, lambda_q2, lambda_k2, rel_bias,
           subln_g, w_attn_up, pool_mix, pool_scale, w_pool_up, w_gate, b_gate, w_out,
           norm_x_g, norm_mem_g, w_xq, w_xkv, w_xo, norm_ffn_g, w_router, b_router,
           w_gu, b_gu, w_down, b_down, final_norm_g):
    b, s, d = x.shape
    n = b * s
    assert w_in.shape[0] == 1, "single-layer block"
    row = lambda a: a.reshape(1, -1)
    bf = lambda a: a[0].astype(BF16)

    q, k, v, g0, gyb = _mixer_in(
        x, row(norm_mix_g[0]), bf(w_in), bf(w_gate), row(b_gate[0]), bf(pool_mix),
        row(pool_scale[0]), bf(w_pool_up), MIXER_TILE)

    bias, lam = _rel_bias(rel_bias, row(lambda_q1[0]), row(lambda_k1[0]),
                          row(lambda_q2[0]), row(lambda_k2[0]), s)
    a = _diff_attn(q, k, v, bias, lam, row(subln_g[0]), ATTN_HEADS_PER_STEP)

    qk_fold, vo_fold = _mem_fold(mem, row(norm_mem_g[0]), bf(w_xkv), bf(w_xq), bf(w_xo))

    x2, hf, eidx, rank, gate_tm, counts = _post(
        x.reshape(n, d), a.reshape(n, d), g0, gyb, bf(w_attn_up), bf(w_out),
        row(norm_x_g[0]), qk_fold, vo_fold, row(norm_ffn_g[0]),
        w_router[0].T, b_router[0].reshape(-1, 1), b, POST_TILE)

    bm = EXPERT_BLOCK
    rows = n * TOP_K + N_EXPERTS * bm
    pstart, nb_used, be, first, parts, nxt, slot = _expert_schedule(
        counts[:, 0], bm, rows // bm)

    dest_w = _worker_index_layout(_route_dest(pstart, eidx, rank, DEST_TILE), n)
    xs = _sc_dispatch(hf, dest_w, rows)
    ys = _experts(be, first, parts, nxt, slot, nb_used, xs, w_gu[0], b_gu[0], w_down[0],
                  b_down[0], bm)
    yg = _sc_gather(ys, dest_w, n)
    out = _combine(x2, gate_tm, row(final_norm_g), yg, COMBINE_TILE)
    return out.reshape(b, s, d)
```

```python
import functools
import math

import numpy as np
import jax
import jax.numpy as jnp
from jax import lax
from jax.experimental import pallas as pl
from jax.experimental.pallas import tpu as pltpu
from jax.experimental.pallas import tpu_sc as plsc

F32 = jnp.float32
BF16 = jnp.bfloat16
I32 = jnp.int32

EPS = 1e-6
CHUNK = 64
N_HEADS = 8
HEAD_DIM = 64
HEAD_W = 2 * HEAD_DIM
POOL_WINDOWS = (2, 4, 8, 16)
POOL_GROUP = 128
POOL_PAD = 16
N_BUCKETS = 32
MAX_DISTANCE = 128
X_HEADS = 4
N_EXPERTS = 32
TOP_K = 4
SWIGLU_ALPHA = 1.702
SWIGLU_LIMIT = 7.0
LAMBDA_INIT = 0.8 - 0.6 * math.exp(-0.3 * 0)

LANES = 128
SUBLANES = 8
ATTN_BLOCK = 256
VMEM_LIMIT = 56 * 1024 * 1024


def _cparams(n_axes, vmem=VMEM_LIMIT):
    return pltpu.CompilerParams(
        dimension_semantics=("arbitrary",) * n_axes, vmem_limit_bytes=vmem)


def _rms(xf, g):
    ms = jnp.mean(xf * xf, axis=-1, keepdims=True)
    return xf * lax.rsqrt(ms + EPS) * g


def _dot(a, b):
    return jnp.dot(a, b, preferred_element_type=F32)


def _pack_halves(x):
    c = x.shape[1] // 2
    as_bits = lambda v: lax.bitcast_convert_type(v.astype(BF16).astype(F32), jnp.uint32)
    return as_bits(x[:, :c]) | (as_bits(x[:, c:]) >> 16)


def _unpack_halves(w):
    hi = lax.bitcast_convert_type(w & jnp.uint32(0xFFFF0000), F32)
    lo = lax.bitcast_convert_type(w << 16, F32)
    return hi, lo


def _dot_nt(a, b, precision=None):
    return lax.dot_general(a, b, (((1,), (1,)), ((), ())),
                           preferred_element_type=F32, precision=precision)


def _mixer_in_kernel(x_ref, g_ref, wq_ref, wk_ref, wv_ref, wu_ref, wg_ref, bg_ref,
                     pmix_ref, pscale_ref, wpu_ref,
                     q_ref, k_ref, v_ref, g0_ref, gyb_ref, ext_ref, wpool_ref):
    ts = x_ref.shape[1]
    d = x_ref.shape[2]
    j = pl.program_id(1)

    @pl.when((pl.program_id(0) == 0) & (j == 0))
    def _():
        for gi in range(len(POOL_WINDOWS)):
            sl = slice(gi * POOL_GROUP, (gi + 1) * POOL_GROUP)
            mix = (pmix_ref[gi].astype(F32) * pscale_ref[:, sl]).astype(BF16)
            wpool_ref[sl, :] = _dot(mix, wpu_ref[sl, :]).astype(BF16)

    h = _rms(x_ref[0], g_ref[...]).astype(BF16)
    q_ref[0] = (_dot(h, wq_ref[...]) * (HEAD_DIM ** -0.5)).astype(BF16)
    k_ref[0] = _dot(h, wk_ref[...]).astype(BF16)
    v_ref[0] = _dot(h, wv_ref[...]).astype(BF16)
    u = _dot(h, wu_ref[...])

    @pl.when(j == 0)
    def _():
        ext_ref[0:POOL_PAD, :] = jnp.zeros((POOL_PAD, u.shape[1]), F32)

    ext_ref[POOL_PAD:POOL_PAD + ts, :] = u
    e = ext_ref[...]
    sums = {}
    s = e
    w = 1
    while w < max(POOL_WINDOWS):
        s = s + pltpu.roll(s, w, 0)
        w *= 2
        sums[w] = s
    ext_ref[0:POOL_PAD, :] = ext_ref[ts:ts + POOL_PAD, :]

    pos = (j * ts + lax.broadcasted_iota(I32, (ts, 1), 0) + 1).astype(F32)
    pooled = []
    for gi, w in enumerate(POOL_WINDOWS):
        sl = slice(gi * POOL_GROUP, (gi + 1) * POOL_GROUP)
        win = sums[w][POOL_PAD:, sl]
        pooled.append(win / jnp.minimum(pos, float(w)) - u[:, sl])
    y_b = _dot(jnp.concatenate(pooled, axis=1).astype(BF16), wpool_ref[...])

    gate = jax.nn.sigmoid(_dot(h, wg_ref[...]) + bg_ref[...])
    g0_ref[...] = gate[:, :d].astype(BF16)
    gyb_ref[...] = (gate[:, d:] * y_b).astype(BF16)


def _mixer_in(x, norm_g, w_in, w_gate, b_gate, pool_mix, pool_scale, w_pool_up, ts):
    b, s, d = x.shape
    aw = N_HEADS * HEAD_W
    pw = len(POOL_WINDOWS) * POOL_GROUP
    n = b * s
    nt = s // ts
    const = lambda *shape: pl.BlockSpec(shape, lambda bi, j: (0,) * len(shape))
    tok3 = pl.BlockSpec((1, ts, aw), lambda bi, j: (bi, j, 0))
    tok2 = pl.BlockSpec((ts, d), lambda bi, j: (bi * nt + j, 0))
    return pl.pallas_call(
        _mixer_in_kernel,
        grid=(b, nt),
        in_specs=[
            pl.BlockSpec((1, ts, d), lambda bi, j: (bi, j, 0)),
            const(1, d),
            pl.BlockSpec((d, aw), lambda bi, j: (0, 0)),
            pl.BlockSpec((d, aw), lambda bi, j: (0, 1)),
            pl.BlockSpec((d, aw), lambda bi, j: (0, 2)),
            pl.BlockSpec((d, pw), lambda bi, j: (0, 3 * aw // pw)),
            const(d, 2 * d),
            const(1, 2 * d),
            const(len(POOL_WINDOWS), POOL_GROUP, POOL_GROUP),
            const(1, pw),
            const(pw, d),
        ],
        out_specs=[tok3, tok3, tok3, tok2, tok2],
        out_shape=[jax.ShapeDtypeStruct((b, s, aw), BF16)] * 3
        + [jax.ShapeDtypeStruct((n, d), BF16)] * 2,
        scratch_shapes=[pltpu.VMEM((ts + POOL_PAD, pw), F32), pltpu.VMEM((pw, d), BF16)],
        compiler_params=_cparams(2),
        name="mixer_in",
    )(x, norm_g, w_in, w_in, w_in, w_in, w_gate, b_gate, pool_mix, pool_scale, w_pool_up)


def _rel_bucket(rel, log=jnp.log, f32=lambda a: a.astype(jnp.float32),
                i32=lambda a: a.astype(jnp.int32), xp=jnp):
    nb = N_BUCKETS // 2
    ret = i32(rel > 0) * nb
    n = xp.abs(rel)
    max_exact = nb // 2
    nf = f32(xp.maximum(n, 1))
    large = max_exact + i32(log(nf / max_exact) / math.log(MAX_DISTANCE / max_exact)
                            * (nb - max_exact))
    large = xp.minimum(large, nb - 1)
    return ret + xp.where(n < max_exact, n, large)


def _far_bucket(block, seq):
    rel = -np.arange(block + 1, max(seq, block + 2), dtype=np.int32)
    bk = _rel_bucket(rel, log=np.log, f32=lambda a: a.astype(np.float32),
                     i32=lambda a: a.astype(np.int32), xp=np)
    assert (bk == bk[0]).all(), "far keys must share one relative-position bucket"
    return int(bk[0])


def _rel_bias_kernel(far_bucket, tab_ref, bidx_ref, lq1_ref, lk1_ref, lq2_ref, lk2_ref,
                     bias_ref, lam_ref):
    h = pl.program_id(0)
    bidx = bidx_ref[...]
    acc = jnp.zeros(bidx.shape, F32)
    for bkt in range(N_BUCKETS):
        acc = jnp.where(bidx == bkt, tab_ref[bkt, h], acc)
    acc = acc - tab_ref[far_bucket, h]
    bias_ref[0] = jnp.where(bidx < 0, -jnp.inf, acc)
    lam = (jnp.exp(jnp.sum(lq1_ref[...] * lk1_ref[...], keepdims=True))
           - jnp.exp(jnp.sum(lq2_ref[...] * lk2_ref[...], keepdims=True)) + LAMBDA_INIT)
    lam_ref[...] = jnp.broadcast_to(lam, lam_ref.shape)


def _rel_bias(rel_bias, lq1, lk1, lq2, lk2, seq):
    blk = ATTN_BLOCK
    qpos = jnp.arange(blk, dtype=I32)[:, None]
    kpos = jnp.arange(blk, dtype=I32)[None, :]
    diag = jnp.where(kpos // CHUNK <= qpos // CHUNK, _rel_bucket(kpos - qpos), -1)
    prev = _rel_bucket(kpos - (qpos + blk))
    bidx = jnp.stack([diag, prev]).astype(I32)
    vec = pl.BlockSpec((1, HEAD_DIM), lambda h: (0, 0))
    return pl.pallas_call(
        functools.partial(_rel_bias_kernel, _far_bucket(blk, seq)),
        grid=(N_HEADS,),
        in_specs=[
            pl.BlockSpec(memory_space=pltpu.SMEM),
            pl.BlockSpec((2, blk, blk), lambda h: (0, 0, 0)),
            vec, vec, vec, vec,
        ],
        out_specs=[
            pl.BlockSpec((1, 2, blk, blk), lambda h: (h, 0, 0, 0)),
            pl.BlockSpec((SUBLANES, LANES), lambda h: (0, 0)),
        ],
        out_shape=[jax.ShapeDtypeStruct((N_HEADS, 2, blk, blk), F32),
                   jax.ShapeDtypeStruct((SUBLANES, LANES), F32)],
        compiler_params=_cparams(1),
        name="rel_bias",
    )(rel_bias, bidx, lq1, lk1, lq2, lk2)


def _diff_attn_kernel(q_ref, k_ref, v_ref, bias_ref, lam_ref, sg_ref, o_ref, vext_ref):
    s_len = q_ref.shape[1]
    tq = ATTN_BLOCK
    n_heads = q_ref.shape[2] // HEAD_W
    lam = lam_ref[0:1, 0:1]
    lane = lax.broadcasted_iota(I32, (tq, HEAD_W), 1)
    cols = [slice(hh * HEAD_W, (hh + 1) * HEAD_W) for hh in range(n_heads)]
    b_diag, b_prev = [], []
    for hh in range(n_heads):
        vext_ref[hh, :, :HEAD_W] = v_ref[0, :, cols[hh]]
        vext_ref[hh, :, HEAD_W:] = jnp.ones((s_len, HEAD_W), BF16)
        b_diag.append(jnp.concatenate([bias_ref[hh, 0]] * 2, axis=0))
        b_prev.append(jnp.concatenate([bias_ref[hh, 1]] * 2, axis=0))

    for qi in range(s_len // tq):
        n_keys = (qi + 1) * tq
        rows = slice(qi * tq, (qi + 1) * tq)
        for hh in range(n_heads):
            q = q_ref[0, rows, cols[hh]]
            zero = jnp.zeros_like(q)
            qs = jnp.concatenate([jnp.where(lane < HEAD_DIM, q, zero),
                                  jnp.where(lane >= HEAD_DIM, q, zero)], axis=0)
            s = _dot_nt(qs, k_ref[0, :n_keys, cols[hh]])
            pieces = []
            if qi >= 2:
                pieces.append(s[:, :n_keys - 2 * tq])
            if qi >= 1:
                pieces.append(s[:, n_keys - 2 * tq:n_keys - tq] + b_prev[hh])
            pieces.append(s[:, n_keys - tq:] + b_diag[hh])
            s = jnp.concatenate(pieces, axis=1) if len(pieces) > 1 else pieces[0]
            m = jnp.max(s, axis=-1, keepdims=True)
            p = jnp.exp(s - m).astype(BF16)
            acc = _dot(p, vext_ref[hh, :n_keys, :])
            o = acc[:, :HEAD_W] / acc[:, HEAD_W:HEAD_W + 1]
            a = o[:tq] - lam * o[tq:]
            o_ref[0, rows, cols[hh]] = (
                _rms(a, sg_ref[...]) * (1.0 - LAMBDA_INIT)).astype(BF16)


def _diff_attn(q, k, v, bias, lam, subln_g, heads_per_step):
    b, s, aw = q.shape
    tq = ATTN_BLOCK
    hp = heads_per_step
    spec = pl.BlockSpec((1, s, hp * HEAD_W), lambda bi, h: (bi, 0, h))
    return pl.pallas_call(
        _diff_attn_kernel,
        grid=(b, N_HEADS // hp),
        in_specs=[
            spec, spec, spec,
            pl.BlockSpec((hp, 2, tq, tq), lambda bi, h: (h, 0, 0, 0)),
            pl.BlockSpec((SUBLANES, LANES), lambda bi, h: (0, 0)),
            pl.BlockSpec((1, HEAD_W), lambda bi, h: (0, 0)),
        ],
        out_specs=spec,
        out_shape=jax.ShapeDtypeStruct((b, s, aw), BF16),
        scratch_shapes=[pltpu.VMEM((hp, s, 2 * HEAD_W), BF16)],
        compiler_params=_cparams(2),
        name="diff_attn",
    )(q, k, v, bias, lam, subln_g)


def _mem_fold_kernel(m_ref, g_ref, wkv_ref, wq_ref, wo_ref, qk_ref, vo_ref):
    d = m_ref.shape[2]
    hd = d // X_HEADS
    m = _rms(m_ref[0], g_ref[...]).astype(BF16)
    kv = _dot(m, wkv_ref[...])
    k = kv[:, :d].astype(BF16)
    v = kv[:, d:].astype(BF16)
    for hh in range(X_HEADS):
        sl = slice(hh * hd, (hh + 1) * hd)
        qk_ref[0, :, hh * m.shape[0]:(hh + 1) * m.shape[0]] = (
            _dot_nt(wq_ref[:, sl], k[:, sl]) * (hd ** -0.5)).astype(BF16)
        vo_ref[0, hh * m.shape[0]:(hh + 1) * m.shape[0], :] = _dot(v[:, sl], wo_ref[sl, :]).astype(BF16)


def _mem_fold(mem, norm_g, w_xkv, w_xq, w_xo):
    b, ml, d = mem.shape
    const = lambda *shape: pl.BlockSpec(shape, lambda bi: (0,) * len(shape))
    return pl.pallas_call(
        _mem_fold_kernel,
        grid=(b,),
        in_specs=[pl.BlockSpec((1, ml, d), lambda bi: (bi, 0, 0)), const(1, d),
                  const(d, 2 * d), const(d, d), const(d, d)],
        out_specs=[pl.BlockSpec((1, d, X_HEADS * ml), lambda bi: (bi, 0, 0)),
                   pl.BlockSpec((1, X_HEADS * ml, d), lambda bi: (bi, 0, 0))],
        out_shape=[jax.ShapeDtypeStruct((b, d, X_HEADS * ml), BF16),
                   jax.ShapeDtypeStruct((b, X_HEADS * ml, d), BF16)],
        compiler_params=_cparams(1),
        name="mem_fold",
    )(mem, norm_g, w_xkv, w_xq, w_xo)


def _post_kernel(x_ref, a_ref, g0_ref, gyb_ref, wau_ref, wo_ref, nxg_ref,
                 qk_ref, vo_ref, nfg_ref, wrt_ref, br_ref,
                 x2_ref, hf_ref, eidx_ref, rank_ref, gate_ref, cnt_ref, run_ref):
    ts, d = x_ref.shape
    first = (pl.program_id(0) == 0) & (pl.program_id(1) == 0)

    y_a = _dot(a_ref[...], wau_ref[...])
    merged = g0_ref[...].astype(F32) * y_a + gyb_ref[...].astype(F32)
    x1 = x_ref[...] + _dot(merged.astype(BF16), wo_ref[...])

    ml = qk_ref.shape[2] // X_HEADS
    hx = _rms(x1, nxg_ref[...]).astype(BF16)
    s_all = _dot(hx, qk_ref[0])
    probs = []
    for hh in range(X_HEADS):
        s = s_all[:, hh * ml:(hh + 1) * ml]
        p = jnp.exp(s - jnp.max(s, axis=-1, keepdims=True))
        probs.append((p / jnp.sum(p, axis=-1, keepdims=True)).astype(BF16))
    x2 = x1 + _dot(jnp.concatenate(probs, axis=1), vo_ref[0])
    x2_ref[...] = x2
    hf = _rms(x2, nfg_ref[...])
    hf_ref[...] = _pack_halves(hf)

    logits = _dot_nt(wrt_ref[...], hf, precision=lax.Precision.HIGHEST) + br_ref[...]
    ne = logits.shape[0]
    eid = lax.broadcasted_iota(I32, logits.shape, 0).astype(F32)
    work = logits
    vals, idxs, hots = [], [], []
    for _ in range(TOP_K):
        mx = jnp.max(work, axis=0, keepdims=True)
        idx = jnp.min(jnp.where(work == mx, eid, float(ne)), axis=0, keepdims=True)
        hot = eid == idx
        vals.append(mx)
        idxs.append(idx.astype(I32))
        hots.append(hot)
        work = jnp.where(hot, -jnp.inf, work)
    ex = [jnp.exp(vv - vals[0]) for vv in vals]
    den = ex[0] + ex[1] + ex[2] + ex[3]
    gates = [e_ / den for e_ in ex]

    multi = (hots[0] | hots[1] | hots[2] | hots[3])
    multi_f = jnp.where(multi, 1.0, 0.0).astype(F32)
    tri = (lax.broadcasted_iota(I32, (ts, ts), 0)
           < lax.broadcasted_iota(I32, (ts, ts), 1))
    before = _dot(multi_f.astype(BF16), jnp.where(tri, 1.0, 0.0).astype(BF16))

    @pl.when(first)
    def _():
        run_ref[...] = jnp.zeros(run_ref.shape, F32)

    run = run_ref[...]
    pos = before + run
    ranks = [jnp.sum(jnp.where(hot, pos, 0.0), axis=0, keepdims=True) for hot in hots]
    run_new = run + jnp.sum(multi_f, axis=1, keepdims=True)
    run_ref[...] = run_new
    cnt_ref[...] = jnp.broadcast_to(run_new, cnt_ref.shape).astype(I32)

    eidx_ref[...] = jnp.concatenate(idxs, axis=0)
    rank_ref[...] = jnp.concatenate(ranks, axis=0).astype(I32)
    g_rows = jnp.concatenate(gates + [jnp.zeros((LANES - TOP_K, ts), F32)], axis=0)
    gate_ref[...] = g_rows.T


def _post(x2d, a2d, g0, gyb, w_attn_up, w_out, norm_x_g, qk_fold, vo_fold,
          norm_ffn_g, w_router_t, b_router, batch, ts):
    n, d = x2d.shape
    nt = n // batch // ts
    tok = pl.BlockSpec((ts, d), lambda bi, j: (bi * nt + j, 0))
    lanes = pl.BlockSpec((TOP_K, ts), lambda bi, j: (0, bi * nt + j))
    const = lambda *shape: pl.BlockSpec(shape, lambda bi, j: (0,) * len(shape),
                                        pipeline_mode=pl.Buffered(1))
    fold = lambda arr: pl.BlockSpec((1,) + arr.shape[1:], lambda bi, j: (bi, 0, 0))
    return pl.pallas_call(
        _post_kernel,
        grid=(batch, nt),
        in_specs=[tok, tok, tok, tok, const(d, d), const(d, d), const(1, d),
                  fold(qk_fold), fold(vo_fold), const(1, d), const(N_EXPERTS, d),
                  const(N_EXPERTS, 1)],
        out_specs=[tok, pl.BlockSpec((ts, d // 2), lambda bi, j: (bi * nt + j, 0)),
                   lanes, lanes,
                   pl.BlockSpec((ts, LANES), lambda bi, j: (bi * nt + j, 0)),
                   pl.BlockSpec((N_EXPERTS, LANES), lambda bi, j: (0, 0))],
        out_shape=[jax.ShapeDtypeStruct((n, d), F32),
                   jax.ShapeDtypeStruct((n, d // 2), jnp.uint32),
                   jax.ShapeDtypeStruct((TOP_K, n), I32), jax.ShapeDtypeStruct((TOP_K, n), I32),
                   jax.ShapeDtypeStruct((n, LANES), F32),
                   jax.ShapeDtypeStruct((N_EXPERTS, LANES), I32)],
        scratch_shapes=[pltpu.VMEM((N_EXPERTS, 1), F32)],
        compiler_params=_cparams(2),
        name="post",
    )(x2d, a2d, g0, gyb, w_attn_up, w_out, norm_x_g, qk_fold, vo_fold,
      norm_ffn_g, w_router_t, b_router)


def _route_dest_kernel(pstart_ref, eidx_ref, rank_ref, dest_ref):
    eidx = eidx_ref[...]
    start = jnp.zeros(eidx.shape, I32)
    for e in range(N_EXPERTS):
        start = jnp.where(eidx == e, pstart_ref[e], start)
    dest_ref[...] = start + rank_ref[...]


def _route_dest(pstart, eidx, rank, tl):
    k, n = eidx.shape
    lanes = pl.BlockSpec((k, tl), lambda i: (0, i))
    return pl.pallas_call(
        _route_dest_kernel,
        grid=(n // tl,),
        in_specs=[pl.BlockSpec(memory_space=pltpu.SMEM), lanes, lanes],
        out_specs=lanes,
        out_shape=jax.ShapeDtypeStruct((k, n), I32),
        compiler_params=_cparams(1),
        name="route_dest",
    )(pstart, eidx, rank)


SC_CORES = 2
SC_SUBCORES = 16
SC_WORKERS = SC_CORES * SC_SUBCORES
MOVE_CHUNK = 64
MOVE_BUFFERS = 3


def _worker_index_layout(dest, n):
    per_w = n // SC_WORKERS
    n_chunks = per_w // MOVE_CHUNK
    d4 = dest.reshape(TOP_K, SC_WORKERS, n_chunks, MOVE_CHUNK)
    return jnp.transpose(d4, (1, 2, 0, 3)).reshape(SC_WORKERS, n_chunks * TOP_K, MOVE_CHUNK)


def _sc_mesh():
    return plsc.VectorSubcoreMesh(core_axis_name="c", subcore_axis_name="s",
                                  num_cores=SC_CORES, num_subcores=SC_SUBCORES)


def _sc_dispatch(hf, dest_w, rows):
    n, d = hf.shape
    per_w = n // SC_WORKERS
    n_chunks = per_w // MOVE_CHUNK

    def body(hf_hbm, dest_hbm, xs_hbm, idx_v, rows_v, rsem, wsem):
        wid = lax.axis_index("s") * SC_CORES + lax.axis_index("c")
        base = wid * per_w
        pltpu.sync_copy(dest_hbm.at[wid], idx_v)

        def read(c):
            b = c % MOVE_BUFFERS
            return pltpu.async_copy(hf_hbm.at[pl.ds(base + c * MOVE_CHUNK, MOVE_CHUNK)],
                                    rows_v.at[b], rsem.at[b])

        scatters = [[] for _ in range(MOVE_BUFFERS)]
        pending = read(0)
        for c in range(n_chunks):
            b = c % MOVE_BUFFERS
            pending.wait()
            if c + 1 < n_chunks:
                for cp in scatters[(c + 1) % MOVE_BUFFERS]:
                    cp.wait()
                pending = read(c + 1)
            scatters[b] = [
                pltpu.async_copy(rows_v.at[b], xs_hbm.at[idx_v.at[c * TOP_K + kk]], wsem.at[b])
                for kk in range(TOP_K)]
        for group in scatters:
            for cp in group:
                cp.wait()

    return pl.kernel(
        body,
        out_type=jax.ShapeDtypeStruct((rows, d), hf.dtype),
        mesh=_sc_mesh(),
        scratch_types=[pltpu.VMEM((n_chunks * TOP_K, MOVE_CHUNK), I32),
                       pltpu.VMEM((MOVE_BUFFERS, MOVE_CHUNK, d), hf.dtype),
                       pltpu.SemaphoreType.DMA((MOVE_BUFFERS,)),
                       pltpu.SemaphoreType.DMA((MOVE_BUFFERS,))],
        name="sc_dispatch",
    )(hf, dest_w)


def _sc_gather(ys, dest_w, n):
    _, d = ys.shape
    per_w = n // SC_WORKERS
    n_chunks = per_w // MOVE_CHUNK

    def body(ys_hbm, dest_hbm, yg_hbm, idx_v, rows_v, rsem, wsem):
        wid = lax.axis_index("s") * SC_CORES + lax.axis_index("c")
        base = wid * per_w
        pltpu.sync_copy(dest_hbm.at[wid], idx_v)
        n_moves = n_chunks * TOP_K

        def gather(m):
            b = m % MOVE_BUFFERS
            return pltpu.async_copy(ys_hbm.at[idx_v.at[m]], rows_v.at[b], rsem.at[b])

        def write(m):
            b = m % MOVE_BUFFERS
            c, kk = divmod(m, TOP_K)
            return pltpu.async_copy(
                rows_v.at[b], yg_hbm.at[pl.ds(kk * n + base + c * MOVE_CHUNK, MOVE_CHUNK)],
                wsem.at[b])

        writes = [None] * MOVE_BUFFERS
        pending = gather(0)
        for m in range(n_moves):
            pending.wait()
            if m + 1 < n_moves:
                nb = (m + 1) % MOVE_BUFFERS
                if writes[nb] is not None:
                    writes[nb].wait()
                pending = gather(m + 1)
            writes[m % MOVE_BUFFERS] = write(m)
        for wr in writes:
            if wr is not None:
                wr.wait()

    return pl.kernel(
        body,
        out_type=jax.ShapeDtypeStruct((TOP_K * n, d), ys.dtype),
        mesh=_sc_mesh(),
        scratch_types=[pltpu.VMEM((n_chunks * TOP_K, MOVE_CHUNK), I32),
                       pltpu.VMEM((MOVE_BUFFERS, MOVE_CHUNK, d), ys.dtype),
                       pltpu.SemaphoreType.DMA((MOVE_BUFFERS,)),
                       pltpu.SemaphoreType.DMA((MOVE_BUFFERS,))],
        name="sc_gather",
    )(ys, dest_w)


def _combine_kernel(x2_ref, gate_ref, fg_ref, yg_ref, o_ref):
    g = gate_ref[...]
    x2 = x2_ref[...]
    c = x2.shape[1] // 2
    acc_hi, acc_lo = x2[:, :c], x2[:, c:]
    for kk in range(TOP_K):
        hi, lo = _unpack_halves(yg_ref[kk])
        acc_hi = acc_hi + g[:, kk:kk + 1] * hi
        acc_lo = acc_lo + g[:, kk:kk + 1] * lo
    o_ref[...] = _rms(jnp.concatenate([acc_hi, acc_lo], axis=1), fg_ref[...])


def _combine(x2, gate_tm, final_g, yg, ts):
    n, d = x2.shape
    dw = yg.shape[1]
    return pl.pallas_call(
        _combine_kernel,
        grid=(n // ts,),
        in_specs=[pl.BlockSpec((ts, d), lambda i: (i, 0)),
                  pl.BlockSpec((ts, LANES), lambda i: (i, 0)),
                  pl.BlockSpec((1, d), lambda i: (0, 0)),
                  pl.BlockSpec((TOP_K, ts, dw), lambda i: (0, i, 0))],
        out_specs=pl.BlockSpec((ts, d), lambda i: (i, 0)),
        out_shape=jax.ShapeDtypeStruct((n, d), F32),
        compiler_params=_cparams(1),
        name="combine",
    )(x2, gate_tm, final_g, yg.reshape(TOP_K, n, dw))


def _experts_kernel(be_ref, first_ref, parts_ref, nxt_ref, slot_ref, nb_ref,
                    xs_ref, bgu_ref, bd_ref, wgu_hbm, wd_hbm, ys_ref,
                    wgu_st, wd_st, wgu_bf, wd_bf, sem_w):
    dff = wd_bf.shape[0]
    bm = xs_ref.shape[0] // EXPERT_BLOCKS_PER_STEP

    def fetch(ex, sl):
        return (pltpu.make_async_copy(wgu_hbm.at[ex], wgu_st.at[sl], sem_w.at[0, sl]),
                pltpu.make_async_copy(wd_hbm.at[ex], wd_st.at[sl], sem_w.at[1, sl]))

    @pl.when(pl.program_id(0) == 0)
    def _():
        e0 = be_ref[0]
        for cp in fetch(e0, slot_ref[e0]):
            cp.start()

    for sub in range(EXPERT_BLOCKS_PER_STEP):
        blk = pl.program_id(0) * EXPERT_BLOCKS_PER_STEP + sub
        e = be_ref[blk]
        rows = slice(sub * bm, (sub + 1) * bm)

        @pl.when(first_ref[blk] == 1)
        def _(e=e):
            sl = slot_ref[e]
            for cp in fetch(e, sl):
                cp.wait()
            nx = nxt_ref[e]

            @pl.when(nx >= 0)
            def _():
                for cp in fetch(nx, 1 - sl):
                    cp.start(priority=1)

            wgu_bf[...] = wgu_st[sl].astype(BF16)
            wd_bf[...] = wd_st[sl].astype(BF16)

        def mlp(e, rows):
            xb = jnp.concatenate(_unpack_halves(xs_ref[rows, :]), axis=1).astype(BF16)
            gu = _dot(xb, wgu_bf[...]) + bgu_ref[e]
            x_glu = jnp.minimum(gu[:, :dff], SWIGLU_LIMIT)
            x_lin = jnp.clip(gu[:, dff:], -SWIGLU_LIMIT, SWIGLU_LIMIT)
            act = x_glu * jax.nn.sigmoid(SWIGLU_ALPHA * x_glu) * (x_lin + 1.0)
            ys_ref[rows, :] = _pack_halves(_dot(act.astype(BF16), wd_bf[...]) + bd_ref[e])

        def clear(rows):
            ys_ref[rows, :] = jnp.zeros((rows.stop - rows.start, ys_ref.shape[1]), jnp.uint32)

        part = bm // EXPERT_TAIL_PARTS
        for live in range(EXPERT_TAIL_PARTS + 1):
            @pl.when(parts_ref[blk] == live)
            def _(e=e, rows=rows, live=live):
                cut = rows.start + live * part
                if live:
                    mlp(e, slice(rows.start, cut))
                if live < EXPERT_TAIL_PARTS:
                    clear(slice(cut, rows.stop))


def _expert_schedule(counts, bm, n_blocks):
    ne = counts.shape[0]
    ids = jnp.arange(ne, dtype=I32)
    upto = ids[None, :] <= ids[:, None]
    padded = (counts + bm - 1) // bm * bm
    pend = jnp.sum(jnp.where(upto, padded[None, :], 0), axis=1)
    pstart = (pend - padded).astype(I32)
    nb_used = (jnp.sum(padded) // bm).astype(I32)
    blk = jnp.arange(n_blocks, dtype=I32)
    blk_c = jnp.minimum(blk, nb_used - 1)
    be = jnp.minimum(jnp.sum((pend[None, :] <= (blk_c * bm)[:, None]).astype(I32), axis=1),
                     ne - 1)
    started = jnp.sum(((pstart[None, :] == (blk * bm)[:, None]) & (padded[None, :] > 0))
                      .astype(I32), axis=1)
    first = ((blk < nb_used) & (started > 0)).astype(I32)
    used = counts > 0
    seq = jnp.sum(jnp.where(upto & used[None, :], 1, 0), axis=1) - 1
    later = used[None, :] & (ids[None, :] > ids[:, None])
    nxt = jnp.where(jnp.any(later, axis=1), jnp.argmax(later, axis=1), -1).astype(I32)
    slot = (seq & 1).astype(I32)
    hit = ids[None, :] == be[:, None]
    left = jnp.sum(jnp.where(hit, (pstart + counts)[None, :], 0), axis=1) - blk * bm
    part = bm // EXPERT_TAIL_PARTS
    parts = jnp.where(blk < nb_used,
                      jnp.minimum((left + part - 1) // part, EXPERT_TAIL_PARTS), 0).astype(I32)
    return pstart, nb_used.reshape(1), be, first, parts, nxt, slot


def _experts(be, first, parts, nxt, slot, nb_used, xs, w_gu, b_gu, w_down, b_down, bm):
    rows, dw = xs.shape
    ne, d, dff2 = w_gu.shape
    dff = dff2 // 2
    sb = EXPERT_BLOCKS_PER_STEP * bm
    assert rows % sb == 0
    last = lambda j, be, fi, ha, nx, sl, nb: (
        jnp.maximum(jnp.minimum(j, (nb[0] - 1) // EXPERT_BLOCKS_PER_STEP), 0), 0)
    const = lambda *shape: pl.BlockSpec(shape,
                                        lambda j, be, fi, ha, nx, sl, nb: (0,) * len(shape))
    grid_spec = pltpu.PrefetchScalarGridSpec(
        num_scalar_prefetch=6,
        grid=(rows // sb,),
        in_specs=[
            pl.BlockSpec((sb, dw), last),
            const(ne, 1, dff2),
            const(ne, 1, d),
            pl.BlockSpec(memory_space=pl.ANY),
            pl.BlockSpec(memory_space=pl.ANY),
        ],
        out_specs=pl.BlockSpec((sb, dw), lambda j, be, fi, ha, nx, sl, nb: (j, 0)),
        scratch_shapes=[pltpu.VMEM((2, d, dff2), F32), pltpu.VMEM((2, dff, d), F32),
                        pltpu.VMEM((d, dff2), BF16), pltpu.VMEM((dff, d), BF16),
                        pltpu.SemaphoreType.DMA((2, 2))],
    )
    return pl.pallas_call(
        _experts_kernel,
        grid_spec=grid_spec,
        out_shape=jax.ShapeDtypeStruct((rows, dw), jnp.uint32),
        compiler_params=_cparams(1),
        name="experts",
    )(be, first, parts, nxt, slot, nb_used, xs, b_gu.reshape(ne, 1, dff2),
      b_down.reshape(ne, 1, d), w_gu, w_down)


MIXER_TILE = 512
ATTN_HEADS_PER_STEP = 4
POST_TILE = 512
DEST_TILE = 2048
COMBINE_TILE = 1024
EXPERT_BLOCK = 1024
EXPERT_TAIL_PARTS = 4
EXPERT_BLOCKS_PER_STEP = 1


def kernel(x, mem, norm_mix_g, w_in, lambda_q1, lambda_k1, lambda_q2, lambda_k2, rel_bias,
           subln_g, w_attn_up, pool_mix, pool_scale, w_pool_up, w_gate, b_gate, w_out,
           norm_x_g, norm_mem_g, w_xq, w_xkv, w_xo, norm_ffn_g, w_router, b_router,
           w_gu, b_gu, w_down, b_down, final_norm_g):
    b, s, d = x.shape
    n = b * s
    assert w_in.shape[0] == 1, "single-layer block"
    row = lambda a: a.reshape(1, -1)
    bf = lambda a: a[0].astype(BF16)

    q, k, v, g0, gyb = _mixer_in(
        x, row(norm_mix_g[0]), bf(w_in), bf(w_gate), row(b_gate[0]), bf(pool_mix),
        row(pool_scale[0]), bf(w_pool_up), MIXER_TILE)

    bias, lam = _rel_bias(rel_bias, row(lambda_q1[0]), row(lambda_k1[0]),
                          row(lambda_q2[0]), row(lambda_k2[0]), s)
    a = _diff_attn(q, k, v, bias, lam, row(subln_g[0]), ATTN_HEADS_PER_STEP)

    qk_fold, vo_fold = _mem_fold(mem, row(norm_mem_g[0]), bf(w_xkv), bf(w_xq), bf(w_xo))

    x2, hf, eidx, rank, gate_tm, counts = _post(
        x.reshape(n, d), a.reshape(n, d), g0, gyb, bf(w_attn_up), bf(w_out),
        row(norm_x_g[0]), qk_fold, vo_fold, row(norm_ffn_g[0]),
        w_router[0].T, b_router[0].reshape(-1, 1), b, POST_TILE)

    bm = EXPERT_BLOCK
    rows = n * TOP_K + N_EXPERTS * bm
    pstart, nb_used, be, first, parts, nxt, slot = _expert_schedule(
        counts[:, 0], bm, rows // bm)

    dest_w = _worker_index_layout(_route_dest(pstart, eidx, rank, DEST_TILE), n)
    xs = _sc_dispatch(hf, dest_w, rows)
    ys = _experts(be, first, parts, nxt, slot, nb_used, xs, w_gu[0], b_gu[0], w_down[0],
                  b_down[0], bm)
    yg = _sc_gather(ys, dest_w, n)
    out = _combine(x2, gate_tm, row(final_norm_g), yg, COMBINE_TILE)
    return out.reshape(b, s, d)
```

```python
import functools
import math

import numpy as np
import jax
import jax.numpy as jnp
from jax import lax
from jax.experimental import pallas as pl
from jax.experimental.pallas import tpu as pltpu
from jax.experimental.pallas import tpu_sc as plsc

F32 = jnp.float32
BF16 = jnp.bfloat16
I32 = jnp.int32

EPS = 1e-6
CHUNK = 64
N_HEADS = 8
HEAD_DIM = 64
HEAD_W = 2 * HEAD_DIM
POOL_WINDOWS = (2, 4, 8, 16)
POOL_GROUP = 128
POOL_PAD = 16
N_BUCKETS = 32
MAX_DISTANCE = 128
X_HEADS = 4
N_EXPERTS = 32
TOP_K = 4
SWIGLU_ALPHA = 1.702
SWIGLU_LIMIT = 7.0
LAMBDA_INIT = 0.8 - 0.6 * math.exp(-0.3 * 0)

LANES = 128
SUBLANES = 8
ATTN_BLOCK = 256
VMEM_LIMIT = 56 * 1024 * 1024


def _cparams(n_axes, vmem=VMEM_LIMIT):
    return pltpu.CompilerParams(
        dimension_semantics=("arbitrary",) * n_axes, vmem_limit_bytes=vmem)


def _rms(xf, g):
    ms = jnp.mean(xf * xf, axis=-1, keepdims=True)
    return xf * lax.rsqrt(ms + EPS) * g


def _dot(a, b):
    return jnp.dot(a, b, preferred_element_type=F32)


def _pack_halves(x):
    c = x.shape[1] // 2
    as_bits = lambda v: lax.bitcast_convert_type(v.astype(BF16).astype(F32), jnp.uint32)
    return as_bits(x[:, :c]) | (as_bits(x[:, c:]) >> 16)


def _unpack_halves(w):
    hi = lax.bitcast_convert_type(w & jnp.uint32(0xFFFF0000), F32)
    lo = lax.bitcast_convert_type(w << 16, F32)
    return hi, lo


def _dot_nt(a, b, precision=None):
    return lax.dot_general(a, b, (((1,), (1,)), ((), ())),
                           preferred_element_type=F32, precision=precision)


def _mixer_in_kernel(x_ref, g_ref, wq_ref, wk_ref, wv_ref, wu_ref, wg_ref, bg_ref,
                     pmix_ref, pscale_ref, wpu_ref,
                     q_ref, k_ref, v_ref, g0_ref, gyb_ref, ext_ref, wpool_ref):
    ts = x_ref.shape[1]
    d = x_ref.shape[2]
    j = pl.program_id(1)

    @pl.when((pl.program_id(0) == 0) & (j == 0))
    def _():
        for gi in range(len(POOL_WINDOWS)):
            sl = slice(gi * POOL_GROUP, (gi + 1) * POOL_GROUP)
            mix = (pmix_ref[gi].astype(F32) * pscale_ref[:, sl]).astype(BF16)
            wpool_ref[sl, :] = _dot(mix, wpu_ref[sl, :]).astype(BF16)

    h = _rms(x_ref[0], g_ref[...]).astype(BF16)
    q_ref[0] = (_dot(h, wq_ref[...]) * (HEAD_DIM ** -0.5)).astype(BF16)
    k_ref[0] = _dot(h, wk_ref[...]).astype(BF16)
    v_ref[0] = _dot(h, wv_ref[...]).astype(BF16)
    u = _dot(h, wu_ref[...])

    @pl.when(j == 0)
    def _():
        ext_ref[0:POOL_PAD, :] = jnp.zeros((POOL_PAD, u.shape[1]), F32)

    ext_ref[POOL_PAD:POOL_PAD + ts, :] = u
    e = ext_ref[...]
    sums = {}
    s = e
    w = 1
    while w < max(POOL_WINDOWS):
        s = s + pltpu.roll(s, w, 0)
        w *= 2
        sums[w] = s
    ext_ref[0:POOL_PAD, :] = ext_ref[ts:ts + POOL_PAD, :]

    pos = (j * ts + lax.broadcasted_iota(I32, (ts, 1), 0) + 1).astype(F32)
    pooled = []
    for gi, w in enumerate(POOL_WINDOWS):
        sl = slice(gi * POOL_GROUP, (gi + 1) * POOL_GROUP)
        win = sums[w][POOL_PAD:, sl]
        pooled.append(win / jnp.minimum(pos, float(w)) - u[:, sl])
    y_b = _dot(jnp.concatenate(pooled, axis=1).astype(BF16), wpool_ref[...])

    gate = jax.nn.sigmoid(_dot(h, wg_ref[...]) + bg_ref[...])
    g0_ref[...] = gate[:, :d].astype(BF16)
    gyb_ref[...] = (gate[:, d:] * y_b).astype(BF16)


def _mixer_in(x, norm_g, w_in, w_gate, b_gate, pool_mix, pool_scale, w_pool_up, ts):
    b, s, d = x.shape
    aw = N_HEADS * HEAD_W
    pw = len(POOL_WINDOWS) * POOL_GROUP
    n = b * s
    nt = s // ts
    const = lambda *shape: pl.BlockSpec(shape, lambda bi, j: (0,) * len(shape))
    tok3 = pl.BlockSpec((1, ts, aw), lambda bi, j: (bi, j, 0))
    tok2 = pl.BlockSpec((ts, d), lambda bi, j: (bi * nt + j, 0))
    return pl.pallas_call(
        _mixer_in_kernel,
        grid=(b, nt),
        in_specs=[
            pl.BlockSpec((1, ts, d), lambda bi, j: (bi, j, 0)),
            const(1, d),
            pl.BlockSpec((d, aw), lambda bi, j: (0, 0)),
            pl.BlockSpec((d, aw), lambda bi, j: (0, 1)),
            pl.BlockSpec((d, aw), lambda bi, j: (0, 2)),
            pl.BlockSpec((d, pw), lambda bi, j: (0, 3 * aw // pw)),
            const(d, 2 * d),
            const(1, 2 * d),
            const(len(POOL_WINDOWS), POOL_GROUP, POOL_GROUP),
            const(1, pw),
            const(pw, d),
        ],
        out_specs=[tok3, tok3, tok3, tok2, tok2],
        out_shape=[jax.ShapeDtypeStruct((b, s, aw), BF16)] * 3
        + [jax.ShapeDtypeStruct((n, d), BF16)] * 2,
        scratch_shapes=[pltpu.VMEM((ts + POOL_PAD, pw), F32), pltpu.VMEM((pw, d), BF16)],
        compiler_params=_cparams(2),
        name="mixer_in",
    )(x, norm_g, w_in, w_in, w_in, w_in, w_gate, b_gate, pool_mix, pool_scale, w_pool_up)


def _rel_bucket(rel, log=jnp.log, f32=lambda a: a.astype(jnp.float32),
                i32=lambda a: a.astype(jnp.int32), xp=jnp):
    nb = N_BUCKETS // 2
    ret = i32(rel > 0) * nb
    n = xp.abs(rel)
    max_exact = nb // 2
    nf = f32(xp.maximum(n, 1))
    large = max_exact + i32(log(nf / max_exact) / math.log(MAX_DISTANCE / max_exact)
                            * (nb - max_exact))
    large = xp.minimum(large, nb - 1)
    return ret + xp.where(n < max_exact, n, large)


def _far_bucket(block, seq):
    rel = -np.arange(block + 1, max(seq, block + 2), dtype=np.int32)
    bk = _rel_bucket(rel, log=np.log, f32=lambda a: a.astype(np.float32),
                     i32=lambda a: a.astype(np.int32), xp=np)
    assert (bk == bk[0]).all(), "far keys must share one relative-position bucket"
    return int(bk[0])


def _rel_bias_kernel(far_bucket, tab_ref, bidx_ref, lq1_ref, lk1_ref, lq2_ref, lk2_ref,
                     bias_ref, lam_ref):
    h = pl.program_id(0)
    bidx = bidx_ref[...]
    acc = jnp.zeros(bidx.shape, F32)
    for bkt in range(N_BUCKETS):
        acc = jnp.where(bidx == bkt, tab_ref[bkt, h], acc)
    acc = acc - tab_ref[far_bucket, h]
    bias_ref[0] = jnp.where(bidx < 0, -jnp.inf, acc)
    lam = (jnp.exp(jnp.sum(lq1_ref[...] * lk1_ref[...], keepdims=True))
           - jnp.exp(jnp.sum(lq2_ref[...] * lk2_ref[...], keepdims=True)) + LAMBDA_INIT)
    lam_ref[...] = jnp.broadcast_to(lam, lam_ref.shape)


def _rel_bias(rel_bias, lq1, lk1, lq2, lk2, seq):
    blk = ATTN_BLOCK
    qpos = jnp.arange(blk, dtype=I32)[:, None]
    kpos = jnp.arange(blk, dtype=I32)[None, :]
    diag = jnp.where(kpos // CHUNK <= qpos // CHUNK, _rel_bucket(kpos - qpos), -1)
    prev = _rel_bucket(kpos - (qpos + blk))
    bidx = jnp.stack([diag, prev]).astype(I32)
    vec = pl.BlockSpec((1, HEAD_DIM), lambda h: (0, 0))
    return pl.pallas_call(
        functools.partial(_rel_bias_kernel, _far_bucket(blk, seq)),
        grid=(N_HEADS,),
        in_specs=[
            pl.BlockSpec(memory_space=pltpu.SMEM),
            pl.BlockSpec((2, blk, blk), lambda h: (0, 0, 0)),
            vec, vec, vec, vec,
        ],
        out_specs=[
            pl.BlockSpec((1, 2, blk, blk), lambda h: (h, 0, 0, 0)),
            pl.BlockSpec((SUBLANES, LANES), lambda h: (0, 0)),
        ],
        out_shape=[jax.ShapeDtypeStruct((N_HEADS, 2, blk, blk), F32),
                   jax.ShapeDtypeStruct((SUBLANES, LANES), F32)],
        compiler_params=_cparams(1),
        name="rel_bias",
    )(rel_bias, bidx, lq1, lk1, lq2, lk2)


def _diff_attn_kernel(q_ref, k_ref, v_ref, bias_ref, lam_ref, sg_ref, o_ref, vext_ref):
    s_len = q_ref.shape[1]
    tq = ATTN_BLOCK
    n_heads = q_ref.shape[2] // HEAD_W
    lam = lam_ref[0:1, 0:1]
    lane = lax.broadcasted_iota(I32, (tq, HEAD_W), 1)
    cols = [slice(hh * HEAD_W, (hh + 1) * HEAD_W) for hh in range(n_heads)]
    b_diag, b_prev = [], []
    for hh in range(n_heads):
        vext_ref[hh, :, :HEAD_W] = v_ref[0, :, cols[hh]]
        vext_ref[hh, :, HEAD_W:] = jnp.ones((s_len, HEAD_W), BF16)
        b_diag.append(jnp.concatenate([bias_ref[hh, 0]] * 2, axis=0))
        b_prev.append(jnp.concatenate([bias_ref[hh, 1]] * 2, axis=0))

    for qi in range(s_len // tq):
        n_keys = (qi + 1) * tq
        rows = slice(qi * tq, (qi + 1) * tq)
        for hh in range(n_heads):
            q = q_ref[0, rows, cols[hh]]
            zero = jnp.zeros_like(q)
            qs = jnp.concatenate([jnp.where(lane < HEAD_DIM, q, zero),
                                  jnp.where(lane >= HEAD_DIM, q, zero)], axis=0)
            s = _dot_nt(qs, k_ref[0, :n_keys, cols[hh]])
            pieces = []
            if qi >= 2:
                pieces.append(s[:, :n_keys - 2 * tq])
            if qi >= 1:
                pieces.append(s[:, n_keys - 2 * tq:n_keys - tq] + b_prev[hh])
            pieces.append(s[:, n_keys - tq:] + b_diag[hh])
            s = jnp.concatenate(pieces, axis=1) if len(pieces) > 1 else pieces[0]
            m = jnp.max(s, axis=-1, keepdims=True)
            p = jnp.exp(s - m).astype(BF16)
            acc = _dot(p, vext_ref[hh, :n_keys, :])
            o = acc[:, :HEAD_W] / acc[:, HEAD_W:HEAD_W + 1]
            a = o[:tq] - lam * o[tq:]
            o_ref[0, rows, cols[hh]] = (
                _rms(a, sg_ref[...]) * (1.0 - LAMBDA_INIT)).astype(BF16)


def _diff_attn(q, k, v, bias, lam, subln_g, heads_per_step):
    b, s, aw = q.shape
    tq = ATTN_BLOCK
    hp = heads_per_step
    spec = pl.BlockSpec((1, s, hp * HEAD_W), lambda bi, h: (bi, 0, h))
    return pl.pallas_call(
        _diff_attn_kernel,
        grid=(b, N_HEADS // hp),
        in_specs=[
            spec, spec, spec,
            pl.BlockSpec((hp, 2, tq, tq), lambda bi, h: (h, 0, 0, 0)),
            pl.BlockSpec((SUBLANES, LANES), lambda bi, h: (0, 0)),
            pl.BlockSpec((1, HEAD_W), lambda bi, h: (0, 0)),
        ],
        out_specs=spec,
        out_shape=jax.ShapeDtypeStruct((b, s, aw), BF16),
        scratch_shapes=[pltpu.VMEM((hp, s, 2 * HEAD_W), BF16)],
        compiler_params=_cparams(2),
        name="diff_attn",
    )(q, k, v, bias, lam, subln_g)


def _mem_fold_kernel(m_ref, g_ref, wkv_ref, wq_ref, wo_ref, qk_ref, vo_ref):
    d = m_ref.shape[2]
    hd = d // X_HEADS
    m = _rms(m_ref[0], g_ref[...]).astype(BF16)
    kv = _dot(m, wkv_ref[...])
    k = kv[:, :d].astype(BF16)
    v = kv[:, d:].astype(BF16)
    for hh in range(X_HEADS):
        sl = slice(hh * hd, (hh + 1) * hd)
        qk_ref[0, :, hh * m.shape[0]:(hh + 1) * m.shape[0]] = (
            _dot_nt(wq_ref[:, sl], k[:, sl]) * (hd ** -0.5)).astype(BF16)
        vo_ref[0, hh * m.shape[0]:(hh + 1) * m.shape[0], :] = _dot(v[:, sl], wo_ref[sl, :]).astype(BF16)


def _mem_fold(mem, norm_g, w_xkv, w_xq, w_xo):
    b, ml, d = mem.shape
    const = lambda *shape: pl.BlockSpec(shape, lambda bi: (0,) * len(shape))
    return pl.pallas_call(
        _mem_fold_kernel,
        grid=(b,),
        in_specs=[pl.BlockSpec((1, ml, d), lambda bi: (bi, 0, 0)), const(1, d),
                  const(d, 2 * d), const(d, d), const(d, d)],
        out_specs=[pl.BlockSpec((1, d, X_HEADS * ml), lambda bi: (bi, 0, 0)),
                   pl.BlockSpec((1, X_HEADS * ml, d), lambda bi: (bi, 0, 0))],
        out_shape=[jax.ShapeDtypeStruct((b, d, X_HEADS * ml), BF16),
                   jax.ShapeDtypeStruct((b, X_HEADS * ml, d), BF16)],
        compiler_params=_cparams(1),
        name="mem_fold",
    )(mem, norm_g, w_xkv, w_xq, w_xo)


def _post_kernel(x_ref, a_ref, g0_ref, gyb_ref, wau_ref, wo_ref, nxg_ref,
                 qk_ref, vo_ref, nfg_ref, wrt_ref, br_ref,
                 x2_ref, hf_ref, eidx_ref, rank_ref, gate_ref, cnt_ref, run_ref):
    ts, d = x_ref.shape
    first = (pl.program_id(0) == 0) & (pl.program_id(1) == 0)

    y_a = _dot(a_ref[...], wau_ref[...])
    merged = g0_ref[...].astype(F32) * y_a + gyb_ref[...].astype(F32)
    x1 = x_ref[...] + _dot(merged.astype(BF16), wo_ref[...])

    ml = qk_ref.shape[2] // X_HEADS
    hx = _rms(x1, nxg_ref[...]).astype(BF16)
    s_all = _dot(hx, qk_ref[0])
    probs = []
    for hh in range(X_HEADS):
        s = s_all[:, hh * ml:(hh + 1) * ml]
        p = jnp.exp(s - jnp.max(s, axis=-1, keepdims=True))
        probs.append((p / jnp.sum(p, axis=-1, keepdims=True)).astype(BF16))
    x2 = x1 + _dot(jnp.concatenate(probs, axis=1), vo_ref[0])
    x2_ref[...] = x2
    hf = _rms(x2, nfg_ref[...])
    hf_ref[...] = _pack_halves(hf)

    logits = _dot_nt(wrt_ref[...], hf, precision=lax.Precision.HIGHEST) + br_ref[...]
    ne = logits.shape[0]
    eid = lax.broadcasted_iota(I32, logits.shape, 0).astype(F32)
    work = logits
    vals, idxs, hots = [], [], []
    for _ in range(TOP_K):
        mx = jnp.max(work, axis=0, keepdims=True)
        idx = jnp.min(jnp.where(work == mx, eid, float(ne)), axis=0, keepdims=True)
        hot = eid == idx
        vals.append(mx)
        idxs.append(idx.astype(I32))
        hots.append(hot)
        work = jnp.where(hot, -jnp.inf, work)
    ex = [jnp.exp(vv - vals[0]) for vv in vals]
    den = ex[0] + ex[1] + ex[2] + ex[3]
    gates = [e_ / den for e_ in ex]

    multi = (hots[0] | hots[1] | hots[2] | hots[3])
    multi_f = jnp.where(multi, 1.0, 0.0).astype(F32)
    tri = (lax.broadcasted_iota(I32, (ts, ts), 0)
           < lax.broadcasted_iota(I32, (ts, ts), 1))
    before = _dot(multi_f.astype(BF16), jnp.where(tri, 1.0, 0.0).astype(BF16))

    @pl.when(first)
    def _():
        run_ref[...] = jnp.zeros(run_ref.shape, F32)

    run = run_ref[...]
    pos = before + run
    ranks = [jnp.sum(jnp.where(hot, pos, 0.0), axis=0, keepdims=True) for hot in hots]
    run_new = run + jnp.sum(multi_f, axis=1, keepdims=True)
    run_ref[...] = run_new
    cnt_ref[...] = jnp.broadcast_to(run_new, cnt_ref.shape).astype(I32)

    eidx_ref[...] = jnp.concatenate(idxs, axis=0)
    rank_ref[...] = jnp.concatenate(ranks, axis=0).astype(I32)
    g_rows = jnp.concatenate(gates + [jnp.zeros((LANES - TOP_K, ts), F32)], axis=0)
    gate_ref[...] = g_rows.T


def _post(x2d, a2d, g0, gyb, w_attn_up, w_out, norm_x_g, qk_fold, vo_fold,
          norm_ffn_g, w_router_t, b_router, batch, ts):
    n, d = x2d.shape
    nt = n // batch // ts
    tok = pl.BlockSpec((ts, d), lambda bi, j: (bi * nt + j, 0))
    lanes = pl.BlockSpec((TOP_K, ts), lambda bi, j: (0, bi * nt + j))
    const = lambda *shape: pl.BlockSpec(shape, lambda bi, j: (0,) * len(shape),
                                        pipeline_mode=pl.Buffered(1))
    fold = lambda arr: pl.BlockSpec((1,) + arr.shape[1:], lambda bi, j: (bi, 0, 0))
    return pl.pallas_call(
        _post_kernel,
        grid=(batch, nt),
        in_specs=[tok, tok, tok, tok, const(d, d), const(d, d), const(1, d),
                  fold(qk_fold), fold(vo_fold), const(1, d), const(N_EXPERTS, d),
                  const(N_EXPERTS, 1)],
        out_specs=[tok, pl.BlockSpec((ts, d // 2), lambda bi, j: (bi * nt + j, 0)),
                   lanes, lanes,
                   pl.BlockSpec((ts, LANES), lambda bi, j: (bi * nt + j, 0)),
                   pl.BlockSpec((N_EXPERTS, LANES), lambda bi, j: (0, 0))],
        out_shape=[jax.ShapeDtypeStruct((n, d), F32),
                   jax.ShapeDtypeStruct((n, d // 2), jnp.uint32),
                   jax.ShapeDtypeStruct((TOP_K, n), I32), jax.ShapeDtypeStruct((TOP_K, n), I32),
                   jax.ShapeDtypeStruct((n, LANES), F32),
                   jax.ShapeDtypeStruct((N_EXPERTS, LANES), I32)],
        scratch_shapes=[pltpu.VMEM((N_EXPERTS, 1), F32)],
        compiler_params=_cparams(2),
        name="post",
    )(x2d, a2d, g0, gyb, w_attn_up, w_out, norm_x_g, qk_fold, vo_fold,
      norm_ffn_g, w_router_t, b_router)


def _route_dest_kernel(pstart_ref, eidx_ref, rank_ref, dest_ref):
    eidx = eidx_ref[...]
    start = jnp.zeros(eidx.shape, I32)
    for e in range(N_EXPERTS):
        start = jnp.where(eidx == e, pstart_ref[e], start)
    dest_ref[...] = start + rank_ref[...]


def _route_dest(pstart, eidx, rank, tl):
    k, n = eidx.shape
    lanes = pl.BlockSpec((k, tl), lambda i: (0, i))
    return pl.pallas_call(
        _route_dest_kernel,
        grid=(n // tl,),
        in_specs=[pl.BlockSpec(memory_space=pltpu.SMEM), lanes, lanes],
        out_specs=lanes,
        out_shape=jax.ShapeDtypeStruct((k, n), I32),
        compiler_params=_cparams(1),
        name="route_dest",
    )(pstart, eidx, rank)


SC_CORES = 2
SC_SUBCORES = 16
SC_WORKERS = SC_CORES * SC_SUBCORES
MOVE_CHUNK = 64
MOVE_BUFFERS = 3


def _worker_index_layout(dest, n):
    per_w = n // SC_WORKERS
    n_chunks = per_w // MOVE_CHUNK
    d4 = dest.reshape(TOP_K, SC_WORKERS, n_chunks, MOVE_CHUNK)
    return jnp.transpose(d4, (1, 2, 0, 3)).reshape(SC_WORKERS, n_chunks * TOP_K, MOVE_CHUNK)


def _sc_mesh():
    return plsc.VectorSubcoreMesh(core_axis_name="c", subcore_axis_name="s",
                                  num_cores=SC_CORES, num_subcores=SC_SUBCORES)


def _sc_dispatch(hf, dest_w, rows):
    n, d = hf.shape
    per_w = n // SC_WORKERS
    n_chunks = per_w // MOVE_CHUNK

    def body(hf_hbm, dest_hbm, xs_hbm, idx_v, rows_v, rsem, wsem):
        wid = lax.axis_index("s") * SC_CORES + lax.axis_index("c")
        base = wid * per_w
        pltpu.sync_copy(dest_hbm.at[wid], idx_v)

        def read(c):
            b = c % MOVE_BUFFERS
            return pltpu.async_copy(hf_hbm.at[pl.ds(base + c * MOVE_CHUNK, MOVE_CHUNK)],
                                    rows_v.at[b], rsem.at[b])

        scatters = [[] for _ in range(MOVE_BUFFERS)]
        pending = read(0)
        for c in range(n_chunks):
            b = c % MOVE_BUFFERS
            pending.wait()
            if c + 1 < n_chunks:
                for cp in scatters[(c + 1) % MOVE_BUFFERS]:
                    cp.wait()
                pending = read(c + 1)
            scatters[b] = [
                pltpu.async_copy(rows_v.at[b], xs_hbm.at[idx_v.at[c * TOP_K + kk]], wsem.at[b])
                for kk in range(TOP_K)]
        for group in scatters:
            for cp in group:
                cp.wait()

    return pl.kernel(
        body,
        out_type=jax.ShapeDtypeStruct((rows, d), hf.dtype),
        mesh=_sc_mesh(),
        scratch_types=[pltpu.VMEM((n_chunks * TOP_K, MOVE_CHUNK), I32),
                       pltpu.VMEM((MOVE_BUFFERS, MOVE_CHUNK, d), hf.dtype),
                       pltpu.SemaphoreType.DMA((MOVE_BUFFERS,)),
                       pltpu.SemaphoreType.DMA((MOVE_BUFFERS,))],
        name="sc_dispatch",
    )(hf, dest_w)


def _sc_gather(ys, dest_w, n):
    _, d = ys.shape
    per_w = n // SC_WORKERS
    n_chunks = per_w // MOVE_CHUNK

    def body(ys_hbm, dest_hbm, yg_hbm, idx_v, rows_v, rsem, wsem):
        wid = lax.axis_index("s") * SC_CORES + lax.axis_index("c")
        base = wid * per_w
        pltpu.sync_copy(dest_hbm.at[wid], idx_v)
        n_moves = n_chunks * TOP_K

        def gather(m):
            b = m % MOVE_BUFFERS
            return pltpu.async_copy(ys_hbm.at[idx_v.at[m]], rows_v.at[b], rsem.at[b])

        def write(m):
            b = m % MOVE_BUFFERS
            c, kk = divmod(m, TOP_K)
            return pltpu.async_copy(
                rows_v.at[b], yg_hbm.at[pl.ds(kk * n + base + c * MOVE_CHUNK, MOVE_CHUNK)],
                wsem.at[b])

        writes = [None] * MOVE_BUFFERS
        pending = gather(0)
        for m in range(n_moves):
            pending.wait()
            if m + 1 < n_moves:
                nb = (m + 1) % MOVE_BUFFERS
                if writes[nb] is not None:
                    writes[nb].wait()
                pending = gather(m + 1)
            writes[m % MOVE_BUFFERS] = write(m)
        for wr in writes:
            if wr is not None:
                wr.wait()

    return pl.kernel(
        body,
        out_type=jax.ShapeDtypeStruct((TOP_K * n, d), ys.dtype),
        mesh=_sc_mesh(),
        scratch_types=[pltpu.VMEM((n_chunks * TOP_K, MOVE_CHUNK), I32),
                       pltpu.VMEM((MOVE_BUFFERS, MOVE_CHUNK, d), ys.dtype),
                       pltpu.SemaphoreType.DMA((MOVE_BUFFERS,)),
                       pltpu.SemaphoreType.DMA((MOVE_BUFFERS,))],
        name="sc_gather",
    )(ys, dest_w)


def _combine_kernel(x2_ref, gate_ref, fg_ref, yg_ref, *rest):
    o_ref = rest[-1]
    g = gate_ref[...]
    x2 = x2_ref[...]
    c = x2.shape[1] // 2
    acc_hi, acc_lo = x2[:, :c], x2[:, c:]
    for kk in range(TOP_K):
        hi, lo = _unpack_halves(yg_ref[kk])
        acc_hi = acc_hi + g[:, kk:kk + 1] * hi
        acc_lo = acc_lo + g[:, kk:kk + 1] * lo
    o_ref[...] = _rms(jnp.concatenate([acc_hi, acc_lo], axis=1), fg_ref[...])


def _combine(x2, gate_tm, final_g, yg, ts, part, n_parts, prev_out):
    n, d = x2.shape
    dw = yg.shape[1]
    n_part = n // n_parts
    off = part * (n_part // ts)
    in_specs = [pl.BlockSpec((ts, d), lambda i: (i + off, 0)),
                pl.BlockSpec((ts, LANES), lambda i: (i + off, 0)),
                pl.BlockSpec((1, d), lambda i: (0, 0)),
                pl.BlockSpec((TOP_K, ts, dw), lambda i: (0, i, 0))]
    args = [x2, gate_tm, final_g, yg.reshape(TOP_K, n_part, dw)]
    aliases = {}
    if prev_out is not None:
        in_specs.append(pl.BlockSpec(memory_space=pl.ANY))
        args.append(prev_out)
        aliases = {len(args) - 1: 0}
    return pl.pallas_call(
        _combine_kernel,
        grid=(n_part // ts,),
        in_specs=in_specs,
        out_specs=pl.BlockSpec((ts, d), lambda i: (i + off, 0)),
        out_shape=jax.ShapeDtypeStruct((n, d), F32),
        input_output_aliases=aliases,
        compiler_params=_cparams(1),
        name="combine",
    )(*args)


def _experts_kernel(be_ref, first_ref, parts_ref, nxt_ref, slot_ref, nb_ref,
                    xs_ref, bgu_ref, bd_ref, wgu_hbm, wd_hbm, ys_ref,
                    wgu_st, wd_st, wgu_bf, wd_bf, sem_w):
    dff = wd_bf.shape[0]
    bm = xs_ref.shape[0] // EXPERT_BLOCKS_PER_STEP

    def fetch(ex, sl):
        return (pltpu.make_async_copy(wgu_hbm.at[ex], wgu_st.at[sl], sem_w.at[0, sl]),
                pltpu.make_async_copy(wd_hbm.at[ex], wd_st.at[sl], sem_w.at[1, sl]))

    @pl.when(pl.program_id(0) == 0)
    def _():
        e0 = be_ref[0]
        for cp in fetch(e0, slot_ref[e0]):
            cp.start()

    for sub in range(EXPERT_BLOCKS_PER_STEP):
        blk = pl.program_id(0) * EXPERT_BLOCKS_PER_STEP + sub
        e = be_ref[blk]
        rows = slice(sub * bm, (sub + 1) * bm)

        @pl.when(first_ref[blk] == 1)
        def _(e=e):
            sl = slot_ref[e]
            for cp in fetch(e, sl):
                cp.wait()
            nx = nxt_ref[e]

            @pl.when(nx >= 0)
            def _():
                for cp in fetch(nx, 1 - sl):
                    cp.start(priority=1)

            wgu_bf[...] = wgu_st[sl].astype(BF16)
            wd_bf[...] = wd_st[sl].astype(BF16)

        def mlp(e, rows):
            xb = jnp.concatenate(_unpack_halves(xs_ref[rows, :]), axis=1).astype(BF16)
            gu = _dot(xb, wgu_bf[...]) + bgu_ref[e]
            x_glu = jnp.minimum(gu[:, :dff], SWIGLU_LIMIT)
            x_lin = jnp.clip(gu[:, dff:], -SWIGLU_LIMIT, SWIGLU_LIMIT)
            act = x_glu * jax.nn.sigmoid(SWIGLU_ALPHA * x_glu) * (x_lin + 1.0)
            ys_ref[rows, :] = _pack_halves(_dot(act.astype(BF16), wd_bf[...]) + bd_ref[e])

        def clear(rows):
            ys_ref[rows, :] = jnp.zeros((rows.stop - rows.start, ys_ref.shape[1]), jnp.uint32)

        part = bm // EXPERT_TAIL_PARTS
        for live in range(EXPERT_TAIL_PARTS + 1):
            @pl.when(parts_ref[blk] == live)
            def _(e=e, rows=rows, live=live):
                cut = rows.start + live * part
                if live:
                    mlp(e, slice(rows.start, cut))
                if live < EXPERT_TAIL_PARTS:
                    clear(slice(cut, rows.stop))


def _expert_schedule(counts, bm, n_blocks):
    ne = counts.shape[0]
    ids = jnp.arange(ne, dtype=I32)
    upto = ids[None, :] <= ids[:, None]
    padded = (counts + bm - 1) // bm * bm
    pend = jnp.sum(jnp.where(upto, padded[None, :], 0), axis=1)
    pstart = (pend - padded).astype(I32)
    nb_used = (jnp.sum(padded) // bm).astype(I32)
    blk = jnp.arange(n_blocks, dtype=I32)
    blk_c = jnp.minimum(blk, nb_used - 1)
    be = jnp.minimum(jnp.sum((pend[None, :] <= (blk_c * bm)[:, None]).astype(I32), axis=1),
                     ne - 1)
    started = jnp.sum(((pstart[None, :] == (blk * bm)[:, None]) & (padded[None, :] > 0))
                      .astype(I32), axis=1)
    first = ((blk < nb_used) & (started > 0)).astype(I32)
    used = counts > 0
    seq = jnp.sum(jnp.where(upto & used[None, :], 1, 0), axis=1) - 1
    later = used[None, :] & (ids[None, :] > ids[:, None])
    nxt = jnp.where(jnp.any(later, axis=1), jnp.argmax(later, axis=1), -1).astype(I32)
    slot = (seq & 1).astype(I32)
    hit = ids[None, :] == be[:, None]
    left = jnp.sum(jnp.where(hit, (pstart + counts)[None, :], 0), axis=1) - blk * bm
    part = bm // EXPERT_TAIL_PARTS
    parts = jnp.where(blk < nb_used,
                      jnp.minimum((left + part - 1) // part, EXPERT_TAIL_PARTS), 0).astype(I32)
    return pstart, nb_used.reshape(1), be, first, parts, nxt, slot


def _experts(be, first, parts, nxt, slot, nb_used, xs, w_gu, b_gu, w_down, b_down, bm):
    rows, dw = xs.shape
    ne, d, dff2 = w_gu.shape
    dff = dff2 // 2
    sb = EXPERT_BLOCKS_PER_STEP * bm
    assert rows % sb == 0
    last = lambda j, be, fi, ha, nx, sl, nb: (
        jnp.maximum(jnp.minimum(j, (nb[0] - 1) // EXPERT_BLOCKS_PER_STEP), 0), 0)
    const = lambda *shape: pl.BlockSpec(shape,
                                        lambda j, be, fi, ha, nx, sl, nb: (0,) * len(shape))
    grid_spec = pltpu.PrefetchScalarGridSpec(
        num_scalar_prefetch=6,
        grid=(rows // sb,),
        in_specs=[
            pl.BlockSpec((sb, dw), last),
            const(ne, 1, dff2),
            const(ne, 1, d),
            pl.BlockSpec(memory_space=pl.ANY),
            pl.BlockSpec(memory_space=pl.ANY),
        ],
        out_specs=pl.BlockSpec((sb, dw), lambda j, be, fi, ha, nx, sl, nb: (j, 0)),
        scratch_shapes=[pltpu.VMEM((2, d, dff2), F32), pltpu.VMEM((2, dff, d), F32),
                        pltpu.VMEM((d, dff2), BF16), pltpu.VMEM((dff, d), BF16),
                        pltpu.SemaphoreType.DMA((2, 2))],
    )
    return pl.pallas_call(
        _experts_kernel,
        grid_spec=grid_spec,
        out_shape=jax.ShapeDtypeStruct((rows, dw), jnp.uint32),
        compiler_params=_cparams(1),
        name="experts",
    )(be, first, parts, nxt, slot, nb_used, xs, b_gu.reshape(ne, 1, dff2),
      b_down.reshape(ne, 1, d), w_gu, w_down)


MIXER_TILE = 512
ATTN_HEADS_PER_STEP = 4
POST_TILE = 512
DEST_TILE = 2048
COMBINE_TILE = 1024
GATHER_PARTS = 2
EXPERT_BLOCK = 1024
EXPERT_TAIL_PARTS = 4
EXPERT_BLOCKS_PER_STEP = 1


def kernel(x, mem, norm_mix_g, w_in, lambda_q1, lambda_k1, lambda_q2, lambda_k2, rel_bias,
           subln_g, w_attn_up, pool_mix, pool_scale, w_pool_up, w_gate, b_gate, w_out,
           norm_x_g, norm_mem_g, w_xq, w_xkv, w_xo, norm_ffn_g, w_router, b_router,
           w_gu, b_gu, w_down, b_down, final_norm_g):
    b, s, d = x.shape
    n = b * s
    assert w_in.shape[0] == 1, "single-layer block"
    row = lambda a: a.reshape(1, -1)
    bf = lambda a: a[0].astype(BF16)

    q, k, v, g0, gyb = _mixer_in(
        x, row(norm_mix_g[0]), bf(w_in), bf(w_gate), row(b_gate[0]), bf(pool_mix),
        row(pool_scale[0]), bf(w_pool_up), MIXER_TILE)

    bias, lam = _rel_bias(rel_bias, row(lambda_q1[0]), row(lambda_k1[0]),
                          row(lambda_q2[0]), row(lambda_k2[0]), s)
    a = _diff_attn(q, k, v, bias, lam, row(subln_g[0]), ATTN_HEADS_PER_STEP)

    qk_fold, vo_fold = _mem_fold(mem, row(norm_mem_g[0]), bf(w_xkv), bf(w_xq), bf(w_xo))

    x2, hf, eidx, rank, gate_tm, counts = _post(
        x.reshape(n, d), a.reshape(n, d), g0, gyb, bf(w_attn_up), bf(w_out),
        row(norm_x_g[0]), qk_fold, vo_fold, row(norm_ffn_g[0]),
        w_router[0].T, b_router[0].reshape(-1, 1), b, POST_TILE)

    bm = EXPERT_BLOCK
    rows = n * TOP_K + N_EXPERTS * bm
    pstart, nb_used, be, first, parts, nxt, slot = _expert_schedule(
        counts[:, 0], bm, rows // bm)

    dest = _route_dest(pstart, eidx, rank, DEST_TILE)
    xs = _sc_dispatch(hf, _worker_index_layout(dest, n), rows)
    ys = _experts(be, first, parts, nxt, slot, nb_used, xs, w_gu[0], b_gu[0], w_down[0],
                  b_down[0], bm)
    n_part = n // GATHER_PARTS
    out = None
    for p in range(GATHER_PARTS):
        dest_p = _worker_index_layout(dest[:, p * n_part:(p + 1) * n_part], n_part)
        yg = _sc_gather(ys, dest_p, n_part)
        out = _combine(x2, gate_tm, row(final_norm_g), yg, COMBINE_TILE, p, GATHER_PARTS, out)
    return out.reshape(b, s, d)
```

```python
import functools
import math

import numpy as np
import jax
import jax.numpy as jnp
from jax import lax
from jax.experimental import pallas as pl
from jax.experimental.pallas import tpu as pltpu
from jax.experimental.pallas import tpu_sc as plsc

F32 = jnp.float32
BF16 = jnp.bfloat16
I32 = jnp.int32

EPS = 1e-6
CHUNK = 64
N_HEADS = 8
HEAD_DIM = 64
HEAD_W = 2 * HEAD_DIM
POOL_WINDOWS = (2, 4, 8, 16)
POOL_GROUP = 128
POOL_PAD = 16
N_BUCKETS = 32
MAX_DISTANCE = 128
X_HEADS = 4
N_EXPERTS = 32
TOP_K = 4
SWIGLU_ALPHA = 1.702
SWIGLU_LIMIT = 7.0
LAMBDA_INIT = 0.8 - 0.6 * math.exp(-0.3 * 0)

LANES = 128
SUBLANES = 8
ATTN_BLOCK = 256
VMEM_LIMIT = 56 * 1024 * 1024


def _cparams(n_axes, vmem=VMEM_LIMIT):
    return pltpu.CompilerParams(
        dimension_semantics=("arbitrary",) * n_axes, vmem_limit_bytes=vmem)


def _rms(xf, g):
    ms = jnp.mean(xf * xf, axis=-1, keepdims=True)
    return xf * lax.rsqrt(ms + EPS) * g


def _dot(a, b):
    return jnp.dot(a, b, preferred_element_type=F32)


def _pack_halves(x):
    c = x.shape[1] // 2
    as_bits = lambda v: lax.bitcast_convert_type(v.astype(BF16).astype(F32), jnp.uint32)
    return as_bits(x[:, :c]) | (as_bits(x[:, c:]) >> 16)


def _unpack_halves(w):
    hi = lax.bitcast_convert_type(w & jnp.uint32(0xFFFF0000), F32)
    lo = lax.bitcast_convert_type(w << 16, F32)
    return hi, lo


def _dot_nt(a, b, precision=None):
    return lax.dot_general(a, b, (((1,), (1,)), ((), ())),
                           preferred_element_type=F32, precision=precision)


def _mixer_in_kernel(x_ref, g_ref, wq_ref, wk_ref, wv_ref, wu_ref, wg_ref, bg_ref,
                     pmix_ref, pscale_ref, wpu_ref,
                     q_ref, k_ref, v_ref, g0_ref, gyb_ref, ext_ref, wpool_ref):
    ts = x_ref.shape[1]
    d = x_ref.shape[2]
    j = pl.program_id(1)

    @pl.when((pl.program_id(0) == 0) & (j == 0))
    def _():
        for gi in range(len(POOL_WINDOWS)):
            sl = slice(gi * POOL_GROUP, (gi + 1) * POOL_GROUP)
            mix = (pmix_ref[gi].astype(F32) * pscale_ref[:, sl]).astype(BF16)
            wpool_ref[sl, :] = _dot(mix, wpu_ref[sl, :]).astype(BF16)

    h = _rms(x_ref[0], g_ref[...]).astype(BF16)
    q_ref[0] = (_dot(h, wq_ref[...]) * (HEAD_DIM ** -0.5)).astype(BF16)
    k_ref[0] = _dot(h, wk_ref[...]).astype(BF16)
    v_ref[0] = _dot(h, wv_ref[...]).astype(BF16)
    u = _dot(h, wu_ref[...])

    @pl.when(j == 0)
    def _():
        ext_ref[0:POOL_PAD, :] = jnp.zeros((POOL_PAD, u.shape[1]), F32)

    ext_ref[POOL_PAD:POOL_PAD + ts, :] = u
    e = ext_ref[...]
    sums = {}
    s = e
    w = 1
    while w < max(POOL_WINDOWS):
        s = s + pltpu.roll(s, w, 0)
        w *= 2
        sums[w] = s
    ext_ref[0:POOL_PAD, :] = ext_ref[ts:ts + POOL_PAD, :]

    pos = (j * ts + lax.broadcasted_iota(I32, (ts, 1), 0) + 1).astype(F32)
    pooled = []
    for gi, w in enumerate(POOL_WINDOWS):
        sl = slice(gi * POOL_GROUP, (gi + 1) * POOL_GROUP)
        win = sums[w][POOL_PAD:, sl]
        pooled.append(win / jnp.minimum(pos, float(w)) - u[:, sl])
    y_b = _dot(jnp.concatenate(pooled, axis=1).astype(BF16), wpool_ref[...])

    gate = jax.nn.sigmoid(_dot(h, wg_ref[...]) + bg_ref[...])
    g0_ref[...] = gate[:, :d].astype(BF16)
    gyb_ref[...] = (gate[:, d:] * y_b).astype(BF16)


def _mixer_in(x, norm_g, w_in, w_gate, b_gate, pool_mix, pool_scale, w_pool_up, ts):
    b, s, d = x.shape
    aw = N_HEADS * HEAD_W
    pw = len(POOL_WINDOWS) * POOL_GROUP
    n = b * s
    nt = s // ts
    const = lambda *shape: pl.BlockSpec(shape, lambda bi, j: (0,) * len(shape))
    tok3 = pl.BlockSpec((1, ts, aw), lambda bi, j: (bi, j, 0))
    tok2 = pl.BlockSpec((ts, d), lambda bi, j: (bi * nt + j, 0))
    return pl.pallas_call(
        _mixer_in_kernel,
        grid=(b, nt),
        in_specs=[
            pl.BlockSpec((1, ts, d), lambda bi, j: (bi, j, 0)),
            const(1, d),
            pl.BlockSpec((d, aw), lambda bi, j: (0, 0)),
            pl.BlockSpec((d, aw), lambda bi, j: (0, 1)),
            pl.BlockSpec((d, aw), lambda bi, j: (0, 2)),
            pl.BlockSpec((d, pw), lambda bi, j: (0, 3 * aw // pw)),
            const(d, 2 * d),
            const(1, 2 * d),
            const(len(POOL_WINDOWS), POOL_GROUP, POOL_GROUP),
            const(1, pw),
            const(pw, d),
        ],
        out_specs=[tok3, tok3, tok3, tok2, tok2],
        out_shape=[jax.ShapeDtypeStruct((b, s, aw), BF16)] * 3
        + [jax.ShapeDtypeStruct((n, d), BF16)] * 2,
        scratch_shapes=[pltpu.VMEM((ts + POOL_PAD, pw), F32), pltpu.VMEM((pw, d), BF16)],
        compiler_params=_cparams(2),
        name="mixer_in",
    )(x, norm_g, w_in, w_in, w_in, w_in, w_gate, b_gate, pool_mix, pool_scale, w_pool_up)


def _rel_bucket(rel, log=jnp.log, f32=lambda a: a.astype(jnp.float32),
                i32=lambda a: a.astype(jnp.int32), xp=jnp):
    nb = N_BUCKETS // 2
    ret = i32(rel > 0) * nb
    n = xp.abs(rel)
    max_exact = nb // 2
    nf = f32(xp.maximum(n, 1))
    large = max_exact + i32(log(nf / max_exact) / math.log(MAX_DISTANCE / max_exact)
                            * (nb - max_exact))
    large = xp.minimum(large, nb - 1)
    return ret + xp.where(n < max_exact, n, large)


def _far_bucket(block, seq):
    rel = -np.arange(block + 1, max(seq, block + 2), dtype=np.int32)
    bk = _rel_bucket(rel, log=np.log, f32=lambda a: a.astype(np.float32),
                     i32=lambda a: a.astype(np.int32), xp=np)
    assert (bk == bk[0]).all(), "far keys must share one relative-position bucket"
    return int(bk[0])


def _rel_bias_kernel(far_bucket, tab_ref, bidx_ref, lq1_ref, lk1_ref, lq2_ref, lk2_ref,
                     bias_ref, lam_ref):
    h = pl.program_id(0)
    bidx = bidx_ref[...]
    acc = jnp.zeros(bidx.shape, F32)
    for bkt in range(N_BUCKETS):
        acc = jnp.where(bidx == bkt, tab_ref[bkt, h], acc)
    acc = acc - tab_ref[far_bucket, h]
    bias_ref[0] = jnp.where(bidx < 0, -jnp.inf, acc)
    lam = (jnp.exp(jnp.sum(lq1_ref[...] * lk1_ref[...], keepdims=True))
           - jnp.exp(jnp.sum(lq2_ref[...] * lk2_ref[...], keepdims=True)) + LAMBDA_INIT)
    lam_ref[...] = jnp.broadcast_to(lam, lam_ref.shape)


def _rel_bias(rel_bias, lq1, lk1, lq2, lk2, seq):
    blk = ATTN_BLOCK
    qpos = jnp.arange(blk, dtype=I32)[:, None]
    kpos = jnp.arange(blk, dtype=I32)[None, :]
    diag = jnp.where(kpos // CHUNK <= qpos // CHUNK, _rel_bucket(kpos - qpos), -1)
    prev = _rel_bucket(kpos - (qpos + blk))
    bidx = jnp.stack([diag, prev]).astype(I32)
    vec = pl.BlockSpec((1, HEAD_DIM), lambda h: (0, 0))
    return pl.pallas_call(
        functools.partial(_rel_bias_kernel, _far_bucket(blk, seq)),
        grid=(N_HEADS,),
        in_specs=[
            pl.BlockSpec(memory_space=pltpu.SMEM),
            pl.BlockSpec((2, blk, blk), lambda h: (0, 0, 0)),
            vec, vec, vec, vec,
        ],
        out_specs=[
            pl.BlockSpec((1, 2, blk, blk), lambda h: (h, 0, 0, 0)),
            pl.BlockSpec((SUBLANES, LANES), lambda h: (0, 0)),
        ],
        out_shape=[jax.ShapeDtypeStruct((N_HEADS, 2, blk, blk), F32),
                   jax.ShapeDtypeStruct((SUBLANES, LANES), F32)],
        compiler_params=_cparams(1),
        name="rel_bias",
    )(rel_bias, bidx, lq1, lk1, lq2, lk2)


def _diff_attn_kernel(q_ref, k_ref, v_ref, bias_ref, lam_ref, sg_ref, o_ref, vext_ref):
    s_len = q_ref.shape[1]
    tq = ATTN_BLOCK
    n_heads = q_ref.shape[2] // HEAD_W
    lam = lam_ref[0:1, 0:1]
    lane = lax.broadcasted_iota(I32, (tq, HEAD_W), 1)
    cols = [slice(hh * HEAD_W, (hh + 1) * HEAD_W) for hh in range(n_heads)]
    b_diag, b_prev = [], []
    for hh in range(n_heads):
        vext_ref[hh, :, :HEAD_W] = v_ref[0, :, cols[hh]]
        vext_ref[hh, :, HEAD_W:] = jnp.ones((s_len, HEAD_W), BF16)
        b_diag.append(jnp.concatenate([bias_ref[hh, 0]] * 2, axis=0))
        b_prev.append(jnp.concatenate([bias_ref[hh, 1]] * 2, axis=0))

    for qi in range(s_len // tq):
        n_keys = (qi + 1) * tq
        rows = slice(qi * tq, (qi + 1) * tq)
        for hh in range(n_heads):
            q = q_ref[0, rows, cols[hh]]
            zero = jnp.zeros_like(q)
            qs = jnp.concatenate([jnp.where(lane < HEAD_DIM, q, zero),
                                  jnp.where(lane >= HEAD_DIM, q, zero)], axis=0)
            s = _dot_nt(qs, k_ref[0, :n_keys, cols[hh]])
            pieces = []
            if qi >= 2:
                pieces.append(s[:, :n_keys - 2 * tq])
            if qi >= 1:
                pieces.append(s[:, n_keys - 2 * tq:n_keys - tq] + b_prev[hh])
            pieces.append(s[:, n_keys - tq:] + b_diag[hh])
            s = jnp.concatenate(pieces, axis=1) if len(pieces) > 1 else pieces[0]
            m = jnp.max(s, axis=-1, keepdims=True)
            p = jnp.exp(s - m).astype(BF16)
            acc = _dot(p, vext_ref[hh, :n_keys, :])
            o = acc[:, :HEAD_W] / acc[:, HEAD_W:HEAD_W + 1]
            a = o[:tq] - lam * o[tq:]
            o_ref[0, rows, cols[hh]] = (
                _rms(a, sg_ref[...]) * (1.0 - LAMBDA_INIT)).astype(BF16)


def _diff_attn(q, k, v, bias, lam, subln_g, heads_per_step):
    b, s, aw = q.shape
    tq = ATTN_BLOCK
    hp = heads_per_step
    spec = pl.BlockSpec((1, s, hp * HEAD_W), lambda bi, h: (bi, 0, h))
    return pl.pallas_call(
        _diff_attn_kernel,
        grid=(b, N_HEADS // hp),
        in_specs=[
            spec, spec, spec,
            pl.BlockSpec((hp, 2, tq, tq), lambda bi, h: (h, 0, 0, 0)),
            pl.BlockSpec((SUBLANES, LANES), lambda bi, h: (0, 0)),
            pl.BlockSpec((1, HEAD_W), lambda bi, h: (0, 0)),
        ],
        out_specs=spec,
        out_shape=jax.ShapeDtypeStruct((b, s, aw), BF16),
        scratch_shapes=[pltpu.VMEM((hp, s, 2 * HEAD_W), BF16)],
        compiler_params=_cparams(2),
        name="diff_attn",
    )(q, k, v, bias, lam, subln_g)


def _mem_fold_kernel(m_ref, g_ref, wkv_ref, wq_ref, wo_ref, qk_ref, vo_ref):
    d = m_ref.shape[2]
    hd = d // X_HEADS
    m = _rms(m_ref[0], g_ref[...]).astype(BF16)
    kv = _dot(m, wkv_ref[...])
    k = kv[:, :d].astype(BF16)
    v = kv[:, d:].astype(BF16)
    for hh in range(X_HEADS):
        sl = slice(hh * hd, (hh + 1) * hd)
        qk_ref[0, :, hh * m.shape[0]:(hh + 1) * m.shape[0]] = (
            _dot_nt(wq_ref[:, sl], k[:, sl]) * (hd ** -0.5)).astype(BF16)
        vo_ref[0, hh * m.shape[0]:(hh + 1) * m.shape[0], :] = _dot(v[:, sl], wo_ref[sl, :]).astype(BF16)


def _mem_fold(mem, norm_g, w_xkv, w_xq, w_xo):
    b, ml, d = mem.shape
    const = lambda *shape: pl.BlockSpec(shape, lambda bi: (0,) * len(shape))
    return pl.pallas_call(
        _mem_fold_kernel,
        grid=(b,),
        in_specs=[pl.BlockSpec((1, ml, d), lambda bi: (bi, 0, 0)), const(1, d),
                  const(d, 2 * d), const(d, d), const(d, d)],
        out_specs=[pl.BlockSpec((1, d, X_HEADS * ml), lambda bi: (bi, 0, 0)),
                   pl.BlockSpec((1, X_HEADS * ml, d), lambda bi: (bi, 0, 0))],
        out_shape=[jax.ShapeDtypeStruct((b, d, X_HEADS * ml), BF16),
                   jax.ShapeDtypeStruct((b, X_HEADS * ml, d), BF16)],
        compiler_params=_cparams(1),
        name="mem_fold",
    )(mem, norm_g, w_xkv, w_xq, w_xo)


def _post_kernel(x_ref, a_ref, g0_ref, gyb_ref, wau_ref, wo_ref, nxg_ref,
                 qk_ref, vo_ref, nfg_ref, wrt_ref, br_ref,
                 x2_ref, hf_ref, eidx_ref, rank_ref, gate_ref, cnt_ref, run_ref):
    ts, d = x_ref.shape
    first = (pl.program_id(0) == 0) & (pl.program_id(1) == 0)

    y_a = _dot(a_ref[...], wau_ref[...])
    merged = g0_ref[...].astype(F32) * y_a + gyb_ref[...].astype(F32)
    x1 = x_ref[...] + _dot(merged.astype(BF16), wo_ref[...])

    ml = qk_ref.shape[2] // X_HEADS
    hx = _rms(x1, nxg_ref[...]).astype(BF16)
    s_all = _dot(hx, qk_ref[0])
    probs = []
    for hh in range(X_HEADS):
        s = s_all[:, hh * ml:(hh + 1) * ml]
        p = jnp.exp(s - jnp.max(s, axis=-1, keepdims=True))
        probs.append((p / jnp.sum(p, axis=-1, keepdims=True)).astype(BF16))
    x2 = x1 + _dot(jnp.concatenate(probs, axis=1), vo_ref[0])
    x2_ref[...] = x2
    hf = _rms(x2, nfg_ref[...])
    hf_ref[...] = _pack_halves(hf)

    logits = _dot_nt(wrt_ref[...], hf, precision=lax.Precision.HIGHEST) + br_ref[...]
    ne = logits.shape[0]
    eid = lax.broadcasted_iota(I32, logits.shape, 0).astype(F32)
    work = logits
    vals, idxs, hots = [], [], []
    for _ in range(TOP_K):
        mx = jnp.max(work, axis=0, keepdims=True)
        idx = jnp.min(jnp.where(work == mx, eid, float(ne)), axis=0, keepdims=True)
        hot = eid == idx
        vals.append(mx)
        idxs.append(idx.astype(I32))
        hots.append(hot)
        work = jnp.where(hot, -jnp.inf, work)
    ex = [jnp.exp(vv - vals[0]) for vv in vals]
    den = ex[0] + ex[1] + ex[2] + ex[3]
    gates = [e_ / den for e_ in ex]

    multi = (hots[0] | hots[1] | hots[2] | hots[3])
    multi_f = jnp.where(multi, 1.0, 0.0).astype(F32)
    tri = (lax.broadcasted_iota(I32, (ts, ts), 0)
           < lax.broadcasted_iota(I32, (ts, ts), 1))
    before = _dot(multi_f.astype(BF16), jnp.where(tri, 1.0, 0.0).astype(BF16))

    @pl.when(first)
    def _():
        run_ref[...] = jnp.zeros(run_ref.shape, F32)

    run = run_ref[...]
    pos = before + run
    ranks = [jnp.sum(jnp.where(hot, pos, 0.0), axis=0, keepdims=True) for hot in hots]
    run_new = run + jnp.sum(multi_f, axis=1, keepdims=True)
    run_ref[...] = run_new
    cnt_ref[...] = jnp.broadcast_to(run_new, cnt_ref.shape).astype(I32)

    eidx_ref[...] = jnp.concatenate(idxs, axis=0)
    rank_ref[...] = jnp.concatenate(ranks, axis=0).astype(I32)
    g_rows = jnp.concatenate(gates + [jnp.zeros((LANES - TOP_K, ts), F32)], axis=0)
    gate_ref[...] = g_rows.T


def _post(x2d, a2d, g0, gyb, w_attn_up, w_out, norm_x_g, qk_fold, vo_fold,
          norm_ffn_g, w_router_t, b_router, batch, ts):
    n, d = x2d.shape
    nt = n // batch // ts
    tok = pl.BlockSpec((ts, d), lambda bi, j: (bi * nt + j, 0))
    lanes = pl.BlockSpec((TOP_K, ts), lambda bi, j: (0, bi * nt + j))
    const = lambda *shape: pl.BlockSpec(shape, lambda bi, j: (0,) * len(shape),
                                        pipeline_mode=pl.Buffered(1))
    fold = lambda arr: pl.BlockSpec((1,) + arr.shape[1:], lambda bi, j: (bi, 0, 0))
    return pl.pallas_call(
        _post_kernel,
        grid=(batch, nt),
        in_specs=[tok, tok, tok, tok, const(d, d), const(d, d), const(1, d),
                  fold(qk_fold), fold(vo_fold), const(1, d), const(N_EXPERTS, d),
                  const(N_EXPERTS, 1)],
        out_specs=[tok, pl.BlockSpec((ts, d // 2), lambda bi, j: (bi * nt + j, 0)),
                   lanes, lanes,
                   pl.BlockSpec((ts, LANES), lambda bi, j: (bi * nt + j, 0)),
                   pl.BlockSpec((N_EXPERTS, LANES), lambda bi, j: (0, 0))],
        out_shape=[jax.ShapeDtypeStruct((n, d), F32),
                   jax.ShapeDtypeStruct((n, d // 2), jnp.uint32),
                   jax.ShapeDtypeStruct((TOP_K, n), I32), jax.ShapeDtypeStruct((TOP_K, n), I32),
                   jax.ShapeDtypeStruct((n, LANES), F32),
                   jax.ShapeDtypeStruct((N_EXPERTS, LANES), I32)],
        scratch_shapes=[pltpu.VMEM((N_EXPERTS, 1), F32)],
        compiler_params=_cparams(2),
        name="post",
    )(x2d, a2d, g0, gyb, w_attn_up, w_out, norm_x_g, qk_fold, vo_fold,
      norm_ffn_g, w_router_t, b_router)


def _route_dest_kernel(pstart_ref, eidx_ref, rank_ref, dest_ref):
    eidx = eidx_ref[...]
    start = jnp.zeros(eidx.shape, I32)
    for e in range(N_EXPERTS):
        start = jnp.where(eidx == e, pstart_ref[e], start)
    dest_ref[...] = start + rank_ref[...]


def _route_dest(pstart, eidx, rank, tl):
    k, n = eidx.shape
    lanes = pl.BlockSpec((k, tl), lambda i: (0, i))
    return pl.pallas_call(
        _route_dest_kernel,
        grid=(n // tl,),
        in_specs=[pl.BlockSpec(memory_space=pltpu.SMEM), lanes, lanes],
        out_specs=lanes,
        out_shape=jax.ShapeDtypeStruct((k, n), I32),
        compiler_params=_cparams(1),
        name="route_dest",
    )(pstart, eidx, rank)


SC_CORES = 2
SC_SUBCORES = 16
SC_WORKERS = SC_CORES * SC_SUBCORES
MOVE_CHUNK = 64
MOVE_BUFFERS = 3


def _worker_index_layout(dest, n):
    per_w = n // SC_WORKERS
    n_chunks = per_w // MOVE_CHUNK
    d4 = dest.reshape(TOP_K, SC_WORKERS, n_chunks, MOVE_CHUNK)
    return jnp.transpose(d4, (1, 2, 0, 3)).reshape(SC_WORKERS, n_chunks * TOP_K, MOVE_CHUNK)


def _sc_mesh():
    return plsc.VectorSubcoreMesh(core_axis_name="c", subcore_axis_name="s",
                                  num_cores=SC_CORES, num_subcores=SC_SUBCORES)


def _sc_dispatch(hf, dest_w, rows):
    n, d = hf.shape
    per_w = n // SC_WORKERS
    n_chunks = per_w // MOVE_CHUNK

    def body(hf_hbm, dest_hbm, xs_hbm, idx_v, rows_v, rsem, wsem):
        wid = lax.axis_index("s") * SC_CORES + lax.axis_index("c")
        base = wid * per_w
        pltpu.sync_copy(dest_hbm.at[wid], idx_v)

        def read(c):
            b = c % MOVE_BUFFERS
            return pltpu.async_copy(hf_hbm.at[pl.ds(base + c * MOVE_CHUNK, MOVE_CHUNK)],
                                    rows_v.at[b], rsem.at[b])

        scatters = [[] for _ in range(MOVE_BUFFERS)]
        pending = read(0)
        for c in range(n_chunks):
            b = c % MOVE_BUFFERS
            pending.wait()
            if c + 1 < n_chunks:
                for cp in scatters[(c + 1) % MOVE_BUFFERS]:
                    cp.wait()
                pending = read(c + 1)
            scatters[b] = [
                pltpu.async_copy(rows_v.at[b], xs_hbm.at[idx_v.at[c * TOP_K + kk]], wsem.at[b])
                for kk in range(TOP_K)]
        for group in scatters:
            for cp in group:
                cp.wait()

    return pl.kernel(
        body,
        out_type=jax.ShapeDtypeStruct((rows, d), hf.dtype),
        mesh=_sc_mesh(),
        scratch_types=[pltpu.VMEM((n_chunks * TOP_K, MOVE_CHUNK), I32),
                       pltpu.VMEM((MOVE_BUFFERS, MOVE_CHUNK, d), hf.dtype),
                       pltpu.SemaphoreType.DMA((MOVE_BUFFERS,)),
                       pltpu.SemaphoreType.DMA((MOVE_BUFFERS,))],
        name="sc_dispatch",
    )(hf, dest_w)


def _sc_gather(ys, dest_w, n):
    _, d = ys.shape
    per_w = n // SC_WORKERS
    n_chunks = per_w // MOVE_CHUNK

    def body(ys_hbm, dest_hbm, yg_hbm, idx_v, rows_v, rsem, wsem):
        wid = lax.axis_index("s") * SC_CORES + lax.axis_index("c")
        base = wid * per_w
        pltpu.sync_copy(dest_hbm.at[wid], idx_v)
        n_moves = n_chunks * TOP_K

        def gather(m):
            b = m % MOVE_BUFFERS
            return pltpu.async_copy(ys_hbm.at[idx_v.at[m]], rows_v.at[b], rsem.at[b])

        def write(m):
            b = m % MOVE_BUFFERS
            c, kk = divmod(m, TOP_K)
            return pltpu.async_copy(
                rows_v.at[b], yg_hbm.at[pl.ds(kk * n + base + c * MOVE_CHUNK, MOVE_CHUNK)],
                wsem.at[b])

        writes = [None] * MOVE_BUFFERS
        pending = gather(0)
        for m in range(n_moves):
            pending.wait()
            if m + 1 < n_moves:
                nb = (m + 1) % MOVE_BUFFERS
                if writes[nb] is not None:
                    writes[nb].wait()
                pending = gather(m + 1)
            writes[m % MOVE_BUFFERS] = write(m)
        for wr in writes:
            if wr is not None:
                wr.wait()

    return pl.kernel(
        body,
        out_type=jax.ShapeDtypeStruct((TOP_K * n, d), ys.dtype),
        mesh=_sc_mesh(),
        scratch_types=[pltpu.VMEM((n_chunks * TOP_K, MOVE_CHUNK), I32),
                       pltpu.VMEM((MOVE_BUFFERS, MOVE_CHUNK, d), ys.dtype),
                       pltpu.SemaphoreType.DMA((MOVE_BUFFERS,)),
                       pltpu.SemaphoreType.DMA((MOVE_BUFFERS,))],
        name="sc_gather",
    )(ys, dest_w)


def _combine_kernel(x2_ref, gate_ref, fg_ref, yg_ref, *rest):
    o_ref = rest[-1]
    g = gate_ref[...]
    x2 = x2_ref[...]
    c = x2.shape[1] // 2
    acc_hi, acc_lo = x2[:, :c], x2[:, c:]
    for kk in range(TOP_K):
        hi, lo = _unpack_halves(yg_ref[kk])
        acc_hi = acc_hi + g[:, kk:kk + 1] * hi
        acc_lo = acc_lo + g[:, kk:kk + 1] * lo
    o_ref[...] = _rms(jnp.concatenate([acc_hi, acc_lo], axis=1), fg_ref[...])


def _combine(x2, gate_tm, final_g, yg, ts, part, n_parts, prev_out):
    n, d = x2.shape
    dw = yg.shape[1]
    n_part = n // n_parts
    off = part * (n_part // ts)
    in_specs = [pl.BlockSpec((ts, d), lambda i: (i + off, 0)),
                pl.BlockSpec((ts, LANES), lambda i: (i + off, 0)),
                pl.BlockSpec((1, d), lambda i: (0, 0)),
                pl.BlockSpec((TOP_K, ts, dw), lambda i: (0, i, 0))]
    args = [x2, gate_tm, final_g, yg.reshape(TOP_K, n_part, dw)]
    aliases = {}
    if prev_out is not None:
        in_specs.append(pl.BlockSpec(memory_space=pl.ANY))
        args.append(prev_out)
        aliases = {len(args) - 1: 0}
    return pl.pallas_call(
        _combine_kernel,
        grid=(n_part // ts,),
        in_specs=in_specs,
        out_specs=pl.BlockSpec((ts, d), lambda i: (i + off, 0)),
        out_shape=jax.ShapeDtypeStruct((n, d), F32),
        input_output_aliases=aliases,
        compiler_params=_cparams(1),
        name="combine",
    )(*args)


def _experts_kernel(be_ref, first_ref, parts_ref, nxt_ref, slot_ref, nb_ref,
                    xs_ref, bgu_ref, bd_ref, wgu_hbm, wd_hbm, ys_ref,
                    wgu_st, wd_st, wgu_bf, wd_bf, sem_w):
    dff = wd_bf.shape[0]
    bm = xs_ref.shape[0] // EXPERT_BLOCKS_PER_STEP

    def fetch(ex, sl):
        return (pltpu.make_async_copy(wgu_hbm.at[ex], wgu_st.at[sl], sem_w.at[0, sl]),
                pltpu.make_async_copy(wd_hbm.at[ex], wd_st.at[sl], sem_w.at[1, sl]))

    @pl.when(pl.program_id(0) == 0)
    def _():
        e0 = be_ref[0]
        for cp in fetch(e0, slot_ref[e0]):
            cp.start()

    for sub in range(EXPERT_BLOCKS_PER_STEP):
        blk = pl.program_id(0) * EXPERT_BLOCKS_PER_STEP + sub
        e = be_ref[blk]
        rows = slice(sub * bm, (sub + 1) * bm)

        @pl.when(first_ref[blk] == 1)
        def _(e=e):
            sl = slot_ref[e]
            for cp in fetch(e, sl):
                cp.wait()
            nx = nxt_ref[e]

            @pl.when(nx >= 0)
            def _():
                for cp in fetch(nx, 1 - sl):
                    cp.start(priority=1)

            wgu_bf[...] = wgu_st[sl].astype(BF16)
            wd_bf[...] = wd_st[sl].astype(BF16)

        def mlp(e, rows):
            xb = jnp.concatenate(_unpack_halves(xs_ref[rows, :]), axis=1).astype(BF16)
            gu = _dot(xb, wgu_bf[...]) + bgu_ref[e]
            x_glu = jnp.minimum(gu[:, :dff], SWIGLU_LIMIT)
            x_lin = jnp.clip(gu[:, dff:], -SWIGLU_LIMIT, SWIGLU_LIMIT)
            act = x_glu * jax.nn.sigmoid(SWIGLU_ALPHA * x_glu) * (x_lin + 1.0)
            ys_ref[rows, :] = _pack_halves(_dot(act.astype(BF16), wd_bf[...]) + bd_ref[e])

        def clear(rows):
            ys_ref[rows, :] = jnp.zeros((rows.stop - rows.start, ys_ref.shape[1]), jnp.uint32)

        part = bm // EXPERT_TAIL_PARTS
        for live in range(EXPERT_TAIL_PARTS + 1):
            @pl.when(parts_ref[blk] == live)
            def _(e=e, rows=rows, live=live):
                cut = rows.start + live * part
                if live:
                    mlp(e, slice(rows.start, cut))
                if live < EXPERT_TAIL_PARTS:
                    clear(slice(cut, rows.stop))


def _expert_schedule(counts, bm, n_blocks):
    ne = counts.shape[0]
    ids = jnp.arange(ne, dtype=I32)
    upto = ids[None, :] <= ids[:, None]
    padded = (counts + bm - 1) // bm * bm
    pend = jnp.sum(jnp.where(upto, padded[None, :], 0), axis=1)
    pstart = (pend - padded).astype(I32)
    nb_used = (jnp.sum(padded) // bm).astype(I32)
    blk = jnp.arange(n_blocks, dtype=I32)
    blk_c = jnp.minimum(blk, nb_used - 1)
    be = jnp.minimum(jnp.sum((pend[None, :] <= (blk_c * bm)[:, None]).astype(I32), axis=1),
                     ne - 1)
    started = jnp.sum(((pstart[None, :] == (blk * bm)[:, None]) & (padded[None, :] > 0))
                      .astype(I32), axis=1)
    first = ((blk < nb_used) & (started > 0)).astype(I32)
    used = counts > 0
    seq = jnp.sum(jnp.where(upto & used[None, :], 1, 0), axis=1) - 1
    later = used[None, :] & (ids[None, :] > ids[:, None])
    nxt = jnp.where(jnp.any(later, axis=1), jnp.argmax(later, axis=1), -1).astype(I32)
    slot = (seq & 1).astype(I32)
    hit = ids[None, :] == be[:, None]
    left = jnp.sum(jnp.where(hit, (pstart + counts)[None, :], 0), axis=1) - blk * bm
    part = bm // EXPERT_TAIL_PARTS
    parts = jnp.where(blk < nb_used,
                      jnp.minimum((left + part - 1) // part, EXPERT_TAIL_PARTS), 0).astype(I32)
    return pstart, nb_used.reshape(1), be, first, parts, nxt, slot


def _experts(be, first, parts, nxt, slot, nb_used, xs, w_gu, b_gu, w_down, b_down, bm):
    rows, dw = xs.shape
    ne, d, dff2 = w_gu.shape
    dff = dff2 // 2
    sb = EXPERT_BLOCKS_PER_STEP * bm
    assert rows % sb == 0
    last = lambda j, be, fi, ha, nx, sl, nb: (
        jnp.maximum(jnp.minimum(j, (nb[0] - 1) // EXPERT_BLOCKS_PER_STEP), 0), 0)
    const = lambda *shape: pl.BlockSpec(shape,
                                        lambda j, be, fi, ha, nx, sl, nb: (0,) * len(shape))
    grid_spec = pltpu.PrefetchScalarGridSpec(
        num_scalar_prefetch=6,
        grid=(rows // sb,),
        in_specs=[
            pl.BlockSpec((sb, dw), last),
            const(ne, 1, dff2),
            const(ne, 1, d),
            pl.BlockSpec(memory_space=pl.ANY),
            pl.BlockSpec(memory_space=pl.ANY),
        ],
        out_specs=pl.BlockSpec((sb, dw), lambda j, be, fi, ha, nx, sl, nb: (j, 0)),
        scratch_shapes=[pltpu.VMEM((2, d, dff2), F32), pltpu.VMEM((2, dff, d), F32),
                        pltpu.VMEM((d, dff2), BF16), pltpu.VMEM((dff, d), BF16),
                        pltpu.SemaphoreType.DMA((2, 2))],
    )
    return pl.pallas_call(
        _experts_kernel,
        grid_spec=grid_spec,
        out_shape=jax.ShapeDtypeStruct((rows, dw), jnp.uint32),
        compiler_params=_cparams(1),
        name="experts",
    )(be, first, parts, nxt, slot, nb_used, xs, b_gu.reshape(ne, 1, dff2),
      b_down.reshape(ne, 1, d), w_gu, w_down)


MIXER_TILE = 512
ATTN_HEADS_PER_STEP = 4
POST_TILE = 512
DEST_TILE = 2048
COMBINE_TILE = 1024
GATHER_PARTS = 4
EXPERT_BLOCK = 1024
EXPERT_TAIL_PARTS = 4
EXPERT_BLOCKS_PER_STEP = 1


def kernel(x, mem, norm_mix_g, w_in, lambda_q1, lambda_k1, lambda_q2, lambda_k2, rel_bias,
           subln_g, w_attn_up, pool_mix, pool_scale, w_pool_up, w_gate, b_gate, w_out,
           norm_x_g, norm_mem_g, w_xq, w_xkv, w_xo, norm_ffn_g, w_router, b_router,
           w_gu, b_gu, w_down, b_down, final_norm_g):
    b, s, d = x.shape
    n = b * s
    assert w_in.shape[0] == 1, "single-layer block"
    row = lambda a: a.reshape(1, -1)
    bf = lambda a: a[0].astype(BF16)

    q, k, v, g0, gyb = _mixer_in(
        x, row(norm_mix_g[0]), bf(w_in), bf(w_gate), row(b_gate[0]), bf(pool_mix),
        row(pool_scale[0]), bf(w_pool_up), MIXER_TILE)

    bias, lam = _rel_bias(rel_bias, row(lambda_q1[0]), row(lambda_k1[0]),
                          row(lambda_q2[0]), row(lambda_k2[0]), s)
    a = _diff_attn(q, k, v, bias, lam, row(subln_g[0]), ATTN_HEADS_PER_STEP)

    qk_fold, vo_fold = _mem_fold(mem, row(norm_mem_g[0]), bf(w_xkv), bf(w_xq), bf(w_xo))

    x2, hf, eidx, rank, gate_tm, counts = _post(
        x.reshape(n, d), a.reshape(n, d), g0, gyb, bf(w_attn_up), bf(w_out),
        row(norm_x_g[0]), qk_fold, vo_fold, row(norm_ffn_g[0]),
        w_router[0].T, b_router[0].reshape(-1, 1), b, POST_TILE)

    bm = EXPERT_BLOCK
    rows = n * TOP_K + N_EXPERTS * bm
    pstart, nb_used, be, first, parts, nxt, slot = _expert_schedule(
        counts[:, 0], bm, rows // bm)

    dest = _route_dest(pstart, eidx, rank, DEST_TILE)
    xs = _sc_dispatch(hf, _worker_index_layout(dest, n), rows)
    ys = _experts(be, first, parts, nxt, slot, nb_used, xs, w_gu[0], b_gu[0], w_down[0],
                  b_down[0], bm)
    n_part = n // GATHER_PARTS
    out = None
    for p in range(GATHER_PARTS):
        dest_p = _worker_index_layout(dest[:, p * n_part:(p + 1) * n_part], n_part)
        yg = _sc_gather(ys, dest_p, n_part)
        out = _combine(x2, gate_tm, row(final_norm_g), yg, COMBINE_TILE, p, GATHER_PARTS, out)
    return out.reshape(b, s, d)
```

```python
import functools
import math

import numpy as np
import jax
import jax.numpy as jnp
from jax import lax
from jax.experimental import pallas as pl
from jax.experimental.pallas import tpu as pltpu
from jax.experimental.pallas import tpu_sc as plsc

F32 = jnp.float32
BF16 = jnp.bfloat16
I32 = jnp.int32

EPS = 1e-6
CHUNK = 64
N_HEADS = 8
HEAD_DIM = 64
HEAD_W = 2 * HEAD_DIM
POOL_WINDOWS = (2, 4, 8, 16)
POOL_GROUP = 128
POOL_PAD = 16
N_BUCKETS = 32
MAX_DISTANCE = 128
X_HEADS = 4
N_EXPERTS = 32
TOP_K = 4
SWIGLU_ALPHA = 1.702
SWIGLU_LIMIT = 7.0
LAMBDA_INIT = 0.8 - 0.6 * math.exp(-0.3 * 0)

LANES = 128
SUBLANES = 8
ATTN_BLOCK = 256
VMEM_LIMIT = 56 * 1024 * 1024


def _cparams(n_axes, vmem=VMEM_LIMIT):
    return pltpu.CompilerParams(
        dimension_semantics=("arbitrary",) * n_axes, vmem_limit_bytes=vmem)


def _rms(xf, g):
    ms = jnp.mean(xf * xf, axis=-1, keepdims=True)
    return xf * lax.rsqrt(ms + EPS) * g


def _dot(a, b):
    return jnp.dot(a, b, preferred_element_type=F32)


def _pack_halves(x):
    c = x.shape[1] // 2
    as_bits = lambda v: lax.bitcast_convert_type(v.astype(BF16).astype(F32), jnp.uint32)
    return as_bits(x[:, :c]) | (as_bits(x[:, c:]) >> 16)


def _unpack_halves(w):
    hi = lax.bitcast_convert_type(w & jnp.uint32(0xFFFF0000), F32)
    lo = lax.bitcast_convert_type(w << 16, F32)
    return hi, lo


def _dot_nt(a, b, precision=None):
    return lax.dot_general(a, b, (((1,), (1,)), ((), ())),
                           preferred_element_type=F32, precision=precision)


def _mixer_in_kernel(x_ref, g_ref, wq_ref, wk_ref, wv_ref, wu_ref, wg_ref, bg_ref,
                     pmix_ref, pscale_ref, wpu_ref,
                     q_ref, k_ref, v_ref, g0_ref, gyb_ref, ext_ref, wpool_ref):
    ts = x_ref.shape[1]
    d = x_ref.shape[2]
    j = pl.program_id(1)

    @pl.when((pl.program_id(0) == 0) & (j == 0))
    def _():
        for gi in range(len(POOL_WINDOWS)):
            sl = slice(gi * POOL_GROUP, (gi + 1) * POOL_GROUP)
            mix = (pmix_ref[gi].astype(F32) * pscale_ref[:, sl]).astype(BF16)
            wpool_ref[sl, :] = _dot(mix, wpu_ref[sl, :]).astype(BF16)

    h = _rms(x_ref[0], g_ref[...]).astype(BF16)
    q_ref[0] = (_dot(h, wq_ref[...]) * (HEAD_DIM ** -0.5)).astype(BF16)
    k_ref[0] = _dot(h, wk_ref[...]).astype(BF16)
    v_ref[0] = _dot(h, wv_ref[...]).astype(BF16)
    u = _dot(h, wu_ref[...])

    @pl.when(j == 0)
    def _():
        ext_ref[0:POOL_PAD, :] = jnp.zeros((POOL_PAD, u.shape[1]), F32)

    ext_ref[POOL_PAD:POOL_PAD + ts, :] = u
    e = ext_ref[...]
    sums = {}
    s = e
    w = 1
    while w < max(POOL_WINDOWS):
        s = s + pltpu.roll(s, w, 0)
        w *= 2
        sums[w] = s
    ext_ref[0:POOL_PAD, :] = ext_ref[ts:ts + POOL_PAD, :]

    pos = (j * ts + lax.broadcasted_iota(I32, (ts, 1), 0) + 1).astype(F32)
    pooled = []
    for gi, w in enumerate(POOL_WINDOWS):
        sl = slice(gi * POOL_GROUP, (gi + 1) * POOL_GROUP)
        win = sums[w][POOL_PAD:, sl]
        pooled.append(win / jnp.minimum(pos, float(w)) - u[:, sl])
    y_b = _dot(jnp.concatenate(pooled, axis=1).astype(BF16), wpool_ref[...])

    gate = jax.nn.sigmoid(_dot(h, wg_ref[...]) + bg_ref[...])
    g0_ref[...] = gate[:, :d].astype(BF16)
    gyb_ref[...] = (gate[:, d:] * y_b).astype(BF16)


def _mixer_in(x, norm_g, w_in, w_gate, b_gate, pool_mix, pool_scale, w_pool_up, ts):
    b, s, d = x.shape
    aw = N_HEADS * HEAD_W
    pw = len(POOL_WINDOWS) * POOL_GROUP
    n = b * s
    nt = s // ts
    const = lambda *shape: pl.BlockSpec(shape, lambda bi, j: (0,) * len(shape))
    tok3 = pl.BlockSpec((1, ts, aw), lambda bi, j: (bi, j, 0))
    tok2 = pl.BlockSpec((ts, d), lambda bi, j: (bi * nt + j, 0))
    return pl.pallas_call(
        _mixer_in_kernel,
        grid=(b, nt),
        in_specs=[
            pl.BlockSpec((1, ts, d), lambda bi, j: (bi, j, 0)),
            const(1, d),
            pl.BlockSpec((d, aw), lambda bi, j: (0, 0)),
            pl.BlockSpec((d, aw), lambda bi, j: (0, 1)),
            pl.BlockSpec((d, aw), lambda bi, j: (0, 2)),
            pl.BlockSpec((d, pw), lambda bi, j: (0, 3 * aw // pw)),
            const(d, 2 * d),
            const(1, 2 * d),
            const(len(POOL_WINDOWS), POOL_GROUP, POOL_GROUP),
            const(1, pw),
            const(pw, d),
        ],
        out_specs=[tok3, tok3, tok3, tok2, tok2],
        out_shape=[jax.ShapeDtypeStruct((b, s, aw), BF16)] * 3
        + [jax.ShapeDtypeStruct((n, d), BF16)] * 2,
        scratch_shapes=[pltpu.VMEM((ts + POOL_PAD, pw), F32), pltpu.VMEM((pw, d), BF16)],
        compiler_params=_cparams(2),
        name="mixer_in",
    )(x, norm_g, w_in, w_in, w_in, w_in, w_gate, b_gate, pool_mix, pool_scale, w_pool_up)


def _rel_bucket(rel, log=jnp.log, f32=lambda a: a.astype(jnp.float32),
                i32=lambda a: a.astype(jnp.int32), xp=jnp):
    nb = N_BUCKETS // 2
    ret = i32(rel > 0) * nb
    n = xp.abs(rel)
    max_exact = nb // 2
    nf = f32(xp.maximum(n, 1))
    large = max_exact + i32(log(nf / max_exact) / math.log(MAX_DISTANCE / max_exact)
                            * (nb - max_exact))
    large = xp.minimum(large, nb - 1)
    return ret + xp.where(n < max_exact, n, large)


def _far_bucket(block, seq):
    rel = -np.arange(block + 1, max(seq, block + 2), dtype=np.int32)
    bk = _rel_bucket(rel, log=np.log, f32=lambda a: a.astype(np.float32),
                     i32=lambda a: a.astype(np.int32), xp=np)
    assert (bk == bk[0]).all(), "far keys must share one relative-position bucket"
    return int(bk[0])


def _rel_bias_kernel(far_bucket, tab_ref, bidx_ref, lq1_ref, lk1_ref, lq2_ref, lk2_ref,
                     bias_ref, lam_ref):
    h = pl.program_id(0)
    bidx = bidx_ref[...]
    acc = jnp.zeros(bidx.shape, F32)
    for bkt in range(N_BUCKETS):
        acc = jnp.where(bidx == bkt, tab_ref[bkt, h], acc)
    acc = acc - tab_ref[far_bucket, h]
    bias_ref[0] = jnp.where(bidx < 0, -jnp.inf, acc)
    lam = (jnp.exp(jnp.sum(lq1_ref[...] * lk1_ref[...], keepdims=True))
           - jnp.exp(jnp.sum(lq2_ref[...] * lk2_ref[...], keepdims=True)) + LAMBDA_INIT)
    lam_ref[...] = jnp.broadcast_to(lam, lam_ref.shape)


def _diff_attn_kernel(q_ref, k_ref, v_ref, bias_ref, lam_ref, sg_ref, o_ref, vext_ref):
    s_len = q_ref.shape[1]
    tq = ATTN_BLOCK
    n_heads = q_ref.shape[2] // HEAD_W
    lam = lam_ref[0:1, 0:1]
    lane = lax.broadcasted_iota(I32, (tq, HEAD_W), 1)
    cols = [slice(hh * HEAD_W, (hh + 1) * HEAD_W) for hh in range(n_heads)]
    b_diag, b_prev = [], []
    for hh in range(n_heads):
        vext_ref[hh, :, :HEAD_W] = v_ref[0, :, cols[hh]]
        vext_ref[hh, :, HEAD_W:] = jnp.ones((s_len, HEAD_W), BF16)
        b_diag.append(jnp.concatenate([bias_ref[hh, 0]] * 2, axis=0))
        b_prev.append(jnp.concatenate([bias_ref[hh, 1]] * 2, axis=0))

    for qi in range(s_len // tq):
        n_keys = (qi + 1) * tq
        rows = slice(qi * tq, (qi + 1) * tq)
        for hh in range(n_heads):
            q = q_ref[0, rows, cols[hh]]
            zero = jnp.zeros_like(q)
            qs = jnp.concatenate([jnp.where(lane < HEAD_DIM, q, zero),
                                  jnp.where(lane >= HEAD_DIM, q, zero)], axis=0)
            s = _dot_nt(qs, k_ref[0, :n_keys, cols[hh]])
            pieces = []
            if qi >= 2:
                pieces.append(s[:, :n_keys - 2 * tq])
            if qi >= 1:
                pieces.append(s[:, n_keys - 2 * tq:n_keys - tq] + b_prev[hh])
            pieces.append(s[:, n_keys - tq:] + b_diag[hh])
            s = jnp.concatenate(pieces, axis=1) if len(pieces) > 1 else pieces[0]
            m = jnp.max(s, axis=-1, keepdims=True)
            p = jnp.exp(s - m).astype(BF16)
            acc = _dot(p, vext_ref[hh, :n_keys, :])
            o = acc[:, :HEAD_W] / acc[:, HEAD_W:HEAD_W + 1]
            a = o[:tq] - lam * o[tq:]
            o_ref[0, rows, cols[hh]] = (
                _rms(a, sg_ref[...]) * (1.0 - LAMBDA_INIT)).astype(BF16)


def _diff_attn(q, k, v, bias, lam, subln_g, heads_per_step):
    b, s, aw = q.shape
    tq = ATTN_BLOCK
    hp = heads_per_step
    spec = pl.BlockSpec((1, s, hp * HEAD_W), lambda bi, h: (bi, 0, h))
    return pl.pallas_call(
        _diff_attn_kernel,
        grid=(b, N_HEADS // hp),
        in_specs=[
            spec, spec, spec,
            pl.BlockSpec((hp, 2, tq, tq), lambda bi, h: (h, 0, 0, 0)),
            pl.BlockSpec((SUBLANES, LANES), lambda bi, h: (0, 0)),
            pl.BlockSpec((1, HEAD_W), lambda bi, h: (0, 0)),
        ],
        out_specs=spec,
        out_shape=jax.ShapeDtypeStruct((b, s, aw), BF16),
        scratch_shapes=[pltpu.VMEM((hp, s, 2 * HEAD_W), BF16)],
        compiler_params=_cparams(2),
        name="diff_attn",
    )(q, k, v, bias, lam, subln_g)


def _mem_fold_kernel(m_ref, g_ref, wkv_ref, wq_ref, wo_ref, qk_ref, vo_ref):
    d = m_ref.shape[2]
    hd = d // X_HEADS
    m = _rms(m_ref[0], g_ref[...]).astype(BF16)
    kv = _dot(m, wkv_ref[...])
    k = kv[:, :d].astype(BF16)
    v = kv[:, d:].astype(BF16)
    for hh in range(X_HEADS):
        sl = slice(hh * hd, (hh + 1) * hd)
        qk_ref[0, :, hh * m.shape[0]:(hh + 1) * m.shape[0]] = (
            _dot_nt(wq_ref[:, sl], k[:, sl]) * (hd ** -0.5)).astype(BF16)
        vo_ref[0, hh * m.shape[0]:(hh + 1) * m.shape[0], :] = _dot(v[:, sl], wo_ref[sl, :]).astype(BF16)


def _side_tables_kernel(far_bucket, tab_ref, bidx_ref, lq1_ref, lk1_ref, lq2_ref, lk2_ref,
                        m_ref, g_ref, wkv_ref, wq_ref, wo_ref,
                        bias_ref, lam_ref, qk_ref, vo_ref):
    _rel_bias_kernel(far_bucket, tab_ref, bidx_ref, lq1_ref, lk1_ref, lq2_ref, lk2_ref,
                     bias_ref, lam_ref)
    _mem_fold_kernel(m_ref, g_ref, wkv_ref, wq_ref, wo_ref, qk_ref, vo_ref)


def _side_tables(rel_bias, lq1, lk1, lq2, lk2, seq, mem, norm_g, w_xkv, w_xq, w_xo):
    b, ml, d = mem.shape
    assert b == N_HEADS, "one grid step per head and per batch element"
    blk = ATTN_BLOCK
    qpos = jnp.arange(blk, dtype=I32)[:, None]
    kpos = jnp.arange(blk, dtype=I32)[None, :]
    diag = jnp.where(kpos // CHUNK <= qpos // CHUNK, _rel_bucket(kpos - qpos), -1)
    prev = _rel_bucket(kpos - (qpos + blk))
    bidx = jnp.stack([diag, prev]).astype(I32)
    const = lambda *shape: pl.BlockSpec(shape, lambda i: (0,) * len(shape))
    return pl.pallas_call(
        functools.partial(_side_tables_kernel, _far_bucket(blk, seq)),
        grid=(b,),
        in_specs=[
            pl.BlockSpec(memory_space=pltpu.SMEM),
            const(2, blk, blk),
            const(1, HEAD_DIM), const(1, HEAD_DIM), const(1, HEAD_DIM), const(1, HEAD_DIM),
            pl.BlockSpec((1, ml, d), lambda i: (i, 0, 0)), const(1, d),
            const(d, 2 * d), const(d, d), const(d, d),
        ],
        out_specs=[
            pl.BlockSpec((1, 2, blk, blk), lambda i: (i, 0, 0, 0)),
            const(SUBLANES, LANES),
            pl.BlockSpec((1, d, X_HEADS * ml), lambda i: (i, 0, 0)),
            pl.BlockSpec((1, X_HEADS * ml, d), lambda i: (i, 0, 0)),
        ],
        out_shape=[jax.ShapeDtypeStruct((N_HEADS, 2, blk, blk), F32),
                   jax.ShapeDtypeStruct((SUBLANES, LANES), F32),
                   jax.ShapeDtypeStruct((b, d, X_HEADS * ml), BF16),
                   jax.ShapeDtypeStruct((b, X_HEADS * ml, d), BF16)],
        compiler_params=_cparams(1),
        name="side_tables",
    )(rel_bias, bidx, lq1, lk1, lq2, lk2, mem, norm_g, w_xkv, w_xq, w_xo)


def _post_kernel(x_ref, a_ref, g0_ref, gyb_ref, wau_ref, wo_ref, nxg_ref,
                 qk_ref, vo_ref, nfg_ref, wrt_ref, br_ref,
                 x2_ref, hf_ref, eidx_ref, rank_ref, gate_ref, cnt_ref, run_ref):
    ts, d = x_ref.shape
    first = (pl.program_id(0) == 0) & (pl.program_id(1) == 0)

    y_a = _dot(a_ref[...], wau_ref[...])
    merged = g0_ref[...].astype(F32) * y_a + gyb_ref[...].astype(F32)
    x1 = x_ref[...] + _dot(merged.astype(BF16), wo_ref[...])

    ml = qk_ref.shape[2] // X_HEADS
    hx = _rms(x1, nxg_ref[...]).astype(BF16)
    s_all = _dot(hx, qk_ref[0])
    probs = []
    for hh in range(X_HEADS):
        s = s_all[:, hh * ml:(hh + 1) * ml]
        p = jnp.exp(s - jnp.max(s, axis=-1, keepdims=True))
        probs.append((p / jnp.sum(p, axis=-1, keepdims=True)).astype(BF16))
    x2 = x1 + _dot(jnp.concatenate(probs, axis=1), vo_ref[0])
    x2_ref[...] = x2
    hf = _rms(x2, nfg_ref[...])
    hf_ref[...] = _pack_halves(hf)

    logits = _dot_nt(wrt_ref[...], hf, precision=lax.Precision.HIGHEST) + br_ref[...]
    ne = logits.shape[0]
    eid = lax.broadcasted_iota(I32, logits.shape, 0).astype(F32)
    work = logits
    vals, idxs, hots = [], [], []
    for _ in range(TOP_K):
        mx = jnp.max(work, axis=0, keepdims=True)
        idx = jnp.min(jnp.where(work == mx, eid, float(ne)), axis=0, keepdims=True)
        hot = eid == idx
        vals.append(mx)
        idxs.append(idx.astype(I32))
        hots.append(hot)
        work = jnp.where(hot, -jnp.inf, work)
    ex = [jnp.exp(vv - vals[0]) for vv in vals]
    den = ex[0] + ex[1] + ex[2] + ex[3]
    gates = [e_ / den for e_ in ex]

    multi = (hots[0] | hots[1] | hots[2] | hots[3])
    multi_f = jnp.where(multi, 1.0, 0.0).astype(F32)
    tri = (lax.broadcasted_iota(I32, (ts, ts), 0)
           < lax.broadcasted_iota(I32, (ts, ts), 1))
    before = _dot(multi_f.astype(BF16), jnp.where(tri, 1.0, 0.0).astype(BF16))

    @pl.when(first)
    def _():
        run_ref[...] = jnp.zeros(run_ref.shape, F32)

    run = run_ref[...]
    pos = before + run
    ranks = [jnp.sum(jnp.where(hot, pos, 0.0), axis=0, keepdims=True) for hot in hots]
    run_new = run + jnp.sum(multi_f, axis=1, keepdims=True)
    run_ref[...] = run_new
    cnt_ref[...] = jnp.broadcast_to(run_new, cnt_ref.shape).astype(I32)

    eidx_ref[...] = jnp.concatenate(idxs, axis=0)
    rank_ref[...] = jnp.concatenate(ranks, axis=0).astype(I32)
    g_rows = jnp.concatenate(gates + [jnp.zeros((LANES - TOP_K, ts), F32)], axis=0)
    gate_ref[...] = g_rows.T


def _post(x2d, a2d, g0, gyb, w_attn_up, w_out, norm_x_g, qk_fold, vo_fold,
          norm_ffn_g, w_router_t, b_router, batch, ts):
    n, d = x2d.shape
    nt = n // batch // ts
    tok = pl.BlockSpec((ts, d), lambda bi, j: (bi * nt + j, 0))
    lanes = pl.BlockSpec((TOP_K, ts), lambda bi, j: (0, bi * nt + j))
    const = lambda *shape: pl.BlockSpec(shape, lambda bi, j: (0,) * len(shape),
                                        pipeline_mode=pl.Buffered(1))
    fold = lambda arr: pl.BlockSpec((1,) + arr.shape[1:], lambda bi, j: (bi, 0, 0))
    return pl.pallas_call(
        _post_kernel,
        grid=(batch, nt),
        in_specs=[tok, tok, tok, tok, const(d, d), const(d, d), const(1, d),
                  fold(qk_fold), fold(vo_fold), const(1, d), const(N_EXPERTS, d),
                  const(N_EXPERTS, 1)],
        out_specs=[tok, pl.BlockSpec((ts, d // 2), lambda bi, j: (bi * nt + j, 0)),
                   lanes, lanes,
                   pl.BlockSpec((ts, LANES), lambda bi, j: (bi * nt + j, 0)),
                   pl.BlockSpec((N_EXPERTS, LANES), lambda bi, j: (0, 0))],
        out_shape=[jax.ShapeDtypeStruct((n, d), F32),
                   jax.ShapeDtypeStruct((n, d // 2), jnp.uint32),
                   jax.ShapeDtypeStruct((TOP_K, n), I32), jax.ShapeDtypeStruct((TOP_K, n), I32),
                   jax.ShapeDtypeStruct((n, LANES), F32),
                   jax.ShapeDtypeStruct((N_EXPERTS, LANES), I32)],
        scratch_shapes=[pltpu.VMEM((N_EXPERTS, 1), F32)],
        compiler_params=_cparams(2),
        name="post",
    )(x2d, a2d, g0, gyb, w_attn_up, w_out, norm_x_g, qk_fold, vo_fold,
      norm_ffn_g, w_router_t, b_router)


def _route_dest_kernel(pstart_ref, eidx_ref, rank_ref, dest_ref):
    eidx = eidx_ref[...]
    start = jnp.zeros(eidx.shape, I32)
    for e in range(N_EXPERTS):
        start = jnp.where(eidx == e, pstart_ref[e], start)
    dest_ref[...] = start + rank_ref[...]


def _route_dest(pstart, eidx, rank, tl):
    k, n = eidx.shape
    lanes = pl.BlockSpec((k, tl), lambda i: (0, i))
    return pl.pallas_call(
        _route_dest_kernel,
        grid=(n // tl,),
        in_specs=[pl.BlockSpec(memory_space=pltpu.SMEM), lanes, lanes],
        out_specs=lanes,
        out_shape=jax.ShapeDtypeStruct((k, n), I32),
        compiler_params=_cparams(1),
        name="route_dest",
    )(pstart, eidx, rank)


SC_CORES = 2
SC_SUBCORES = 16
SC_WORKERS = SC_CORES * SC_SUBCORES
MOVE_CHUNK = 64
MOVE_BUFFERS = 3


def _worker_index_layout(dest, n):
    per_w = n // SC_WORKERS
    n_chunks = per_w // MOVE_CHUNK
    d4 = dest.reshape(TOP_K, SC_WORKERS, n_chunks, MOVE_CHUNK)
    return jnp.transpose(d4, (1, 2, 0, 3)).reshape(SC_WORKERS, n_chunks * TOP_K, MOVE_CHUNK)


def _sc_mesh():
    return plsc.VectorSubcoreMesh(core_axis_name="c", subcore_axis_name="s",
                                  num_cores=SC_CORES, num_subcores=SC_SUBCORES)


def _sc_dispatch(hf, dest_w, rows):
    n, d = hf.shape
    per_w = n // SC_WORKERS
    n_chunks = per_w // MOVE_CHUNK

    def body(hf_hbm, dest_hbm, xs_hbm, idx_v, rows_v, rsem, wsem):
        wid = lax.axis_index("s") * SC_CORES + lax.axis_index("c")
        base = wid * per_w
        pltpu.sync_copy(dest_hbm.at[wid], idx_v)

        def read(c):
            b = c % MOVE_BUFFERS
            return pltpu.async_copy(hf_hbm.at[pl.ds(base + c * MOVE_CHUNK, MOVE_CHUNK)],
                                    rows_v.at[b], rsem.at[b])

        scatters = [[] for _ in range(MOVE_BUFFERS)]
        pending = read(0)
        for c in range(n_chunks):
            b = c % MOVE_BUFFERS
            pending.wait()
            if c + 1 < n_chunks:
                for cp in scatters[(c + 1) % MOVE_BUFFERS]:
                    cp.wait()
                pending = read(c + 1)
            scatters[b] = [
                pltpu.async_copy(rows_v.at[b], xs_hbm.at[idx_v.at[c * TOP_K + kk]], wsem.at[b])
                for kk in range(TOP_K)]
        for group in scatters:
            for cp in group:
                cp.wait()

    return pl.kernel(
        body,
        out_type=jax.ShapeDtypeStruct((rows, d), hf.dtype),
        mesh=_sc_mesh(),
        scratch_types=[pltpu.VMEM((n_chunks * TOP_K, MOVE_CHUNK), I32),
                       pltpu.VMEM((MOVE_BUFFERS, MOVE_CHUNK, d), hf.dtype),
                       pltpu.SemaphoreType.DMA((MOVE_BUFFERS,)),
                       pltpu.SemaphoreType.DMA((MOVE_BUFFERS,))],
        name="sc_dispatch",
    )(hf, dest_w)


def _sc_gather(ys, dest_w, n):
    _, d = ys.shape
    per_w = n // SC_WORKERS
    n_chunks = per_w // MOVE_CHUNK

    def body(ys_hbm, dest_hbm, yg_hbm, idx_v, rows_v, rsem, wsem):
        wid = lax.axis_index("s") * SC_CORES + lax.axis_index("c")
        base = wid * per_w
        pltpu.sync_copy(dest_hbm.at[wid], idx_v)
        n_moves = n_chunks * TOP_K

        def gather(m):
            b = m % MOVE_BUFFERS
            return pltpu.async_copy(ys_hbm.at[idx_v.at[m]], rows_v.at[b], rsem.at[b])

        def write(m):
            b = m % MOVE_BUFFERS
            c, kk = divmod(m, TOP_K)
            return pltpu.async_copy(
                rows_v.at[b], yg_hbm.at[pl.ds(kk * n + base + c * MOVE_CHUNK, MOVE_CHUNK)],
                wsem.at[b])

        writes = [None] * MOVE_BUFFERS
        pending = gather(0)
        for m in range(n_moves):
            pending.wait()
            if m + 1 < n_moves:
                nb = (m + 1) % MOVE_BUFFERS
                if writes[nb] is not None:
                    writes[nb].wait()
                pending = gather(m + 1)
            writes[m % MOVE_BUFFERS] = write(m)
        for wr in writes:
            if wr is not None:
                wr.wait()

    return pl.kernel(
        body,
        out_type=jax.ShapeDtypeStruct((TOP_K * n, d), ys.dtype),
        mesh=_sc_mesh(),
        scratch_types=[pltpu.VMEM((n_chunks * TOP_K, MOVE_CHUNK), I32),
                       pltpu.VMEM((MOVE_BUFFERS, MOVE_CHUNK, d), ys.dtype),
                       pltpu.SemaphoreType.DMA((MOVE_BUFFERS,)),
                       pltpu.SemaphoreType.DMA((MOVE_BUFFERS,))],
        name="sc_gather",
    )(ys, dest_w)


def _combine_kernel(x2_ref, gate_ref, fg_ref, yg_ref, o_ref):
    g = gate_ref[...]
    x2 = x2_ref[...]
    c = x2.shape[1] // 2
    acc_hi, acc_lo = x2[:, :c], x2[:, c:]
    for kk in range(TOP_K):
        hi, lo = _unpack_halves(yg_ref[kk])
        acc_hi = acc_hi + g[:, kk:kk + 1] * hi
        acc_lo = acc_lo + g[:, kk:kk + 1] * lo
    o_ref[...] = _rms(jnp.concatenate([acc_hi, acc_lo], axis=1), fg_ref[...])


def _combine(x2, gate_tm, final_g, yg, ts):
    n, d = x2.shape
    dw = yg.shape[1]
    return pl.pallas_call(
        _combine_kernel,
        grid=(n // ts,),
        in_specs=[pl.BlockSpec((ts, d), lambda i: (i, 0)),
                  pl.BlockSpec((ts, LANES), lambda i: (i, 0)),
                  pl.BlockSpec((1, d), lambda i: (0, 0)),
                  pl.BlockSpec((TOP_K, ts, dw), lambda i: (0, i, 0))],
        out_specs=pl.BlockSpec((ts, d), lambda i: (i, 0)),
        out_shape=jax.ShapeDtypeStruct((n, d), F32),
        compiler_params=_cparams(1),
        name="combine",
    )(x2, gate_tm, final_g, yg.reshape(TOP_K, n, dw))


def _experts_kernel(be_ref, first_ref, parts_ref, nxt_ref, slot_ref, nb_ref,
                    xs_ref, bgu_ref, bd_ref, wgu_hbm, wd_hbm, ys_ref,
                    wgu_st, wd_st, wgu_bf, wd_bf, sem_w):
    dff = wd_bf.shape[0]
    bm = xs_ref.shape[0] // EXPERT_BLOCKS_PER_STEP

    def fetch(ex, sl):
        return (pltpu.make_async_copy(wgu_hbm.at[ex], wgu_st.at[sl], sem_w.at[0, sl]),
                pltpu.make_async_copy(wd_hbm.at[ex], wd_st.at[sl], sem_w.at[1, sl]))

    @pl.when(pl.program_id(0) == 0)
    def _():
        e0 = be_ref[0]
        for cp in fetch(e0, slot_ref[e0]):
            cp.start()

    for sub in range(EXPERT_BLOCKS_PER_STEP):
        blk = pl.program_id(0) * EXPERT_BLOCKS_PER_STEP + sub
        e = be_ref[blk]
        rows = slice(sub * bm, (sub + 1) * bm)

        @pl.when(first_ref[blk] == 1)
        def _(e=e):
            sl = slot_ref[e]
            for cp in fetch(e, sl):
                cp.wait()
            nx = nxt_ref[e]

            @pl.when(nx >= 0)
            def _():
                for cp in fetch(nx, 1 - sl):
                    cp.start(priority=1)

            wgu_bf[...] = wgu_st[sl].astype(BF16)
            wd_bf[...] = wd_st[sl].astype(BF16)

        def mlp(e, rows):
            xb = jnp.concatenate(_unpack_halves(xs_ref[rows, :]), axis=1).astype(BF16)
            gu = _dot(xb, wgu_bf[...]) + bgu_ref[e]
            x_glu = jnp.minimum(gu[:, :dff], SWIGLU_LIMIT)
            x_lin = jnp.clip(gu[:, dff:], -SWIGLU_LIMIT, SWIGLU_LIMIT)
            act = x_glu * jax.nn.sigmoid(SWIGLU_ALPHA * x_glu) * (x_lin + 1.0)
            ys_ref[rows, :] = _pack_halves(_dot(act.astype(BF16), wd_bf[...]) + bd_ref[e])

        def clear(rows):
            ys_ref[rows, :] = jnp.zeros((rows.stop - rows.start, ys_ref.shape[1]), jnp.uint32)

        part = bm // EXPERT_TAIL_PARTS
        for live in range(EXPERT_TAIL_PARTS + 1):
            @pl.when(parts_ref[blk] == live)
            def _(e=e, rows=rows, live=live):
                cut = rows.start + live * part
                if live:
                    mlp(e, slice(rows.start, cut))
                if live < EXPERT_TAIL_PARTS:
                    clear(slice(cut, rows.stop))


def _expert_schedule(counts, bm, n_blocks):
    ne = counts.shape[0]
    ids = jnp.arange(ne, dtype=I32)
    upto = ids[None, :] <= ids[:, None]
    padded = (counts + bm - 1) // bm * bm
    pend = jnp.sum(jnp.where(upto, padded[None, :], 0), axis=1)
    pstart = (pend - padded).astype(I32)
    nb_used = (jnp.sum(padded) // bm).astype(I32)
    blk = jnp.arange(n_blocks, dtype=I32)
    blk_c = jnp.minimum(blk, nb_used - 1)
    be = jnp.minimum(jnp.sum((pend[None, :] <= (blk_c * bm)[:, None]).astype(I32), axis=1),
                     ne - 1)
    started = jnp.sum(((pstart[None, :] == (blk * bm)[:, None]) & (padded[None, :] > 0))
                      .astype(I32), axis=1)
    first = ((blk < nb_used) & (started > 0)).astype(I32)
    used = counts > 0
    seq = jnp.sum(jnp.where(upto & used[None, :], 1, 0), axis=1) - 1
    later = used[None, :] & (ids[None, :] > ids[:, None])
    nxt = jnp.where(jnp.any(later, axis=1), jnp.argmax(later, axis=1), -1).astype(I32)
    slot = (seq & 1).astype(I32)
    hit = ids[None, :] == be[:, None]
    left = jnp.sum(jnp.where(hit, (pstart + counts)[None, :], 0), axis=1) - blk * bm
    part = bm // EXPERT_TAIL_PARTS
    parts = jnp.where(blk < nb_used,
                      jnp.minimum((left + part - 1) // part, EXPERT_TAIL_PARTS), 0).astype(I32)
    return pstart, nb_used.reshape(1), be, first, parts, nxt, slot


def _experts(be, first, parts, nxt, slot, nb_used, xs, w_gu, b_gu, w_down, b_down, bm):
    rows, dw = xs.shape
    ne, d, dff2 = w_gu.shape
    dff = dff2 // 2
    sb = EXPERT_BLOCKS_PER_STEP * bm
    assert rows % sb == 0
    last = lambda j, be, fi, ha, nx, sl, nb: (
        jnp.maximum(jnp.minimum(j, (nb[0] - 1) // EXPERT_BLOCKS_PER_STEP), 0), 0)
    const = lambda *shape: pl.BlockSpec(shape,
                                        lambda j, be, fi, ha, nx, sl, nb: (0,) * len(shape))
    grid_spec = pltpu.PrefetchScalarGridSpec(
        num_scalar_prefetch=6,
        grid=(rows // sb,),
        in_specs=[
            pl.BlockSpec((sb, dw), last),
            const(ne, 1, dff2),
            const(ne, 1, d),
            pl.BlockSpec(memory_space=pl.ANY),
            pl.BlockSpec(memory_space=pl.ANY),
        ],
        out_specs=pl.BlockSpec((sb, dw), lambda j, be, fi, ha, nx, sl, nb: (j, 0)),
        scratch_shapes=[pltpu.VMEM((2, d, dff2), F32), pltpu.VMEM((2, dff, d), F32),
                        pltpu.VMEM((d, dff2), BF16), pltpu.VMEM((dff, d), BF16),
                        pltpu.SemaphoreType.DMA((2, 2))],
    )
    return pl.pallas_call(
        _experts_kernel,
        grid_spec=grid_spec,
        out_shape=jax.ShapeDtypeStruct((rows, dw), jnp.uint32),
        compiler_params=_cparams(1),
        name="experts",
    )(be, first, parts, nxt, slot, nb_used, xs, b_gu.reshape(ne, 1, dff2),
      b_down.reshape(ne, 1, d), w_gu, w_down)


MIXER_TILE = 512
ATTN_HEADS_PER_STEP = 4
POST_TILE = 512
DEST_TILE = 2048
COMBINE_TILE = 1024
EXPERT_BLOCK = 1024
EXPERT_TAIL_PARTS = 4
EXPERT_BLOCKS_PER_STEP = 1


def kernel(x, mem, norm_mix_g, w_in, lambda_q1, lambda_k1, lambda_q2, lambda_k2, rel_bias,
           subln_g, w_attn_up, pool_mix, pool_scale, w_pool_up, w_gate, b_gate, w_out,
           norm_x_g, norm_mem_g, w_xq, w_xkv, w_xo, norm_ffn_g, w_router, b_router,
           w_gu, b_gu, w_down, b_down, final_norm_g):
    b, s, d = x.shape
    n = b * s
    assert w_in.shape[0] == 1, "single-layer block"
    row = lambda a: a.reshape(1, -1)
    bf = lambda a: a[0].astype(BF16)

    q, k, v, g0, gyb = _mixer_in(
        x, row(norm_mix_g[0]), bf(w_in), bf(w_gate), row(b_gate[0]), bf(pool_mix),
        row(pool_scale[0]), bf(w_pool_up), MIXER_TILE)

    bias, lam, qk_fold, vo_fold = _side_tables(
        rel_bias, row(lambda_q1[0]), row(lambda_k1[0]), row(lambda_q2[0]), row(lambda_k2[0]), s,
        mem, row(norm_mem_g[0]), bf(w_xkv), bf(w_xq), bf(w_xo))
    a = _diff_attn(q, k, v, bias, lam, row(subln_g[0]), ATTN_HEADS_PER_STEP)


    x2, hf, eidx, rank, gate_tm, counts = _post(
        x.reshape(n, d), a.reshape(n, d), g0, gyb, bf(w_attn_up), bf(w_out),
        row(norm_x_g[0]), qk_fold, vo_fold, row(norm_ffn_g[0]),
        w_router[0].T, b_router[0].reshape(-1, 1), b, POST_TILE)

    bm = EXPERT_BLOCK
    rows = n * TOP_K + N_EXPERTS * bm
    pstart, nb_used, be, first, parts, nxt, slot = _expert_schedule(
        counts[:, 0], bm, rows // bm)

    dest_w = _worker_index_layout(_route_dest(pstart, eidx, rank, DEST_TILE), n)
    xs = _sc_dispatch(hf, dest_w, rows)
    ys = _experts(be, first, parts, nxt, slot, nb_used, xs, w_gu[0], b_gu[0], w_down[0],
                  b_down[0], bm)
    yg = _sc_gather(ys, dest_w, n)
    out = _combine(x2, gate_tm, row(final_norm_g), yg, COMBINE_TILE)
    return out.reshape(b, s, d)
```

```python
import functools
import math

import numpy as np
import jax
import jax.numpy as jnp
from jax import lax
from jax.experimental import pallas as pl
from jax.experimental.pallas import tpu as pltpu
from jax.experimental.pallas import tpu_sc as plsc

F32 = jnp.float32
BF16 = jnp.bfloat16
I32 = jnp.int32

EPS = 1e-6
CHUNK = 64
N_HEADS = 8
HEAD_DIM = 64
HEAD_W = 2 * HEAD_DIM
POOL_WINDOWS = (2, 4, 8, 16)
POOL_GROUP = 128
POOL_PAD = 16
N_BUCKETS = 32
MAX_DISTANCE = 128
X_HEADS = 4
N_EXPERTS = 32
TOP_K = 4
SWIGLU_ALPHA = 1.702
SWIGLU_LIMIT = 7.0
LAMBDA_INIT = 0.8 - 0.6 * math.exp(-0.3 * 0)

LANES = 128
SUBLANES = 8
ATTN_BLOCK = 256
VMEM_LIMIT = 56 * 1024 * 1024


def _cparams(n_axes, vmem=VMEM_LIMIT):
    return pltpu.CompilerParams(
        dimension_semantics=("arbitrary",) * n_axes, vmem_limit_bytes=vmem)


def _rms(xf, g):
    ms = jnp.mean(xf * xf, axis=-1, keepdims=True)
    return xf * lax.rsqrt(ms + EPS) * g


def _dot(a, b):
    return jnp.dot(a, b, preferred_element_type=F32)


def _pack_halves(x):
    c = x.shape[1] // 2
    as_bits = lambda v: lax.bitcast_convert_type(v.astype(BF16).astype(F32), jnp.uint32)
    return as_bits(x[:, :c]) | (as_bits(x[:, c:]) >> 16)


def _unpack_halves(w):
    hi = lax.bitcast_convert_type(w & jnp.uint32(0xFFFF0000), F32)
    lo = lax.bitcast_convert_type(w << 16, F32)
    return hi, lo


def _dot_nt(a, b, precision=None):
    return lax.dot_general(a, b, (((1,), (1,)), ((), ())),
                           preferred_element_type=F32, precision=precision)


def _mixer_in_kernel(x_ref, g_ref, wq_ref, wk_ref, wv_ref, wu_ref, wg_ref, bg_ref,
                     pmix_ref, pscale_ref, wpu_ref,
                     q_ref, k_ref, v_ref, g0_ref, gyb_ref, ext_ref, wpool_ref):
    ts = x_ref.shape[1]
    d = x_ref.shape[2]
    j = pl.program_id(1)

    @pl.when((pl.program_id(0) == 0) & (j == 0))
    def _():
        for gi in range(len(POOL_WINDOWS)):
            sl = slice(gi * POOL_GROUP, (gi + 1) * POOL_GROUP)
            mix = (pmix_ref[gi].astype(F32) * pscale_ref[:, sl]).astype(BF16)
            wpool_ref[sl, :] = _dot(mix, wpu_ref[sl, :]).astype(BF16)

    h = _rms(x_ref[0], g_ref[...]).astype(BF16)
    q_ref[0] = (_dot(h, wq_ref[...]) * (HEAD_DIM ** -0.5)).astype(BF16)
    k_ref[0] = _dot(h, wk_ref[...]).astype(BF16)
    v_ref[0] = _dot(h, wv_ref[...]).astype(BF16)
    u = _dot(h, wu_ref[...])

    @pl.when(j == 0)
    def _():
        ext_ref[0:POOL_PAD, :] = jnp.zeros((POOL_PAD, u.shape[1]), F32)

    ext_ref[POOL_PAD:POOL_PAD + ts, :] = u
    e = ext_ref[...]
    sums = {}
    s = e
    w = 1
    while w < max(POOL_WINDOWS):
        s = s + pltpu.roll(s, w, 0)
        w *= 2
        sums[w] = s
    ext_ref[0:POOL_PAD, :] = ext_ref[ts:ts + POOL_PAD, :]

    pos = (j * ts + lax.broadcasted_iota(I32, (ts, 1), 0) + 1).astype(F32)
    pooled = []
    for gi, w in enumerate(POOL_WINDOWS):
        sl = slice(gi * POOL_GROUP, (gi + 1) * POOL_GROUP)
        win = sums[w][POOL_PAD:, sl]
        pooled.append(win / jnp.minimum(pos, float(w)) - u[:, sl])
    y_b = _dot(jnp.concatenate(pooled, axis=1).astype(BF16), wpool_ref[...])

    gate = jax.nn.sigmoid(_dot(h, wg_ref[...]) + bg_ref[...])
    g0_ref[...] = gate[:, :d].astype(BF16)
    gyb_ref[...] = (gate[:, d:] * y_b).astype(BF16)


def _mixer_in(x, norm_g, w_in, w_gate, b_gate, pool_mix, pool_scale, w_pool_up, ts):
    b, s, d = x.shape
    aw = N_HEADS * HEAD_W
    pw = len(POOL_WINDOWS) * POOL_GROUP
    n = b * s
    nt = s // ts
    const = lambda *shape: pl.BlockSpec(shape, lambda bi, j: (0,) * len(shape))
    tok3 = pl.BlockSpec((1, ts, aw), lambda bi, j: (bi, j, 0))
    tok2 = pl.BlockSpec((ts, d), lambda bi, j: (bi * nt + j, 0))
    return pl.pallas_call(
        _mixer_in_kernel,
        grid=(b, nt),
        in_specs=[
            pl.BlockSpec((1, ts, d), lambda bi, j: (bi, j, 0)),
            const(1, d),
            pl.BlockSpec((d, aw), lambda bi, j: (0, 0)),
            pl.BlockSpec((d, aw), lambda bi, j: (0, 1)),
            pl.BlockSpec((d, aw), lambda bi, j: (0, 2)),
            pl.BlockSpec((d, pw), lambda bi, j: (0, 3 * aw // pw)),
            const(d, 2 * d),
            const(1, 2 * d),
            const(len(POOL_WINDOWS), POOL_GROUP, POOL_GROUP),
            const(1, pw),
            const(pw, d),
        ],
        out_specs=[tok3, tok3, tok3, tok2, tok2],
        out_shape=[jax.ShapeDtypeStruct((b, s, aw), BF16)] * 3
        + [jax.ShapeDtypeStruct((n, d), BF16)] * 2,
        scratch_shapes=[pltpu.VMEM((ts + POOL_PAD, pw), F32), pltpu.VMEM((pw, d), BF16)],
        compiler_params=_cparams(2),
        name="mixer_in",
    )(x, norm_g, w_in, w_in, w_in, w_in, w_gate, b_gate, pool_mix, pool_scale, w_pool_up)


def _rel_bucket(rel, log=jnp.log, f32=lambda a: a.astype(jnp.float32),
                i32=lambda a: a.astype(jnp.int32), xp=jnp):
    nb = N_BUCKETS // 2
    ret = i32(rel > 0) * nb
    n = xp.abs(rel)
    max_exact = nb // 2
    nf = f32(xp.maximum(n, 1))
    large = max_exact + i32(log(nf / max_exact) / math.log(MAX_DISTANCE / max_exact)
                            * (nb - max_exact))
    large = xp.minimum(large, nb - 1)
    return ret + xp.where(n < max_exact, n, large)


def _far_bucket(block, seq):
    rel = -np.arange(block + 1, max(seq, block + 2), dtype=np.int32)
    bk = _rel_bucket(rel, log=np.log, f32=lambda a: a.astype(np.float32),
                     i32=lambda a: a.astype(np.int32), xp=np)
    assert (bk == bk[0]).all(), "far keys must share one relative-position bucket"
    return int(bk[0])


def _rel_bias_kernel(far_bucket, tab_ref, bidx_ref, lq1_ref, lk1_ref, lq2_ref, lk2_ref,
                     bias_ref, lam_ref):
    h = pl.program_id(0)
    bidx = bidx_ref[...]
    acc = jnp.zeros(bidx.shape, F32)
    for bkt in range(N_BUCKETS):
        acc = jnp.where(bidx == bkt, tab_ref[bkt, h], acc)
    acc = acc - tab_ref[far_bucket, h]
    bias_ref[0] = jnp.where(bidx < 0, -jnp.inf, acc)
    lam = (jnp.exp(jnp.sum(lq1_ref[...] * lk1_ref[...], keepdims=True))
           - jnp.exp(jnp.sum(lq2_ref[...] * lk2_ref[...], keepdims=True)) + LAMBDA_INIT)
    lam_ref[...] = jnp.broadcast_to(lam, lam_ref.shape)


def _rel_bias(rel_bias, lq1, lk1, lq2, lk2, seq):
    blk = ATTN_BLOCK
    qpos = jnp.arange(blk, dtype=I32)[:, None]
    kpos = jnp.arange(blk, dtype=I32)[None, :]
    diag = jnp.where(kpos // CHUNK <= qpos // CHUNK, _rel_bucket(kpos - qpos), -1)
    prev = _rel_bucket(kpos - (qpos + blk))
    bidx = jnp.stack([diag, prev]).astype(I32)
    vec = pl.BlockSpec((1, HEAD_DIM), lambda h: (0, 0))
    return pl.pallas_call(
        functools.partial(_rel_bias_kernel, _far_bucket(blk, seq)),
        grid=(N_HEADS,),
        in_specs=[
            pl.BlockSpec(memory_space=pltpu.SMEM),
            pl.BlockSpec((2, blk, blk), lambda h: (0, 0, 0)),
            vec, vec, vec, vec,
        ],
        out_specs=[
            pl.BlockSpec((1, 2, blk, blk), lambda h: (h, 0, 0, 0)),
            pl.BlockSpec((SUBLANES, LANES), lambda h: (0, 0)),
        ],
        out_shape=[jax.ShapeDtypeStruct((N_HEADS, 2, blk, blk), F32),
                   jax.ShapeDtypeStruct((SUBLANES, LANES), F32)],
        compiler_params=_cparams(1),
        name="rel_bias",
    )(rel_bias, bidx, lq1, lk1, lq2, lk2)


def _diff_attn_kernel(q_ref, k_ref, v_ref, bias_ref, lam_ref, sg_ref, o_ref, vext_ref):
    s_len = q_ref.shape[1]
    tq = ATTN_BLOCK
    n_heads = q_ref.shape[2] // HEAD_W
    lam = lam_ref[0:1, 0:1]
    lane = lax.broadcasted_iota(I32, (tq, HEAD_W), 1)
    cols = [slice(hh * HEAD_W, (hh + 1) * HEAD_W) for hh in range(n_heads)]
    b_diag, b_prev = [], []
    for hh in range(n_heads):
        vext_ref[hh, :, :HEAD_W] = v_ref[0, :, cols[hh]]
        vext_ref[hh, :, HEAD_W:] = jnp.ones((s_len, HEAD_W), BF16)
        b_diag.append(jnp.concatenate([bias_ref[hh, 0]] * 2, axis=0))
        b_prev.append(jnp.concatenate([bias_ref[hh, 1]] * 2, axis=0))

    for qi in range(s_len // tq):
        n_keys = (qi + 1) * tq
        rows = slice(qi * tq, (qi + 1) * tq)
        for hh in range(n_heads):
            q = q_ref[0, rows, cols[hh]]
            zero = jnp.zeros_like(q)
            qs = jnp.concatenate([jnp.where(lane < HEAD_DIM, q, zero),
                                  jnp.where(lane >= HEAD_DIM, q, zero)], axis=0)
            s = _dot_nt(qs, k_ref[0, :n_keys, cols[hh]])
            pieces = []
            if qi >= 2:
                pieces.append(s[:, :n_keys - 2 * tq])
            if qi >= 1:
                pieces.append(s[:, n_keys - 2 * tq:n_keys - tq] + b_prev[hh])
            pieces.append(s[:, n_keys - tq:] + b_diag[hh])
            s = jnp.concatenate(pieces, axis=1) if len(pieces) > 1 else pieces[0]
            m = jnp.max(s, axis=-1, keepdims=True)
            p = jnp.exp(s - m).astype(BF16)
            acc = _dot(p, vext_ref[hh, :n_keys, :])
            o = acc[:, :HEAD_W] / acc[:, HEAD_W:HEAD_W + 1]
            a = o[:tq] - lam * o[tq:]
            o_ref[0, rows, cols[hh]] = (
                _rms(a, sg_ref[...]) * (1.0 - LAMBDA_INIT)).astype(BF16)


def _diff_attn(q, k, v, bias, lam, subln_g, heads_per_step):
    b, s, aw = q.shape
    tq = ATTN_BLOCK
    hp = heads_per_step
    spec = pl.BlockSpec((1, s, hp * HEAD_W), lambda bi, h: (bi, 0, h))
    return pl.pallas_call(
        _diff_attn_kernel,
        grid=(b, N_HEADS // hp),
        in_specs=[
            spec, spec, spec,
            pl.BlockSpec((hp, 2, tq, tq), lambda bi, h: (h, 0, 0, 0)),
            pl.BlockSpec((SUBLANES, LANES), lambda bi, h: (0, 0)),
            pl.BlockSpec((1, HEAD_W), lambda bi, h: (0, 0)),
        ],
        out_specs=spec,
        out_shape=jax.ShapeDtypeStruct((b, s, aw), BF16),
        scratch_shapes=[pltpu.VMEM((hp, s, 2 * HEAD_W), BF16)],
        compiler_params=_cparams(2),
        name="diff_attn",
    )(q, k, v, bias, lam, subln_g)


def _mem_fold_kernel(m_ref, g_ref, wkv_ref, wq_ref, wo_ref, qk_ref, vo_ref):
    d = m_ref.shape[2]
    hd = d // X_HEADS
    m = _rms(m_ref[0], g_ref[...]).astype(BF16)
    kv = _dot(m, wkv_ref[...])
    k = kv[:, :d].astype(BF16)
    v = kv[:, d:].astype(BF16)
    for hh in range(X_HEADS):
        sl = slice(hh * hd, (hh + 1) * hd)
        qk_ref[0, :, hh * m.shape[0]:(hh + 1) * m.shape[0]] = (
            _dot_nt(wq_ref[:, sl], k[:, sl]) * (hd ** -0.5)).astype(BF16)
        vo_ref[0, hh * m.shape[0]:(hh + 1) * m.shape[0], :] = _dot(v[:, sl], wo_ref[sl, :]).astype(BF16)


def _mem_fold(mem, norm_g, w_xkv, w_xq, w_xo):
    b, ml, d = mem.shape
    const = lambda *shape: pl.BlockSpec(shape, lambda bi: (0,) * len(shape))
    return pl.pallas_call(
        _mem_fold_kernel,
        grid=(b,),
        in_specs=[pl.BlockSpec((1, ml, d), lambda bi: (bi, 0, 0)), const(1, d),
                  const(d, 2 * d), const(d, d), const(d, d)],
        out_specs=[pl.BlockSpec((1, d, X_HEADS * ml), lambda bi: (bi, 0, 0)),
                   pl.BlockSpec((1, X_HEADS * ml, d), lambda bi: (bi, 0, 0))],
        out_shape=[jax.ShapeDtypeStruct((b, d, X_HEADS * ml), BF16),
                   jax.ShapeDtypeStruct((b, X_HEADS * ml, d), BF16)],
        compiler_params=_cparams(1),
        name="mem_fold",
    )(mem, norm_g, w_xkv, w_xq, w_xo)


def _post_kernel(x_ref, a_ref, g0_ref, gyb_ref, wau_ref, wo_ref, nxg_ref,
                 qk_ref, vo_ref, nfg_ref, wrt_ref, br_ref,
                 x2_ref, hf_ref, eidx_ref, rank_ref, gate_ref, cnt_ref, run_ref):
    ts, d = x_ref.shape
    first = (pl.program_id(0) == 0) & (pl.program_id(1) == 0)

    y_a = _dot(a_ref[...], wau_ref[...])
    merged = g0_ref[...].astype(F32) * y_a + gyb_ref[...].astype(F32)
    x1 = x_ref[...] + _dot(merged.astype(BF16), wo_ref[...])

    ml = qk_ref.shape[2] // X_HEADS
    hx = _rms(x1, nxg_ref[...]).astype(BF16)
    s_all = _dot(hx, qk_ref[0])
    probs = []
    for hh in range(X_HEADS):
        s = s_all[:, hh * ml:(hh + 1) * ml]
        p = jnp.exp(s - jnp.max(s, axis=-1, keepdims=True))
        probs.append((p / jnp.sum(p, axis=-1, keepdims=True)).astype(BF16))
    x2 = x1 + _dot(jnp.concatenate(probs, axis=1), vo_ref[0])
    x2_ref[...] = x2
    hf = _rms(x2, nfg_ref[...])
    hf_ref[...] = _pack_halves(hf)

    logits = _dot_nt(wrt_ref[...], hf, precision=lax.Precision.HIGHEST) + br_ref[...]
    ne = logits.shape[0]
    eid = lax.broadcasted_iota(I32, logits.shape, 0).astype(F32)
    work = logits
    vals, idxs, hots = [], [], []
    for _ in range(TOP_K):
        mx = jnp.max(work, axis=0, keepdims=True)
        idx = jnp.min(jnp.where(work == mx, eid, float(ne)), axis=0, keepdims=True)
        hot = eid == idx
        vals.append(mx)
        idxs.append(idx.astype(I32))
        hots.append(hot)
        work = jnp.where(hot, -jnp.inf, work)
    ex = [jnp.exp(vv - vals[0]) for vv in vals]
    den = ex[0] + ex[1] + ex[2] + ex[3]
    gates = [e_ / den for e_ in ex]

    multi = (hots[0] | hots[1] | hots[2] | hots[3])
    multi_f = jnp.where(multi, 1.0, 0.0).astype(F32)
    tri = (lax.broadcasted_iota(I32, (ts, ts), 0)
           < lax.broadcasted_iota(I32, (ts, ts), 1))
    before = _dot(multi_f.astype(BF16), jnp.where(tri, 1.0, 0.0).astype(BF16))

    @pl.when(first)
    def _():
        run_ref[...] = jnp.zeros(run_ref.shape, F32)

    run = run_ref[...]
    pos = before + run
    ranks = [jnp.sum(jnp.where(hot, pos, 0.0), axis=0, keepdims=True) for hot in hots]
    run_new = run + jnp.sum(multi_f, axis=1, keepdims=True)
    run_ref[...] = run_new
    cnt_ref[...] = jnp.broadcast_to(run_new, cnt_ref.shape).astype(I32)

    eidx_ref[...] = jnp.concatenate(idxs, axis=0)
    rank_ref[...] = jnp.concatenate(ranks, axis=0).astype(I32)
    g_rows = jnp.concatenate(gates + [jnp.zeros((LANES - TOP_K, ts), F32)], axis=0)
    gate_ref[...] = g_rows.T


def _post(x2d, a2d, g0, gyb, w_attn_up, w_out, norm_x_g, qk_fold, vo_fold,
          norm_ffn_g, w_router_t, b_router, batch, ts):
    n, d = x2d.shape
    nt = n // batch // ts
    tok = pl.BlockSpec((ts, d), lambda bi, j: (bi * nt + j, 0))
    lanes = pl.BlockSpec((TOP_K, ts), lambda bi, j: (0, bi * nt + j))
    const = lambda *shape: pl.BlockSpec(shape, lambda bi, j: (0,) * len(shape),
                                        pipeline_mode=pl.Buffered(1))
    fold = lambda arr: pl.BlockSpec((1,) + arr.shape[1:], lambda bi, j: (bi, 0, 0))
    return pl.pallas_call(
        _post_kernel,
        grid=(batch, nt),
        in_specs=[tok, tok, tok, tok, const(d, d), const(d, d), const(1, d),
                  fold(qk_fold), fold(vo_fold), const(1, d), const(N_EXPERTS, d),
                  const(N_EXPERTS, 1)],
        out_specs=[tok, pl.BlockSpec((ts, d // 2), lambda bi, j: (bi * nt + j, 0)),
                   lanes, lanes,
                   pl.BlockSpec((ts, LANES), lambda bi, j: (bi * nt + j, 0)),
                   pl.BlockSpec((N_EXPERTS, LANES), lambda bi, j: (0, 0))],
        out_shape=[jax.ShapeDtypeStruct((n, d), F32),
                   jax.ShapeDtypeStruct((n, d // 2), jnp.uint32),
                   jax.ShapeDtypeStruct((TOP_K, n), I32), jax.ShapeDtypeStruct((TOP_K, n), I32),
                   jax.ShapeDtypeStruct((n, LANES), F32),
                   jax.ShapeDtypeStruct((N_EXPERTS, LANES), I32)],
        scratch_shapes=[pltpu.VMEM((N_EXPERTS, 1), F32)],
        compiler_params=_cparams(2),
        name="post",
    )(x2d, a2d, g0, gyb, w_attn_up, w_out, norm_x_g, qk_fold, vo_fold,
      norm_ffn_g, w_router_t, b_router)


def _route_dest_kernel(pstart_ref, eidx_ref, rank_ref, dest_ref):
    eidx = eidx_ref[...]
    start = jnp.zeros(eidx.shape, I32)
    for e in range(N_EXPERTS):
        start = jnp.where(eidx == e, pstart_ref[e], start)
    dest_ref[...] = start + rank_ref[...]


def _route_dest(pstart, eidx, rank, tl):
    k, n = eidx.shape
    lanes = pl.BlockSpec((k, tl), lambda i: (0, i))
    return pl.pallas_call(
        _route_dest_kernel,
        grid=(n // tl,),
        in_specs=[pl.BlockSpec(memory_space=pltpu.SMEM), lanes, lanes],
        out_specs=lanes,
        out_shape=jax.ShapeDtypeStruct((k, n), I32),
        compiler_params=_cparams(1),
        name="route_dest",
    )(pstart, eidx, rank)


SC_CORES = 2
SC_SUBCORES = 16
SC_WORKERS = SC_CORES * SC_SUBCORES
MOVE_CHUNK = 32
MOVE_BUFFERS = 6


def _worker_index_layout(dest, n):
    per_w = n // SC_WORKERS
    n_chunks = per_w // MOVE_CHUNK
    d4 = dest.reshape(TOP_K, SC_WORKERS, n_chunks, MOVE_CHUNK)
    return jnp.transpose(d4, (1, 2, 0, 3)).reshape(SC_WORKERS, n_chunks * TOP_K, MOVE_CHUNK)


def _sc_mesh():
    return plsc.VectorSubcoreMesh(core_axis_name="c", subcore_axis_name="s",
                                  num_cores=SC_CORES, num_subcores=SC_SUBCORES)


def _sc_dispatch(hf, dest_w, rows):
    n, d = hf.shape
    per_w = n // SC_WORKERS
    n_chunks = per_w // MOVE_CHUNK

    def body(hf_hbm, dest_hbm, xs_hbm, idx_v, rows_v, rsem, wsem):
        wid = lax.axis_index("s") * SC_CORES + lax.axis_index("c")
        base = wid * per_w
        pltpu.sync_copy(dest_hbm.at[wid], idx_v)

        def read(c):
            b = c % MOVE_BUFFERS
            return pltpu.async_copy(hf_hbm.at[pl.ds(base + c * MOVE_CHUNK, MOVE_CHUNK)],
                                    rows_v.at[b], rsem.at[b])

        scatters = [[] for _ in range(MOVE_BUFFERS)]
        pending = read(0)
        for c in range(n_chunks):
            b = c % MOVE_BUFFERS
            pending.wait()
            if c + 1 < n_chunks:
                for cp in scatters[(c + 1) % MOVE_BUFFERS]:
                    cp.wait()
                pending = read(c + 1)
            scatters[b] = [
                pltpu.async_copy(rows_v.at[b], xs_hbm.at[idx_v.at[c * TOP_K + kk]], wsem.at[b])
                for kk in range(TOP_K)]
        for group in scatters:
            for cp in group:
                cp.wait()

    return pl.kernel(
        body,
        out_type=jax.ShapeDtypeStruct((rows, d), hf.dtype),
        mesh=_sc_mesh(),
        scratch_types=[pltpu.VMEM((n_chunks * TOP_K, MOVE_CHUNK), I32),
                       pltpu.VMEM((MOVE_BUFFERS, MOVE_CHUNK, d), hf.dtype),
                       pltpu.SemaphoreType.DMA((MOVE_BUFFERS,)),
                       pltpu.SemaphoreType.DMA((MOVE_BUFFERS,))],
        name="sc_dispatch",
    )(hf, dest_w)


def _sc_gather(ys, dest_w, n):
    _, d = ys.shape
    per_w = n // SC_WORKERS
    n_chunks = per_w // MOVE_CHUNK

    def body(ys_hbm, dest_hbm, yg_hbm, idx_v, rows_v, rsem, wsem):
        wid = lax.axis_index("s") * SC_CORES + lax.axis_index("c")
        base = wid * per_w
        pltpu.sync_copy(dest_hbm.at[wid], idx_v)
        n_moves = n_chunks * TOP_K

        def gather(m):
            b = m % MOVE_BUFFERS
            return pltpu.async_copy(ys_hbm.at[idx_v.at[m]], rows_v.at[b], rsem.at[b])

        def write(m):
            b = m % MOVE_BUFFERS
            c, kk = divmod(m, TOP_K)
            return pltpu.async_copy(
                rows_v.at[b], yg_hbm.at[pl.ds(kk * n + base + c * MOVE_CHUNK, MOVE_CHUNK)],
                wsem.at[b])

        writes = [None] * MOVE_BUFFERS
        pending = gather(0)
        for m in range(n_moves):
            pending.wait()
            if m + 1 < n_moves:
                nb = (m + 1) % MOVE_BUFFERS
                if writes[nb] is not None:
                    writes[nb].wait()
                pending = gather(m + 1)
            writes[m % MOVE_BUFFERS] = write(m)
        for wr in writes:
            if wr is not None:
                wr.wait()

    return pl.kernel(
        body,
        out_type=jax.ShapeDtypeStruct((TOP_K * n, d), ys.dtype),
        mesh=_sc_mesh(),
        scratch_types=[pltpu.VMEM((n_chunks * TOP_K, MOVE_CHUNK), I32),
                       pltpu.VMEM((MOVE_BUFFERS, MOVE_CHUNK, d), ys.dtype),
                       pltpu.SemaphoreType.DMA((MOVE_BUFFERS,)),
                       pltpu.SemaphoreType.DMA((MOVE_BUFFERS,))],
        name="sc_gather",
    )(ys, dest_w)


def _combine_kernel(x2_ref, gate_ref, fg_ref, yg_ref, o_ref):
    g = gate_ref[...]
    x2 = x2_ref[...]
    c = x2.shape[1] // 2
    acc_hi, acc_lo = x2[:, :c], x2[:, c:]
    for kk in range(TOP_K):
        hi, lo = _unpack_halves(yg_ref[kk])
        acc_hi = acc_hi + g[:, kk:kk + 1] * hi
        acc_lo = acc_lo + g[:, kk:kk + 1] * lo
    o_ref[...] = _rms(jnp.concatenate([acc_hi, acc_lo], axis=1), fg_ref[...])


def _combine(x2, gate_tm, final_g, yg, ts):
    n, d = x2.shape
    dw = yg.shape[1]
    return pl.pallas_call(
        _combine_kernel,
        grid=(n // ts,),
        in_specs=[pl.BlockSpec((ts, d), lambda i: (i, 0)),
                  pl.BlockSpec((ts, LANES), lambda i: (i, 0)),
                  pl.BlockSpec((1, d), lambda i: (0, 0)),
                  pl.BlockSpec((TOP_K, ts, dw), lambda i: (0, i, 0))],
        out_specs=pl.BlockSpec((ts, d), lambda i: (i, 0)),
        out_shape=jax.ShapeDtypeStruct((n, d), F32),
        compiler_params=_cparams(1),
        name="combine",
    )(x2, gate_tm, final_g, yg.reshape(TOP_K, n, dw))


def _experts_kernel(be_ref, first_ref, parts_ref, nxt_ref, slot_ref, nb_ref,
                    xs_ref, bgu_ref, bd_ref, wgu_hbm, wd_hbm, ys_ref,
                    wgu_st, wd_st, wgu_bf, wd_bf, sem_w):
    dff = wd_bf.shape[0]
    bm = xs_ref.shape[0] // EXPERT_BLOCKS_PER_STEP

    def fetch(ex, sl):
        return (pltpu.make_async_copy(wgu_hbm.at[ex], wgu_st.at[sl], sem_w.at[0, sl]),
                pltpu.make_async_copy(wd_hbm.at[ex], wd_st.at[sl], sem_w.at[1, sl]))

    @pl.when(pl.program_id(0) == 0)
    def _():
        e0 = be_ref[0]
        for cp in fetch(e0, slot_ref[e0]):
            cp.start()

    for sub in range(EXPERT_BLOCKS_PER_STEP):
        blk = pl.program_id(0) * EXPERT_BLOCKS_PER_STEP + sub
        e = be_ref[blk]
        rows = slice(sub * bm, (sub + 1) * bm)

        @pl.when(first_ref[blk] == 1)
        def _(e=e):
            sl = slot_ref[e]
            for cp in fetch(e, sl):
                cp.wait()
            nx = nxt_ref[e]

            @pl.when(nx >= 0)
            def _():
                for cp in fetch(nx, 1 - sl):
                    cp.start(priority=1)

            wgu_bf[...] = wgu_st[sl].astype(BF16)
            wd_bf[...] = wd_st[sl].astype(BF16)

        def mlp(e, rows):
            xb = jnp.concatenate(_unpack_halves(xs_ref[rows, :]), axis=1).astype(BF16)
            gu = _dot(xb, wgu_bf[...]) + bgu_ref[e]
            x_glu = jnp.minimum(gu[:, :dff], SWIGLU_LIMIT)
            x_lin = jnp.clip(gu[:, dff:], -SWIGLU_LIMIT, SWIGLU_LIMIT)
            act = x_glu * jax.nn.sigmoid(SWIGLU_ALPHA * x_glu) * (x_lin + 1.0)
            ys_ref[rows, :] = _pack_halves(_dot(act.astype(BF16), wd_bf[...]) + bd_ref[e])

        def clear(rows):
            ys_ref[rows, :] = jnp.zeros((rows.stop - rows.start, ys_ref.shape[1]), jnp.uint32)

        part = bm // EXPERT_TAIL_PARTS
        for live in range(EXPERT_TAIL_PARTS + 1):
            @pl.when(parts_ref[blk] == live)
            def _(e=e, rows=rows, live=live):
                cut = rows.start + live * part
                if live:
                    mlp(e, slice(rows.start, cut))
                if live < EXPERT_TAIL_PARTS:
                    clear(slice(cut, rows.stop))


def _expert_schedule(counts, bm, n_blocks):
    ne = counts.shape[0]
    ids = jnp.arange(ne, dtype=I32)
    upto = ids[None, :] <= ids[:, None]
    padded = (counts + bm - 1) // bm * bm
    pend = jnp.sum(jnp.where(upto, padded[None, :], 0), axis=1)
    pstart = (pend - padded).astype(I32)
    nb_used = (jnp.sum(padded) // bm).astype(I32)
    blk = jnp.arange(n_blocks, dtype=I32)
    blk_c = jnp.minimum(blk, nb_used - 1)
    be = jnp.minimum(jnp.sum((pend[None, :] <= (blk_c * bm)[:, None]).astype(I32), axis=1),
                     ne - 1)
    started = jnp.sum(((pstart[None, :] == (blk * bm)[:, None]) & (padded[None, :] > 0))
                      .astype(I32), axis=1)
    first = ((blk < nb_used) & (started > 0)).astype(I32)
    used = counts > 0
    seq = jnp.sum(jnp.where(upto & used[None, :], 1, 0), axis=1) - 1
    later = used[None, :] & (ids[None, :] > ids[:, None])
    nxt = jnp.where(jnp.any(later, axis=1), jnp.argmax(later, axis=1), -1).astype(I32)
    slot = (seq & 1).astype(I32)
    hit = ids[None, :] == be[:, None]
    left = jnp.sum(jnp.where(hit, (pstart + counts)[None, :], 0), axis=1) - blk * bm
    part = bm // EXPERT_TAIL_PARTS
    parts = jnp.where(blk < nb_used,
                      jnp.minimum((left + part - 1) // part, EXPERT_TAIL_PARTS), 0).astype(I32)
    return pstart, nb_used.reshape(1), be, first, parts, nxt, slot


def _experts(be, first, parts, nxt, slot, nb_used, xs, w_gu, b_gu, w_down, b_down, bm):
    rows, dw = xs.shape
    ne, d, dff2 = w_gu.shape
    dff = dff2 // 2
    sb = EXPERT_BLOCKS_PER_STEP * bm
    assert rows % sb == 0
    last = lambda j, be, fi, ha, nx, sl, nb: (
        jnp.maximum(jnp.minimum(j, (nb[0] - 1) // EXPERT_BLOCKS_PER_STEP), 0), 0)
    const = lambda *shape: pl.BlockSpec(shape,
                                        lambda j, be, fi, ha, nx, sl, nb: (0,) * len(shape))
    grid_spec = pltpu.PrefetchScalarGridSpec(
        num_scalar_prefetch=6,
        grid=(rows // sb,),
        in_specs=[
            pl.BlockSpec((sb, dw), last),
            const(ne, 1, dff2),
            const(ne, 1, d),
            pl.BlockSpec(memory_space=pl.ANY),
            pl.BlockSpec(memory_space=pl.ANY),
        ],
        out_specs=pl.BlockSpec((sb, dw), lambda j, be, fi, ha, nx, sl, nb: (j, 0)),
        scratch_shapes=[pltpu.VMEM((2, d, dff2), F32), pltpu.VMEM((2, dff, d), F32),
                        pltpu.VMEM((d, dff2), BF16), pltpu.VMEM((dff, d), BF16),
                        pltpu.SemaphoreType.DMA((2, 2))],
    )
    return pl.pallas_call(
        _experts_kernel,
        grid_spec=grid_spec,
        out_shape=jax.ShapeDtypeStruct((rows, dw), jnp.uint32),
        compiler_params=_cparams(1),
        name="experts",
    )(be, first, parts, nxt, slot, nb_used, xs, b_gu.reshape(ne, 1, dff2),
      b_down.reshape(ne, 1, d), w_gu, w_down)


MIXER_TILE = 512
ATTN_HEADS_PER_STEP = 4
POST_TILE = 512
DEST_TILE = 2048
COMBINE_TILE = 1024
EXPERT_BLOCK = 1024
EXPERT_TAIL_PARTS = 4
EXPERT_BLOCKS_PER_STEP = 1


def kernel(x, mem, norm_mix_g, w_in, lambda_q1, lambda_k1, lambda_q2, lambda_k2, rel_bias,
           subln_g, w_attn_up, pool_mix, pool_scale, w_pool_up, w_gate, b_gate, w_out,
           norm_x_g, norm_mem_g, w_xq, w_xkv, w_xo, norm_ffn_g, w_router, b_router,
           w_gu, b_gu, w_down, b_down, final_norm_g):
    b, s, d = x.shape
    n = b * s
    assert w_in.shape[0] == 1, "single-layer block"
    row = lambda a: a.reshape(1, -1)
    bf = lambda a: a[0].astype(BF16)

    q, k, v, g0, gyb = _mixer_in(
        x, row(norm_mix_g[0]), bf(w_in), bf(w_gate), row(b_gate[0]), bf(pool_mix),
        row(pool_scale[0]), bf(w_pool_up), MIXER_TILE)

    bias, lam = _rel_bias(rel_bias, row(lambda_q1[0]), row(lambda_k1[0]),
                          row(lambda_q2[0]), row(lambda_k2[0]), s)
    a = _diff_attn(q, k, v, bias, lam, row(subln_g[0]), ATTN_HEADS_PER_STEP)

    qk_fold, vo_fold = _mem_fold(mem, row(norm_mem_g[0]), bf(w_xkv), bf(w_xq), bf(w_xo))

    x2, hf, eidx, rank, gate_tm, counts = _post(
        x.reshape(n, d), a.reshape(n, d), g0, gyb, bf(w_attn_up), bf(w_out),
        row(norm_x_g[0]), qk_fold, vo_fold, row(norm_ffn_g[0]),
        w_router[0].T, b_router[0].reshape(-1, 1), b, POST_TILE)

    bm = EXPERT_BLOCK
    rows = n * TOP_K + N_EXPERTS * bm
    pstart, nb_used, be, first, parts, nxt, slot = _expert_schedule(
        counts[:, 0], bm, rows // bm)

    dest_w = _worker_index_layout(_route_dest(pstart, eidx, rank, DEST_TILE), n)
    xs = _sc_dispatch(hf, dest_w, rows)
    ys = _experts(be, first, parts, nxt, slot, nb_used, xs, w_gu[0], b_gu[0], w_down[0],
                  b_down[0], bm)
    yg = _sc_gather(ys, dest_w, n)
    out = _combine(x2, gate_tm, row(final_norm_g), yg, COMBINE_TILE)
    return out.reshape(b, s, d)
```

```python
import functools
import math

import numpy as np
import jax
import jax.numpy as jnp
from jax import lax
from jax.experimental import pallas as pl
from jax.experimental.pallas import tpu as pltpu
from jax.experimental.pallas import tpu_sc as plsc

F32 = jnp.float32
BF16 = jnp.bfloat16
I32 = jnp.int32

EPS = 1e-6
CHUNK = 64
N_HEADS = 8
HEAD_DIM = 64
HEAD_W = 2 * HEAD_DIM
POOL_WINDOWS = (2, 4, 8, 16)
POOL_GROUP = 128
POOL_PAD = 16
N_BUCKETS = 32
MAX_DISTANCE = 128
X_HEADS = 4
N_EXPERTS = 32
TOP_K = 4
SWIGLU_ALPHA = 1.702
SWIGLU_LIMIT = 7.0
LAMBDA_INIT = 0.8 - 0.6 * math.exp(-0.3 * 0)

LANES = 128
SUBLANES = 8
ATTN_BLOCK = 256
VMEM_LIMIT = 56 * 1024 * 1024


def _cparams(n_axes, vmem=VMEM_LIMIT):
    return pltpu.CompilerParams(
        dimension_semantics=("arbitrary",) * n_axes, vmem_limit_bytes=vmem)


def _rms(xf, g):
    ms = jnp.mean(xf * xf, axis=-1, keepdims=True)
    return xf * lax.rsqrt(ms + EPS) * g


def _dot(a, b):
    return jnp.dot(a, b, preferred_element_type=F32)


def _pack_halves(x):
    c = x.shape[1] // 2
    as_bits = lambda v: lax.bitcast_convert_type(v.astype(BF16).astype(F32), jnp.uint32)
    return as_bits(x[:, :c]) | (as_bits(x[:, c:]) >> 16)


def _unpack_halves(w):
    hi = lax.bitcast_convert_type(w & jnp.uint32(0xFFFF0000), F32)
    lo = lax.bitcast_convert_type(w << 16, F32)
    return hi, lo


def _dot_nt(a, b, precision=None):
    return lax.dot_general(a, b, (((1,), (1,)), ((), ())),
                           preferred_element_type=F32, precision=precision)


def _mixer_in_kernel(x_ref, g_ref, wq_ref, wk_ref, wv_ref, wu_ref, wg_ref, bg_ref,
                     pmix_ref, pscale_ref, wpu_ref,
                     q_ref, k_ref, v_ref, g0_ref, gyb_ref, ext_ref, wpool_ref):
    ts = x_ref.shape[1]
    d = x_ref.shape[2]
    j = pl.program_id(1)

    @pl.when((pl.program_id(0) == 0) & (j == 0))
    def _():
        for gi in range(len(POOL_WINDOWS)):
            sl = slice(gi * POOL_GROUP, (gi + 1) * POOL_GROUP)
            mix = (pmix_ref[gi].astype(F32) * pscale_ref[:, sl]).astype(BF16)
            wpool_ref[sl, :] = _dot(mix, wpu_ref[sl, :]).astype(BF16)

    h = _rms(x_ref[0], g_ref[...]).astype(BF16)
    q_ref[0] = (_dot(h, wq_ref[...]) * (HEAD_DIM ** -0.5)).astype(BF16)
    k_ref[0] = _dot(h, wk_ref[...]).astype(BF16)
    v_ref[0] = _dot(h, wv_ref[...]).astype(BF16)
    u = _dot(h, wu_ref[...])

    @pl.when(j == 0)
    def _():
        ext_ref[0:POOL_PAD, :] = jnp.zeros((POOL_PAD, u.shape[1]), F32)

    ext_ref[POOL_PAD:POOL_PAD + ts, :] = u
    e = ext_ref[...]
    sums = {}
    s = e
    w = 1
    while w < max(POOL_WINDOWS):
        s = s + pltpu.roll(s, w, 0)
        w *= 2
        sums[w] = s
    ext_ref[0:POOL_PAD, :] = ext_ref[ts:ts + POOL_PAD, :]

    pos = (j * ts + lax.broadcasted_iota(I32, (ts, 1), 0) + 1).astype(F32)
    pooled = []
    for gi, w in enumerate(POOL_WINDOWS):
        sl = slice(gi * POOL_GROUP, (gi + 1) * POOL_GROUP)
        win = sums[w][POOL_PAD:, sl]
        pooled.append(win / jnp.minimum(pos, float(w)) - u[:, sl])
    y_b = _dot(jnp.concatenate(pooled, axis=1).astype(BF16), wpool_ref[...])

    gate = jax.nn.sigmoid(_dot(h, wg_ref[...]) + bg_ref[...])
    g0_ref[...] = gate[:, :d].astype(BF16)
    gyb_ref[...] = (gate[:, d:] * y_b).astype(BF16)


def _mixer_in(x, norm_g, w_in, w_gate, b_gate, pool_mix, pool_scale, w_pool_up, ts):
    b, s, d = x.shape
    aw = N_HEADS * HEAD_W
    pw = len(POOL_WINDOWS) * POOL_GROUP
    n = b * s
    nt = s // ts
    const = lambda *shape: pl.BlockSpec(shape, lambda bi, j: (0,) * len(shape))
    tok3 = pl.BlockSpec((1, ts, aw), lambda bi, j: (bi, j, 0))
    tok2 = pl.BlockSpec((ts, d), lambda bi, j: (bi * nt + j, 0))
    return pl.pallas_call(
        _mixer_in_kernel,
        grid=(b, nt),
        in_specs=[
            pl.BlockSpec((1, ts, d), lambda bi, j: (bi, j, 0)),
            const(1, d),
            pl.BlockSpec((d, aw), lambda bi, j: (0, 0)),
            pl.BlockSpec((d, aw), lambda bi, j: (0, 1)),
            pl.BlockSpec((d, aw), lambda bi, j: (0, 2)),
            pl.BlockSpec((d, pw), lambda bi, j: (0, 3 * aw // pw)),
            const(d, 2 * d),
            const(1, 2 * d),
            const(len(POOL_WINDOWS), POOL_GROUP, POOL_GROUP),
            const(1, pw),
            const(pw, d),
        ],
        out_specs=[tok3, tok3, tok3, tok2, tok2],
        out_shape=[jax.ShapeDtypeStruct((b, s, aw), BF16)] * 3
        + [jax.ShapeDtypeStruct((n, d), BF16)] * 2,
        scratch_shapes=[pltpu.VMEM((ts + POOL_PAD, pw), F32), pltpu.VMEM((pw, d), BF16)],
        compiler_params=_cparams(2),
        name="mixer_in",
    )(x, norm_g, w_in, w_in, w_in, w_in, w_gate, b_gate, pool_mix, pool_scale, w_pool_up)


def _rel_bucket(rel, log=jnp.log, f32=lambda a: a.astype(jnp.float32),
                i32=lambda a: a.astype(jnp.int32), xp=jnp):
    nb = N_BUCKETS // 2
    ret = i32(rel > 0) * nb
    n = xp.abs(rel)
    max_exact = nb // 2
    nf = f32(xp.maximum(n, 1))
    large = max_exact + i32(log(nf / max_exact) / math.log(MAX_DISTANCE / max_exact)
                            * (nb - max_exact))
    large = xp.minimum(large, nb - 1)
    return ret + xp.where(n < max_exact, n, large)


def _far_bucket(block, seq):
    rel = -np.arange(block + 1, max(seq, block + 2), dtype=np.int32)
    bk = _rel_bucket(rel, log=np.log, f32=lambda a: a.astype(np.float32),
                     i32=lambda a: a.astype(np.int32), xp=np)
    assert (bk == bk[0]).all(), "far keys must share one relative-position bucket"
    return int(bk[0])


def _rel_bias_kernel(far_bucket, tab_ref, bidx_ref, lq1_ref, lk1_ref, lq2_ref, lk2_ref,
                     bias_ref, lam_ref):
    h = pl.program_id(0)
    bidx = bidx_ref[...]
    acc = jnp.zeros(bidx.shape, F32)
    for bkt in range(N_BUCKETS):
        acc = jnp.where(bidx == bkt, tab_ref[bkt, h], acc)
    acc = acc - tab_ref[far_bucket, h]
    bias_ref[0] = jnp.where(bidx < 0, -jnp.inf, acc)
    lam = (jnp.exp(jnp.sum(lq1_ref[...] * lk1_ref[...], keepdims=True))
           - jnp.exp(jnp.sum(lq2_ref[...] * lk2_ref[...], keepdims=True)) + LAMBDA_INIT)
    lam_ref[...] = jnp.broadcast_to(lam, lam_ref.shape)


def _rel_bias(rel_bias, lq1, lk1, lq2, lk2, seq):
    blk = ATTN_BLOCK
    qpos = jnp.arange(blk, dtype=I32)[:, None]
    kpos = jnp.arange(blk, dtype=I32)[None, :]
    diag = jnp.where(kpos // CHUNK <= qpos // CHUNK, _rel_bucket(kpos - qpos), -1)
    prev = _rel_bucket(kpos - (qpos + blk))
    bidx = jnp.stack([diag, prev]).astype(I32)
    vec = pl.BlockSpec((1, HEAD_DIM), lambda h: (0, 0))
    return pl.pallas_call(
        functools.partial(_rel_bias_kernel, _far_bucket(blk, seq)),
        grid=(N_HEADS,),
        in_specs=[
            pl.BlockSpec(memory_space=pltpu.SMEM),
            pl.BlockSpec((2, blk, blk), lambda h: (0, 0, 0)),
            vec, vec, vec, vec,
        ],
        out_specs=[
            pl.BlockSpec((1, 2, blk, blk), lambda h: (h, 0, 0, 0)),
            pl.BlockSpec((SUBLANES, LANES), lambda h: (0, 0)),
        ],
        out_shape=[jax.ShapeDtypeStruct((N_HEADS, 2, blk, blk), F32),
                   jax.ShapeDtypeStruct((SUBLANES, LANES), F32)],
        compiler_params=_cparams(1),
        name="rel_bias",
    )(rel_bias, bidx, lq1, lk1, lq2, lk2)


def _diff_attn_kernel(q_ref, k_ref, v_ref, bias_ref, lam_ref, sg_ref, o_ref, vext_ref):
    s_len = q_ref.shape[1]
    tq = ATTN_BLOCK
    n_heads = q_ref.shape[2] // HEAD_W
    lam = lam_ref[0:1, 0:1]
    lane = lax.broadcasted_iota(I32, (tq, HEAD_W), 1)
    cols = [slice(hh * HEAD_W, (hh + 1) * HEAD_W) for hh in range(n_heads)]
    b_diag, b_prev = [], []
    for hh in range(n_heads):
        vext_ref[hh, :, :HEAD_W] = v_ref[0, :, cols[hh]]
        vext_ref[hh, :, HEAD_W:] = jnp.ones((s_len, HEAD_W), BF16)
        b_diag.append(jnp.concatenate([bias_ref[hh, 0]] * 2, axis=0))
        b_prev.append(jnp.concatenate([bias_ref[hh, 1]] * 2, axis=0))

    for qi in range(s_len // tq):
        n_keys = (qi + 1) * tq
        rows = slice(qi * tq, (qi + 1) * tq)
        for hh in range(n_heads):
            q = q_ref[0, rows, cols[hh]]
            zero = jnp.zeros_like(q)
            qs = jnp.concatenate([jnp.where(lane < HEAD_DIM, q, zero),
                                  jnp.where(lane >= HEAD_DIM, q, zero)], axis=0)
            s = _dot_nt(qs, k_ref[0, :n_keys, cols[hh]])
            pieces = []
            if qi >= 2:
                pieces.append(s[:, :n_keys - 2 * tq])
            if qi >= 1:
                pieces.append(s[:, n_keys - 2 * tq:n_keys - tq] + b_prev[hh])
            pieces.append(s[:, n_keys - tq:] + b_diag[hh])
            s = jnp.concatenate(pieces, axis=1) if len(pieces) > 1 else pieces[0]
            m = jnp.max(s, axis=-1, keepdims=True)
            p = jnp.exp(s - m).astype(BF16)
            acc = _dot(p, vext_ref[hh, :n_keys, :])
            o = acc[:, :HEAD_W] / acc[:, HEAD_W:HEAD_W + 1]
            a = o[:tq] - lam * o[tq:]
            o_ref[0, rows, cols[hh]] = (
                _rms(a, sg_ref[...]) * (1.0 - LAMBDA_INIT)).astype(BF16)


def _diff_attn(q, k, v, bias, lam, subln_g, heads_per_step):
    b, s, aw = q.shape
    tq = ATTN_BLOCK
    hp = heads_per_step
    spec = pl.BlockSpec((1, s, hp * HEAD_W), lambda bi, h: (bi, 0, h))
    return pl.pallas_call(
        _diff_attn_kernel,
        grid=(b, N_HEADS // hp),
        in_specs=[
            spec, spec, spec,
            pl.BlockSpec((hp, 2, tq, tq), lambda bi, h: (h, 0, 0, 0)),
            pl.BlockSpec((SUBLANES, LANES), lambda bi, h: (0, 0)),
            pl.BlockSpec((1, HEAD_W), lambda bi, h: (0, 0)),
        ],
        out_specs=spec,
        out_shape=jax.ShapeDtypeStruct((b, s, aw), BF16),
        scratch_shapes=[pltpu.VMEM((hp, s, 2 * HEAD_W), BF16)],
        compiler_params=_cparams(2),
        name="diff_attn",
    )(q, k, v, bias, lam, subln_g)


def _mem_fold_kernel(m_ref, g_ref, wkv_ref, wq_ref, wo_ref, qk_ref, vo_ref):
    d = m_ref.shape[2]
    hd = d // X_HEADS
    m = _rms(m_ref[0], g_ref[...]).astype(BF16)
    kv = _dot(m, wkv_ref[...])
    k = kv[:, :d].astype(BF16)
    v = kv[:, d:].astype(BF16)
    for hh in range(X_HEADS):
        sl = slice(hh * hd, (hh + 1) * hd)
        qk_ref[0, :, hh * m.shape[0]:(hh + 1) * m.shape[0]] = (
            _dot_nt(wq_ref[:, sl], k[:, sl]) * (hd ** -0.5)).astype(BF16)
        vo_ref[0, hh * m.shape[0]:(hh + 1) * m.shape[0], :] = _dot(v[:, sl], wo_ref[sl, :]).astype(BF16)


def _mem_fold(mem, norm_g, w_xkv, w_xq, w_xo):
    b, ml, d = mem.shape
    const = lambda *shape: pl.BlockSpec(shape, lambda bi: (0,) * len(shape))
    return pl.pallas_call(
        _mem_fold_kernel,
        grid=(b,),
        in_specs=[pl.BlockSpec((1, ml, d), lambda bi: (bi, 0, 0)), const(1, d),
                  const(d, 2 * d), const(d, d), const(d, d)],
        out_specs=[pl.BlockSpec((1, d, X_HEADS * ml), lambda bi: (bi, 0, 0)),
                   pl.BlockSpec((1, X_HEADS * ml, d), lambda bi: (bi, 0, 0))],
        out_shape=[jax.ShapeDtypeStruct((b, d, X_HEADS * ml), BF16),
                   jax.ShapeDtypeStruct((b, X_HEADS * ml, d), BF16)],
        compiler_params=_cparams(1),
        name="mem_fold",
    )(mem, norm_g, w_xkv, w_xq, w_xo)


def _post_kernel(x_ref, a_ref, g0_ref, gyb_ref, wau_ref, wo_ref, nxg_ref,
                 qk_ref, vo_ref, nfg_ref, wrt_ref, br_ref,
                 x2_ref, hf_ref, eidx_ref, rank_ref, gate_ref, cnt_ref, run_ref):
    ts, d = x_ref.shape
    first = (pl.program_id(0) == 0) & (pl.program_id(1) == 0)

    y_a = _dot(a_ref[...], wau_ref[...])
    merged = g0_ref[...].astype(F32) * y_a + gyb_ref[...].astype(F32)
    x1 = x_ref[...] + _dot(merged.astype(BF16), wo_ref[...])

    ml = qk_ref.shape[2] // X_HEADS
    hx = _rms(x1, nxg_ref[...]).astype(BF16)
    s_all = _dot(hx, qk_ref[0])
    probs = []
    for hh in range(X_HEADS):
        s = s_all[:, hh * ml:(hh + 1) * ml]
        p = jnp.exp(s - jnp.max(s, axis=-1, keepdims=True))
        probs.append((p / jnp.sum(p, axis=-1, keepdims=True)).astype(BF16))
    x2 = x1 + _dot(jnp.concatenate(probs, axis=1), vo_ref[0])
    x2_ref[...] = x2
    hf = _rms(x2, nfg_ref[...])
    hf_ref[...] = _pack_halves(hf)

    wr = wrt_ref[...]
    wr_hi = wr.astype(BF16)
    wr_lo = (wr - wr_hi.astype(F32)).astype(BF16)
    hf_hi = hf.astype(BF16)
    hf_lo = (hf - hf_hi.astype(F32)).astype(BF16)
    logits = (_dot_nt(wr_hi, hf_hi) + (_dot_nt(wr_hi, hf_lo) + _dot_nt(wr_lo, hf_hi))
              + br_ref[...])
    ne = logits.shape[0]
    eid = lax.broadcasted_iota(I32, logits.shape, 0).astype(F32)
    work = logits
    vals, idxs, hots = [], [], []
    for _ in range(TOP_K):
        mx = jnp.max(work, axis=0, keepdims=True)
        idx = jnp.min(jnp.where(work == mx, eid, float(ne)), axis=0, keepdims=True)
        hot = eid == idx
        vals.append(mx)
        idxs.append(idx.astype(I32))
        hots.append(hot)
        work = jnp.where(hot, -jnp.inf, work)
    ex = [jnp.exp(vv - vals[0]) for vv in vals]
    den = ex[0] + ex[1] + ex[2] + ex[3]
    gates = [e_ / den for e_ in ex]

    multi = (hots[0] | hots[1] | hots[2] | hots[3])
    multi_f = jnp.where(multi, 1.0, 0.0).astype(F32)
    tri = (lax.broadcasted_iota(I32, (ts, ts), 0)
           < lax.broadcasted_iota(I32, (ts, ts), 1))
    before = _dot(multi_f.astype(BF16), jnp.where(tri, 1.0, 0.0).astype(BF16))

    @pl.when(first)
    def _():
        run_ref[...] = jnp.zeros(run_ref.shape, F32)

    run = run_ref[...]
    pos = before + run
    ranks = [jnp.sum(jnp.where(hot, pos, 0.0), axis=0, keepdims=True) for hot in hots]
    run_new = run + jnp.sum(multi_f, axis=1, keepdims=True)
    run_ref[...] = run_new
    cnt_ref[...] = jnp.broadcast_to(run_new, cnt_ref.shape).astype(I32)

    eidx_ref[...] = jnp.concatenate(idxs, axis=0)
    rank_ref[...] = jnp.concatenate(ranks, axis=0).astype(I32)
    g_rows = jnp.concatenate(gates + [jnp.zeros((LANES - TOP_K, ts), F32)], axis=0)
    gate_ref[...] = g_rows.T


def _post(x2d, a2d, g0, gyb, w_attn_up, w_out, norm_x_g, qk_fold, vo_fold,
          norm_ffn_g, w_router_t, b_router, batch, ts):
    n, d = x2d.shape
    nt = n // batch // ts
    tok = pl.BlockSpec((ts, d), lambda bi, j: (bi * nt + j, 0))
    lanes = pl.BlockSpec((TOP_K, ts), lambda bi, j: (0, bi * nt + j))
    const = lambda *shape: pl.BlockSpec(shape, lambda bi, j: (0,) * len(shape),
                                        pipeline_mode=pl.Buffered(1))
    fold = lambda arr: pl.BlockSpec((1,) + arr.shape[1:], lambda bi, j: (bi, 0, 0))
    return pl.pallas_call(
        _post_kernel,
        grid=(batch, nt),
        in_specs=[tok, tok, tok, tok, const(d, d), const(d, d), const(1, d),
                  fold(qk_fold), fold(vo_fold), const(1, d), const(N_EXPERTS, d),
                  const(N_EXPERTS, 1)],
        out_specs=[tok, pl.BlockSpec((ts, d // 2), lambda bi, j: (bi * nt + j, 0)),
                   lanes, lanes,
                   pl.BlockSpec((ts, LANES), lambda bi, j: (bi * nt + j, 0)),
                   pl.BlockSpec((N_EXPERTS, LANES), lambda bi, j: (0, 0))],
        out_shape=[jax.ShapeDtypeStruct((n, d), F32),
                   jax.ShapeDtypeStruct((n, d // 2), jnp.uint32),
                   jax.ShapeDtypeStruct((TOP_K, n), I32), jax.ShapeDtypeStruct((TOP_K, n), I32),
                   jax.ShapeDtypeStruct((n, LANES), F32),
                   jax.ShapeDtypeStruct((N_EXPERTS, LANES), I32)],
        scratch_shapes=[pltpu.VMEM((N_EXPERTS, 1), F32)],
        compiler_params=_cparams(2),
        name="post",
    )(x2d, a2d, g0, gyb, w_attn_up, w_out, norm_x_g, qk_fold, vo_fold,
      norm_ffn_g, w_router_t, b_router)


def _route_dest_kernel(pstart_ref, eidx_ref, rank_ref, dest_ref):
    eidx = eidx_ref[...]
    start = jnp.zeros(eidx.shape, I32)
    for e in range(N_EXPERTS):
        start = jnp.where(eidx == e, pstart_ref[e], start)
    dest_ref[...] = start + rank_ref[...]


def _route_dest(pstart, eidx, rank, tl):
    k, n = eidx.shape
    lanes = pl.BlockSpec((k, tl), lambda i: (0, i))
    return pl.pallas_call(
        _route_dest_kernel,
        grid=(n // tl,),
        in_specs=[pl.BlockSpec(memory_space=pltpu.SMEM), lanes, lanes],
        out_specs=lanes,
        out_shape=jax.ShapeDtypeStruct((k, n), I32),
        compiler_params=_cparams(1),
        name="route_dest",
    )(pstart, eidx, rank)


SC_CORES = 2
SC_SUBCORES = 16
SC_WORKERS = SC_CORES * SC_SUBCORES
MOVE_CHUNK = 64
MOVE_BUFFERS = 3


def _worker_index_layout(dest, n):
    per_w = n // SC_WORKERS
    n_chunks = per_w // MOVE_CHUNK
    d4 = dest.reshape(TOP_K, SC_WORKERS, n_chunks, MOVE_CHUNK)
    return jnp.transpose(d4, (1, 2, 0, 3)).reshape(SC_WORKERS, n_chunks * TOP_K, MOVE_CHUNK)


def _sc_mesh():
    return plsc.VectorSubcoreMesh(core_axis_name="c", subcore_axis_name="s",
                                  num_cores=SC_CORES, num_subcores=SC_SUBCORES)


def _sc_dispatch(hf, dest_w, rows):
    n, d = hf.shape
    per_w = n // SC_WORKERS
    n_chunks = per_w // MOVE_CHUNK

    def body(hf_hbm, dest_hbm, xs_hbm, idx_v, rows_v, rsem, wsem):
        wid = lax.axis_index("s") * SC_CORES + lax.axis_index("c")
        base = wid * per_w
        pltpu.sync_copy(dest_hbm.at[wid], idx_v)

        def read(c):
            b = c % MOVE_BUFFERS
            return pltpu.async_copy(hf_hbm.at[pl.ds(base + c * MOVE_CHUNK, MOVE_CHUNK)],
                                    rows_v.at[b], rsem.at[b])

        scatters = [[] for _ in range(MOVE_BUFFERS)]
        pending = read(0)
        for c in range(n_chunks):
            b = c % MOVE_BUFFERS
            pending.wait()
            if c + 1 < n_chunks:
                for cp in scatters[(c + 1) % MOVE_BUFFERS]:
                    cp.wait()
                pending = read(c + 1)
            scatters[b] = [
                pltpu.async_copy(rows_v.at[b], xs_hbm.at[idx_v.at[c * TOP_K + kk]], wsem.at[b])
                for kk in range(TOP_K)]
        for group in scatters:
            for cp in group:
                cp.wait()

    return pl.kernel(
        body,
        out_type=jax.ShapeDtypeStruct((rows, d), hf.dtype),
        mesh=_sc_mesh(),
        scratch_types=[pltpu.VMEM((n_chunks * TOP_K, MOVE_CHUNK), I32),
                       pltpu.VMEM((MOVE_BUFFERS, MOVE_CHUNK, d), hf.dtype),
                       pltpu.SemaphoreType.DMA((MOVE_BUFFERS,)),
                       pltpu.SemaphoreType.DMA((MOVE_BUFFERS,))],
        name="sc_dispatch",
    )(hf, dest_w)


def _sc_gather(ys, dest_w, n):
    _, d = ys.shape
    per_w = n // SC_WORKERS
    n_chunks = per_w // MOVE_CHUNK

    def body(ys_hbm, dest_hbm, yg_hbm, idx_v, rows_v, rsem, wsem):
        wid = lax.axis_index("s") * SC_CORES + lax.axis_index("c")
        base = wid * per_w
        pltpu.sync_copy(dest_hbm.at[wid], idx_v)
        n_moves = n_chunks * TOP_K

        def gather(m):
            b = m % MOVE_BUFFERS
            return pltpu.async_copy(ys_hbm.at[idx_v.at[m]], rows_v.at[b], rsem.at[b])

        def write(m):
            b = m % MOVE_BUFFERS
            c, kk = divmod(m, TOP_K)
            return pltpu.async_copy(
                rows_v.at[b], yg_hbm.at[pl.ds(kk * n + base + c * MOVE_CHUNK, MOVE_CHUNK)],
                wsem.at[b])

        writes = [None] * MOVE_BUFFERS
        pending = gather(0)
        for m in range(n_moves):
            pending.wait()
            if m + 1 < n_moves:
                nb = (m + 1) % MOVE_BUFFERS
                if writes[nb] is not None:
                    writes[nb].wait()
                pending = gather(m + 1)
            writes[m % MOVE_BUFFERS] = write(m)
        for wr in writes:
            if wr is not None:
                wr.wait()

    return pl.kernel(
        body,
        out_type=jax.ShapeDtypeStruct((TOP_K * n, d), ys.dtype),
        mesh=_sc_mesh(),
        scratch_types=[pltpu.VMEM((n_chunks * TOP_K, MOVE_CHUNK), I32),
                       pltpu.VMEM((MOVE_BUFFERS, MOVE_CHUNK, d), ys.dtype),
                       pltpu.SemaphoreType.DMA((MOVE_BUFFERS,)),
                       pltpu.SemaphoreType.DMA((MOVE_BUFFERS,))],
        name="sc_gather",
    )(ys, dest_w)


def _combine_kernel(x2_ref, gate_ref, fg_ref, yg_ref, o_ref):
    g = gate_ref[...]
    x2 = x2_ref[...]
    c = x2.shape[1] // 2
    acc_hi, acc_lo = x2[:, :c], x2[:, c:]
    for kk in range(TOP_K):
        hi, lo = _unpack_halves(yg_ref[kk])
        acc_hi = acc_hi + g[:, kk:kk + 1] * hi
        acc_lo = acc_lo + g[:, kk:kk + 1] * lo
    o_ref[...] = _rms(jnp.concatenate([acc_hi, acc_lo], axis=1), fg_ref[...])


def _combine(x2, gate_tm, final_g, yg, ts):
    n, d = x2.shape
    dw = yg.shape[1]
    return pl.pallas_call(
        _combine_kernel,
        grid=(n // ts,),
        in_specs=[pl.BlockSpec((ts, d), lambda i: (i, 0)),
                  pl.BlockSpec((ts, LANES), lambda i: (i, 0)),
                  pl.BlockSpec((1, d), lambda i: (0, 0)),
                  pl.BlockSpec((TOP_K, ts, dw), lambda i: (0, i, 0))],
        out_specs=pl.BlockSpec((ts, d), lambda i: (i, 0)),
        out_shape=jax.ShapeDtypeStruct((n, d), F32),
        compiler_params=_cparams(1),
        name="combine",
    )(x2, gate_tm, final_g, yg.reshape(TOP_K, n, dw))


def _experts_kernel(be_ref, first_ref, parts_ref, nxt_ref, slot_ref, nb_ref,
                    xs_ref, bgu_ref, bd_ref, wgu_hbm, wd_hbm, ys_ref,
                    wgu_st, wd_st, wgu_bf, wd_bf, sem_w):
    dff = wd_bf.shape[0]
    bm = xs_ref.shape[0] // EXPERT_BLOCKS_PER_STEP

    def fetch(ex, sl):
        return (pltpu.make_async_copy(wgu_hbm.at[ex], wgu_st.at[sl], sem_w.at[0, sl]),
                pltpu.make_async_copy(wd_hbm.at[ex], wd_st.at[sl], sem_w.at[1, sl]))

    @pl.when(pl.program_id(0) == 0)
    def _():
        e0 = be_ref[0]
        for cp in fetch(e0, slot_ref[e0]):
            cp.start()

    for sub in range(EXPERT_BLOCKS_PER_STEP):
        blk = pl.program_id(0) * EXPERT_BLOCKS_PER_STEP + sub
        e = be_ref[blk]
        rows = slice(sub * bm, (sub + 1) * bm)

        @pl.when(first_ref[blk] == 1)
        def _(e=e):
            sl = slot_ref[e]
            for cp in fetch(e, sl):
                cp.wait()
            nx = nxt_ref[e]

            @pl.when(nx >= 0)
            def _():
                for cp in fetch(nx, 1 - sl):
                    cp.start(priority=1)

            wgu_bf[...] = wgu_st[sl].astype(BF16)
            wd_bf[...] = wd_st[sl].astype(BF16)

        def mlp(e, rows):
            xb = jnp.concatenate(_unpack_halves(xs_ref[rows, :]), axis=1).astype(BF16)
            gu = _dot(xb, wgu_bf[...]) + bgu_ref[e]
            x_glu = jnp.minimum(gu[:, :dff], SWIGLU_LIMIT)
            x_lin = jnp.clip(gu[:, dff:], -SWIGLU_LIMIT, SWIGLU_LIMIT)
            act = x_glu * jax.nn.sigmoid(SWIGLU_ALPHA * x_glu) * (x_lin + 1.0)
            ys_ref[rows, :] = _pack_halves(_dot(act.astype(BF16), wd_bf[...]) + bd_ref[e])

        def clear(rows):
            ys_ref[rows, :] = jnp.zeros((rows.stop - rows.start, ys_ref.shape[1]), jnp.uint32)

        part = bm // EXPERT_TAIL_PARTS
        for live in range(EXPERT_TAIL_PARTS + 1):
            @pl.when(parts_ref[blk] == live)
            def _(e=e, rows=rows, live=live):
                cut = rows.start + live * part
                if live:
                    mlp(e, slice(rows.start, cut))
                if live < EXPERT_TAIL_PARTS:
                    clear(slice(cut, rows.stop))


def _expert_schedule(counts, bm, n_blocks):
    ne = counts.shape[0]
    ids = jnp.arange(ne, dtype=I32)
    upto = ids[None, :] <= ids[:, None]
    padded = (counts + bm - 1) // bm * bm
    pend = jnp.sum(jnp.where(upto, padded[None, :], 0), axis=1)
    pstart = (pend - padded).astype(I32)
    nb_used = (jnp.sum(padded) // bm).astype(I32)
    blk = jnp.arange(n_blocks, dtype=I32)
    blk_c = jnp.minimum(blk, nb_used - 1)
    be = jnp.minimum(jnp.sum((pend[None, :] <= (blk_c * bm)[:, None]).astype(I32), axis=1),
                     ne - 1)
    started = jnp.sum(((pstart[None, :] == (blk * bm)[:, None]) & (padded[None, :] > 0))
                      .astype(I32), axis=1)
    first = ((blk < nb_used) & (started > 0)).astype(I32)
    used = counts > 0
    seq = jnp.sum(jnp.where(upto & used[None, :], 1, 0), axis=1) - 1
    later = used[None, :] & (ids[None, :] > ids[:, None])
    nxt = jnp.where(jnp.any(later, axis=1), jnp.argmax(later, axis=1), -1).astype(I32)
    slot = (seq & 1).astype(I32)
    hit = ids[None, :] == be[:, None]
    left = jnp.sum(jnp.where(hit, (pstart + counts)[None, :], 0), axis=1) - blk * bm
    part = bm // EXPERT_TAIL_PARTS
    parts = jnp.where(blk < nb_used,
                      jnp.minimum((left + part - 1) // part, EXPERT_TAIL_PARTS), 0).astype(I32)
    return pstart, nb_used.reshape(1), be, first, parts, nxt, slot


def _experts(be, first, parts, nxt, slot, nb_used, xs, w_gu, b_gu, w_down, b_down, bm):
    rows, dw = xs.shape
    ne, d, dff2 = w_gu.shape
    dff = dff2 // 2
    sb = EXPERT_BLOCKS_PER_STEP * bm
    assert rows % sb == 0
    last = lambda j, be, fi, ha, nx, sl, nb: (
        jnp.maximum(jnp.minimum(j, (nb[0] - 1) // EXPERT_BLOCKS_PER_STEP), 0), 0)
    const = lambda *shape: pl.BlockSpec(shape,
                                        lambda j, be, fi, ha, nx, sl, nb: (0,) * len(shape))
    grid_spec = pltpu.PrefetchScalarGridSpec(
        num_scalar_prefetch=6,
        grid=(rows // sb,),
        in_specs=[
            pl.BlockSpec((sb, dw), last),
            const(ne, 1, dff2),
            const(ne, 1, d),
            pl.BlockSpec(memory_space=pl.ANY),
            pl.BlockSpec(memory_space=pl.ANY),
        ],
        out_specs=pl.BlockSpec((sb, dw), lambda j, be, fi, ha, nx, sl, nb: (j, 0)),
        scratch_shapes=[pltpu.VMEM((2, d, dff2), F32), pltpu.VMEM((2, dff, d), F32),
                        pltpu.VMEM((d, dff2), BF16), pltpu.VMEM((dff, d), BF16),
                        pltpu.SemaphoreType.DMA((2, 2))],
    )
    return pl.pallas_call(
        _experts_kernel,
        grid_spec=grid_spec,
        out_shape=jax.ShapeDtypeStruct((rows, dw), jnp.uint32),
        compiler_params=_cparams(1),
        name="experts",
    )(be, first, parts, nxt, slot, nb_used, xs, b_gu.reshape(ne, 1, dff2),
      b_down.reshape(ne, 1, d), w_gu, w_down)


MIXER_TILE = 512
ATTN_HEADS_PER_STEP = 4
POST_TILE = 512
DEST_TILE = 2048
COMBINE_TILE = 1024
EXPERT_BLOCK = 1024
EXPERT_TAIL_PARTS = 4
EXPERT_BLOCKS_PER_STEP = 1


def kernel(x, mem, norm_mix_g, w_in, lambda_q1, lambda_k1, lambda_q2, lambda_k2, rel_bias,
           subln_g, w_attn_up, pool_mix, pool_scale, w_pool_up, w_gate, b_gate, w_out,
           norm_x_g, norm_mem_g, w_xq, w_xkv, w_xo, norm_ffn_g, w_router, b_router,
           w_gu, b_gu, w_down, b_down, final_norm_g):
    b, s, d = x.shape
    n = b * s
    assert w_in.shape[0] == 1, "single-layer block"
    row = lambda a: a.reshape(1, -1)
    bf = lambda a: a[0].astype(BF16)

    q, k, v, g0, gyb = _mixer_in(
        x, row(norm_mix_g[0]), bf(w_in), bf(w_gate), row(b_gate[0]), bf(pool_mix),
        row(pool_scale[0]), bf(w_pool_up), MIXER_TILE)

    bias, lam = _rel_bias(rel_bias, row(lambda_q1[0]), row(lambda_k1[0]),
                          row(lambda_q2[0]), row(lambda_k2[0]), s)
    a = _diff_attn(q, k, v, bias, lam, row(subln_g[0]), ATTN_HEADS_PER_STEP)

    qk_fold, vo_fold = _mem_fold(mem, row(norm_mem_g[0]), bf(w_xkv), bf(w_xq), bf(w_xo))

    x2, hf, eidx, rank, gate_tm, counts = _post(
        x.reshape(n, d), a.reshape(n, d), g0, gyb, bf(w_attn_up), bf(w_out),
        row(norm_x_g[0]), qk_fold, vo_fold, row(norm_ffn_g[0]),
        w_router[0].T, b_router[0].reshape(-1, 1), b, POST_TILE)

    bm = EXPERT_BLOCK
    rows = n * TOP_K + N_EXPERTS * bm
    pstart, nb_used, be, first, parts, nxt, slot = _expert_schedule(
        counts[:, 0], bm, rows // bm)

    dest_w = _worker_index_layout(_route_dest(pstart, eidx, rank, DEST_TILE), n)
    xs = _sc_dispatch(hf, dest_w, rows)
    ys = _experts(be, first, parts, nxt, slot, nb_used, xs, w_gu[0], b_gu[0], w_down[0],
                  b_down[0], bm)
    yg = _sc_gather(ys, dest_w, n)
    out = _combine(x2, gate_tm, row(final_norm_g), yg, COMBINE_TILE)
    return out.reshape(b, s, d)
```

```python
import functools
import math

import numpy as np
import jax
import jax.numpy as jnp
from jax import lax
from jax.experimental import pallas as pl
from jax.experimental.pallas import tpu as pltpu
from jax.experimental.pallas import tpu_sc as plsc

F32 = jnp.float32
BF16 = jnp.bfloat16
I32 = jnp.int32

EPS = 1e-6
CHUNK = 64
N_HEADS = 8
HEAD_DIM = 64
HEAD_W = 2 * HEAD_DIM
POOL_WINDOWS = (2, 4, 8, 16)
POOL_GROUP = 128
POOL_PAD = 16
N_BUCKETS = 32
MAX_DISTANCE = 128
X_HEADS = 4
N_EXPERTS = 32
TOP_K = 4
SWIGLU_ALPHA = 1.702
SWIGLU_LIMIT = 7.0
LAMBDA_INIT = 0.8 - 0.6 * math.exp(-0.3 * 0)

LANES = 128
SUBLANES = 8
ATTN_BLOCK = 256
VMEM_LIMIT = 56 * 1024 * 1024


def _cparams(n_axes, vmem=VMEM_LIMIT):
    return pltpu.CompilerParams(
        dimension_semantics=("arbitrary",) * n_axes, vmem_limit_bytes=vmem)


def _rms(xf, g):
    ms = jnp.mean(xf * xf, axis=-1, keepdims=True)
    return xf * lax.rsqrt(ms + EPS) * g


def _dot(a, b):
    return jnp.dot(a, b, preferred_element_type=F32)


def _pack_halves(x):
    c = x.shape[1] // 2
    as_bits = lambda v: lax.bitcast_convert_type(v.astype(BF16).astype(F32), jnp.uint32)
    return as_bits(x[:, :c]) | (as_bits(x[:, c:]) >> 16)


def _unpack_halves(w):
    hi = lax.bitcast_convert_type(w & jnp.uint32(0xFFFF0000), F32)
    lo = lax.bitcast_convert_type(w << 16, F32)
    return hi, lo


def _dot_nt(a, b, precision=None):
    return lax.dot_general(a, b, (((1,), (1,)), ((), ())),
                           preferred_element_type=F32, precision=precision)


def _mixer_in_kernel(x_ref, g_ref, wq_ref, wk_ref, wv_ref, wu_ref, wg_ref, bg_ref,
                     pmix_ref, pscale_ref, wpu_ref,
                     q_ref, k_ref, v_ref, g0_ref, gyb_ref, ext_ref, wpool_ref):
    ts = x_ref.shape[1]
    d = x_ref.shape[2]
    j = pl.program_id(1)

    @pl.when((pl.program_id(0) == 0) & (j == 0))
    def _():
        for gi in range(len(POOL_WINDOWS)):
            sl = slice(gi * POOL_GROUP, (gi + 1) * POOL_GROUP)
            mix = (pmix_ref[gi].astype(F32) * pscale_ref[:, sl]).astype(BF16)
            wpool_ref[sl, :] = _dot(mix, wpu_ref[sl, :]).astype(BF16)

    h = _rms(x_ref[0], g_ref[...]).astype(BF16)
    q_ref[0] = (_dot(h, wq_ref[...]) * (HEAD_DIM ** -0.5)).astype(BF16)
    k_ref[0] = _dot(h, wk_ref[...]).astype(BF16)
    v_ref[0] = _dot(h, wv_ref[...]).astype(BF16)
    u = _dot(h, wu_ref[...])

    @pl.when(j == 0)
    def _():
        ext_ref[0:POOL_PAD, :] = jnp.zeros((POOL_PAD, u.shape[1]), F32)

    ext_ref[POOL_PAD:POOL_PAD + ts, :] = u
    e = ext_ref[...]
    sums = {}
    s = e
    w = 1
    while w < max(POOL_WINDOWS):
        s = s + pltpu.roll(s, w, 0)
        w *= 2
        sums[w] = s
    ext_ref[0:POOL_PAD, :] = ext_ref[ts:ts + POOL_PAD, :]

    pos = (j * ts + lax.broadcasted_iota(I32, (ts, 1), 0) + 1).astype(F32)
    pooled = []
    for gi, w in enumerate(POOL_WINDOWS):
        sl = slice(gi * POOL_GROUP, (gi + 1) * POOL_GROUP)
        win = sums[w][POOL_PAD:, sl]
        pooled.append(win / jnp.minimum(pos, float(w)) - u[:, sl])
    y_b = _dot(jnp.concatenate(pooled, axis=1).astype(BF16), wpool_ref[...])

    gate = jax.nn.sigmoid(_dot(h, wg_ref[...]) + bg_ref[...])
    g0_ref[...] = gate[:, :d].astype(BF16)
    gyb_ref[...] = (gate[:, d:] * y_b).astype(BF16)


def _mixer_in(x, norm_g, w_in, w_gate, b_gate, pool_mix, pool_scale, w_pool_up, ts):
    b, s, d = x.shape
    aw = N_HEADS * HEAD_W
    pw = len(POOL_WINDOWS) * POOL_GROUP
    n = b * s
    nt = s // ts
    const = lambda *shape: pl.BlockSpec(shape, lambda bi, j: (0,) * len(shape))
    tok3 = pl.BlockSpec((1, ts, aw), lambda bi, j: (bi, j, 0))
    tok2 = pl.BlockSpec((ts, d), lambda bi, j: (bi * nt + j, 0))
    return pl.pallas_call(
        _mixer_in_kernel,
        grid=(b, nt),
        in_specs=[
            pl.BlockSpec((1, ts, d), lambda bi, j: (bi, j, 0)),
            const(1, d),
            pl.BlockSpec((d, aw), lambda bi, j: (0, 0)),
            pl.BlockSpec((d, aw), lambda bi, j: (0, 1)),
            pl.BlockSpec((d, aw), lambda bi, j: (0, 2)),
            pl.BlockSpec((d, pw), lambda bi, j: (0, 3 * aw // pw)),
            const(d, 2 * d),
            const(1, 2 * d),
            const(len(POOL_WINDOWS), POOL_GROUP, POOL_GROUP),
            const(1, pw),
            const(pw, d),
        ],
        out_specs=[tok3, tok3, tok3, tok2, tok2],
        out_shape=[jax.ShapeDtypeStruct((b, s, aw), BF16)] * 3
        + [jax.ShapeDtypeStruct((n, d), BF16)] * 2,
        scratch_shapes=[pltpu.VMEM((ts + POOL_PAD, pw), F32), pltpu.VMEM((pw, d), BF16)],
        compiler_params=_cparams(2),
        name="mixer_in",
    )(x, norm_g, w_in, w_in, w_in, w_in, w_gate, b_gate, pool_mix, pool_scale, w_pool_up)


def _rel_bucket(rel, log=jnp.log, f32=lambda a: a.astype(jnp.float32),
                i32=lambda a: a.astype(jnp.int32), xp=jnp):
    nb = N_BUCKETS // 2
    ret = i32(rel > 0) * nb
    n = xp.abs(rel)
    max_exact = nb // 2
    nf = f32(xp.maximum(n, 1))
    large = max_exact + i32(log(nf / max_exact) / math.log(MAX_DISTANCE / max_exact)
                            * (nb - max_exact))
    large = xp.minimum(large, nb - 1)
    return ret + xp.where(n < max_exact, n, large)


def _far_bucket(block, seq):
    rel = -np.arange(block + 1, max(seq, block + 2), dtype=np.int32)
    bk = _rel_bucket(rel, log=np.log, f32=lambda a: a.astype(np.float32),
                     i32=lambda a: a.astype(np.int32), xp=np)
    assert (bk == bk[0]).all(), "far keys must share one relative-position bucket"
    return int(bk[0])


def _rel_bias_kernel(far_bucket, tab_ref, bidx_ref, lq1_ref, lk1_ref, lq2_ref, lk2_ref,
                     bias_ref, lam_ref):
    h = pl.program_id(0)
    bidx = bidx_ref[...]
    acc = jnp.zeros(bidx.shape, F32)
    for bkt in range(N_BUCKETS):
        acc = jnp.where(bidx == bkt, tab_ref[bkt, h], acc)
    acc = acc - tab_ref[far_bucket, h]
    bias_ref[0] = jnp.where(bidx < 0, -jnp.inf, acc)
    lam = (jnp.exp(jnp.sum(lq1_ref[...] * lk1_ref[...], keepdims=True))
           - jnp.exp(jnp.sum(lq2_ref[...] * lk2_ref[...], keepdims=True)) + LAMBDA_INIT)
    lam_ref[...] = jnp.broadcast_to(lam, lam_ref.shape)


def _rel_bias(rel_bias, lq1, lk1, lq2, lk2, seq):
    blk = ATTN_BLOCK
    qpos = jnp.arange(blk, dtype=I32)[:, None]
    kpos = jnp.arange(blk, dtype=I32)[None, :]
    diag = jnp.where(kpos // CHUNK <= qpos // CHUNK, _rel_bucket(kpos - qpos), -1)
    prev = _rel_bucket(kpos - (qpos + blk))
    bidx = jnp.stack([diag, prev]).astype(I32)
    vec = pl.BlockSpec((1, HEAD_DIM), lambda h: (0, 0))
    return pl.pallas_call(
        functools.partial(_rel_bias_kernel, _far_bucket(blk, seq)),
        grid=(N_HEADS,),
        in_specs=[
            pl.BlockSpec(memory_space=pltpu.SMEM),
            pl.BlockSpec((2, blk, blk), lambda h: (0, 0, 0)),
            vec, vec, vec, vec,
        ],
        out_specs=[
            pl.BlockSpec((1, 2, blk, blk), lambda h: (h, 0, 0, 0)),
            pl.BlockSpec((SUBLANES, LANES), lambda h: (0, 0)),
        ],
        out_shape=[jax.ShapeDtypeStruct((N_HEADS, 2, blk, blk), F32),
                   jax.ShapeDtypeStruct((SUBLANES, LANES), F32)],
        compiler_params=_cparams(1),
        name="rel_bias",
    )(rel_bias, bidx, lq1, lk1, lq2, lk2)


def _diff_attn_kernel(q_ref, k_ref, v_ref, bias_ref, lam_ref, sg_ref, o_ref, vext_ref):
    s_len = q_ref.shape[1]
    tq = ATTN_BLOCK
    n_heads = q_ref.shape[2] // HEAD_W
    lam = lam_ref[0:1, 0:1]
    lane = lax.broadcasted_iota(I32, (tq, HEAD_W), 1)
    cols = [slice(hh * HEAD_W, (hh + 1) * HEAD_W) for hh in range(n_heads)]
    b_diag, b_prev = [], []
    for hh in range(n_heads):
        vext_ref[hh, :, :HEAD_W] = v_ref[0, :, cols[hh]]
        vext_ref[hh, :, HEAD_W:] = jnp.ones((s_len, HEAD_W), BF16)
        b_diag.append(jnp.concatenate([bias_ref[hh, 0]] * 2, axis=0))
        b_prev.append(jnp.concatenate([bias_ref[hh, 1]] * 2, axis=0))

    for qi in range(s_len // tq):
        n_keys = (qi + 1) * tq
        rows = slice(qi * tq, (qi + 1) * tq)
        for hh in range(n_heads):
            q = q_ref[0, rows, cols[hh]]
            zero = jnp.zeros_like(q)
            qs = jnp.concatenate([jnp.where(lane < HEAD_DIM, q, zero),
                                  jnp.where(lane >= HEAD_DIM, q, zero)], axis=0)
            s = _dot_nt(qs, k_ref[0, :n_keys, cols[hh]])
            pieces = []
            if qi >= 2:
                pieces.append(s[:, :n_keys - 2 * tq])
            if qi >= 1:
                pieces.append(s[:, n_keys - 2 * tq:n_keys - tq] + b_prev[hh])
            pieces.append(s[:, n_keys - tq:] + b_diag[hh])
            s = jnp.concatenate(pieces, axis=1) if len(pieces) > 1 else pieces[0]
            m = jnp.max(s, axis=-1, keepdims=True)
            p = jnp.exp(s - m).astype(BF16)
            acc = _dot(p, vext_ref[hh, :n_keys, :])
            o = acc[:, :HEAD_W] / acc[:, HEAD_W:HEAD_W + 1]
            a = o[:tq] - lam * o[tq:]
            o_ref[0, rows, cols[hh]] = (
                _rms(a, sg_ref[...]) * (1.0 - LAMBDA_INIT)).astype(BF16)


def _diff_attn(q, k, v, bias, lam, subln_g, heads_per_step):
    b, s, aw = q.shape
    tq = ATTN_BLOCK
    hp = heads_per_step
    spec = pl.BlockSpec((1, s, hp * HEAD_W), lambda bi, h: (bi, 0, h))
    return pl.pallas_call(
        _diff_attn_kernel,
        grid=(b, N_HEADS // hp),
        in_specs=[
            spec, spec, spec,
            pl.BlockSpec((hp, 2, tq, tq), lambda bi, h: (h, 0, 0, 0)),
            pl.BlockSpec((SUBLANES, LANES), lambda bi, h: (0, 0)),
            pl.BlockSpec((1, HEAD_W), lambda bi, h: (0, 0)),
        ],
        out_specs=spec,
        out_shape=jax.ShapeDtypeStruct((b, s, aw), BF16),
        scratch_shapes=[pltpu.VMEM((hp, s, 2 * HEAD_W), BF16)],
        compiler_params=_cparams(2),
        name="diff_attn",
    )(q, k, v, bias, lam, subln_g)


def _mem_fold_kernel(m_ref, g_ref, wkv_ref, wq_ref, wo_ref, qk_ref, vo_ref):
    d = m_ref.shape[2]
    hd = d // X_HEADS
    m = _rms(m_ref[0], g_ref[...]).astype(BF16)
    kv = _dot(m, wkv_ref[...])
    k = kv[:, :d].astype(BF16)
    v = kv[:, d:].astype(BF16)
    for hh in range(X_HEADS):
        sl = slice(hh * hd, (hh + 1) * hd)
        qk_ref[0, :, hh * m.shape[0]:(hh + 1) * m.shape[0]] = (
            _dot_nt(wq_ref[:, sl], k[:, sl]) * (hd ** -0.5)).astype(BF16)
        vo_ref[0, hh * m.shape[0]:(hh + 1) * m.shape[0], :] = _dot(v[:, sl], wo_ref[sl, :]).astype(BF16)


def _mem_fold(mem, norm_g, w_xkv, w_xq, w_xo):
    b, ml, d = mem.shape
    const = lambda *shape: pl.BlockSpec(shape, lambda bi: (0,) * len(shape))
    return pl.pallas_call(
        _mem_fold_kernel,
        grid=(b,),
        in_specs=[pl.BlockSpec((1, ml, d), lambda bi: (bi, 0, 0)), const(1, d),
                  const(d, 2 * d), const(d, d), const(d, d)],
        out_specs=[pl.BlockSpec((1, d, X_HEADS * ml), lambda bi: (bi, 0, 0)),
                   pl.BlockSpec((1, X_HEADS * ml, d), lambda bi: (bi, 0, 0))],
        out_shape=[jax.ShapeDtypeStruct((b, d, X_HEADS * ml), BF16),
                   jax.ShapeDtypeStruct((b, X_HEADS * ml, d), BF16)],
        compiler_params=_cparams(1),
        name="mem_fold",
    )(mem, norm_g, w_xkv, w_xq, w_xo)


def _post_kernel(x_ref, a_ref, g0_ref, gyb_ref, wau_ref, wo_ref, nxg_ref,
                 qk_ref, vo_ref, nfg_ref, wrt_ref, br_ref,
                 x2_ref, hf_ref, eidx_ref, rank_ref, gate_ref, cnt_ref, run_ref, tri_ref):
    ts, d = x_ref.shape
    first = (pl.program_id(0) == 0) & (pl.program_id(1) == 0)

    y_a = _dot(a_ref[...], wau_ref[...])
    merged = g0_ref[...].astype(F32) * y_a + gyb_ref[...].astype(F32)
    x1 = x_ref[...] + _dot(merged.astype(BF16), wo_ref[...])

    ml = qk_ref.shape[2] // X_HEADS
    hx = _rms(x1, nxg_ref[...]).astype(BF16)
    s_all = _dot(hx, qk_ref[0])
    probs = []
    for hh in range(X_HEADS):
        s = s_all[:, hh * ml:(hh + 1) * ml]
        p = jnp.exp(s - jnp.max(s, axis=-1, keepdims=True))
        probs.append((p / jnp.sum(p, axis=-1, keepdims=True)).astype(BF16))
    x2 = x1 + _dot(jnp.concatenate(probs, axis=1), vo_ref[0])
    x2_ref[...] = x2
    hf = _rms(x2, nfg_ref[...])
    hf_ref[...] = _pack_halves(hf)

    wr = wrt_ref[...]
    wr_hi = wr.astype(BF16)
    wr_lo = (wr - wr_hi.astype(F32)).astype(BF16)
    hf_hi = hf.astype(BF16)
    hf_lo = (hf - hf_hi.astype(F32)).astype(BF16)
    logits = (_dot_nt(wr_hi, hf_hi) + (_dot_nt(wr_hi, hf_lo) + _dot_nt(wr_lo, hf_hi))
              + br_ref[...])
    ne = logits.shape[0]
    eid = lax.broadcasted_iota(I32, logits.shape, 0).astype(F32)
    work = logits
    vals, idxs, hots = [], [], []
    for _ in range(TOP_K):
        mx = jnp.max(work, axis=0, keepdims=True)
        idx = jnp.min(jnp.where(work == mx, eid, float(ne)), axis=0, keepdims=True)
        hot = eid == idx
        vals.append(mx)
        idxs.append(idx.astype(I32))
        hots.append(hot)
        work = jnp.where(hot, -jnp.inf, work)
    ex = [jnp.exp(vv - vals[0]) for vv in vals]
    den = ex[0] + ex[1] + ex[2] + ex[3]
    gates = [e_ / den for e_ in ex]

    multi = (hots[0] | hots[1] | hots[2] | hots[3])
    multi_f = jnp.where(multi, 1.0, 0.0).astype(F32)

    @pl.when(first)
    def _():
        run_ref[...] = jnp.zeros(run_ref.shape, F32)
        tri = (lax.broadcasted_iota(I32, (ts, ts), 0)
               < lax.broadcasted_iota(I32, (ts, ts), 1))
        tri_ref[...] = jnp.where(tri, 1.0, 0.0).astype(BF16)

    before = _dot(multi_f.astype(BF16), tri_ref[...])

    run = run_ref[...]
    pos = before + run
    ranks = [jnp.sum(jnp.where(hot, pos, 0.0), axis=0, keepdims=True) for hot in hots]
    run_new = run + jnp.sum(multi_f, axis=1, keepdims=True)
    run_ref[...] = run_new
    cnt_ref[...] = jnp.broadcast_to(run_new, cnt_ref.shape).astype(I32)

    eidx_ref[...] = jnp.concatenate(idxs, axis=0)
    rank_ref[...] = jnp.concatenate(ranks, axis=0).astype(I32)
    g_rows = jnp.concatenate(gates + [jnp.zeros((LANES - TOP_K, ts), F32)], axis=0)
    gate_ref[...] = g_rows.T


def _post(x2d, a2d, g0, gyb, w_attn_up, w_out, norm_x_g, qk_fold, vo_fold,
          norm_ffn_g, w_router_t, b_router, batch, ts):
    n, d = x2d.shape
    nt = n // batch // ts
    tok = pl.BlockSpec((ts, d), lambda bi, j: (bi * nt + j, 0))
    lanes = pl.BlockSpec((TOP_K, ts), lambda bi, j: (0, bi * nt + j))
    const = lambda *shape: pl.BlockSpec(shape, lambda bi, j: (0,) * len(shape),
                                        pipeline_mode=pl.Buffered(1))
    fold = lambda arr: pl.BlockSpec((1,) + arr.shape[1:], lambda bi, j: (bi, 0, 0))
    return pl.pallas_call(
        _post_kernel,
        grid=(batch, nt),
        in_specs=[tok, tok, tok, tok, const(d, d), const(d, d), const(1, d),
                  fold(qk_fold), fold(vo_fold), const(1, d), const(N_EXPERTS, d),
                  const(N_EXPERTS, 1)],
        out_specs=[tok, pl.BlockSpec((ts, d // 2), lambda bi, j: (bi * nt + j, 0)),
                   lanes, lanes,
                   pl.BlockSpec((ts, LANES), lambda bi, j: (bi * nt + j, 0)),
                   pl.BlockSpec((N_EXPERTS, LANES), lambda bi, j: (0, 0))],
        out_shape=[jax.ShapeDtypeStruct((n, d), F32),
                   jax.ShapeDtypeStruct((n, d // 2), jnp.uint32),
                   jax.ShapeDtypeStruct((TOP_K, n), I32), jax.ShapeDtypeStruct((TOP_K, n), I32),
                   jax.ShapeDtypeStruct((n, LANES), F32),
                   jax.ShapeDtypeStruct((N_EXPERTS, LANES), I32)],
        scratch_shapes=[pltpu.VMEM((N_EXPERTS, 1), F32), pltpu.VMEM((ts, ts), BF16)],
        compiler_params=_cparams(2),
        name="post",
    )(x2d, a2d, g0, gyb, w_attn_up, w_out, norm_x_g, qk_fold, vo_fold,
      norm_ffn_g, w_router_t, b_router)


def _route_dest_kernel(pstart_ref, eidx_ref, rank_ref, dest_ref):
    eidx = eidx_ref[...]
    start = jnp.zeros(eidx.shape, I32)
    for e in range(N_EXPERTS):
        start = jnp.where(eidx == e, pstart_ref[e], start)
    dest_ref[...] = start + rank_ref[...]


def _route_dest(pstart, eidx, rank, tl):
    k, n = eidx.shape
    lanes = pl.BlockSpec((k, tl), lambda i: (0, i))
    return pl.pallas_call(
        _route_dest_kernel,
        grid=(n // tl,),
        in_specs=[pl.BlockSpec(memory_space=pltpu.SMEM), lanes, lanes],
        out_specs=lanes,
        out_shape=jax.ShapeDtypeStruct((k, n), I32),
        compiler_params=_cparams(1),
        name="route_dest",
    )(pstart, eidx, rank)


SC_CORES = 2
SC_SUBCORES = 16
SC_WORKERS = SC_CORES * SC_SUBCORES
MOVE_CHUNK = 64
MOVE_BUFFERS = 3


def _worker_index_layout(dest, n):
    per_w = n // SC_WORKERS
    n_chunks = per_w // MOVE_CHUNK
    d4 = dest.reshape(TOP_K, SC_WORKERS, n_chunks, MOVE_CHUNK)
    return jnp.transpose(d4, (1, 2, 0, 3)).reshape(SC_WORKERS, n_chunks * TOP_K, MOVE_CHUNK)


def _sc_mesh():
    return plsc.VectorSubcoreMesh(core_axis_name="c", subcore_axis_name="s",
                                  num_cores=SC_CORES, num_subcores=SC_SUBCORES)


def _sc_dispatch(hf, dest_w, rows):
    n, d = hf.shape
    per_w = n // SC_WORKERS
    n_chunks = per_w // MOVE_CHUNK

    def body(hf_hbm, dest_hbm, xs_hbm, idx_v, rows_v, rsem, wsem):
        wid = lax.axis_index("s") * SC_CORES + lax.axis_index("c")
        base = wid * per_w
        pltpu.sync_copy(dest_hbm.at[wid], idx_v)

        def read(c):
            b = c % MOVE_BUFFERS
            return pltpu.async_copy(hf_hbm.at[pl.ds(base + c * MOVE_CHUNK, MOVE_CHUNK)],
                                    rows_v.at[b], rsem.at[b])

        scatters = [[] for _ in range(MOVE_BUFFERS)]
        pending = read(0)
        for c in range(n_chunks):
            b = c % MOVE_BUFFERS
            pending.wait()
            if c + 1 < n_chunks:
                for cp in scatters[(c + 1) % MOVE_BUFFERS]:
                    cp.wait()
                pending = read(c + 1)
            scatters[b] = [
                pltpu.async_copy(rows_v.at[b], xs_hbm.at[idx_v.at[c * TOP_K + kk]], wsem.at[b])
                for kk in range(TOP_K)]
        for group in scatters:
            for cp in group:
                cp.wait()

    return pl.kernel(
        body,
        out_type=jax.ShapeDtypeStruct((rows, d), hf.dtype),
        mesh=_sc_mesh(),
        scratch_types=[pltpu.VMEM((n_chunks * TOP_K, MOVE_CHUNK), I32),
                       pltpu.VMEM((MOVE_BUFFERS, MOVE_CHUNK, d), hf.dtype),
                       pltpu.SemaphoreType.DMA((MOVE_BUFFERS,)),
                       pltpu.SemaphoreType.DMA((MOVE_BUFFERS,))],
        name="sc_dispatch",
    )(hf, dest_w)


def _sc_gather(ys, dest_w, n):
    _, d = ys.shape
    per_w = n // SC_WORKERS
    n_chunks = per_w // MOVE_CHUNK

    def body(ys_hbm, dest_hbm, yg_hbm, idx_v, rows_v, rsem, wsem):
        wid = lax.axis_index("s") * SC_CORES + lax.axis_index("c")
        base = wid * per_w
        pltpu.sync_copy(dest_hbm.at[wid], idx_v)
        n_moves = n_chunks * TOP_K

        def gather(m):
            b = m % MOVE_BUFFERS
            return pltpu.async_copy(ys_hbm.at[idx_v.at[m]], rows_v.at[b], rsem.at[b])

        def write(m):
            b = m % MOVE_BUFFERS
            c, kk = divmod(m, TOP_K)
            return pltpu.async_copy(
                rows_v.at[b], yg_hbm.at[pl.ds(kk * n + base + c * MOVE_CHUNK, MOVE_CHUNK)],
                wsem.at[b])

        writes = [None] * MOVE_BUFFERS
        pending = gather(0)
        for m in range(n_moves):
            pending.wait()
            if m + 1 < n_moves:
                nb = (m + 1) % MOVE_BUFFERS
                if writes[nb] is not None:
                    writes[nb].wait()
                pending = gather(m + 1)
            writes[m % MOVE_BUFFERS] = write(m)
        for wr in writes:
            if wr is not None:
                wr.wait()

    return pl.kernel(
        body,
        out_type=jax.ShapeDtypeStruct((TOP_K * n, d), ys.dtype),
        mesh=_sc_mesh(),
        scratch_types=[pltpu.VMEM((n_chunks * TOP_K, MOVE_CHUNK), I32),
                       pltpu.VMEM((MOVE_BUFFERS, MOVE_CHUNK, d), ys.dtype),
                       pltpu.SemaphoreType.DMA((MOVE_BUFFERS,)),
                       pltpu.SemaphoreType.DMA((MOVE_BUFFERS,))],
        name="sc_gather",
    )(ys, dest_w)


def _combine_kernel(x2_ref, gate_ref, fg_ref, yg_ref, o_ref):
    g = gate_ref[...]
    x2 = x2_ref[...]
    c = x2.shape[1] // 2
    acc_hi, acc_lo = x2[:, :c], x2[:, c:]
    for kk in range(TOP_K):
        hi, lo = _unpack_halves(yg_ref[kk])
        acc_hi = acc_hi + g[:, kk:kk + 1] * hi
        acc_lo = acc_lo + g[:, kk:kk + 1] * lo
    o_ref[...] = _rms(jnp.concatenate([acc_hi, acc_lo], axis=1), fg_ref[...])


def _combine(x2, gate_tm, final_g, yg, ts):
    n, d = x2.shape
    dw = yg.shape[1]
    return pl.pallas_call(
        _combine_kernel,
        grid=(n // ts,),
        in_specs=[pl.BlockSpec((ts, d), lambda i: (i, 0)),
                  pl.BlockSpec((ts, LANES), lambda i: (i, 0)),
                  pl.BlockSpec((1, d), lambda i: (0, 0)),
                  pl.BlockSpec((TOP_K, ts, dw), lambda i: (0, i, 0))],
        out_specs=pl.BlockSpec((ts, d), lambda i: (i, 0)),
        out_shape=jax.ShapeDtypeStruct((n, d), F32),
        compiler_params=_cparams(1),
        name="combine",
    )(x2, gate_tm, final_g, yg.reshape(TOP_K, n, dw))


def _experts_kernel(be_ref, first_ref, parts_ref, nxt_ref, slot_ref, nb_ref,
                    xs_ref, bgu_ref, bd_ref, wgu_hbm, wd_hbm, ys_ref,
                    wgu_st, wd_st, wgu_bf, wd_bf, sem_w):
    dff = wd_bf.shape[0]
    bm = xs_ref.shape[0] // EXPERT_BLOCKS_PER_STEP

    def fetch(ex, sl):
        return (pltpu.make_async_copy(wgu_hbm.at[ex], wgu_st.at[sl], sem_w.at[0, sl]),
                pltpu.make_async_copy(wd_hbm.at[ex], wd_st.at[sl], sem_w.at[1, sl]))

    @pl.when(pl.program_id(0) == 0)
    def _():
        e0 = be_ref[0]
        for cp in fetch(e0, slot_ref[e0]):
            cp.start()

    for sub in range(EXPERT_BLOCKS_PER_STEP):
        blk = pl.program_id(0) * EXPERT_BLOCKS_PER_STEP + sub
        e = be_ref[blk]
        rows = slice(sub * bm, (sub + 1) * bm)

        @pl.when(first_ref[blk] == 1)
        def _(e=e):
            sl = slot_ref[e]
            for cp in fetch(e, sl):
                cp.wait()
            nx = nxt_ref[e]

            @pl.when(nx >= 0)
            def _():
                for cp in fetch(nx, 1 - sl):
                    cp.start(priority=1)

            wgu_bf[...] = wgu_st[sl].astype(BF16)
            wd_bf[...] = wd_st[sl].astype(BF16)

        def mlp(e, rows):
            xb = jnp.concatenate(_unpack_halves(xs_ref[rows, :]), axis=1).astype(BF16)
            gu = _dot(xb, wgu_bf[...]) + bgu_ref[e]
            x_glu = jnp.minimum(gu[:, :dff], SWIGLU_LIMIT)
            x_lin = jnp.clip(gu[:, dff:], -SWIGLU_LIMIT, SWIGLU_LIMIT)
            act = x_glu * jax.nn.sigmoid(SWIGLU_ALPHA * x_glu) * (x_lin + 1.0)
            ys_ref[rows, :] = _pack_halves(_dot(act.astype(BF16), wd_bf[...]) + bd_ref[e])

        def clear(rows):
            ys_ref[rows, :] = jnp.zeros((rows.stop - rows.start, ys_ref.shape[1]), jnp.uint32)

        part = bm // EXPERT_TAIL_PARTS
        for live in range(EXPERT_TAIL_PARTS + 1):
            @pl.when(parts_ref[blk] == live)
            def _(e=e, rows=rows, live=live):
                cut = rows.start + live * part
                if live:
                    mlp(e, slice(rows.start, cut))
                if live < EXPERT_TAIL_PARTS:
                    clear(slice(cut, rows.stop))


def _expert_schedule(counts, bm, n_blocks):
    ne = counts.shape[0]
    ids = jnp.arange(ne, dtype=I32)
    upto = ids[None, :] <= ids[:, None]
    padded = (counts + bm - 1) // bm * bm
    pend = jnp.sum(jnp.where(upto, padded[None, :], 0), axis=1)
    pstart = (pend - padded).astype(I32)
    nb_used = (jnp.sum(padded) // bm).astype(I32)
    blk = jnp.arange(n_blocks, dtype=I32)
    blk_c = jnp.minimum(blk, nb_used - 1)
    be = jnp.minimum(jnp.sum((pend[None, :] <= (blk_c * bm)[:, None]).astype(I32), axis=1),
                     ne - 1)
    started = jnp.sum(((pstart[None, :] == (blk * bm)[:, None]) & (padded[None, :] > 0))
                      .astype(I32), axis=1)
    first = ((blk < nb_used) & (started > 0)).astype(I32)
    used = counts > 0
    seq = jnp.sum(jnp.where(upto & used[None, :], 1, 0), axis=1) - 1
    later = used[None, :] & (ids[None, :] > ids[:, None])
    nxt = jnp.where(jnp.any(later, axis=1), jnp.argmax(later, axis=1), -1).astype(I32)
    slot = (seq & 1).astype(I32)
    hit = ids[None, :] == be[:, None]
    left = jnp.sum(jnp.where(hit, (pstart + counts)[None, :], 0), axis=1) - blk * bm
    part = bm // EXPERT_TAIL_PARTS
    parts = jnp.where(blk < nb_used,
                      jnp.minimum((left + part - 1) // part, EXPERT_TAIL_PARTS), 0).astype(I32)
    return pstart, nb_used.reshape(1), be, first, parts, nxt, slot


def _experts(be, first, parts, nxt, slot, nb_used, xs, w_gu, b_gu, w_down, b_down, bm):
    rows, dw = xs.shape
    ne, d, dff2 = w_gu.shape
    dff = dff2 // 2
    sb = EXPERT_BLOCKS_PER_STEP * bm
    assert rows % sb == 0
    last = lambda j, be, fi, ha, nx, sl, nb: (
        jnp.maximum(jnp.minimum(j, (nb[0] - 1) // EXPERT_BLOCKS_PER_STEP), 0), 0)
    const = lambda *shape: pl.BlockSpec(shape,
                                        lambda j, be, fi, ha, nx, sl, nb: (0,) * len(shape))
    grid_spec = pltpu.PrefetchScalarGridSpec(
        num_scalar_prefetch=6,
        grid=(rows // sb,),
        in_specs=[
            pl.BlockSpec((sb, dw), last),
            const(ne, 1, dff2),
            const(ne, 1, d),
            pl.BlockSpec(memory_space=pl.ANY),
            pl.BlockSpec(memory_space=pl.ANY),
        ],
        out_specs=pl.BlockSpec((sb, dw), lambda j, be, fi, ha, nx, sl, nb: (j, 0)),
        scratch_shapes=[pltpu.VMEM((2, d, dff2), F32), pltpu.VMEM((2, dff, d), F32),
                        pltpu.VMEM((d, dff2), BF16), pltpu.VMEM((dff, d), BF16),
                        pltpu.SemaphoreType.DMA((2, 2))],
    )
    return pl.pallas_call(
        _experts_kernel,
        grid_spec=grid_spec,
        out_shape=jax.ShapeDtypeStruct((rows, dw), jnp.uint32),
        compiler_params=_cparams(1),
        name="experts",
    )(be, first, parts, nxt, slot, nb_used, xs, b_gu.reshape(ne, 1, dff2),
      b_down.reshape(ne, 1, d), w_gu, w_down)


MIXER_TILE = 512
ATTN_HEADS_PER_STEP = 4
POST_TILE = 512
DEST_TILE = 2048
COMBINE_TILE = 1024
EXPERT_BLOCK = 1024
EXPERT_TAIL_PARTS = 4
EXPERT_BLOCKS_PER_STEP = 1


def kernel(x, mem, norm_mix_g, w_in, lambda_q1, lambda_k1, lambda_q2, lambda_k2, rel_bias,
           subln_g, w_attn_up, pool_mix, pool_scale, w_pool_up, w_gate, b_gate, w_out,
           norm_x_g, norm_mem_g, w_xq, w_xkv, w_xo, norm_ffn_g, w_router, b_router,
           w_gu, b_gu, w_down, b_down, final_norm_g):
    b, s, d = x.shape
    n = b * s
    assert w_in.shape[0] == 1, "single-layer block"
    row = lambda a: a.reshape(1, -1)
    bf = lambda a: a[0].astype(BF16)

    q, k, v, g0, gyb = _mixer_in(
        x, row(norm_mix_g[0]), bf(w_in), bf(w_gate), row(b_gate[0]), bf(pool_mix),
        row(pool_scale[0]), bf(w_pool_up), MIXER_TILE)

    bias, lam = _rel_bias(rel_bias, row(lambda_q1[0]), row(lambda_k1[0]),
                          row(lambda_q2[0]), row(lambda_k2[0]), s)
    a = _diff_attn(q, k, v, bias, lam, row(subln_g[0]), ATTN_HEADS_PER_STEP)

    qk_fold, vo_fold = _mem_fold(mem, row(norm_mem_g[0]), bf(w_xkv), bf(w_xq), bf(w_xo))

    x2, hf, eidx, rank, gate_tm, counts = _post(
        x.reshape(n, d), a.reshape(n, d), g0, gyb, bf(w_attn_up), bf(w_out),
        row(norm_x_g[0]), qk_fold, vo_fold, row(norm_ffn_g[0]),
        w_router[0].T, b_router[0].reshape(-1, 1), b, POST_TILE)

    bm = EXPERT_BLOCK
    rows = n * TOP_K + N_EXPERTS * bm
    pstart, nb_used, be, first, parts, nxt, slot = _expert_schedule(
        counts[:, 0], bm, rows // bm)

    dest_w = _worker_index_layout(_route_dest(pstart, eidx, rank, DEST_TILE), n)
    xs = _sc_dispatch(hf, dest_w, rows)
    ys = _experts(be, first, parts, nxt, slot, nb_used, xs, w_gu[0], b_gu[0], w_down[0],
                  b_down[0], bm)
    yg = _sc_gather(ys, dest_w, n)
    out = _combine(x2, gate_tm, row(final_norm_g), yg, COMBINE_TILE)
    return out.reshape(b, s, d)
```
